```python
import math
import jax, jax.numpy as jnp
from jax import lax
import numpy as np

D_MODEL = 2048
BATCH = 8
SEQ = 2048
DEPTH = 2

HEAD_DIM = 128
N_HEADS = D_MODEL // HEAD_DIM
N_SB_HEADS = N_HEADS // 2
N_FOX_HEADS = N_HEADS - N_SB_HEADS
D_FF = 256 * ((8 * D_MODEL // 3 + 255) // 256)
Q_BLOCK = 128
ROPE_THETA = 500000.0
ROPE_DIMS = HEAD_DIM // 4
DILATED_PATTERNS = ((128, 1), (512, 4), (2048, 16))
RMS_EPS = 1e-6
NEG_INF = -1e30
N_EVEN = (DEPTH + 1) // 2
N_ODD = DEPTH // 2

kernel_name = "hybrid_sb_fox_dilated_macaron"


def rmsnorm(x, g):
    xf = x.astype(jnp.float32)
    y = xf * lax.rsqrt(jnp.mean(xf * xf, axis=-1, keepdims=True) + RMS_EPS)
    return (y * g.astype(jnp.float32)).astype(x.dtype)


def swiglu(x, w_gate, w_up, w_down):
    return (jax.nn.silu(x @ w_gate) * (x @ w_up)) @ w_down


def partial_rope(x, positions):
    half = ROPE_DIMS // 2
    freqs = ROPE_THETA ** (-jnp.arange(half, dtype=jnp.float32) / half)
    ang = positions[:, None] * freqs[None, :]
    cos = jnp.cos(ang)[None, :, None, :]
    sin = jnp.sin(ang)[None, :, None, :]
    xr = x[..., :ROPE_DIMS].astype(jnp.float32)
    x1, x2 = xr[..., :half], xr[..., half:]
    rot = jnp.concatenate([x1 * cos - x2 * sin, x2 * cos + x1 * sin], axis=-1)
    return jnp.concatenate([rot.astype(x.dtype), x[..., ROPE_DIMS:]], axis=-1)


def stick_breaking_attention(q, k, v):
    B, H, S, hd = q.shape
    n_blocks = S // Q_BLOCK
    scale = hd ** -0.5
    key_pos = jnp.arange(S)

    def block(i):
        qb = lax.dynamic_slice_in_dim(q, i * Q_BLOCK, Q_BLOCK, axis=2)
        z = jnp.einsum('bhqd,bhkd->bhqk', qb, k).astype(jnp.float32) * scale
        q_pos = i * Q_BLOCK + jnp.arange(Q_BLOCK)
        strict = key_pos[None, :] < q_pos[:, None]
        log_not_beta = jnp.where(strict, -jax.nn.softplus(z), 0.0)
        after = lax.cumsum(log_not_beta, axis=3, reverse=True) - log_not_beta
        w = jnp.where(strict, jnp.exp(jax.nn.log_sigmoid(z) + after), 0.0)
        return jnp.einsum('bhqk,bhkd->bhqd', w.astype(v.dtype), v)

    out = lax.map(block, jnp.arange(n_blocks))
    return jnp.moveaxis(out, 0, 2).reshape(B, H, S, hd)


def forgetting_attention(q, k, v, log_f):
    B, H, S, hd = q.shape
    n_blocks = S // Q_BLOCK
    scale = hd ** -0.5
    cum_f = lax.cumsum(log_f, axis=2)
    key_pos = jnp.arange(S)

    def block(i):
        qb = lax.dynamic_slice_in_dim(q, i * Q_BLOCK, Q_BLOCK, axis=2)
        fq = lax.dynamic_slice_in_dim(cum_f, i * Q_BLOCK, Q_BLOCK, axis=2)
        z = (jnp.einsum('bhqd,bhkd->bhqk', qb, k).astype(jnp.float32) * scale
             + fq[..., :, None] - cum_f[:, :, None, :])
        q_pos = i * Q_BLOCK + jnp.arange(Q_BLOCK)
        causal = key_pos[None, :] <= q_pos[:, None]
        p = jax.nn.softmax(jnp.where(causal, z, NEG_INF), axis=-1)
        return jnp.einsum('bhqk,bhkd->bhqd', p.astype(v.dtype), v)

    out = lax.map(block, jnp.arange(n_blocks))
    return jnp.moveaxis(out, 0, 2).reshape(B, H, S, hd)


def dilated_branch(q, k, v, window, dilation):
    B, S, H, hd = q.shape
    steps = window // dilation
    L = S // dilation
    n_blocks = -(-L // steps)
    Lp = n_blocks * steps
    scale = hd ** -0.5

    def to_sub(t):
        return t.reshape(B, L, dilation, H, hd).transpose(0, 2, 3, 1, 4)

    qs = jnp.pad(to_sub(q), ((0, 0), (0, 0), (0, 0), (0, Lp - L), (0, 0)))
    qs = qs.reshape(B, dilation, H, n_blocks, steps, hd)

    def banded(t):
        tp = jnp.pad(to_sub(t), ((0, 0), (0, 0), (0, 0), (steps, Lp - L), (0, 0)))
        prev = tp[:, :, :, :Lp].reshape(B, dilation, H, n_blocks, steps, hd)
        cur = tp[:, :, :, steps:].reshape(B, dilation, H, n_blocks, steps, hd)
        return jnp.concatenate([prev, cur], axis=4)

    kb, vb = banded(k), banded(v)
    z = jnp.einsum('brhnqd,brhnkd->brhnqk', qs, kb).astype(jnp.float32) * scale
    a = jnp.arange(steps)[:, None]
    c = jnp.arange(2 * steps)[None, :]
    dist = a + steps - c
    key_sub = jnp.arange(n_blocks)[:, None, None] * steps + c - steps
    valid = (dist >= 0) & (dist <= steps) & (key_sub >= 0)
    z = jnp.where(valid, z, NEG_INF)
    m = jnp.max(z, axis=-1, keepdims=True)
    e = jnp.exp(z - m)
    denom = jnp.sum(e, axis=-1, keepdims=True)
    o = jnp.einsum('brhnqk,brhnkd->brhnqd', (e / denom).astype(v.dtype), vb)
    lse = (m + jnp.log(denom))[..., 0]
    o = o.reshape(B, dilation, H, Lp, hd)[:, :, :, :L]
    o = o.transpose(0, 3, 1, 2, 4).reshape(B, S, H, hd)
    lse = lse.reshape(B, dilation, H, Lp)[..., :L].transpose(0, 3, 1, 2).reshape(B, S, H)
    return o, lse


def dilated_attention(q, k, v):
    outs, lses = [], []
    for window, dilation in DILATED_PATTERNS:
        o, lse = dilated_branch(q, k, v, window, dilation)
        outs.append(o)
        lses.append(lse)
    w = jax.nn.softmax(jnp.stack(lses, axis=0), axis=0)
    out = jnp.sum(w[..., None] * jnp.stack(outs, axis=0).astype(jnp.float32), axis=0)
    return out.astype(q.dtype)


def sb_fox_mixer(x, w_in, b_forget, w_out):
    B, S, _ = x.shape
    h = x @ w_in
    qkv = h[..., :3 * D_MODEL].reshape(B, S, 3, N_HEADS, HEAD_DIM).transpose(2, 0, 3, 1, 4)
    q, k, v = qkv[0], qkv[1], qkv[2]
    log_f = jax.nn.log_sigmoid(
        (h[..., 3 * D_MODEL:] + b_forget).astype(jnp.float32)).transpose(0, 2, 1)
    o_sb = stick_breaking_attention(q[:, :N_SB_HEADS], k[:, :N_SB_HEADS], v[:, :N_SB_HEADS])
    o_fox = forgetting_attention(q[:, N_SB_HEADS:], k[:, N_SB_HEADS:], v[:, N_SB_HEADS:], log_f)
    o = jnp.concatenate([o_sb, o_fox], axis=1).transpose(0, 2, 1, 3).reshape(B, S, D_MODEL)
    return o @ w_out


def dilated_mixer(x, w_qkv, w_out):
    B, S, _ = x.shape
    qkv = (x @ w_qkv).reshape(B, S, 3, N_HEADS, HEAD_DIM)
    positions = jnp.arange(S, dtype=jnp.float32)
    q = partial_rope(qkv[:, :, 0], positions)
    k = partial_rope(qkv[:, :, 1], positions)
    v = qkv[:, :, 2]
    o = dilated_attention(q, k, v).reshape(B, S, D_MODEL)
    return o @ w_out


def _fwd_setup_inputs(seed: int = 0) -> dict:
    key = jax.random.key(seed)
    ks = jax.random.split(key, 16)
    f32 = jnp.float32

    def dense(k, shape, fan_in):
        return jax.random.normal(k, shape, f32) * fan_in ** -0.5

    x = jax.random.normal(ks[0], (BATCH, SEQ, D_MODEL), f32)
    norm_g = 1.0 + 0.02 * jax.random.normal(ks[1], (DEPTH, 3, D_MODEL), f32)
    ffn1_w_gate = dense(ks[2], (DEPTH, D_MODEL, D_FF), D_MODEL)
    ffn1_w_up = dense(ks[3], (DEPTH, D_MODEL, D_FF), D_MODEL)
    ffn1_w_down = dense(ks[4], (DEPTH, D_FF, D_MODEL), D_FF)
    ffn2_w_gate = dense(ks[5], (DEPTH, D_MODEL, D_FF), D_MODEL)
    ffn2_w_up = dense(ks[6], (DEPTH, D_MODEL, D_FF), D_MODEL)
    ffn2_w_down = dense(ks[7], (DEPTH, D_FF, D_MODEL), D_FF)
    even_w_in = dense(ks[8], (N_EVEN, D_MODEL, 3 * D_MODEL + N_FOX_HEADS), D_MODEL)
    even_b_forget = 3.0 + 0.5 * jax.random.normal(ks[9], (N_EVEN, N_FOX_HEADS), f32)
    even_w_out = dense(ks[10], (N_EVEN, D_MODEL, D_MODEL), D_MODEL)
    odd_w_qkv = dense(ks[11], (N_ODD, D_MODEL, 3 * D_MODEL), D_MODEL)
    odd_w_out = dense(ks[12], (N_ODD, D_MODEL, D_MODEL), D_MODEL)
    final_norm_g = 1.0 + 0.02 * jax.random.normal(ks[13], (D_MODEL,), f32)
    return {
        "x": x, "norm_g": norm_g,
        "ffn1_w_gate": ffn1_w_gate, "ffn1_w_up": ffn1_w_up, "ffn1_w_down": ffn1_w_down,
        "ffn2_w_gate": ffn2_w_gate, "ffn2_w_up": ffn2_w_up, "ffn2_w_down": ffn2_w_down,
        "even_w_in": even_w_in, "even_b_forget": even_b_forget, "even_w_out": even_w_out,
        "odd_w_qkv": odd_w_qkv, "odd_w_out": odd_w_out, "final_norm_g": final_norm_g,
    }


def _fwd_reference(x, norm_g, ffn1_w_gate, ffn1_w_up, ffn1_w_down, ffn2_w_gate, ffn2_w_up,
              ffn2_w_down, even_w_in, even_b_forget, even_w_out, odd_w_qkv, odd_w_out,
              final_norm_g):
    for layer in range(DEPTH):
        j = layer // 2
        x = x + 0.5 * swiglu(rmsnorm(x, norm_g[layer, 0]),
                             ffn1_w_gate[layer], ffn1_w_up[layer], ffn1_w_down[layer])
        h = rmsnorm(x, norm_g[layer, 1])
        if layer % 2 == 0:
            x = x + sb_fox_mixer(h, even_w_in[j], even_b_forget[j], even_w_out[j])
        else:
            x = x + dilated_mixer(h, odd_w_qkv[j], odd_w_out[j])
        x = x + 0.5 * swiglu(rmsnorm(x, norm_g[layer, 2]),
                             ffn2_w_gate[layer], ffn2_w_up[layer], ffn2_w_down[layer])
    return rmsnorm(x, final_norm_g)


import jax as _jax
import jax.numpy as _jnp

TWIN_FORMAT = 'train_step'
FWD_PARAMS = ['x', 'norm_g', 'ffn1_w_gate', 'ffn1_w_up', 'ffn1_w_down', 'ffn2_w_gate', 'ffn2_w_up', 'ffn2_w_down', 'even_w_in', 'even_b_forget', 'even_w_out', 'odd_w_qkv', 'odd_w_out', 'final_norm_g']
TWIN_WEIGHTS = ['norm_g', 'ffn1_w_gate', 'ffn1_w_up', 'ffn1_w_down', 'ffn2_w_gate', 'ffn2_w_up', 'ffn2_w_down', 'even_w_in', 'even_b_forget', 'even_w_out', 'odd_w_qkv', 'odd_w_out', 'final_norm_g']
TWIN_DIFF_INPUT = 'x'
TWIN_INPUTS = ['x', 'norm_g', 'ffn1_w_gate', 'ffn1_w_up', 'ffn1_w_down', 'ffn2_w_gate', 'ffn2_w_up', 'ffn2_w_down', 'even_w_in', 'even_b_forget', 'even_w_out', 'odd_w_qkv', 'odd_w_out', 'final_norm_g', 'loss_target', 'm_norm_g', 'm_ffn1_w_gate', 'm_ffn1_w_up', 'm_ffn1_w_down', 'm_ffn2_w_gate', 'm_ffn2_w_up', 'm_ffn2_w_down', 'm_even_w_in', 'm_even_b_forget', 'm_even_w_out', 'm_odd_w_qkv', 'm_odd_w_out', 'm_final_norm_g', 'v_norm_g', 'v_ffn1_w_gate', 'v_ffn1_w_up', 'v_ffn1_w_down', 'v_ffn2_w_gate', 'v_ffn2_w_up', 'v_ffn2_w_down', 'v_even_w_in', 'v_even_b_forget', 'v_even_w_out', 'v_odd_w_qkv', 'v_odd_w_out', 'v_final_norm_g']
TWIN_OUTPUTS = ['loss', 'grad_x', 'grad_norm_g', 'grad_ffn1_w_gate', 'grad_ffn1_w_up', 'grad_ffn1_w_down', 'grad_ffn2_w_gate', 'grad_ffn2_w_up', 'grad_ffn2_w_down', 'grad_even_w_in', 'grad_even_b_forget', 'grad_even_w_out', 'grad_odd_w_qkv', 'grad_odd_w_out', 'grad_final_norm_g', 'delta_norm_g', 'delta_ffn1_w_gate', 'delta_ffn1_w_up', 'delta_ffn1_w_down', 'delta_ffn2_w_gate', 'delta_ffn2_w_up', 'delta_ffn2_w_down', 'delta_even_w_in', 'delta_even_b_forget', 'delta_even_w_out', 'delta_odd_w_qkv', 'delta_odd_w_out', 'delta_final_norm_g', 'new_m_norm_g', 'new_m_ffn1_w_gate', 'new_m_ffn1_w_up', 'new_m_ffn1_w_down', 'new_m_ffn2_w_gate', 'new_m_ffn2_w_up', 'new_m_ffn2_w_down', 'new_m_even_w_in', 'new_m_even_b_forget', 'new_m_even_w_out', 'new_m_odd_w_qkv', 'new_m_odd_w_out', 'new_m_final_norm_g', 'new_v_norm_g', 'new_v_ffn1_w_gate', 'new_v_ffn1_w_up', 'new_v_ffn1_w_down', 'new_v_ffn2_w_gate', 'new_v_ffn2_w_up', 'new_v_ffn2_w_down', 'new_v_even_w_in', 'new_v_even_b_forget', 'new_v_even_w_out', 'new_v_odd_w_qkv', 'new_v_odd_w_out', 'new_v_final_norm_g']
TWIN_LEAF_KINDS = {'loss': 'loss', 'grad_x': 'grad_x', 'grad_norm_g': 'grad_w', 'grad_ffn1_w_gate': 'grad_w', 'grad_ffn1_w_up': 'grad_w', 'grad_ffn1_w_down': 'grad_w', 'grad_ffn2_w_gate': 'grad_w', 'grad_ffn2_w_up': 'grad_w', 'grad_ffn2_w_down': 'grad_w', 'grad_even_w_in': 'grad_w', 'grad_even_b_forget': 'grad_w', 'grad_even_w_out': 'grad_w', 'grad_odd_w_qkv': 'grad_w', 'grad_odd_w_out': 'grad_w', 'grad_final_norm_g': 'grad_w', 'delta_norm_g': 'delta_w', 'delta_ffn1_w_gate': 'delta_w', 'delta_ffn1_w_up': 'delta_w', 'delta_ffn1_w_down': 'delta_w', 'delta_ffn2_w_gate': 'delta_w', 'delta_ffn2_w_up': 'delta_w', 'delta_ffn2_w_down': 'delta_w', 'delta_even_w_in': 'delta_w', 'delta_even_b_forget': 'delta_w', 'delta_even_w_out': 'delta_w', 'delta_odd_w_qkv': 'delta_w', 'delta_odd_w_out': 'delta_w', 'delta_final_norm_g': 'delta_w', 'new_m_norm_g': 'new_m', 'new_m_ffn1_w_gate': 'new_m', 'new_m_ffn1_w_up': 'new_m', 'new_m_ffn1_w_down': 'new_m', 'new_m_ffn2_w_gate': 'new_m', 'new_m_ffn2_w_up': 'new_m', 'new_m_ffn2_w_down': 'new_m', 'new_m_even_w_in': 'new_m', 'new_m_even_b_forget': 'new_m', 'new_m_even_w_out': 'new_m', 'new_m_odd_w_qkv': 'new_m', 'new_m_odd_w_out': 'new_m', 'new_m_final_norm_g': 'new_m', 'new_v_norm_g': 'new_v', 'new_v_ffn1_w_gate': 'new_v', 'new_v_ffn1_w_up': 'new_v', 'new_v_ffn1_w_down': 'new_v', 'new_v_ffn2_w_gate': 'new_v', 'new_v_ffn2_w_up': 'new_v', 'new_v_ffn2_w_down': 'new_v', 'new_v_even_w_in': 'new_v', 'new_v_even_b_forget': 'new_v', 'new_v_even_w_out': 'new_v', 'new_v_odd_w_qkv': 'new_v', 'new_v_odd_w_out': 'new_v', 'new_v_final_norm_g': 'new_v'}


def _forward(args):
    return _fwd_reference(*[args[k] for k in FWD_PARAMS])


def _output_shape():
    out = _jax.eval_shape(lambda: _forward(_fwd_setup_inputs(0)))
    return out.shape, out.dtype

N_MICROBATCH = 1
ADAM_LR = 0.001
ADAM_B1 = 0.9
ADAM_B2 = 0.999
ADAM_EPS = 1e-08
ADAM_WD = 0.01
ADAM_STEP = 10
PER_EXAMPLE_BATCH_AXIS = {'x': 0, 'loss_target': 0}
SHARED_INPUTS = []
_WEIGHT_DTYPES = {'norm_g': _jnp.float32, 'ffn1_w_gate': _jnp.float32, 'ffn1_w_up': _jnp.float32, 'ffn1_w_down': _jnp.float32, 'ffn2_w_gate': _jnp.float32, 'ffn2_w_up': _jnp.float32, 'ffn2_w_down': _jnp.float32, 'even_w_in': _jnp.float32, 'even_b_forget': _jnp.float32, 'even_w_out': _jnp.float32, 'odd_w_qkv': _jnp.float32, 'odd_w_out': _jnp.float32, 'final_norm_g': _jnp.float32}
MOMENT_SCALE = {'norm_g': 2.668315e-02, 'ffn1_w_gate': 1.160276e-02, 'ffn1_w_up': 1.124878e-02, 'ffn1_w_down': 1.866247e-02, 'ffn2_w_gate': 9.811825e-03, 'ffn2_w_up': 9.516739e-03, 'ffn2_w_down': 1.577834e-02, 'even_w_in': 2.236748e-02, 'even_b_forget': 1.441965e-01, 'even_w_out': 3.074068e-02, 'odd_w_qkv': 1.037608e-02, 'odd_w_out': 1.220307e-02, 'final_norm_g': 7.999063e+00}


def _to_microbatches(a, axis):
    t = _jnp.moveaxis(a, axis, 0)
    t = t.reshape((N_MICROBATCH, t.shape[0] // N_MICROBATCH) + t.shape[1:])
    return _jnp.moveaxis(t, 1, axis + 1)


def setup_inputs(seed: int = 0) -> dict:
    inp = _fwd_setup_inputs(seed)
    key = _jax.random.fold_in(_jax.random.key(seed), 7919)
    shape, _ = _output_shape()
    out = dict(inp)
    out["loss_target"] = _jax.random.normal(_jax.random.fold_in(key, 0), shape, _jnp.float32)
    for i, name in enumerate(TWIN_WEIGHTS):
        w = inp[name].astype(_jnp.float32)
        if MOMENT_SCALE is None:
            s = _jnp.sqrt(_jnp.mean(_jnp.square(w)) + 1e-30)
        else:
            s = MOMENT_SCALE[name]
        km, kv = _jax.random.split(_jax.random.fold_in(key, i + 1))
        out[name] = w
        out["m_" + name] = s * _jax.random.normal(km, w.shape, _jnp.float32)
        out["v_" + name] = (s * s) * _jax.random.uniform(kv, w.shape, _jnp.float32, 0.5, 1.5)
    if N_MICROBATCH > 1:
        for name, axis in PER_EXAMPLE_BATCH_AXIS.items():
            out[name] = _to_microbatches(out[name], axis)
    return {'x': out['x'], 'norm_g': out['norm_g'], 'ffn1_w_gate': out['ffn1_w_gate'], 'ffn1_w_up': out['ffn1_w_up'], 'ffn1_w_down': out['ffn1_w_down'], 'ffn2_w_gate': out['ffn2_w_gate'], 'ffn2_w_up': out['ffn2_w_up'], 'ffn2_w_down': out['ffn2_w_down'], 'even_w_in': out['even_w_in'], 'even_b_forget': out['even_b_forget'], 'even_w_out': out['even_w_out'], 'odd_w_qkv': out['odd_w_qkv'], 'odd_w_out': out['odd_w_out'], 'final_norm_g': out['final_norm_g'], 'loss_target': out['loss_target'], 'm_norm_g': out['m_norm_g'], 'm_ffn1_w_gate': out['m_ffn1_w_gate'], 'm_ffn1_w_up': out['m_ffn1_w_up'], 'm_ffn1_w_down': out['m_ffn1_w_down'], 'm_ffn2_w_gate': out['m_ffn2_w_gate'], 'm_ffn2_w_up': out['m_ffn2_w_up'], 'm_ffn2_w_down': out['m_ffn2_w_down'], 'm_even_w_in': out['m_even_w_in'], 'm_even_b_forget': out['m_even_b_forget'], 'm_even_w_out': out['m_even_w_out'], 'm_odd_w_qkv': out['m_odd_w_qkv'], 'm_odd_w_out': out['m_odd_w_out'], 'm_final_norm_g': out['m_final_norm_g'], 'v_norm_g': out['v_norm_g'], 'v_ffn1_w_gate': out['v_ffn1_w_gate'], 'v_ffn1_w_up': out['v_ffn1_w_up'], 'v_ffn1_w_down': out['v_ffn1_w_down'], 'v_ffn2_w_gate': out['v_ffn2_w_gate'], 'v_ffn2_w_up': out['v_ffn2_w_up'], 'v_ffn2_w_down': out['v_ffn2_w_down'], 'v_even_w_in': out['v_even_w_in'], 'v_even_b_forget': out['v_even_b_forget'], 'v_even_w_out': out['v_even_w_out'], 'v_odd_w_qkv': out['v_odd_w_qkv'], 'v_odd_w_out': out['v_odd_w_out'], 'v_final_norm_g': out['v_final_norm_g']}


def _loss(weights, diff, rest, loss_target):
    with _jax.named_scope("forward"):
        args = {**rest, TWIN_DIFF_INPUT: diff, **{k: w.astype(_WEIGHT_DTYPES[k]) for k, w in weights.items()}}
        y = _forward(args)
    with _jax.named_scope("loss_head"):
        err = _jnp.square(y.astype(_jnp.float32) - loss_target)
        return 0.5 * _jnp.sum(_jnp.mean(err, axis=-1)) if err.ndim else 0.5 * err


def _adamw(w, g, m, v):
    m = ADAM_B1 * m + (1.0 - ADAM_B1) * g
    v = ADAM_B2 * v + (1.0 - ADAM_B2) * _jnp.square(g)
    m_hat = m / (1.0 - ADAM_B1 ** ADAM_STEP)
    v_hat = v / (1.0 - ADAM_B2 ** ADAM_STEP)
    delta = -ADAM_LR * (m_hat / (_jnp.sqrt(v_hat) + ADAM_EPS) + ADAM_WD * w)
    return delta, m, v


def reference(x, norm_g, ffn1_w_gate, ffn1_w_up, ffn1_w_down, ffn2_w_gate, ffn2_w_up, ffn2_w_down, even_w_in, even_b_forget, even_w_out, odd_w_qkv, odd_w_out, final_norm_g, loss_target, m_norm_g, m_ffn1_w_gate, m_ffn1_w_up, m_ffn1_w_down, m_ffn2_w_gate, m_ffn2_w_up, m_ffn2_w_down, m_even_w_in, m_even_b_forget, m_even_w_out, m_odd_w_qkv, m_odd_w_out, m_final_norm_g, v_norm_g, v_ffn1_w_gate, v_ffn1_w_up, v_ffn1_w_down, v_ffn2_w_gate, v_ffn2_w_up, v_ffn2_w_down, v_even_w_in, v_even_b_forget, v_even_w_out, v_odd_w_qkv, v_odd_w_out, v_final_norm_g):
    given = dict(x=x, norm_g=norm_g, ffn1_w_gate=ffn1_w_gate, ffn1_w_up=ffn1_w_up, ffn1_w_down=ffn1_w_down, ffn2_w_gate=ffn2_w_gate, ffn2_w_up=ffn2_w_up, ffn2_w_down=ffn2_w_down, even_w_in=even_w_in, even_b_forget=even_b_forget, even_w_out=even_w_out, odd_w_qkv=odd_w_qkv, odd_w_out=odd_w_out, final_norm_g=final_norm_g, loss_target=loss_target, m_norm_g=m_norm_g, m_ffn1_w_gate=m_ffn1_w_gate, m_ffn1_w_up=m_ffn1_w_up, m_ffn1_w_down=m_ffn1_w_down, m_ffn2_w_gate=m_ffn2_w_gate, m_ffn2_w_up=m_ffn2_w_up, m_ffn2_w_down=m_ffn2_w_down, m_even_w_in=m_even_w_in, m_even_b_forget=m_even_b_forget, m_even_w_out=m_even_w_out, m_odd_w_qkv=m_odd_w_qkv, m_odd_w_out=m_odd_w_out, m_final_norm_g=m_final_norm_g, v_norm_g=v_norm_g, v_ffn1_w_gate=v_ffn1_w_gate, v_ffn1_w_up=v_ffn1_w_up, v_ffn1_w_down=v_ffn1_w_down, v_ffn2_w_gate=v_ffn2_w_gate, v_ffn2_w_up=v_ffn2_w_up, v_ffn2_w_down=v_ffn2_w_down, v_even_w_in=v_even_w_in, v_even_b_forget=v_even_b_forget, v_even_w_out=v_even_w_out, v_odd_w_qkv=v_odd_w_qkv, v_odd_w_out=v_odd_w_out, v_final_norm_g=v_final_norm_g)
    weights = {n: given[n] for n in TWIN_WEIGHTS}
    shared = {n: given[n] for n in SHARED_INPUTS}
    per_example = {n: given[n] for n in ['x']}
    grad_fn = _jax.value_and_grad(_loss, argnums=(0, 1))

    def one_microbatch(ex, loss_target):
        ex = dict(ex)
        diff = ex.pop(TWIN_DIFF_INPUT)
        return grad_fn(weights, diff, {**shared, **ex}, loss_target)

    if N_MICROBATCH == 1:
        loss, (grad_w, grad_x) = one_microbatch(per_example, given["loss_target"])
    else:
        def body(carry, xs):
            loss_sum, grad_sum = carry
            l_k, (gw_k, gx_k) = one_microbatch(xs[0], xs[1])
            with _jax.named_scope("update"):
                return (loss_sum + l_k, _jax.tree.map(_jnp.add, grad_sum, gw_k)), gx_k

        init = (_jnp.zeros((), _jnp.float32), _jax.tree.map(_jnp.zeros_like, weights))
        (loss, grad_w), grad_x = _jax.lax.scan(body, init, (per_example, given["loss_target"]))
    with _jax.named_scope("update"):
        delta_w, new_m, new_v = {}, {}, {}
        for n in TWIN_WEIGHTS:
            delta_w[n], new_m[n], new_v[n] = _adamw(weights[n], grad_w[n], given["m_" + n], given["v_" + n])
    return (loss, grad_x, *[grad_w[n] for n in TWIN_WEIGHTS], *[delta_w[n] for n in TWIN_WEIGHTS],
            *[new_m[n] for n in TWIN_WEIGHTS], *[new_v[n] for n in TWIN_WEIGHTS])
```

```python
import functools
import math

import jax
import jax.numpy as jnp
from jax import lax
from jax.experimental import pallas as pl
from jax.experimental.pallas import tpu as pltpu

F32 = jnp.float32
BF16 = jnp.bfloat16

HEAD_DIM = 128
ROPE_DIMS = 32
ROPE_THETA = 500000.0
DILATED_PATTERNS = ((128, 1), (512, 4), (2048, 16))
RMS_EPS = 1e-6
NEG_INF = -1e30
ADAM_LR = 0.001
ADAM_B1 = 0.9
ADAM_B2 = 0.999
ADAM_EPS = 1e-08
ADAM_WD = 0.01
ADAM_STEP = 10

N_CHIPS = 4
N_CORES = 2
LANES = 128
BLK = 128
VMEM_BYTES_V7X = 64 * 2**20
MESH = pl.DeviceIdType.MESH


def _vmem_limit(block_bytes, scratch_bytes=0):
    need = 2 * block_bytes + scratch_bytes + 12 * 2**20
    return int(min(need, VMEM_BYTES_V7X - 6 * 2**20))


def _nbytes(shape, dtype):
    return math.prod(shape) * jnp.dtype(dtype).itemsize


def _tile(dim, target):
    best = None
    for t in range(LANES, min(dim, target) + 1, LANES):
        if dim % t == 0:
            best = t
    assert best is not None, (dim, target)
    return best


def _row_tile(rows, target):
    if rows <= target:
        return rows
    best = None
    for t in range(8, target + 1, 8):
        if rows % t == 0:
            best = t
    assert best is not None, (rows, target)
    return best


def _mm(name, a, b, mode, out_dtype, res=None, alpha=1.0, tm_target=512, tn_target=1536, tk_target=1024):
    a3 = a.ndim == 3
    b3 = b.ndim == 3
    if mode == "nn":
        assert not a3 and not b3
        (M, K), (K2, N) = a.shape, b.shape
    elif mode == "nt":
        assert not b3
        if a3:
            P, M, Kp = a.shape
            K = P * Kp
        else:
            M, K = a.shape
        N, K2 = b.shape
    else:
        assert mode == "tn" and not a3
        K, M = a.shape
        if b3:
            P, K2, Np = b.shape
            N = P * Np
        else:
            K2, N = b.shape
    assert K == K2, (name, a.shape, b.shape)
    tm = _tile(M, tm_target)
    tn = _tile(Np if b3 else N, tn_target)
    tk = _tile(Kp if a3 else K, tk_target)
    nk = K // tk
    grid = (M // tm, N // tn, nk)

    if mode == "nn":
        a_spec = pl.BlockSpec((tm, tk), lambda i, j, k: (i, k))
        b_spec = pl.BlockSpec((tk, tn), lambda i, j, k: (k, j))
        dims = (((1,), (0,)), ((), ()))
    elif mode == "nt":
        if a3:
            nkp = Kp // tk
            a_spec = pl.BlockSpec((None, tm, tk), lambda i, j, k: (k // nkp, i, k % nkp))
        else:
            a_spec = pl.BlockSpec((tm, tk), lambda i, j, k: (i, k))
        b_spec = pl.BlockSpec((tn, tk), lambda i, j, k: (j, k))
        dims = (((1,), (1,)), ((), ()))
    else:
        a_spec = pl.BlockSpec((tk, tm), lambda i, j, k: (k, i))
        if b3:
            njp = Np // tn
            b_spec = pl.BlockSpec((None, tk, tn), lambda i, j, k: (j // njp, k, j % njp))
        else:
            b_spec = pl.BlockSpec((tk, tn), lambda i, j, k: (k, j))
        dims = (((0,), (0,)), ((), ()))
    o_spec = pl.BlockSpec((tm, tn), lambda i, j, k: (i, j))
    has_res = res is not None

    def body(*refs):
        if has_res:
            a_ref, b_ref, r_ref, o_ref, acc_ref = refs
        else:
            a_ref, b_ref, o_ref, acc_ref = refs
        k = pl.program_id(2)

        @pl.when(k == 0)
        def _():
            acc_ref[...] = jnp.zeros_like(acc_ref)

        acc_ref[...] += lax.dot_general(a_ref[...], b_ref[...], dims, preferred_element_type=F32)

        @pl.when(k == nk - 1)
        def _():
            y = acc_ref[...] * alpha
            if has_res:
                y = r_ref[...] + y
            o_ref[...] = y.astype(o_ref.dtype)

    in_specs = [a_spec, b_spec] + ([o_spec] if has_res else [])
    args = (a, b) + ((res,) if has_res else ())
    blk = (_nbytes((tm, tk), a.dtype) + _nbytes((tk, tn), b.dtype) + _nbytes((tm, tn), out_dtype)
           + (_nbytes((tm, tn), F32) if has_res else 0))
    return pl.pallas_call(
        body, name=name, grid=grid, in_specs=in_specs, out_specs=o_spec,
        out_shape=jax.ShapeDtypeStruct((M, N), out_dtype),
        scratch_shapes=[pltpu.VMEM((tm, tn), F32)],
        compiler_params=pltpu.CompilerParams(
            dimension_semantics=("parallel", "parallel", "arbitrary"),
            vmem_limit_bytes=_vmem_limit(blk, 2 * _nbytes((tm, tn), F32))),
    )(*args)


def _rms_fwd(name, x, g):
    S, D = x.shape
    tr = _row_tile(S, 256)

    def body(x_ref, g_ref, n_ref):
        xv = x_ref[...]
        r = lax.rsqrt(jnp.mean(xv * xv, axis=-1, keepdims=True) + RMS_EPS)
        n_ref[...] = (xv * r * g_ref[...]).astype(BF16)

    return pl.pallas_call(
        body, name=name, grid=(S // tr,),
        in_specs=[pl.BlockSpec((tr, D), lambda i: (i, 0)), pl.BlockSpec((1, D), lambda i: (0, 0))],
        out_specs=pl.BlockSpec((tr, D), lambda i: (i, 0)),
        out_shape=jax.ShapeDtypeStruct((S, D), BF16),
        compiler_params=pltpu.CompilerParams(dimension_semantics=("parallel",)),
    )(x, g)


def _rms_bwd(name, dn, x, g, dres):
    S, D = x.shape
    tr = _row_tile(S, 256)

    def body(dn_ref, x_ref, g_ref, dres_ref, dx_ref, dxb_ref, dg_ref):
        i = pl.program_id(0)
        xv = x_ref[...]
        dnv = dn_ref[...]
        r = lax.rsqrt(jnp.mean(xv * xv, axis=-1, keepdims=True) + RMS_EPS)
        u = dnv * g_ref[...]
        dot = jnp.mean(u * xv, axis=-1, keepdims=True)
        dx = dres_ref[...] + r * u - xv * (r * r * r * dot)
        dx_ref[...] = dx
        dxb_ref[...] = dx.astype(BF16)

        @pl.when(i == 0)
        def _():
            dg_ref[...] = jnp.zeros_like(dg_ref)

        dg_ref[...] += jnp.sum(dnv * xv * r, axis=0, keepdims=True)

    row = pl.BlockSpec((tr, D), lambda i: (i, 0))
    vec = pl.BlockSpec((1, D), lambda i: (0, 0))
    return pl.pallas_call(
        body, name=name, grid=(S // tr,),
        in_specs=[row, row, vec, row], out_specs=[row, row, vec],
        out_shape=[jax.ShapeDtypeStruct((S, D), F32), jax.ShapeDtypeStruct((S, D), BF16),
                   jax.ShapeDtypeStruct((1, D), F32)],
        compiler_params=pltpu.CompilerParams(dimension_semantics=("arbitrary",)),
    )(dn, x, g, dres)


def _loss_head(name, x, g, target):
    S, D = x.shape
    tr = _row_tile(S, 256)

    def body(x_ref, g_ref, t_ref, dx_ref, dxb_ref, dg_ref, loss_ref):
        i = pl.program_id(0)
        xv = x_ref[...]
        gv = g_ref[...]
        r = lax.rsqrt(jnp.mean(xv * xv, axis=-1, keepdims=True) + RMS_EPS)
        diff = xv * r * gv - t_ref[...]
        part = 0.5 * jnp.sum(jnp.mean(diff * diff, axis=-1, keepdims=True), axis=0, keepdims=True)
        dy = diff * (1.0 / D)
        u = dy * gv
        dot = jnp.mean(u * xv, axis=-1, keepdims=True)
        dx = r * u - xv * (r * r * r * dot)
        dx_ref[...] = dx
        dxb_ref[...] = dx.astype(BF16)

        @pl.when(i == 0)
        def _():
            dg_ref[...] = jnp.zeros_like(dg_ref)
            loss_ref[...] = jnp.zeros_like(loss_ref)

        dg_ref[...] += jnp.sum(dy * xv * r, axis=0, keepdims=True)
        loss_ref[...] += jnp.broadcast_to(part, loss_ref.shape)

    row = pl.BlockSpec((tr, D), lambda i: (i, 0))
    vec = pl.BlockSpec((1, D), lambda i: (0, 0))
    lvec = pl.BlockSpec((1, LANES), lambda i: (0, 0))
    return pl.pallas_call(
        body, name=name, grid=(S // tr,),
        in_specs=[row, vec, row], out_specs=[row, row, vec, lvec],
        out_shape=[jax.ShapeDtypeStruct((S, D), F32), jax.ShapeDtypeStruct((S, D), BF16),
                   jax.ShapeDtypeStruct((1, D), F32), jax.ShapeDtypeStruct((1, LANES), F32)],
        compiler_params=pltpu.CompilerParams(dimension_semantics=("arbitrary",)),
    )(x, g, target)


def _swiglu_fwd(name, gu, fs):
    S, two_f = gu.shape
    nslab = two_f // (2 * fs)
    tr = _row_tile(S, 256)

    def body(gu_ref, h_ref):
        gv = gu_ref[:, :fs]
        uv = gu_ref[:, fs:]
        h_ref[...] = (gv * jax.nn.sigmoid(gv) * uv).astype(BF16)

    return pl.pallas_call(
        body, name=name, grid=(S // tr, nslab),
        in_specs=[pl.BlockSpec((tr, 2 * fs), lambda i, k: (i, k))],
        out_specs=pl.BlockSpec((tr, fs), lambda i, k: (i, k)),
        out_shape=jax.ShapeDtypeStruct((S, nslab * fs), BF16),
        compiler_params=pltpu.CompilerParams(dimension_semantics=("parallel", "parallel")),
    )(gu)


def _swiglu_bwd(name, dh, gu, fs):
    S, two_f = gu.shape
    nslab = two_f // (2 * fs)
    tr = _row_tile(S, 256)

    def body(dh_ref, gu_ref, o_ref):
        gv = gu_ref[:, :fs]
        uv = gu_ref[:, fs:]
        dhv = dh_ref[...]
        sg = jax.nn.sigmoid(gv)
        silu = gv * sg
        o_ref[:, :fs] = (dhv * uv * (sg + silu * (1.0 - sg))).astype(BF16)
        o_ref[:, fs:] = (dhv * silu).astype(BF16)

    return pl.pallas_call(
        body, name=name, grid=(S // tr, nslab),
        in_specs=[pl.BlockSpec((tr, fs), lambda i, k: (i, k)), pl.BlockSpec((tr, 2 * fs), lambda i, k: (i, k))],
        out_specs=pl.BlockSpec((tr, 2 * fs), lambda i, k: (i, k)),
        out_shape=jax.ShapeDtypeStruct((S, two_f), BF16),
        compiler_params=pltpu.CompilerParams(dimension_semantics=("parallel", "parallel")),
    )(dh, gu)


def _tri_rows(r0, nrows, ncols, lower):
    row = lax.broadcasted_iota(jnp.int32, (nrows, ncols), 0) + r0
    col = lax.broadcasted_iota(jnp.int32, (nrows, ncols), 1)
    return jnp.where((col <= row) if lower else (col >= row), 1.0, 0.0).astype(F32)


def _gate_fwd(name, hf, b):
    S = hf.shape[0]
    tb = _row_tile(S, 256)

    def body(hf_ref, b_ref, cf_ref, cft_ref, lf_ref):
        zz = hf_ref[...] + b_ref[...]
        lf_ref[...] = jnp.minimum(zz, 0.0) - jnp.log1p(jnp.exp(-jnp.abs(zz)))

        def blk(i, c):
            r0 = pl.multiple_of(i * tb, tb)
            tri = _tri_rows(r0, tb, S, True)
            cf_ref[pl.ds(r0, tb), :] = jnp.dot(tri, lf_ref[...], precision=lax.Precision.HIGHEST,
                                               preferred_element_type=F32)
            return c

        lax.fori_loop(0, S // tb, blk, 0)
        cft_ref[...] = cf_ref[...].T

    full = pl.BlockSpec((S, LANES), lambda: (0, 0))
    return pl.pallas_call(
        body, name=name, in_specs=[full, pl.BlockSpec((1, LANES), lambda: (0, 0))],
        out_specs=[full, pl.BlockSpec((LANES, S), lambda: (0, 0))],
        out_shape=[jax.ShapeDtypeStruct((S, LANES), F32), jax.ShapeDtypeStruct((LANES, S), F32)],
        scratch_shapes=[pltpu.VMEM((S, LANES), F32)],
    )(hf, b)


def _gate_bwd(name, dcft, drow, hf, b):
    S = hf.shape[0]
    tb = _row_tile(S, 256)

    def body(dcft_ref, drow_ref, hf_ref, b_ref, dhf_ref, db_ref, dcf_ref, dlf_ref):
        dcf_ref[...] = dcft_ref[...].T + drow_ref[...]

        def blk(i, c):
            r0 = pl.multiple_of(i * tb, tb)
            tri = _tri_rows(r0, tb, S, False)
            dlf_ref[pl.ds(r0, tb), :] = jnp.dot(tri, dcf_ref[...], precision=lax.Precision.HIGHEST,
                                                preferred_element_type=F32)
            return c

        lax.fori_loop(0, S // tb, blk, 0)
        zz = hf_ref[...] + b_ref[...]
        dhf = dlf_ref[...] * jax.nn.sigmoid(-zz)
        dhf_ref[...] = dhf.astype(BF16)
        db_ref[...] = jnp.sum(dhf, axis=0, keepdims=True)

    full = pl.BlockSpec((S, LANES), lambda: (0, 0))
    vec = pl.BlockSpec((1, LANES), lambda: (0, 0))
    return pl.pallas_call(
        body, name=name, in_specs=[pl.BlockSpec((LANES, S), lambda: (0, 0)), full, full, vec],
        out_specs=[full, vec],
        out_shape=[jax.ShapeDtypeStruct((S, LANES), BF16), jax.ShapeDtypeStruct((1, LANES), F32)],
        scratch_shapes=[pltpu.VMEM((S, LANES), F32), pltpu.VMEM((S, LANES), F32)],
    )(dcft, drow, hf, b)


def _rope_tables(S):
    half = ROPE_DIMS // 2
    freqs = ROPE_THETA ** (-jnp.arange(half, dtype=F32) / half)
    ang = jnp.arange(S, dtype=F32)[:, None] * freqs[None, :]
    cos, sin = jnp.cos(ang), jnp.sin(ang)
    pad = HEAD_DIM - ROPE_DIMS
    c = jnp.concatenate([cos, cos, jnp.ones((S, pad), F32)], axis=1)
    s = jnp.concatenate([-sin, sin, jnp.zeros((S, pad), F32)], axis=1)
    return c, s


def _rope_swap(x):
    half = ROPE_DIMS // 2
    lane = lax.broadcasted_iota(jnp.int32, x.shape, 1)
    upper = jnp.where(lane < ROPE_DIMS, pltpu.roll(x, half, 1), 0.0)
    return jnp.where(lane < half, pltpu.roll(x, HEAD_DIM - half, 1), upper)


def _rope(x, c, s):
    return x * c + _rope_swap(x) * s


def _rope_t(dy, c, s):
    return dy * c + _rope_swap(dy * s)


def _split_dot(x, t):
    hi = x.astype(BF16)
    lo = (x - hi.astype(F32)).astype(BF16)
    return (jnp.dot(hi, t, preferred_element_type=F32) + jnp.dot(lo, t, preferred_element_type=F32))


_NT = (((1,), (1,)), ((), ()))
_TN = (((0,), (0,)), ((), ()))


def _dot_nt(a, b):
    return lax.dot_general(a, b, _NT, preferred_element_type=F32)


def _dot_tn(a, b):
    return lax.dot_general(a, b, _TN, preferred_element_type=F32)


def _blk(i):
    return pl.ds(pl.multiple_of(i * BLK, BLK), BLK)


def _delta(i, j):
    row = lax.broadcasted_iota(jnp.int32, (BLK, BLK), 0)
    col = lax.broadcasted_iota(jnp.int32, (BLK, BLK), 1)
    return (row - col) + (i - j) * BLK


def _dilated_mult(delta):
    c = jnp.zeros(delta.shape, F32)
    for window, dil in DILATED_PATTERNS:
        ok = (delta >= 0) & (delta <= window) & ((delta & (dil - 1)) == 0)
        c = c + jnp.where(ok, 1.0, 0.0)
    return c


def _sb_terms(z, mask, t_ex, run):
    t = jnp.log1p(jnp.exp(-jnp.abs(z)))
    lsig = jnp.minimum(z, 0.0) - t
    m = jnp.where(mask, -(jnp.maximum(z, 0.0) + t), 0.0)
    after = _split_dot(m, t_ex)
    a = jnp.where(mask, jnp.exp(lsig + after + run), 0.0)
    return a, m, lsig


def _attn_fwd(name, hq, layer_kind, n_heads, n_sb, cf=None, cft=None, rope_c=None, rope_s=None):
    S = hq.shape[0]
    D = n_heads * HEAD_DIM
    nq = S // BLK
    scale = HEAD_DIM ** -0.5
    even = layer_kind == "even"

    def body(*refs):
        if even:
            q_ref, k_ref, v_ref, cf_ref, cft_ref, o_ref, ob_ref, lse_ref, qs, ks, vs = refs
        else:
            q_ref, k_ref, v_ref, c_ref, s_ref, o_ref, ob_ref, lse_ref, qs, ks, vs = refs
        h = pl.program_id(0)
        if even:
            qs[...] = q_ref[...].astype(BF16)
            ks[...] = k_ref[...].astype(BF16)
        else:
            qs[...] = _rope(q_ref[...], c_ref[...], s_ref[...]).astype(BF16)
            ks[...] = _rope(k_ref[...], c_ref[...], s_ref[...]).astype(BF16)
        vs[...] = v_ref[...].astype(BF16)

        def softmax_head(hh):
            def qblock(i, carry):
                qi = qs[_blk(i), :]
                if even:
                    lane = lax.broadcasted_iota(jnp.int32, (BLK, LANES), 1)
                    cfq = jnp.sum(jnp.where(lane == hh, cf_ref[_blk(i), :], 0.0), axis=1, keepdims=True)

                def kblock(j, c):
                    m_run, l_run, acc = c
                    z = _dot_nt(qi, ks[_blk(j), :]) * scale
                    delta = _delta(i, j)
                    if even:
                        z = z + cfq - cft_ref[hh, :, _blk(j)]
                        ok = delta >= 0
                    else:
                        mult = _dilated_mult(delta)
                        ok = mult > 0.0
                    z = jnp.where(ok, z, NEG_INF)
                    m_new = jnp.maximum(m_run, jnp.max(z, axis=1, keepdims=True))
                    p = jnp.exp(z - m_new)
                    if not even:
                        p = p * mult
                    alpha = jnp.exp(m_run - m_new)
                    l_new = alpha * l_run + jnp.sum(p, axis=1, keepdims=True)
                    acc = alpha * acc + jnp.dot(p.astype(BF16), vs[_blk(j), :], preferred_element_type=F32)
                    return m_new, l_new, acc

                init = (jnp.full((BLK, 1), NEG_INF, F32), jnp.zeros((BLK, 1), F32), jnp.zeros((BLK, HEAD_DIM), F32))
                m_run, l_run, acc = lax.fori_loop(0, i + 1, kblock, init)
                o = acc / l_run
                o_ref[_blk(i), :] = o
                ob_ref[_blk(i), :] = o.astype(BF16)
                lse_ref[_blk(i), :] = jnp.broadcast_to(m_run + jnp.log(l_run), (BLK, HEAD_DIM))
                return carry

            lax.fori_loop(0, nq, qblock, 0)

        def sb_head():
            row = lax.broadcasted_iota(jnp.int32, (BLK, BLK), 0)
            col = lax.broadcasted_iota(jnp.int32, (BLK, BLK), 1)
            t_ex = jnp.where(row > col, 1.0, 0.0).astype(BF16)

            def qblock(i, carry):
                qi = qs[_blk(i), :]

                def kblock(jj, c):
                    run, acc = c
                    j = i - jj
                    z = _dot_nt(qi, ks[_blk(j), :]) * scale
                    a, m, _ = _sb_terms(z, _delta(i, j) > 0, t_ex, run)
                    acc = acc + jnp.dot(a.astype(BF16), vs[_blk(j), :], preferred_element_type=F32)
                    return run + jnp.sum(m, axis=1, keepdims=True), acc

                init = (jnp.zeros((BLK, 1), F32), jnp.zeros((BLK, HEAD_DIM), F32))
                _, acc = lax.fori_loop(0, i + 1, kblock, init)
                o_ref[_blk(i), :] = acc
                ob_ref[_blk(i), :] = acc.astype(BF16)
                lse_ref[_blk(i), :] = jnp.zeros((BLK, HEAD_DIM), F32)
                return carry

            lax.fori_loop(0, nq, qblock, 0)

        if even:
            @pl.when(h < n_sb)
            def _():
                sb_head()

            @pl.when(h >= n_sb)
            def _():
                softmax_head(h - n_sb)
        else:
            softmax_head(h)

    head = lambda off: pl.BlockSpec((S, HEAD_DIM), lambda h, off=off: (0, off + h))
    full = pl.BlockSpec((S, LANES), lambda h: (0, 0))
    if even:
        extra_specs = [full, pl.BlockSpec(cft.shape, lambda h: (0, 0, 0))]
        extra = (cf, cft)
    else:
        extra_specs = [full, full]
        extra = (rope_c, rope_s)
    blk_bytes = 8 * _nbytes((S, HEAD_DIM), F32)
    return pl.pallas_call(
        body, name=name, grid=(n_heads,),
        in_specs=[head(0), head(n_heads), head(2 * n_heads)] + extra_specs,
        out_specs=[head(0), head(0), head(0)],
        out_shape=[jax.ShapeDtypeStruct((S, D), F32), jax.ShapeDtypeStruct((S, D), BF16),
                   jax.ShapeDtypeStruct((S, D), F32)],
        scratch_shapes=[pltpu.VMEM((S, HEAD_DIM), BF16)] * 3,
        compiler_params=pltpu.CompilerParams(dimension_semantics=("arbitrary",),
                                             vmem_limit_bytes=_vmem_limit(blk_bytes, 3 * _nbytes((S, HEAD_DIM), BF16))),
    )(hq, hq, hq, *extra)


def _attn_bwd(name, hq, do, o, lse, layer_kind, n_heads, n_sb, cf=None, cft=None, rope_c=None, rope_s=None):
    S = hq.shape[0]
    D = n_heads * HEAD_DIM
    nq = S // BLK
    scale = HEAD_DIM ** -0.5
    even = layer_kind == "even"

    def body(*refs):
        if even:
            (q_ref, k_ref, v_ref, do_ref, o_ref, lse_ref, cf_ref, cft_ref,
             dh_ref, dcft_ref, drow_ref, qs, ks, vs, dos, dq_acc, dk_acc, dv_acc) = refs
        else:
            (q_ref, k_ref, v_ref, do_ref, o_ref, lse_ref, c_ref, s_ref,
             dh_ref, qs, ks, vs, dos, dq_acc, dk_acc, dv_acc) = refs
        h = pl.program_id(0)
        if even:
            qs[...] = q_ref[...].astype(BF16)
            ks[...] = k_ref[...].astype(BF16)

            @pl.when(h == 0)
            def _():
                dcft_ref[...] = jnp.zeros_like(dcft_ref)
                drow_ref[...] = jnp.zeros_like(drow_ref)
        else:
            qs[...] = _rope(q_ref[...], c_ref[...], s_ref[...]).astype(BF16)
            ks[...] = _rope(k_ref[...], c_ref[...], s_ref[...]).astype(BF16)
        vs[...] = v_ref[...].astype(BF16)
        dos[...] = do_ref[...].astype(BF16)
        dk_acc[...] = jnp.zeros_like(dk_acc)
        dv_acc[...] = jnp.zeros_like(dv_acc)

        def softmax_head(hh):
            def qblock(i, carry):
                qi = qs[_blk(i), :]
                doi = dos[_blk(i), :]
                dvec = jnp.sum(do_ref[_blk(i), :] * o_ref[_blk(i), :], axis=1, keepdims=True)
                lse_i = jnp.max(lse_ref[_blk(i), :], axis=1, keepdims=True)
                if even:
                    lane = lax.broadcasted_iota(jnp.int32, (BLK, LANES), 1)
                    cfq = jnp.sum(jnp.where(lane == hh, cf_ref[_blk(i), :], 0.0), axis=1, keepdims=True)

                def kblock(j, c):
                    dq, ds_rows = c
                    kj = ks[_blk(j), :]
                    z = _dot_nt(qi, kj) * scale
                    delta = _delta(i, j)
                    if even:
                        z = z + cfq - cft_ref[hh, :, _blk(j)]
                        ok = delta >= 0
                    else:
                        mult = _dilated_mult(delta)
                        ok = mult > 0.0
                    p = jnp.exp(jnp.where(ok, z, NEG_INF) - lse_i)
                    if not even:
                        p = p * mult
                    dp = _dot_nt(doi, vs[_blk(j), :])
                    ds = p * (dp - dvec)
                    dsb = (ds * scale).astype(BF16)
                    dk_acc[_blk(j), :] += _dot_tn(dsb, qi)
                    dv_acc[_blk(j), :] += _dot_tn(p.astype(BF16), doi)
                    if even:
                        dcft_ref[hh, :, _blk(j)] += -jnp.sum(ds, axis=0, keepdims=True)
                    return (dq + jnp.dot(dsb, kj, preferred_element_type=F32),
                            ds_rows + jnp.sum(ds, axis=1, keepdims=True))

                dq, ds_rows = lax.fori_loop(0, i + 1, kblock,
                                            (jnp.zeros((BLK, HEAD_DIM), F32), jnp.zeros((BLK, 1), F32)))
                dq_acc[_blk(i), :] = dq
                if even:
                    drow_ref[_blk(i), :] += jnp.where(lane == hh, ds_rows, 0.0)
                return carry

            lax.fori_loop(0, nq, qblock, 0)

        def sb_head():
            row = lax.broadcasted_iota(jnp.int32, (BLK, BLK), 0)
            col = lax.broadcasted_iota(jnp.int32, (BLK, BLK), 1)
            t_ex = jnp.where(row > col, 1.0, 0.0).astype(BF16)
            t_in = jnp.where(row >= col, 1.0, 0.0).astype(BF16)

            def qblock(i, carry):
                qi = qs[_blk(i), :]
                doi = dos[_blk(i), :]

                def e_total(jj, c):
                    run, tot = c
                    j = i - jj
                    z = _dot_nt(qi, ks[_blk(j), :]) * scale
                    a, m, _ = _sb_terms(z, _delta(i, j) > 0, t_ex, run)
                    e = _dot_nt(doi, vs[_blk(j), :]) * a
                    return run + jnp.sum(m, axis=1, keepdims=True), tot + jnp.sum(e, axis=1, keepdims=True)

                zero = jnp.zeros((BLK, 1), F32)
                _, e_tot = lax.fori_loop(0, i + 1, e_total, (zero, zero))

                def kblock(jj, c):
                    run, e_run, dq = c
                    j = i - jj
                    kj = ks[_blk(j), :]
                    z = _dot_nt(qi, kj) * scale
                    mask = _delta(i, j) > 0
                    a, m, lsig = _sb_terms(z, mask, t_ex, run)
                    sig = jnp.exp(lsig)
                    e = _dot_nt(doi, vs[_blk(j), :]) * a
                    e_before = e_tot - (_split_dot(e, t_in) + e_run)
                    dz = jnp.where(mask, e * (1.0 - sig) - sig * e_before, 0.0)
                    dzb = (dz * scale).astype(BF16)
                    dk_acc[_blk(j), :] += _dot_tn(dzb, qi)
                    dv_acc[_blk(j), :] += _dot_tn(a.astype(BF16), doi)
                    return (run + jnp.sum(m, axis=1, keepdims=True), e_run + jnp.sum(e, axis=1, keepdims=True),
                            dq + jnp.dot(dzb, kj, preferred_element_type=F32))

                _, _, dq = lax.fori_loop(0, i + 1, kblock, (zero, zero, jnp.zeros((BLK, HEAD_DIM), F32)))
                dq_acc[_blk(i), :] = dq
                return carry

            lax.fori_loop(0, nq, qblock, 0)

        if even:
            @pl.when(h < n_sb)
            def _():
                sb_head()

            @pl.when(h >= n_sb)
            def _():
                softmax_head(h - n_sb)

            dh_ref[0] = dq_acc[...].astype(BF16)
            dh_ref[1] = dk_acc[...].astype(BF16)
        else:
            softmax_head(h)
            dh_ref[0] = _rope_t(dq_acc[...], c_ref[...], s_ref[...]).astype(BF16)
            dh_ref[1] = _rope_t(dk_acc[...], c_ref[...], s_ref[...]).astype(BF16)
        dh_ref[2] = dv_acc[...].astype(BF16)

    head = lambda off: pl.BlockSpec((S, HEAD_DIM), lambda h, off=off: (0, off + h))
    full = pl.BlockSpec((S, LANES), lambda h: (0, 0))
    tfull = pl.BlockSpec((n_heads - n_sb, 1, S), lambda h: (0, 0, 0))
    dh_spec = pl.BlockSpec((3, S, HEAD_DIM), lambda h: (0, 0, h))
    dh_shape = jax.ShapeDtypeStruct((3, S, D), BF16)
    if even:
        extra_specs, extra = [full, tfull], (cf, cft)
        out_specs = [dh_spec, tfull, full]
        out_shape = [dh_shape, jax.ShapeDtypeStruct((n_heads - n_sb, 1, S), F32),
                     jax.ShapeDtypeStruct((S, LANES), F32)]
    else:
        extra_specs, extra = [full, full], (rope_c, rope_s)
        out_specs = [dh_spec]
        out_shape = [dh_shape]
    blk_bytes = 10 * _nbytes((S, HEAD_DIM), F32)
    scratch_bytes = 4 * _nbytes((S, HEAD_DIM), BF16) + 3 * _nbytes((S, HEAD_DIM), F32)
    return pl.pallas_call(
        body, name=name, grid=(n_heads,),
        in_specs=[head(0), head(n_heads), head(2 * n_heads), head(0), head(0), head(0)] + extra_specs,
        out_specs=out_specs, out_shape=out_shape,
        scratch_shapes=[pltpu.VMEM((S, HEAD_DIM), BF16)] * 4 + [pltpu.VMEM((S, HEAD_DIM), F32)] * 3,
        compiler_params=pltpu.CompilerParams(dimension_semantics=("arbitrary",),
                                             vmem_limit_bytes=_vmem_limit(blk_bytes, scratch_bytes)),
    )(hq, hq, hq, do, o, lse, *extra)


def _ffn_fwd(tag, x, g, wgu, wd, fs):
    n = _rms_fwd(tag + "_norm", x, g)
    gu = _mm(tag + "_gu", n, wgu, "nn", F32)
    h = _swiglu_fwd(tag + "_act", gu, fs)
    y = _mm(tag + "_down", h, wd, "nn", F32, res=x, alpha=0.5)
    return y, (x, g, n, gu, h)


def _ffn_bwd(tag, dx, dxb, wgu, wd, fs, saved):
    x, g, n, gu, h = saved
    dh = _mm(tag + "_dh", dxb, wd, "nt", F32, alpha=0.5)
    dwd = _mm(tag + "_dwd", h, dxb, "tn", F32, alpha=0.5)
    dgu = _swiglu_bwd(tag + "_dact", dh, gu, fs)
    dwgu = _mm(tag + "_dwgu", n, dgu, "tn", F32)
    dn = _mm(tag + "_dn", dgu, wgu, "nt", F32)
    dx_in, dxb_in, dg = _rms_bwd(tag + "_dnorm", dn, x, g, dx)
    return dx_in, dxb_in, dg, dwgu, dwd


def _mixer_fwd(tag, kind, x, g, wqkv, wout, n_heads, n_sb, wf=None, bf=None, rope=None):
    n = _rms_fwd(tag + "_norm", x, g)
    hq = _mm(tag + "_qkv", n, wqkv, "nn", F32)
    if kind == "even":
        hf = _mm(tag + "_gate", n, wf, "nn", F32)
        cf, cft = _gate_fwd(tag + "_cumgate", hf, bf)
        cft = cft[:n_heads - n_sb].reshape(n_heads - n_sb, 1, -1)
        o, ob, lse = _attn_fwd(tag + "_attn", hq, kind, n_heads, n_sb, cf=cf, cft=cft)
    else:
        hf = cf = cft = None
        o, ob, lse = _attn_fwd(tag + "_attn", hq, kind, n_heads, n_sb, rope_c=rope[0], rope_s=rope[1])
    y = _mm(tag + "_out", ob, wout, "nn", F32, res=x)
    return y, (x, g, n, hq, hf, cf, cft, o, ob, lse)


def _mixer_bwd(tag, kind, dx, dxb, wqkv, wout, n_heads, n_sb, saved, wf=None, bf=None, rope=None):
    x, g, n, hq, hf, cf, cft, o, ob, lse = saved
    do = _mm(tag + "_do", dxb, wout, "nt", F32)
    dwout = _mm(tag + "_dwout", ob, dxb, "tn", F32)
    if kind == "even":
        dh3, dcft, drow = _attn_bwd(tag + "_dattn", hq, do, o, lse, kind, n_heads, n_sb, cf=cf, cft=cft)
    else:
        (dh3,) = _attn_bwd(tag + "_dattn", hq, do, o, lse, kind, n_heads, n_sb, rope_c=rope[0], rope_s=rope[1])
    dwqkv = _mm(tag + "_dwqkv", n, dh3, "tn", F32)
    dn = _mm(tag + "_dn", dh3, wqkv, "nt", F32)
    dwf = db = None
    if kind == "even":
        n_fox = n_heads - n_sb
        dcft = jnp.pad(dcft.reshape(n_fox, -1), ((0, LANES - n_fox), (0, 0)))
        dhf, db = _gate_bwd(tag + "_dcumgate", dcft, drow, hf, bf)
        dwf = _mm(tag + "_dwf", n, dhf, "tn", F32)
        dn = _mm(tag + "_dn_gate", dhf, wf, "nt", F32, res=dn)
    dx_in, dxb_in, dg = _rms_bwd(tag + "_dnorm", dn, x, g, dx)
    return dx_in, dxb_in, dg, dwqkv, dwout, dwf, db


def _local_step(x, target, w, fs, n_heads, n_sb):
    S, D = x.shape
    rope = _rope_tables(S)
    kinds = ("even", "odd")
    saved = []
    h = x
    for l, kind in enumerate(kinds):
        ng = [w["norm_g"][l, i][None, :] for i in range(3)]
        h, s1 = _ffn_fwd(f"l{l}_ffn1", h, ng[0], w["wgu1"][l], w["wd1"][l], fs)
        if kind == "even":
            h, s2 = _mixer_fwd(f"l{l}_mix", kind, h, ng[1], w["wqkv_e"], w["wout_e"], n_heads, n_sb,
                               wf=w["wf"], bf=w["bf"])
        else:
            h, s2 = _mixer_fwd(f"l{l}_mix", kind, h, ng[1], w["wqkv_o"], w["wout_o"], n_heads, n_sb, rope=rope)
        h, s3 = _ffn_fwd(f"l{l}_ffn2", h, ng[2], w["wgu2"][l], w["wd2"][l], fs)
        saved.append((s1, s2, s3))

    dx, dxb, dfinal, loss = _loss_head("loss_head", h, w["final_g"], target)
    grads = {"dfinal": dfinal, "dnorm": [[None] * 3 for _ in kinds],
             "dwgu1": [None, None], "dwd1": [None, None], "dwgu2": [None, None], "dwd2": [None, None]}
    for l in (1, 0):
        kind = kinds[l]
        s1, s2, s3 = saved[l]
        dx, dxb, dg, grads["dwgu2"][l], grads["dwd2"][l] = _ffn_bwd(
            f"l{l}_ffn2", dx, dxb, w["wgu2"][l], w["wd2"][l], fs, s3)
        grads["dnorm"][l][2] = dg
        if kind == "even":
            dx, dxb, dg, grads["dwqkv_e"], grads["dwout_e"], grads["dwf"], grads["db"] = _mixer_bwd(
                f"l{l}_mix", kind, dx, dxb, w["wqkv_e"], w["wout_e"], n_heads, n_sb, s2, wf=w["wf"], bf=w["bf"])
        else:
            dx, dxb, dg, grads["dwqkv_o"], grads["dwout_o"], _, _ = _mixer_bwd(
                f"l{l}_mix", kind, dx, dxb, w["wqkv_o"], w["wout_o"], n_heads, n_sb, s2, rope=rope)
        grads["dnorm"][l][1] = dg
        dx, dxb, dg, grads["dwgu1"][l], grads["dwd1"][l] = _ffn_bwd(
            f"l{l}_ffn1", dx, dxb, w["wgu1"][l], w["wd1"][l], fs, s1)
        grads["dnorm"][l][0] = dg
    return loss, dx, grads


def _cast_bf16(name, w):
    shape = w.shape
    w2 = w.reshape(-1, shape[-1])
    R, C = w2.shape
    tr = _row_tile(R, 512)

    def body(w_ref, o_ref):
        o_ref[...] = w_ref[...].astype(BF16)

    spec = pl.BlockSpec((tr, C), lambda i: (i, 0))
    out = pl.pallas_call(
        body, name=name, grid=(R // tr,), in_specs=[spec], out_specs=spec,
        out_shape=jax.ShapeDtypeStruct((R, C), BF16),
        compiler_params=pltpu.CompilerParams(dimension_semantics=("parallel",)),
    )(w2)
    return out.reshape(shape)


def _region_shape(grad, kind):
    if kind == "lead":
        return grad.shape[1] // N_CORES, grad.shape[2]
    rows, cols = grad.shape
    if kind == "cols":
        return rows // N_CORES, cols // N_CHIPS
    return rows // (N_CHIPS * N_CORES), cols


def _region_add(name, grad, kind, landed, core):
    rh, cw = _region_shape(grad, kind)
    tr = _row_tile(rh, 256)
    nrb = rh // tr
    if kind == "cols":
        g_spec = pl.BlockSpec((tr, cw), lambda k, r, c: (c[0] * nrb + r, k))
    elif kind == "rows":
        g_spec = pl.BlockSpec((tr, cw), lambda k, r, c: ((N_CORES * k + c[0]) * nrb + r, 0))
    else:
        g_spec = pl.BlockSpec((None, tr, cw), lambda k, r, c: (k, c[0] * nrb + r, 0))
    l_spec = pl.BlockSpec((None, tr, cw), lambda k, r, c: (k, r, 0))

    def body(c_ref, g_ref, l_ref, o_ref):
        o_ref[...] = g_ref[...] + l_ref[...]

    grid_spec = pltpu.PrefetchScalarGridSpec(
        num_scalar_prefetch=1, grid=(N_CHIPS, nrb), in_specs=[g_spec, l_spec], out_specs=l_spec)
    return pl.pallas_call(
        body, name=name, grid_spec=grid_spec, out_shape=jax.ShapeDtypeStruct(landed.shape, F32),
        input_output_aliases={2: 0},
        compiler_params=pltpu.CompilerParams(dimension_semantics=("parallel", "parallel"),
                                             vmem_limit_bytes=_vmem_limit(3 * _nbytes((tr, cw), F32))),
    )(core, grad, landed)


def _sum_leading(name, parts):
    n, R, C = parts.shape
    tr = _row_tile(R, max(8, (2**20 // (C * 4)) // 8 * 8))

    def body(p_ref, o_ref):
        acc = p_ref[0]
        for s in range(1, n):
            acc = acc + p_ref[s]
        o_ref[...] = acc

    return pl.pallas_call(
        body, name=name, grid=(R // tr,),
        in_specs=[pl.BlockSpec((n, tr, C), lambda i: (0, i, 0))],
        out_specs=pl.BlockSpec((tr, C), lambda i: (i, 0)),
        out_shape=jax.ShapeDtypeStruct((R, C), F32),
        compiler_params=pltpu.CompilerParams(dimension_semantics=("parallel",)),
    )(parts)


def _adamw(name, w, g, m, v):
    shape = w.shape
    to2d = lambda t: t.reshape(-1, shape[-1]) if t.ndim > 1 else t.reshape(1, -1)
    w2, g2, m2, v2 = (to2d(t) for t in (w, g, m, v))
    R, C = w2.shape
    tr = _row_tile(R, 256)

    def body(w_ref, g_ref, m_ref, v_ref, d_ref, nm_ref, nv_ref):
        gv = g_ref[...]
        nm = ADAM_B1 * m_ref[...] + (1.0 - ADAM_B1) * gv
        nv = ADAM_B2 * v_ref[...] + (1.0 - ADAM_B2) * (gv * gv)
        m_hat = nm / (1.0 - ADAM_B1 ** ADAM_STEP)
        v_hat = nv / (1.0 - ADAM_B2 ** ADAM_STEP)
        d_ref[...] = -ADAM_LR * (m_hat / (jnp.sqrt(v_hat) + ADAM_EPS) + ADAM_WD * w_ref[...])
        nm_ref[...] = nm
        nv_ref[...] = nv

    spec = pl.BlockSpec((tr, C), lambda i: (i, 0))
    sds = jax.ShapeDtypeStruct((R, C), F32)
    d, nm, nv = pl.pallas_call(
        body, name=name, grid=(R // tr,), in_specs=[spec] * 4, out_specs=[spec] * 3, out_shape=[sds] * 3,
        compiler_params=pltpu.CompilerParams(dimension_semantics=("parallel",),
                                             vmem_limit_bytes=_vmem_limit(7 * _nbytes((tr, C), F32))),
    )(w2, g2, m2, v2)
    return d.reshape(shape), nm.reshape(shape), nv.reshape(shape)


_ANY = pl.BlockSpec(memory_space=pl.ANY)


def _mesh_pos():
    return lax.axis_index("x"), lax.axis_index("y"), lax.axis_index("c")


def _other_chips(x, y):
    return [(1 - x, y), (x, 1 - y), (1 - x, 1 - y)]


def _gather_over_chips(name, srcs, out_shapes, placements):
    n = len(placements)
    n_src = len(srcs)
    n_out = len(out_shapes)

    def body(*refs):
        src = refs[:n_src]
        out = refs[n_src:n_src + n_out]
        send_sems, recv_sems, local_sems = refs[n_src + n_out:]
        x, y, c = _mesh_pos()
        chips = _other_chips(x, y)
        mine = 2 * x + y
        sends = []
        for a, (si, sview, oi, dview) in enumerate(placements):
            shard = sview(src[si])
            own = pltpu.make_async_copy(shard, dview(out[oi], mine), local_sems.at[a])
            own.start()
            sends.append(own)
            for p, (px, py) in enumerate(chips):
                cp = pltpu.make_async_remote_copy(
                    src_ref=shard, dst_ref=dview(out[oi], mine), send_sem=send_sems.at[a, p],
                    recv_sem=recv_sems.at[a, p], device_id=(px, py, c), device_id_type=MESH)
                cp.start()
                sends.append(cp)
        for a, (si, sview, oi, dview) in enumerate(placements):
            for p, (px, py) in enumerate(chips):
                pltpu.make_async_remote_copy(
                    src_ref=sview(src[si]), dst_ref=dview(out[oi], 2 * px + py), send_sem=send_sems.at[a, p],
                    recv_sem=recv_sems.at[a, p], device_id=(px, py, c), device_id_type=MESH).wait_recv()
        for a in range(n):
            sends[4 * a].wait()
            for p in range(3):
                sends[4 * a + 1 + p].wait_send()

    return pl.pallas_call(
        body, name=name, in_specs=[_ANY] * n_src, out_specs=[_ANY] * n_out, out_shape=list(out_shapes),
        scratch_shapes=[pltpu.SemaphoreType.DMA((n, 3)), pltpu.SemaphoreType.DMA((n, 3)),
                        pltpu.SemaphoreType.DMA((n,))],
        compiler_params=pltpu.CompilerParams(has_side_effects=True),
    )(*srcs)


def _region_view(ref, kind, k, c):
    if kind == "lead":
        rh = ref.shape[1] // N_CORES
        return ref.at[k, pl.ds(pl.multiple_of(c * rh, 8), rh), :]
    rows, cols = ref.shape
    if kind == "cols":
        rh, cw = rows // N_CORES, cols // N_CHIPS
        return ref.at[pl.ds(pl.multiple_of(c * rh, 8), rh), pl.ds(k * cw, cw)]
    rh = rows // (N_CHIPS * N_CORES)
    return ref.at[pl.ds(pl.multiple_of((N_CORES * k + c) * rh, 8), rh), :]


def _send_to_sibling(name, grads, kinds):
    n = len(grads)
    shapes = [jax.ShapeDtypeStruct((N_CHIPS,) + _region_shape(g, kd), F32) for g, kd in zip(grads, kinds)]

    def body(*refs):
        g_ref, land = refs[:n], refs[n:2 * n]
        send_sems, recv_sems = refs[2 * n:]
        x, y, c = _mesh_pos()
        copies = []
        for a in range(n):
            for k in range(N_CHIPS):
                cp = pltpu.make_async_remote_copy(
                    src_ref=_region_view(g_ref[a], kinds[a], k, 1 - c), dst_ref=land[a].at[k],
                    send_sem=send_sems.at[a, k], recv_sem=recv_sems.at[a, k],
                    device_id=(x, y, 1 - c), device_id_type=MESH)
                cp.start()
                copies.append(cp)
        for cp in copies:
            cp.wait_recv()
        for cp in copies:
            cp.wait_send()

    return pl.pallas_call(
        body, name=name, in_specs=[_ANY] * n, out_specs=[_ANY] * n, out_shape=shapes,
        scratch_shapes=[pltpu.SemaphoreType.DMA((n, N_CHIPS)), pltpu.SemaphoreType.DMA((n, N_CHIPS))],
        compiler_params=pltpu.CompilerParams(has_side_effects=True),
    )(*grads)


def _scatter_over_chips(name, pair_sums):
    n = len(pair_sums)

    def body(*refs):
        p_ref, land = refs[:n], refs[n:2 * n]
        send_sems, recv_sems, local_sems = refs[2 * n:]
        x, y, c = _mesh_pos()
        chips = _other_chips(x, y)
        mine = 2 * x + y
        sends = []
        for a in range(n):
            own = pltpu.make_async_copy(p_ref[a].at[mine], land[a].at[mine], local_sems.at[a])
            own.start()
            sends.append(own)
            for p, (px, py) in enumerate(chips):
                cp = pltpu.make_async_remote_copy(
                    src_ref=p_ref[a].at[2 * px + py], dst_ref=land[a].at[mine], send_sem=send_sems.at[a, p],
                    recv_sem=recv_sems.at[a, p], device_id=(px, py, c), device_id_type=MESH)
                cp.start()
                sends.append(cp)
        for a in range(n):
            for p, (px, py) in enumerate(chips):
                pltpu.make_async_remote_copy(
                    src_ref=p_ref[a].at[mine], dst_ref=land[a].at[2 * px + py], send_sem=send_sems.at[a, p],
                    recv_sem=recv_sems.at[a, p], device_id=(px, py, c), device_id_type=MESH).wait_recv()
        for a in range(n):
            sends[4 * a].wait()
            for p in range(3):
                sends[4 * a + 1 + p].wait_send()

    return pl.pallas_call(
        body, name=name, in_specs=[_ANY] * n, out_specs=[_ANY] * n,
        out_shape=[jax.ShapeDtypeStruct(p.shape, F32) for p in pair_sums],
        scratch_shapes=[pltpu.SemaphoreType.DMA((n, 3)), pltpu.SemaphoreType.DMA((n, 3)),
                        pltpu.SemaphoreType.DMA((n,))],
        compiler_params=pltpu.CompilerParams(has_side_effects=True),
    )(*pair_sums)


def _swap_halves(name, halves):
    n = len(halves)

    def body(*refs):
        h_ref, out = refs[:n], refs[n:2 * n]
        send_sems, recv_sems, local_sems = refs[2 * n:]
        x, y, c = _mesh_pos()
        sends = []
        for a in range(n):
            rh = h_ref[a].shape[0]
            rows = lambda cc, rh=rh: pl.ds(pl.multiple_of(cc * rh, 8), rh)
            own = pltpu.make_async_copy(h_ref[a], out[a].at[rows(c), :], local_sems.at[a])
            own.start()
            cp = pltpu.make_async_remote_copy(
                src_ref=h_ref[a], dst_ref=out[a].at[rows(c), :], send_sem=send_sems.at[a], recv_sem=recv_sems.at[a],
                device_id=(x, y, 1 - c), device_id_type=MESH)
            cp.start()
            sends.append((own, cp))
        for a in range(n):
            rh = h_ref[a].shape[0]
            pltpu.make_async_remote_copy(
                src_ref=h_ref[a], dst_ref=out[a].at[pl.ds(pl.multiple_of((1 - c) * rh, 8), rh), :],
                send_sem=send_sems.at[a], recv_sem=recv_sems.at[a],
                device_id=(x, y, 1 - c), device_id_type=MESH).wait_recv()
        for own, cp in sends:
            cp.wait_send()
            own.wait()

    return pl.pallas_call(
        body, name=name, in_specs=[_ANY] * n, out_specs=[_ANY] * n,
        out_shape=[jax.ShapeDtypeStruct((N_CORES * h.shape[0], h.shape[1]), F32) for h in halves],
        scratch_shapes=[pltpu.SemaphoreType.DMA((n,)), pltpu.SemaphoreType.DMA((n,)), pltpu.SemaphoreType.DMA((n,))],
        compiler_params=pltpu.CompilerParams(has_side_effects=True),
    )(*halves)


def _gather_all_devices(name, block):
    R, C = block.shape
    ndev = N_CHIPS * N_CORES

    def body(b_ref, out_ref, send_sems, recv_sems, local_sem):
        x, y, c = _mesh_pos()
        mine = 4 * x + 2 * y + c
        own = pltpu.make_async_copy(b_ref, out_ref.at[mine], local_sem)
        own.start()
        sends = []
        for mask in range(1, ndev):
            fx, fy, fc = (mask >> 2) & 1, (mask >> 1) & 1, mask & 1
            px, py, pc = x ^ fx, y ^ fy, c ^ fc
            cp = pltpu.make_async_remote_copy(
                src_ref=b_ref, dst_ref=out_ref.at[mine], send_sem=send_sems.at[mask - 1],
                recv_sem=recv_sems.at[mask - 1], device_id=(px, py, pc), device_id_type=MESH)
            cp.start()
            sends.append(cp)
        for mask in range(1, ndev):
            fx, fy, fc = (mask >> 2) & 1, (mask >> 1) & 1, mask & 1
            px, py, pc = x ^ fx, y ^ fy, c ^ fc
            pltpu.make_async_remote_copy(
                src_ref=b_ref, dst_ref=out_ref.at[4 * px + 2 * py + pc], send_sem=send_sems.at[mask - 1],
                recv_sem=recv_sems.at[mask - 1], device_id=(px, py, pc), device_id_type=MESH).wait_recv()
        for cp in sends:
            cp.wait_send()
        own.wait()

    return pl.pallas_call(
        body, name=name, in_specs=[_ANY], out_specs=_ANY,
        out_shape=jax.ShapeDtypeStruct((ndev, R, C), F32),
        scratch_shapes=[pltpu.SemaphoreType.DMA((ndev - 1,)), pltpu.SemaphoreType.DMA((ndev - 1,)),
                        pltpu.SemaphoreType.DMA(())],
        compiler_params=pltpu.CompilerParams(has_side_effects=True),
    )(block)


def _reduce_scatter(grads, kinds, core):
    landed = _send_to_sibling("rs_pair_send", grads, kinds)
    pair = [_region_add(f"rs_pair_add_{a}", g, kd, ld, core) for a, (g, kd, ld) in enumerate(zip(grads, kinds, landed))]
    parts = _scatter_over_chips("rs_chip_send", pair)
    halves = []
    for a, p in enumerate(parts):
        nchip, rh, cw = p.shape
        halves.append(_sum_leading(f"rs_chip_add_{a}", p))
    return _swap_halves("rs_swap_halves", halves)


def kernel(x, norm_g, ffn1_w_gate, ffn1_w_up, ffn1_w_down, ffn2_w_gate, ffn2_w_up, ffn2_w_down, even_w_in, even_b_forget, even_w_out, odd_w_qkv, odd_w_out, final_norm_g, loss_target, m_norm_g, m_ffn1_w_gate, m_ffn1_w_up, m_ffn1_w_down, m_ffn2_w_gate, m_ffn2_w_up, m_ffn2_w_down, m_even_w_in, m_even_b_forget, m_even_w_out, m_odd_w_qkv, m_odd_w_out, m_final_norm_g, v_norm_g, v_ffn1_w_gate, v_ffn1_w_up, v_ffn1_w_down, v_ffn2_w_gate, v_ffn2_w_up, v_ffn2_w_down, v_even_w_in, v_even_b_forget, v_even_w_out, v_odd_w_qkv, v_odd_w_out, v_final_norm_g):
    _, S, D = x.shape
    L = norm_g.shape[0]
    assert L == 2 and even_w_in.shape[0] == 1 and odd_w_qkv.shape[0] == 1
    fs = ffn1_w_gate.shape[2]
    F = N_CHIPS * fs
    wc = even_w_in.shape[2]
    n_heads = D // HEAD_DIM
    n_fox = N_CHIPS * wc - 3 * D
    n_sb = n_heads - n_fox
    qs = odd_w_qkv.shape[2]
    os_ = even_w_out.shape[1]
    ns = norm_g.shape[2]
    xi, yi, ci = _mesh_pos()
    chip = 2 * xi + yi

    names = ["g1", "u1", "d1", "g2", "u2", "d2", "win", "woute", "wqkv", "wouto"]
    shards = [ffn1_w_gate, ffn1_w_up, ffn1_w_down, ffn2_w_gate, ffn2_w_up, ffn2_w_down,
              even_w_in[0], even_w_out[0], odd_w_qkv[0], odd_w_out[0]]
    srcs = [_cast_bf16("cast_" + nm, s) for nm, s in zip(names, shards)] + [norm_g]
    lane = lambda start, size: pl.ds(pl.multiple_of(start, LANES), size)
    sub = lambda start, size: pl.ds(pl.multiple_of(start, 8), size)
    gate_view = lambda r, k: r.at[:, lane(k * 2 * fs, fs)]
    up_view = lambda r, k: r.at[:, lane(k * 2 * fs + fs, fs)]
    down_view = lambda r, k: r.at[sub(k * fs, fs), :]
    whole = lambda r: r
    out_shapes, placements = [], []
    for f in range(2):
        for l in range(L):
            layer = lambda r, l=l: r.at[l]
            out_shapes += [jax.ShapeDtypeStruct((D, 2 * F), BF16), jax.ShapeDtypeStruct((F, D), BF16)]
            o = len(out_shapes) - 2
            placements += [(3 * f, layer, o, gate_view), (3 * f + 1, layer, o, up_view), (3 * f + 2, layer, o + 1, down_view)]
    o = len(out_shapes)
    out_shapes += [
        jax.ShapeDtypeStruct((N_CHIPS, D, wc), BF16), jax.ShapeDtypeStruct((D, D), BF16),
        jax.ShapeDtypeStruct((D, N_CHIPS * qs), BF16), jax.ShapeDtypeStruct((D, D), BF16),
        jax.ShapeDtypeStruct((L, 3, N_CHIPS * ns), F32)]
    placements += [
        (6, whole, o, lambda r, k: r.at[k]), (7, whole, o + 1, lambda r, k: r.at[sub(k * os_, os_), :]),
        (8, whole, o + 2, lambda r, k: r.at[:, lane(k * qs, qs)]), (9, whole, o + 3, lambda r, k: r.at[sub(k * os_, os_), :]),
        (10, whole, o + 4, lambda r, k: r.at[:, :, lane(k * ns, ns)])]
    full = _gather_over_chips("gather_weights", srcs, out_shapes, placements)
    wgu1, wd1 = [full[2 * l] for l in range(L)], [full[2 * l + 1] for l in range(L)]
    wgu2, wd2 = [full[2 * L + 2 * l] for l in range(L)], [full[2 * L + 2 * l + 1] for l in range(L)]
    win4, wout_e, wqkv_o, wout_o, norm_full = full[4 * L:]
    win = jnp.concatenate([win4[k] for k in range(N_CHIPS)], axis=1)
    wf = jnp.pad(win[:, 3 * D:], ((0, 0), (0, LANES - n_fox)))
    bf = jnp.pad(even_b_forget, ((0, 0), (0, LANES - n_fox)))
    w = {"wgu1": wgu1, "wd1": wd1, "wgu2": wgu2, "wd2": wd2, "wqkv_e": win[:, :3 * D], "wf": wf, "bf": bf,
         "wout_e": wout_e, "wqkv_o": wqkv_o, "wout_o": wout_o, "norm_g": norm_full,
         "final_g": final_norm_g[None, :]}

    loss_vec, grad_x, g = _local_step(x[0], loss_target[0], w, fs, n_heads, n_sb)

    dwin = jnp.concatenate([g["dwqkv_e"], g["dwf"][:, :n_fox]], axis=1)
    dwin4 = jnp.stack([dwin[:, k * wc:(k + 1) * wc] for k in range(N_CHIPS)])
    rs_in = (g["dwgu1"] + g["dwgu2"] + g["dwd1"] + g["dwd2"]
             + [dwin4, g["dwqkv_o"], g["dwout_e"], g["dwout_o"]])
    kinds = ["cols"] * (2 * L) + ["rows"] * (2 * L) + ["lead", "cols", "rows", "rows"]
    red = _reduce_scatter(rs_in, kinds, ci.astype(jnp.int32).reshape(1))
    gu1, gu2, gd1, gd2 = red[0:L], red[L:2 * L], red[2 * L:3 * L], red[3 * L:4 * L]
    g_win, g_qkv_o, g_wout_e, g_wout_o = red[4 * L:]

    small_rows = [g["dnorm"][l][i] for l in range(L) for i in range(3)] + [
        g["dfinal"], jnp.pad(g["db"], ((0, 0), (0, D - LANES))), jnp.pad(loss_vec, ((0, 0), (0, D - LANES)))]
    small = jnp.concatenate(small_rows + [jnp.zeros((16 - len(small_rows), D), F32)], axis=0)
    small_sum = _sum_leading("small_sum", _gather_all_devices("small_gather", small))
    loss = small_sum[3 * L + 2, 0]
    g_norm = lax.dynamic_slice_in_dim(small_sum[:3 * L].reshape(L, 3, D), chip * ns, ns, axis=2)
    g_final = small_sum[3 * L]
    g_bf = small_sum[3 * L + 1, :n_fox][None, :]

    grads = [
        g_norm,
        jnp.stack([t[:, :fs] for t in gu1]), jnp.stack([t[:, fs:] for t in gu1]), jnp.stack(gd1),
        jnp.stack([t[:, :fs] for t in gu2]), jnp.stack([t[:, fs:] for t in gu2]), jnp.stack(gd2),
        g_win[None], g_bf, g_wout_e[None], g_qkv_o[None], g_wout_o[None], g_final]
    weights = [norm_g, ffn1_w_gate, ffn1_w_up, ffn1_w_down, ffn2_w_gate, ffn2_w_up, ffn2_w_down,
               even_w_in, even_b_forget, even_w_out, odd_w_qkv, odd_w_out, final_norm_g]
    ms = [m_norm_g, m_ffn1_w_gate, m_ffn1_w_up, m_ffn1_w_down, m_ffn2_w_gate, m_ffn2_w_up, m_ffn2_w_down,
          m_even_w_in, m_even_b_forget, m_even_w_out, m_odd_w_qkv, m_odd_w_out, m_final_norm_g]
    vs = [v_norm_g, v_ffn1_w_gate, v_ffn1_w_up, v_ffn1_w_down, v_ffn2_w_gate, v_ffn2_w_up, v_ffn2_w_down,
          v_even_w_in, v_even_b_forget, v_even_w_out, v_odd_w_qkv, v_odd_w_out, v_final_norm_g]
    deltas, new_ms, new_vs = [], [], []
    for i, (wt, gt, mt, vt) in enumerate(zip(weights, grads, ms, vs)):
        d, nm, nv = _adamw(f"adamw_{i}", wt, gt, mt, vt)
        deltas.append(d)
        new_ms.append(nm)
        new_vs.append(nv)
    return (loss, grad_x[None], *grads, *deltas, *new_ms, *new_vs)
```

```python
import functools
import math

import jax
import jax.numpy as jnp
from jax import lax
from jax.experimental import pallas as pl
from jax.experimental.pallas import tpu as pltpu

F32 = jnp.float32
BF16 = jnp.bfloat16

HEAD_DIM = 128
ROPE_DIMS = 32
ROPE_THETA = 500000.0
DILATED_PATTERNS = ((128, 1), (512, 4), (2048, 16))
RMS_EPS = 1e-6
NEG_INF = -1e30
ADAM_LR = 0.001
ADAM_B1 = 0.9
ADAM_B2 = 0.999
ADAM_EPS = 1e-08
ADAM_WD = 0.01
ADAM_STEP = 10

N_CHIPS = 4
N_CORES = 2
LANES = 128
BLK = 128
VMEM_BYTES_V7X = 64 * 2**20
MESH = pl.DeviceIdType.MESH


def _vmem_limit(block_bytes, scratch_bytes=0):
    need = 2 * block_bytes + scratch_bytes + 12 * 2**20
    return int(min(need, VMEM_BYTES_V7X - 6 * 2**20))


def _nbytes(shape, dtype):
    return math.prod(shape) * jnp.dtype(dtype).itemsize


def _tile(dim, target):
    best = None
    for t in range(LANES, min(dim, target) + 1, LANES):
        if dim % t == 0:
            best = t
    assert best is not None, (dim, target)
    return best


def _row_tile(rows, target, step=8):
    if rows <= target:
        return rows
    best = None
    for t in range(step, target + 1, step):
        if rows % t == 0:
            best = t
    assert best is not None, (rows, target)
    return best


def _mm(name, a, b, mode, out_dtype, res=None, alpha=1.0, tm_target=512, tn_target=1536, tk_target=1024):
    a3 = a.ndim == 3
    b3 = b.ndim == 3
    if mode == "nn":
        assert not a3 and not b3
        (M, K), (K2, N) = a.shape, b.shape
    elif mode == "nt":
        assert not b3
        if a3:
            P, M, Kp = a.shape
            K = P * Kp
        else:
            M, K = a.shape
        N, K2 = b.shape
    else:
        assert mode == "tn" and not a3
        K, M = a.shape
        if b3:
            P, K2, Np = b.shape
            N = P * Np
        else:
            K2, N = b.shape
    assert K == K2, (name, a.shape, b.shape)
    tm = _tile(M, tm_target)
    tn = _tile(Np if b3 else N, tn_target)
    tk = _tile(Kp if a3 else K, tk_target)
    nk = K // tk
    grid = (M // tm, N // tn, nk)

    if mode == "nn":
        a_spec = pl.BlockSpec((tm, tk), lambda i, j, k: (i, k))
        b_spec = pl.BlockSpec((tk, tn), lambda i, j, k: (k, j))
        dims = (((1,), (0,)), ((), ()))
    elif mode == "nt":
        if a3:
            nkp = Kp // tk
            a_spec = pl.BlockSpec((None, tm, tk), lambda i, j, k: (k // nkp, i, k % nkp))
        else:
            a_spec = pl.BlockSpec((tm, tk), lambda i, j, k: (i, k))
        b_spec = pl.BlockSpec((tn, tk), lambda i, j, k: (j, k))
        dims = (((1,), (1,)), ((), ()))
    else:
        a_spec = pl.BlockSpec((tk, tm), lambda i, j, k: (k, i))
        if b3:
            njp = Np // tn
            b_spec = pl.BlockSpec((None, tk, tn), lambda i, j, k: (j // njp, k, j % njp))
        else:
            b_spec = pl.BlockSpec((tk, tn), lambda i, j, k: (k, j))
        dims = (((0,), (0,)), ((), ()))
    o_spec = pl.BlockSpec((tm, tn), lambda i, j, k: (i, j))
    has_res = res is not None

    def body(*refs):
        if has_res:
            a_ref, b_ref, r_ref, o_ref, acc_ref = refs
        else:
            a_ref, b_ref, o_ref, acc_ref = refs
        k = pl.program_id(2)

        @pl.when(k == 0)
        def _():
            acc_ref[...] = jnp.zeros_like(acc_ref)

        acc_ref[...] += lax.dot_general(a_ref[...], b_ref[...], dims, preferred_element_type=F32)

        @pl.when(k == nk - 1)
        def _():
            y = acc_ref[...] * alpha
            if has_res:
                y = r_ref[...] + y
            o_ref[...] = y.astype(o_ref.dtype)

    in_specs = [a_spec, b_spec] + ([o_spec] if has_res else [])
    args = (a, b) + ((res,) if has_res else ())
    blk = (_nbytes((tm, tk), a.dtype) + _nbytes((tk, tn), b.dtype) + _nbytes((tm, tn), out_dtype)
           + (_nbytes((tm, tn), F32) if has_res else 0))
    return pl.pallas_call(
        body, name=name, grid=grid, in_specs=in_specs, out_specs=o_spec,
        out_shape=jax.ShapeDtypeStruct((M, N), out_dtype),
        scratch_shapes=[pltpu.VMEM((tm, tn), F32)],
        compiler_params=pltpu.CompilerParams(
            dimension_semantics=("parallel", "parallel", "arbitrary"),
            vmem_limit_bytes=_vmem_limit(blk, 2 * _nbytes((tm, tn), F32))),
    )(*args)


def _rms_fwd(name, x, g):
    S, D = x.shape
    tr = _row_tile(S, 256)

    def body(x_ref, g_ref, n_ref):
        xv = x_ref[...]
        r = lax.rsqrt(jnp.mean(xv * xv, axis=-1, keepdims=True) + RMS_EPS)
        n_ref[...] = (xv * r * g_ref[...]).astype(BF16)

    return pl.pallas_call(
        body, name=name, grid=(S // tr,),
        in_specs=[pl.BlockSpec((tr, D), lambda i: (i, 0)), pl.BlockSpec((1, D), lambda i: (0, 0))],
        out_specs=pl.BlockSpec((tr, D), lambda i: (i, 0)),
        out_shape=jax.ShapeDtypeStruct((S, D), BF16),
        compiler_params=pltpu.CompilerParams(dimension_semantics=("parallel",)),
    )(x, g)


def _rms_bwd(name, dn, x, g, dres):
    S, D = x.shape
    tr = _row_tile(S, 256)

    def body(dn_ref, x_ref, g_ref, dres_ref, dx_ref, dxb_ref, dg_ref):
        i = pl.program_id(0)
        xv = x_ref[...]
        dnv = dn_ref[...]
        r = lax.rsqrt(jnp.mean(xv * xv, axis=-1, keepdims=True) + RMS_EPS)
        u = dnv * g_ref[...]
        dot = jnp.mean(u * xv, axis=-1, keepdims=True)
        dx = dres_ref[...] + r * u - xv * (r * r * r * dot)
        dx_ref[...] = dx
        dxb_ref[...] = dx.astype(BF16)

        @pl.when(i == 0)
        def _():
            dg_ref[...] = jnp.zeros_like(dg_ref)

        dg_ref[...] += jnp.sum(dnv * xv * r, axis=0, keepdims=True)

    row = pl.BlockSpec((tr, D), lambda i: (i, 0))
    vec = pl.BlockSpec((1, D), lambda i: (0, 0))
    return pl.pallas_call(
        body, name=name, grid=(S // tr,),
        in_specs=[row, row, vec, row], out_specs=[row, row, vec],
        out_shape=[jax.ShapeDtypeStruct((S, D), F32), jax.ShapeDtypeStruct((S, D), BF16),
                   jax.ShapeDtypeStruct((1, D), F32)],
        compiler_params=pltpu.CompilerParams(dimension_semantics=("arbitrary",)),
    )(dn, x, g, dres)


def _loss_head(name, x, g, target):
    S, D = x.shape
    tr = _row_tile(S, 256)

    def body(x_ref, g_ref, t_ref, dx_ref, dxb_ref, dg_ref, loss_ref):
        i = pl.program_id(0)
        xv = x_ref[...]
        gv = g_ref[...]
        r = lax.rsqrt(jnp.mean(xv * xv, axis=-1, keepdims=True) + RMS_EPS)
        diff = xv * r * gv - t_ref[...]
        part = 0.5 * jnp.sum(jnp.mean(diff * diff, axis=-1, keepdims=True), axis=0, keepdims=True)
        dy = diff * (1.0 / D)
        u = dy * gv
        dot = jnp.mean(u * xv, axis=-1, keepdims=True)
        dx = r * u - xv * (r * r * r * dot)
        dx_ref[...] = dx
        dxb_ref[...] = dx.astype(BF16)

        @pl.when(i == 0)
        def _():
            dg_ref[...] = jnp.zeros_like(dg_ref)
            loss_ref[...] = jnp.zeros_like(loss_ref)

        dg_ref[...] += jnp.sum(dy * xv * r, axis=0, keepdims=True)
        loss_ref[...] += jnp.broadcast_to(part, loss_ref.shape)

    row = pl.BlockSpec((tr, D), lambda i: (i, 0))
    vec = pl.BlockSpec((1, D), lambda i: (0, 0))
    lvec = pl.BlockSpec((1, LANES), lambda i: (0, 0))
    return pl.pallas_call(
        body, name=name, grid=(S // tr,),
        in_specs=[row, vec, row], out_specs=[row, row, vec, lvec],
        out_shape=[jax.ShapeDtypeStruct((S, D), F32), jax.ShapeDtypeStruct((S, D), BF16),
                   jax.ShapeDtypeStruct((1, D), F32), jax.ShapeDtypeStruct((1, LANES), F32)],
        compiler_params=pltpu.CompilerParams(dimension_semantics=("arbitrary",)),
    )(x, g, target)


def _swiglu_fwd(name, gu, fs):
    S, two_f = gu.shape
    nslab = two_f // (2 * fs)
    tr = _row_tile(S, 256)

    def body(gu_ref, h_ref):
        gv = gu_ref[:, :fs]
        uv = gu_ref[:, fs:]
        h_ref[...] = (gv * jax.nn.sigmoid(gv) * uv).astype(BF16)

    return pl.pallas_call(
        body, name=name, grid=(S // tr, nslab),
        in_specs=[pl.BlockSpec((tr, 2 * fs), lambda i, k: (i, k))],
        out_specs=pl.BlockSpec((tr, fs), lambda i, k: (i, k)),
        out_shape=jax.ShapeDtypeStruct((S, nslab * fs), BF16),
        compiler_params=pltpu.CompilerParams(dimension_semantics=("parallel", "parallel")),
    )(gu)


def _swiglu_bwd(name, dh, gu, fs):
    S, two_f = gu.shape
    nslab = two_f // (2 * fs)
    tr = _row_tile(S, 256)

    def body(dh_ref, gu_ref, o_ref):
        gv = gu_ref[:, :fs]
        uv = gu_ref[:, fs:]
        dhv = dh_ref[...]
        sg = jax.nn.sigmoid(gv)
        silu = gv * sg
        o_ref[:, :fs] = (dhv * uv * (sg + silu * (1.0 - sg))).astype(BF16)
        o_ref[:, fs:] = (dhv * silu).astype(BF16)

    return pl.pallas_call(
        body, name=name, grid=(S // tr, nslab),
        in_specs=[pl.BlockSpec((tr, fs), lambda i, k: (i, k)), pl.BlockSpec((tr, 2 * fs), lambda i, k: (i, k))],
        out_specs=pl.BlockSpec((tr, 2 * fs), lambda i, k: (i, k)),
        out_shape=jax.ShapeDtypeStruct((S, two_f), BF16),
        compiler_params=pltpu.CompilerParams(dimension_semantics=("parallel", "parallel")),
    )(dh, gu)


def _tri_rows(r0, nrows, ncols, lower):
    row = lax.broadcasted_iota(jnp.int32, (nrows, ncols), 0) + r0
    col = lax.broadcasted_iota(jnp.int32, (nrows, ncols), 1)
    return jnp.where((col <= row) if lower else (col >= row), 1.0, 0.0).astype(F32)


def _gate_fwd(name, hf, b):
    S = hf.shape[0]
    tb = _row_tile(S, 256)

    def body(hf_ref, b_ref, cf_ref, cft_ref, lf_ref):
        zz = hf_ref[...] + b_ref[...]
        lf_ref[...] = jnp.minimum(zz, 0.0) - jnp.log1p(jnp.exp(-jnp.abs(zz)))

        def blk(i, c):
            r0 = pl.multiple_of(i * tb, tb)
            tri = _tri_rows(r0, tb, S, True)
            cf_ref[pl.ds(r0, tb), :] = jnp.dot(tri, lf_ref[...], precision=lax.Precision.HIGHEST,
                                               preferred_element_type=F32)
            return c

        lax.fori_loop(0, S // tb, blk, 0)
        cft_ref[...] = cf_ref[...].T

    full = pl.BlockSpec((S, LANES), lambda: (0, 0))
    return pl.pallas_call(
        body, name=name, in_specs=[full, pl.BlockSpec((1, LANES), lambda: (0, 0))],
        out_specs=[full, pl.BlockSpec((LANES, S), lambda: (0, 0))],
        out_shape=[jax.ShapeDtypeStruct((S, LANES), F32), jax.ShapeDtypeStruct((LANES, S), F32)],
        scratch_shapes=[pltpu.VMEM((S, LANES), F32)],
    )(hf, b)


def _gate_bwd(name, dcft, drow, hf, b):
    S = hf.shape[0]
    tb = _row_tile(S, 256)

    def body(dcft_ref, drow_ref, hf_ref, b_ref, dhf_ref, db_ref, dcf_ref, dlf_ref):
        dcf_ref[...] = dcft_ref[...].T + drow_ref[...]

        def blk(i, c):
            r0 = pl.multiple_of(i * tb, tb)
            tri = _tri_rows(r0, tb, S, False)
            dlf_ref[pl.ds(r0, tb), :] = jnp.dot(tri, dcf_ref[...], precision=lax.Precision.HIGHEST,
                                                preferred_element_type=F32)
            return c

        lax.fori_loop(0, S // tb, blk, 0)
        zz = hf_ref[...] + b_ref[...]
        dhf = dlf_ref[...] * jax.nn.sigmoid(-zz)
        dhf_ref[...] = dhf.astype(BF16)
        db_ref[...] = jnp.sum(dhf, axis=0, keepdims=True)

    full = pl.BlockSpec((S, LANES), lambda: (0, 0))
    vec = pl.BlockSpec((1, LANES), lambda: (0, 0))
    return pl.pallas_call(
        body, name=name, in_specs=[pl.BlockSpec((LANES, S), lambda: (0, 0)), full, full, vec],
        out_specs=[full, vec],
        out_shape=[jax.ShapeDtypeStruct((S, LANES), BF16), jax.ShapeDtypeStruct((1, LANES), F32)],
        scratch_shapes=[pltpu.VMEM((S, LANES), F32), pltpu.VMEM((S, LANES), F32)],
    )(dcft, drow, hf, b)


def _rope_tables(S):
    half = ROPE_DIMS // 2
    freqs = ROPE_THETA ** (-jnp.arange(half, dtype=F32) / half)
    ang = jnp.arange(S, dtype=F32)[:, None] * freqs[None, :]
    cos, sin = jnp.cos(ang), jnp.sin(ang)
    pad = HEAD_DIM - ROPE_DIMS
    c = jnp.concatenate([cos, cos, jnp.ones((S, pad), F32)], axis=1)
    s = jnp.concatenate([-sin, sin, jnp.zeros((S, pad), F32)], axis=1)
    return c, s


def _rope_swap(x):
    half = ROPE_DIMS // 2
    lane = lax.broadcasted_iota(jnp.int32, x.shape, 1)
    upper = jnp.where(lane < ROPE_DIMS, pltpu.roll(x, half, 1), 0.0)
    return jnp.where(lane < half, pltpu.roll(x, HEAD_DIM - half, 1), upper)


def _rope(x, c, s):
    return x * c + _rope_swap(x) * s


def _rope_t(dy, c, s):
    return dy * c + _rope_swap(dy * s)


def _split_dot(x, t):
    hi = x.astype(BF16)
    lo = (x - hi.astype(F32)).astype(BF16)
    return (jnp.dot(hi, t, preferred_element_type=F32) + jnp.dot(lo, t, preferred_element_type=F32))


_NT = (((1,), (1,)), ((), ()))
_TN = (((0,), (0,)), ((), ()))


def _dot_nt(a, b):
    return lax.dot_general(a, b, _NT, preferred_element_type=F32)


def _dot_tn(a, b):
    return lax.dot_general(a, b, _TN, preferred_element_type=F32)


def _blk(i):
    return pl.ds(pl.multiple_of(i * BLK, BLK), BLK)


def _delta(i, j):
    row = lax.broadcasted_iota(jnp.int32, (BLK, BLK), 0)
    col = lax.broadcasted_iota(jnp.int32, (BLK, BLK), 1)
    return (row - col) + (i - j) * BLK


def _dilated_mult(delta):
    c = jnp.zeros(delta.shape, F32)
    for window, dil in DILATED_PATTERNS:
        ok = (delta >= 0) & (delta <= window) & ((delta & (dil - 1)) == 0)
        c = c + jnp.where(ok, 1.0, 0.0)
    return c


def _sb_terms(z, mask, t_ex, run):
    t = jnp.log1p(jnp.exp(-jnp.abs(z)))
    lsig = jnp.minimum(z, 0.0) - t
    m = jnp.where(mask, -(jnp.maximum(z, 0.0) + t), 0.0)
    after = _split_dot(m, t_ex)
    a = jnp.where(mask, jnp.exp(lsig + after + run), 0.0)
    return a, m, lsig


def _attn_fwd(name, hq, layer_kind, n_heads, n_sb, cf=None, cft=None, rope_c=None, rope_s=None):
    S = hq.shape[0]
    D = n_heads * HEAD_DIM
    nq = S // BLK
    scale = HEAD_DIM ** -0.5
    even = layer_kind == "even"

    def body(*refs):
        if even:
            q_ref, k_ref, v_ref, cf_ref, cft_ref, o_ref, ob_ref, lse_ref, qs, ks, vs = refs
        else:
            q_ref, k_ref, v_ref, c_ref, s_ref, o_ref, ob_ref, lse_ref, qs, ks, vs = refs
        h = pl.program_id(0)
        if even:
            qs[...] = q_ref[...].astype(BF16)
            ks[...] = k_ref[...].astype(BF16)
        else:
            qs[...] = _rope(q_ref[...], c_ref[...], s_ref[...]).astype(BF16)
            ks[...] = _rope(k_ref[...], c_ref[...], s_ref[...]).astype(BF16)
        vs[...] = v_ref[...].astype(BF16)

        def softmax_head(hh):
            def qblock(i, carry):
                qi = qs[_blk(i), :]
                if even:
                    lane = lax.broadcasted_iota(jnp.int32, (BLK, LANES), 1)
                    cfq = jnp.sum(jnp.where(lane == hh, cf_ref[_blk(i), :], 0.0), axis=1, keepdims=True)

                def kblock(j, c):
                    m_run, l_run, acc = c
                    z = _dot_nt(qi, ks[_blk(j), :]) * scale
                    delta = _delta(i, j)
                    if even:
                        z = z + cfq - cft_ref[hh, :, _blk(j)]
                        ok = delta >= 0
                    else:
                        mult = _dilated_mult(delta)
                        ok = mult > 0.0
                    z = jnp.where(ok, z, NEG_INF)
                    m_new = jnp.maximum(m_run, jnp.max(z, axis=1, keepdims=True))
                    p = jnp.exp(z - m_new)
                    if not even:
                        p = p * mult
                    alpha = jnp.exp(m_run - m_new)
                    l_new = alpha * l_run + jnp.sum(p, axis=1, keepdims=True)
                    acc = alpha * acc + jnp.dot(p.astype(BF16), vs[_blk(j), :], preferred_element_type=F32)
                    return m_new, l_new, acc

                init = (jnp.full((BLK, 1), NEG_INF, F32), jnp.zeros((BLK, 1), F32), jnp.zeros((BLK, HEAD_DIM), F32))
                m_run, l_run, acc = lax.fori_loop(0, i + 1, kblock, init)
                o = acc / l_run
                o_ref[_blk(i), :] = o
                ob_ref[_blk(i), :] = o.astype(BF16)
                lse_ref[_blk(i), :] = jnp.broadcast_to(m_run + jnp.log(l_run), (BLK, HEAD_DIM))
                return carry

            lax.fori_loop(0, nq, qblock, 0)

        def sb_head():
            row = lax.broadcasted_iota(jnp.int32, (BLK, BLK), 0)
            col = lax.broadcasted_iota(jnp.int32, (BLK, BLK), 1)
            t_ex = jnp.where(row > col, 1.0, 0.0).astype(BF16)

            def qblock(i, carry):
                qi = qs[_blk(i), :]

                def kblock(jj, c):
                    run, acc = c
                    j = i - jj
                    z = _dot_nt(qi, ks[_blk(j), :]) * scale
                    a, m, _ = _sb_terms(z, _delta(i, j) > 0, t_ex, run)
                    acc = acc + jnp.dot(a.astype(BF16), vs[_blk(j), :], preferred_element_type=F32)
                    return run + jnp.sum(m, axis=1, keepdims=True), acc

                init = (jnp.zeros((BLK, 1), F32), jnp.zeros((BLK, HEAD_DIM), F32))
                _, acc = lax.fori_loop(0, i + 1, kblock, init)
                o_ref[_blk(i), :] = acc
                ob_ref[_blk(i), :] = acc.astype(BF16)
                lse_ref[_blk(i), :] = jnp.zeros((BLK, HEAD_DIM), F32)
                return carry

            lax.fori_loop(0, nq, qblock, 0)

        if even:
            @pl.when(h < n_sb)
            def _():
                sb_head()

            @pl.when(h >= n_sb)
            def _():
                softmax_head(h - n_sb)
        else:
            softmax_head(h)

    head = lambda off: pl.BlockSpec((S, HEAD_DIM), lambda h, off=off: (0, off + h))
    full = pl.BlockSpec((S, LANES), lambda h: (0, 0))
    if even:
        extra_specs = [full, pl.BlockSpec(cft.shape, lambda h: (0, 0, 0))]
        extra = (cf, cft)
    else:
        extra_specs = [full, full]
        extra = (rope_c, rope_s)
    blk_bytes = 8 * _nbytes((S, HEAD_DIM), F32)
    return pl.pallas_call(
        body, name=name, grid=(n_heads,),
        in_specs=[head(0), head(n_heads), head(2 * n_heads)] + extra_specs,
        out_specs=[head(0), head(0), head(0)],
        out_shape=[jax.ShapeDtypeStruct((S, D), F32), jax.ShapeDtypeStruct((S, D), BF16),
                   jax.ShapeDtypeStruct((S, D), F32)],
        scratch_shapes=[pltpu.VMEM((S, HEAD_DIM), BF16)] * 3,
        compiler_params=pltpu.CompilerParams(dimension_semantics=("arbitrary",),
                                             vmem_limit_bytes=_vmem_limit(blk_bytes, 3 * _nbytes((S, HEAD_DIM), BF16))),
    )(hq, hq, hq, *extra)


def _attn_bwd(name, hq, do, o, lse, layer_kind, n_heads, n_sb, cf=None, cft=None, rope_c=None, rope_s=None):
    S = hq.shape[0]
    D = n_heads * HEAD_DIM
    nq = S // BLK
    scale = HEAD_DIM ** -0.5
    even = layer_kind == "even"

    def body(*refs):
        if even:
            (q_ref, k_ref, v_ref, do_ref, o_ref, lse_ref, cf_ref, cft_ref,
             dh_ref, dcft_ref, drow_ref, qs, ks, vs, dos, dq_acc, dk_acc, dv_acc) = refs
        else:
            (q_ref, k_ref, v_ref, do_ref, o_ref, lse_ref, c_ref, s_ref,
             dh_ref, qs, ks, vs, dos, dq_acc, dk_acc, dv_acc) = refs
        h = pl.program_id(0)
        if even:
            qs[...] = q_ref[...].astype(BF16)
            ks[...] = k_ref[...].astype(BF16)

            @pl.when(h == 0)
            def _():
                dcft_ref[...] = jnp.zeros_like(dcft_ref)
                drow_ref[...] = jnp.zeros_like(drow_ref)
        else:
            qs[...] = _rope(q_ref[...], c_ref[...], s_ref[...]).astype(BF16)
            ks[...] = _rope(k_ref[...], c_ref[...], s_ref[...]).astype(BF16)
        vs[...] = v_ref[...].astype(BF16)
        dos[...] = do_ref[...].astype(BF16)
        dk_acc[...] = jnp.zeros_like(dk_acc)
        dv_acc[...] = jnp.zeros_like(dv_acc)

        def softmax_head(hh):
            def qblock(i, carry):
                qi = qs[_blk(i), :]
                doi = dos[_blk(i), :]
                dvec = jnp.sum(do_ref[_blk(i), :] * o_ref[_blk(i), :], axis=1, keepdims=True)
                lse_i = jnp.max(lse_ref[_blk(i), :], axis=1, keepdims=True)
                if even:
                    lane = lax.broadcasted_iota(jnp.int32, (BLK, LANES), 1)
                    cfq = jnp.sum(jnp.where(lane == hh, cf_ref[_blk(i), :], 0.0), axis=1, keepdims=True)

                def kblock(j, c):
                    dq, ds_rows = c
                    kj = ks[_blk(j), :]
                    z = _dot_nt(qi, kj) * scale
                    delta = _delta(i, j)
                    if even:
                        z = z + cfq - cft_ref[hh, :, _blk(j)]
                        ok = delta >= 0
                    else:
                        mult = _dilated_mult(delta)
                        ok = mult > 0.0
                    p = jnp.exp(jnp.where(ok, z, NEG_INF) - lse_i)
                    if not even:
                        p = p * mult
                    dp = _dot_nt(doi, vs[_blk(j), :])
                    ds = p * (dp - dvec)
                    dsb = (ds * scale).astype(BF16)
                    dk_acc[_blk(j), :] += _dot_tn(dsb, qi)
                    dv_acc[_blk(j), :] += _dot_tn(p.astype(BF16), doi)
                    if even:
                        dcft_ref[hh, :, _blk(j)] += -jnp.sum(ds, axis=0, keepdims=True)
                    return (dq + jnp.dot(dsb, kj, preferred_element_type=F32),
                            ds_rows + jnp.sum(ds, axis=1, keepdims=True))

                dq, ds_rows = lax.fori_loop(0, i + 1, kblock,
                                            (jnp.zeros((BLK, HEAD_DIM), F32), jnp.zeros((BLK, 1), F32)))
                dq_acc[_blk(i), :] = dq
                if even:
                    drow_ref[_blk(i), :] += jnp.where(lane == hh, ds_rows, 0.0)
                return carry

            lax.fori_loop(0, nq, qblock, 0)

        def sb_head():
            row = lax.broadcasted_iota(jnp.int32, (BLK, BLK), 0)
            col = lax.broadcasted_iota(jnp.int32, (BLK, BLK), 1)
            t_ex = jnp.where(row > col, 1.0, 0.0).astype(BF16)
            t_in = jnp.where(row >= col, 1.0, 0.0).astype(BF16)

            def qblock(i, carry):
                qi = qs[_blk(i), :]
                doi = dos[_blk(i), :]

                def e_total(jj, c):
                    run, tot = c
                    j = i - jj
                    z = _dot_nt(qi, ks[_blk(j), :]) * scale
                    a, m, _ = _sb_terms(z, _delta(i, j) > 0, t_ex, run)
                    e = _dot_nt(doi, vs[_blk(j), :]) * a
                    return run + jnp.sum(m, axis=1, keepdims=True), tot + jnp.sum(e, axis=1, keepdims=True)

                zero = jnp.zeros((BLK, 1), F32)
                _, e_tot = lax.fori_loop(0, i + 1, e_total, (zero, zero))

                def kblock(jj, c):
                    run, e_run, dq = c
                    j = i - jj
                    kj = ks[_blk(j), :]
                    z = _dot_nt(qi, kj) * scale
                    mask = _delta(i, j) > 0
                    a, m, lsig = _sb_terms(z, mask, t_ex, run)
                    sig = jnp.exp(lsig)
                    e = _dot_nt(doi, vs[_blk(j), :]) * a
                    e_before = e_tot - (_split_dot(e, t_in) + e_run)
                    dz = jnp.where(mask, e * (1.0 - sig) - sig * e_before, 0.0)
                    dzb = (dz * scale).astype(BF16)
                    dk_acc[_blk(j), :] += _dot_tn(dzb, qi)
                    dv_acc[_blk(j), :] += _dot_tn(a.astype(BF16), doi)
                    return (run + jnp.sum(m, axis=1, keepdims=True), e_run + jnp.sum(e, axis=1, keepdims=True),
                            dq + jnp.dot(dzb, kj, preferred_element_type=F32))

                _, _, dq = lax.fori_loop(0, i + 1, kblock, (zero, zero, jnp.zeros((BLK, HEAD_DIM), F32)))
                dq_acc[_blk(i), :] = dq
                return carry

            lax.fori_loop(0, nq, qblock, 0)

        if even:
            @pl.when(h < n_sb)
            def _():
                sb_head()

            @pl.when(h >= n_sb)
            def _():
                softmax_head(h - n_sb)

            dh_ref[0] = dq_acc[...].astype(BF16)
            dh_ref[1] = dk_acc[...].astype(BF16)
        else:
            softmax_head(h)
            dh_ref[0] = _rope_t(dq_acc[...], c_ref[...], s_ref[...]).astype(BF16)
            dh_ref[1] = _rope_t(dk_acc[...], c_ref[...], s_ref[...]).astype(BF16)
        dh_ref[2] = dv_acc[...].astype(BF16)

    head = lambda off: pl.BlockSpec((S, HEAD_DIM), lambda h, off=off: (0, off + h))
    full = pl.BlockSpec((S, LANES), lambda h: (0, 0))
    tfull = pl.BlockSpec((n_heads - n_sb, 1, S), lambda h: (0, 0, 0))
    dh_spec = pl.BlockSpec((3, S, HEAD_DIM), lambda h: (0, 0, h))
    dh_shape = jax.ShapeDtypeStruct((3, S, D), BF16)
    if even:
        extra_specs, extra = [full, tfull], (cf, cft)
        out_specs = [dh_spec, tfull, full]
        out_shape = [dh_shape, jax.ShapeDtypeStruct((n_heads - n_sb, 1, S), F32),
                     jax.ShapeDtypeStruct((S, LANES), F32)]
    else:
        extra_specs, extra = [full, full], (rope_c, rope_s)
        out_specs = [dh_spec]
        out_shape = [dh_shape]
    blk_bytes = 10 * _nbytes((S, HEAD_DIM), F32)
    scratch_bytes = 4 * _nbytes((S, HEAD_DIM), BF16) + 3 * _nbytes((S, HEAD_DIM), F32)
    return pl.pallas_call(
        body, name=name, grid=(n_heads,),
        in_specs=[head(0), head(n_heads), head(2 * n_heads), head(0), head(0), head(0)] + extra_specs,
        out_specs=out_specs, out_shape=out_shape,
        scratch_shapes=[pltpu.VMEM((S, HEAD_DIM), BF16)] * 4 + [pltpu.VMEM((S, HEAD_DIM), F32)] * 3,
        compiler_params=pltpu.CompilerParams(dimension_semantics=("arbitrary",),
                                             vmem_limit_bytes=_vmem_limit(blk_bytes, scratch_bytes)),
    )(hq, hq, hq, do, o, lse, *extra)


def _ffn_fwd(tag, x, g, wgu, wd, fs):
    n = _rms_fwd(tag + "_norm", x, g)
    gu = _mm(tag + "_gu", n, wgu, "nn", F32)
    h = _swiglu_fwd(tag + "_act", gu, fs)
    y = _mm(tag + "_down", h, wd, "nn", F32, res=x, alpha=0.5)
    return y, (x, g, n, gu, h)


def _ffn_bwd(tag, dx, dxb, wgu, wd, fs, saved):
    x, g, n, gu, h = saved
    dh = _mm(tag + "_dh", dxb, wd, "nt", F32, alpha=0.5)
    dwd = _mm(tag + "_dwd", h, dxb, "tn", F32, alpha=0.5)
    dgu = _swiglu_bwd(tag + "_dact", dh, gu, fs)
    dwgu = _mm(tag + "_dwgu", n, dgu, "tn", F32)
    dn = _mm(tag + "_dn", dgu, wgu, "nt", F32)
    dx_in, dxb_in, dg = _rms_bwd(tag + "_dnorm", dn, x, g, dx)
    return dx_in, dxb_in, dg, dwgu, dwd


def _mixer_fwd(tag, kind, x, g, wqkv, wout, n_heads, n_sb, wf=None, bf=None, rope=None):
    n = _rms_fwd(tag + "_norm", x, g)
    hq = _mm(tag + "_qkv", n, wqkv, "nn", F32)
    if kind == "even":
        hf = _mm(tag + "_gate", n, wf, "nn", F32)
        cf, cft = _gate_fwd(tag + "_cumgate", hf, bf)
        cft = cft[:n_heads - n_sb].reshape(n_heads - n_sb, 1, -1)
        o, ob, lse = _attn_fwd(tag + "_attn", hq, kind, n_heads, n_sb, cf=cf, cft=cft)
    else:
        hf = cf = cft = None
        o, ob, lse = _attn_fwd(tag + "_attn", hq, kind, n_heads, n_sb, rope_c=rope[0], rope_s=rope[1])
    y = _mm(tag + "_out", ob, wout, "nn", F32, res=x)
    return y, (x, g, n, hq, hf, cf, cft, o, ob, lse)


def _mixer_bwd(tag, kind, dx, dxb, wqkv, wout, n_heads, n_sb, saved, wf=None, bf=None, rope=None):
    x, g, n, hq, hf, cf, cft, o, ob, lse = saved
    do = _mm(tag + "_do", dxb, wout, "nt", F32)
    dwout = _mm(tag + "_dwout", ob, dxb, "tn", F32)
    if kind == "even":
        dh3, dcft, drow = _attn_bwd(tag + "_dattn", hq, do, o, lse, kind, n_heads, n_sb, cf=cf, cft=cft)
    else:
        (dh3,) = _attn_bwd(tag + "_dattn", hq, do, o, lse, kind, n_heads, n_sb, rope_c=rope[0], rope_s=rope[1])
    dwqkv = _mm(tag + "_dwqkv", n, dh3, "tn", F32)
    dn = _mm(tag + "_dn", dh3, wqkv, "nt", F32)
    dwf = db = None
    if kind == "even":
        n_fox = n_heads - n_sb
        dcft = jnp.pad(dcft.reshape(n_fox, -1), ((0, LANES - n_fox), (0, 0)))
        dhf, db = _gate_bwd(tag + "_dcumgate", dcft, drow, hf, bf)
        dwf = _mm(tag + "_dwf", n, dhf, "tn", F32)
        dn = _mm(tag + "_dn_gate", dhf, wf, "nt", F32, res=dn)
    dx_in, dxb_in, dg = _rms_bwd(tag + "_dnorm", dn, x, g, dx)
    return dx_in, dxb_in, dg, dwqkv, dwout, dwf, db


def _local_step(x, target, w, fs, n_heads, n_sb):
    S, D = x.shape
    rope = _rope_tables(S)
    kinds = ("even", "odd")
    saved = []
    h = x
    for l, kind in enumerate(kinds):
        ng = [w["norm_g"][l, i][None, :] for i in range(3)]
        h, s1 = _ffn_fwd(f"l{l}_ffn1", h, ng[0], w["wgu1"][l], w["wd1"][l], fs)
        if kind == "even":
            h, s2 = _mixer_fwd(f"l{l}_mix", kind, h, ng[1], w["wqkv_e"], w["wout_e"], n_heads, n_sb,
                               wf=w["wf"], bf=w["bf"])
        else:
            h, s2 = _mixer_fwd(f"l{l}_mix", kind, h, ng[1], w["wqkv_o"], w["wout_o"], n_heads, n_sb, rope=rope)
        h, s3 = _ffn_fwd(f"l{l}_ffn2", h, ng[2], w["wgu2"][l], w["wd2"][l], fs)
        saved.append((s1, s2, s3))

    dx, dxb, dfinal, loss = _loss_head("loss_head", h, w["final_g"], target)
    grads = {"dfinal": dfinal, "dnorm": [[None] * 3 for _ in kinds],
             "dwgu1": [None, None], "dwd1": [None, None], "dwgu2": [None, None], "dwd2": [None, None]}
    for l in (1, 0):
        kind = kinds[l]
        s1, s2, s3 = saved[l]
        dx, dxb, dg, grads["dwgu2"][l], grads["dwd2"][l] = _ffn_bwd(
            f"l{l}_ffn2", dx, dxb, w["wgu2"][l], w["wd2"][l], fs, s3)
        grads["dnorm"][l][2] = dg
        if kind == "even":
            dx, dxb, dg, grads["dwqkv_e"], grads["dwout_e"], grads["dwf"], grads["db"] = _mixer_bwd(
                f"l{l}_mix", kind, dx, dxb, w["wqkv_e"], w["wout_e"], n_heads, n_sb, s2, wf=w["wf"], bf=w["bf"])
        else:
            dx, dxb, dg, grads["dwqkv_o"], grads["dwout_o"], _, _ = _mixer_bwd(
                f"l{l}_mix", kind, dx, dxb, w["wqkv_o"], w["wout_o"], n_heads, n_sb, s2, rope=rope)
        grads["dnorm"][l][1] = dg
        dx, dxb, dg, grads["dwgu1"][l], grads["dwd1"][l] = _ffn_bwd(
            f"l{l}_ffn1", dx, dxb, w["wgu1"][l], w["wd1"][l], fs, s1)
        grads["dnorm"][l][0] = dg
    return loss, dx, grads


def _cast_into(name, shard, layer, chip, full_shape, place, full=None):
    R, C = shard.shape[-2:]
    tr = _row_tile(R, 512, step=16)
    if layer is None:
        in_spec = pl.BlockSpec((tr, C), lambda i, k: (i, 0))
    else:
        in_spec = pl.BlockSpec((None, tr, C), lambda i, k: (layer, i, 0))
    lead = (None,) * (len(full_shape) - 2)
    out_spec = pl.BlockSpec(lead + (tr, C), lambda i, k: place(i, k[0]))

    def body(*refs):
        k_ref, w_ref = refs[:2]
        o_ref = refs[-1]
        o_ref[...] = w_ref[...].astype(BF16)

    in_specs = [in_spec] + ([_ANY] if full is not None else [])
    args = (chip, shard) + ((full,) if full is not None else ())
    grid_spec = pltpu.PrefetchScalarGridSpec(num_scalar_prefetch=1, grid=(R // tr,), in_specs=in_specs, out_specs=out_spec)
    return pl.pallas_call(
        body, name=name, grid_spec=grid_spec, out_shape=jax.ShapeDtypeStruct(full_shape, BF16),
        input_output_aliases={2: 0} if full is not None else {},
        compiler_params=pltpu.CompilerParams(dimension_semantics=("arbitrary",)),
    )(*args)


def _region_shape(grad, kind):
    if kind == "lead":
        return grad.shape[1] // N_CORES, grad.shape[2]
    rows, cols = grad.shape
    if kind == "cols":
        return rows // N_CORES, cols // N_CHIPS
    return rows // (N_CHIPS * N_CORES), cols


def _region_add(name, grad, kind, landed, core):
    rh, cw = _region_shape(grad, kind)
    tr = _row_tile(rh, 256, step=16)
    nrb = rh // tr
    if kind == "cols":
        g_spec = pl.BlockSpec((tr, cw), lambda k, r, c: (c[0] * nrb + r, k))
    elif kind == "rows":
        g_spec = pl.BlockSpec((tr, cw), lambda k, r, c: ((N_CORES * k + c[0]) * nrb + r, 0))
    else:
        g_spec = pl.BlockSpec((None, tr, cw), lambda k, r, c: (k, c[0] * nrb + r, 0))
    l_spec = pl.BlockSpec((None, tr, cw), lambda k, r, c: (k, r, 0))

    def body(c_ref, g_ref, l_ref, o_ref):
        o_ref[...] = (g_ref[...] + l_ref[...]).astype(BF16)

    grid_spec = pltpu.PrefetchScalarGridSpec(
        num_scalar_prefetch=1, grid=(N_CHIPS, nrb), in_specs=[g_spec, l_spec], out_specs=l_spec)
    return pl.pallas_call(
        body, name=name, grid_spec=grid_spec, out_shape=jax.ShapeDtypeStruct(landed.shape, BF16),
        compiler_params=pltpu.CompilerParams(dimension_semantics=("parallel", "parallel"),
                                             vmem_limit_bytes=_vmem_limit(3 * _nbytes((tr, cw), F32))),
    )(core, grad, landed)


def _chip_sum(name, pair, landed, pos):
    _, rh, cw = pair.shape
    tr = _row_tile(rh, max(16, 2**20 // (cw * 4)), step=16)
    nrb = rh // tr

    def body(p_ref, own_ref, l_ref, o_ref):
        acc = own_ref[...].astype(F32)
        for s in range(N_CHIPS - 1):
            acc = acc + l_ref[s].astype(F32)
        o_ref[...] = acc

    grid_spec = pltpu.PrefetchScalarGridSpec(
        num_scalar_prefetch=1, grid=(nrb,),
        in_specs=[pl.BlockSpec((None, tr, cw), lambda r, p: (p[0], r, 0)),
                  pl.BlockSpec((N_CHIPS - 1, tr, cw), lambda r, p: (0, r, 0))],
        out_specs=pl.BlockSpec((tr, cw), lambda r, p: (p[1] * nrb + r, 0)))
    return pl.pallas_call(
        body, name=name, grid_spec=grid_spec, out_shape=jax.ShapeDtypeStruct((N_CORES * rh, cw), F32),
        compiler_params=pltpu.CompilerParams(dimension_semantics=("arbitrary",)),
    )(pos, pair, landed)


def _sum_leading(name, parts):
    n, R, C = parts.shape
    tr = _row_tile(R, max(8, (2**20 // (C * 4)) // 8 * 8))

    def body(p_ref, o_ref):
        acc = p_ref[0]
        for s in range(1, n):
            acc = acc + p_ref[s]
        o_ref[...] = acc

    return pl.pallas_call(
        body, name=name, grid=(R // tr,),
        in_specs=[pl.BlockSpec((n, tr, C), lambda i: (0, i, 0))],
        out_specs=pl.BlockSpec((tr, C), lambda i: (i, 0)),
        out_shape=jax.ShapeDtypeStruct((R, C), F32),
        compiler_params=pltpu.CompilerParams(dimension_semantics=("parallel",)),
    )(parts)


def _adamw(name, w, g, m, v):
    shape = w.shape
    to2d = lambda t: t.reshape(-1, shape[-1]) if t.ndim > 1 else t.reshape(1, -1)
    w2, g2, m2, v2 = (to2d(t) for t in (w, g, m, v))
    R, C = w2.shape
    tr = _row_tile(R, 256)

    def body(w_ref, g_ref, m_ref, v_ref, d_ref, nm_ref, nv_ref):
        gv = g_ref[...]
        nm = ADAM_B1 * m_ref[...] + (1.0 - ADAM_B1) * gv
        nv = ADAM_B2 * v_ref[...] + (1.0 - ADAM_B2) * (gv * gv)
        m_hat = nm / (1.0 - ADAM_B1 ** ADAM_STEP)
        v_hat = nv / (1.0 - ADAM_B2 ** ADAM_STEP)
        d_ref[...] = -ADAM_LR * (m_hat / (jnp.sqrt(v_hat) + ADAM_EPS) + ADAM_WD * w_ref[...])
        nm_ref[...] = nm
        nv_ref[...] = nv

    spec = pl.BlockSpec((tr, C), lambda i: (i, 0))
    sds = jax.ShapeDtypeStruct((R, C), F32)
    d, nm, nv = pl.pallas_call(
        body, name=name, grid=(R // tr,), in_specs=[spec] * 4, out_specs=[spec] * 3, out_shape=[sds] * 3,
        compiler_params=pltpu.CompilerParams(dimension_semantics=("parallel",),
                                             vmem_limit_bytes=_vmem_limit(7 * _nbytes((tr, C), F32))),
    )(w2, g2, m2, v2)
    return d.reshape(shape), nm.reshape(shape), nv.reshape(shape)


_ANY = pl.BlockSpec(memory_space=pl.ANY)


def _mesh_pos():
    return lax.axis_index("x"), lax.axis_index("y"), lax.axis_index("c")


def _other_chips(x, y):
    return [(1 - x, y), (x, 1 - y), (1 - x, 1 - y)]


def _gather_over_chips(name, fulls, views):
    n = len(views)
    nf = len(fulls)

    def body(*refs):
        full = refs[nf:2 * nf]
        ici_send, ici_recv, d2d_send, d2d_recv = refs[2 * nf:]
        x, y, c = _mesh_pos()
        chips = _other_chips(x, y)
        mine = 2 * x + y
        sibling = (x, y, 1 - c)

        def ici(a, p, k):
            i, view, _ = views[a]
            part = view(full[i], k, c)
            return pltpu.make_async_remote_copy(
                src_ref=part, dst_ref=part, send_sem=ici_send.at[a, p], recv_sem=ici_recv.at[a, p],
                device_id=(*chips[p], c), device_id_type=MESH)

        def d2d(a, p, h):
            i, view, _ = views[a]
            px, py = chips[p]
            part = view(full[i], 2 * px + py, h)
            return pltpu.make_async_remote_copy(
                src_ref=part, dst_ref=part, send_sem=d2d_send.at[a, p], recv_sem=d2d_recv.at[a, p],
                device_id=sibling, device_id_type=MESH)

        sends = [ici(a, p, mine) for a in range(n) for p in range(3)]
        for cp in sends:
            cp.start()
        passed = []
        for a in range(n):
            for p, (px, py) in enumerate(chips):
                ici(a, p, 2 * px + py).wait_recv()
                if views[a][2]:
                    fwd = d2d(a, p, c)
                    fwd.start()
                    passed.append(fwd)
        for a in range(n):
            if views[a][2]:
                for p in range(3):
                    d2d(a, p, 1 - c).wait_recv()
        for cp in sends + passed:
            cp.wait_send()

    return pl.pallas_call(
        body, name=name, in_specs=[_ANY] * nf, out_specs=[_ANY] * nf,
        out_shape=[jax.ShapeDtypeStruct(f.shape, f.dtype) for f in fulls],
        input_output_aliases={i: i for i in range(nf)},
        scratch_shapes=[pltpu.SemaphoreType.DMA((n, 3))] * 4,
        compiler_params=pltpu.CompilerParams(has_side_effects=True),
    )(*fulls)


def _region_view(ref, kind, k, c):
    if kind == "lead":
        rh = ref.shape[1] // N_CORES
        return ref.at[k, pl.ds(pl.multiple_of(c * rh, 8), rh), :]
    rows, cols = ref.shape
    if kind == "cols":
        rh, cw = rows // N_CORES, cols // N_CHIPS
        return ref.at[pl.ds(pl.multiple_of(c * rh, 8), rh), pl.ds(k * cw, cw)]
    rh = rows // (N_CHIPS * N_CORES)
    return ref.at[pl.ds(pl.multiple_of((N_CORES * k + c) * rh, 8), rh), :]


def _send_to_sibling(name, grads, kinds):
    n = len(grads)
    shapes = [jax.ShapeDtypeStruct((N_CHIPS,) + _region_shape(g, kd), F32) for g, kd in zip(grads, kinds)]

    def body(*refs):
        g_ref, land = refs[:n], refs[n:2 * n]
        send_sems, recv_sems = refs[2 * n:]
        x, y, c = _mesh_pos()
        copies = []
        for a in range(n):
            for k in range(N_CHIPS):
                cp = pltpu.make_async_remote_copy(
                    src_ref=_region_view(g_ref[a], kinds[a], k, 1 - c), dst_ref=land[a].at[k],
                    send_sem=send_sems.at[a, k], recv_sem=recv_sems.at[a, k],
                    device_id=(x, y, 1 - c), device_id_type=MESH)
                cp.start()
                copies.append(cp)
        for cp in copies:
            cp.wait_recv()
        for cp in copies:
            cp.wait_send()

    return pl.pallas_call(
        body, name=name, in_specs=[_ANY] * n, out_specs=[_ANY] * n, out_shape=shapes,
        scratch_shapes=[pltpu.SemaphoreType.DMA((n, N_CHIPS)), pltpu.SemaphoreType.DMA((n, N_CHIPS))],
        compiler_params=pltpu.CompilerParams(has_side_effects=True),
    )(*grads)


def _scatter_over_chips(name, pair_sums):
    n = len(pair_sums)

    def body(*refs):
        p_ref, land = refs[:n], refs[n:2 * n]
        send_sems, recv_sems = refs[2 * n:]
        x, y, c = _mesh_pos()
        chips = _other_chips(x, y)
        sends = []
        for a in range(n):
            for p, (px, py) in enumerate(chips):
                cp = pltpu.make_async_remote_copy(
                    src_ref=p_ref[a].at[2 * px + py], dst_ref=land[a].at[p], send_sem=send_sems.at[a, p],
                    recv_sem=recv_sems.at[a, p], device_id=(px, py, c), device_id_type=MESH)
                cp.start()
                sends.append(cp)
        for cp in sends:
            cp.wait_recv()
        for cp in sends:
            cp.wait_send()

    return pl.pallas_call(
        body, name=name, in_specs=[_ANY] * n, out_specs=[_ANY] * n,
        out_shape=[jax.ShapeDtypeStruct((N_CHIPS - 1,) + p.shape[1:], p.dtype) for p in pair_sums],
        scratch_shapes=[pltpu.SemaphoreType.DMA((n, 3)), pltpu.SemaphoreType.DMA((n, 3))],
        compiler_params=pltpu.CompilerParams(has_side_effects=True),
    )(*pair_sums)


def _swap_halves(name, shards):
    n = len(shards)

    def body(*refs):
        out = refs[n:2 * n]
        send_sems, recv_sems = refs[2 * n:]
        x, y, c = _mesh_pos()
        sends = []
        for a in range(n):
            rh = out[a].shape[0] // N_CORES
            mine = out[a].at[pl.ds(pl.multiple_of(c * rh, 8), rh), :]
            cp = pltpu.make_async_remote_copy(
                src_ref=mine, dst_ref=mine, send_sem=send_sems.at[a], recv_sem=recv_sems.at[a],
                device_id=(x, y, 1 - c), device_id_type=MESH)
            cp.start()
            sends.append(cp)
        for a in range(n):
            rh = out[a].shape[0] // N_CORES
            theirs = out[a].at[pl.ds(pl.multiple_of((1 - c) * rh, 8), rh), :]
            pltpu.make_async_remote_copy(
                src_ref=theirs, dst_ref=theirs, send_sem=send_sems.at[a], recv_sem=recv_sems.at[a],
                device_id=(x, y, 1 - c), device_id_type=MESH).wait_recv()
        for cp in sends:
            cp.wait_send()

    return pl.pallas_call(
        body, name=name, in_specs=[_ANY] * n, out_specs=[_ANY] * n,
        out_shape=[jax.ShapeDtypeStruct(s.shape, s.dtype) for s in shards],
        input_output_aliases={i: i for i in range(n)},
        scratch_shapes=[pltpu.SemaphoreType.DMA((n,)), pltpu.SemaphoreType.DMA((n,))],
        compiler_params=pltpu.CompilerParams(has_side_effects=True),
    )(*shards)


def _gather_all_devices(name, block):
    R, C = block.shape
    ndev = N_CHIPS * N_CORES

    def body(b_ref, out_ref, send_sems, recv_sems, local_sem):
        x, y, c = _mesh_pos()
        mine = 4 * x + 2 * y + c
        own = pltpu.make_async_copy(b_ref, out_ref.at[mine], local_sem)
        own.start()
        sends = []
        for mask in range(1, ndev):
            fx, fy, fc = (mask >> 2) & 1, (mask >> 1) & 1, mask & 1
            px, py, pc = x ^ fx, y ^ fy, c ^ fc
            cp = pltpu.make_async_remote_copy(
                src_ref=b_ref, dst_ref=out_ref.at[mine], send_sem=send_sems.at[mask - 1],
                recv_sem=recv_sems.at[mask - 1], device_id=(px, py, pc), device_id_type=MESH)
            cp.start()
            sends.append(cp)
        for mask in range(1, ndev):
            fx, fy, fc = (mask >> 2) & 1, (mask >> 1) & 1, mask & 1
            px, py, pc = x ^ fx, y ^ fy, c ^ fc
            pltpu.make_async_remote_copy(
                src_ref=b_ref, dst_ref=out_ref.at[4 * px + 2 * py + pc], send_sem=send_sems.at[mask - 1],
                recv_sem=recv_sems.at[mask - 1], device_id=(px, py, pc), device_id_type=MESH).wait_recv()
        for cp in sends:
            cp.wait_send()
        own.wait()

    return pl.pallas_call(
        body, name=name, in_specs=[_ANY], out_specs=_ANY,
        out_shape=jax.ShapeDtypeStruct((ndev, R, C), F32),
        scratch_shapes=[pltpu.SemaphoreType.DMA((ndev - 1,)), pltpu.SemaphoreType.DMA((ndev - 1,)),
                        pltpu.SemaphoreType.DMA(())],
        compiler_params=pltpu.CompilerParams(has_side_effects=True),
    )(block)


def _reduce_scatter(grads, kinds, pos):
    landed = _send_to_sibling("rs_pair_send", grads, kinds)
    pair = [_region_add(f"rs_pair_add_{a}", g, kd, ld, pos[1:]) for a, (g, kd, ld) in enumerate(zip(grads, kinds, landed))]
    parts = _scatter_over_chips("rs_chip_send", pair)
    shards = [_chip_sum(f"rs_chip_add_{a}", p, ld, pos) for a, (p, ld) in enumerate(zip(pair, parts))]
    return _swap_halves("rs_swap_halves", shards)


def kernel(x, norm_g, ffn1_w_gate, ffn1_w_up, ffn1_w_down, ffn2_w_gate, ffn2_w_up, ffn2_w_down, even_w_in, even_b_forget, even_w_out, odd_w_qkv, odd_w_out, final_norm_g, loss_target, m_norm_g, m_ffn1_w_gate, m_ffn1_w_up, m_ffn1_w_down, m_ffn2_w_gate, m_ffn2_w_up, m_ffn2_w_down, m_even_w_in, m_even_b_forget, m_even_w_out, m_odd_w_qkv, m_odd_w_out, m_final_norm_g, v_norm_g, v_ffn1_w_gate, v_ffn1_w_up, v_ffn1_w_down, v_ffn2_w_gate, v_ffn2_w_up, v_ffn2_w_down, v_even_w_in, v_even_b_forget, v_even_w_out, v_odd_w_qkv, v_odd_w_out, v_final_norm_g):
    _, S, D = x.shape
    L = norm_g.shape[0]
    assert L == 2 and even_w_in.shape[0] == 1 and odd_w_qkv.shape[0] == 1
    fs = ffn1_w_gate.shape[2]
    F = N_CHIPS * fs
    wc = even_w_in.shape[2]
    n_heads = D // HEAD_DIM
    n_fox = N_CHIPS * wc - 3 * D
    n_sb = n_heads - n_fox
    qs = odd_w_qkv.shape[2]
    os_ = even_w_out.shape[1]
    ns = norm_g.shape[2]
    xi, yi, ci = _mesh_pos()
    chip = 2 * xi + yi

    pos = jnp.stack([chip, ci]).astype(jnp.int32)
    kchip = pos[:1]
    lane = lambda start, size: pl.ds(pl.multiple_of(start, LANES), size)
    sub = lambda start, size: pl.ds(pl.multiple_of(start, 16), size)
    gate_view = lambda r, k, h: r.at[sub(h * (D // 2), D // 2), lane(k * 2 * fs, fs)]
    up_view = lambda r, k, h: r.at[sub(h * (D // 2), D // 2), lane(k * 2 * fs + fs, fs)]
    down_view = lambda r, k, h: r.at[sub(k * fs + h * (fs // 2), fs // 2), :]
    out_view = lambda r, k, h: r.at[sub(k * os_ + h * (os_ // 2), os_ // 2), :]
    tr_d = _row_tile(fs, 512, step=16)
    tr_o = _row_tile(os_, 512, step=16)
    fulls, views = [], []
    for f, (wg, wu, wd) in enumerate(((ffn1_w_gate, ffn1_w_up, ffn1_w_down), (ffn2_w_gate, ffn2_w_up, ffn2_w_down))):
        for l in range(L):
            t = f"cast_ffn{f + 1}_l{l}"
            gu = _cast_into(t + "_gate", wg, l, kchip, (D, 2 * F), lambda i, k: (i, 2 * k))
            gu = _cast_into(t + "_up", wu, l, kchip, (D, 2 * F), lambda i, k: (i, 2 * k + 1), full=gu)
            dn = _cast_into(t + "_down", wd, l, kchip, (F, D), lambda i, k: (k * (fs // tr_d) + i, 0))
            views += [(len(fulls), gate_view, True), (len(fulls), up_view, True), (len(fulls) + 1, down_view, True)]
            fulls += [gu, dn]
    o = len(fulls)
    fulls += [
        _cast_into("cast_win", even_w_in, 0, kchip, (N_CHIPS, D, wc), lambda i, k: (k, i, 0)),
        _cast_into("cast_wout_e", even_w_out, 0, kchip, (D, D), lambda i, k: (k * (os_ // tr_o) + i, 0)),
        _cast_into("cast_wqkv_o", odd_w_qkv, 0, kchip, (D, N_CHIPS * qs), lambda i, k: (i, k)),
        _cast_into("cast_wout_o", odd_w_out, 0, kchip, (D, D), lambda i, k: (k * (os_ // tr_o) + i, 0)),
        lax.dynamic_update_slice(jnp.zeros((L, 3, N_CHIPS * ns), F32), norm_g, (0, 0, chip * ns))]
    views += [
        (o, lambda r, k, h: r.at[k, sub(h * (D // 2), D // 2), :], True), (o + 1, out_view, True),
        (o + 2, lambda r, k, h: r.at[sub(h * (D // 2), D // 2), lane(k * qs, qs)], True), (o + 3, out_view, True),
        (o + 4, lambda r, k, h: r.at[:, :, lane(k * ns, ns)], False)]
    full = _gather_over_chips("gather_weights", fulls, views)
    wgu1, wd1 = [full[2 * l] for l in range(L)], [full[2 * l + 1] for l in range(L)]
    wgu2, wd2 = [full[2 * L + 2 * l] for l in range(L)], [full[2 * L + 2 * l + 1] for l in range(L)]
    win4, wout_e, wqkv_o, wout_o, norm_full = full[4 * L:]
    win = jnp.concatenate([win4[k] for k in range(N_CHIPS)], axis=1)
    wf = jnp.pad(win[:, 3 * D:], ((0, 0), (0, LANES - n_fox)))
    bf = jnp.pad(even_b_forget, ((0, 0), (0, LANES - n_fox)))
    w = {"wgu1": wgu1, "wd1": wd1, "wgu2": wgu2, "wd2": wd2, "wqkv_e": win[:, :3 * D], "wf": wf, "bf": bf,
         "wout_e": wout_e, "wqkv_o": wqkv_o, "wout_o": wout_o, "norm_g": norm_full,
         "final_g": final_norm_g[None, :]}

    loss_vec, grad_x, g = _local_step(x[0], loss_target[0], w, fs, n_heads, n_sb)

    dwin = jnp.concatenate([g["dwqkv_e"], g["dwf"][:, :n_fox]], axis=1)
    dwin4 = jnp.stack([dwin[:, k * wc:(k + 1) * wc] for k in range(N_CHIPS)])
    rs_in = (g["dwgu1"] + g["dwgu2"] + g["dwd1"] + g["dwd2"]
             + [dwin4, g["dwqkv_o"], g["dwout_e"], g["dwout_o"]])
    kinds = ["cols"] * (2 * L) + ["rows"] * (2 * L) + ["lead", "cols", "rows", "rows"]
    red = _reduce_scatter(rs_in, kinds, pos)
    gu1, gu2, gd1, gd2 = red[0:L], red[L:2 * L], red[2 * L:3 * L], red[3 * L:4 * L]
    g_win, g_qkv_o, g_wout_e, g_wout_o = red[4 * L:]

    small_rows = [g["dnorm"][l][i] for l in range(L) for i in range(3)] + [
        g["dfinal"], jnp.pad(g["db"], ((0, 0), (0, D - LANES))), jnp.pad(loss_vec, ((0, 0), (0, D - LANES)))]
    small = jnp.concatenate(small_rows + [jnp.zeros((16 - len(small_rows), D), F32)], axis=0)
    small_sum = _sum_leading("small_sum", _gather_all_devices("small_gather", small))
    loss = small_sum[3 * L + 2, 0]
    g_norm = lax.dynamic_slice_in_dim(small_sum[:3 * L].reshape(L, 3, D), chip * ns, ns, axis=2)
    g_final = small_sum[3 * L]
    g_bf = small_sum[3 * L + 1, :n_fox][None, :]

    grads = [
        g_norm,
        jnp.stack([t[:, :fs] for t in gu1]), jnp.stack([t[:, fs:] for t in gu1]), jnp.stack(gd1),
        jnp.stack([t[:, :fs] for t in gu2]), jnp.stack([t[:, fs:] for t in gu2]), jnp.stack(gd2),
        g_win[None], g_bf, g_wout_e[None], g_qkv_o[None], g_wout_o[None], g_final]
    weights = [norm_g, ffn1_w_gate, ffn1_w_up, ffn1_w_down, ffn2_w_gate, ffn2_w_up, ffn2_w_down,
               even_w_in, even_b_forget, even_w_out, odd_w_qkv, odd_w_out, final_norm_g]
    ms = [m_norm_g, m_ffn1_w_gate, m_ffn1_w_up, m_ffn1_w_down, m_ffn2_w_gate, m_ffn2_w_up, m_ffn2_w_down,
          m_even_w_in, m_even_b_forget, m_even_w_out, m_odd_w_qkv, m_odd_w_out, m_final_norm_g]
    vs = [v_norm_g, v_ffn1_w_gate, v_ffn1_w_up, v_ffn1_w_down, v_ffn2_w_gate, v_ffn2_w_up, v_ffn2_w_down,
          v_even_w_in, v_even_b_forget, v_even_w_out, v_odd_w_qkv, v_odd_w_out, v_final_norm_g]
    deltas, new_ms, new_vs = [], [], []
    for i, (wt, gt, mt, vt) in enumerate(zip(weights, grads, ms, vs)):
        d, nm, nv = _adamw(f"adamw_{i}", wt, gt, mt, vt)
        deltas.append(d)
        new_ms.append(nm)
        new_vs.append(nv)
    return (loss, grad_x[None], *grads, *deltas, *new_ms, *new_vs)
```

```python
import functools
import math

import jax
import jax.numpy as jnp
from jax import lax
from jax.experimental import pallas as pl
from jax.experimental.pallas import tpu as pltpu

F32 = jnp.float32
BF16 = jnp.bfloat16

HEAD_DIM = 128
ROPE_DIMS = 32
ROPE_THETA = 500000.0
DILATED_PATTERNS = ((128, 1), (512, 4), (2048, 16))
RMS_EPS = 1e-6
NEG_INF = -1e30
ADAM_LR = 0.001
ADAM_B1 = 0.9
ADAM_B2 = 0.999
ADAM_EPS = 1e-08
ADAM_WD = 0.01
ADAM_STEP = 10

N_CHIPS = 4
N_CORES = 2
LANES = 128
BLK = 128
VMEM_BYTES_V7X = 64 * 2**20
MESH = pl.DeviceIdType.MESH


def _vmem_limit(block_bytes, scratch_bytes=0):
    need = 2 * block_bytes + scratch_bytes + 12 * 2**20
    return int(min(need, VMEM_BYTES_V7X - 6 * 2**20))


def _nbytes(shape, dtype):
    return math.prod(shape) * jnp.dtype(dtype).itemsize


def _tile(dim, target):
    best = None
    for t in range(LANES, min(dim, target) + 1, LANES):
        if dim % t == 0:
            best = t
    assert best is not None, (dim, target)
    return best


def _row_tile(rows, target, step=8):
    if rows <= target:
        return rows
    best = None
    for t in range(step, target + 1, step):
        if rows % t == 0:
            best = t
    assert best is not None, (rows, target)
    return best


def _mm(name, a, b, mode, out_dtype, res=None, alpha=1.0, tm_target=512, tn_target=1536, tk_target=2048):
    a3 = a.ndim == 3
    b3 = b.ndim == 3
    if mode == "nn":
        assert not a3 and not b3
        (M, K), (K2, N) = a.shape, b.shape
    elif mode == "nt":
        assert not b3
        if a3:
            P, M, Kp = a.shape
            K = P * Kp
        else:
            M, K = a.shape
        N, K2 = b.shape
    else:
        assert mode == "tn" and not a3
        K, M = a.shape
        if b3:
            P, K2, Np = b.shape
            N = P * Np
        else:
            K2, N = b.shape
    assert K == K2, (name, a.shape, b.shape)
    tm = _tile(M, tm_target)
    tn = _tile(Np if b3 else N, tn_target)
    tk = _tile(Kp if a3 else K, tk_target)
    nk = K // tk
    grid = (M // tm, N // tn, nk)

    if mode == "nn":
        a_spec = pl.BlockSpec((tm, tk), lambda i, j, k: (i, k))
        b_spec = pl.BlockSpec((tk, tn), lambda i, j, k: (k, j))
        dims = (((1,), (0,)), ((), ()))
    elif mode == "nt":
        if a3:
            nkp = Kp // tk
            a_spec = pl.BlockSpec((None, tm, tk), lambda i, j, k: (k // nkp, i, k % nkp))
        else:
            a_spec = pl.BlockSpec((tm, tk), lambda i, j, k: (i, k))
        b_spec = pl.BlockSpec((tn, tk), lambda i, j, k: (j, k))
        dims = (((1,), (1,)), ((), ()))
    else:
        a_spec = pl.BlockSpec((tk, tm), lambda i, j, k: (k, i))
        if b3:
            njp = Np // tn
            b_spec = pl.BlockSpec((None, tk, tn), lambda i, j, k: (j // njp, k, j % njp))
        else:
            b_spec = pl.BlockSpec((tk, tn), lambda i, j, k: (k, j))
        dims = (((0,), (0,)), ((), ()))
    o_spec = pl.BlockSpec((tm, tn), lambda i, j, k: (i, j))
    has_res = res is not None

    def finish(y, r_ref, o_ref):
        if alpha != 1.0:
            y = y * alpha
        if has_res:
            y = r_ref[...] + y
        o_ref[...] = y.astype(o_ref.dtype)

    def body(*refs):
        a_ref, b_ref = refs[:2]
        r_ref = refs[2] if has_res else None
        o_ref = refs[3] if has_res else refs[2]
        part = lax.dot_general(a_ref[...], b_ref[...], dims, preferred_element_type=F32)
        if nk == 1:
            finish(part, r_ref, o_ref)
            return
        acc_ref = refs[-1]
        k = pl.program_id(2)

        @pl.when(k == 0)
        def _():
            acc_ref[...] = part

        @pl.when(k > 0)
        def _():
            acc_ref[...] += part

        @pl.when(k == nk - 1)
        def _():
            finish(acc_ref[...], r_ref, o_ref)

    in_specs = [a_spec, b_spec] + ([o_spec] if has_res else [])
    args = (a, b) + ((res,) if has_res else ())
    blk = (_nbytes((tm, tk), a.dtype) + _nbytes((tk, tn), b.dtype) + _nbytes((tm, tn), out_dtype)
           + (_nbytes((tm, tn), F32) if has_res else 0))
    return pl.pallas_call(
        body, name=name, grid=grid, in_specs=in_specs, out_specs=o_spec,
        out_shape=jax.ShapeDtypeStruct((M, N), out_dtype),
        scratch_shapes=[pltpu.VMEM((tm, tn), F32)] if nk > 1 else [],
        compiler_params=pltpu.CompilerParams(
            dimension_semantics=("parallel", "parallel", "arbitrary"),
            vmem_limit_bytes=_vmem_limit(blk, 2 * _nbytes((tm, tn), F32))),
    )(*args)


def _rms_fwd(name, x, g):
    S, D = x.shape
    tr = _row_tile(S, 256)

    def body(x_ref, g_ref, n_ref):
        xv = x_ref[...]
        r = lax.rsqrt(jnp.mean(xv * xv, axis=-1, keepdims=True) + RMS_EPS)
        n_ref[...] = (xv * r * g_ref[...]).astype(BF16)

    return pl.pallas_call(
        body, name=name, grid=(S // tr,),
        in_specs=[pl.BlockSpec((tr, D), lambda i: (i, 0)), pl.BlockSpec((1, D), lambda i: (0, 0))],
        out_specs=pl.BlockSpec((tr, D), lambda i: (i, 0)),
        out_shape=jax.ShapeDtypeStruct((S, D), BF16),
        compiler_params=pltpu.CompilerParams(dimension_semantics=("parallel",)),
    )(x, g)


def _rms_bwd(name, dn, x, g, dres):
    S, D = x.shape
    tr = _row_tile(S, 256)

    def body(dn_ref, x_ref, g_ref, dres_ref, dx_ref, dxb_ref, dg_ref):
        i = pl.program_id(0)
        xv = x_ref[...]
        dnv = dn_ref[...]
        r = lax.rsqrt(jnp.mean(xv * xv, axis=-1, keepdims=True) + RMS_EPS)
        u = dnv * g_ref[...]
        dot = jnp.mean(u * xv, axis=-1, keepdims=True)
        dx = dres_ref[...] + r * u - xv * (r * r * r * dot)
        dx_ref[...] = dx
        dxb_ref[...] = dx.astype(BF16)

        @pl.when(i == 0)
        def _():
            dg_ref[...] = jnp.zeros_like(dg_ref)

        dg_ref[...] += jnp.sum(dnv * xv * r, axis=0, keepdims=True)

    row = pl.BlockSpec((tr, D), lambda i: (i, 0))
    vec = pl.BlockSpec((1, D), lambda i: (0, 0))
    return pl.pallas_call(
        body, name=name, grid=(S // tr,),
        in_specs=[row, row, vec, row], out_specs=[row, row, vec],
        out_shape=[jax.ShapeDtypeStruct((S, D), F32), jax.ShapeDtypeStruct((S, D), BF16),
                   jax.ShapeDtypeStruct((1, D), F32)],
        compiler_params=pltpu.CompilerParams(dimension_semantics=("arbitrary",)),
    )(dn, x, g, dres)


def _loss_head(name, x, g, target):
    S, D = x.shape
    tr = _row_tile(S, 256)

    def body(x_ref, g_ref, t_ref, dx_ref, dxb_ref, dg_ref, loss_ref):
        i = pl.program_id(0)
        xv = x_ref[...]
        gv = g_ref[...]
        r = lax.rsqrt(jnp.mean(xv * xv, axis=-1, keepdims=True) + RMS_EPS)
        diff = xv * r * gv - t_ref[...]
        part = 0.5 * jnp.sum(jnp.mean(diff * diff, axis=-1, keepdims=True), axis=0, keepdims=True)
        dy = diff * (1.0 / D)
        u = dy * gv
        dot = jnp.mean(u * xv, axis=-1, keepdims=True)
        dx = r * u - xv * (r * r * r * dot)
        dx_ref[...] = dx
        dxb_ref[...] = dx.astype(BF16)

        @pl.when(i == 0)
        def _():
            dg_ref[...] = jnp.zeros_like(dg_ref)
            loss_ref[...] = jnp.zeros_like(loss_ref)

        dg_ref[...] += jnp.sum(dy * xv * r, axis=0, keepdims=True)
        loss_ref[...] += jnp.broadcast_to(part, loss_ref.shape)

    row = pl.BlockSpec((tr, D), lambda i: (i, 0))
    vec = pl.BlockSpec((1, D), lambda i: (0, 0))
    lvec = pl.BlockSpec((1, LANES), lambda i: (0, 0))
    return pl.pallas_call(
        body, name=name, grid=(S // tr,),
        in_specs=[row, vec, row], out_specs=[row, row, vec, lvec],
        out_shape=[jax.ShapeDtypeStruct((S, D), F32), jax.ShapeDtypeStruct((S, D), BF16),
                   jax.ShapeDtypeStruct((1, D), F32), jax.ShapeDtypeStruct((1, LANES), F32)],
        compiler_params=pltpu.CompilerParams(dimension_semantics=("arbitrary",)),
    )(x, g, target)


def _swiglu_fwd(name, gu, fs):
    S, two_f = gu.shape
    nslab = two_f // (2 * fs)
    tr = _row_tile(S, 256)

    def body(gu_ref, h_ref):
        gv = gu_ref[:, :fs]
        uv = gu_ref[:, fs:]
        h_ref[...] = (gv * jax.nn.sigmoid(gv) * uv).astype(BF16)

    return pl.pallas_call(
        body, name=name, grid=(S // tr, nslab),
        in_specs=[pl.BlockSpec((tr, 2 * fs), lambda i, k: (i, k))],
        out_specs=pl.BlockSpec((tr, fs), lambda i, k: (i, k)),
        out_shape=jax.ShapeDtypeStruct((S, nslab * fs), BF16),
        compiler_params=pltpu.CompilerParams(dimension_semantics=("parallel", "parallel")),
    )(gu)


def _swiglu_bwd(name, dh, gu, fs):
    S, two_f = gu.shape
    nslab = two_f // (2 * fs)
    tr = _row_tile(S, 256)

    def body(dh_ref, gu_ref, o_ref):
        gv = gu_ref[:, :fs]
        uv = gu_ref[:, fs:]
        dhv = dh_ref[...]
        sg = jax.nn.sigmoid(gv)
        silu = gv * sg
        o_ref[:, :fs] = (dhv * uv * (sg + silu * (1.0 - sg))).astype(BF16)
        o_ref[:, fs:] = (dhv * silu).astype(BF16)

    return pl.pallas_call(
        body, name=name, grid=(S // tr, nslab),
        in_specs=[pl.BlockSpec((tr, fs), lambda i, k: (i, k)), pl.BlockSpec((tr, 2 * fs), lambda i, k: (i, k))],
        out_specs=pl.BlockSpec((tr, 2 * fs), lambda i, k: (i, k)),
        out_shape=jax.ShapeDtypeStruct((S, two_f), BF16),
        compiler_params=pltpu.CompilerParams(dimension_semantics=("parallel", "parallel")),
    )(dh, gu)


def _tri_rows(r0, nrows, ncols, lower):
    row = lax.broadcasted_iota(jnp.int32, (nrows, ncols), 0) + r0
    col = lax.broadcasted_iota(jnp.int32, (nrows, ncols), 1)
    return jnp.where((col <= row) if lower else (col >= row), 1.0, 0.0).astype(F32)


def _gate_fwd(name, hf, b):
    S = hf.shape[0]
    tb = _row_tile(S, 256)

    def body(hf_ref, b_ref, cf_ref, cft_ref, lf_ref):
        zz = hf_ref[...] + b_ref[...]
        lf_ref[...] = jnp.minimum(zz, 0.0) - jnp.log1p(jnp.exp(-jnp.abs(zz)))

        def blk(i, c):
            r0 = pl.multiple_of(i * tb, tb)
            tri = _tri_rows(r0, tb, S, True)
            cf_ref[pl.ds(r0, tb), :] = jnp.dot(tri, lf_ref[...], precision=lax.Precision.HIGHEST,
                                               preferred_element_type=F32)
            return c

        lax.fori_loop(0, S // tb, blk, 0)
        cft_ref[...] = cf_ref[...].T

    full = pl.BlockSpec((S, LANES), lambda: (0, 0))
    return pl.pallas_call(
        body, name=name, in_specs=[full, pl.BlockSpec((1, LANES), lambda: (0, 0))],
        out_specs=[full, pl.BlockSpec((LANES, S), lambda: (0, 0))],
        out_shape=[jax.ShapeDtypeStruct((S, LANES), F32), jax.ShapeDtypeStruct((LANES, S), F32)],
        scratch_shapes=[pltpu.VMEM((S, LANES), F32)],
    )(hf, b)


def _gate_bwd(name, dcft, drow, hf, b):
    S = hf.shape[0]
    tb = _row_tile(S, 256)

    def body(dcft_ref, drow_ref, hf_ref, b_ref, dhf_ref, db_ref, dcf_ref, dlf_ref):
        dcf_ref[...] = dcft_ref[...].T + drow_ref[...]

        def blk(i, c):
            r0 = pl.multiple_of(i * tb, tb)
            tri = _tri_rows(r0, tb, S, False)
            dlf_ref[pl.ds(r0, tb), :] = jnp.dot(tri, dcf_ref[...], precision=lax.Precision.HIGHEST,
                                                preferred_element_type=F32)
            return c

        lax.fori_loop(0, S // tb, blk, 0)
        zz = hf_ref[...] + b_ref[...]
        dhf = dlf_ref[...] * jax.nn.sigmoid(-zz)
        dhf_ref[...] = dhf.astype(BF16)
        db_ref[...] = jnp.sum(dhf, axis=0, keepdims=True)

    full = pl.BlockSpec((S, LANES), lambda: (0, 0))
    vec = pl.BlockSpec((1, LANES), lambda: (0, 0))
    return pl.pallas_call(
        body, name=name, in_specs=[pl.BlockSpec((LANES, S), lambda: (0, 0)), full, full, vec],
        out_specs=[full, vec],
        out_shape=[jax.ShapeDtypeStruct((S, LANES), BF16), jax.ShapeDtypeStruct((1, LANES), F32)],
        scratch_shapes=[pltpu.VMEM((S, LANES), F32), pltpu.VMEM((S, LANES), F32)],
    )(dcft, drow, hf, b)


def _rope_tables(S):
    half = ROPE_DIMS // 2
    freqs = ROPE_THETA ** (-jnp.arange(half, dtype=F32) / half)
    ang = jnp.arange(S, dtype=F32)[:, None] * freqs[None, :]
    cos, sin = jnp.cos(ang), jnp.sin(ang)
    pad = HEAD_DIM - ROPE_DIMS
    c = jnp.concatenate([cos, cos, jnp.ones((S, pad), F32)], axis=1)
    s = jnp.concatenate([-sin, sin, jnp.zeros((S, pad), F32)], axis=1)
    return c, s


def _rope_swap(x):
    half = ROPE_DIMS // 2
    lane = lax.broadcasted_iota(jnp.int32, x.shape, 1)
    upper = jnp.where(lane < ROPE_DIMS, pltpu.roll(x, half, 1), 0.0)
    return jnp.where(lane < half, pltpu.roll(x, HEAD_DIM - half, 1), upper)


def _rope(x, c, s):
    return x * c + _rope_swap(x) * s


def _rope_t(dy, c, s):
    return dy * c + _rope_swap(dy * s)


def _split_dot(x, t):
    hi = x.astype(BF16)
    lo = (x - hi.astype(F32)).astype(BF16)
    return (jnp.dot(hi, t, preferred_element_type=F32) + jnp.dot(lo, t, preferred_element_type=F32))


_NT = (((1,), (1,)), ((), ()))
_TN = (((0,), (0,)), ((), ()))


def _dot_nt(a, b):
    return lax.dot_general(a, b, _NT, preferred_element_type=F32)


def _dot_tn(a, b):
    return lax.dot_general(a, b, _TN, preferred_element_type=F32)


def _blk(i):
    return pl.ds(pl.multiple_of(i * BLK, BLK), BLK)


def _delta(i, j):
    row = lax.broadcasted_iota(jnp.int32, (BLK, BLK), 0)
    col = lax.broadcasted_iota(jnp.int32, (BLK, BLK), 1)
    return (row - col) + (i - j) * BLK


def _dilated_mult(delta):
    c = jnp.zeros(delta.shape, F32)
    for window, dil in DILATED_PATTERNS:
        ok = (delta >= 0) & (delta <= window) & ((delta & (dil - 1)) == 0)
        c = c + jnp.where(ok, 1.0, 0.0)
    return c


def _sb_terms(z, mask, t_ex, run):
    t = jnp.log1p(jnp.exp(-jnp.abs(z)))
    lsig = jnp.minimum(z, 0.0) - t
    m = jnp.where(mask, -(jnp.maximum(z, 0.0) + t), 0.0)
    after = _split_dot(m, t_ex)
    a = jnp.where(mask, jnp.exp(lsig + after + run), 0.0)
    return a, m, lsig


def _attn_fwd(name, hq, layer_kind, n_heads, n_sb, cf=None, cft=None, rope_c=None, rope_s=None):
    S = hq.shape[0]
    D = n_heads * HEAD_DIM
    nq = S // BLK
    scale = HEAD_DIM ** -0.5
    even = layer_kind == "even"

    def body(*refs):
        if even:
            q_ref, k_ref, v_ref, cf_ref, cft_ref, o_ref, ob_ref, lse_ref, qs, ks, vs = refs
        else:
            q_ref, k_ref, v_ref, c_ref, s_ref, o_ref, ob_ref, lse_ref, qs, ks, vs = refs
        h = pl.program_id(0)
        if even:
            qs[...] = q_ref[...].astype(BF16)
            ks[...] = k_ref[...].astype(BF16)
        else:
            qs[...] = _rope(q_ref[...], c_ref[...], s_ref[...]).astype(BF16)
            ks[...] = _rope(k_ref[...], c_ref[...], s_ref[...]).astype(BF16)
        vs[...] = v_ref[...].astype(BF16)

        def softmax_head(hh):
            def qblock(i, carry):
                qi = qs[_blk(i), :]
                if even:
                    lane = lax.broadcasted_iota(jnp.int32, (BLK, LANES), 1)
                    cfq = jnp.sum(jnp.where(lane == hh, cf_ref[_blk(i), :], 0.0), axis=1, keepdims=True)

                def kblock(j, c):
                    m_run, l_run, acc = c
                    z = _dot_nt(qi, ks[_blk(j), :]) * scale
                    delta = _delta(i, j)
                    if even:
                        z = z + cfq - cft_ref[hh, :, _blk(j)]
                        ok = delta >= 0
                    else:
                        mult = _dilated_mult(delta)
                        ok = mult > 0.0
                    z = jnp.where(ok, z, NEG_INF)
                    m_new = jnp.maximum(m_run, jnp.max(z, axis=1, keepdims=True))
                    p = jnp.exp(z - m_new)
                    if not even:
                        p = p * mult
                    alpha = jnp.exp(m_run - m_new)
                    l_new = alpha * l_run + jnp.sum(p, axis=1, keepdims=True)
                    acc = alpha * acc + jnp.dot(p.astype(BF16), vs[_blk(j), :], preferred_element_type=F32)
                    return m_new, l_new, acc

                init = (jnp.full((BLK, 1), NEG_INF, F32), jnp.zeros((BLK, 1), F32), jnp.zeros((BLK, HEAD_DIM), F32))
                m_run, l_run, acc = lax.fori_loop(0, i + 1, kblock, init)
                o = acc / l_run
                o_ref[_blk(i), :] = o
                ob_ref[_blk(i), :] = o.astype(BF16)
                lse_ref[_blk(i), :] = jnp.broadcast_to(m_run + jnp.log(l_run), (BLK, HEAD_DIM))
                return carry

            lax.fori_loop(0, nq, qblock, 0)

        def sb_head():
            row = lax.broadcasted_iota(jnp.int32, (BLK, BLK), 0)
            col = lax.broadcasted_iota(jnp.int32, (BLK, BLK), 1)
            t_ex = jnp.where(row > col, 1.0, 0.0).astype(BF16)

            def qblock(i, carry):
                qi = qs[_blk(i), :]

                def kblock(jj, c):
                    run, acc = c
                    j = i - jj
                    z = _dot_nt(qi, ks[_blk(j), :]) * scale
                    a, m, _ = _sb_terms(z, _delta(i, j) > 0, t_ex, run)
                    acc = acc + jnp.dot(a.astype(BF16), vs[_blk(j), :], preferred_element_type=F32)
                    return run + jnp.sum(m, axis=1, keepdims=True), acc

                init = (jnp.zeros((BLK, 1), F32), jnp.zeros((BLK, HEAD_DIM), F32))
                _, acc = lax.fori_loop(0, i + 1, kblock, init)
                o_ref[_blk(i), :] = acc
                ob_ref[_blk(i), :] = acc.astype(BF16)
                lse_ref[_blk(i), :] = jnp.zeros((BLK, HEAD_DIM), F32)
                return carry

            lax.fori_loop(0, nq, qblock, 0)

        if even:
            @pl.when(h < n_sb)
            def _():
                sb_head()

            @pl.when(h >= n_sb)
            def _():
                softmax_head(h - n_sb)
        else:
            softmax_head(h)

    head = lambda off: pl.BlockSpec((S, HEAD_DIM), lambda h, off=off: (0, off + h))
    full = pl.BlockSpec((S, LANES), lambda h: (0, 0))
    if even:
        extra_specs = [full, pl.BlockSpec(cft.shape, lambda h: (0, 0, 0))]
        extra = (cf, cft)
    else:
        extra_specs = [full, full]
        extra = (rope_c, rope_s)
    blk_bytes = 8 * _nbytes((S, HEAD_DIM), F32)
    return pl.pallas_call(
        body, name=name, grid=(n_heads,),
        in_specs=[head(0), head(n_heads), head(2 * n_heads)] + extra_specs,
        out_specs=[head(0), head(0), head(0)],
        out_shape=[jax.ShapeDtypeStruct((S, D), F32), jax.ShapeDtypeStruct((S, D), BF16),
                   jax.ShapeDtypeStruct((S, D), F32)],
        scratch_shapes=[pltpu.VMEM((S, HEAD_DIM), BF16)] * 3,
        compiler_params=pltpu.CompilerParams(dimension_semantics=("arbitrary",),
                                             vmem_limit_bytes=_vmem_limit(blk_bytes, 3 * _nbytes((S, HEAD_DIM), BF16))),
    )(hq, hq, hq, *extra)


def _attn_bwd(name, hq, do, o, lse, layer_kind, n_heads, n_sb, cf=None, cft=None, rope_c=None, rope_s=None):
    S = hq.shape[0]
    D = n_heads * HEAD_DIM
    nq = S // BLK
    scale = HEAD_DIM ** -0.5
    even = layer_kind == "even"

    def body(*refs):
        if even:
            (q_ref, k_ref, v_ref, do_ref, o_ref, lse_ref, cf_ref, cft_ref,
             dh_ref, dcft_ref, drow_ref, qs, ks, vs, dos, dq_acc, dk_acc, dv_acc) = refs
        else:
            (q_ref, k_ref, v_ref, do_ref, o_ref, lse_ref, c_ref, s_ref,
             dh_ref, qs, ks, vs, dos, dq_acc, dk_acc, dv_acc) = refs
        h = pl.program_id(0)
        if even:
            qs[...] = q_ref[...].astype(BF16)
            ks[...] = k_ref[...].astype(BF16)

            @pl.when(h == 0)
            def _():
                dcft_ref[...] = jnp.zeros_like(dcft_ref)
                drow_ref[...] = jnp.zeros_like(drow_ref)
        else:
            qs[...] = _rope(q_ref[...], c_ref[...], s_ref[...]).astype(BF16)
            ks[...] = _rope(k_ref[...], c_ref[...], s_ref[...]).astype(BF16)
        vs[...] = v_ref[...].astype(BF16)
        dos[...] = do_ref[...].astype(BF16)
        dk_acc[...] = jnp.zeros_like(dk_acc)
        dv_acc[...] = jnp.zeros_like(dv_acc)

        def softmax_head(hh):
            def qblock(i, carry):
                qi = qs[_blk(i), :]
                doi = dos[_blk(i), :]
                dvec = jnp.sum(do_ref[_blk(i), :] * o_ref[_blk(i), :], axis=1, keepdims=True)
                lse_i = jnp.max(lse_ref[_blk(i), :], axis=1, keepdims=True)
                if even:
                    lane = lax.broadcasted_iota(jnp.int32, (BLK, LANES), 1)
                    cfq = jnp.sum(jnp.where(lane == hh, cf_ref[_blk(i), :], 0.0), axis=1, keepdims=True)

                def kblock(j, c):
                    dq, ds_rows = c
                    kj = ks[_blk(j), :]
                    z = _dot_nt(qi, kj) * scale
                    delta = _delta(i, j)
                    if even:
                        z = z + cfq - cft_ref[hh, :, _blk(j)]
                        ok = delta >= 0
                    else:
                        mult = _dilated_mult(delta)
                        ok = mult > 0.0
                    p = jnp.exp(jnp.where(ok, z, NEG_INF) - lse_i)
                    if not even:
                        p = p * mult
                    dp = _dot_nt(doi, vs[_blk(j), :])
                    ds = p * (dp - dvec)
                    dsb = (ds * scale).astype(BF16)
                    dk_acc[_blk(j), :] += _dot_tn(dsb, qi)
                    dv_acc[_blk(j), :] += _dot_tn(p.astype(BF16), doi)
                    if even:
                        dcft_ref[hh, :, _blk(j)] += -jnp.sum(ds, axis=0, keepdims=True)
                    return (dq + jnp.dot(dsb, kj, preferred_element_type=F32),
                            ds_rows + jnp.sum(ds, axis=1, keepdims=True))

                dq, ds_rows = lax.fori_loop(0, i + 1, kblock,
                                            (jnp.zeros((BLK, HEAD_DIM), F32), jnp.zeros((BLK, 1), F32)))
                dq_acc[_blk(i), :] = dq
                if even:
                    drow_ref[_blk(i), :] += jnp.where(lane == hh, ds_rows, 0.0)
                return carry

            lax.fori_loop(0, nq, qblock, 0)

        def sb_head():
            row = lax.broadcasted_iota(jnp.int32, (BLK, BLK), 0)
            col = lax.broadcasted_iota(jnp.int32, (BLK, BLK), 1)
            t_ex = jnp.where(row > col, 1.0, 0.0).astype(BF16)
            t_in = jnp.where(row >= col, 1.0, 0.0).astype(BF16)

            def qblock(i, carry):
                qi = qs[_blk(i), :]
                doi = dos[_blk(i), :]

                def e_total(jj, c):
                    run, tot = c
                    j = i - jj
                    z = _dot_nt(qi, ks[_blk(j), :]) * scale
                    a, m, _ = _sb_terms(z, _delta(i, j) > 0, t_ex, run)
                    e = _dot_nt(doi, vs[_blk(j), :]) * a
                    return run + jnp.sum(m, axis=1, keepdims=True), tot + jnp.sum(e, axis=1, keepdims=True)

                zero = jnp.zeros((BLK, 1), F32)
                _, e_tot = lax.fori_loop(0, i + 1, e_total, (zero, zero))

                def kblock(jj, c):
                    run, e_run, dq = c
                    j = i - jj
                    kj = ks[_blk(j), :]
                    z = _dot_nt(qi, kj) * scale
                    mask = _delta(i, j) > 0
                    a, m, lsig = _sb_terms(z, mask, t_ex, run)
                    sig = jnp.exp(lsig)
                    e = _dot_nt(doi, vs[_blk(j), :]) * a
                    e_before = e_tot - (_split_dot(e, t_in) + e_run)
                    dz = jnp.where(mask, e * (1.0 - sig) - sig * e_before, 0.0)
                    dzb = (dz * scale).astype(BF16)
                    dk_acc[_blk(j), :] += _dot_tn(dzb, qi)
                    dv_acc[_blk(j), :] += _dot_tn(a.astype(BF16), doi)
                    return (run + jnp.sum(m, axis=1, keepdims=True), e_run + jnp.sum(e, axis=1, keepdims=True),
                            dq + jnp.dot(dzb, kj, preferred_element_type=F32))

                _, _, dq = lax.fori_loop(0, i + 1, kblock, (zero, zero, jnp.zeros((BLK, HEAD_DIM), F32)))
                dq_acc[_blk(i), :] = dq
                return carry

            lax.fori_loop(0, nq, qblock, 0)

        if even:
            @pl.when(h < n_sb)
            def _():
                sb_head()

            @pl.when(h >= n_sb)
            def _():
                softmax_head(h - n_sb)

            dh_ref[0] = dq_acc[...].astype(BF16)
            dh_ref[1] = dk_acc[...].astype(BF16)
        else:
            softmax_head(h)
            dh_ref[0] = _rope_t(dq_acc[...], c_ref[...], s_ref[...]).astype(BF16)
            dh_ref[1] = _rope_t(dk_acc[...], c_ref[...], s_ref[...]).astype(BF16)
        dh_ref[2] = dv_acc[...].astype(BF16)

    head = lambda off: pl.BlockSpec((S, HEAD_DIM), lambda h, off=off: (0, off + h))
    full = pl.BlockSpec((S, LANES), lambda h: (0, 0))
    tfull = pl.BlockSpec((n_heads - n_sb, 1, S), lambda h: (0, 0, 0))
    dh_spec = pl.BlockSpec((3, S, HEAD_DIM), lambda h: (0, 0, h))
    dh_shape = jax.ShapeDtypeStruct((3, S, D), BF16)
    if even:
        extra_specs, extra = [full, tfull], (cf, cft)
        out_specs = [dh_spec, tfull, full]
        out_shape = [dh_shape, jax.ShapeDtypeStruct((n_heads - n_sb, 1, S), F32),
                     jax.ShapeDtypeStruct((S, LANES), F32)]
    else:
        extra_specs, extra = [full, full], (rope_c, rope_s)
        out_specs = [dh_spec]
        out_shape = [dh_shape]
    blk_bytes = 10 * _nbytes((S, HEAD_DIM), F32)
    scratch_bytes = 4 * _nbytes((S, HEAD_DIM), BF16) + 3 * _nbytes((S, HEAD_DIM), F32)
    return pl.pallas_call(
        body, name=name, grid=(n_heads,),
        in_specs=[head(0), head(n_heads), head(2 * n_heads), head(0), head(0), head(0)] + extra_specs,
        out_specs=out_specs, out_shape=out_shape,
        scratch_shapes=[pltpu.VMEM((S, HEAD_DIM), BF16)] * 4 + [pltpu.VMEM((S, HEAD_DIM), F32)] * 3,
        compiler_params=pltpu.CompilerParams(dimension_semantics=("arbitrary",),
                                             vmem_limit_bytes=_vmem_limit(blk_bytes, scratch_bytes)),
    )(hq, hq, hq, do, o, lse, *extra)


def _query_block(S):
    return min(512, S)


def _offsets(d, bq):
    row = jnp.arange(bq, dtype=jnp.int32)[:, None]
    col = jnp.arange(BLK, dtype=jnp.int32)[None, :]
    return d * BLK + row - col


def _causal_tables(bq, strict):
    r = bq // BLK
    tabs = []
    for d in range(-(r - 1), 1):
        delta = _offsets(d, bq)
        tabs.append(jnp.where((delta > 0) if strict else (delta >= 0), 1.0, 0.0))
    tabs.append(jnp.ones((bq, BLK), F32))
    return jnp.stack(tabs).astype(F32)


def _dilated_tables(bq):
    r = bq // BLK
    limit = sorted(w for w, _ in DILATED_PATTERNS)[-2]
    assert all(BLK % dil == 0 for _, dil in DILATED_PATTERNS)
    d_far = -(-(limit + BLK) // BLK)
    tabs = []
    for d in range(-(r - 1), d_far + 1):
        mult = _dilated_mult(_offsets(d, bq))
        tabs.append(jnp.where(mult > 0, jnp.log(jnp.maximum(mult, 1.0)), NEG_INF))
    return jnp.stack(tabs).astype(F32)


def _qblk(i, bq):
    return pl.ds(pl.multiple_of(i * bq, bq), bq)


def _sb_block(z, valid, t_ex, run):
    t = jnp.log1p(jnp.exp(-jnp.abs(z)))
    lsig = jnp.minimum(z, 0.0) - t
    m = -(jnp.maximum(z, 0.0) + t) * valid
    after = _split_dot(m, t_ex)
    a = jnp.exp(lsig + after + run) * valid
    return a, m, lsig


def _attn_fwd_wide(name, hq, layer_kind, n_heads, n_sb, cf=None, cft=None, rope_c=None, rope_s=None):
    S = hq.shape[0]
    D = n_heads * HEAD_DIM
    bq = _query_block(S)
    r = bq // BLK
    nq = S // bq
    scale = HEAD_DIM ** -0.5
    even = layer_kind == "even"
    if even:
        tabs = (jnp.where(_causal_tables(bq, False) > 0, 0.0, NEG_INF), _causal_tables(bq, True))
    else:
        tabs = (_dilated_tables(bq),)
    n_tab = tabs[0].shape[0]

    def body(*refs):
        if even:
            q_ref, k_ref, v_ref, cf_ref, cft_ref, bias_ref, valid_ref, o_ref, ob_ref, lse_ref, qs, ks, vs = refs
        else:
            q_ref, k_ref, v_ref, c_ref, s_ref, bias_ref, o_ref, ob_ref, lse_ref, qs, ks, vs = refs
        h = pl.program_id(0)
        if even:
            qs[...] = q_ref[...].astype(BF16)
            ks[...] = k_ref[...].astype(BF16)
        else:
            qs[...] = _rope(q_ref[...], c_ref[...], s_ref[...]).astype(BF16)
            ks[...] = _rope(k_ref[...], c_ref[...], s_ref[...]).astype(BF16)
        vs[...] = v_ref[...].astype(BF16)

        def softmax_head(hh):
            def qblock(i, carry):
                qi = qs[_qblk(i, bq), :]
                if even:
                    lane = lax.broadcasted_iota(jnp.int32, (bq, LANES), 1)
                    cfq = jnp.sum(jnp.where(lane == hh, cf_ref[_qblk(i, bq), :], 0.0), axis=1, keepdims=True)

                def kblock(j, c):
                    m_run, l_run, acc = c
                    z = _dot_nt(qi, ks[_blk(j), :]) * scale + bias_ref[jnp.minimum(r * i - j + (r - 1), n_tab - 1)]
                    if even:
                        z = z + (cfq - cft_ref[hh, :, _blk(j)])
                    m_new = jnp.maximum(m_run, jnp.max(z, axis=1, keepdims=True))
                    p = jnp.exp(z - m_new)
                    alpha = jnp.exp(m_run - m_new)
                    l_new = alpha * l_run + jnp.sum(p, axis=1, keepdims=True)
                    acc = alpha * acc + jnp.dot(p.astype(BF16), vs[_blk(j), :], preferred_element_type=F32)
                    return m_new, l_new, acc

                init = (jnp.full((bq, 1), NEG_INF, F32), jnp.zeros((bq, 1), F32), jnp.zeros((bq, HEAD_DIM), F32))
                m_run, l_run, acc = lax.fori_loop(0, r * (i + 1), kblock, init)
                o = acc / l_run
                o_ref[_qblk(i, bq), :] = o
                ob_ref[_qblk(i, bq), :] = o.astype(BF16)
                lse_ref[_qblk(i, bq), :] = jnp.broadcast_to(m_run + jnp.log(l_run), (bq, HEAD_DIM))
                return carry

            lax.fori_loop(0, nq, qblock, 0)

        def sb_head():
            row = lax.broadcasted_iota(jnp.int32, (BLK, BLK), 0)
            col = lax.broadcasted_iota(jnp.int32, (BLK, BLK), 1)
            t_ex = jnp.where(row > col, 1.0, 0.0).astype(BF16)

            def qblock(i, carry):
                qi = qs[_qblk(i, bq), :]

                def kblock(jj, c):
                    run, acc = c
                    j = r * (i + 1) - 1 - jj
                    z = _dot_nt(qi, ks[_blk(j), :]) * scale
                    a, m, _ = _sb_block(z, valid_ref[jnp.minimum(r * i - j + (r - 1), r)], t_ex, run)
                    acc = acc + jnp.dot(a.astype(BF16), vs[_blk(j), :], preferred_element_type=F32)
                    return run + jnp.sum(m, axis=1, keepdims=True), acc

                init = (jnp.zeros((bq, 1), F32), jnp.zeros((bq, HEAD_DIM), F32))
                _, acc = lax.fori_loop(0, r * (i + 1), kblock, init)
                o_ref[_qblk(i, bq), :] = acc
                ob_ref[_qblk(i, bq), :] = acc.astype(BF16)
                lse_ref[_qblk(i, bq), :] = jnp.zeros((bq, HEAD_DIM), F32)
                return carry

            lax.fori_loop(0, nq, qblock, 0)

        if even:
            @pl.when(h < n_sb)
            def _():
                sb_head()

            @pl.when(h >= n_sb)
            def _():
                softmax_head(h - n_sb)
        else:
            softmax_head(h)

    head = lambda off: pl.BlockSpec((S, HEAD_DIM), lambda h, off=off: (0, off + h))
    full = pl.BlockSpec((S, LANES), lambda h: (0, 0))
    tab_specs = [pl.BlockSpec(t.shape, lambda h: (0, 0, 0)) for t in tabs]
    if even:
        extra_specs = [full, pl.BlockSpec(cft.shape, lambda h: (0, 0, 0))] + tab_specs
        extra = (cf, cft) + tabs
    else:
        extra_specs = [full, full] + tab_specs
        extra = (rope_c, rope_s) + tabs
    blk_bytes = 8 * _nbytes((S, HEAD_DIM), F32) + sum(_nbytes(t.shape, F32) for t in tabs)
    return pl.pallas_call(
        body, name=name, grid=(n_heads,),
        in_specs=[head(0), head(n_heads), head(2 * n_heads)] + extra_specs,
        out_specs=[head(0), head(0), head(0)],
        out_shape=[jax.ShapeDtypeStruct((S, D), F32), jax.ShapeDtypeStruct((S, D), BF16),
                   jax.ShapeDtypeStruct((S, D), F32)],
        scratch_shapes=[pltpu.VMEM((S, HEAD_DIM), BF16)] * 3,
        compiler_params=pltpu.CompilerParams(dimension_semantics=("arbitrary",),
                                             vmem_limit_bytes=_vmem_limit(blk_bytes, 3 * _nbytes((S, HEAD_DIM), BF16))),
    )(hq, hq, hq, *extra)


def _attn_bwd_wide(name, hq, do, o, lse, layer_kind, n_heads, n_sb, cf=None, cft=None, rope_c=None, rope_s=None):
    S = hq.shape[0]
    D = n_heads * HEAD_DIM
    bq = _query_block(S)
    r = bq // BLK
    nq = S // bq
    scale = HEAD_DIM ** -0.5
    even = layer_kind == "even"
    if even:
        tabs = (jnp.where(_causal_tables(bq, False) > 0, 0.0, NEG_INF), _causal_tables(bq, True))
    else:
        tabs = (_dilated_tables(bq),)
    n_tab = tabs[0].shape[0]

    def body(*refs):
        if even:
            (q_ref, k_ref, v_ref, do_ref, o_ref, lse_ref, cf_ref, cft_ref, bias_ref, valid_ref,
             dh_ref, dcft_ref, drow_ref, qs, ks, vs, dos, dq_acc, dk_acc, dv_acc) = refs
        else:
            (q_ref, k_ref, v_ref, do_ref, o_ref, lse_ref, c_ref, s_ref, bias_ref,
             dh_ref, qs, ks, vs, dos, dq_acc, dk_acc, dv_acc) = refs
        h = pl.program_id(0)
        if even:
            qs[...] = q_ref[...].astype(BF16)
            ks[...] = k_ref[...].astype(BF16)

            @pl.when(h == 0)
            def _():
                dcft_ref[...] = jnp.zeros_like(dcft_ref)
                drow_ref[...] = jnp.zeros_like(drow_ref)
        else:
            qs[...] = _rope(q_ref[...], c_ref[...], s_ref[...]).astype(BF16)
            ks[...] = _rope(k_ref[...], c_ref[...], s_ref[...]).astype(BF16)
        vs[...] = v_ref[...].astype(BF16)
        dos[...] = do_ref[...].astype(BF16)
        dk_acc[...] = jnp.zeros_like(dk_acc)
        dv_acc[...] = jnp.zeros_like(dv_acc)

        def softmax_head(hh):
            def qblock(i, carry):
                qi = qs[_qblk(i, bq), :]
                doi = dos[_qblk(i, bq), :]
                dvec = jnp.sum(do_ref[_qblk(i, bq), :] * o_ref[_qblk(i, bq), :], axis=1, keepdims=True)
                lse_i = jnp.max(lse_ref[_qblk(i, bq), :], axis=1, keepdims=True)
                if even:
                    lane = lax.broadcasted_iota(jnp.int32, (bq, LANES), 1)
                    cfq = jnp.sum(jnp.where(lane == hh, cf_ref[_qblk(i, bq), :], 0.0), axis=1, keepdims=True)

                def kblock(j, c):
                    dq, ds_rows = c
                    kj = ks[_blk(j), :]
                    z = _dot_nt(qi, kj) * scale + bias_ref[jnp.minimum(r * i - j + (r - 1), n_tab - 1)]
                    if even:
                        z = z + (cfq - cft_ref[hh, :, _blk(j)])
                    p = jnp.exp(z - lse_i)
                    dp = _dot_nt(doi, vs[_blk(j), :])
                    ds = p * (dp - dvec)
                    dsb = (ds * scale).astype(BF16)
                    dk_acc[_blk(j), :] += _dot_tn(dsb, qi)
                    dv_acc[_blk(j), :] += _dot_tn(p.astype(BF16), doi)
                    if even:
                        dcft_ref[hh, :, _blk(j)] += -jnp.sum(ds, axis=0, keepdims=True)
                    return (dq + jnp.dot(dsb, kj, preferred_element_type=F32),
                            ds_rows + jnp.sum(ds, axis=1, keepdims=True))

                dq, ds_rows = lax.fori_loop(0, r * (i + 1), kblock,
                                            (jnp.zeros((bq, HEAD_DIM), F32), jnp.zeros((bq, 1), F32)))
                dq_acc[_qblk(i, bq), :] = dq
                if even:
                    drow_ref[_qblk(i, bq), :] += jnp.where(lane == hh, ds_rows, 0.0)
                return carry

            lax.fori_loop(0, nq, qblock, 0)

        def sb_head():
            row = lax.broadcasted_iota(jnp.int32, (BLK, BLK), 0)
            col = lax.broadcasted_iota(jnp.int32, (BLK, BLK), 1)
            t_ex = jnp.where(row > col, 1.0, 0.0).astype(BF16)
            t_in = jnp.where(row >= col, 1.0, 0.0).astype(BF16)

            def qblock(i, carry):
                qi = qs[_qblk(i, bq), :]
                doi = dos[_qblk(i, bq), :]
                nkb = r * (i + 1)

                def e_total(jj, c):
                    run, tot = c
                    j = nkb - 1 - jj
                    z = _dot_nt(qi, ks[_blk(j), :]) * scale
                    a, m, _ = _sb_block(z, valid_ref[jnp.minimum(r * i - j + (r - 1), r)], t_ex, run)
                    e = _dot_nt(doi, vs[_blk(j), :]) * a
                    return run + jnp.sum(m, axis=1, keepdims=True), tot + jnp.sum(e, axis=1, keepdims=True)

                zero = jnp.zeros((bq, 1), F32)
                _, e_tot = lax.fori_loop(0, nkb, e_total, (zero, zero))

                def kblock(jj, c):
                    run, e_run, dq = c
                    j = nkb - 1 - jj
                    kj = ks[_blk(j), :]
                    z = _dot_nt(qi, kj) * scale
                    valid = valid_ref[jnp.minimum(r * i - j + (r - 1), r)]
                    a, m, lsig = _sb_block(z, valid, t_ex, run)
                    sig = jnp.exp(lsig)
                    e = _dot_nt(doi, vs[_blk(j), :]) * a
                    e_before = e_tot - (_split_dot(e, t_in) + e_run)
                    dz = (e * (1.0 - sig) - sig * e_before) * valid
                    dzb = (dz * scale).astype(BF16)
                    dk_acc[_blk(j), :] += _dot_tn(dzb, qi)
                    dv_acc[_blk(j), :] += _dot_tn(a.astype(BF16), doi)
                    return (run + jnp.sum(m, axis=1, keepdims=True), e_run + jnp.sum(e, axis=1, keepdims=True),
                            dq + jnp.dot(dzb, kj, preferred_element_type=F32))

                _, _, dq = lax.fori_loop(0, nkb, kblock, (zero, zero, jnp.zeros((bq, HEAD_DIM), F32)))
                dq_acc[_qblk(i, bq), :] = dq
                return carry

            lax.fori_loop(0, nq, qblock, 0)

        if even:
            @pl.when(h < n_sb)
            def _():
                sb_head()

            @pl.when(h >= n_sb)
            def _():
                softmax_head(h - n_sb)

            dh_ref[0] = dq_acc[...].astype(BF16)
            dh_ref[1] = dk_acc[...].astype(BF16)
        else:
            softmax_head(h)
            dh_ref[0] = _rope_t(dq_acc[...], c_ref[...], s_ref[...]).astype(BF16)
            dh_ref[1] = _rope_t(dk_acc[...], c_ref[...], s_ref[...]).astype(BF16)
        dh_ref[2] = dv_acc[...].astype(BF16)

    head = lambda off: pl.BlockSpec((S, HEAD_DIM), lambda h, off=off: (0, off + h))
    full = pl.BlockSpec((S, LANES), lambda h: (0, 0))
    tfull = pl.BlockSpec((n_heads - n_sb, 1, S), lambda h: (0, 0, 0))
    tab_specs = [pl.BlockSpec(t.shape, lambda h: (0, 0, 0)) for t in tabs]
    dh_spec = pl.BlockSpec((3, S, HEAD_DIM), lambda h: (0, 0, h))
    dh_shape = jax.ShapeDtypeStruct((3, S, D), BF16)
    if even:
        extra_specs, extra = [full, tfull] + tab_specs, (cf, cft) + tabs
        out_specs = [dh_spec, tfull, full]
        out_shape = [dh_shape, jax.ShapeDtypeStruct((n_heads - n_sb, 1, S), F32),
                     jax.ShapeDtypeStruct((S, LANES), F32)]
    else:
        extra_specs, extra = [full, full] + tab_specs, (rope_c, rope_s) + tabs
        out_specs = [dh_spec]
        out_shape = [dh_shape]
    blk_bytes = 10 * _nbytes((S, HEAD_DIM), F32) + sum(_nbytes(t.shape, F32) for t in tabs)
    scratch_bytes = 4 * _nbytes((S, HEAD_DIM), BF16) + 3 * _nbytes((S, HEAD_DIM), F32)
    return pl.pallas_call(
        body, name=name, grid=(n_heads,),
        in_specs=[head(0), head(n_heads), head(2 * n_heads), head(0), head(0), head(0)] + extra_specs,
        out_specs=out_specs, out_shape=out_shape,
        scratch_shapes=[pltpu.VMEM((S, HEAD_DIM), BF16)] * 4 + [pltpu.VMEM((S, HEAD_DIM), F32)] * 3,
        compiler_params=pltpu.CompilerParams(dimension_semantics=("arbitrary",),
                                             vmem_limit_bytes=_vmem_limit(blk_bytes, scratch_bytes)),
    )(hq, hq, hq, do, o, lse, *extra)


def _ffn_fwd(tag, x, g, wgu, wd, fs):
    n = _rms_fwd(tag + "_norm", x, g)
    gu = _mm(tag + "_gu", n, wgu, "nn", F32)
    h = _swiglu_fwd(tag + "_act", gu, fs)
    y = _mm(tag + "_down", h, wd, "nn", F32, res=x, alpha=0.5)
    return y, (x, g, n, gu, h)


def _ffn_bwd(tag, dx, dxb, wgu, wd, fs, saved):
    x, g, n, gu, h = saved
    dh = _mm(tag + "_dh", dxb, wd, "nt", F32, alpha=0.5)
    dwd = _mm(tag + "_dwd", h, dxb, "tn", F32, alpha=0.5)
    dgu = _swiglu_bwd(tag + "_dact", dh, gu, fs)
    dwgu = _mm(tag + "_dwgu", n, dgu, "tn", F32)
    dn = _mm(tag + "_dn", dgu, wgu, "nt", F32)
    dx_in, dxb_in, dg = _rms_bwd(tag + "_dnorm", dn, x, g, dx)
    return dx_in, dxb_in, dg, dwgu, dwd


def _mixer_fwd(tag, kind, x, g, wqkv, wout, n_heads, n_sb, wf=None, bf=None, rope=None):
    n = _rms_fwd(tag + "_norm", x, g)
    hq = _mm(tag + "_qkv", n, wqkv, "nn", F32)
    if kind == "even":
        hf = _mm(tag + "_gate", n, wf, "nn", F32)
        cf, cft = _gate_fwd(tag + "_cumgate", hf, bf)
        cft = cft[:n_heads - n_sb].reshape(n_heads - n_sb, 1, -1)
        o, ob, lse = _attn_fwd_wide(tag + "_attn", hq, kind, n_heads, n_sb, cf=cf, cft=cft)
    else:
        hf = cf = cft = None
        o, ob, lse = _attn_fwd_wide(tag + "_attn", hq, kind, n_heads, n_sb, rope_c=rope[0], rope_s=rope[1])
    y = _mm(tag + "_out", ob, wout, "nn", F32, res=x)
    return y, (x, g, n, hq, hf, cf, cft, o, ob, lse)


def _mixer_bwd(tag, kind, dx, dxb, wqkv, wout, n_heads, n_sb, saved, wf=None, bf=None, rope=None):
    x, g, n, hq, hf, cf, cft, o, ob, lse = saved
    do = _mm(tag + "_do", dxb, wout, "nt", F32)
    dwout = _mm(tag + "_dwout", ob, dxb, "tn", F32)
    if kind == "even":
        dh3, dcft, drow = _attn_bwd_wide(tag + "_dattn", hq, do, o, lse, kind, n_heads, n_sb, cf=cf, cft=cft)
    else:
        (dh3,) = _attn_bwd_wide(tag + "_dattn", hq, do, o, lse, kind, n_heads, n_sb, rope_c=rope[0], rope_s=rope[1])
    dwqkv = _mm(tag + "_dwqkv", n, dh3, "tn", F32)
    dn = _mm(tag + "_dn", dh3, wqkv, "nt", F32)
    dwf = db = None
    if kind == "even":
        n_fox = n_heads - n_sb
        dcft = jnp.pad(dcft.reshape(n_fox, -1), ((0, LANES - n_fox), (0, 0)))
        dhf, db = _gate_bwd(tag + "_dcumgate", dcft, drow, hf, bf)
        dwf = _mm(tag + "_dwf", n, dhf, "tn", F32)
        dn = _mm(tag + "_dn_gate", dhf, wf, "nt", F32, res=dn)
    dx_in, dxb_in, dg = _rms_bwd(tag + "_dnorm", dn, x, g, dx)
    return dx_in, dxb_in, dg, dwqkv, dwout, dwf, db


def _local_step(x, target, w, fs, n_heads, n_sb):
    S, D = x.shape
    rope = _rope_tables(S)
    kinds = ("even", "odd")
    saved = []
    h = x
    for l, kind in enumerate(kinds):
        ng = [w["norm_g"][l, i][None, :] for i in range(3)]
        h, s1 = _ffn_fwd(f"l{l}_ffn1", h, ng[0], w["wgu1"][l], w["wd1"][l], fs)
        if kind == "even":
            h, s2 = _mixer_fwd(f"l{l}_mix", kind, h, ng[1], w["wqkv_e"], w["wout_e"], n_heads, n_sb,
                               wf=w["wf"], bf=w["bf"])
        else:
            h, s2 = _mixer_fwd(f"l{l}_mix", kind, h, ng[1], w["wqkv_o"], w["wout_o"], n_heads, n_sb, rope=rope)
        h, s3 = _ffn_fwd(f"l{l}_ffn2", h, ng[2], w["wgu2"][l], w["wd2"][l], fs)
        saved.append((s1, s2, s3))

    dx, dxb, dfinal, loss = _loss_head("loss_head", h, w["final_g"], target)
    grads = {"dfinal": dfinal, "dnorm": [[None] * 3 for _ in kinds],
             "dwgu1": [None, None], "dwd1": [None, None], "dwgu2": [None, None], "dwd2": [None, None]}
    for l in (1, 0):
        kind = kinds[l]
        s1, s2, s3 = saved[l]
        dx, dxb, dg, grads["dwgu2"][l], grads["dwd2"][l] = _ffn_bwd(
            f"l{l}_ffn2", dx, dxb, w["wgu2"][l], w["wd2"][l], fs, s3)
        grads["dnorm"][l][2] = dg
        if kind == "even":
            dx, dxb, dg, grads["dwqkv_e"], grads["dwout_e"], grads["dwf"], grads["db"] = _mixer_bwd(
                f"l{l}_mix", kind, dx, dxb, w["wqkv_e"], w["wout_e"], n_heads, n_sb, s2, wf=w["wf"], bf=w["bf"])
        else:
            dx, dxb, dg, grads["dwqkv_o"], grads["dwout_o"], _, _ = _mixer_bwd(
                f"l{l}_mix", kind, dx, dxb, w["wqkv_o"], w["wout_o"], n_heads, n_sb, s2, rope=rope)
        grads["dnorm"][l][1] = dg
        dx, dxb, dg, grads["dwgu1"][l], grads["dwd1"][l] = _ffn_bwd(
            f"l{l}_ffn1", dx, dxb, w["wgu1"][l], w["wd1"][l], fs, s1)
        grads["dnorm"][l][0] = dg
    return loss, dx, grads


def _cast_into(name, shard, layer, chip, full_shape, place, full=None):
    R, C = shard.shape[-2:]
    tr = _row_tile(R, 512, step=16)
    if layer is None:
        in_spec = pl.BlockSpec((tr, C), lambda i, k: (i, 0))
    else:
        in_spec = pl.BlockSpec((None, tr, C), lambda i, k: (layer, i, 0))
    lead = (None,) * (len(full_shape) - 2)
    out_spec = pl.BlockSpec(lead + (tr, C), lambda i, k: place(i, k[0]))

    def body(*refs):
        k_ref, w_ref = refs[:2]
        o_ref = refs[-1]
        o_ref[...] = w_ref[...].astype(BF16)

    in_specs = [in_spec] + ([_ANY] if full is not None else [])
    args = (chip, shard) + ((full,) if full is not None else ())
    grid_spec = pltpu.PrefetchScalarGridSpec(num_scalar_prefetch=1, grid=(R // tr,), in_specs=in_specs, out_specs=out_spec)
    return pl.pallas_call(
        body, name=name, grid_spec=grid_spec, out_shape=jax.ShapeDtypeStruct(full_shape, BF16),
        input_output_aliases={2: 0} if full is not None else {},
        compiler_params=pltpu.CompilerParams(dimension_semantics=("arbitrary",)),
    )(*args)


def _region_shape(grad, kind):
    if kind == "lead":
        return grad.shape[1] // N_CORES, grad.shape[2]
    rows, cols = grad.shape
    if kind == "cols":
        return rows // N_CORES, cols // N_CHIPS
    return rows // (N_CHIPS * N_CORES), cols


def _region_add(name, grad, kind, landed, core):
    rh, cw = _region_shape(grad, kind)
    tr = _row_tile(rh, 256, step=16)
    nrb = rh // tr
    if kind == "cols":
        g_spec = pl.BlockSpec((tr, cw), lambda k, r, c: (c[0] * nrb + r, k))
    elif kind == "rows":
        g_spec = pl.BlockSpec((tr, cw), lambda k, r, c: ((N_CORES * k + c[0]) * nrb + r, 0))
    else:
        g_spec = pl.BlockSpec((None, tr, cw), lambda k, r, c: (k, c[0] * nrb + r, 0))
    l_spec = pl.BlockSpec((None, tr, cw), lambda k, r, c: (k, r, 0))

    def body(c_ref, g_ref, l_ref, o_ref):
        o_ref[...] = (g_ref[...] + l_ref[...]).astype(BF16)

    grid_spec = pltpu.PrefetchScalarGridSpec(
        num_scalar_prefetch=1, grid=(N_CHIPS, nrb), in_specs=[g_spec, l_spec], out_specs=l_spec)
    return pl.pallas_call(
        body, name=name, grid_spec=grid_spec, out_shape=jax.ShapeDtypeStruct(landed.shape, BF16),
        compiler_params=pltpu.CompilerParams(dimension_semantics=("parallel", "parallel"),
                                             vmem_limit_bytes=_vmem_limit(3 * _nbytes((tr, cw), F32))),
    )(core, grad, landed)


def _chip_sum(name, pair, landed, pos):
    _, rh, cw = pair.shape
    tr = _row_tile(rh, max(16, 2**20 // (cw * 4)), step=16)
    nrb = rh // tr

    def body(p_ref, own_ref, l_ref, o_ref):
        acc = own_ref[...].astype(F32)
        for s in range(N_CHIPS - 1):
            acc = acc + l_ref[s].astype(F32)
        o_ref[...] = acc

    grid_spec = pltpu.PrefetchScalarGridSpec(
        num_scalar_prefetch=1, grid=(nrb,),
        in_specs=[pl.BlockSpec((None, tr, cw), lambda r, p: (p[0], r, 0)),
                  pl.BlockSpec((N_CHIPS - 1, tr, cw), lambda r, p: (0, r, 0))],
        out_specs=pl.BlockSpec((tr, cw), lambda r, p: (p[1] * nrb + r, 0)))
    return pl.pallas_call(
        body, name=name, grid_spec=grid_spec, out_shape=jax.ShapeDtypeStruct((N_CORES * rh, cw), F32),
        compiler_params=pltpu.CompilerParams(dimension_semantics=("arbitrary",)),
    )(pos, pair, landed)


def _sum_leading(name, parts):
    n, R, C = parts.shape
    tr = _row_tile(R, max(8, (2**20 // (C * 4)) // 8 * 8))

    def body(p_ref, o_ref):
        acc = p_ref[0]
        for s in range(1, n):
            acc = acc + p_ref[s]
        o_ref[...] = acc

    return pl.pallas_call(
        body, name=name, grid=(R // tr,),
        in_specs=[pl.BlockSpec((n, tr, C), lambda i: (0, i, 0))],
        out_specs=pl.BlockSpec((tr, C), lambda i: (i, 0)),
        out_shape=jax.ShapeDtypeStruct((R, C), F32),
        compiler_params=pltpu.CompilerParams(dimension_semantics=("parallel",)),
    )(parts)


def _adamw(name, w, g, m, v):
    shape = w.shape
    to2d = lambda t: t.reshape(-1, shape[-1]) if t.ndim > 1 else t.reshape(1, -1)
    w2, g2, m2, v2 = (to2d(t) for t in (w, g, m, v))
    R, C = w2.shape
    tr = _row_tile(R, 256)

    def body(w_ref, g_ref, m_ref, v_ref, d_ref, nm_ref, nv_ref):
        gv = g_ref[...]
        nm = ADAM_B1 * m_ref[...] + (1.0 - ADAM_B1) * gv
        nv = ADAM_B2 * v_ref[...] + (1.0 - ADAM_B2) * (gv * gv)
        m_hat = nm / (1.0 - ADAM_B1 ** ADAM_STEP)
        v_hat = nv / (1.0 - ADAM_B2 ** ADAM_STEP)
        d_ref[...] = -ADAM_LR * (m_hat / (jnp.sqrt(v_hat) + ADAM_EPS) + ADAM_WD * w_ref[...])
        nm_ref[...] = nm
        nv_ref[...] = nv

    spec = pl.BlockSpec((tr, C), lambda i: (i, 0))
    sds = jax.ShapeDtypeStruct((R, C), F32)
    d, nm, nv = pl.pallas_call(
        body, name=name, grid=(R // tr,), in_specs=[spec] * 4, out_specs=[spec] * 3, out_shape=[sds] * 3,
        compiler_params=pltpu.CompilerParams(dimension_semantics=("parallel",),
                                             vmem_limit_bytes=_vmem_limit(7 * _nbytes((tr, C), F32))),
    )(w2, g2, m2, v2)
    return d.reshape(shape), nm.reshape(shape), nv.reshape(shape)


_ANY = pl.BlockSpec(memory_space=pl.ANY)


def _mesh_pos():
    return lax.axis_index("x"), lax.axis_index("y"), lax.axis_index("c")


def _other_chips(x, y):
    return [(1 - x, y), (x, 1 - y), (1 - x, 1 - y)]


def _gather_over_chips(name, fulls, views):
    n = len(views)
    nf = len(fulls)

    def body(*refs):
        full = refs[nf:2 * nf]
        ici_send, ici_recv, d2d_send, d2d_recv = refs[2 * nf:]
        x, y, c = _mesh_pos()
        chips = _other_chips(x, y)
        mine = 2 * x + y
        sibling = (x, y, 1 - c)

        def ici(a, p, k):
            i, view, _ = views[a]
            part = view(full[i], k, c)
            return pltpu.make_async_remote_copy(
                src_ref=part, dst_ref=part, send_sem=ici_send.at[a, p], recv_sem=ici_recv.at[a, p],
                device_id=(*chips[p], c), device_id_type=MESH)

        def d2d(a, p, h):
            i, view, _ = views[a]
            px, py = chips[p]
            part = view(full[i], 2 * px + py, h)
            return pltpu.make_async_remote_copy(
                src_ref=part, dst_ref=part, send_sem=d2d_send.at[a, p], recv_sem=d2d_recv.at[a, p],
                device_id=sibling, device_id_type=MESH)

        sends = [ici(a, p, mine) for a in range(n) for p in range(3)]
        for cp in sends:
            cp.start()
        passed = []
        for a in range(n):
            for p, (px, py) in enumerate(chips):
                ici(a, p, 2 * px + py).wait_recv()
                if views[a][2]:
                    fwd = d2d(a, p, c)
                    fwd.start()
                    passed.append(fwd)
        for a in range(n):
            if views[a][2]:
                for p in range(3):
                    d2d(a, p, 1 - c).wait_recv()
        for cp in sends + passed:
            cp.wait_send()

    return pl.pallas_call(
        body, name=name, in_specs=[_ANY] * nf, out_specs=[_ANY] * nf,
        out_shape=[jax.ShapeDtypeStruct(f.shape, f.dtype) for f in fulls],
        input_output_aliases={i: i for i in range(nf)},
        scratch_shapes=[pltpu.SemaphoreType.DMA((n, 3))] * 4,
        compiler_params=pltpu.CompilerParams(has_side_effects=True),
    )(*fulls)


def _region_view(ref, kind, k, c):
    if kind == "lead":
        rh = ref.shape[1] // N_CORES
        return ref.at[k, pl.ds(pl.multiple_of(c * rh, 8), rh), :]
    rows, cols = ref.shape
    if kind == "cols":
        rh, cw = rows // N_CORES, cols // N_CHIPS
        return ref.at[pl.ds(pl.multiple_of(c * rh, 8), rh), pl.ds(k * cw, cw)]
    rh = rows // (N_CHIPS * N_CORES)
    return ref.at[pl.ds(pl.multiple_of((N_CORES * k + c) * rh, 8), rh), :]


def _send_to_sibling(name, grads, kinds):
    n = len(grads)
    shapes = [jax.ShapeDtypeStruct((N_CHIPS,) + _region_shape(g, kd), F32) for g, kd in zip(grads, kinds)]

    def body(*refs):
        g_ref, land = refs[:n], refs[n:2 * n]
        send_sems, recv_sems = refs[2 * n:]
        x, y, c = _mesh_pos()
        copies = []
        for a in range(n):
            for k in range(N_CHIPS):
                cp = pltpu.make_async_remote_copy(
                    src_ref=_region_view(g_ref[a], kinds[a], k, 1 - c), dst_ref=land[a].at[k],
                    send_sem=send_sems.at[a, k], recv_sem=recv_sems.at[a, k],
                    device_id=(x, y, 1 - c), device_id_type=MESH)
                cp.start()
                copies.append(cp)
        for cp in copies:
            cp.wait_recv()
        for cp in copies:
            cp.wait_send()

    return pl.pallas_call(
        body, name=name, in_specs=[_ANY] * n, out_specs=[_ANY] * n, out_shape=shapes,
        scratch_shapes=[pltpu.SemaphoreType.DMA((n, N_CHIPS)), pltpu.SemaphoreType.DMA((n, N_CHIPS))],
        compiler_params=pltpu.CompilerParams(has_side_effects=True),
    )(*grads)


def _scatter_over_chips(name, pair_sums):
    n = len(pair_sums)

    def body(*refs):
        p_ref, land = refs[:n], refs[n:2 * n]
        send_sems, recv_sems = refs[2 * n:]
        x, y, c = _mesh_pos()
        chips = _other_chips(x, y)
        sends = []
        for a in range(n):
            for p, (px, py) in enumerate(chips):
                cp = pltpu.make_async_remote_copy(
                    src_ref=p_ref[a].at[2 * px + py], dst_ref=land[a].at[p], send_sem=send_sems.at[a, p],
                    recv_sem=recv_sems.at[a, p], device_id=(px, py, c), device_id_type=MESH)
                cp.start()
                sends.append(cp)
        for cp in sends:
            cp.wait_recv()
        for cp in sends:
            cp.wait_send()

    return pl.pallas_call(
        body, name=name, in_specs=[_ANY] * n, out_specs=[_ANY] * n,
        out_shape=[jax.ShapeDtypeStruct((N_CHIPS - 1,) + p.shape[1:], p.dtype) for p in pair_sums],
        scratch_shapes=[pltpu.SemaphoreType.DMA((n, 3)), pltpu.SemaphoreType.DMA((n, 3))],
        compiler_params=pltpu.CompilerParams(has_side_effects=True),
    )(*pair_sums)


def _swap_halves(name, shards):
    n = len(shards)

    def body(*refs):
        out = refs[n:2 * n]
        send_sems, recv_sems = refs[2 * n:]
        x, y, c = _mesh_pos()
        sends = []
        for a in range(n):
            rh = out[a].shape[0] // N_CORES
            mine = out[a].at[pl.ds(pl.multiple_of(c * rh, 8), rh), :]
            cp = pltpu.make_async_remote_copy(
                src_ref=mine, dst_ref=mine, send_sem=send_sems.at[a], recv_sem=recv_sems.at[a],
                device_id=(x, y, 1 - c), device_id_type=MESH)
            cp.start()
            sends.append(cp)
        for a in range(n):
            rh = out[a].shape[0] // N_CORES
            theirs = out[a].at[pl.ds(pl.multiple_of((1 - c) * rh, 8), rh), :]
            pltpu.make_async_remote_copy(
                src_ref=theirs, dst_ref=theirs, send_sem=send_sems.at[a], recv_sem=recv_sems.at[a],
                device_id=(x, y, 1 - c), device_id_type=MESH).wait_recv()
        for cp in sends:
            cp.wait_send()

    return pl.pallas_call(
        body, name=name, in_specs=[_ANY] * n, out_specs=[_ANY] * n,
        out_shape=[jax.ShapeDtypeStruct(s.shape, s.dtype) for s in shards],
        input_output_aliases={i: i for i in range(n)},
        scratch_shapes=[pltpu.SemaphoreType.DMA((n,)), pltpu.SemaphoreType.DMA((n,))],
        compiler_params=pltpu.CompilerParams(has_side_effects=True),
    )(*shards)


def _gather_all_devices(name, block):
    R, C = block.shape
    ndev = N_CHIPS * N_CORES

    def body(b_ref, out_ref, send_sems, recv_sems, local_sem):
        x, y, c = _mesh_pos()
        mine = 4 * x + 2 * y + c
        own = pltpu.make_async_copy(b_ref, out_ref.at[mine], local_sem)
        own.start()
        sends = []
        for mask in range(1, ndev):
            fx, fy, fc = (mask >> 2) & 1, (mask >> 1) & 1, mask & 1
            px, py, pc = x ^ fx, y ^ fy, c ^ fc
            cp = pltpu.make_async_remote_copy(
                src_ref=b_ref, dst_ref=out_ref.at[mine], send_sem=send_sems.at[mask - 1],
                recv_sem=recv_sems.at[mask - 1], device_id=(px, py, pc), device_id_type=MESH)
            cp.start()
            sends.append(cp)
        for mask in range(1, ndev):
            fx, fy, fc = (mask >> 2) & 1, (mask >> 1) & 1, mask & 1
            px, py, pc = x ^ fx, y ^ fy, c ^ fc
            pltpu.make_async_remote_copy(
                src_ref=b_ref, dst_ref=out_ref.at[4 * px + 2 * py + pc], send_sem=send_sems.at[mask - 1],
                recv_sem=recv_sems.at[mask - 1], device_id=(px, py, pc), device_id_type=MESH).wait_recv()
        for cp in sends:
            cp.wait_send()
        own.wait()

    return pl.pallas_call(
        body, name=name, in_specs=[_ANY], out_specs=_ANY,
        out_shape=jax.ShapeDtypeStruct((ndev, R, C), F32),
        scratch_shapes=[pltpu.SemaphoreType.DMA((ndev - 1,)), pltpu.SemaphoreType.DMA((ndev - 1,)),
                        pltpu.SemaphoreType.DMA(())],
        compiler_params=pltpu.CompilerParams(has_side_effects=True),
    )(block)


def _reduce_scatter(grads, kinds, pos):
    landed = _send_to_sibling("rs_pair_send", grads, kinds)
    pair = [_region_add(f"rs_pair_add_{a}", g, kd, ld, pos[1:]) for a, (g, kd, ld) in enumerate(zip(grads, kinds, landed))]
    parts = _scatter_over_chips("rs_chip_send", pair)
    shards = [_chip_sum(f"rs_chip_add_{a}", p, ld, pos) for a, (p, ld) in enumerate(zip(pair, parts))]
    return _swap_halves("rs_swap_halves", shards)


def kernel(x, norm_g, ffn1_w_gate, ffn1_w_up, ffn1_w_down, ffn2_w_gate, ffn2_w_up, ffn2_w_down, even_w_in, even_b_forget, even_w_out, odd_w_qkv, odd_w_out, final_norm_g, loss_target, m_norm_g, m_ffn1_w_gate, m_ffn1_w_up, m_ffn1_w_down, m_ffn2_w_gate, m_ffn2_w_up, m_ffn2_w_down, m_even_w_in, m_even_b_forget, m_even_w_out, m_odd_w_qkv, m_odd_w_out, m_final_norm_g, v_norm_g, v_ffn1_w_gate, v_ffn1_w_up, v_ffn1_w_down, v_ffn2_w_gate, v_ffn2_w_up, v_ffn2_w_down, v_even_w_in, v_even_b_forget, v_even_w_out, v_odd_w_qkv, v_odd_w_out, v_final_norm_g):
    _, S, D = x.shape
    L = norm_g.shape[0]
    assert L == 2 and even_w_in.shape[0] == 1 and odd_w_qkv.shape[0] == 1
    fs = ffn1_w_gate.shape[2]
    F = N_CHIPS * fs
    wc = even_w_in.shape[2]
    n_heads = D // HEAD_DIM
    n_fox = N_CHIPS * wc - 3 * D
    n_sb = n_heads - n_fox
    qs = odd_w_qkv.shape[2]
    os_ = even_w_out.shape[1]
    ns = norm_g.shape[2]
    xi, yi, ci = _mesh_pos()
    chip = 2 * xi + yi

    pos = jnp.stack([chip, ci]).astype(jnp.int32)
    kchip = pos[:1]
    lane = lambda start, size: pl.ds(pl.multiple_of(start, LANES), size)
    sub = lambda start, size: pl.ds(pl.multiple_of(start, 16), size)
    gate_view = lambda r, k, h: r.at[sub(h * (D // 2), D // 2), lane(k * 2 * fs, fs)]
    up_view = lambda r, k, h: r.at[sub(h * (D // 2), D // 2), lane(k * 2 * fs + fs, fs)]
    down_view = lambda r, k, h: r.at[sub(k * fs + h * (fs // 2), fs // 2), :]
    out_view = lambda r, k, h: r.at[sub(k * os_ + h * (os_ // 2), os_ // 2), :]
    tr_d = _row_tile(fs, 512, step=16)
    tr_o = _row_tile(os_, 512, step=16)
    fulls, views = [], []
    for f, (wg, wu, wd) in enumerate(((ffn1_w_gate, ffn1_w_up, ffn1_w_down), (ffn2_w_gate, ffn2_w_up, ffn2_w_down))):
        for l in range(L):
            t = f"cast_ffn{f + 1}_l{l}"
            gu = _cast_into(t + "_gate", wg, l, kchip, (D, 2 * F), lambda i, k: (i, 2 * k))
            gu = _cast_into(t + "_up", wu, l, kchip, (D, 2 * F), lambda i, k: (i, 2 * k + 1), full=gu)
            dn = _cast_into(t + "_down", wd, l, kchip, (F, D), lambda i, k: (k * (fs // tr_d) + i, 0))
            views += [(len(fulls), gate_view, True), (len(fulls), up_view, True), (len(fulls) + 1, down_view, True)]
            fulls += [gu, dn]
    o = len(fulls)
    fulls += [
        _cast_into("cast_win", even_w_in, 0, kchip, (N_CHIPS, D, wc), lambda i, k: (k, i, 0)),
        _cast_into("cast_wout_e", even_w_out, 0, kchip, (D, D), lambda i, k: (k * (os_ // tr_o) + i, 0)),
        _cast_into("cast_wqkv_o", odd_w_qkv, 0, kchip, (D, N_CHIPS * qs), lambda i, k: (i, k)),
        _cast_into("cast_wout_o", odd_w_out, 0, kchip, (D, D), lambda i, k: (k * (os_ // tr_o) + i, 0)),
        lax.dynamic_update_slice(jnp.zeros((L, 3, N_CHIPS * ns), F32), norm_g, (0, 0, chip * ns))]
    views += [
        (o, lambda r, k, h: r.at[k, sub(h * (D // 2), D // 2), :], True), (o + 1, out_view, True),
        (o + 2, lambda r, k, h: r.at[sub(h * (D // 2), D // 2), lane(k * qs, qs)], True), (o + 3, out_view, True),
        (o + 4, lambda r, k, h: r.at[:, :, lane(k * ns, ns)], False)]
    full = _gather_over_chips("gather_weights", fulls, views)
    wgu1, wd1 = [full[2 * l] for l in range(L)], [full[2 * l + 1] for l in range(L)]
    wgu2, wd2 = [full[2 * L + 2 * l] for l in range(L)], [full[2 * L + 2 * l + 1] for l in range(L)]
    win4, wout_e, wqkv_o, wout_o, norm_full = full[4 * L:]
    win = jnp.concatenate([win4[k] for k in range(N_CHIPS)], axis=1)
    wf = jnp.pad(win[:, 3 * D:], ((0, 0), (0, LANES - n_fox)))
    bf = jnp.pad(even_b_forget, ((0, 0), (0, LANES - n_fox)))
    w = {"wgu1": wgu1, "wd1": wd1, "wgu2": wgu2, "wd2": wd2, "wqkv_e": win[:, :3 * D], "wf": wf, "bf": bf,
         "wout_e": wout_e, "wqkv_o": wqkv_o, "wout_o": wout_o, "norm_g": norm_full,
         "final_g": final_norm_g[None, :]}

    loss_vec, grad_x, g = _local_step(x[0], loss_target[0], w, fs, n_heads, n_sb)

    dwin = jnp.concatenate([g["dwqkv_e"], g["dwf"][:, :n_fox]], axis=1)
    dwin4 = jnp.stack([dwin[:, k * wc:(k + 1) * wc] for k in range(N_CHIPS)])
    rs_in = (g["dwgu1"] + g["dwgu2"] + g["dwd1"] + g["dwd2"]
             + [dwin4, g["dwqkv_o"], g["dwout_e"], g["dwout_o"]])
    kinds = ["cols"] * (2 * L) + ["rows"] * (2 * L) + ["lead", "cols", "rows", "rows"]
    red = _reduce_scatter(rs_in, kinds, pos)
    gu1, gu2, gd1, gd2 = red[0:L], red[L:2 * L], red[2 * L:3 * L], red[3 * L:4 * L]
    g_win, g_qkv_o, g_wout_e, g_wout_o = red[4 * L:]

    small_rows = [g["dnorm"][l][i] for l in range(L) for i in range(3)] + [
        g["dfinal"], jnp.pad(g["db"], ((0, 0), (0, D - LANES))), jnp.pad(loss_vec, ((0, 0), (0, D - LANES)))]
    small = jnp.concatenate(small_rows + [jnp.zeros((16 - len(small_rows), D), F32)], axis=0)
    small_sum = _sum_leading("small_sum", _gather_all_devices("small_gather", small))
    loss = small_sum[3 * L + 2, 0]
    g_norm = lax.dynamic_slice_in_dim(small_sum[:3 * L].reshape(L, 3, D), chip * ns, ns, axis=2)
    g_final = small_sum[3 * L]
    g_bf = small_sum[3 * L + 1, :n_fox][None, :]

    grads = [
        g_norm,
        jnp.stack([t[:, :fs] for t in gu1]), jnp.stack([t[:, fs:] for t in gu1]), jnp.stack(gd1),
        jnp.stack([t[:, :fs] for t in gu2]), jnp.stack([t[:, fs:] for t in gu2]), jnp.stack(gd2),
        g_win[None], g_bf, g_wout_e[None], g_qkv_o[None], g_wout_o[None], g_final]
    weights = [norm_g, ffn1_w_gate, ffn1_w_up, ffn1_w_down, ffn2_w_gate, ffn2_w_up, ffn2_w_down,
               even_w_in, even_b_forget, even_w_out, odd_w_qkv, odd_w_out, final_norm_g]
    ms = [m_norm_g, m_ffn1_w_gate, m_ffn1_w_up, m_ffn1_w_down, m_ffn2_w_gate, m_ffn2_w_up, m_ffn2_w_down,
          m_even_w_in, m_even_b_forget, m_even_w_out, m_odd_w_qkv, m_odd_w_out, m_final_norm_g]
    vs = [v_norm_g, v_ffn1_w_gate, v_ffn1_w_up, v_ffn1_w_down, v_ffn2_w_gate, v_ffn2_w_up, v_ffn2_w_down,
          v_even_w_in, v_even_b_forget, v_even_w_out, v_odd_w_qkv, v_odd_w_out, v_final_norm_g]
    deltas, new_ms, new_vs = [], [], []
    for i, (wt, gt, mt, vt) in enumerate(zip(weights, grads, ms, vs)):
        d, nm, nv = _adamw(f"adamw_{i}", wt, gt, mt, vt)
        deltas.append(d)
        new_ms.append(nm)
        new_vs.append(nv)
    return (loss, grad_x[None], *grads, *deltas, *new_ms, *new_vs)
```

```python
import functools
import math

import jax
import jax.numpy as jnp
from jax import lax
from jax.experimental import pallas as pl
from jax.experimental.pallas import tpu as pltpu

F32 = jnp.float32
BF16 = jnp.bfloat16

HEAD_DIM = 128
ROPE_DIMS = 32
ROPE_THETA = 500000.0
DILATED_PATTERNS = ((128, 1), (512, 4), (2048, 16))
RMS_EPS = 1e-6
NEG_INF = -1e30
ADAM_LR = 0.001
ADAM_B1 = 0.9
ADAM_B2 = 0.999
ADAM_EPS = 1e-08
ADAM_WD = 0.01
ADAM_STEP = 10

N_CHIPS = 4
N_CORES = 2
LANES = 128
BLK = 128
VMEM_BYTES_V7X = 64 * 2**20
MESH = pl.DeviceIdType.MESH


def _vmem_limit(block_bytes, scratch_bytes=0):
    need = 2 * block_bytes + scratch_bytes + 12 * 2**20
    return int(min(need, VMEM_BYTES_V7X - 6 * 2**20))


def _nbytes(shape, dtype):
    return math.prod(shape) * jnp.dtype(dtype).itemsize


def _tile(dim, target):
    best = None
    for t in range(LANES, min(dim, target) + 1, LANES):
        if dim % t == 0:
            best = t
    assert best is not None, (dim, target)
    return best


def _row_tile(rows, target, step=8):
    if rows <= target:
        return rows
    best = None
    for t in range(step, target + 1, step):
        if rows % t == 0:
            best = t
    assert best is not None, (rows, target)
    return best


def _mm(name, a, b, mode, out_dtype, res=None, alpha=1.0, tm_target=512, tn_target=1536, tk_target=2048):
    a3 = a.ndim == 3
    b3 = b.ndim == 3
    if mode == "nn":
        assert not a3 and not b3
        (M, K), (K2, N) = a.shape, b.shape
    elif mode == "nt":
        assert not b3
        if a3:
            P, M, Kp = a.shape
            K = P * Kp
        else:
            M, K = a.shape
        N, K2 = b.shape
    else:
        assert mode == "tn" and not a3
        K, M = a.shape
        if b3:
            P, K2, Np = b.shape
            N = P * Np
        else:
            K2, N = b.shape
    assert K == K2, (name, a.shape, b.shape)
    tm = _tile(M, tm_target)
    tn = _tile(Np if b3 else N, tn_target)
    tk = _tile(Kp if a3 else K, tk_target)
    nk = K // tk
    grid = (M // tm, N // tn, nk)

    if mode == "nn":
        a_spec = pl.BlockSpec((tm, tk), lambda i, j, k: (i, k))
        b_spec = pl.BlockSpec((tk, tn), lambda i, j, k: (k, j))
        dims = (((1,), (0,)), ((), ()))
    elif mode == "nt":
        if a3:
            nkp = Kp // tk
            a_spec = pl.BlockSpec((None, tm, tk), lambda i, j, k: (k // nkp, i, k % nkp))
        else:
            a_spec = pl.BlockSpec((tm, tk), lambda i, j, k: (i, k))
        b_spec = pl.BlockSpec((tn, tk), lambda i, j, k: (j, k))
        dims = (((1,), (1,)), ((), ()))
    else:
        a_spec = pl.BlockSpec((tk, tm), lambda i, j, k: (k, i))
        if b3:
            njp = Np // tn
            b_spec = pl.BlockSpec((None, tk, tn), lambda i, j, k: (j // njp, k, j % njp))
        else:
            b_spec = pl.BlockSpec((tk, tn), lambda i, j, k: (k, j))
        dims = (((0,), (0,)), ((), ()))
    o_spec = pl.BlockSpec((tm, tn), lambda i, j, k: (i, j))
    has_res = res is not None

    def finish(y, r_ref, o_ref):
        if alpha != 1.0:
            y = y * alpha
        if has_res:
            y = r_ref[...] + y
        o_ref[...] = y.astype(o_ref.dtype)

    def body(*refs):
        a_ref, b_ref = refs[:2]
        r_ref = refs[2] if has_res else None
        o_ref = refs[3] if has_res else refs[2]
        part = lax.dot_general(a_ref[...], b_ref[...], dims, preferred_element_type=F32)
        if nk == 1:
            finish(part, r_ref, o_ref)
            return
        acc_ref = refs[-1]
        k = pl.program_id(2)

        @pl.when(k == 0)
        def _():
            acc_ref[...] = part

        @pl.when(k > 0)
        def _():
            acc_ref[...] += part

        @pl.when(k == nk - 1)
        def _():
            finish(acc_ref[...], r_ref, o_ref)

    in_specs = [a_spec, b_spec] + ([o_spec] if has_res else [])
    args = (a, b) + ((res,) if has_res else ())
    blk = (_nbytes((tm, tk), a.dtype) + _nbytes((tk, tn), b.dtype) + _nbytes((tm, tn), out_dtype)
           + (_nbytes((tm, tn), F32) if has_res else 0))
    return pl.pallas_call(
        body, name=name, grid=grid, in_specs=in_specs, out_specs=o_spec,
        out_shape=jax.ShapeDtypeStruct((M, N), out_dtype),
        scratch_shapes=[pltpu.VMEM((tm, tn), F32)] if nk > 1 else [],
        compiler_params=pltpu.CompilerParams(
            dimension_semantics=("parallel", "parallel", "arbitrary"),
            vmem_limit_bytes=_vmem_limit(blk, 2 * _nbytes((tm, tn), F32))),
    )(*args)


def _rms_fwd(name, x, g):
    S, D = x.shape
    tr = _row_tile(S, 256)

    def body(x_ref, g_ref, n_ref):
        xv = x_ref[...]
        r = lax.rsqrt(jnp.mean(xv * xv, axis=-1, keepdims=True) + RMS_EPS)
        n_ref[...] = (xv * r * g_ref[...]).astype(BF16)

    return pl.pallas_call(
        body, name=name, grid=(S // tr,),
        in_specs=[pl.BlockSpec((tr, D), lambda i: (i, 0)), pl.BlockSpec((1, D), lambda i: (0, 0))],
        out_specs=pl.BlockSpec((tr, D), lambda i: (i, 0)),
        out_shape=jax.ShapeDtypeStruct((S, D), BF16),
        compiler_params=pltpu.CompilerParams(dimension_semantics=("parallel",)),
    )(x, g)


def _rms_bwd(name, dn, x, g, dres):
    S, D = x.shape
    tr = _row_tile(S, 256)

    def body(dn_ref, x_ref, g_ref, dres_ref, dx_ref, dxb_ref, dg_ref):
        i = pl.program_id(0)
        xv = x_ref[...]
        dnv = dn_ref[...]
        r = lax.rsqrt(jnp.mean(xv * xv, axis=-1, keepdims=True) + RMS_EPS)
        u = dnv * g_ref[...]
        dot = jnp.mean(u * xv, axis=-1, keepdims=True)
        dx = dres_ref[...] + r * u - xv * (r * r * r * dot)
        dx_ref[...] = dx
        dxb_ref[...] = dx.astype(BF16)

        @pl.when(i == 0)
        def _():
            dg_ref[...] = jnp.zeros_like(dg_ref)

        dg_ref[...] += jnp.sum(dnv * xv * r, axis=0, keepdims=True)

    row = pl.BlockSpec((tr, D), lambda i: (i, 0))
    vec = pl.BlockSpec((1, D), lambda i: (0, 0))
    return pl.pallas_call(
        body, name=name, grid=(S // tr,),
        in_specs=[row, row, vec, row], out_specs=[row, row, vec],
        out_shape=[jax.ShapeDtypeStruct((S, D), F32), jax.ShapeDtypeStruct((S, D), BF16),
                   jax.ShapeDtypeStruct((1, D), F32)],
        compiler_params=pltpu.CompilerParams(dimension_semantics=("arbitrary",)),
    )(dn, x, g, dres)


def _loss_head(name, x, g, target):
    S, D = x.shape
    tr = _row_tile(S, 256)

    def body(x_ref, g_ref, t_ref, dx_ref, dxb_ref, dg_ref, loss_ref):
        i = pl.program_id(0)
        xv = x_ref[...]
        gv = g_ref[...]
        r = lax.rsqrt(jnp.mean(xv * xv, axis=-1, keepdims=True) + RMS_EPS)
        diff = xv * r * gv - t_ref[...]
        part = 0.5 * jnp.sum(jnp.mean(diff * diff, axis=-1, keepdims=True), axis=0, keepdims=True)
        dy = diff * (1.0 / D)
        u = dy * gv
        dot = jnp.mean(u * xv, axis=-1, keepdims=True)
        dx = r * u - xv * (r * r * r * dot)
        dx_ref[...] = dx
        dxb_ref[...] = dx.astype(BF16)

        @pl.when(i == 0)
        def _():
            dg_ref[...] = jnp.zeros_like(dg_ref)
            loss_ref[...] = jnp.zeros_like(loss_ref)

        dg_ref[...] += jnp.sum(dy * xv * r, axis=0, keepdims=True)
        loss_ref[...] += jnp.broadcast_to(part, loss_ref.shape)

    row = pl.BlockSpec((tr, D), lambda i: (i, 0))
    vec = pl.BlockSpec((1, D), lambda i: (0, 0))
    lvec = pl.BlockSpec((1, LANES), lambda i: (0, 0))
    return pl.pallas_call(
        body, name=name, grid=(S // tr,),
        in_specs=[row, vec, row], out_specs=[row, row, vec, lvec],
        out_shape=[jax.ShapeDtypeStruct((S, D), F32), jax.ShapeDtypeStruct((S, D), BF16),
                   jax.ShapeDtypeStruct((1, D), F32), jax.ShapeDtypeStruct((1, LANES), F32)],
        compiler_params=pltpu.CompilerParams(dimension_semantics=("arbitrary",)),
    )(x, g, target)


def _swiglu_fwd(name, gu, fs):
    S, two_f = gu.shape
    nslab = two_f // (2 * fs)
    tr = _row_tile(S, 256)

    def body(gu_ref, h_ref):
        gv = gu_ref[:, :fs]
        uv = gu_ref[:, fs:]
        h_ref[...] = (gv * jax.nn.sigmoid(gv) * uv).astype(BF16)

    return pl.pallas_call(
        body, name=name, grid=(S // tr, nslab),
        in_specs=[pl.BlockSpec((tr, 2 * fs), lambda i, k: (i, k))],
        out_specs=pl.BlockSpec((tr, fs), lambda i, k: (i, k)),
        out_shape=jax.ShapeDtypeStruct((S, nslab * fs), BF16),
        compiler_params=pltpu.CompilerParams(dimension_semantics=("parallel", "parallel")),
    )(gu)


def _swiglu_bwd(name, dh, gu, fs):
    S, two_f = gu.shape
    nslab = two_f // (2 * fs)
    tr = _row_tile(S, 256)

    def body(dh_ref, gu_ref, o_ref):
        gv = gu_ref[:, :fs]
        uv = gu_ref[:, fs:]
        dhv = dh_ref[...]
        sg = jax.nn.sigmoid(gv)
        silu = gv * sg
        o_ref[:, :fs] = (dhv * uv * (sg + silu * (1.0 - sg))).astype(BF16)
        o_ref[:, fs:] = (dhv * silu).astype(BF16)

    return pl.pallas_call(
        body, name=name, grid=(S // tr, nslab),
        in_specs=[pl.BlockSpec((tr, fs), lambda i, k: (i, k)), pl.BlockSpec((tr, 2 * fs), lambda i, k: (i, k))],
        out_specs=pl.BlockSpec((tr, 2 * fs), lambda i, k: (i, k)),
        out_shape=jax.ShapeDtypeStruct((S, two_f), BF16),
        compiler_params=pltpu.CompilerParams(dimension_semantics=("parallel", "parallel")),
    )(dh, gu)


def _tri_rows(r0, nrows, ncols, lower):
    row = lax.broadcasted_iota(jnp.int32, (nrows, ncols), 0) + r0
    col = lax.broadcasted_iota(jnp.int32, (nrows, ncols), 1)
    return jnp.where((col <= row) if lower else (col >= row), 1.0, 0.0).astype(F32)


def _gate_fwd(name, hf, b):
    S = hf.shape[0]
    tb = _row_tile(S, 256)

    def body(hf_ref, b_ref, cf_ref, cft_ref, lf_ref):
        zz = hf_ref[...] + b_ref[...]
        lf_ref[...] = jnp.minimum(zz, 0.0) - jnp.log1p(jnp.exp(-jnp.abs(zz)))

        def blk(i, c):
            r0 = pl.multiple_of(i * tb, tb)
            tri = _tri_rows(r0, tb, S, True)
            cf_ref[pl.ds(r0, tb), :] = jnp.dot(tri, lf_ref[...], precision=lax.Precision.HIGHEST,
                                               preferred_element_type=F32)
            return c

        lax.fori_loop(0, S // tb, blk, 0)
        cft_ref[...] = cf_ref[...].T

    full = pl.BlockSpec((S, LANES), lambda: (0, 0))
    return pl.pallas_call(
        body, name=name, in_specs=[full, pl.BlockSpec((1, LANES), lambda: (0, 0))],
        out_specs=[full, pl.BlockSpec((LANES, S), lambda: (0, 0))],
        out_shape=[jax.ShapeDtypeStruct((S, LANES), F32), jax.ShapeDtypeStruct((LANES, S), F32)],
        scratch_shapes=[pltpu.VMEM((S, LANES), F32)],
    )(hf, b)


def _gate_bwd(name, dcft, drow, hf, b):
    S = hf.shape[0]
    tb = _row_tile(S, 256)

    def body(dcft_ref, drow_ref, hf_ref, b_ref, dhf_ref, db_ref, dcf_ref, dlf_ref):
        dcf_ref[...] = dcft_ref[...].T + drow_ref[...]

        def blk(i, c):
            r0 = pl.multiple_of(i * tb, tb)
            tri = _tri_rows(r0, tb, S, False)
            dlf_ref[pl.ds(r0, tb), :] = jnp.dot(tri, dcf_ref[...], precision=lax.Precision.HIGHEST,
                                                preferred_element_type=F32)
            return c

        lax.fori_loop(0, S // tb, blk, 0)
        zz = hf_ref[...] + b_ref[...]
        dhf = dlf_ref[...] * jax.nn.sigmoid(-zz)
        dhf_ref[...] = dhf.astype(BF16)
        db_ref[...] = jnp.sum(dhf, axis=0, keepdims=True)

    full = pl.BlockSpec((S, LANES), lambda: (0, 0))
    vec = pl.BlockSpec((1, LANES), lambda: (0, 0))
    return pl.pallas_call(
        body, name=name, in_specs=[pl.BlockSpec((LANES, S), lambda: (0, 0)), full, full, vec],
        out_specs=[full, vec],
        out_shape=[jax.ShapeDtypeStruct((S, LANES), BF16), jax.ShapeDtypeStruct((1, LANES), F32)],
        scratch_shapes=[pltpu.VMEM((S, LANES), F32), pltpu.VMEM((S, LANES), F32)],
    )(dcft, drow, hf, b)


def _rope_tables(S):
    half = ROPE_DIMS // 2
    freqs = ROPE_THETA ** (-jnp.arange(half, dtype=F32) / half)
    ang = jnp.arange(S, dtype=F32)[:, None] * freqs[None, :]
    cos, sin = jnp.cos(ang), jnp.sin(ang)
    pad = HEAD_DIM - ROPE_DIMS
    c = jnp.concatenate([cos, cos, jnp.ones((S, pad), F32)], axis=1)
    s = jnp.concatenate([-sin, sin, jnp.zeros((S, pad), F32)], axis=1)
    return c, s


def _rope_swap(x):
    half = ROPE_DIMS // 2
    lane = lax.broadcasted_iota(jnp.int32, x.shape, 1)
    upper = jnp.where(lane < ROPE_DIMS, pltpu.roll(x, half, 1), 0.0)
    return jnp.where(lane < half, pltpu.roll(x, HEAD_DIM - half, 1), upper)


def _rope(x, c, s):
    return x * c + _rope_swap(x) * s


def _rope_t(dy, c, s):
    return dy * c + _rope_swap(dy * s)


def _split_dot(x, t):
    hi = x.astype(BF16)
    lo = (x - hi.astype(F32)).astype(BF16)
    return (jnp.dot(hi, t, preferred_element_type=F32) + jnp.dot(lo, t, preferred_element_type=F32))


_NT = (((1,), (1,)), ((), ()))
_TN = (((0,), (0,)), ((), ()))


def _dot_nt(a, b):
    return lax.dot_general(a, b, _NT, preferred_element_type=F32)


def _dot_tn(a, b):
    return lax.dot_general(a, b, _TN, preferred_element_type=F32)


def _blk(i):
    return pl.ds(pl.multiple_of(i * BLK, BLK), BLK)


def _delta(i, j):
    row = lax.broadcasted_iota(jnp.int32, (BLK, BLK), 0)
    col = lax.broadcasted_iota(jnp.int32, (BLK, BLK), 1)
    return (row - col) + (i - j) * BLK


def _dilated_mult(delta):
    c = jnp.zeros(delta.shape, F32)
    for window, dil in DILATED_PATTERNS:
        ok = (delta >= 0) & (delta <= window) & ((delta & (dil - 1)) == 0)
        c = c + jnp.where(ok, 1.0, 0.0)
    return c


def _sb_terms(z, mask, t_ex, run):
    t = jnp.log1p(jnp.exp(-jnp.abs(z)))
    lsig = jnp.minimum(z, 0.0) - t
    m = jnp.where(mask, -(jnp.maximum(z, 0.0) + t), 0.0)
    after = _split_dot(m, t_ex)
    a = jnp.where(mask, jnp.exp(lsig + after + run), 0.0)
    return a, m, lsig


def _attn_fwd(name, hq, layer_kind, n_heads, n_sb, cf=None, cft=None, rope_c=None, rope_s=None):
    S = hq.shape[0]
    D = n_heads * HEAD_DIM
    nq = S // BLK
    scale = HEAD_DIM ** -0.5
    even = layer_kind == "even"

    def body(*refs):
        if even:
            q_ref, k_ref, v_ref, cf_ref, cft_ref, o_ref, ob_ref, lse_ref, qs, ks, vs = refs
        else:
            q_ref, k_ref, v_ref, c_ref, s_ref, o_ref, ob_ref, lse_ref, qs, ks, vs = refs
        h = pl.program_id(0)
        if even:
            qs[...] = q_ref[...].astype(BF16)
            ks[...] = k_ref[...].astype(BF16)
        else:
            qs[...] = _rope(q_ref[...], c_ref[...], s_ref[...]).astype(BF16)
            ks[...] = _rope(k_ref[...], c_ref[...], s_ref[...]).astype(BF16)
        vs[...] = v_ref[...].astype(BF16)

        def softmax_head(hh):
            def qblock(i, carry):
                qi = qs[_blk(i), :]
                if even:
                    lane = lax.broadcasted_iota(jnp.int32, (BLK, LANES), 1)
                    cfq = jnp.sum(jnp.where(lane == hh, cf_ref[_blk(i), :], 0.0), axis=1, keepdims=True)

                def kblock(j, c):
                    m_run, l_run, acc = c
                    z = _dot_nt(qi, ks[_blk(j), :]) * scale
                    delta = _delta(i, j)
                    if even:
                        z = z + cfq - cft_ref[hh, :, _blk(j)]
                        ok = delta >= 0
                    else:
                        mult = _dilated_mult(delta)
                        ok = mult > 0.0
                    z = jnp.where(ok, z, NEG_INF)
                    m_new = jnp.maximum(m_run, jnp.max(z, axis=1, keepdims=True))
                    p = jnp.exp(z - m_new)
                    if not even:
                        p = p * mult
                    alpha = jnp.exp(m_run - m_new)
                    l_new = alpha * l_run + jnp.sum(p, axis=1, keepdims=True)
                    acc = alpha * acc + jnp.dot(p.astype(BF16), vs[_blk(j), :], preferred_element_type=F32)
                    return m_new, l_new, acc

                init = (jnp.full((BLK, 1), NEG_INF, F32), jnp.zeros((BLK, 1), F32), jnp.zeros((BLK, HEAD_DIM), F32))
                m_run, l_run, acc = lax.fori_loop(0, i + 1, kblock, init)
                o = acc / l_run
                o_ref[_blk(i), :] = o
                ob_ref[_blk(i), :] = o.astype(BF16)
                lse_ref[_blk(i), :] = jnp.broadcast_to(m_run + jnp.log(l_run), (BLK, HEAD_DIM))
                return carry

            lax.fori_loop(0, nq, qblock, 0)

        def sb_head():
            row = lax.broadcasted_iota(jnp.int32, (BLK, BLK), 0)
            col = lax.broadcasted_iota(jnp.int32, (BLK, BLK), 1)
            t_ex = jnp.where(row > col, 1.0, 0.0).astype(BF16)

            def qblock(i, carry):
                qi = qs[_blk(i), :]

                def kblock(jj, c):
                    run, acc = c
                    j = i - jj
                    z = _dot_nt(qi, ks[_blk(j), :]) * scale
                    a, m, _ = _sb_terms(z, _delta(i, j) > 0, t_ex, run)
                    acc = acc + jnp.dot(a.astype(BF16), vs[_blk(j), :], preferred_element_type=F32)
                    return run + jnp.sum(m, axis=1, keepdims=True), acc

                init = (jnp.zeros((BLK, 1), F32), jnp.zeros((BLK, HEAD_DIM), F32))
                _, acc = lax.fori_loop(0, i + 1, kblock, init)
                o_ref[_blk(i), :] = acc
                ob_ref[_blk(i), :] = acc.astype(BF16)
                lse_ref[_blk(i), :] = jnp.zeros((BLK, HEAD_DIM), F32)
                return carry

            lax.fori_loop(0, nq, qblock, 0)

        if even:
            @pl.when(h < n_sb)
            def _():
                sb_head()

            @pl.when(h >= n_sb)
            def _():
                softmax_head(h - n_sb)
        else:
            softmax_head(h)

    head = lambda off: pl.BlockSpec((S, HEAD_DIM), lambda h, off=off: (0, off + h))
    full = pl.BlockSpec((S, LANES), lambda h: (0, 0))
    if even:
        extra_specs = [full, pl.BlockSpec(cft.shape, lambda h: (0, 0, 0))]
        extra = (cf, cft)
    else:
        extra_specs = [full, full]
        extra = (rope_c, rope_s)
    blk_bytes = 8 * _nbytes((S, HEAD_DIM), F32)
    return pl.pallas_call(
        body, name=name, grid=(n_heads,),
        in_specs=[head(0), head(n_heads), head(2 * n_heads)] + extra_specs,
        out_specs=[head(0), head(0), head(0)],
        out_shape=[jax.ShapeDtypeStruct((S, D), F32), jax.ShapeDtypeStruct((S, D), BF16),
                   jax.ShapeDtypeStruct((S, D), F32)],
        scratch_shapes=[pltpu.VMEM((S, HEAD_DIM), BF16)] * 3,
        compiler_params=pltpu.CompilerParams(dimension_semantics=("arbitrary",),
                                             vmem_limit_bytes=_vmem_limit(blk_bytes, 3 * _nbytes((S, HEAD_DIM), BF16))),
    )(hq, hq, hq, *extra)


def _attn_bwd(name, hq, do, o, lse, layer_kind, n_heads, n_sb, cf=None, cft=None, rope_c=None, rope_s=None):
    S = hq.shape[0]
    D = n_heads * HEAD_DIM
    nq = S // BLK
    scale = HEAD_DIM ** -0.5
    even = layer_kind == "even"

    def body(*refs):
        if even:
            (q_ref, k_ref, v_ref, do_ref, o_ref, lse_ref, cf_ref, cft_ref,
             dh_ref, dcft_ref, drow_ref, qs, ks, vs, dos, dq_acc, dk_acc, dv_acc) = refs
        else:
            (q_ref, k_ref, v_ref, do_ref, o_ref, lse_ref, c_ref, s_ref,
             dh_ref, qs, ks, vs, dos, dq_acc, dk_acc, dv_acc) = refs
        h = pl.program_id(0)
        if even:
            qs[...] = q_ref[...].astype(BF16)
            ks[...] = k_ref[...].astype(BF16)

            @pl.when(h == 0)
            def _():
                dcft_ref[...] = jnp.zeros_like(dcft_ref)
                drow_ref[...] = jnp.zeros_like(drow_ref)
        else:
            qs[...] = _rope(q_ref[...], c_ref[...], s_ref[...]).astype(BF16)
            ks[...] = _rope(k_ref[...], c_ref[...], s_ref[...]).astype(BF16)
        vs[...] = v_ref[...].astype(BF16)
        dos[...] = do_ref[...].astype(BF16)
        dk_acc[...] = jnp.zeros_like(dk_acc)
        dv_acc[...] = jnp.zeros_like(dv_acc)

        def softmax_head(hh):
            def qblock(i, carry):
                qi = qs[_blk(i), :]
                doi = dos[_blk(i), :]
                dvec = jnp.sum(do_ref[_blk(i), :] * o_ref[_blk(i), :], axis=1, keepdims=True)
                lse_i = jnp.max(lse_ref[_blk(i), :], axis=1, keepdims=True)
                if even:
                    lane = lax.broadcasted_iota(jnp.int32, (BLK, LANES), 1)
                    cfq = jnp.sum(jnp.where(lane == hh, cf_ref[_blk(i), :], 0.0), axis=1, keepdims=True)

                def kblock(j, c):
                    dq, ds_rows = c
                    kj = ks[_blk(j), :]
                    z = _dot_nt(qi, kj) * scale
                    delta = _delta(i, j)
                    if even:
                        z = z + cfq - cft_ref[hh, :, _blk(j)]
                        ok = delta >= 0
                    else:
                        mult = _dilated_mult(delta)
                        ok = mult > 0.0
                    p = jnp.exp(jnp.where(ok, z, NEG_INF) - lse_i)
                    if not even:
                        p = p * mult
                    dp = _dot_nt(doi, vs[_blk(j), :])
                    ds = p * (dp - dvec)
                    dsb = (ds * scale).astype(BF16)
                    dk_acc[_blk(j), :] += _dot_tn(dsb, qi)
                    dv_acc[_blk(j), :] += _dot_tn(p.astype(BF16), doi)
                    if even:
                        dcft_ref[hh, :, _blk(j)] += -jnp.sum(ds, axis=0, keepdims=True)
                    return (dq + jnp.dot(dsb, kj, preferred_element_type=F32),
                            ds_rows + jnp.sum(ds, axis=1, keepdims=True))

                dq, ds_rows = lax.fori_loop(0, i + 1, kblock,
                                            (jnp.zeros((BLK, HEAD_DIM), F32), jnp.zeros((BLK, 1), F32)))
                dq_acc[_blk(i), :] = dq
                if even:
                    drow_ref[_blk(i), :] += jnp.where(lane == hh, ds_rows, 0.0)
                return carry

            lax.fori_loop(0, nq, qblock, 0)

        def sb_head():
            row = lax.broadcasted_iota(jnp.int32, (BLK, BLK), 0)
            col = lax.broadcasted_iota(jnp.int32, (BLK, BLK), 1)
            t_ex = jnp.where(row > col, 1.0, 0.0).astype(BF16)
            t_in = jnp.where(row >= col, 1.0, 0.0).astype(BF16)

            def qblock(i, carry):
                qi = qs[_blk(i), :]
                doi = dos[_blk(i), :]

                def e_total(jj, c):
                    run, tot = c
                    j = i - jj
                    z = _dot_nt(qi, ks[_blk(j), :]) * scale
                    a, m, _ = _sb_terms(z, _delta(i, j) > 0, t_ex, run)
                    e = _dot_nt(doi, vs[_blk(j), :]) * a
                    return run + jnp.sum(m, axis=1, keepdims=True), tot + jnp.sum(e, axis=1, keepdims=True)

                zero = jnp.zeros((BLK, 1), F32)
                _, e_tot = lax.fori_loop(0, i + 1, e_total, (zero, zero))

                def kblock(jj, c):
                    run, e_run, dq = c
                    j = i - jj
                    kj = ks[_blk(j), :]
                    z = _dot_nt(qi, kj) * scale
                    mask = _delta(i, j) > 0
                    a, m, lsig = _sb_terms(z, mask, t_ex, run)
                    sig = jnp.exp(lsig)
                    e = _dot_nt(doi, vs[_blk(j), :]) * a
                    e_before = e_tot - (_split_dot(e, t_in) + e_run)
                    dz = jnp.where(mask, e * (1.0 - sig) - sig * e_before, 0.0)
                    dzb = (dz * scale).astype(BF16)
                    dk_acc[_blk(j), :] += _dot_tn(dzb, qi)
                    dv_acc[_blk(j), :] += _dot_tn(a.astype(BF16), doi)
                    return (run + jnp.sum(m, axis=1, keepdims=True), e_run + jnp.sum(e, axis=1, keepdims=True),
                            dq + jnp.dot(dzb, kj, preferred_element_type=F32))

                _, _, dq = lax.fori_loop(0, i + 1, kblock, (zero, zero, jnp.zeros((BLK, HEAD_DIM), F32)))
                dq_acc[_blk(i), :] = dq
                return carry

            lax.fori_loop(0, nq, qblock, 0)

        if even:
            @pl.when(h < n_sb)
            def _():
                sb_head()

            @pl.when(h >= n_sb)
            def _():
                softmax_head(h - n_sb)

            dh_ref[0] = dq_acc[...].astype(BF16)
            dh_ref[1] = dk_acc[...].astype(BF16)
        else:
            softmax_head(h)
            dh_ref[0] = _rope_t(dq_acc[...], c_ref[...], s_ref[...]).astype(BF16)
            dh_ref[1] = _rope_t(dk_acc[...], c_ref[...], s_ref[...]).astype(BF16)
        dh_ref[2] = dv_acc[...].astype(BF16)

    head = lambda off: pl.BlockSpec((S, HEAD_DIM), lambda h, off=off: (0, off + h))
    full = pl.BlockSpec((S, LANES), lambda h: (0, 0))
    tfull = pl.BlockSpec((n_heads - n_sb, 1, S), lambda h: (0, 0, 0))
    dh_spec = pl.BlockSpec((3, S, HEAD_DIM), lambda h: (0, 0, h))
    dh_shape = jax.ShapeDtypeStruct((3, S, D), BF16)
    if even:
        extra_specs, extra = [full, tfull], (cf, cft)
        out_specs = [dh_spec, tfull, full]
        out_shape = [dh_shape, jax.ShapeDtypeStruct((n_heads - n_sb, 1, S), F32),
                     jax.ShapeDtypeStruct((S, LANES), F32)]
    else:
        extra_specs, extra = [full, full], (rope_c, rope_s)
        out_specs = [dh_spec]
        out_shape = [dh_shape]
    blk_bytes = 10 * _nbytes((S, HEAD_DIM), F32)
    scratch_bytes = 4 * _nbytes((S, HEAD_DIM), BF16) + 3 * _nbytes((S, HEAD_DIM), F32)
    return pl.pallas_call(
        body, name=name, grid=(n_heads,),
        in_specs=[head(0), head(n_heads), head(2 * n_heads), head(0), head(0), head(0)] + extra_specs,
        out_specs=out_specs, out_shape=out_shape,
        scratch_shapes=[pltpu.VMEM((S, HEAD_DIM), BF16)] * 4 + [pltpu.VMEM((S, HEAD_DIM), F32)] * 3,
        compiler_params=pltpu.CompilerParams(dimension_semantics=("arbitrary",),
                                             vmem_limit_bytes=_vmem_limit(blk_bytes, scratch_bytes)),
    )(hq, hq, hq, do, o, lse, *extra)


def _query_block(S):
    return min(512, S)


def _offsets(d, bq):
    row = jnp.arange(bq, dtype=jnp.int32)[:, None]
    col = jnp.arange(BLK, dtype=jnp.int32)[None, :]
    return d * BLK + row - col


def _causal_tables(bq, strict):
    r = bq // BLK
    tabs = []
    for d in range(-(r - 1), 1):
        delta = _offsets(d, bq)
        tabs.append(jnp.where((delta > 0) if strict else (delta >= 0), 1.0, 0.0))
    tabs.append(jnp.ones((bq, BLK), F32))
    return jnp.stack(tabs).astype(F32)


def _dilated_tables(bq):
    r = bq // BLK
    limit = sorted(w for w, _ in DILATED_PATTERNS)[-2]
    assert all(BLK % dil == 0 for _, dil in DILATED_PATTERNS)
    d_far = -(-(limit + BLK) // BLK)
    tabs = []
    for d in range(-(r - 1), d_far + 1):
        mult = _dilated_mult(_offsets(d, bq))
        tabs.append(jnp.where(mult > 0, jnp.log(jnp.maximum(mult, 1.0)), NEG_INF))
    return jnp.stack(tabs).astype(F32)


def _qblk(i, bq):
    return pl.ds(pl.multiple_of(i * bq, bq), bq)


def _sb_block(z, valid, t_ex, run):
    t = jnp.log1p(jnp.exp(-jnp.abs(z)))
    lsig = jnp.minimum(z, 0.0) - t
    m = -(jnp.maximum(z, 0.0) + t) * valid
    after = _split_dot(m, t_ex)
    a = jnp.exp(lsig + after + run) * valid
    return a, m, lsig


def _attn_fwd_wide(name, hq, layer_kind, n_heads, n_sb, cf=None, cft=None, rope_c=None, rope_s=None):
    S = hq.shape[0]
    D = n_heads * HEAD_DIM
    bq = _query_block(S)
    r = bq // BLK
    nq = S // bq
    scale = HEAD_DIM ** -0.5
    even = layer_kind == "even"
    if even:
        tabs = (jnp.where(_causal_tables(bq, False) > 0, 0.0, NEG_INF), _causal_tables(bq, True))
    else:
        tabs = (_dilated_tables(bq),)
    n_tab = tabs[0].shape[0]

    def body(*refs):
        if even:
            q_ref, k_ref, v_ref, cf_ref, cft_ref, bias_ref, valid_ref, o_ref, ob_ref, lse_ref, qs, ks, vs = refs
        else:
            q_ref, k_ref, v_ref, c_ref, s_ref, bias_ref, o_ref, ob_ref, lse_ref, qs, ks, vs = refs
        h = pl.program_id(0)
        if even:
            qs[...] = q_ref[...].astype(BF16)
            ks[...] = k_ref[...].astype(BF16)
        else:
            qs[...] = _rope(q_ref[...], c_ref[...], s_ref[...]).astype(BF16)
            ks[...] = _rope(k_ref[...], c_ref[...], s_ref[...]).astype(BF16)
        vs[...] = v_ref[...].astype(BF16)

        def softmax_head(hh):
            def qblock(i, carry):
                qi = qs[_qblk(i, bq), :]
                if even:
                    lane = lax.broadcasted_iota(jnp.int32, (bq, LANES), 1)
                    cfq = jnp.sum(jnp.where(lane == hh, cf_ref[_qblk(i, bq), :], 0.0), axis=1, keepdims=True)

                def kblock(j, c):
                    m_run, l_run, acc = c
                    z = _dot_nt(qi, ks[_blk(j), :]) * scale + bias_ref[jnp.minimum(r * i - j + (r - 1), n_tab - 1)]
                    if even:
                        z = z + (cfq - cft_ref[hh, :, _blk(j)])
                    m_new = jnp.maximum(m_run, jnp.max(z, axis=1, keepdims=True))
                    p = jnp.exp(z - m_new)
                    alpha = jnp.exp(m_run - m_new)
                    l_new = alpha * l_run + jnp.sum(p, axis=1, keepdims=True)
                    acc = alpha * acc + jnp.dot(p.astype(BF16), vs[_blk(j), :], preferred_element_type=F32)
                    return m_new, l_new, acc

                init = (jnp.full((bq, 1), NEG_INF, F32), jnp.zeros((bq, 1), F32), jnp.zeros((bq, HEAD_DIM), F32))
                m_run, l_run, acc = lax.fori_loop(0, r * (i + 1), kblock, init)
                o = acc / l_run
                o_ref[_qblk(i, bq), :] = o
                ob_ref[_qblk(i, bq), :] = o.astype(BF16)
                lse_ref[_qblk(i, bq), :] = jnp.broadcast_to(m_run + jnp.log(l_run), (bq, HEAD_DIM))
                return carry

            lax.fori_loop(0, nq, qblock, 0)

        def sb_head():
            row = lax.broadcasted_iota(jnp.int32, (BLK, BLK), 0)
            col = lax.broadcasted_iota(jnp.int32, (BLK, BLK), 1)
            t_ex = jnp.where(row > col, 1.0, 0.0).astype(BF16)

            def qblock(i, carry):
                qi = qs[_qblk(i, bq), :]

                def kblock(jj, c):
                    run, acc = c
                    j = r * (i + 1) - 1 - jj
                    z = _dot_nt(qi, ks[_blk(j), :]) * scale
                    a, m, _ = _sb_block(z, valid_ref[jnp.minimum(r * i - j + (r - 1), r)], t_ex, run)
                    acc = acc + jnp.dot(a.astype(BF16), vs[_blk(j), :], preferred_element_type=F32)
                    return run + jnp.sum(m, axis=1, keepdims=True), acc

                init = (jnp.zeros((bq, 1), F32), jnp.zeros((bq, HEAD_DIM), F32))
                _, acc = lax.fori_loop(0, r * (i + 1), kblock, init)
                o_ref[_qblk(i, bq), :] = acc
                ob_ref[_qblk(i, bq), :] = acc.astype(BF16)
                lse_ref[_qblk(i, bq), :] = jnp.zeros((bq, HEAD_DIM), F32)
                return carry

            lax.fori_loop(0, nq, qblock, 0)

        if even:
            @pl.when(h < n_sb)
            def _():
                sb_head()

            @pl.when(h >= n_sb)
            def _():
                softmax_head(h - n_sb)
        else:
            softmax_head(h)

    head = lambda off: pl.BlockSpec((S, HEAD_DIM), lambda h, off=off: (0, off + h))
    full = pl.BlockSpec((S, LANES), lambda h: (0, 0))
    tab_specs = [pl.BlockSpec(t.shape, lambda h: (0, 0, 0)) for t in tabs]
    if even:
        extra_specs = [full, pl.BlockSpec(cft.shape, lambda h: (0, 0, 0))] + tab_specs
        extra = (cf, cft) + tabs
    else:
        extra_specs = [full, full] + tab_specs
        extra = (rope_c, rope_s) + tabs
    blk_bytes = 8 * _nbytes((S, HEAD_DIM), F32) + sum(_nbytes(t.shape, F32) for t in tabs)
    return pl.pallas_call(
        body, name=name, grid=(n_heads,),
        in_specs=[head(0), head(n_heads), head(2 * n_heads)] + extra_specs,
        out_specs=[head(0), head(0), head(0)],
        out_shape=[jax.ShapeDtypeStruct((S, D), F32), jax.ShapeDtypeStruct((S, D), BF16),
                   jax.ShapeDtypeStruct((S, D), F32)],
        scratch_shapes=[pltpu.VMEM((S, HEAD_DIM), BF16)] * 3,
        compiler_params=pltpu.CompilerParams(dimension_semantics=("arbitrary",),
                                             vmem_limit_bytes=_vmem_limit(blk_bytes, 3 * _nbytes((S, HEAD_DIM), BF16))),
    )(hq, hq, hq, *extra)


def _attn_bwd_wide(name, hq, do, o, lse, layer_kind, n_heads, n_sb, cf=None, cft=None, rope_c=None, rope_s=None):
    S = hq.shape[0]
    D = n_heads * HEAD_DIM
    bq = _query_block(S)
    r = bq // BLK
    nq = S // bq
    scale = HEAD_DIM ** -0.5
    even = layer_kind == "even"
    if even:
        tabs = (jnp.where(_causal_tables(bq, False) > 0, 0.0, NEG_INF), _causal_tables(bq, True))
    else:
        tabs = (_dilated_tables(bq),)
    n_tab = tabs[0].shape[0]

    def body(*refs):
        if even:
            (q_ref, k_ref, v_ref, do_ref, o_ref, lse_ref, cf_ref, cft_ref, bias_ref, valid_ref,
             dh_ref, dcft_ref, drow_ref, qs, ks, vs, dos, dq_acc, dk_acc, dv_acc) = refs
        else:
            (q_ref, k_ref, v_ref, do_ref, o_ref, lse_ref, c_ref, s_ref, bias_ref,
             dh_ref, qs, ks, vs, dos, dq_acc, dk_acc, dv_acc) = refs
        h = pl.program_id(0)
        if even:
            qs[...] = q_ref[...].astype(BF16)
            ks[...] = k_ref[...].astype(BF16)

            @pl.when(h == 0)
            def _():
                dcft_ref[...] = jnp.zeros_like(dcft_ref)
                drow_ref[...] = jnp.zeros_like(drow_ref)
        else:
            qs[...] = _rope(q_ref[...], c_ref[...], s_ref[...]).astype(BF16)
            ks[...] = _rope(k_ref[...], c_ref[...], s_ref[...]).astype(BF16)
        vs[...] = v_ref[...].astype(BF16)
        dos[...] = do_ref[...].astype(BF16)
        dk_acc[...] = jnp.zeros_like(dk_acc)
        dv_acc[...] = jnp.zeros_like(dv_acc)

        def softmax_head(hh):
            def qblock(i, carry):
                qi = qs[_qblk(i, bq), :]
                doi = dos[_qblk(i, bq), :]
                dvec = jnp.sum(do_ref[_qblk(i, bq), :] * o_ref[_qblk(i, bq), :], axis=1, keepdims=True)
                lse_i = jnp.max(lse_ref[_qblk(i, bq), :], axis=1, keepdims=True)
                if even:
                    lane = lax.broadcasted_iota(jnp.int32, (bq, LANES), 1)
                    cfq = jnp.sum(jnp.where(lane == hh, cf_ref[_qblk(i, bq), :], 0.0), axis=1, keepdims=True)

                def kblock(j, c):
                    dq, ds_rows = c
                    kj = ks[_blk(j), :]
                    z = _dot_nt(qi, kj) * scale + bias_ref[jnp.minimum(r * i - j + (r - 1), n_tab - 1)]
                    if even:
                        z = z + (cfq - cft_ref[hh, :, _blk(j)])
                    p = jnp.exp(z - lse_i)
                    dp = _dot_nt(doi, vs[_blk(j), :])
                    ds = p * (dp - dvec)
                    dsb = (ds * scale).astype(BF16)
                    dk_acc[_blk(j), :] += _dot_tn(dsb, qi)
                    dv_acc[_blk(j), :] += _dot_tn(p.astype(BF16), doi)
                    if even:
                        dcft_ref[hh, :, _blk(j)] += -jnp.sum(ds, axis=0, keepdims=True)
                    return (dq + jnp.dot(dsb, kj, preferred_element_type=F32),
                            ds_rows + jnp.sum(ds, axis=1, keepdims=True))

                dq, ds_rows = lax.fori_loop(0, r * (i + 1), kblock,
                                            (jnp.zeros((bq, HEAD_DIM), F32), jnp.zeros((bq, 1), F32)))
                dq_acc[_qblk(i, bq), :] = dq
                if even:
                    drow_ref[_qblk(i, bq), :] += jnp.where(lane == hh, ds_rows, 0.0)
                return carry

            lax.fori_loop(0, nq, qblock, 0)

        def sb_head():
            row = lax.broadcasted_iota(jnp.int32, (BLK, BLK), 0)
            col = lax.broadcasted_iota(jnp.int32, (BLK, BLK), 1)
            t_ex = jnp.where(row > col, 1.0, 0.0).astype(BF16)
            t_in = jnp.where(row >= col, 1.0, 0.0).astype(BF16)

            def qblock(i, carry):
                qi = qs[_qblk(i, bq), :]
                doi = dos[_qblk(i, bq), :]
                nkb = r * (i + 1)

                def e_total(jj, c):
                    run, tot = c
                    j = nkb - 1 - jj
                    z = _dot_nt(qi, ks[_blk(j), :]) * scale
                    a, m, _ = _sb_block(z, valid_ref[jnp.minimum(r * i - j + (r - 1), r)], t_ex, run)
                    e = _dot_nt(doi, vs[_blk(j), :]) * a
                    return run + jnp.sum(m, axis=1, keepdims=True), tot + jnp.sum(e, axis=1, keepdims=True)

                zero = jnp.zeros((bq, 1), F32)
                _, e_tot = lax.fori_loop(0, nkb, e_total, (zero, zero))

                def kblock(jj, c):
                    run, e_run, dq = c
                    j = nkb - 1 - jj
                    kj = ks[_blk(j), :]
                    z = _dot_nt(qi, kj) * scale
                    valid = valid_ref[jnp.minimum(r * i - j + (r - 1), r)]
                    a, m, lsig = _sb_block(z, valid, t_ex, run)
                    sig = jnp.exp(lsig)
                    e = _dot_nt(doi, vs[_blk(j), :]) * a
                    e_before = e_tot - (_split_dot(e, t_in) + e_run)
                    dz = (e * (1.0 - sig) - sig * e_before) * valid
                    dzb = (dz * scale).astype(BF16)
                    dk_acc[_blk(j), :] += _dot_tn(dzb, qi)
                    dv_acc[_blk(j), :] += _dot_tn(a.astype(BF16), doi)
                    return (run + jnp.sum(m, axis=1, keepdims=True), e_run + jnp.sum(e, axis=1, keepdims=True),
                            dq + jnp.dot(dzb, kj, preferred_element_type=F32))

                _, _, dq = lax.fori_loop(0, nkb, kblock, (zero, zero, jnp.zeros((bq, HEAD_DIM), F32)))
                dq_acc[_qblk(i, bq), :] = dq
                return carry

            lax.fori_loop(0, nq, qblock, 0)

        if even:
            @pl.when(h < n_sb)
            def _():
                sb_head()

            @pl.when(h >= n_sb)
            def _():
                softmax_head(h - n_sb)

            dh_ref[0] = dq_acc[...].astype(BF16)
            dh_ref[1] = dk_acc[...].astype(BF16)
        else:
            softmax_head(h)
            dh_ref[0] = _rope_t(dq_acc[...], c_ref[...], s_ref[...]).astype(BF16)
            dh_ref[1] = _rope_t(dk_acc[...], c_ref[...], s_ref[...]).astype(BF16)
        dh_ref[2] = dv_acc[...].astype(BF16)

    head = lambda off: pl.BlockSpec((S, HEAD_DIM), lambda h, off=off: (0, off + h))
    full = pl.BlockSpec((S, LANES), lambda h: (0, 0))
    tfull = pl.BlockSpec((n_heads - n_sb, 1, S), lambda h: (0, 0, 0))
    tab_specs = [pl.BlockSpec(t.shape, lambda h: (0, 0, 0)) for t in tabs]
    dh_spec = pl.BlockSpec((3, S, HEAD_DIM), lambda h: (0, 0, h))
    dh_shape = jax.ShapeDtypeStruct((3, S, D), BF16)
    if even:
        extra_specs, extra = [full, tfull] + tab_specs, (cf, cft) + tabs
        out_specs = [dh_spec, tfull, full]
        out_shape = [dh_shape, jax.ShapeDtypeStruct((n_heads - n_sb, 1, S), F32),
                     jax.ShapeDtypeStruct((S, LANES), F32)]
    else:
        extra_specs, extra = [full, full] + tab_specs, (rope_c, rope_s) + tabs
        out_specs = [dh_spec]
        out_shape = [dh_shape]
    blk_bytes = 10 * _nbytes((S, HEAD_DIM), F32) + sum(_nbytes(t.shape, F32) for t in tabs)
    scratch_bytes = 4 * _nbytes((S, HEAD_DIM), BF16) + 3 * _nbytes((S, HEAD_DIM), F32)
    return pl.pallas_call(
        body, name=name, grid=(n_heads,),
        in_specs=[head(0), head(n_heads), head(2 * n_heads), head(0), head(0), head(0)] + extra_specs,
        out_specs=out_specs, out_shape=out_shape,
        scratch_shapes=[pltpu.VMEM((S, HEAD_DIM), BF16)] * 4 + [pltpu.VMEM((S, HEAD_DIM), F32)] * 3,
        compiler_params=pltpu.CompilerParams(dimension_semantics=("arbitrary",),
                                             vmem_limit_bytes=_vmem_limit(blk_bytes, scratch_bytes)),
    )(hq, hq, hq, do, o, lse, *extra)


def _ffn_fwd(tag, x, g, wgu, wd, fs):
    n = _rms_fwd(tag + "_norm", x, g)
    gu = _mm(tag + "_gu", n, wgu, "nn", F32)
    h = _swiglu_fwd(tag + "_act", gu, fs)
    y = _mm(tag + "_down", h, wd, "nn", F32, res=x, alpha=0.5)
    return y, (x, g, n, gu, h)


def _ffn_bwd(tag, dx, dxb, wgu, wd, fs, saved):
    x, g, n, gu, h = saved
    dh = _mm(tag + "_dh", dxb, wd, "nt", F32, alpha=0.5)
    dwd = _mm(tag + "_dwd", h, dxb, "tn", F32, alpha=0.5)
    dgu = _swiglu_bwd(tag + "_dact", dh, gu, fs)
    dwgu = _mm(tag + "_dwgu", n, dgu, "tn", F32)
    dn = _mm(tag + "_dn", dgu, wgu, "nt", F32)
    dx_in, dxb_in, dg = _rms_bwd(tag + "_dnorm", dn, x, g, dx)
    return dx_in, dxb_in, dg, dwgu, dwd


def _mixer_fwd(tag, kind, x, g, wqkv, wout, n_heads, n_sb, wf=None, bf=None, rope=None):
    n = _rms_fwd(tag + "_norm", x, g)
    hq = _mm(tag + "_qkv", n, wqkv, "nn", F32)
    if kind == "even":
        hf = _mm(tag + "_gate", n, wf, "nn", F32)
        cf, cft = _gate_fwd(tag + "_cumgate", hf, bf)
        cft = cft[:n_heads - n_sb].reshape(n_heads - n_sb, 1, -1)
        o, ob, lse = _attn_fwd_wide(tag + "_attn", hq, kind, n_heads, n_sb, cf=cf, cft=cft)
    else:
        hf = cf = cft = None
        o, ob, lse = _attn_fwd_wide(tag + "_attn", hq, kind, n_heads, n_sb, rope_c=rope[0], rope_s=rope[1])
    y = _mm(tag + "_out", ob, wout, "nn", F32, res=x)
    return y, (x, g, n, hq, hf, cf, cft, o, ob, lse)


def _mixer_bwd(tag, kind, dx, dxb, wqkv, wout, n_heads, n_sb, saved, wf=None, bf=None, rope=None):
    x, g, n, hq, hf, cf, cft, o, ob, lse = saved
    do = _mm(tag + "_do", dxb, wout, "nt", F32)
    dwout = _mm(tag + "_dwout", ob, dxb, "tn", F32)
    if kind == "even":
        dh3, dcft, drow = _attn_bwd_wide(tag + "_dattn", hq, do, o, lse, kind, n_heads, n_sb, cf=cf, cft=cft)
    else:
        (dh3,) = _attn_bwd_wide(tag + "_dattn", hq, do, o, lse, kind, n_heads, n_sb, rope_c=rope[0], rope_s=rope[1])
    dwqkv = _mm(tag + "_dwqkv", n, dh3, "tn", F32)
    dn = _mm(tag + "_dn", dh3, wqkv, "nt", F32)
    dwf = db = None
    if kind == "even":
        n_fox = n_heads - n_sb
        dcft = jnp.pad(dcft.reshape(n_fox, -1), ((0, LANES - n_fox), (0, 0)))
        dhf, db = _gate_bwd(tag + "_dcumgate", dcft, drow, hf, bf)
        dwf = _mm(tag + "_dwf", n, dhf, "tn", F32)
        dn = _mm(tag + "_dn_gate", dhf, wf, "nt", F32, res=dn)
    dx_in, dxb_in, dg = _rms_bwd(tag + "_dnorm", dn, x, g, dx)
    return dx_in, dxb_in, dg, dwqkv, dwout, dwf, db


def _local_step(x, target, w, fs, n_heads, n_sb):
    S, D = x.shape
    rope = _rope_tables(S)
    kinds = ("even", "odd")
    saved = []
    h = x
    if callable(w):
        fetch, w = w, {"norm_g": w("norm_g", None), "final_g": w("final_g", None),
                       "wgu1": [None, None], "wd1": [None, None], "wgu2": [None, None], "wd2": [None, None]}
    else:
        fetch = None
    for l, kind in enumerate(kinds):
        ng = [w["norm_g"][l, i][None, :] for i in range(3)]
        if fetch:
            w["wgu1"][l], w["wd1"][l] = fetch(("ffn1", l), h)
        h, s1 = _ffn_fwd(f"l{l}_ffn1", h, ng[0], w["wgu1"][l], w["wd1"][l], fs)
        if fetch:
            w.update(fetch(("mix", l), h))
        if kind == "even":
            h, s2 = _mixer_fwd(f"l{l}_mix", kind, h, ng[1], w["wqkv_e"], w["wout_e"], n_heads, n_sb,
                               wf=w["wf"], bf=w["bf"])
        else:
            h, s2 = _mixer_fwd(f"l{l}_mix", kind, h, ng[1], w["wqkv_o"], w["wout_o"], n_heads, n_sb, rope=rope)
        if fetch:
            w["wgu2"][l], w["wd2"][l] = fetch(("ffn2", l), h)
        h, s3 = _ffn_fwd(f"l{l}_ffn2", h, ng[2], w["wgu2"][l], w["wd2"][l], fs)
        saved.append((s1, s2, s3))

    dx, dxb, dfinal, loss = _loss_head("loss_head", h, w["final_g"], target)
    grads = {"dfinal": dfinal, "dnorm": [[None] * 3 for _ in kinds],
             "dwgu1": [None, None], "dwd1": [None, None], "dwgu2": [None, None], "dwd2": [None, None]}
    for l in (1, 0):
        kind = kinds[l]
        s1, s2, s3 = saved[l]
        dx, dxb, dg, grads["dwgu2"][l], grads["dwd2"][l] = _ffn_bwd(
            f"l{l}_ffn2", dx, dxb, w["wgu2"][l], w["wd2"][l], fs, s3)
        grads["dnorm"][l][2] = dg
        if kind == "even":
            dx, dxb, dg, grads["dwqkv_e"], grads["dwout_e"], grads["dwf"], grads["db"] = _mixer_bwd(
                f"l{l}_mix", kind, dx, dxb, w["wqkv_e"], w["wout_e"], n_heads, n_sb, s2, wf=w["wf"], bf=w["bf"])
        else:
            dx, dxb, dg, grads["dwqkv_o"], grads["dwout_o"], _, _ = _mixer_bwd(
                f"l{l}_mix", kind, dx, dxb, w["wqkv_o"], w["wout_o"], n_heads, n_sb, s2, rope=rope)
        grads["dnorm"][l][1] = dg
        dx, dxb, dg, grads["dwgu1"][l], grads["dwd1"][l] = _ffn_bwd(
            f"l{l}_ffn1", dx, dxb, w["wgu1"][l], w["wd1"][l], fs, s1)
        grads["dnorm"][l][0] = dg
    return loss, dx, grads


def _cast_into(name, shard, layer, chip, full_shape, place, full=None):
    R, C = shard.shape[-2:]
    tr = _row_tile(R, 512, step=16)
    if layer is None:
        in_spec = pl.BlockSpec((tr, C), lambda i, k: (i, 0))
    else:
        in_spec = pl.BlockSpec((None, tr, C), lambda i, k: (layer, i, 0))
    lead = (None,) * (len(full_shape) - 2)
    out_spec = pl.BlockSpec(lead + (tr, C), lambda i, k: place(i, k[0]))

    def body(*refs):
        k_ref, w_ref = refs[:2]
        o_ref = refs[-1]
        o_ref[...] = w_ref[...].astype(BF16)

    in_specs = [in_spec] + ([_ANY] if full is not None else [])
    args = (chip, shard) + ((full,) if full is not None else ())
    grid_spec = pltpu.PrefetchScalarGridSpec(num_scalar_prefetch=1, grid=(R // tr,), in_specs=in_specs, out_specs=out_spec)
    return pl.pallas_call(
        body, name=name, grid_spec=grid_spec, out_shape=jax.ShapeDtypeStruct(full_shape, BF16),
        input_output_aliases={2: 0} if full is not None else {},
        compiler_params=pltpu.CompilerParams(dimension_semantics=("arbitrary",)),
    )(*args)


def _region_shape(grad, kind):
    if kind == "lead":
        return grad.shape[1] // N_CORES, grad.shape[2]
    rows, cols = grad.shape
    if kind == "cols":
        return rows // N_CORES, cols // N_CHIPS
    return rows // (N_CHIPS * N_CORES), cols


def _region_add(name, grad, kind, landed, core):
    rh, cw = _region_shape(grad, kind)
    tr = _row_tile(rh, 256, step=16)
    nrb = rh // tr
    if kind == "cols":
        g_spec = pl.BlockSpec((tr, cw), lambda k, r, c: (c[0] * nrb + r, k))
    elif kind == "rows":
        g_spec = pl.BlockSpec((tr, cw), lambda k, r, c: ((N_CORES * k + c[0]) * nrb + r, 0))
    else:
        g_spec = pl.BlockSpec((None, tr, cw), lambda k, r, c: (k, c[0] * nrb + r, 0))
    l_spec = pl.BlockSpec((None, tr, cw), lambda k, r, c: (k, r, 0))

    def body(c_ref, g_ref, l_ref, o_ref):
        o_ref[...] = (g_ref[...] + l_ref[...]).astype(BF16)

    grid_spec = pltpu.PrefetchScalarGridSpec(
        num_scalar_prefetch=1, grid=(N_CHIPS, nrb), in_specs=[g_spec, l_spec], out_specs=l_spec)
    return pl.pallas_call(
        body, name=name, grid_spec=grid_spec, out_shape=jax.ShapeDtypeStruct(landed.shape, BF16),
        compiler_params=pltpu.CompilerParams(dimension_semantics=("parallel", "parallel"),
                                             vmem_limit_bytes=_vmem_limit(3 * _nbytes((tr, cw), F32))),
    )(core, grad, landed)


def _chip_sum(name, pair, landed, pos):
    _, rh, cw = pair.shape
    tr = _row_tile(rh, max(16, 2**20 // (cw * 4)), step=16)
    nrb = rh // tr

    def body(p_ref, own_ref, l_ref, o_ref):
        acc = own_ref[...].astype(F32)
        for s in range(N_CHIPS - 1):
            acc = acc + l_ref[s].astype(F32)
        o_ref[...] = acc

    grid_spec = pltpu.PrefetchScalarGridSpec(
        num_scalar_prefetch=1, grid=(nrb,),
        in_specs=[pl.BlockSpec((None, tr, cw), lambda r, p: (p[0], r, 0)),
                  pl.BlockSpec((N_CHIPS - 1, tr, cw), lambda r, p: (0, r, 0))],
        out_specs=pl.BlockSpec((tr, cw), lambda r, p: (p[1] * nrb + r, 0)))
    return pl.pallas_call(
        body, name=name, grid_spec=grid_spec, out_shape=jax.ShapeDtypeStruct((N_CORES * rh, cw), F32),
        compiler_params=pltpu.CompilerParams(dimension_semantics=("arbitrary",)),
    )(pos, pair, landed)


def _sum_leading(name, parts):
    n, R, C = parts.shape
    tr = _row_tile(R, max(8, (2**20 // (C * 4)) // 8 * 8))

    def body(p_ref, o_ref):
        acc = p_ref[0]
        for s in range(1, n):
            acc = acc + p_ref[s]
        o_ref[...] = acc

    return pl.pallas_call(
        body, name=name, grid=(R // tr,),
        in_specs=[pl.BlockSpec((n, tr, C), lambda i: (0, i, 0))],
        out_specs=pl.BlockSpec((tr, C), lambda i: (i, 0)),
        out_shape=jax.ShapeDtypeStruct((R, C), F32),
        compiler_params=pltpu.CompilerParams(dimension_semantics=("parallel",)),
    )(parts)


def _adamw(name, w, g, m, v):
    shape = w.shape
    to2d = lambda t: t.reshape(-1, shape[-1]) if t.ndim > 1 else t.reshape(1, -1)
    w2, g2, m2, v2 = (to2d(t) for t in (w, g, m, v))
    R, C = w2.shape
    tr = _row_tile(R, 256)

    def body(w_ref, g_ref, m_ref, v_ref, d_ref, nm_ref, nv_ref):
        gv = g_ref[...]
        nm = ADAM_B1 * m_ref[...] + (1.0 - ADAM_B1) * gv
        nv = ADAM_B2 * v_ref[...] + (1.0 - ADAM_B2) * (gv * gv)
        m_hat = nm / (1.0 - ADAM_B1 ** ADAM_STEP)
        v_hat = nv / (1.0 - ADAM_B2 ** ADAM_STEP)
        d_ref[...] = -ADAM_LR * (m_hat / (jnp.sqrt(v_hat) + ADAM_EPS) + ADAM_WD * w_ref[...])
        nm_ref[...] = nm
        nv_ref[...] = nv

    spec = pl.BlockSpec((tr, C), lambda i: (i, 0))
    sds = jax.ShapeDtypeStruct((R, C), F32)
    d, nm, nv = pl.pallas_call(
        body, name=name, grid=(R // tr,), in_specs=[spec] * 4, out_specs=[spec] * 3, out_shape=[sds] * 3,
        compiler_params=pltpu.CompilerParams(dimension_semantics=("parallel",),
                                             vmem_limit_bytes=_vmem_limit(7 * _nbytes((tr, C), F32))),
    )(w2, g2, m2, v2)
    return d.reshape(shape), nm.reshape(shape), nv.reshape(shape)


_ANY = pl.BlockSpec(memory_space=pl.ANY)


def _mesh_pos():
    return lax.axis_index("x"), lax.axis_index("y"), lax.axis_index("c")


def _other_chips(x, y):
    return [(1 - x, y), (x, 1 - y), (1 - x, 1 - y)]


def _gather_over_chips(name, fulls, views):
    n = len(views)
    nf = len(fulls)

    def body(*refs):
        full = refs[nf:2 * nf]
        ici_send, ici_recv, d2d_send, d2d_recv = refs[2 * nf:]
        x, y, c = _mesh_pos()
        chips = _other_chips(x, y)
        mine = 2 * x + y
        sibling = (x, y, 1 - c)

        def ici(a, p, k):
            i, view, _ = views[a]
            part = view(full[i], k, c)
            return pltpu.make_async_remote_copy(
                src_ref=part, dst_ref=part, send_sem=ici_send.at[a, p], recv_sem=ici_recv.at[a, p],
                device_id=(*chips[p], c), device_id_type=MESH)

        def d2d(a, p, h):
            i, view, _ = views[a]
            px, py = chips[p]
            part = view(full[i], 2 * px + py, h)
            return pltpu.make_async_remote_copy(
                src_ref=part, dst_ref=part, send_sem=d2d_send.at[a, p], recv_sem=d2d_recv.at[a, p],
                device_id=sibling, device_id_type=MESH)

        sends = [ici(a, p, mine) for a in range(n) for p in range(3)]
        for cp in sends:
            cp.start()
        passed = []
        for a in range(n):
            for p, (px, py) in enumerate(chips):
                ici(a, p, 2 * px + py).wait_recv()
                if views[a][2]:
                    fwd = d2d(a, p, c)
                    fwd.start()
                    passed.append(fwd)
        for a in range(n):
            if views[a][2]:
                for p in range(3):
                    d2d(a, p, 1 - c).wait_recv()
        for cp in sends + passed:
            cp.wait_send()

    return pl.pallas_call(
        body, name=name, in_specs=[_ANY] * nf, out_specs=[_ANY] * nf,
        out_shape=[jax.ShapeDtypeStruct(f.shape, f.dtype) for f in fulls],
        input_output_aliases={i: i for i in range(nf)},
        scratch_shapes=[pltpu.SemaphoreType.DMA((n, 3))] * 4,
        compiler_params=pltpu.CompilerParams(has_side_effects=True),
    )(*fulls)


_HBM = pl.BlockSpec(memory_space=pltpu.HBM)
_SEM = pl.BlockSpec(memory_space=pltpu.SEMAPHORE)


def _in_hbm(arrays):
    return [pltpu.with_memory_space_constraint(a, pltpu.HBM) for a in arrays]


def _gather_start(name, fulls, views):
    nf = len(fulls)
    ng = 1 + max(g for _, _, g in views)

    def body(*refs):
        full = refs[nf:2 * nf]
        send_sems, recv_sems = refs[2 * nf:2 * nf + ng], refs[2 * nf + ng:]
        x, y, c = _mesh_pos()
        chips = _other_chips(x, y)
        for i, view, g in views:
            part = view(full[i], 2 * x + y, c)
            for px, py in chips:
                pltpu.make_async_remote_copy(
                    src_ref=part, dst_ref=part, send_sem=send_sems[g], recv_sem=recv_sems[g],
                    device_id=(px, py, c), device_id_type=MESH).start()

    outs = pl.pallas_call(
        body, name=name, in_specs=[_HBM] * nf, out_specs=[_HBM] * nf + [_SEM] * (2 * ng),
        out_shape=[pltpu.HBM(f.shape, f.dtype) for f in fulls] + [pltpu.SemaphoreType.DMA(())] * (2 * ng),
        input_output_aliases={i: i for i in range(nf)},
        compiler_params=pltpu.CompilerParams(has_side_effects=pltpu.SideEffectType.DATAFLOW_SIDE_EFFECTING),
    )(*_in_hbm(fulls))
    return list(outs[:nf]), list(outs[nf:nf + ng]), list(outs[nf + ng:])


def _gather_wait(name, fulls, views, send_sem, recv_sem, after):
    nf = len(fulls)

    def body(*refs):
        send_ref, recv_ref = refs[nf], refs[nf + 1]
        full = refs[nf + 3:]
        x, y, c = _mesh_pos()
        copies = [pltpu.make_async_remote_copy(
            src_ref=view(full[i], 2 * x + y, c), dst_ref=view(full[i], 2 * px + py, c),
            send_sem=send_ref, recv_sem=recv_ref, device_id=(px, py, c), device_id_type=MESH)
            for i, view in views for px, py in _other_chips(x, y)]
        for cp in copies:
            cp.wait_send()
        for cp in copies:
            cp.wait_recv()

    outs = pl.pallas_call(
        body, name=name, in_specs=[_HBM] * nf + [_SEM, _SEM, _ANY], out_specs=[_HBM] * nf,
        out_shape=[pltpu.HBM(f.shape, f.dtype) for f in fulls],
        input_output_aliases={i: i for i in range(nf)},
        compiler_params=pltpu.CompilerParams(has_side_effects=pltpu.SideEffectType.DATAFLOW_SIDE_EFFECTING),
    )(*fulls, send_sem, recv_sem, after)
    return list(outs)


def _forward_to_sibling(name, fulls, views):
    n, nf = len(views), len(fulls)

    def body(*refs):
        full = refs[nf:2 * nf]
        send_sems, recv_sems = refs[2 * nf:]
        x, y, c = _mesh_pos()
        chips = _other_chips(x, y)

        def copy(a, p, h):
            i, view = views[a]
            px, py = chips[p]
            part = view(full[i], 2 * px + py, h)
            return pltpu.make_async_remote_copy(
                src_ref=part, dst_ref=part, send_sem=send_sems.at[a, p], recv_sem=recv_sems.at[a, p],
                device_id=(x, y, 1 - c), device_id_type=MESH)

        sends = [copy(a, p, c) for a in range(n) for p in range(3)]
        for cp in sends:
            cp.start()
        for a in range(n):
            for p in range(3):
                copy(a, p, 1 - c).wait_recv()
        for cp in sends:
            cp.wait_send()

    return pl.pallas_call(
        body, name=name, in_specs=[_ANY] * nf, out_specs=[_ANY] * nf,
        out_shape=[jax.ShapeDtypeStruct(f.shape, f.dtype) for f in fulls],
        input_output_aliases={i: i for i in range(nf)},
        scratch_shapes=[pltpu.SemaphoreType.DMA((n, 3))] * 2,
        compiler_params=pltpu.CompilerParams(has_side_effects=True),
    )(*fulls)


def _region_view(ref, kind, k, c):
    if kind == "lead":
        rh = ref.shape[1] // N_CORES
        return ref.at[k, pl.ds(pl.multiple_of(c * rh, 8), rh), :]
    rows, cols = ref.shape
    if kind == "cols":
        rh, cw = rows // N_CORES, cols // N_CHIPS
        return ref.at[pl.ds(pl.multiple_of(c * rh, 8), rh), pl.ds(k * cw, cw)]
    rh = rows // (N_CHIPS * N_CORES)
    return ref.at[pl.ds(pl.multiple_of((N_CORES * k + c) * rh, 8), rh), :]


def _send_to_sibling(name, grads, kinds):
    n = len(grads)
    shapes = [jax.ShapeDtypeStruct((N_CHIPS,) + _region_shape(g, kd), F32) for g, kd in zip(grads, kinds)]

    def body(*refs):
        g_ref, land = refs[:n], refs[n:2 * n]
        send_sems, recv_sems = refs[2 * n:]
        x, y, c = _mesh_pos()
        copies = []
        for a in range(n):
            for k in range(N_CHIPS):
                cp = pltpu.make_async_remote_copy(
                    src_ref=_region_view(g_ref[a], kinds[a], k, 1 - c), dst_ref=land[a].at[k],
                    send_sem=send_sems.at[a, k], recv_sem=recv_sems.at[a, k],
                    device_id=(x, y, 1 - c), device_id_type=MESH)
                cp.start()
                copies.append(cp)
        for cp in copies:
            cp.wait_recv()
        for cp in copies:
            cp.wait_send()

    return pl.pallas_call(
        body, name=name, in_specs=[_ANY] * n, out_specs=[_ANY] * n, out_shape=shapes,
        scratch_shapes=[pltpu.SemaphoreType.DMA((n, N_CHIPS)), pltpu.SemaphoreType.DMA((n, N_CHIPS))],
        compiler_params=pltpu.CompilerParams(has_side_effects=True),
    )(*grads)


def _scatter_over_chips(name, pair_sums):
    n = len(pair_sums)

    def body(*refs):
        p_ref, land = refs[:n], refs[n:2 * n]
        send_sems, recv_sems = refs[2 * n:]
        x, y, c = _mesh_pos()
        chips = _other_chips(x, y)
        sends = []
        for a in range(n):
            for p, (px, py) in enumerate(chips):
                cp = pltpu.make_async_remote_copy(
                    src_ref=p_ref[a].at[2 * px + py], dst_ref=land[a].at[p], send_sem=send_sems.at[a, p],
                    recv_sem=recv_sems.at[a, p], device_id=(px, py, c), device_id_type=MESH)
                cp.start()
                sends.append(cp)
        for cp in sends:
            cp.wait_recv()
        for cp in sends:
            cp.wait_send()

    return pl.pallas_call(
        body, name=name, in_specs=[_ANY] * n, out_specs=[_ANY] * n,
        out_shape=[jax.ShapeDtypeStruct((N_CHIPS - 1,) + p.shape[1:], p.dtype) for p in pair_sums],
        scratch_shapes=[pltpu.SemaphoreType.DMA((n, 3)), pltpu.SemaphoreType.DMA((n, 3))],
        compiler_params=pltpu.CompilerParams(has_side_effects=True),
    )(*pair_sums)


def _swap_halves(name, shards):
    n = len(shards)

    def body(*refs):
        out = refs[n:2 * n]
        send_sems, recv_sems = refs[2 * n:]
        x, y, c = _mesh_pos()
        sends = []
        for a in range(n):
            rh = out[a].shape[0] // N_CORES
            mine = out[a].at[pl.ds(pl.multiple_of(c * rh, 8), rh), :]
            cp = pltpu.make_async_remote_copy(
                src_ref=mine, dst_ref=mine, send_sem=send_sems.at[a], recv_sem=recv_sems.at[a],
                device_id=(x, y, 1 - c), device_id_type=MESH)
            cp.start()
            sends.append(cp)
        for a in range(n):
            rh = out[a].shape[0] // N_CORES
            theirs = out[a].at[pl.ds(pl.multiple_of((1 - c) * rh, 8), rh), :]
            pltpu.make_async_remote_copy(
                src_ref=theirs, dst_ref=theirs, send_sem=send_sems.at[a], recv_sem=recv_sems.at[a],
                device_id=(x, y, 1 - c), device_id_type=MESH).wait_recv()
        for cp in sends:
            cp.wait_send()

    return pl.pallas_call(
        body, name=name, in_specs=[_ANY] * n, out_specs=[_ANY] * n,
        out_shape=[jax.ShapeDtypeStruct(s.shape, s.dtype) for s in shards],
        input_output_aliases={i: i for i in range(n)},
        scratch_shapes=[pltpu.SemaphoreType.DMA((n,)), pltpu.SemaphoreType.DMA((n,))],
        compiler_params=pltpu.CompilerParams(has_side_effects=True),
    )(*shards)


def _gather_all_devices(name, block):
    R, C = block.shape
    ndev = N_CHIPS * N_CORES

    def body(b_ref, out_ref, send_sems, recv_sems, local_sem):
        x, y, c = _mesh_pos()
        mine = 4 * x + 2 * y + c
        own = pltpu.make_async_copy(b_ref, out_ref.at[mine], local_sem)
        own.start()
        sends = []
        for mask in range(1, ndev):
            fx, fy, fc = (mask >> 2) & 1, (mask >> 1) & 1, mask & 1
            px, py, pc = x ^ fx, y ^ fy, c ^ fc
            cp = pltpu.make_async_remote_copy(
                src_ref=b_ref, dst_ref=out_ref.at[mine], send_sem=send_sems.at[mask - 1],
                recv_sem=recv_sems.at[mask - 1], device_id=(px, py, pc), device_id_type=MESH)
            cp.start()
            sends.append(cp)
        for mask in range(1, ndev):
            fx, fy, fc = (mask >> 2) & 1, (mask >> 1) & 1, mask & 1
            px, py, pc = x ^ fx, y ^ fy, c ^ fc
            pltpu.make_async_remote_copy(
                src_ref=b_ref, dst_ref=out_ref.at[4 * px + 2 * py + pc], send_sem=send_sems.at[mask - 1],
                recv_sem=recv_sems.at[mask - 1], device_id=(px, py, pc), device_id_type=MESH).wait_recv()
        for cp in sends:
            cp.wait_send()
        own.wait()

    return pl.pallas_call(
        body, name=name, in_specs=[_ANY], out_specs=_ANY,
        out_shape=jax.ShapeDtypeStruct((ndev, R, C), F32),
        scratch_shapes=[pltpu.SemaphoreType.DMA((ndev - 1,)), pltpu.SemaphoreType.DMA((ndev - 1,)),
                        pltpu.SemaphoreType.DMA(())],
        compiler_params=pltpu.CompilerParams(has_side_effects=True),
    )(block)


def _reduce_scatter(grads, kinds, pos):
    landed = _send_to_sibling("rs_pair_send", grads, kinds)
    pair = [_region_add(f"rs_pair_add_{a}", g, kd, ld, pos[1:]) for a, (g, kd, ld) in enumerate(zip(grads, kinds, landed))]
    parts = _scatter_over_chips("rs_chip_send", pair)
    shards = [_chip_sum(f"rs_chip_add_{a}", p, ld, pos) for a, (p, ld) in enumerate(zip(pair, parts))]
    return _swap_halves("rs_swap_halves", shards)


def kernel(x, norm_g, ffn1_w_gate, ffn1_w_up, ffn1_w_down, ffn2_w_gate, ffn2_w_up, ffn2_w_down, even_w_in, even_b_forget, even_w_out, odd_w_qkv, odd_w_out, final_norm_g, loss_target, m_norm_g, m_ffn1_w_gate, m_ffn1_w_up, m_ffn1_w_down, m_ffn2_w_gate, m_ffn2_w_up, m_ffn2_w_down, m_even_w_in, m_even_b_forget, m_even_w_out, m_odd_w_qkv, m_odd_w_out, m_final_norm_g, v_norm_g, v_ffn1_w_gate, v_ffn1_w_up, v_ffn1_w_down, v_ffn2_w_gate, v_ffn2_w_up, v_ffn2_w_down, v_even_w_in, v_even_b_forget, v_even_w_out, v_odd_w_qkv, v_odd_w_out, v_final_norm_g):
    _, S, D = x.shape
    L = norm_g.shape[0]
    assert L == 2 and even_w_in.shape[0] == 1 and odd_w_qkv.shape[0] == 1
    fs = ffn1_w_gate.shape[2]
    F = N_CHIPS * fs
    wc = even_w_in.shape[2]
    n_heads = D // HEAD_DIM
    n_fox = N_CHIPS * wc - 3 * D
    n_sb = n_heads - n_fox
    qs = odd_w_qkv.shape[2]
    os_ = even_w_out.shape[1]
    ns = norm_g.shape[2]
    xi, yi, ci = _mesh_pos()
    chip = 2 * xi + yi

    pos = jnp.stack([chip, ci]).astype(jnp.int32)
    kchip = pos[:1]
    lane = lambda start, size: pl.ds(pl.multiple_of(start, LANES), size)
    sub = lambda start, size: pl.ds(pl.multiple_of(start, 16), size)
    gate_view = lambda r, k, h: r.at[sub(h * (D // 2), D // 2), lane(k * 2 * fs, fs)]
    up_view = lambda r, k, h: r.at[sub(h * (D // 2), D // 2), lane(k * 2 * fs + fs, fs)]
    down_view = lambda r, k, h: r.at[sub(k * fs + h * (fs // 2), fs // 2), :]
    out_view = lambda r, k, h: r.at[sub(k * os_ + h * (os_ // 2), os_ // 2), :]
    tr_d = _row_tile(fs, 512, step=16)
    tr_o = _row_tile(os_, 512, step=16)
    ffn_w = {"ffn1": (ffn1_w_gate, ffn1_w_up, ffn1_w_down), "ffn2": (ffn2_w_gate, ffn2_w_up, ffn2_w_down)}
    win_view = lambda r, k, h: r.at[k, sub(h * (D // 2), D // 2), :]
    qkv_view = lambda r, k, h: r.at[sub(h * (D // 2), D // 2), lane(k * qs, qs)]
    fulls, views, groups = [], [], {}
    for l in range(L):
        for blk in ("ffn1", "mix", "ffn2"):
            o, v0 = len(fulls), len(views)
            if blk == "mix" and l == 0:
                fulls += [_cast_into("cast_win", even_w_in, 0, kchip, (N_CHIPS, D, wc), lambda i, k: (k, i, 0)),
                          _cast_into("cast_wout_e", even_w_out, 0, kchip, (D, D), lambda i, k: (k * (os_ // tr_o) + i, 0))]
                views += [(o, win_view), (o + 1, out_view)]
            elif blk == "mix":
                fulls += [_cast_into("cast_wqkv_o", odd_w_qkv, 0, kchip, (D, N_CHIPS * qs), lambda i, k: (i, k)),
                          _cast_into("cast_wout_o", odd_w_out, 0, kchip, (D, D), lambda i, k: (k * (os_ // tr_o) + i, 0))]
                views += [(o, qkv_view), (o + 1, out_view)]
            else:
                wg, wu, wd = ffn_w[blk]
                t = f"cast_{blk}_l{l}"
                gu = _cast_into(t + "_gate", wg, l, kchip, (D, 2 * F), lambda i, k: (i, 2 * k))
                gu = _cast_into(t + "_up", wu, l, kchip, (D, 2 * F), lambda i, k: (i, 2 * k + 1), full=gu)
                dn = _cast_into(t + "_down", wd, l, kchip, (F, D), lambda i, k: (k * (fs // tr_d) + i, 0))
                fulls += [gu, dn]
                views += [(o, gate_view), (o, up_view), (o + 1, down_view)]
            gid = len(groups)
            views[v0:] = [(i, view, gid) for i, view in views[v0:]]
            groups[(blk, l)] = (gid, list(range(o, len(fulls))), list(range(v0, len(views))))
    norm_own = lax.dynamic_update_slice(jnp.zeros((L, 3, N_CHIPS * ns), F32), norm_g, (0, 0, chip * ns))
    (norm_full,) = _gather_over_chips("gather_norm", [norm_own], [(0, lambda r, k, h: r.at[:, :, lane(k * ns, ns)], False)])
    started, send_sems, recv_sems = _gather_start("gather_start", fulls, views)

    def fetch(block, after):
        if block == "norm_g":
            return norm_full
        if block == "final_g":
            return final_norm_g[None, :]
        gid, arrays, rows = groups[block]
        tag = f"{block[0]}_l{block[1]}"
        local = [(views[a][0] - arrays[0], views[a][1]) for a in rows]
        got = _gather_wait("gather_wait_" + tag, [started[i] for i in arrays], local, send_sems[gid], recv_sems[gid],
                           x if after is None else after)
        got = _forward_to_sibling("gather_pass_" + tag, got, local)
        if block[0] != "mix":
            return got
        if block[1] == 1:
            return {"wqkv_o": got[0], "wout_o": got[1]}
        win = jnp.concatenate([got[0][k] for k in range(N_CHIPS)], axis=1)
        return {"wqkv_e": win[:, :3 * D], "wf": jnp.pad(win[:, 3 * D:], ((0, 0), (0, LANES - n_fox))),
                "bf": jnp.pad(even_b_forget, ((0, 0), (0, LANES - n_fox))), "wout_e": got[1]}

    loss_vec, grad_x, g = _local_step(x[0], loss_target[0], fetch, fs, n_heads, n_sb)

    dwin = jnp.concatenate([g["dwqkv_e"], g["dwf"][:, :n_fox]], axis=1)
    dwin4 = jnp.stack([dwin[:, k * wc:(k + 1) * wc] for k in range(N_CHIPS)])
    rs_in = (g["dwgu1"] + g["dwgu2"] + g["dwd1"] + g["dwd2"]
             + [dwin4, g["dwqkv_o"], g["dwout_e"], g["dwout_o"]])
    kinds = ["cols"] * (2 * L) + ["rows"] * (2 * L) + ["lead", "cols", "rows", "rows"]
    red = _reduce_scatter(rs_in, kinds, pos)
    gu1, gu2, gd1, gd2 = red[0:L], red[L:2 * L], red[2 * L:3 * L], red[3 * L:4 * L]
    g_win, g_qkv_o, g_wout_e, g_wout_o = red[4 * L:]

    small_rows = [g["dnorm"][l][i] for l in range(L) for i in range(3)] + [
        g["dfinal"], jnp.pad(g["db"], ((0, 0), (0, D - LANES))), jnp.pad(loss_vec, ((0, 0), (0, D - LANES)))]
    small = jnp.concatenate(small_rows + [jnp.zeros((16 - len(small_rows), D), F32)], axis=0)
    small_sum = _sum_leading("small_sum", _gather_all_devices("small_gather", small))
    loss = small_sum[3 * L + 2, 0]
    g_norm = lax.dynamic_slice_in_dim(small_sum[:3 * L].reshape(L, 3, D), chip * ns, ns, axis=2)
    g_final = small_sum[3 * L]
    g_bf = small_sum[3 * L + 1, :n_fox][None, :]

    grads = [
        g_norm,
        jnp.stack([t[:, :fs] for t in gu1]), jnp.stack([t[:, fs:] for t in gu1]), jnp.stack(gd1),
        jnp.stack([t[:, :fs] for t in gu2]), jnp.stack([t[:, fs:] for t in gu2]), jnp.stack(gd2),
        g_win[None], g_bf, g_wout_e[None], g_qkv_o[None], g_wout_o[None], g_final]
    weights = [norm_g, ffn1_w_gate, ffn1_w_up, ffn1_w_down, ffn2_w_gate, ffn2_w_up, ffn2_w_down,
               even_w_in, even_b_forget, even_w_out, odd_w_qkv, odd_w_out, final_norm_g]
    ms = [m_norm_g, m_ffn1_w_gate, m_ffn1_w_up, m_ffn1_w_down, m_ffn2_w_gate, m_ffn2_w_up, m_ffn2_w_down,
          m_even_w_in, m_even_b_forget, m_even_w_out, m_odd_w_qkv, m_odd_w_out, m_final_norm_g]
    vs = [v_norm_g, v_ffn1_w_gate, v_ffn1_w_up, v_ffn1_w_down, v_ffn2_w_gate, v_ffn2_w_up, v_ffn2_w_down,
          v_even_w_in, v_even_b_forget, v_even_w_out, v_odd_w_qkv, v_odd_w_out, v_final_norm_g]
    deltas, new_ms, new_vs = [], [], []
    for i, (wt, gt, mt, vt) in enumerate(zip(weights, grads, ms, vs)):
        d, nm, nv = _adamw(f"adamw_{i}", wt, gt, mt, vt)
        deltas.append(d)
        new_ms.append(nm)
        new_vs.append(nv)
    return (loss, grad_x[None], *grads, *deltas, *new_ms, *new_vs)
```

```python
import functools
import math

import jax
import jax.numpy as jnp
from jax import lax
from jax.experimental import pallas as pl
from jax.experimental.pallas import tpu as pltpu

F32 = jnp.float32
BF16 = jnp.bfloat16

HEAD_DIM = 128
ROPE_DIMS = 32
ROPE_THETA = 500000.0
DILATED_PATTERNS = ((128, 1), (512, 4), (2048, 16))
RMS_EPS = 1e-6
NEG_INF = -1e30
ADAM_LR = 0.001
ADAM_B1 = 0.9
ADAM_B2 = 0.999
ADAM_EPS = 1e-08
ADAM_WD = 0.01
ADAM_STEP = 10

N_CHIPS = 4
N_CORES = 2
LANES = 128
BLK = 128
VMEM_BYTES_V7X = 64 * 2**20
MESH = pl.DeviceIdType.MESH


def _vmem_limit(block_bytes, scratch_bytes=0):
    need = 2 * block_bytes + scratch_bytes + 12 * 2**20
    return int(min(need, VMEM_BYTES_V7X - 6 * 2**20))


def _nbytes(shape, dtype):
    return math.prod(shape) * jnp.dtype(dtype).itemsize


def _tile(dim, target):
    best = None
    for t in range(LANES, min(dim, target) + 1, LANES):
        if dim % t == 0:
            best = t
    assert best is not None, (dim, target)
    return best


def _row_tile(rows, target, step=8):
    if rows <= target:
        return rows
    best = None
    for t in range(step, target + 1, step):
        if rows % t == 0:
            best = t
    assert best is not None, (rows, target)
    return best


def _mm(name, a, b, mode, out_dtype, res=None, alpha=1.0, after=None, tm_target=512, tn_target=1536, tk_target=2048):
    a3 = a.ndim == 3
    b3 = b.ndim == 3
    if mode == "nn":
        assert not a3 and not b3
        (M, K), (K2, N) = a.shape, b.shape
    elif mode == "nt":
        assert not b3
        if a3:
            P, M, Kp = a.shape
            K = P * Kp
        else:
            M, K = a.shape
        N, K2 = b.shape
    else:
        assert mode == "tn" and not a3
        K, M = a.shape
        if b3:
            P, K2, Np = b.shape
            N = P * Np
        else:
            K2, N = b.shape
    assert K == K2, (name, a.shape, b.shape)
    tm = _tile(M, tm_target)
    tn = _tile(Np if b3 else N, tn_target)
    tk = _tile(Kp if a3 else K, tk_target)
    nk = K // tk
    grid = (M // tm, N // tn, nk)

    if mode == "nn":
        a_spec = pl.BlockSpec((tm, tk), lambda i, j, k: (i, k))
        b_spec = pl.BlockSpec((tk, tn), lambda i, j, k: (k, j))
        dims = (((1,), (0,)), ((), ()))
    elif mode == "nt":
        if a3:
            nkp = Kp // tk
            a_spec = pl.BlockSpec((None, tm, tk), lambda i, j, k: (k // nkp, i, k % nkp))
        else:
            a_spec = pl.BlockSpec((tm, tk), lambda i, j, k: (i, k))
        b_spec = pl.BlockSpec((tn, tk), lambda i, j, k: (j, k))
        dims = (((1,), (1,)), ((), ()))
    else:
        a_spec = pl.BlockSpec((tk, tm), lambda i, j, k: (k, i))
        if b3:
            njp = Np // tn
            b_spec = pl.BlockSpec((None, tk, tn), lambda i, j, k: (j // njp, k, j % njp))
        else:
            b_spec = pl.BlockSpec((tk, tn), lambda i, j, k: (k, j))
        dims = (((0,), (0,)), ((), ()))
    o_spec = pl.BlockSpec((tm, tn), lambda i, j, k: (i, j))
    has_res = res is not None

    def finish(y, r_ref, o_ref):
        if alpha != 1.0:
            y = y * alpha
        if has_res:
            y = r_ref[...] + y
        o_ref[...] = y.astype(o_ref.dtype)

    n_in = 2 + has_res + (after is not None)

    def body(*refs):
        a_ref, b_ref = refs[:2]
        r_ref = refs[2] if has_res else None
        o_ref = refs[n_in]
        part = lax.dot_general(a_ref[...], b_ref[...], dims, preferred_element_type=F32)
        if nk == 1:
            finish(part, r_ref, o_ref)
            return
        acc_ref = refs[-1]
        k = pl.program_id(2)

        @pl.when(k == 0)
        def _():
            acc_ref[...] = part

        @pl.when(k > 0)
        def _():
            acc_ref[...] += part

        @pl.when(k == nk - 1)
        def _():
            finish(acc_ref[...], r_ref, o_ref)

    in_specs = [a_spec, b_spec] + ([o_spec] if has_res else []) + ([_ANY] if after is not None else [])
    args = (a, b) + ((res,) if has_res else ()) + ((after,) if after is not None else ())
    blk = (_nbytes((tm, tk), a.dtype) + _nbytes((tk, tn), b.dtype) + _nbytes((tm, tn), out_dtype)
           + (_nbytes((tm, tn), F32) if has_res else 0))
    return pl.pallas_call(
        body, name=name, grid=grid, in_specs=in_specs, out_specs=o_spec,
        out_shape=jax.ShapeDtypeStruct((M, N), out_dtype),
        scratch_shapes=[pltpu.VMEM((tm, tn), F32)] if nk > 1 else [],
        compiler_params=pltpu.CompilerParams(
            dimension_semantics=("parallel", "parallel", "arbitrary"),
            vmem_limit_bytes=_vmem_limit(blk, 2 * _nbytes((tm, tn), F32))),
    )(*args)


def _rms_fwd(name, x, g):
    S, D = x.shape
    tr = _row_tile(S, 256)

    def body(x_ref, g_ref, n_ref):
        xv = x_ref[...]
        r = lax.rsqrt(jnp.mean(xv * xv, axis=-1, keepdims=True) + RMS_EPS)
        n_ref[...] = (xv * r * g_ref[...]).astype(BF16)

    return pl.pallas_call(
        body, name=name, grid=(S // tr,),
        in_specs=[pl.BlockSpec((tr, D), lambda i: (i, 0)), pl.BlockSpec((1, D), lambda i: (0, 0))],
        out_specs=pl.BlockSpec((tr, D), lambda i: (i, 0)),
        out_shape=jax.ShapeDtypeStruct((S, D), BF16),
        compiler_params=pltpu.CompilerParams(dimension_semantics=("parallel",)),
    )(x, g)


def _rms_bwd(name, dn, x, g, dres):
    S, D = x.shape
    tr = _row_tile(S, 256)

    def body(dn_ref, x_ref, g_ref, dres_ref, dx_ref, dxb_ref, dg_ref):
        i = pl.program_id(0)
        xv = x_ref[...]
        dnv = dn_ref[...]
        r = lax.rsqrt(jnp.mean(xv * xv, axis=-1, keepdims=True) + RMS_EPS)
        u = dnv * g_ref[...]
        dot = jnp.mean(u * xv, axis=-1, keepdims=True)
        dx = dres_ref[...] + r * u - xv * (r * r * r * dot)
        dx_ref[...] = dx
        dxb_ref[...] = dx.astype(BF16)

        @pl.when(i == 0)
        def _():
            dg_ref[...] = jnp.zeros_like(dg_ref)

        dg_ref[...] += jnp.sum(dnv * xv * r, axis=0, keepdims=True)

    row = pl.BlockSpec((tr, D), lambda i: (i, 0))
    vec = pl.BlockSpec((1, D), lambda i: (0, 0))
    return pl.pallas_call(
        body, name=name, grid=(S // tr,),
        in_specs=[row, row, vec, row], out_specs=[row, row, vec],
        out_shape=[jax.ShapeDtypeStruct((S, D), F32), jax.ShapeDtypeStruct((S, D), BF16),
                   jax.ShapeDtypeStruct((1, D), F32)],
        compiler_params=pltpu.CompilerParams(dimension_semantics=("arbitrary",)),
    )(dn, x, g, dres)


def _loss_head(name, x, g, target):
    S, D = x.shape
    tr = _row_tile(S, 256)

    def body(x_ref, g_ref, t_ref, dx_ref, dxb_ref, dg_ref, loss_ref):
        i = pl.program_id(0)
        xv = x_ref[...]
        gv = g_ref[...]
        r = lax.rsqrt(jnp.mean(xv * xv, axis=-1, keepdims=True) + RMS_EPS)
        diff = xv * r * gv - t_ref[...]
        part = 0.5 * jnp.sum(jnp.mean(diff * diff, axis=-1, keepdims=True), axis=0, keepdims=True)
        dy = diff * (1.0 / D)
        u = dy * gv
        dot = jnp.mean(u * xv, axis=-1, keepdims=True)
        dx = r * u - xv * (r * r * r * dot)
        dx_ref[...] = dx
        dxb_ref[...] = dx.astype(BF16)

        @pl.when(i == 0)
        def _():
            dg_ref[...] = jnp.zeros_like(dg_ref)
            loss_ref[...] = jnp.zeros_like(loss_ref)

        dg_ref[...] += jnp.sum(dy * xv * r, axis=0, keepdims=True)
        loss_ref[...] += jnp.broadcast_to(part, loss_ref.shape)

    row = pl.BlockSpec((tr, D), lambda i: (i, 0))
    vec = pl.BlockSpec((1, D), lambda i: (0, 0))
    lvec = pl.BlockSpec((1, LANES), lambda i: (0, 0))
    return pl.pallas_call(
        body, name=name, grid=(S // tr,),
        in_specs=[row, vec, row], out_specs=[row, row, vec, lvec],
        out_shape=[jax.ShapeDtypeStruct((S, D), F32), jax.ShapeDtypeStruct((S, D), BF16),
                   jax.ShapeDtypeStruct((1, D), F32), jax.ShapeDtypeStruct((1, LANES), F32)],
        compiler_params=pltpu.CompilerParams(dimension_semantics=("arbitrary",)),
    )(x, g, target)


def _swiglu_fwd(name, gu, fs):
    S, two_f = gu.shape
    nslab = two_f // (2 * fs)
    tr = _row_tile(S, 256)

    def body(gu_ref, h_ref):
        gv = gu_ref[:, :fs]
        uv = gu_ref[:, fs:]
        h_ref[...] = (gv * jax.nn.sigmoid(gv) * uv).astype(BF16)

    return pl.pallas_call(
        body, name=name, grid=(S // tr, nslab),
        in_specs=[pl.BlockSpec((tr, 2 * fs), lambda i, k: (i, k))],
        out_specs=pl.BlockSpec((tr, fs), lambda i, k: (i, k)),
        out_shape=jax.ShapeDtypeStruct((S, nslab * fs), BF16),
        compiler_params=pltpu.CompilerParams(dimension_semantics=("parallel", "parallel")),
    )(gu)


def _swiglu_bwd(name, dh, gu, fs):
    S, two_f = gu.shape
    nslab = two_f // (2 * fs)
    tr = _row_tile(S, 256)

    def body(dh_ref, gu_ref, o_ref):
        gv = gu_ref[:, :fs]
        uv = gu_ref[:, fs:]
        dhv = dh_ref[...]
        sg = jax.nn.sigmoid(gv)
        silu = gv * sg
        o_ref[:, :fs] = (dhv * uv * (sg + silu * (1.0 - sg))).astype(BF16)
        o_ref[:, fs:] = (dhv * silu).astype(BF16)

    return pl.pallas_call(
        body, name=name, grid=(S // tr, nslab),
        in_specs=[pl.BlockSpec((tr, fs), lambda i, k: (i, k)), pl.BlockSpec((tr, 2 * fs), lambda i, k: (i, k))],
        out_specs=pl.BlockSpec((tr, 2 * fs), lambda i, k: (i, k)),
        out_shape=jax.ShapeDtypeStruct((S, two_f), BF16),
        compiler_params=pltpu.CompilerParams(dimension_semantics=("parallel", "parallel")),
    )(dh, gu)


def _tri_rows(r0, nrows, ncols, lower):
    row = lax.broadcasted_iota(jnp.int32, (nrows, ncols), 0) + r0
    col = lax.broadcasted_iota(jnp.int32, (nrows, ncols), 1)
    return jnp.where((col <= row) if lower else (col >= row), 1.0, 0.0).astype(F32)


def _gate_fwd(name, hf, b):
    S = hf.shape[0]
    tb = _row_tile(S, 256)

    def body(hf_ref, b_ref, cf_ref, cft_ref, lf_ref):
        zz = hf_ref[...] + b_ref[...]
        lf_ref[...] = jnp.minimum(zz, 0.0) - jnp.log1p(jnp.exp(-jnp.abs(zz)))

        def blk(i, c):
            r0 = pl.multiple_of(i * tb, tb)
            tri = _tri_rows(r0, tb, S, True)
            cf_ref[pl.ds(r0, tb), :] = jnp.dot(tri, lf_ref[...], precision=lax.Precision.HIGHEST,
                                               preferred_element_type=F32)
            return c

        lax.fori_loop(0, S // tb, blk, 0)
        cft_ref[...] = cf_ref[...].T

    full = pl.BlockSpec((S, LANES), lambda: (0, 0))
    return pl.pallas_call(
        body, name=name, in_specs=[full, pl.BlockSpec((1, LANES), lambda: (0, 0))],
        out_specs=[full, pl.BlockSpec((LANES, S), lambda: (0, 0))],
        out_shape=[jax.ShapeDtypeStruct((S, LANES), F32), jax.ShapeDtypeStruct((LANES, S), F32)],
        scratch_shapes=[pltpu.VMEM((S, LANES), F32)],
    )(hf, b)


def _gate_bwd(name, dcft, drow, hf, b):
    S = hf.shape[0]
    tb = _row_tile(S, 256)

    def body(dcft_ref, drow_ref, hf_ref, b_ref, dhf_ref, db_ref, dcf_ref, dlf_ref):
        dcf_ref[...] = dcft_ref[...].T + drow_ref[...]

        def blk(i, c):
            r0 = pl.multiple_of(i * tb, tb)
            tri = _tri_rows(r0, tb, S, False)
            dlf_ref[pl.ds(r0, tb), :] = jnp.dot(tri, dcf_ref[...], precision=lax.Precision.HIGHEST,
                                                preferred_element_type=F32)
            return c

        lax.fori_loop(0, S // tb, blk, 0)
        zz = hf_ref[...] + b_ref[...]
        dhf = dlf_ref[...] * jax.nn.sigmoid(-zz)
        dhf_ref[...] = dhf.astype(BF16)
        db_ref[...] = jnp.sum(dhf, axis=0, keepdims=True)

    full = pl.BlockSpec((S, LANES), lambda: (0, 0))
    vec = pl.BlockSpec((1, LANES), lambda: (0, 0))
    return pl.pallas_call(
        body, name=name, in_specs=[pl.BlockSpec((LANES, S), lambda: (0, 0)), full, full, vec],
        out_specs=[full, vec],
        out_shape=[jax.ShapeDtypeStruct((S, LANES), BF16), jax.ShapeDtypeStruct((1, LANES), F32)],
        scratch_shapes=[pltpu.VMEM((S, LANES), F32), pltpu.VMEM((S, LANES), F32)],
    )(dcft, drow, hf, b)


def _rope_tables(S):
    half = ROPE_DIMS // 2
    freqs = ROPE_THETA ** (-jnp.arange(half, dtype=F32) / half)
    ang = jnp.arange(S, dtype=F32)[:, None] * freqs[None, :]
    cos, sin = jnp.cos(ang), jnp.sin(ang)
    pad = HEAD_DIM - ROPE_DIMS
    c = jnp.concatenate([cos, cos, jnp.ones((S, pad), F32)], axis=1)
    s = jnp.concatenate([-sin, sin, jnp.zeros((S, pad), F32)], axis=1)
    return c, s


def _rope_swap(x):
    half = ROPE_DIMS // 2
    lane = lax.broadcasted_iota(jnp.int32, x.shape, 1)
    upper = jnp.where(lane < ROPE_DIMS, pltpu.roll(x, half, 1), 0.0)
    return jnp.where(lane < half, pltpu.roll(x, HEAD_DIM - half, 1), upper)


def _rope(x, c, s):
    return x * c + _rope_swap(x) * s


def _rope_t(dy, c, s):
    return dy * c + _rope_swap(dy * s)


def _split_dot(x, t):
    hi = x.astype(BF16)
    lo = (x - hi.astype(F32)).astype(BF16)
    return (jnp.dot(hi, t, preferred_element_type=F32) + jnp.dot(lo, t, preferred_element_type=F32))


_NT = (((1,), (1,)), ((), ()))
_TN = (((0,), (0,)), ((), ()))


def _dot_nt(a, b):
    return lax.dot_general(a, b, _NT, preferred_element_type=F32)


def _dot_tn(a, b):
    return lax.dot_general(a, b, _TN, preferred_element_type=F32)


def _blk(i):
    return pl.ds(pl.multiple_of(i * BLK, BLK), BLK)


def _delta(i, j):
    row = lax.broadcasted_iota(jnp.int32, (BLK, BLK), 0)
    col = lax.broadcasted_iota(jnp.int32, (BLK, BLK), 1)
    return (row - col) + (i - j) * BLK


def _dilated_mult(delta):
    c = jnp.zeros(delta.shape, F32)
    for window, dil in DILATED_PATTERNS:
        ok = (delta >= 0) & (delta <= window) & ((delta & (dil - 1)) == 0)
        c = c + jnp.where(ok, 1.0, 0.0)
    return c


def _sb_terms(z, mask, t_ex, run):
    t = jnp.log1p(jnp.exp(-jnp.abs(z)))
    lsig = jnp.minimum(z, 0.0) - t
    m = jnp.where(mask, -(jnp.maximum(z, 0.0) + t), 0.0)
    after = _split_dot(m, t_ex)
    a = jnp.where(mask, jnp.exp(lsig + after + run), 0.0)
    return a, m, lsig


def _attn_fwd(name, hq, layer_kind, n_heads, n_sb, cf=None, cft=None, rope_c=None, rope_s=None):
    S = hq.shape[0]
    D = n_heads * HEAD_DIM
    nq = S // BLK
    scale = HEAD_DIM ** -0.5
    even = layer_kind == "even"

    def body(*refs):
        if even:
            q_ref, k_ref, v_ref, cf_ref, cft_ref, o_ref, ob_ref, lse_ref, qs, ks, vs = refs
        else:
            q_ref, k_ref, v_ref, c_ref, s_ref, o_ref, ob_ref, lse_ref, qs, ks, vs = refs
        h = pl.program_id(0)
        if even:
            qs[...] = q_ref[...].astype(BF16)
            ks[...] = k_ref[...].astype(BF16)
        else:
            qs[...] = _rope(q_ref[...], c_ref[...], s_ref[...]).astype(BF16)
            ks[...] = _rope(k_ref[...], c_ref[...], s_ref[...]).astype(BF16)
        vs[...] = v_ref[...].astype(BF16)

        def softmax_head(hh):
            def qblock(i, carry):
                qi = qs[_blk(i), :]
                if even:
                    lane = lax.broadcasted_iota(jnp.int32, (BLK, LANES), 1)
                    cfq = jnp.sum(jnp.where(lane == hh, cf_ref[_blk(i), :], 0.0), axis=1, keepdims=True)

                def kblock(j, c):
                    m_run, l_run, acc = c
                    z = _dot_nt(qi, ks[_blk(j), :]) * scale
                    delta = _delta(i, j)
                    if even:
                        z = z + cfq - cft_ref[hh, :, _blk(j)]
                        ok = delta >= 0
                    else:
                        mult = _dilated_mult(delta)
                        ok = mult > 0.0
                    z = jnp.where(ok, z, NEG_INF)
                    m_new = jnp.maximum(m_run, jnp.max(z, axis=1, keepdims=True))
                    p = jnp.exp(z - m_new)
                    if not even:
                        p = p * mult
                    alpha = jnp.exp(m_run - m_new)
                    l_new = alpha * l_run + jnp.sum(p, axis=1, keepdims=True)
                    acc = alpha * acc + jnp.dot(p.astype(BF16), vs[_blk(j), :], preferred_element_type=F32)
                    return m_new, l_new, acc

                init = (jnp.full((BLK, 1), NEG_INF, F32), jnp.zeros((BLK, 1), F32), jnp.zeros((BLK, HEAD_DIM), F32))
                m_run, l_run, acc = lax.fori_loop(0, i + 1, kblock, init)
                o = acc / l_run
                o_ref[_blk(i), :] = o
                ob_ref[_blk(i), :] = o.astype(BF16)
                lse_ref[_blk(i), :] = jnp.broadcast_to(m_run + jnp.log(l_run), (BLK, HEAD_DIM))
                return carry

            lax.fori_loop(0, nq, qblock, 0)

        def sb_head():
            row = lax.broadcasted_iota(jnp.int32, (BLK, BLK), 0)
            col = lax.broadcasted_iota(jnp.int32, (BLK, BLK), 1)
            t_ex = jnp.where(row > col, 1.0, 0.0).astype(BF16)

            def qblock(i, carry):
                qi = qs[_blk(i), :]

                def kblock(jj, c):
                    run, acc = c
                    j = i - jj
                    z = _dot_nt(qi, ks[_blk(j), :]) * scale
                    a, m, _ = _sb_terms(z, _delta(i, j) > 0, t_ex, run)
                    acc = acc + jnp.dot(a.astype(BF16), vs[_blk(j), :], preferred_element_type=F32)
                    return run + jnp.sum(m, axis=1, keepdims=True), acc

                init = (jnp.zeros((BLK, 1), F32), jnp.zeros((BLK, HEAD_DIM), F32))
                _, acc = lax.fori_loop(0, i + 1, kblock, init)
                o_ref[_blk(i), :] = acc
                ob_ref[_blk(i), :] = acc.astype(BF16)
                lse_ref[_blk(i), :] = jnp.zeros((BLK, HEAD_DIM), F32)
                return carry

            lax.fori_loop(0, nq, qblock, 0)

        if even:
            @pl.when(h < n_sb)
            def _():
                sb_head()

            @pl.when(h >= n_sb)
            def _():
                softmax_head(h - n_sb)
        else:
            softmax_head(h)

    head = lambda off: pl.BlockSpec((S, HEAD_DIM), lambda h, off=off: (0, off + h))
    full = pl.BlockSpec((S, LANES), lambda h: (0, 0))
    if even:
        extra_specs = [full, pl.BlockSpec(cft.shape, lambda h: (0, 0, 0))]
        extra = (cf, cft)
    else:
        extra_specs = [full, full]
        extra = (rope_c, rope_s)
    blk_bytes = 8 * _nbytes((S, HEAD_DIM), F32)
    return pl.pallas_call(
        body, name=name, grid=(n_heads,),
        in_specs=[head(0), head(n_heads), head(2 * n_heads)] + extra_specs,
        out_specs=[head(0), head(0), head(0)],
        out_shape=[jax.ShapeDtypeStruct((S, D), F32), jax.ShapeDtypeStruct((S, D), BF16),
                   jax.ShapeDtypeStruct((S, D), F32)],
        scratch_shapes=[pltpu.VMEM((S, HEAD_DIM), BF16)] * 3,
        compiler_params=pltpu.CompilerParams(dimension_semantics=("arbitrary",),
                                             vmem_limit_bytes=_vmem_limit(blk_bytes, 3 * _nbytes((S, HEAD_DIM), BF16))),
    )(hq, hq, hq, *extra)


def _attn_bwd(name, hq, do, o, lse, layer_kind, n_heads, n_sb, cf=None, cft=None, rope_c=None, rope_s=None):
    S = hq.shape[0]
    D = n_heads * HEAD_DIM
    nq = S // BLK
    scale = HEAD_DIM ** -0.5
    even = layer_kind == "even"

    def body(*refs):
        if even:
            (q_ref, k_ref, v_ref, do_ref, o_ref, lse_ref, cf_ref, cft_ref,
             dh_ref, dcft_ref, drow_ref, qs, ks, vs, dos, dq_acc, dk_acc, dv_acc) = refs
        else:
            (q_ref, k_ref, v_ref, do_ref, o_ref, lse_ref, c_ref, s_ref,
             dh_ref, qs, ks, vs, dos, dq_acc, dk_acc, dv_acc) = refs
        h = pl.program_id(0)
        if even:
            qs[...] = q_ref[...].astype(BF16)
            ks[...] = k_ref[...].astype(BF16)

            @pl.when(h == 0)
            def _():
                dcft_ref[...] = jnp.zeros_like(dcft_ref)
                drow_ref[...] = jnp.zeros_like(drow_ref)
        else:
            qs[...] = _rope(q_ref[...], c_ref[...], s_ref[...]).astype(BF16)
            ks[...] = _rope(k_ref[...], c_ref[...], s_ref[...]).astype(BF16)
        vs[...] = v_ref[...].astype(BF16)
        dos[...] = do_ref[...].astype(BF16)
        dk_acc[...] = jnp.zeros_like(dk_acc)
        dv_acc[...] = jnp.zeros_like(dv_acc)

        def softmax_head(hh):
            def qblock(i, carry):
                qi = qs[_blk(i), :]
                doi = dos[_blk(i), :]
                dvec = jnp.sum(do_ref[_blk(i), :] * o_ref[_blk(i), :], axis=1, keepdims=True)
                lse_i = jnp.max(lse_ref[_blk(i), :], axis=1, keepdims=True)
                if even:
                    lane = lax.broadcasted_iota(jnp.int32, (BLK, LANES), 1)
                    cfq = jnp.sum(jnp.where(lane == hh, cf_ref[_blk(i), :], 0.0), axis=1, keepdims=True)

                def kblock(j, c):
                    dq, ds_rows = c
                    kj = ks[_blk(j), :]
                    z = _dot_nt(qi, kj) * scale
                    delta = _delta(i, j)
                    if even:
                        z = z + cfq - cft_ref[hh, :, _blk(j)]
                        ok = delta >= 0
                    else:
                        mult = _dilated_mult(delta)
                        ok = mult > 0.0
                    p = jnp.exp(jnp.where(ok, z, NEG_INF) - lse_i)
                    if not even:
                        p = p * mult
                    dp = _dot_nt(doi, vs[_blk(j), :])
                    ds = p * (dp - dvec)
                    dsb = (ds * scale).astype(BF16)
                    dk_acc[_blk(j), :] += _dot_tn(dsb, qi)
                    dv_acc[_blk(j), :] += _dot_tn(p.astype(BF16), doi)
                    if even:
                        dcft_ref[hh, :, _blk(j)] += -jnp.sum(ds, axis=0, keepdims=True)
                    return (dq + jnp.dot(dsb, kj, preferred_element_type=F32),
                            ds_rows + jnp.sum(ds, axis=1, keepdims=True))

                dq, ds_rows = lax.fori_loop(0, i + 1, kblock,
                                            (jnp.zeros((BLK, HEAD_DIM), F32), jnp.zeros((BLK, 1), F32)))
                dq_acc[_blk(i), :] = dq
                if even:
                    drow_ref[_blk(i), :] += jnp.where(lane == hh, ds_rows, 0.0)
                return carry

            lax.fori_loop(0, nq, qblock, 0)

        def sb_head():
            row = lax.broadcasted_iota(jnp.int32, (BLK, BLK), 0)
            col = lax.broadcasted_iota(jnp.int32, (BLK, BLK), 1)
            t_ex = jnp.where(row > col, 1.0, 0.0).astype(BF16)
            t_in = jnp.where(row >= col, 1.0, 0.0).astype(BF16)

            def qblock(i, carry):
                qi = qs[_blk(i), :]
                doi = dos[_blk(i), :]

                def e_total(jj, c):
                    run, tot = c
                    j = i - jj
                    z = _dot_nt(qi, ks[_blk(j), :]) * scale
                    a, m, _ = _sb_terms(z, _delta(i, j) > 0, t_ex, run)
                    e = _dot_nt(doi, vs[_blk(j), :]) * a
                    return run + jnp.sum(m, axis=1, keepdims=True), tot + jnp.sum(e, axis=1, keepdims=True)

                zero = jnp.zeros((BLK, 1), F32)
                _, e_tot = lax.fori_loop(0, i + 1, e_total, (zero, zero))

                def kblock(jj, c):
                    run, e_run, dq = c
                    j = i - jj
                    kj = ks[_blk(j), :]
                    z = _dot_nt(qi, kj) * scale
                    mask = _delta(i, j) > 0
                    a, m, lsig = _sb_terms(z, mask, t_ex, run)
                    sig = jnp.exp(lsig)
                    e = _dot_nt(doi, vs[_blk(j), :]) * a
                    e_before = e_tot - (_split_dot(e, t_in) + e_run)
                    dz = jnp.where(mask, e * (1.0 - sig) - sig * e_before, 0.0)
                    dzb = (dz * scale).astype(BF16)
                    dk_acc[_blk(j), :] += _dot_tn(dzb, qi)
                    dv_acc[_blk(j), :] += _dot_tn(a.astype(BF16), doi)
                    return (run + jnp.sum(m, axis=1, keepdims=True), e_run + jnp.sum(e, axis=1, keepdims=True),
                            dq + jnp.dot(dzb, kj, preferred_element_type=F32))

                _, _, dq = lax.fori_loop(0, i + 1, kblock, (zero, zero, jnp.zeros((BLK, HEAD_DIM), F32)))
                dq_acc[_blk(i), :] = dq
                return carry

            lax.fori_loop(0, nq, qblock, 0)

        if even:
            @pl.when(h < n_sb)
            def _():
                sb_head()

            @pl.when(h >= n_sb)
            def _():
                softmax_head(h - n_sb)

            dh_ref[0] = dq_acc[...].astype(BF16)
            dh_ref[1] = dk_acc[...].astype(BF16)
        else:
            softmax_head(h)
            dh_ref[0] = _rope_t(dq_acc[...], c_ref[...], s_ref[...]).astype(BF16)
            dh_ref[1] = _rope_t(dk_acc[...], c_ref[...], s_ref[...]).astype(BF16)
        dh_ref[2] = dv_acc[...].astype(BF16)

    head = lambda off: pl.BlockSpec((S, HEAD_DIM), lambda h, off=off: (0, off + h))
    full = pl.BlockSpec((S, LANES), lambda h: (0, 0))
    tfull = pl.BlockSpec((n_heads - n_sb, 1, S), lambda h: (0, 0, 0))
    dh_spec = pl.BlockSpec((3, S, HEAD_DIM), lambda h: (0, 0, h))
    dh_shape = jax.ShapeDtypeStruct((3, S, D), BF16)
    if even:
        extra_specs, extra = [full, tfull], (cf, cft)
        out_specs = [dh_spec, tfull, full]
        out_shape = [dh_shape, jax.ShapeDtypeStruct((n_heads - n_sb, 1, S), F32),
                     jax.ShapeDtypeStruct((S, LANES), F32)]
    else:
        extra_specs, extra = [full, full], (rope_c, rope_s)
        out_specs = [dh_spec]
        out_shape = [dh_shape]
    blk_bytes = 10 * _nbytes((S, HEAD_DIM), F32)
    scratch_bytes = 4 * _nbytes((S, HEAD_DIM), BF16) + 3 * _nbytes((S, HEAD_DIM), F32)
    return pl.pallas_call(
        body, name=name, grid=(n_heads,),
        in_specs=[head(0), head(n_heads), head(2 * n_heads), head(0), head(0), head(0)] + extra_specs,
        out_specs=out_specs, out_shape=out_shape,
        scratch_shapes=[pltpu.VMEM((S, HEAD_DIM), BF16)] * 4 + [pltpu.VMEM((S, HEAD_DIM), F32)] * 3,
        compiler_params=pltpu.CompilerParams(dimension_semantics=("arbitrary",),
                                             vmem_limit_bytes=_vmem_limit(blk_bytes, scratch_bytes)),
    )(hq, hq, hq, do, o, lse, *extra)


def _query_block(S):
    return min(512, S)


def _offsets(d, bq):
    row = jnp.arange(bq, dtype=jnp.int32)[:, None]
    col = jnp.arange(BLK, dtype=jnp.int32)[None, :]
    return d * BLK + row - col


def _causal_tables(bq, strict):
    r = bq // BLK
    tabs = []
    for d in range(-(r - 1), 1):
        delta = _offsets(d, bq)
        tabs.append(jnp.where((delta > 0) if strict else (delta >= 0), 1.0, 0.0))
    tabs.append(jnp.ones((bq, BLK), F32))
    return jnp.stack(tabs).astype(F32)


def _dilated_tables(bq):
    r = bq // BLK
    limit = sorted(w for w, _ in DILATED_PATTERNS)[-2]
    assert all(BLK % dil == 0 for _, dil in DILATED_PATTERNS)
    d_far = -(-(limit + BLK) // BLK)
    tabs = []
    for d in range(-(r - 1), d_far + 1):
        mult = _dilated_mult(_offsets(d, bq))
        tabs.append(jnp.where(mult > 0, jnp.log(jnp.maximum(mult, 1.0)), NEG_INF))
    return jnp.stack(tabs).astype(F32)


def _qblk(i, bq):
    return pl.ds(pl.multiple_of(i * bq, bq), bq)


def _sb_block(z, valid, t_ex, run):
    t = jnp.log1p(jnp.exp(-jnp.abs(z)))
    lsig = jnp.minimum(z, 0.0) - t
    m = -(jnp.maximum(z, 0.0) + t) * valid
    after = _split_dot(m, t_ex)
    a = jnp.exp(lsig + after + run) * valid
    return a, m, lsig


def _attn_fwd_wide(name, hq, layer_kind, n_heads, n_sb, cf=None, cft=None, rope_c=None, rope_s=None):
    S = hq.shape[0]
    D = n_heads * HEAD_DIM
    bq = _query_block(S)
    r = bq // BLK
    nq = S // bq
    scale = HEAD_DIM ** -0.5
    even = layer_kind == "even"
    if even:
        tabs = (jnp.where(_causal_tables(bq, False) > 0, 0.0, NEG_INF), _causal_tables(bq, True))
    else:
        tabs = (_dilated_tables(bq),)
    n_tab = tabs[0].shape[0]

    def body(*refs):
        if even:
            q_ref, k_ref, v_ref, cf_ref, cft_ref, bias_ref, valid_ref, o_ref, ob_ref, lse_ref, qs, ks, vs = refs
        else:
            q_ref, k_ref, v_ref, c_ref, s_ref, bias_ref, o_ref, ob_ref, lse_ref, qs, ks, vs = refs
        h = pl.program_id(0)
        if even:
            qs[...] = q_ref[...].astype(BF16)
            ks[...] = k_ref[...].astype(BF16)
        else:
            qs[...] = _rope(q_ref[...], c_ref[...], s_ref[...]).astype(BF16)
            ks[...] = _rope(k_ref[...], c_ref[...], s_ref[...]).astype(BF16)
        vs[...] = v_ref[...].astype(BF16)

        def softmax_head(hh):
            def qblock(i, carry):
                qi = qs[_qblk(i, bq), :]
                if even:
                    lane = lax.broadcasted_iota(jnp.int32, (bq, LANES), 1)
                    cfq = jnp.sum(jnp.where(lane == hh, cf_ref[_qblk(i, bq), :], 0.0), axis=1, keepdims=True)

                def kblock(j, c):
                    m_run, l_run, acc = c
                    z = _dot_nt(qi, ks[_blk(j), :]) * scale + bias_ref[jnp.minimum(r * i - j + (r - 1), n_tab - 1)]
                    if even:
                        z = z + (cfq - cft_ref[hh, :, _blk(j)])
                    m_new = jnp.maximum(m_run, jnp.max(z, axis=1, keepdims=True))
                    p = jnp.exp(z - m_new)
                    alpha = jnp.exp(m_run - m_new)
                    l_new = alpha * l_run + jnp.sum(p, axis=1, keepdims=True)
                    acc = alpha * acc + jnp.dot(p.astype(BF16), vs[_blk(j), :], preferred_element_type=F32)
                    return m_new, l_new, acc

                init = (jnp.full((bq, 1), NEG_INF, F32), jnp.zeros((bq, 1), F32), jnp.zeros((bq, HEAD_DIM), F32))
                m_run, l_run, acc = lax.fori_loop(0, r * (i + 1), kblock, init)
                o = acc / l_run
                o_ref[_qblk(i, bq), :] = o
                ob_ref[_qblk(i, bq), :] = o.astype(BF16)
                lse_ref[_qblk(i, bq), :] = jnp.broadcast_to(m_run + jnp.log(l_run), (bq, HEAD_DIM))
                return carry

            lax.fori_loop(0, nq, qblock, 0)

        def sb_head():
            row = lax.broadcasted_iota(jnp.int32, (BLK, BLK), 0)
            col = lax.broadcasted_iota(jnp.int32, (BLK, BLK), 1)
            t_ex = jnp.where(row > col, 1.0, 0.0).astype(BF16)

            def qblock(i, carry):
                qi = qs[_qblk(i, bq), :]

                def kblock(jj, c):
                    run, acc = c
                    j = r * (i + 1) - 1 - jj
                    z = _dot_nt(qi, ks[_blk(j), :]) * scale
                    a, m, _ = _sb_block(z, valid_ref[jnp.minimum(r * i - j + (r - 1), r)], t_ex, run)
                    acc = acc + jnp.dot(a.astype(BF16), vs[_blk(j), :], preferred_element_type=F32)
                    return run + jnp.sum(m, axis=1, keepdims=True), acc

                init = (jnp.zeros((bq, 1), F32), jnp.zeros((bq, HEAD_DIM), F32))
                _, acc = lax.fori_loop(0, r * (i + 1), kblock, init)
                o_ref[_qblk(i, bq), :] = acc
                ob_ref[_qblk(i, bq), :] = acc.astype(BF16)
                lse_ref[_qblk(i, bq), :] = jnp.zeros((bq, HEAD_DIM), F32)
                return carry

            lax.fori_loop(0, nq, qblock, 0)

        if even:
            @pl.when(h < n_sb)
            def _():
                sb_head()

            @pl.when(h >= n_sb)
            def _():
                softmax_head(h - n_sb)
        else:
            softmax_head(h)

    head = lambda off: pl.BlockSpec((S, HEAD_DIM), lambda h, off=off: (0, off + h))
    full = pl.BlockSpec((S, LANES), lambda h: (0, 0))
    tab_specs = [pl.BlockSpec(t.shape, lambda h: (0, 0, 0)) for t in tabs]
    if even:
        extra_specs = [full, pl.BlockSpec(cft.shape, lambda h: (0, 0, 0))] + tab_specs
        extra = (cf, cft) + tabs
    else:
        extra_specs = [full, full] + tab_specs
        extra = (rope_c, rope_s) + tabs
    blk_bytes = 8 * _nbytes((S, HEAD_DIM), F32) + sum(_nbytes(t.shape, F32) for t in tabs)
    return pl.pallas_call(
        body, name=name, grid=(n_heads,),
        in_specs=[head(0), head(n_heads), head(2 * n_heads)] + extra_specs,
        out_specs=[head(0), head(0), head(0)],
        out_shape=[jax.ShapeDtypeStruct((S, D), F32), jax.ShapeDtypeStruct((S, D), BF16),
                   jax.ShapeDtypeStruct((S, D), F32)],
        scratch_shapes=[pltpu.VMEM((S, HEAD_DIM), BF16)] * 3,
        compiler_params=pltpu.CompilerParams(dimension_semantics=("arbitrary",),
                                             vmem_limit_bytes=_vmem_limit(blk_bytes, 3 * _nbytes((S, HEAD_DIM), BF16))),
    )(hq, hq, hq, *extra)


def _attn_bwd_wide(name, hq, do, o, lse, layer_kind, n_heads, n_sb, cf=None, cft=None, rope_c=None, rope_s=None):
    S = hq.shape[0]
    D = n_heads * HEAD_DIM
    bq = _query_block(S)
    r = bq // BLK
    nq = S // bq
    scale = HEAD_DIM ** -0.5
    even = layer_kind == "even"
    if even:
        tabs = (jnp.where(_causal_tables(bq, False) > 0, 0.0, NEG_INF), _causal_tables(bq, True))
    else:
        tabs = (_dilated_tables(bq),)
    n_tab = tabs[0].shape[0]

    def body(*refs):
        if even:
            (q_ref, k_ref, v_ref, do_ref, o_ref, lse_ref, cf_ref, cft_ref, bias_ref, valid_ref,
             dh_ref, dcft_ref, drow_ref, qs, ks, vs, dos, dq_acc, dk_acc, dv_acc) = refs
        else:
            (q_ref, k_ref, v_ref, do_ref, o_ref, lse_ref, c_ref, s_ref, bias_ref,
             dh_ref, qs, ks, vs, dos, dq_acc, dk_acc, dv_acc) = refs
        h = pl.program_id(0)
        if even:
            qs[...] = q_ref[...].astype(BF16)
            ks[...] = k_ref[...].astype(BF16)

            @pl.when(h == 0)
            def _():
                dcft_ref[...] = jnp.zeros_like(dcft_ref)
                drow_ref[...] = jnp.zeros_like(drow_ref)
        else:
            qs[...] = _rope(q_ref[...], c_ref[...], s_ref[...]).astype(BF16)
            ks[...] = _rope(k_ref[...], c_ref[...], s_ref[...]).astype(BF16)
        vs[...] = v_ref[...].astype(BF16)
        dos[...] = do_ref[...].astype(BF16)
        dk_acc[...] = jnp.zeros_like(dk_acc)
        dv_acc[...] = jnp.zeros_like(dv_acc)

        def softmax_head(hh):
            def qblock(i, carry):
                qi = qs[_qblk(i, bq), :]
                doi = dos[_qblk(i, bq), :]
                dvec = jnp.sum(do_ref[_qblk(i, bq), :] * o_ref[_qblk(i, bq), :], axis=1, keepdims=True)
                lse_i = jnp.max(lse_ref[_qblk(i, bq), :], axis=1, keepdims=True)
                if even:
                    lane = lax.broadcasted_iota(jnp.int32, (bq, LANES), 1)
                    cfq = jnp.sum(jnp.where(lane == hh, cf_ref[_qblk(i, bq), :], 0.0), axis=1, keepdims=True)

                def kblock(j, c):
                    dq, ds_rows = c
                    kj = ks[_blk(j), :]
                    z = _dot_nt(qi, kj) * scale + bias_ref[jnp.minimum(r * i - j + (r - 1), n_tab - 1)]
                    if even:
                        z = z + (cfq - cft_ref[hh, :, _blk(j)])
                    p = jnp.exp(z - lse_i)
                    dp = _dot_nt(doi, vs[_blk(j), :])
                    ds = p * (dp - dvec)
                    dsb = (ds * scale).astype(BF16)
                    dk_acc[_blk(j), :] += _dot_tn(dsb, qi)
                    dv_acc[_blk(j), :] += _dot_tn(p.astype(BF16), doi)
                    if even:
                        dcft_ref[hh, :, _blk(j)] += -jnp.sum(ds, axis=0, keepdims=True)
                    return (dq + jnp.dot(dsb, kj, preferred_element_type=F32),
                            ds_rows + jnp.sum(ds, axis=1, keepdims=True))

                dq, ds_rows = lax.fori_loop(0, r * (i + 1), kblock,
                                            (jnp.zeros((bq, HEAD_DIM), F32), jnp.zeros((bq, 1), F32)))
                dq_acc[_qblk(i, bq), :] = dq
                if even:
                    drow_ref[_qblk(i, bq), :] += jnp.where(lane == hh, ds_rows, 0.0)
                return carry

            lax.fori_loop(0, nq, qblock, 0)

        def sb_head():
            row = lax.broadcasted_iota(jnp.int32, (BLK, BLK), 0)
            col = lax.broadcasted_iota(jnp.int32, (BLK, BLK), 1)
            t_ex = jnp.where(row > col, 1.0, 0.0).astype(BF16)
            t_in = jnp.where(row >= col, 1.0, 0.0).astype(BF16)

            def qblock(i, carry):
                qi = qs[_qblk(i, bq), :]
                doi = dos[_qblk(i, bq), :]
                nkb = r * (i + 1)

                def e_total(jj, c):
                    run, tot = c
                    j = nkb - 1 - jj
                    z = _dot_nt(qi, ks[_blk(j), :]) * scale
                    a, m, _ = _sb_block(z, valid_ref[jnp.minimum(r * i - j + (r - 1), r)], t_ex, run)
                    e = _dot_nt(doi, vs[_blk(j), :]) * a
                    return run + jnp.sum(m, axis=1, keepdims=True), tot + jnp.sum(e, axis=1, keepdims=True)

                zero = jnp.zeros((bq, 1), F32)
                _, e_tot = lax.fori_loop(0, nkb, e_total, (zero, zero))

                def kblock(jj, c):
                    run, e_run, dq = c
                    j = nkb - 1 - jj
                    kj = ks[_blk(j), :]
                    z = _dot_nt(qi, kj) * scale
                    valid = valid_ref[jnp.minimum(r * i - j + (r - 1), r)]
                    a, m, lsig = _sb_block(z, valid, t_ex, run)
                    sig = jnp.exp(lsig)
                    e = _dot_nt(doi, vs[_blk(j), :]) * a
                    e_before = e_tot - (_split_dot(e, t_in) + e_run)
                    dz = (e * (1.0 - sig) - sig * e_before) * valid
                    dzb = (dz * scale).astype(BF16)
                    dk_acc[_blk(j), :] += _dot_tn(dzb, qi)
                    dv_acc[_blk(j), :] += _dot_tn(a.astype(BF16), doi)
                    return (run + jnp.sum(m, axis=1, keepdims=True), e_run + jnp.sum(e, axis=1, keepdims=True),
                            dq + jnp.dot(dzb, kj, preferred_element_type=F32))

                _, _, dq = lax.fori_loop(0, nkb, kblock, (zero, zero, jnp.zeros((bq, HEAD_DIM), F32)))
                dq_acc[_qblk(i, bq), :] = dq
                return carry

            lax.fori_loop(0, nq, qblock, 0)

        if even:
            @pl.when(h < n_sb)
            def _():
                sb_head()

            @pl.when(h >= n_sb)
            def _():
                softmax_head(h - n_sb)

            dh_ref[0] = dq_acc[...].astype(BF16)
            dh_ref[1] = dk_acc[...].astype(BF16)
        else:
            softmax_head(h)
            dh_ref[0] = _rope_t(dq_acc[...], c_ref[...], s_ref[...]).astype(BF16)
            dh_ref[1] = _rope_t(dk_acc[...], c_ref[...], s_ref[...]).astype(BF16)
        dh_ref[2] = dv_acc[...].astype(BF16)

    head = lambda off: pl.BlockSpec((S, HEAD_DIM), lambda h, off=off: (0, off + h))
    full = pl.BlockSpec((S, LANES), lambda h: (0, 0))
    tfull = pl.BlockSpec((n_heads - n_sb, 1, S), lambda h: (0, 0, 0))
    tab_specs = [pl.BlockSpec(t.shape, lambda h: (0, 0, 0)) for t in tabs]
    dh_spec = pl.BlockSpec((3, S, HEAD_DIM), lambda h: (0, 0, h))
    dh_shape = jax.ShapeDtypeStruct((3, S, D), BF16)
    if even:
        extra_specs, extra = [full, tfull] + tab_specs, (cf, cft) + tabs
        out_specs = [dh_spec, tfull, full]
        out_shape = [dh_shape, jax.ShapeDtypeStruct((n_heads - n_sb, 1, S), F32),
                     jax.ShapeDtypeStruct((S, LANES), F32)]
    else:
        extra_specs, extra = [full, full] + tab_specs, (rope_c, rope_s) + tabs
        out_specs = [dh_spec]
        out_shape = [dh_shape]
    blk_bytes = 10 * _nbytes((S, HEAD_DIM), F32) + sum(_nbytes(t.shape, F32) for t in tabs)
    scratch_bytes = 4 * _nbytes((S, HEAD_DIM), BF16) + 3 * _nbytes((S, HEAD_DIM), F32)
    return pl.pallas_call(
        body, name=name, grid=(n_heads,),
        in_specs=[head(0), head(n_heads), head(2 * n_heads), head(0), head(0), head(0)] + extra_specs,
        out_specs=out_specs, out_shape=out_shape,
        scratch_shapes=[pltpu.VMEM((S, HEAD_DIM), BF16)] * 4 + [pltpu.VMEM((S, HEAD_DIM), F32)] * 3,
        compiler_params=pltpu.CompilerParams(dimension_semantics=("arbitrary",),
                                             vmem_limit_bytes=_vmem_limit(blk_bytes, scratch_bytes)),
    )(hq, hq, hq, do, o, lse, *extra)


def _ffn_fwd(tag, x, g, wgu, wd, fs):
    n = _rms_fwd(tag + "_norm", x, g)
    gu = _mm(tag + "_gu", n, wgu, "nn", F32)
    h = _swiglu_fwd(tag + "_act", gu, fs)
    y = _mm(tag + "_down", h, wd, "nn", F32, res=x, alpha=0.5)
    return y, (x, g, n, gu, h)


def _ffn_bwd(tag, dx, dxb, wgu, wd, fs, saved, after=None):
    x, g, n, gu, h = saved
    dh = _mm(tag + "_dh", dxb, wd, "nt", F32, alpha=0.5, after=after)
    dwd = _mm(tag + "_dwd", h, dxb, "tn", F32, alpha=0.5)
    dgu = _swiglu_bwd(tag + "_dact", dh, gu, fs)
    dwgu = _mm(tag + "_dwgu", n, dgu, "tn", F32)
    dn = _mm(tag + "_dn", dgu, wgu, "nt", F32)
    dx_in, dxb_in, dg = _rms_bwd(tag + "_dnorm", dn, x, g, dx)
    return dx_in, dxb_in, dg, dwgu, dwd


def _mixer_fwd(tag, kind, x, g, wqkv, wout, n_heads, n_sb, wf=None, bf=None, rope=None):
    n = _rms_fwd(tag + "_norm", x, g)
    hq = _mm(tag + "_qkv", n, wqkv, "nn", F32)
    if kind == "even":
        hf = _mm(tag + "_gate", n, wf, "nn", F32)
        cf, cft = _gate_fwd(tag + "_cumgate", hf, bf)
        cft = cft[:n_heads - n_sb].reshape(n_heads - n_sb, 1, -1)
        o, ob, lse = _attn_fwd_wide(tag + "_attn", hq, kind, n_heads, n_sb, cf=cf, cft=cft)
    else:
        hf = cf = cft = None
        o, ob, lse = _attn_fwd_wide(tag + "_attn", hq, kind, n_heads, n_sb, rope_c=rope[0], rope_s=rope[1])
    y = _mm(tag + "_out", ob, wout, "nn", F32, res=x)
    return y, (x, g, n, hq, hf, cf, cft, o, ob, lse)


def _mixer_bwd(tag, kind, dx, dxb, wqkv, wout, n_heads, n_sb, saved, wf=None, bf=None, rope=None, after=None):
    x, g, n, hq, hf, cf, cft, o, ob, lse = saved
    do = _mm(tag + "_do", dxb, wout, "nt", F32, after=after)
    dwout = _mm(tag + "_dwout", ob, dxb, "tn", F32)
    if kind == "even":
        dh3, dcft, drow = _attn_bwd_wide(tag + "_dattn", hq, do, o, lse, kind, n_heads, n_sb, cf=cf, cft=cft)
    else:
        (dh3,) = _attn_bwd_wide(tag + "_dattn", hq, do, o, lse, kind, n_heads, n_sb, rope_c=rope[0], rope_s=rope[1])
    dwqkv = _mm(tag + "_dwqkv", n, dh3, "tn", F32)
    dn = _mm(tag + "_dn", dh3, wqkv, "nt", F32)
    dwf = db = None
    if kind == "even":
        n_fox = n_heads - n_sb
        dcft = jnp.pad(dcft.reshape(n_fox, -1), ((0, LANES - n_fox), (0, 0)))
        dhf, db = _gate_bwd(tag + "_dcumgate", dcft, drow, hf, bf)
        dwf = _mm(tag + "_dwf", n, dhf, "tn", F32)
        dn = _mm(tag + "_dn_gate", dhf, wf, "nt", F32, res=dn)
    dx_in, dxb_in, dg = _rms_bwd(tag + "_dnorm", dn, x, g, dx)
    return dx_in, dxb_in, dg, dwqkv, dwout, dwf, db


def _local_step(x, target, w, fs, n_heads, n_sb, emit=None):
    S, D = x.shape
    rope = _rope_tables(S)
    kinds = ("even", "odd")
    saved = []
    h = x
    if callable(w):
        fetch, w = w, {"norm_g": w("norm_g", None), "final_g": w("final_g", None),
                       "wgu1": [None, None], "wd1": [None, None], "wgu2": [None, None], "wd2": [None, None]}
    else:
        fetch = None
    for l, kind in enumerate(kinds):
        ng = [w["norm_g"][l, i][None, :] for i in range(3)]
        if fetch:
            w["wgu1"][l], w["wd1"][l] = fetch(("ffn1", l), h)
        h, s1 = _ffn_fwd(f"l{l}_ffn1", h, ng[0], w["wgu1"][l], w["wd1"][l], fs)
        if fetch:
            w.update(fetch(("mix", l), h))
        if kind == "even":
            h, s2 = _mixer_fwd(f"l{l}_mix", kind, h, ng[1], w["wqkv_e"], w["wout_e"], n_heads, n_sb,
                               wf=w["wf"], bf=w["bf"])
        else:
            h, s2 = _mixer_fwd(f"l{l}_mix", kind, h, ng[1], w["wqkv_o"], w["wout_o"], n_heads, n_sb, rope=rope)
        if fetch:
            w["wgu2"][l], w["wd2"][l] = fetch(("ffn2", l), h)
        h, s3 = _ffn_fwd(f"l{l}_ffn2", h, ng[2], w["wgu2"][l], w["wd2"][l], fs)
        saved.append((s1, s2, s3))

    dx, dxb, dfinal, loss = _loss_head("loss_head", h, w["final_g"], target)
    grads = {"dfinal": dfinal, "dnorm": [[None] * 3 for _ in kinds],
             "dwgu1": [None, None], "dwd1": [None, None], "dwgu2": [None, None], "dwd2": [None, None]}
    token = None
    for l in (1, 0):
        kind = kinds[l]
        s1, s2, s3 = saved[l]
        dx, dxb, dg, grads["dwgu2"][l], grads["dwd2"][l] = _ffn_bwd(
            f"l{l}_ffn2", dx, dxb, w["wgu2"][l], w["wd2"][l], fs, s3, after=token)
        grads["dnorm"][l][2] = dg
        if emit:
            token = emit(("ffn2", l), grads, dx)
        if kind == "even":
            dx, dxb, dg, grads["dwqkv_e"], grads["dwout_e"], grads["dwf"], grads["db"] = _mixer_bwd(
                f"l{l}_mix", kind, dx, dxb, w["wqkv_e"], w["wout_e"], n_heads, n_sb, s2, wf=w["wf"], bf=w["bf"],
                after=token)
        else:
            dx, dxb, dg, grads["dwqkv_o"], grads["dwout_o"], _, _ = _mixer_bwd(
                f"l{l}_mix", kind, dx, dxb, w["wqkv_o"], w["wout_o"], n_heads, n_sb, s2, rope=rope, after=token)
        grads["dnorm"][l][1] = dg
        if emit:
            token = emit(("mix", l), grads, dx)
        dx, dxb, dg, grads["dwgu1"][l], grads["dwd1"][l] = _ffn_bwd(
            f"l{l}_ffn1", dx, dxb, w["wgu1"][l], w["wd1"][l], fs, s1, after=token)
        grads["dnorm"][l][0] = dg
        if emit:
            token = emit(("ffn1", l), grads, dx)
    return loss, dx, grads


def _cast_into(name, shard, layer, chip, full_shape, place, full=None):
    R, C = shard.shape[-2:]
    tr = _row_tile(R, 512, step=16)
    if layer is None:
        in_spec = pl.BlockSpec((tr, C), lambda i, k: (i, 0))
    else:
        in_spec = pl.BlockSpec((None, tr, C), lambda i, k: (layer, i, 0))
    lead = (None,) * (len(full_shape) - 2)
    out_spec = pl.BlockSpec(lead + (tr, C), lambda i, k: place(i, k[0]))

    def body(*refs):
        k_ref, w_ref = refs[:2]
        o_ref = refs[-1]
        o_ref[...] = w_ref[...].astype(BF16)

    in_specs = [in_spec] + ([_ANY] if full is not None else [])
    args = (chip, shard) + ((full,) if full is not None else ())
    grid_spec = pltpu.PrefetchScalarGridSpec(num_scalar_prefetch=1, grid=(R // tr,), in_specs=in_specs, out_specs=out_spec)
    return pl.pallas_call(
        body, name=name, grid_spec=grid_spec, out_shape=jax.ShapeDtypeStruct(full_shape, BF16),
        input_output_aliases={2: 0} if full is not None else {},
        compiler_params=pltpu.CompilerParams(dimension_semantics=("arbitrary",)),
    )(*args)


def _region_shape(grad, kind):
    if kind == "lead":
        return grad.shape[1] // N_CORES, grad.shape[2]
    rows, cols = grad.shape
    if kind == "cols":
        return rows // N_CORES, cols // N_CHIPS
    return rows // (N_CHIPS * N_CORES), cols


def _region_add(name, grad, kind, landed, core):
    rh, cw = _region_shape(grad, kind)
    tr = _row_tile(rh, 256, step=16)
    nrb = rh // tr
    if kind == "cols":
        g_spec = pl.BlockSpec((tr, cw), lambda k, r, c: (c[0] * nrb + r, k))
    elif kind == "rows":
        g_spec = pl.BlockSpec((tr, cw), lambda k, r, c: ((N_CORES * k + c[0]) * nrb + r, 0))
    else:
        g_spec = pl.BlockSpec((None, tr, cw), lambda k, r, c: (k, c[0] * nrb + r, 0))
    l_spec = pl.BlockSpec((None, tr, cw), lambda k, r, c: (k, r, 0))

    def body(c_ref, g_ref, l_ref, o_ref):
        o_ref[...] = (g_ref[...] + l_ref[...]).astype(BF16)

    grid_spec = pltpu.PrefetchScalarGridSpec(
        num_scalar_prefetch=1, grid=(N_CHIPS, nrb), in_specs=[g_spec, l_spec], out_specs=l_spec)
    return pl.pallas_call(
        body, name=name, grid_spec=grid_spec, out_shape=jax.ShapeDtypeStruct(landed.shape, BF16),
        compiler_params=pltpu.CompilerParams(dimension_semantics=("parallel", "parallel"),
                                             vmem_limit_bytes=_vmem_limit(3 * _nbytes((tr, cw), F32))),
    )(core, grad, landed)


def _chip_sum(name, pair, landed, pos):
    _, rh, cw = pair.shape
    tr = _row_tile(rh, max(16, 2**20 // (cw * 4)), step=16)
    nrb = rh // tr

    def body(p_ref, own_ref, l_ref, o_ref):
        acc = own_ref[...].astype(F32)
        for s in range(N_CHIPS - 1):
            acc = acc + l_ref[s].astype(F32)
        o_ref[...] = acc

    grid_spec = pltpu.PrefetchScalarGridSpec(
        num_scalar_prefetch=1, grid=(nrb,),
        in_specs=[pl.BlockSpec((None, tr, cw), lambda r, p: (p[0], r, 0)),
                  pl.BlockSpec((N_CHIPS - 1, tr, cw), lambda r, p: (0, r, 0))],
        out_specs=pl.BlockSpec((tr, cw), lambda r, p: (p[1] * nrb + r, 0)))
    return pl.pallas_call(
        body, name=name, grid_spec=grid_spec, out_shape=jax.ShapeDtypeStruct((N_CORES * rh, cw), F32),
        compiler_params=pltpu.CompilerParams(dimension_semantics=("arbitrary",)),
    )(pos, pair, landed)


def _sum_leading(name, parts):
    n, R, C = parts.shape
    tr = _row_tile(R, max(8, (2**20 // (C * 4)) // 8 * 8))

    def body(p_ref, o_ref):
        acc = p_ref[0]
        for s in range(1, n):
            acc = acc + p_ref[s]
        o_ref[...] = acc

    return pl.pallas_call(
        body, name=name, grid=(R // tr,),
        in_specs=[pl.BlockSpec((n, tr, C), lambda i: (0, i, 0))],
        out_specs=pl.BlockSpec((tr, C), lambda i: (i, 0)),
        out_shape=jax.ShapeDtypeStruct((R, C), F32),
        compiler_params=pltpu.CompilerParams(dimension_semantics=("parallel",)),
    )(parts)


def _adamw(name, w, g, m, v):
    shape = w.shape
    to2d = lambda t: t.reshape(-1, shape[-1]) if t.ndim > 1 else t.reshape(1, -1)
    w2, g2, m2, v2 = (to2d(t) for t in (w, g, m, v))
    R, C = w2.shape
    tr = _row_tile(R, 256)

    def body(w_ref, g_ref, m_ref, v_ref, d_ref, nm_ref, nv_ref):
        gv = g_ref[...]
        nm = ADAM_B1 * m_ref[...] + (1.0 - ADAM_B1) * gv
        nv = ADAM_B2 * v_ref[...] + (1.0 - ADAM_B2) * (gv * gv)
        m_hat = nm / (1.0 - ADAM_B1 ** ADAM_STEP)
        v_hat = nv / (1.0 - ADAM_B2 ** ADAM_STEP)
        d_ref[...] = -ADAM_LR * (m_hat / (jnp.sqrt(v_hat) + ADAM_EPS) + ADAM_WD * w_ref[...])
        nm_ref[...] = nm
        nv_ref[...] = nv

    spec = pl.BlockSpec((tr, C), lambda i: (i, 0))
    sds = jax.ShapeDtypeStruct((R, C), F32)
    d, nm, nv = pl.pallas_call(
        body, name=name, grid=(R // tr,), in_specs=[spec] * 4, out_specs=[spec] * 3, out_shape=[sds] * 3,
        compiler_params=pltpu.CompilerParams(dimension_semantics=("parallel",),
                                             vmem_limit_bytes=_vmem_limit(7 * _nbytes((tr, C), F32))),
    )(w2, g2, m2, v2)
    return d.reshape(shape), nm.reshape(shape), nv.reshape(shape)


_ANY = pl.BlockSpec(memory_space=pl.ANY)


def _mesh_pos():
    return lax.axis_index("x"), lax.axis_index("y"), lax.axis_index("c")


def _other_chips(x, y):
    return [(1 - x, y), (x, 1 - y), (1 - x, 1 - y)]


def _gather_over_chips(name, fulls, views):
    n = len(views)
    nf = len(fulls)

    def body(*refs):
        full = refs[nf:2 * nf]
        ici_send, ici_recv, d2d_send, d2d_recv = refs[2 * nf:]
        x, y, c = _mesh_pos()
        chips = _other_chips(x, y)
        mine = 2 * x + y
        sibling = (x, y, 1 - c)

        def ici(a, p, k):
            i, view, _ = views[a]
            part = view(full[i], k, c)
            return pltpu.make_async_remote_copy(
                src_ref=part, dst_ref=part, send_sem=ici_send.at[a, p], recv_sem=ici_recv.at[a, p],
                device_id=(*chips[p], c), device_id_type=MESH)

        def d2d(a, p, h):
            i, view, _ = views[a]
            px, py = chips[p]
            part = view(full[i], 2 * px + py, h)
            return pltpu.make_async_remote_copy(
                src_ref=part, dst_ref=part, send_sem=d2d_send.at[a, p], recv_sem=d2d_recv.at[a, p],
                device_id=sibling, device_id_type=MESH)

        sends = [ici(a, p, mine) for a in range(n) for p in range(3)]
        for cp in sends:
            cp.start()
        passed = []
        for a in range(n):
            for p, (px, py) in enumerate(chips):
                ici(a, p, 2 * px + py).wait_recv()
                if views[a][2]:
                    fwd = d2d(a, p, c)
                    fwd.start()
                    passed.append(fwd)
        for a in range(n):
            if views[a][2]:
                for p in range(3):
                    d2d(a, p, 1 - c).wait_recv()
        for cp in sends + passed:
            cp.wait_send()

    return pl.pallas_call(
        body, name=name, in_specs=[_ANY] * nf, out_specs=[_ANY] * nf,
        out_shape=[jax.ShapeDtypeStruct(f.shape, f.dtype) for f in fulls],
        input_output_aliases={i: i for i in range(nf)},
        scratch_shapes=[pltpu.SemaphoreType.DMA((n, 3))] * 4,
        compiler_params=pltpu.CompilerParams(has_side_effects=True),
    )(*fulls)


_HBM = pl.BlockSpec(memory_space=pltpu.HBM)
_SEM = pl.BlockSpec(memory_space=pltpu.SEMAPHORE)


def _in_hbm(arrays):
    return [pltpu.with_memory_space_constraint(a, pltpu.HBM) for a in arrays]


def _gather_start(name, fulls, views):
    nf = len(fulls)
    ng = 1 + max(g for _, _, g in views)

    def body(*refs):
        full = refs[nf:2 * nf]
        send_sems, recv_sems = refs[2 * nf:2 * nf + ng], refs[2 * nf + ng:]
        x, y, c = _mesh_pos()
        chips = _other_chips(x, y)
        for i, view, g in views:
            part = view(full[i], 2 * x + y, c)
            for px, py in chips:
                pltpu.make_async_remote_copy(
                    src_ref=part, dst_ref=part, send_sem=send_sems[g], recv_sem=recv_sems[g],
                    device_id=(px, py, c), device_id_type=MESH).start()

    outs = pl.pallas_call(
        body, name=name, in_specs=[_HBM] * nf, out_specs=[_HBM] * nf + [_SEM] * (2 * ng),
        out_shape=[pltpu.HBM(f.shape, f.dtype) for f in fulls] + [pltpu.SemaphoreType.DMA(())] * (2 * ng),
        input_output_aliases={i: i for i in range(nf)},
        compiler_params=pltpu.CompilerParams(has_side_effects=pltpu.SideEffectType.DATAFLOW_SIDE_EFFECTING),
    )(*_in_hbm(fulls))
    return list(outs[:nf]), list(outs[nf:nf + ng]), list(outs[nf + ng:])


def _gather_wait(name, fulls, views, send_sem, recv_sem, after):
    nf = len(fulls)

    def body(*refs):
        send_ref, recv_ref = refs[nf], refs[nf + 1]
        full = refs[nf + 3:]
        x, y, c = _mesh_pos()
        copies = [pltpu.make_async_remote_copy(
            src_ref=view(full[i], 2 * x + y, c), dst_ref=view(full[i], 2 * px + py, c),
            send_sem=send_ref, recv_sem=recv_ref, device_id=(px, py, c), device_id_type=MESH)
            for i, view in views for px, py in _other_chips(x, y)]
        for cp in copies:
            cp.wait_send()
        for cp in copies:
            cp.wait_recv()

    outs = pl.pallas_call(
        body, name=name, in_specs=[_HBM] * nf + [_SEM, _SEM, _ANY], out_specs=[_HBM] * nf,
        out_shape=[pltpu.HBM(f.shape, f.dtype) for f in fulls],
        input_output_aliases={i: i for i in range(nf)},
        compiler_params=pltpu.CompilerParams(has_side_effects=pltpu.SideEffectType.DATAFLOW_SIDE_EFFECTING),
    )(*fulls, send_sem, recv_sem, after)
    return list(outs)


def _forward_to_sibling(name, fulls, views):
    n, nf = len(views), len(fulls)

    def body(*refs):
        full = refs[nf:2 * nf]
        send_sems, recv_sems = refs[2 * nf:]
        x, y, c = _mesh_pos()
        chips = _other_chips(x, y)

        def copy(a, p, h):
            i, view = views[a]
            px, py = chips[p]
            part = view(full[i], 2 * px + py, h)
            return pltpu.make_async_remote_copy(
                src_ref=part, dst_ref=part, send_sem=send_sems.at[a, p], recv_sem=recv_sems.at[a, p],
                device_id=(x, y, 1 - c), device_id_type=MESH)

        sends = [copy(a, p, c) for a in range(n) for p in range(3)]
        for cp in sends:
            cp.start()
        for a in range(n):
            for p in range(3):
                copy(a, p, 1 - c).wait_recv()
        for cp in sends:
            cp.wait_send()

    return pl.pallas_call(
        body, name=name, in_specs=[_ANY] * nf, out_specs=[_ANY] * nf,
        out_shape=[jax.ShapeDtypeStruct(f.shape, f.dtype) for f in fulls],
        input_output_aliases={i: i for i in range(nf)},
        scratch_shapes=[pltpu.SemaphoreType.DMA((n, 3))] * 2,
        compiler_params=pltpu.CompilerParams(has_side_effects=True),
    )(*fulls)


def _region_view(ref, kind, k, c):
    if kind == "lead":
        rh = ref.shape[1] // N_CORES
        return ref.at[k, pl.ds(pl.multiple_of(c * rh, 8), rh), :]
    rows, cols = ref.shape
    if kind == "cols":
        rh, cw = rows // N_CORES, cols // N_CHIPS
        return ref.at[pl.ds(pl.multiple_of(c * rh, 8), rh), pl.ds(k * cw, cw)]
    rh = rows // (N_CHIPS * N_CORES)
    return ref.at[pl.ds(pl.multiple_of((N_CORES * k + c) * rh, 8), rh), :]


def _send_to_sibling(name, grads, kinds):
    n = len(grads)
    shapes = [jax.ShapeDtypeStruct((N_CHIPS,) + _region_shape(g, kd), F32) for g, kd in zip(grads, kinds)]

    def body(*refs):
        g_ref, land = refs[:n], refs[n:2 * n]
        send_sems, recv_sems = refs[2 * n:]
        x, y, c = _mesh_pos()
        copies = []
        for a in range(n):
            for k in range(N_CHIPS):
                cp = pltpu.make_async_remote_copy(
                    src_ref=_region_view(g_ref[a], kinds[a], k, 1 - c), dst_ref=land[a].at[k],
                    send_sem=send_sems.at[a, k], recv_sem=recv_sems.at[a, k],
                    device_id=(x, y, 1 - c), device_id_type=MESH)
                cp.start()
                copies.append(cp)
        for cp in copies:
            cp.wait_recv()
        for cp in copies:
            cp.wait_send()

    return pl.pallas_call(
        body, name=name, in_specs=[_ANY] * n, out_specs=[_ANY] * n, out_shape=shapes,
        scratch_shapes=[pltpu.SemaphoreType.DMA((n, N_CHIPS)), pltpu.SemaphoreType.DMA((n, N_CHIPS))],
        compiler_params=pltpu.CompilerParams(has_side_effects=True),
    )(*grads)


def _scatter_over_chips(name, pair_sums):
    n = len(pair_sums)

    def body(*refs):
        p_ref, land = refs[:n], refs[n:2 * n]
        send_sems, recv_sems = refs[2 * n:]
        x, y, c = _mesh_pos()
        chips = _other_chips(x, y)
        sends = []
        for a in range(n):
            for p, (px, py) in enumerate(chips):
                cp = pltpu.make_async_remote_copy(
                    src_ref=p_ref[a].at[2 * px + py], dst_ref=land[a].at[p], send_sem=send_sems.at[a, p],
                    recv_sem=recv_sems.at[a, p], device_id=(px, py, c), device_id_type=MESH)
                cp.start()
                sends.append(cp)
        for cp in sends:
            cp.wait_recv()
        for cp in sends:
            cp.wait_send()

    return pl.pallas_call(
        body, name=name, in_specs=[_ANY] * n, out_specs=[_ANY] * n,
        out_shape=[jax.ShapeDtypeStruct((N_CHIPS - 1,) + p.shape[1:], p.dtype) for p in pair_sums],
        scratch_shapes=[pltpu.SemaphoreType.DMA((n, 3)), pltpu.SemaphoreType.DMA((n, 3))],
        compiler_params=pltpu.CompilerParams(has_side_effects=True),
    )(*pair_sums)


def _scatter_start(name, pair_sums):
    n = len(pair_sums)
    lands = [lax.empty((N_CHIPS - 1,) + p.shape[1:], p.dtype) for p in pair_sums]

    def body(*refs):
        p_ref, land = refs[2 * n:3 * n], refs[3 * n:4 * n]
        send_sem, recv_sem, token = refs[4 * n:]
        x, y, c = _mesh_pos()
        for a in range(n):
            for p, (px, py) in enumerate(_other_chips(x, y)):
                pltpu.make_async_remote_copy(
                    src_ref=p_ref[a].at[2 * px + py], dst_ref=land[a].at[p], send_sem=send_sem, recv_sem=recv_sem,
                    device_id=(px, py, c), device_id_type=MESH).start()
        token[...] = jnp.zeros_like(token)

    outs = pl.pallas_call(
        body, name=name, in_specs=[_HBM] * (2 * n),
        out_specs=[_HBM] * (2 * n) + [_SEM, _SEM, pl.BlockSpec(memory_space=pltpu.VMEM)],
        out_shape=[pltpu.HBM(t.shape, t.dtype) for t in list(pair_sums) + lands]
        + [pltpu.SemaphoreType.DMA(()), pltpu.SemaphoreType.DMA(()), jax.ShapeDtypeStruct((8, LANES), F32)],
        input_output_aliases={i: i for i in range(2 * n)},
        compiler_params=pltpu.CompilerParams(has_side_effects=pltpu.SideEffectType.DATAFLOW_SIDE_EFFECTING),
    )(*_in_hbm(list(pair_sums) + lands))
    return list(outs[:n]), list(outs[n:2 * n]), outs[2 * n], outs[2 * n + 1], outs[2 * n + 2]


def _scatter_wait(name, pair_sums, lands, send_sem, recv_sem, after):
    n = len(pair_sums)

    def body(*refs):
        send_ref, recv_ref = refs[2 * n], refs[2 * n + 1]
        p_ref, land = refs[2 * n + 3:3 * n + 3], refs[3 * n + 3:]
        x, y, c = _mesh_pos()
        copies = [pltpu.make_async_remote_copy(
            src_ref=p_ref[a].at[2 * px + py], dst_ref=land[a].at[p], send_sem=send_ref, recv_sem=recv_ref,
            device_id=(px, py, c), device_id_type=MESH)
            for a in range(n) for p, (px, py) in enumerate(_other_chips(x, y))]
        for cp in copies:
            cp.wait_send()
        for cp in copies:
            cp.wait_recv()

    outs = pl.pallas_call(
        body, name=name, in_specs=[_HBM] * (2 * n) + [_SEM, _SEM, _ANY], out_specs=[_HBM] * (2 * n),
        out_shape=[pltpu.HBM(t.shape, t.dtype) for t in list(pair_sums) + list(lands)],
        input_output_aliases={i: i for i in range(2 * n)},
        compiler_params=pltpu.CompilerParams(has_side_effects=pltpu.SideEffectType.DATAFLOW_SIDE_EFFECTING),
    )(*pair_sums, *lands, send_sem, recv_sem, after)
    return list(outs[:n]), list(outs[n:])


def _swap_halves(name, shards):
    n = len(shards)

    def body(*refs):
        out = refs[n:2 * n]
        send_sems, recv_sems = refs[2 * n:]
        x, y, c = _mesh_pos()
        sends = []
        for a in range(n):
            rh = out[a].shape[0] // N_CORES
            mine = out[a].at[pl.ds(pl.multiple_of(c * rh, 8), rh), :]
            cp = pltpu.make_async_remote_copy(
                src_ref=mine, dst_ref=mine, send_sem=send_sems.at[a], recv_sem=recv_sems.at[a],
                device_id=(x, y, 1 - c), device_id_type=MESH)
            cp.start()
            sends.append(cp)
        for a in range(n):
            rh = out[a].shape[0] // N_CORES
            theirs = out[a].at[pl.ds(pl.multiple_of((1 - c) * rh, 8), rh), :]
            pltpu.make_async_remote_copy(
                src_ref=theirs, dst_ref=theirs, send_sem=send_sems.at[a], recv_sem=recv_sems.at[a],
                device_id=(x, y, 1 - c), device_id_type=MESH).wait_recv()
        for cp in sends:
            cp.wait_send()

    return pl.pallas_call(
        body, name=name, in_specs=[_ANY] * n, out_specs=[_ANY] * n,
        out_shape=[jax.ShapeDtypeStruct(s.shape, s.dtype) for s in shards],
        input_output_aliases={i: i for i in range(n)},
        scratch_shapes=[pltpu.SemaphoreType.DMA((n,)), pltpu.SemaphoreType.DMA((n,))],
        compiler_params=pltpu.CompilerParams(has_side_effects=True),
    )(*shards)


def _gather_all_devices(name, block):
    R, C = block.shape
    ndev = N_CHIPS * N_CORES

    def body(b_ref, out_ref, send_sems, recv_sems, local_sem):
        x, y, c = _mesh_pos()
        mine = 4 * x + 2 * y + c
        own = pltpu.make_async_copy(b_ref, out_ref.at[mine], local_sem)
        own.start()
        sends = []
        for mask in range(1, ndev):
            fx, fy, fc = (mask >> 2) & 1, (mask >> 1) & 1, mask & 1
            px, py, pc = x ^ fx, y ^ fy, c ^ fc
            cp = pltpu.make_async_remote_copy(
                src_ref=b_ref, dst_ref=out_ref.at[mine], send_sem=send_sems.at[mask - 1],
                recv_sem=recv_sems.at[mask - 1], device_id=(px, py, pc), device_id_type=MESH)
            cp.start()
            sends.append(cp)
        for mask in range(1, ndev):
            fx, fy, fc = (mask >> 2) & 1, (mask >> 1) & 1, mask & 1
            px, py, pc = x ^ fx, y ^ fy, c ^ fc
            pltpu.make_async_remote_copy(
                src_ref=b_ref, dst_ref=out_ref.at[4 * px + 2 * py + pc], send_sem=send_sems.at[mask - 1],
                recv_sem=recv_sems.at[mask - 1], device_id=(px, py, pc), device_id_type=MESH).wait_recv()
        for cp in sends:
            cp.wait_send()
        own.wait()

    return pl.pallas_call(
        body, name=name, in_specs=[_ANY], out_specs=_ANY,
        out_shape=jax.ShapeDtypeStruct((ndev, R, C), F32),
        scratch_shapes=[pltpu.SemaphoreType.DMA((ndev - 1,)), pltpu.SemaphoreType.DMA((ndev - 1,)),
                        pltpu.SemaphoreType.DMA(())],
        compiler_params=pltpu.CompilerParams(has_side_effects=True),
    )(block)


def _reduce_scatter(grads, kinds, pos):
    landed = _send_to_sibling("rs_pair_send", grads, kinds)
    pair = [_region_add(f"rs_pair_add_{a}", g, kd, ld, pos[1:]) for a, (g, kd, ld) in enumerate(zip(grads, kinds, landed))]
    parts = _scatter_over_chips("rs_chip_send", pair)
    shards = [_chip_sum(f"rs_chip_add_{a}", p, ld, pos) for a, (p, ld) in enumerate(zip(pair, parts))]
    return _swap_halves("rs_swap_halves", shards)


def kernel(x, norm_g, ffn1_w_gate, ffn1_w_up, ffn1_w_down, ffn2_w_gate, ffn2_w_up, ffn2_w_down, even_w_in, even_b_forget, even_w_out, odd_w_qkv, odd_w_out, final_norm_g, loss_target, m_norm_g, m_ffn1_w_gate, m_ffn1_w_up, m_ffn1_w_down, m_ffn2_w_gate, m_ffn2_w_up, m_ffn2_w_down, m_even_w_in, m_even_b_forget, m_even_w_out, m_odd_w_qkv, m_odd_w_out, m_final_norm_g, v_norm_g, v_ffn1_w_gate, v_ffn1_w_up, v_ffn1_w_down, v_ffn2_w_gate, v_ffn2_w_up, v_ffn2_w_down, v_even_w_in, v_even_b_forget, v_even_w_out, v_odd_w_qkv, v_odd_w_out, v_final_norm_g):
    _, S, D = x.shape
    L = norm_g.shape[0]
    assert L == 2 and even_w_in.shape[0] == 1 and odd_w_qkv.shape[0] == 1
    fs = ffn1_w_gate.shape[2]
    F = N_CHIPS * fs
    wc = even_w_in.shape[2]
    n_heads = D // HEAD_DIM
    n_fox = N_CHIPS * wc - 3 * D
    n_sb = n_heads - n_fox
    qs = odd_w_qkv.shape[2]
    os_ = even_w_out.shape[1]
    ns = norm_g.shape[2]
    xi, yi, ci = _mesh_pos()
    chip = 2 * xi + yi

    pos = jnp.stack([chip, ci]).astype(jnp.int32)
    kchip = pos[:1]
    lane = lambda start, size: pl.ds(pl.multiple_of(start, LANES), size)
    sub = lambda start, size: pl.ds(pl.multiple_of(start, 16), size)
    gate_view = lambda r, k, h: r.at[sub(h * (D // 2), D // 2), lane(k * 2 * fs, fs)]
    up_view = lambda r, k, h: r.at[sub(h * (D // 2), D // 2), lane(k * 2 * fs + fs, fs)]
    down_view = lambda r, k, h: r.at[sub(k * fs + h * (fs // 2), fs // 2), :]
    out_view = lambda r, k, h: r.at[sub(k * os_ + h * (os_ // 2), os_ // 2), :]
    tr_d = _row_tile(fs, 512, step=16)
    tr_o = _row_tile(os_, 512, step=16)
    ffn_w = {"ffn1": (ffn1_w_gate, ffn1_w_up, ffn1_w_down), "ffn2": (ffn2_w_gate, ffn2_w_up, ffn2_w_down)}
    win_view = lambda r, k, h: r.at[k, sub(h * (D // 2), D // 2), :]
    qkv_view = lambda r, k, h: r.at[sub(h * (D // 2), D // 2), lane(k * qs, qs)]
    fulls, views, groups = [], [], {}
    for l in range(L):
        for blk in ("ffn1", "mix", "ffn2"):
            o, v0 = len(fulls), len(views)
            if blk == "mix" and l == 0:
                fulls += [_cast_into("cast_win", even_w_in, 0, kchip, (N_CHIPS, D, wc), lambda i, k: (k, i, 0)),
                          _cast_into("cast_wout_e", even_w_out, 0, kchip, (D, D), lambda i, k: (k * (os_ // tr_o) + i, 0))]
                views += [(o, win_view), (o + 1, out_view)]
            elif blk == "mix":
                fulls += [_cast_into("cast_wqkv_o", odd_w_qkv, 0, kchip, (D, N_CHIPS * qs), lambda i, k: (i, k)),
                          _cast_into("cast_wout_o", odd_w_out, 0, kchip, (D, D), lambda i, k: (k * (os_ // tr_o) + i, 0))]
                views += [(o, qkv_view), (o + 1, out_view)]
            else:
                wg, wu, wd = ffn_w[blk]
                t = f"cast_{blk}_l{l}"
                gu = _cast_into(t + "_gate", wg, l, kchip, (D, 2 * F), lambda i, k: (i, 2 * k))
                gu = _cast_into(t + "_up", wu, l, kchip, (D, 2 * F), lambda i, k: (i, 2 * k + 1), full=gu)
                dn = _cast_into(t + "_down", wd, l, kchip, (F, D), lambda i, k: (k * (fs // tr_d) + i, 0))
                fulls += [gu, dn]
                views += [(o, gate_view), (o, up_view), (o + 1, down_view)]
            gid = len(groups)
            views[v0:] = [(i, view, gid) for i, view in views[v0:]]
            groups[(blk, l)] = (gid, list(range(o, len(fulls))), list(range(v0, len(views))))
    norm_own = lax.dynamic_update_slice(jnp.zeros((L, 3, N_CHIPS * ns), F32), norm_g, (0, 0, chip * ns))
    (norm_full,) = _gather_over_chips("gather_norm", [norm_own], [(0, lambda r, k, h: r.at[:, :, lane(k * ns, ns)], False)])
    started, send_sems, recv_sems = _gather_start("gather_start", fulls, views)

    def fetch(block, after):
        if block == "norm_g":
            return norm_full
        if block == "final_g":
            return final_norm_g[None, :]
        gid, arrays, rows = groups[block]
        tag = f"{block[0]}_l{block[1]}"
        local = [(views[a][0] - arrays[0], views[a][1]) for a in rows]
        got = _gather_wait("gather_wait_" + tag, [started[i] for i in arrays], local, send_sems[gid], recv_sems[gid],
                           x if after is None else after)
        got = _forward_to_sibling("gather_pass_" + tag, got, local)
        if block[0] != "mix":
            return got
        if block[1] == 1:
            return {"wqkv_o": got[0], "wout_o": got[1]}
        win = jnp.concatenate([got[0][k] for k in range(N_CHIPS)], axis=1)
        return {"wqkv_e": win[:, :3 * D], "wf": jnp.pad(win[:, 3 * D:], ((0, 0), (0, LANES - n_fox))),
                "bf": jnp.pad(even_b_forget, ((0, 0), (0, LANES - n_fox))), "wout_e": got[1]}

    pending, shards = [], {}

    def finish(after):
        block, pair, lands, ssem, rsem = pending.pop(0)
        tag = f"{block[0]}_l{block[1]}"
        pair, lands = _scatter_wait("rs_chip_wait_" + tag, pair, lands, ssem, rsem, after)
        shards[block] = [_chip_sum(f"rs_chip_add_{tag}_{a}", p, ld, pos) for a, (p, ld) in enumerate(zip(pair, lands))]

    def emit(block, g, dx):
        blk, l = block
        tag = f"{blk}_l{l}"
        if blk == "mix" and l == 0:
            dwin = jnp.concatenate([g["dwqkv_e"], g["dwf"][:, :n_fox]], axis=1)
            mats = [jnp.stack([dwin[:, k * wc:(k + 1) * wc] for k in range(N_CHIPS)]), g["dwout_e"]]
            kinds = ["lead", "rows"]
        elif blk == "mix":
            mats, kinds = [g["dwqkv_o"], g["dwout_o"]], ["cols", "rows"]
        else:
            f = blk[-1]
            mats, kinds = [g["dwgu" + f][l], g["dwd" + f][l]], ["cols", "rows"]
        landed = _send_to_sibling("rs_pair_send_" + tag, mats, kinds)
        pair = [_region_add(f"rs_pair_add_{tag}_{a}", m, kd, ld, pos[1:])
                for a, (m, kd, ld) in enumerate(zip(mats, kinds, landed))]
        pair, lands, ssem, rsem, token = _scatter_start("rs_chip_start_" + tag, pair)
        if pending:
            finish(token)
        pending.append((block, pair, lands, ssem, rsem))
        return token

    loss_vec, grad_x, g = _local_step(x[0], loss_target[0], fetch, fs, n_heads, n_sb, emit=emit)
    finish(grad_x)
    order = [(b, l) for b in ("ffn1", "ffn2", "mix") for l in range(L)]
    red = _swap_halves("rs_swap_halves", [s for b in order for s in shards[b]])
    red = {b: red[2 * i:2 * i + 2] for i, b in enumerate(order)}
    gu1, gd1 = [red[("ffn1", l)][0] for l in range(L)], [red[("ffn1", l)][1] for l in range(L)]
    gu2, gd2 = [red[("ffn2", l)][0] for l in range(L)], [red[("ffn2", l)][1] for l in range(L)]
    (g_win, g_wout_e), (g_qkv_o, g_wout_o) = red[("mix", 0)], red[("mix", 1)]

    small_rows = [g["dnorm"][l][i] for l in range(L) for i in range(3)] + [
        g["dfinal"], jnp.pad(g["db"], ((0, 0), (0, D - LANES))), jnp.pad(loss_vec, ((0, 0), (0, D - LANES)))]
    small = jnp.concatenate(small_rows + [jnp.zeros((16 - len(small_rows), D), F32)], axis=0)
    small_sum = _sum_leading("small_sum", _gather_all_devices("small_gather", small))
    loss = small_sum[3 * L + 2, 0]
    g_norm = lax.dynamic_slice_in_dim(small_sum[:3 * L].reshape(L, 3, D), chip * ns, ns, axis=2)
    g_final = small_sum[3 * L]
    g_bf = small_sum[3 * L + 1, :n_fox][None, :]

    grads = [
        g_norm,
        jnp.stack([t[:, :fs] for t in gu1]), jnp.stack([t[:, fs:] for t in gu1]), jnp.stack(gd1),
        jnp.stack([t[:, :fs] for t in gu2]), jnp.stack([t[:, fs:] for t in gu2]), jnp.stack(gd2),
        g_win[None], g_bf, g_wout_e[None], g_qkv_o[None], g_wout_o[None], g_final]
    weights = [norm_g, ffn1_w_gate, ffn1_w_up, ffn1_w_down, ffn2_w_gate, ffn2_w_up, ffn2_w_down,
               even_w_in, even_b_forget, even_w_out, odd_w_qkv, odd_w_out, final_norm_g]
    ms = [m_norm_g, m_ffn1_w_gate, m_ffn1_w_up, m_ffn1_w_down, m_ffn2_w_gate, m_ffn2_w_up, m_ffn2_w_down,
          m_even_w_in, m_even_b_forget, m_even_w_out, m_odd_w_qkv, m_odd_w_out, m_final_norm_g]
    vs = [v_norm_g, v_ffn1_w_gate, v_ffn1_w_up, v_ffn1_w_down, v_ffn2_w_gate, v_ffn2_w_up, v_ffn2_w_down,
          v_even_w_in, v_even_b_forget, v_even_w_out, v_odd_w_qkv, v_odd_w_out, v_final_norm_g]
    deltas, new_ms, new_vs = [], [], []
    for i, (wt, gt, mt, vt) in enumerate(zip(weights, grads, ms, vs)):
        d, nm, nv = _adamw(f"adamw_{i}", wt, gt, mt, vt)
        deltas.append(d)
        new_ms.append(nm)
        new_vs.append(nv)
    return (loss, grad_x[None], *grads, *deltas, *new_ms, *new_vs)
```

```python
import functools
import math

import jax
import jax.numpy as jnp
from jax import lax
from jax.experimental import pallas as pl
from jax.experimental.pallas import tpu as pltpu

F32 = jnp.float32
BF16 = jnp.bfloat16

HEAD_DIM = 128
ROPE_DIMS = 32
ROPE_THETA = 500000.0
DILATED_PATTERNS = ((128, 1), (512, 4), (2048, 16))
RMS_EPS = 1e-6
NEG_INF = -1e30
ADAM_LR = 0.001
ADAM_B1 = 0.9
ADAM_B2 = 0.999
ADAM_EPS = 1e-08
ADAM_WD = 0.01
ADAM_STEP = 10

N_CHIPS = 4
N_CORES = 2
LANES = 128
BLK = 128
VMEM_BYTES_V7X = 64 * 2**20
MESH = pl.DeviceIdType.MESH


def _vmem_limit(block_bytes, scratch_bytes=0):
    need = 2 * block_bytes + scratch_bytes + 12 * 2**20
    return int(min(need, VMEM_BYTES_V7X - 6 * 2**20))


def _nbytes(shape, dtype):
    return math.prod(shape) * jnp.dtype(dtype).itemsize


def _tile(dim, target):
    best = None
    for t in range(LANES, min(dim, target) + 1, LANES):
        if dim % t == 0:
            best = t
    assert best is not None, (dim, target)
    return best


def _row_tile(rows, target, step=8):
    if rows <= target:
        return rows
    best = None
    for t in range(step, target + 1, step):
        if rows % t == 0:
            best = t
    assert best is not None, (rows, target)
    return best


def _mm(name, a, b, mode, out_dtype, res=None, alpha=1.0, after=None, tm_target=1024, tn_target=1536, tk_target=2048):
    a3 = a.ndim == 3
    b3 = b.ndim == 3
    if mode == "nn":
        assert not a3 and not b3
        (M, K), (K2, N) = a.shape, b.shape
    elif mode == "nt":
        assert not b3
        if a3:
            P, M, Kp = a.shape
            K = P * Kp
        else:
            M, K = a.shape
        N, K2 = b.shape
    else:
        assert mode == "tn" and not a3
        K, M = a.shape
        if b3:
            P, K2, Np = b.shape
            N = P * Np
        else:
            K2, N = b.shape
    assert K == K2, (name, a.shape, b.shape)
    tm = _tile(M, tm_target)
    tn = _tile(Np if b3 else N, tn_target)
    tk = _tile(Kp if a3 else K, tk_target)
    nk = K // tk
    grid = (M // tm, N // tn, nk)

    if mode == "nn":
        a_spec = pl.BlockSpec((tm, tk), lambda i, j, k: (i, k))
        b_spec = pl.BlockSpec((tk, tn), lambda i, j, k: (k, j))
        dims = (((1,), (0,)), ((), ()))
    elif mode == "nt":
        if a3:
            nkp = Kp // tk
            a_spec = pl.BlockSpec((None, tm, tk), lambda i, j, k: (k // nkp, i, k % nkp))
        else:
            a_spec = pl.BlockSpec((tm, tk), lambda i, j, k: (i, k))
        b_spec = pl.BlockSpec((tn, tk), lambda i, j, k: (j, k))
        dims = (((1,), (1,)), ((), ()))
    else:
        a_spec = pl.BlockSpec((tk, tm), lambda i, j, k: (k, i))
        if b3:
            njp = Np // tn
            b_spec = pl.BlockSpec((None, tk, tn), lambda i, j, k: (j // njp, k, j % njp))
        else:
            b_spec = pl.BlockSpec((tk, tn), lambda i, j, k: (k, j))
        dims = (((0,), (0,)), ((), ()))
    o_spec = pl.BlockSpec((tm, tn), lambda i, j, k: (i, j))
    has_res = res is not None

    def finish(y, r_ref, o_ref):
        if alpha != 1.0:
            y = y * alpha
        if has_res:
            y = r_ref[...] + y
        o_ref[...] = y.astype(o_ref.dtype)

    n_in = 2 + has_res + (after is not None)

    def body(*refs):
        a_ref, b_ref = refs[:2]
        r_ref = refs[2] if has_res else None
        o_ref = refs[n_in]
        part = lax.dot_general(a_ref[...], b_ref[...], dims, preferred_element_type=F32)
        if nk == 1:
            finish(part, r_ref, o_ref)
            return
        acc_ref = refs[-1]
        k = pl.program_id(2)

        @pl.when(k == 0)
        def _():
            acc_ref[...] = part

        @pl.when(k > 0)
        def _():
            acc_ref[...] += part

        @pl.when(k == nk - 1)
        def _():
            finish(acc_ref[...], r_ref, o_ref)

    in_specs = [a_spec, b_spec] + ([o_spec] if has_res else []) + ([_ANY] if after is not None else [])
    args = (a, b) + ((res,) if has_res else ()) + ((after,) if after is not None else ())
    blk = (_nbytes((tm, tk), a.dtype) + _nbytes((tk, tn), b.dtype) + _nbytes((tm, tn), out_dtype)
           + (_nbytes((tm, tn), F32) if has_res else 0))
    return pl.pallas_call(
        body, name=name, grid=grid, in_specs=in_specs, out_specs=o_spec,
        out_shape=jax.ShapeDtypeStruct((M, N), out_dtype),
        scratch_shapes=[pltpu.VMEM((tm, tn), F32)] if nk > 1 else [],
        compiler_params=pltpu.CompilerParams(
            dimension_semantics=("parallel", "parallel", "arbitrary"),
            vmem_limit_bytes=_vmem_limit(blk, 2 * _nbytes((tm, tn), F32))),
    )(*args)


def _rms_fwd(name, x, g):
    S, D = x.shape
    tr = _row_tile(S, 256)

    def body(x_ref, g_ref, n_ref):
        xv = x_ref[...]
        r = lax.rsqrt(jnp.mean(xv * xv, axis=-1, keepdims=True) + RMS_EPS)
        n_ref[...] = (xv * r * g_ref[...]).astype(BF16)

    return pl.pallas_call(
        body, name=name, grid=(S // tr,),
        in_specs=[pl.BlockSpec((tr, D), lambda i: (i, 0)), pl.BlockSpec((1, D), lambda i: (0, 0))],
        out_specs=pl.BlockSpec((tr, D), lambda i: (i, 0)),
        out_shape=jax.ShapeDtypeStruct((S, D), BF16),
        compiler_params=pltpu.CompilerParams(dimension_semantics=("parallel",)),
    )(x, g)


def _rms_bwd(name, dn, x, g, dres):
    S, D = x.shape
    tr = _row_tile(S, 256)

    def body(dn_ref, x_ref, g_ref, dres_ref, dx_ref, dxb_ref, dg_ref):
        i = pl.program_id(0)
        xv = x_ref[...]
        dnv = dn_ref[...]
        r = lax.rsqrt(jnp.mean(xv * xv, axis=-1, keepdims=True) + RMS_EPS)
        u = dnv * g_ref[...]
        dot = jnp.mean(u * xv, axis=-1, keepdims=True)
        dx = dres_ref[...] + r * u - xv * (r * r * r * dot)
        dx_ref[...] = dx
        dxb_ref[...] = dx.astype(BF16)

        @pl.when(i == 0)
        def _():
            dg_ref[...] = jnp.zeros_like(dg_ref)

        dg_ref[...] += jnp.sum(dnv * xv * r, axis=0, keepdims=True)

    row = pl.BlockSpec((tr, D), lambda i: (i, 0))
    vec = pl.BlockSpec((1, D), lambda i: (0, 0))
    return pl.pallas_call(
        body, name=name, grid=(S // tr,),
        in_specs=[row, row, vec, row], out_specs=[row, row, vec],
        out_shape=[jax.ShapeDtypeStruct((S, D), F32), jax.ShapeDtypeStruct((S, D), BF16),
                   jax.ShapeDtypeStruct((1, D), F32)],
        compiler_params=pltpu.CompilerParams(dimension_semantics=("arbitrary",)),
    )(dn, x, g, dres)


def _loss_head(name, x, g, target):
    S, D = x.shape
    tr = _row_tile(S, 256)

    def body(x_ref, g_ref, t_ref, dx_ref, dxb_ref, dg_ref, loss_ref):
        i = pl.program_id(0)
        xv = x_ref[...]
        gv = g_ref[...]
        r = lax.rsqrt(jnp.mean(xv * xv, axis=-1, keepdims=True) + RMS_EPS)
        diff = xv * r * gv - t_ref[...]
        part = 0.5 * jnp.sum(jnp.mean(diff * diff, axis=-1, keepdims=True), axis=0, keepdims=True)
        dy = diff * (1.0 / D)
        u = dy * gv
        dot = jnp.mean(u * xv, axis=-1, keepdims=True)
        dx = r * u - xv * (r * r * r * dot)
        dx_ref[...] = dx
        dxb_ref[...] = dx.astype(BF16)

        @pl.when(i == 0)
        def _():
            dg_ref[...] = jnp.zeros_like(dg_ref)
            loss_ref[...] = jnp.zeros_like(loss_ref)

        dg_ref[...] += jnp.sum(dy * xv * r, axis=0, keepdims=True)
        loss_ref[...] += jnp.broadcast_to(part, loss_ref.shape)

    row = pl.BlockSpec((tr, D), lambda i: (i, 0))
    vec = pl.BlockSpec((1, D), lambda i: (0, 0))
    lvec = pl.BlockSpec((1, LANES), lambda i: (0, 0))
    return pl.pallas_call(
        body, name=name, grid=(S // tr,),
        in_specs=[row, vec, row], out_specs=[row, row, vec, lvec],
        out_shape=[jax.ShapeDtypeStruct((S, D), F32), jax.ShapeDtypeStruct((S, D), BF16),
                   jax.ShapeDtypeStruct((1, D), F32), jax.ShapeDtypeStruct((1, LANES), F32)],
        compiler_params=pltpu.CompilerParams(dimension_semantics=("arbitrary",)),
    )(x, g, target)


def _swiglu_fwd(name, gu, fs):
    S, two_f = gu.shape
    nslab = two_f // (2 * fs)
    tr = _row_tile(S, 256)

    def body(gu_ref, h_ref):
        gv = gu_ref[:, :fs]
        uv = gu_ref[:, fs:]
        h_ref[...] = (gv * jax.nn.sigmoid(gv) * uv).astype(BF16)

    return pl.pallas_call(
        body, name=name, grid=(S // tr, nslab),
        in_specs=[pl.BlockSpec((tr, 2 * fs), lambda i, k: (i, k))],
        out_specs=pl.BlockSpec((tr, fs), lambda i, k: (i, k)),
        out_shape=jax.ShapeDtypeStruct((S, nslab * fs), BF16),
        compiler_params=pltpu.CompilerParams(dimension_semantics=("parallel", "parallel")),
    )(gu)


def _swiglu_bwd(name, dh, gu, fs):
    S, two_f = gu.shape
    nslab = two_f // (2 * fs)
    tr = _row_tile(S, 256)

    def body(dh_ref, gu_ref, o_ref):
        gv = gu_ref[:, :fs]
        uv = gu_ref[:, fs:]
        dhv = dh_ref[...]
        sg = jax.nn.sigmoid(gv)
        silu = gv * sg
        o_ref[:, :fs] = (dhv * uv * (sg + silu * (1.0 - sg))).astype(BF16)
        o_ref[:, fs:] = (dhv * silu).astype(BF16)

    return pl.pallas_call(
        body, name=name, grid=(S // tr, nslab),
        in_specs=[pl.BlockSpec((tr, fs), lambda i, k: (i, k)), pl.BlockSpec((tr, 2 * fs), lambda i, k: (i, k))],
        out_specs=pl.BlockSpec((tr, 2 * fs), lambda i, k: (i, k)),
        out_shape=jax.ShapeDtypeStruct((S, two_f), BF16),
        compiler_params=pltpu.CompilerParams(dimension_semantics=("parallel", "parallel")),
    )(dh, gu)


def _tri_rows(r0, nrows, ncols, lower):
    row = lax.broadcasted_iota(jnp.int32, (nrows, ncols), 0) + r0
    col = lax.broadcasted_iota(jnp.int32, (nrows, ncols), 1)
    return jnp.where((col <= row) if lower else (col >= row), 1.0, 0.0).astype(F32)


def _gate_fwd(name, hf, b):
    S = hf.shape[0]
    tb = _row_tile(S, 256)

    def body(hf_ref, b_ref, cf_ref, cft_ref, lf_ref):
        zz = hf_ref[...] + b_ref[...]
        lf_ref[...] = jnp.minimum(zz, 0.0) - jnp.log1p(jnp.exp(-jnp.abs(zz)))

        def blk(i, c):
            r0 = pl.multiple_of(i * tb, tb)
            tri = _tri_rows(r0, tb, S, True)
            cf_ref[pl.ds(r0, tb), :] = jnp.dot(tri, lf_ref[...], precision=lax.Precision.HIGHEST,
                                               preferred_element_type=F32)
            return c

        lax.fori_loop(0, S // tb, blk, 0)
        cft_ref[...] = cf_ref[...].T

    full = pl.BlockSpec((S, LANES), lambda: (0, 0))
    return pl.pallas_call(
        body, name=name, in_specs=[full, pl.BlockSpec((1, LANES), lambda: (0, 0))],
        out_specs=[full, pl.BlockSpec((LANES, S), lambda: (0, 0))],
        out_shape=[jax.ShapeDtypeStruct((S, LANES), F32), jax.ShapeDtypeStruct((LANES, S), F32)],
        scratch_shapes=[pltpu.VMEM((S, LANES), F32)],
    )(hf, b)


def _gate_bwd(name, dcft, drow, hf, b):
    S = hf.shape[0]
    tb = _row_tile(S, 256)

    def body(dcft_ref, drow_ref, hf_ref, b_ref, dhf_ref, db_ref, dcf_ref, dlf_ref):
        dcf_ref[...] = dcft_ref[...].T + drow_ref[...]

        def blk(i, c):
            r0 = pl.multiple_of(i * tb, tb)
            tri = _tri_rows(r0, tb, S, False)
            dlf_ref[pl.ds(r0, tb), :] = jnp.dot(tri, dcf_ref[...], precision=lax.Precision.HIGHEST,
                                                preferred_element_type=F32)
            return c

        lax.fori_loop(0, S // tb, blk, 0)
        zz = hf_ref[...] + b_ref[...]
        dhf = dlf_ref[...] * jax.nn.sigmoid(-zz)
        dhf_ref[...] = dhf.astype(BF16)
        db_ref[...] = jnp.sum(dhf, axis=0, keepdims=True)

    full = pl.BlockSpec((S, LANES), lambda: (0, 0))
    vec = pl.BlockSpec((1, LANES), lambda: (0, 0))
    return pl.pallas_call(
        body, name=name, in_specs=[pl.BlockSpec((LANES, S), lambda: (0, 0)), full, full, vec],
        out_specs=[full, vec],
        out_shape=[jax.ShapeDtypeStruct((S, LANES), BF16), jax.ShapeDtypeStruct((1, LANES), F32)],
        scratch_shapes=[pltpu.VMEM((S, LANES), F32), pltpu.VMEM((S, LANES), F32)],
    )(dcft, drow, hf, b)


def _rope_tables(S):
    half = ROPE_DIMS // 2
    freqs = ROPE_THETA ** (-jnp.arange(half, dtype=F32) / half)
    ang = jnp.arange(S, dtype=F32)[:, None] * freqs[None, :]
    cos, sin = jnp.cos(ang), jnp.sin(ang)
    pad = HEAD_DIM - ROPE_DIMS
    c = jnp.concatenate([cos, cos, jnp.ones((S, pad), F32)], axis=1)
    s = jnp.concatenate([-sin, sin, jnp.zeros((S, pad), F32)], axis=1)
    return c, s


def _rope_swap(x):
    half = ROPE_DIMS // 2
    lane = lax.broadcasted_iota(jnp.int32, x.shape, 1)
    upper = jnp.where(lane < ROPE_DIMS, pltpu.roll(x, half, 1), 0.0)
    return jnp.where(lane < half, pltpu.roll(x, HEAD_DIM - half, 1), upper)


def _rope(x, c, s):
    return x * c + _rope_swap(x) * s


def _rope_t(dy, c, s):
    return dy * c + _rope_swap(dy * s)


def _split_dot(x, t):
    hi = x.astype(BF16)
    lo = (x - hi.astype(F32)).astype(BF16)
    return (jnp.dot(hi, t, preferred_element_type=F32) + jnp.dot(lo, t, preferred_element_type=F32))


_NT = (((1,), (1,)), ((), ()))
_TN = (((0,), (0,)), ((), ()))


def _dot_nt(a, b):
    return lax.dot_general(a, b, _NT, preferred_element_type=F32)


def _dot_tn(a, b):
    return lax.dot_general(a, b, _TN, preferred_element_type=F32)


def _blk(i):
    return pl.ds(pl.multiple_of(i * BLK, BLK), BLK)


def _delta(i, j):
    row = lax.broadcasted_iota(jnp.int32, (BLK, BLK), 0)
    col = lax.broadcasted_iota(jnp.int32, (BLK, BLK), 1)
    return (row - col) + (i - j) * BLK


def _dilated_mult(delta):
    c = jnp.zeros(delta.shape, F32)
    for window, dil in DILATED_PATTERNS:
        ok = (delta >= 0) & (delta <= window) & ((delta & (dil - 1)) == 0)
        c = c + jnp.where(ok, 1.0, 0.0)
    return c


def _sb_terms(z, mask, t_ex, run):
    t = jnp.log1p(jnp.exp(-jnp.abs(z)))
    lsig = jnp.minimum(z, 0.0) - t
    m = jnp.where(mask, -(jnp.maximum(z, 0.0) + t), 0.0)
    after = _split_dot(m, t_ex)
    a = jnp.where(mask, jnp.exp(lsig + after + run), 0.0)
    return a, m, lsig


def _attn_fwd(name, hq, layer_kind, n_heads, n_sb, cf=None, cft=None, rope_c=None, rope_s=None):
    S = hq.shape[0]
    D = n_heads * HEAD_DIM
    nq = S // BLK
    scale = HEAD_DIM ** -0.5
    even = layer_kind == "even"

    def body(*refs):
        if even:
            q_ref, k_ref, v_ref, cf_ref, cft_ref, o_ref, ob_ref, lse_ref, qs, ks, vs = refs
        else:
            q_ref, k_ref, v_ref, c_ref, s_ref, o_ref, ob_ref, lse_ref, qs, ks, vs = refs
        h = pl.program_id(0)
        if even:
            qs[...] = q_ref[...].astype(BF16)
            ks[...] = k_ref[...].astype(BF16)
        else:
            qs[...] = _rope(q_ref[...], c_ref[...], s_ref[...]).astype(BF16)
            ks[...] = _rope(k_ref[...], c_ref[...], s_ref[...]).astype(BF16)
        vs[...] = v_ref[...].astype(BF16)

        def softmax_head(hh):
            def qblock(i, carry):
                qi = qs[_blk(i), :]
                if even:
                    lane = lax.broadcasted_iota(jnp.int32, (BLK, LANES), 1)
                    cfq = jnp.sum(jnp.where(lane == hh, cf_ref[_blk(i), :], 0.0), axis=1, keepdims=True)

                def kblock(j, c):
                    m_run, l_run, acc = c
                    z = _dot_nt(qi, ks[_blk(j), :]) * scale
                    delta = _delta(i, j)
                    if even:
                        z = z + cfq - cft_ref[hh, :, _blk(j)]
                        ok = delta >= 0
                    else:
                        mult = _dilated_mult(delta)
                        ok = mult > 0.0
                    z = jnp.where(ok, z, NEG_INF)
                    m_new = jnp.maximum(m_run, jnp.max(z, axis=1, keepdims=True))
                    p = jnp.exp(z - m_new)
                    if not even:
                        p = p * mult
                    alpha = jnp.exp(m_run - m_new)
                    l_new = alpha * l_run + jnp.sum(p, axis=1, keepdims=True)
                    acc = alpha * acc + jnp.dot(p.astype(BF16), vs[_blk(j), :], preferred_element_type=F32)
                    return m_new, l_new, acc

                init = (jnp.full((BLK, 1), NEG_INF, F32), jnp.zeros((BLK, 1), F32), jnp.zeros((BLK, HEAD_DIM), F32))
                m_run, l_run, acc = lax.fori_loop(0, i + 1, kblock, init)
                o = acc / l_run
                o_ref[_blk(i), :] = o
                ob_ref[_blk(i), :] = o.astype(BF16)
                lse_ref[_blk(i), :] = jnp.broadcast_to(m_run + jnp.log(l_run), (BLK, HEAD_DIM))
                return carry

            lax.fori_loop(0, nq, qblock, 0)

        def sb_head():
            row = lax.broadcasted_iota(jnp.int32, (BLK, BLK), 0)
            col = lax.broadcasted_iota(jnp.int32, (BLK, BLK), 1)
            t_ex = jnp.where(row > col, 1.0, 0.0).astype(BF16)

            def qblock(i, carry):
                qi = qs[_blk(i), :]

                def kblock(jj, c):
                    run, acc = c
                    j = i - jj
                    z = _dot_nt(qi, ks[_blk(j), :]) * scale
                    a, m, _ = _sb_terms(z, _delta(i, j) > 0, t_ex, run)
                    acc = acc + jnp.dot(a.astype(BF16), vs[_blk(j), :], preferred_element_type=F32)
                    return run + jnp.sum(m, axis=1, keepdims=True), acc

                init = (jnp.zeros((BLK, 1), F32), jnp.zeros((BLK, HEAD_DIM), F32))
                _, acc = lax.fori_loop(0, i + 1, kblock, init)
                o_ref[_blk(i), :] = acc
                ob_ref[_blk(i), :] = acc.astype(BF16)
                lse_ref[_blk(i), :] = jnp.zeros((BLK, HEAD_DIM), F32)
                return carry

            lax.fori_loop(0, nq, qblock, 0)

        if even:
            @pl.when(h < n_sb)
            def _():
                sb_head()

            @pl.when(h >= n_sb)
            def _():
                softmax_head(h - n_sb)
        else:
            softmax_head(h)

    head = lambda off: pl.BlockSpec((S, HEAD_DIM), lambda h, off=off: (0, off + h))
    full = pl.BlockSpec((S, LANES), lambda h: (0, 0))
    if even:
        extra_specs = [full, pl.BlockSpec(cft.shape, lambda h: (0, 0, 0))]
        extra = (cf, cft)
    else:
        extra_specs = [full, full]
        extra = (rope_c, rope_s)
    blk_bytes = 8 * _nbytes((S, HEAD_DIM), F32)
    return pl.pallas_call(
        body, name=name, grid=(n_heads,),
        in_specs=[head(0), head(n_heads), head(2 * n_heads)] + extra_specs,
        out_specs=[head(0), head(0), head(0)],
        out_shape=[jax.ShapeDtypeStruct((S, D), F32), jax.ShapeDtypeStruct((S, D), BF16),
                   jax.ShapeDtypeStruct((S, D), F32)],
        scratch_shapes=[pltpu.VMEM((S, HEAD_DIM), BF16)] * 3,
        compiler_params=pltpu.CompilerParams(dimension_semantics=("arbitrary",),
                                             vmem_limit_bytes=_vmem_limit(blk_bytes, 3 * _nbytes((S, HEAD_DIM), BF16))),
    )(hq, hq, hq, *extra)


def _attn_bwd(name, hq, do, o, lse, layer_kind, n_heads, n_sb, cf=None, cft=None, rope_c=None, rope_s=None):
    S = hq.shape[0]
    D = n_heads * HEAD_DIM
    nq = S // BLK
    scale = HEAD_DIM ** -0.5
    even = layer_kind == "even"

    def body(*refs):
        if even:
            (q_ref, k_ref, v_ref, do_ref, o_ref, lse_ref, cf_ref, cft_ref,
             dh_ref, dcft_ref, drow_ref, qs, ks, vs, dos, dq_acc, dk_acc, dv_acc) = refs
        else:
            (q_ref, k_ref, v_ref, do_ref, o_ref, lse_ref, c_ref, s_ref,
             dh_ref, qs, ks, vs, dos, dq_acc, dk_acc, dv_acc) = refs
        h = pl.program_id(0)
        if even:
            qs[...] = q_ref[...].astype(BF16)
            ks[...] = k_ref[...].astype(BF16)

            @pl.when(h == 0)
            def _():
                dcft_ref[...] = jnp.zeros_like(dcft_ref)
                drow_ref[...] = jnp.zeros_like(drow_ref)
        else:
            qs[...] = _rope(q_ref[...], c_ref[...], s_ref[...]).astype(BF16)
            ks[...] = _rope(k_ref[...], c_ref[...], s_ref[...]).astype(BF16)
        vs[...] = v_ref[...].astype(BF16)
        dos[...] = do_ref[...].astype(BF16)
        dk_acc[...] = jnp.zeros_like(dk_acc)
        dv_acc[...] = jnp.zeros_like(dv_acc)

        def softmax_head(hh):
            def qblock(i, carry):
                qi = qs[_blk(i), :]
                doi = dos[_blk(i), :]
                dvec = jnp.sum(do_ref[_blk(i), :] * o_ref[_blk(i), :], axis=1, keepdims=True)
                lse_i = jnp.max(lse_ref[_blk(i), :], axis=1, keepdims=True)
                if even:
                    lane = lax.broadcasted_iota(jnp.int32, (BLK, LANES), 1)
                    cfq = jnp.sum(jnp.where(lane == hh, cf_ref[_blk(i), :], 0.0), axis=1, keepdims=True)

                def kblock(j, c):
                    dq, ds_rows = c
                    kj = ks[_blk(j), :]
                    z = _dot_nt(qi, kj) * scale
                    delta = _delta(i, j)
                    if even:
                        z = z + cfq - cft_ref[hh, :, _blk(j)]
                        ok = delta >= 0
                    else:
                        mult = _dilated_mult(delta)
                        ok = mult > 0.0
                    p = jnp.exp(jnp.where(ok, z, NEG_INF) - lse_i)
                    if not even:
                        p = p * mult
                    dp = _dot_nt(doi, vs[_blk(j), :])
                    ds = p * (dp - dvec)
                    dsb = (ds * scale).astype(BF16)
                    dk_acc[_blk(j), :] += _dot_tn(dsb, qi)
                    dv_acc[_blk(j), :] += _dot_tn(p.astype(BF16), doi)
                    if even:
                        dcft_ref[hh, :, _blk(j)] += -jnp.sum(ds, axis=0, keepdims=True)
                    return (dq + jnp.dot(dsb, kj, preferred_element_type=F32),
                            ds_rows + jnp.sum(ds, axis=1, keepdims=True))

                dq, ds_rows = lax.fori_loop(0, i + 1, kblock,
                                            (jnp.zeros((BLK, HEAD_DIM), F32), jnp.zeros((BLK, 1), F32)))
                dq_acc[_blk(i), :] = dq
                if even:
                    drow_ref[_blk(i), :] += jnp.where(lane == hh, ds_rows, 0.0)
                return carry

            lax.fori_loop(0, nq, qblock, 0)

        def sb_head():
            row = lax.broadcasted_iota(jnp.int32, (BLK, BLK), 0)
            col = lax.broadcasted_iota(jnp.int32, (BLK, BLK), 1)
            t_ex = jnp.where(row > col, 1.0, 0.0).astype(BF16)
            t_in = jnp.where(row >= col, 1.0, 0.0).astype(BF16)

            def qblock(i, carry):
                qi = qs[_blk(i), :]
                doi = dos[_blk(i), :]

                def e_total(jj, c):
                    run, tot = c
                    j = i - jj
                    z = _dot_nt(qi, ks[_blk(j), :]) * scale
                    a, m, _ = _sb_terms(z, _delta(i, j) > 0, t_ex, run)
                    e = _dot_nt(doi, vs[_blk(j), :]) * a
                    return run + jnp.sum(m, axis=1, keepdims=True), tot + jnp.sum(e, axis=1, keepdims=True)

                zero = jnp.zeros((BLK, 1), F32)
                _, e_tot = lax.fori_loop(0, i + 1, e_total, (zero, zero))

                def kblock(jj, c):
                    run, e_run, dq = c
                    j = i - jj
                    kj = ks[_blk(j), :]
                    z = _dot_nt(qi, kj) * scale
                    mask = _delta(i, j) > 0
                    a, m, lsig = _sb_terms(z, mask, t_ex, run)
                    sig = jnp.exp(lsig)
                    e = _dot_nt(doi, vs[_blk(j), :]) * a
                    e_before = e_tot - (_split_dot(e, t_in) + e_run)
                    dz = jnp.where(mask, e * (1.0 - sig) - sig * e_before, 0.0)
                    dzb = (dz * scale).astype(BF16)
                    dk_acc[_blk(j), :] += _dot_tn(dzb, qi)
                    dv_acc[_blk(j), :] += _dot_tn(a.astype(BF16), doi)
                    return (run + jnp.sum(m, axis=1, keepdims=True), e_run + jnp.sum(e, axis=1, keepdims=True),
                            dq + jnp.dot(dzb, kj, preferred_element_type=F32))

                _, _, dq = lax.fori_loop(0, i + 1, kblock, (zero, zero, jnp.zeros((BLK, HEAD_DIM), F32)))
                dq_acc[_blk(i), :] = dq
                return carry

            lax.fori_loop(0, nq, qblock, 0)

        if even:
            @pl.when(h < n_sb)
            def _():
                sb_head()

            @pl.when(h >= n_sb)
            def _():
                softmax_head(h - n_sb)

            dh_ref[0] = dq_acc[...].astype(BF16)
            dh_ref[1] = dk_acc[...].astype(BF16)
        else:
            softmax_head(h)
            dh_ref[0] = _rope_t(dq_acc[...], c_ref[...], s_ref[...]).astype(BF16)
            dh_ref[1] = _rope_t(dk_acc[...], c_ref[...], s_ref[...]).astype(BF16)
        dh_ref[2] = dv_acc[...].astype(BF16)

    head = lambda off: pl.BlockSpec((S, HEAD_DIM), lambda h, off=off: (0, off + h))
    full = pl.BlockSpec((S, LANES), lambda h: (0, 0))
    tfull = pl.BlockSpec((n_heads - n_sb, 1, S), lambda h: (0, 0, 0))
    dh_spec = pl.BlockSpec((3, S, HEAD_DIM), lambda h: (0, 0, h))
    dh_shape = jax.ShapeDtypeStruct((3, S, D), BF16)
    if even:
        extra_specs, extra = [full, tfull], (cf, cft)
        out_specs = [dh_spec, tfull, full]
        out_shape = [dh_shape, jax.ShapeDtypeStruct((n_heads - n_sb, 1, S), F32),
                     jax.ShapeDtypeStruct((S, LANES), F32)]
    else:
        extra_specs, extra = [full, full], (rope_c, rope_s)
        out_specs = [dh_spec]
        out_shape = [dh_shape]
    blk_bytes = 10 * _nbytes((S, HEAD_DIM), F32)
    scratch_bytes = 4 * _nbytes((S, HEAD_DIM), BF16) + 3 * _nbytes((S, HEAD_DIM), F32)
    return pl.pallas_call(
        body, name=name, grid=(n_heads,),
        in_specs=[head(0), head(n_heads), head(2 * n_heads), head(0), head(0), head(0)] + extra_specs,
        out_specs=out_specs, out_shape=out_shape,
        scratch_shapes=[pltpu.VMEM((S, HEAD_DIM), BF16)] * 4 + [pltpu.VMEM((S, HEAD_DIM), F32)] * 3,
        compiler_params=pltpu.CompilerParams(dimension_semantics=("arbitrary",),
                                             vmem_limit_bytes=_vmem_limit(blk_bytes, scratch_bytes)),
    )(hq, hq, hq, do, o, lse, *extra)


def _query_block(S):
    return min(512, S)


def _offsets(d, bq):
    row = jnp.arange(bq, dtype=jnp.int32)[:, None]
    col = jnp.arange(BLK, dtype=jnp.int32)[None, :]
    return d * BLK + row - col


def _causal_tables(bq, strict):
    r = bq // BLK
    tabs = []
    for d in range(-(r - 1), 1):
        delta = _offsets(d, bq)
        tabs.append(jnp.where((delta > 0) if strict else (delta >= 0), 1.0, 0.0))
    tabs.append(jnp.ones((bq, BLK), F32))
    return jnp.stack(tabs).astype(F32)


def _dilated_tables(bq):
    r = bq // BLK
    limit = sorted(w for w, _ in DILATED_PATTERNS)[-2]
    assert all(BLK % dil == 0 for _, dil in DILATED_PATTERNS)
    d_far = -(-(limit + BLK) // BLK)
    tabs = []
    for d in range(-(r - 1), d_far + 1):
        mult = _dilated_mult(_offsets(d, bq))
        tabs.append(jnp.where(mult > 0, jnp.log(jnp.maximum(mult, 1.0)), NEG_INF))
    return jnp.stack(tabs).astype(F32)


def _qblk(i, bq):
    return pl.ds(pl.multiple_of(i * bq, bq), bq)


def _sb_block(z, valid, t_ex, run):
    t = jnp.log1p(jnp.exp(-jnp.abs(z)))
    lsig = jnp.minimum(z, 0.0) - t
    m = -(jnp.maximum(z, 0.0) + t) * valid
    after = _split_dot(m, t_ex)
    a = jnp.exp(lsig + after + run) * valid
    return a, m, lsig


def _attn_fwd_wide(name, hq, layer_kind, n_heads, n_sb, cf=None, cft=None, rope_c=None, rope_s=None):
    S = hq.shape[0]
    D = n_heads * HEAD_DIM
    bq = _query_block(S)
    r = bq // BLK
    nq = S // bq
    scale = HEAD_DIM ** -0.5
    even = layer_kind == "even"
    if even:
        tabs = (jnp.where(_causal_tables(bq, False) > 0, 0.0, NEG_INF), _causal_tables(bq, True))
    else:
        tabs = (_dilated_tables(bq),)
    n_tab = tabs[0].shape[0]

    def body(*refs):
        if even:
            q_ref, k_ref, v_ref, cf_ref, cft_ref, bias_ref, valid_ref, o_ref, ob_ref, lse_ref, qs, ks, vs = refs
        else:
            q_ref, k_ref, v_ref, c_ref, s_ref, bias_ref, o_ref, ob_ref, lse_ref, qs, ks, vs = refs
        h = pl.program_id(0)
        if even:
            qs[...] = q_ref[...].astype(BF16)
            ks[...] = k_ref[...].astype(BF16)
        else:
            qs[...] = _rope(q_ref[...], c_ref[...], s_ref[...]).astype(BF16)
            ks[...] = _rope(k_ref[...], c_ref[...], s_ref[...]).astype(BF16)
        vs[...] = v_ref[...].astype(BF16)

        def softmax_head(hh):
            def qblock(i, carry):
                qi = qs[_qblk(i, bq), :]
                if even:
                    lane = lax.broadcasted_iota(jnp.int32, (bq, LANES), 1)
                    cfq = jnp.sum(jnp.where(lane == hh, cf_ref[_qblk(i, bq), :], 0.0), axis=1, keepdims=True)

                def kblock(j, c):
                    m_run, l_run, acc = c
                    z = _dot_nt(qi, ks[_blk(j), :]) * scale + bias_ref[jnp.minimum(r * i - j + (r - 1), n_tab - 1)]
                    if even:
                        z = z + (cfq - cft_ref[hh, :, _blk(j)])
                    m_new = jnp.maximum(m_run, jnp.max(z, axis=1, keepdims=True))
                    p = jnp.exp(z - m_new)
                    alpha = jnp.exp(m_run - m_new)
                    l_new = alpha * l_run + jnp.sum(p, axis=1, keepdims=True)
                    acc = alpha * acc + jnp.dot(p.astype(BF16), vs[_blk(j), :], preferred_element_type=F32)
                    return m_new, l_new, acc

                init = (jnp.full((bq, 1), NEG_INF, F32), jnp.zeros((bq, 1), F32), jnp.zeros((bq, HEAD_DIM), F32))
                m_run, l_run, acc = lax.fori_loop(0, r * (i + 1), kblock, init)
                o = acc / l_run
                o_ref[_qblk(i, bq), :] = o
                ob_ref[_qblk(i, bq), :] = o.astype(BF16)
                lse_ref[_qblk(i, bq), :] = jnp.broadcast_to(m_run + jnp.log(l_run), (bq, HEAD_DIM))
                return carry

            lax.fori_loop(0, nq, qblock, 0)

        def sb_head():
            row = lax.broadcasted_iota(jnp.int32, (BLK, BLK), 0)
            col = lax.broadcasted_iota(jnp.int32, (BLK, BLK), 1)
            t_ex = jnp.where(row > col, 1.0, 0.0).astype(BF16)

            def qblock(i, carry):
                qi = qs[_qblk(i, bq), :]

                def kblock(jj, c):
                    run, acc, rest = c
                    j = r * (i + 1) - 1 - jj
                    z = _dot_nt(qi, ks[_blk(j), :]) * scale
                    a, m, _ = _sb_block(z, valid_ref[jnp.minimum(r * i - j + (r - 1), r)], t_ex, run)
                    vj = vs[_blk(j), :]
                    hi = a.astype(BF16)
                    lo = (a - hi.astype(F32)).astype(BF16)
                    acc = acc + jnp.dot(hi, vj, preferred_element_type=F32)
                    rest = rest + jnp.dot(lo, vj, preferred_element_type=F32)
                    return run + jnp.sum(m, axis=1, keepdims=True), acc, rest

                zero = jnp.zeros((bq, HEAD_DIM), F32)
                _, acc, rest = lax.fori_loop(0, r * (i + 1), kblock, (jnp.zeros((bq, 1), F32), zero, zero))
                o_ref[_qblk(i, bq), :] = acc + rest
                ob_ref[_qblk(i, bq), :] = acc.astype(BF16)
                lse_ref[_qblk(i, bq), :] = jnp.zeros((bq, HEAD_DIM), F32)
                return carry

            lax.fori_loop(0, nq, qblock, 0)

        if even:
            @pl.when(h < n_sb)
            def _():
                sb_head()

            @pl.when(h >= n_sb)
            def _():
                softmax_head(h - n_sb)
        else:
            softmax_head(h)

    head = lambda off: pl.BlockSpec((S, HEAD_DIM), lambda h, off=off: (0, off + h))
    full = pl.BlockSpec((S, LANES), lambda h: (0, 0))
    tab_specs = [pl.BlockSpec(t.shape, lambda h: (0, 0, 0)) for t in tabs]
    if even:
        extra_specs = [full, pl.BlockSpec(cft.shape, lambda h: (0, 0, 0))] + tab_specs
        extra = (cf, cft) + tabs
    else:
        extra_specs = [full, full] + tab_specs
        extra = (rope_c, rope_s) + tabs
    blk_bytes = 8 * _nbytes((S, HEAD_DIM), F32) + sum(_nbytes(t.shape, F32) for t in tabs)
    return pl.pallas_call(
        body, name=name, grid=(n_heads,),
        in_specs=[head(0), head(n_heads), head(2 * n_heads)] + extra_specs,
        out_specs=[head(0), head(0), head(0)],
        out_shape=[jax.ShapeDtypeStruct((S, D), F32), jax.ShapeDtypeStruct((S, D), BF16),
                   jax.ShapeDtypeStruct((S, D), F32)],
        scratch_shapes=[pltpu.VMEM((S, HEAD_DIM), BF16)] * 3,
        compiler_params=pltpu.CompilerParams(dimension_semantics=("arbitrary",),
                                             vmem_limit_bytes=_vmem_limit(blk_bytes, 3 * _nbytes((S, HEAD_DIM), BF16))),
    )(hq, hq, hq, *extra)


def _attn_bwd_wide(name, hq, do, o, lse, layer_kind, n_heads, n_sb, cf=None, cft=None, rope_c=None, rope_s=None):
    S = hq.shape[0]
    D = n_heads * HEAD_DIM
    bq = _query_block(S)
    r = bq // BLK
    nq = S // bq
    scale = HEAD_DIM ** -0.5
    even = layer_kind == "even"
    if even:
        tabs = (jnp.where(_causal_tables(bq, False) > 0, 0.0, NEG_INF), _causal_tables(bq, True))
    else:
        tabs = (_dilated_tables(bq),)
    n_tab = tabs[0].shape[0]

    def body(*refs):
        if even:
            (q_ref, k_ref, v_ref, do_ref, o_ref, lse_ref, cf_ref, cft_ref, bias_ref, valid_ref,
             dh_ref, dcft_ref, drow_ref, qs, ks, vs, dos, dq_acc, dk_acc, dv_acc) = refs
        else:
            (q_ref, k_ref, v_ref, do_ref, o_ref, lse_ref, c_ref, s_ref, bias_ref,
             dh_ref, qs, ks, vs, dos, dq_acc, dk_acc, dv_acc) = refs
        h = pl.program_id(0)
        if even:
            qs[...] = q_ref[...].astype(BF16)
            ks[...] = k_ref[...].astype(BF16)

            @pl.when(h == 0)
            def _():
                dcft_ref[...] = jnp.zeros_like(dcft_ref)
                drow_ref[...] = jnp.zeros_like(drow_ref)
        else:
            qs[...] = _rope(q_ref[...], c_ref[...], s_ref[...]).astype(BF16)
            ks[...] = _rope(k_ref[...], c_ref[...], s_ref[...]).astype(BF16)
        vs[...] = v_ref[...].astype(BF16)
        dos[...] = do_ref[...].astype(BF16)
        dk_acc[...] = jnp.zeros_like(dk_acc)
        dv_acc[...] = jnp.zeros_like(dv_acc)

        def softmax_head(hh):
            def qblock(i, carry):
                qi = qs[_qblk(i, bq), :]
                doi = dos[_qblk(i, bq), :]
                dvec = jnp.sum(do_ref[_qblk(i, bq), :] * o_ref[_qblk(i, bq), :], axis=1, keepdims=True)
                lse_i = jnp.max(lse_ref[_qblk(i, bq), :], axis=1, keepdims=True)
                if even:
                    lane = lax.broadcasted_iota(jnp.int32, (bq, LANES), 1)
                    cfq = jnp.sum(jnp.where(lane == hh, cf_ref[_qblk(i, bq), :], 0.0), axis=1, keepdims=True)

                def kblock(j, c):
                    dq, ds_rows = c
                    kj = ks[_blk(j), :]
                    z = _dot_nt(qi, kj) * scale + bias_ref[jnp.minimum(r * i - j + (r - 1), n_tab - 1)]
                    if even:
                        z = z + (cfq - cft_ref[hh, :, _blk(j)])
                    p = jnp.exp(z - lse_i)
                    dp = _dot_nt(doi, vs[_blk(j), :])
                    ds = p * (dp - dvec)
                    dsb = (ds * scale).astype(BF16)
                    dk_acc[_blk(j), :] += _dot_tn(dsb, qi)
                    dv_acc[_blk(j), :] += _dot_tn(p.astype(BF16), doi)
                    if even:
                        dcft_ref[hh, :, _blk(j)] += -jnp.sum(ds, axis=0, keepdims=True)
                    return (dq + jnp.dot(dsb, kj, preferred_element_type=F32),
                            ds_rows + jnp.sum(ds, axis=1, keepdims=True))

                dq, ds_rows = lax.fori_loop(0, r * (i + 1), kblock,
                                            (jnp.zeros((bq, HEAD_DIM), F32), jnp.zeros((bq, 1), F32)))
                dq_acc[_qblk(i, bq), :] = dq
                if even:
                    drow_ref[_qblk(i, bq), :] += jnp.where(lane == hh, ds_rows, 0.0)
                return carry

            lax.fori_loop(0, nq, qblock, 0)

        def sb_head():
            row = lax.broadcasted_iota(jnp.int32, (BLK, BLK), 0)
            col = lax.broadcasted_iota(jnp.int32, (BLK, BLK), 1)
            t_ex = jnp.where(row > col, 1.0, 0.0).astype(BF16)
            t_in = jnp.where(row >= col, 1.0, 0.0).astype(BF16)

            def qblock(i, carry):
                qi = qs[_qblk(i, bq), :]
                doi = dos[_qblk(i, bq), :]
                nkb = r * (i + 1)
                e_tot = jnp.sum(doi.astype(F32) * o_ref[_qblk(i, bq), :], axis=1, keepdims=True)
                zero = jnp.zeros((bq, 1), F32)

                def kblock(jj, c):
                    run, e_run, dq = c
                    j = nkb - 1 - jj
                    kj = ks[_blk(j), :]
                    z = _dot_nt(qi, kj) * scale
                    valid = valid_ref[jnp.minimum(r * i - j + (r - 1), r)]
                    a, m, lsig = _sb_block(z, valid, t_ex, run)
                    sig = jnp.exp(lsig)
                    e = _dot_nt(doi, vs[_blk(j), :]) * a
                    e_before = e_tot - (_split_dot(e, t_in) + e_run)
                    dz = (e * (1.0 - sig) - sig * e_before) * valid
                    dzb = (dz * scale).astype(BF16)
                    dk_acc[_blk(j), :] += _dot_tn(dzb, qi)
                    dv_acc[_blk(j), :] += _dot_tn(a.astype(BF16), doi)
                    return (run + jnp.sum(m, axis=1, keepdims=True), e_run + jnp.sum(e, axis=1, keepdims=True),
                            dq + jnp.dot(dzb, kj, preferred_element_type=F32))

                _, _, dq = lax.fori_loop(0, nkb, kblock, (zero, zero, jnp.zeros((bq, HEAD_DIM), F32)))
                dq_acc[_qblk(i, bq), :] = dq
                return carry

            lax.fori_loop(0, nq, qblock, 0)

        if even:
            @pl.when(h < n_sb)
            def _():
                sb_head()

            @pl.when(h >= n_sb)
            def _():
                softmax_head(h - n_sb)

            dh_ref[0] = dq_acc[...].astype(BF16)
            dh_ref[1] = dk_acc[...].astype(BF16)
        else:
            softmax_head(h)
            dh_ref[0] = _rope_t(dq_acc[...], c_ref[...], s_ref[...]).astype(BF16)
            dh_ref[1] = _rope_t(dk_acc[...], c_ref[...], s_ref[...]).astype(BF16)
        dh_ref[2] = dv_acc[...].astype(BF16)

    head = lambda off: pl.BlockSpec((S, HEAD_DIM), lambda h, off=off: (0, off + h))
    full = pl.BlockSpec((S, LANES), lambda h: (0, 0))
    tfull = pl.BlockSpec((n_heads - n_sb, 1, S), lambda h: (0, 0, 0))
    tab_specs = [pl.BlockSpec(t.shape, lambda h: (0, 0, 0)) for t in tabs]
    dh_spec = pl.BlockSpec((3, S, HEAD_DIM), lambda h: (0, 0, h))
    dh_shape = jax.ShapeDtypeStruct((3, S, D), BF16)
    if even:
        extra_specs, extra = [full, tfull] + tab_specs, (cf, cft) + tabs
        out_specs = [dh_spec, tfull, full]
        out_shape = [dh_shape, jax.ShapeDtypeStruct((n_heads - n_sb, 1, S), F32),
                     jax.ShapeDtypeStruct((S, LANES), F32)]
    else:
        extra_specs, extra = [full, full] + tab_specs, (rope_c, rope_s) + tabs
        out_specs = [dh_spec]
        out_shape = [dh_shape]
    blk_bytes = 10 * _nbytes((S, HEAD_DIM), F32) + sum(_nbytes(t.shape, F32) for t in tabs)
    scratch_bytes = 4 * _nbytes((S, HEAD_DIM), BF16) + 3 * _nbytes((S, HEAD_DIM), F32)
    return pl.pallas_call(
        body, name=name, grid=(n_heads,),
        in_specs=[head(0), head(n_heads), head(2 * n_heads), head(0), head(0), head(0)] + extra_specs,
        out_specs=out_specs, out_shape=out_shape,
        scratch_shapes=[pltpu.VMEM((S, HEAD_DIM), BF16)] * 4 + [pltpu.VMEM((S, HEAD_DIM), F32)] * 3,
        compiler_params=pltpu.CompilerParams(dimension_semantics=("arbitrary",),
                                             vmem_limit_bytes=_vmem_limit(blk_bytes, scratch_bytes)),
    )(hq, hq, hq, do, o, lse, *extra)


def _ffn_fwd(tag, x, g, wgu, wd, fs):
    n = _rms_fwd(tag + "_norm", x, g)
    gu = _mm(tag + "_gu", n, wgu, "nn", F32)
    h = _swiglu_fwd(tag + "_act", gu, fs)
    y = _mm(tag + "_down", h, wd, "nn", F32, res=x, alpha=0.5)
    return y, (x, g, n, gu, h)


def _ffn_bwd(tag, dx, dxb, wgu, wd, fs, saved, after=None):
    x, g, n, gu, h = saved
    dh = _mm(tag + "_dh", dxb, wd, "nt", F32, alpha=0.5, after=after)
    dwd = _mm(tag + "_dwd", h, dxb, "tn", BF16, alpha=0.5)
    dgu = _swiglu_bwd(tag + "_dact", dh, gu, fs)
    dwgu = _mm(tag + "_dwgu", n, dgu, "tn", BF16)
    dn = _mm(tag + "_dn", dgu, wgu, "nt", F32)
    dx_in, dxb_in, dg = _rms_bwd(tag + "_dnorm", dn, x, g, dx)
    return dx_in, dxb_in, dg, dwgu, dwd


def _mixer_fwd(tag, kind, x, g, wqkv, wout, n_heads, n_sb, wf=None, bf=None, rope=None):
    n = _rms_fwd(tag + "_norm", x, g)
    hq = _mm(tag + "_qkv", n, wqkv, "nn", F32)
    if kind == "even":
        hf = _mm(tag + "_gate", n, wf, "nn", F32)
        cf, cft = _gate_fwd(tag + "_cumgate", hf, bf)
        cft = cft[:n_heads - n_sb].reshape(n_heads - n_sb, 1, -1)
        o, ob, lse = _attn_fwd_wide(tag + "_attn", hq, kind, n_heads, n_sb, cf=cf, cft=cft)
    else:
        hf = cf = cft = None
        o, ob, lse = _attn_fwd_wide(tag + "_attn", hq, kind, n_heads, n_sb, rope_c=rope[0], rope_s=rope[1])
    y = _mm(tag + "_out", ob, wout, "nn", F32, res=x)
    return y, (x, g, n, hq, hf, cf, cft, o, ob, lse)


def _mixer_bwd(tag, kind, dx, dxb, wqkv, wout, n_heads, n_sb, saved, wf=None, bf=None, rope=None, after=None):
    x, g, n, hq, hf, cf, cft, o, ob, lse = saved
    do = _mm(tag + "_do", dxb, wout, "nt", F32, after=after)
    dwout = _mm(tag + "_dwout", ob, dxb, "tn", BF16)
    if kind == "even":
        dh3, dcft, drow = _attn_bwd_wide(tag + "_dattn", hq, do, o, lse, kind, n_heads, n_sb, cf=cf, cft=cft)
    else:
        (dh3,) = _attn_bwd_wide(tag + "_dattn", hq, do, o, lse, kind, n_heads, n_sb, rope_c=rope[0], rope_s=rope[1])
    dwqkv = _mm(tag + "_dwqkv", n, dh3, "tn", BF16)
    dn = _mm(tag + "_dn", dh3, wqkv, "nt", F32)
    dwf = db = None
    if kind == "even":
        n_fox = n_heads - n_sb
        dcft = jnp.pad(dcft.reshape(n_fox, -1), ((0, LANES - n_fox), (0, 0)))
        dhf, db = _gate_bwd(tag + "_dcumgate", dcft, drow, hf, bf)
        dwf = _mm(tag + "_dwf", n, dhf, "tn", BF16)
        dn = _mm(tag + "_dn_gate", dhf, wf, "nt", F32, res=dn)
    dx_in, dxb_in, dg = _rms_bwd(tag + "_dnorm", dn, x, g, dx)
    return dx_in, dxb_in, dg, dwqkv, dwout, dwf, db


def _local_step(x, target, w, fs, n_heads, n_sb, emit=None):
    S, D = x.shape
    rope = _rope_tables(S)
    kinds = ("even", "odd")
    saved = []
    h = x
    if callable(w):
        fetch, w = w, {"norm_g": w("norm_g", None), "final_g": w("final_g", None),
                       "wgu1": [None, None], "wd1": [None, None], "wgu2": [None, None], "wd2": [None, None]}
    else:
        fetch = None
    for l, kind in enumerate(kinds):
        ng = [w["norm_g"][l, i][None, :] for i in range(3)]
        if fetch:
            w["wgu1"][l], w["wd1"][l] = fetch(("ffn1", l), h)
        h, s1 = _ffn_fwd(f"l{l}_ffn1", h, ng[0], w["wgu1"][l], w["wd1"][l], fs)
        if fetch:
            w.update(fetch(("mix", l), h))
        if kind == "even":
            h, s2 = _mixer_fwd(f"l{l}_mix", kind, h, ng[1], w["wqkv_e"], w["wout_e"], n_heads, n_sb,
                               wf=w["wf"], bf=w["bf"])
        else:
            h, s2 = _mixer_fwd(f"l{l}_mix", kind, h, ng[1], w["wqkv_o"], w["wout_o"], n_heads, n_sb, rope=rope)
        if fetch:
            w["wgu2"][l], w["wd2"][l] = fetch(("ffn2", l), h)
        h, s3 = _ffn_fwd(f"l{l}_ffn2", h, ng[2], w["wgu2"][l], w["wd2"][l], fs)
        saved.append((s1, s2, s3))

    dx, dxb, dfinal, loss = _loss_head("loss_head", h, w["final_g"], target)
    grads = {"dfinal": dfinal, "dnorm": [[None] * 3 for _ in kinds],
             "dwgu1": [None, None], "dwd1": [None, None], "dwgu2": [None, None], "dwd2": [None, None]}
    token = None
    for l in (1, 0):
        kind = kinds[l]
        s1, s2, s3 = saved[l]
        dx, dxb, dg, grads["dwgu2"][l], grads["dwd2"][l] = _ffn_bwd(
            f"l{l}_ffn2", dx, dxb, w["wgu2"][l], w["wd2"][l], fs, s3, after=token)
        grads["dnorm"][l][2] = dg
        if emit:
            token = emit(("ffn2", l), grads, dx)
        if kind == "even":
            dx, dxb, dg, grads["dwqkv_e"], grads["dwout_e"], grads["dwf"], grads["db"] = _mixer_bwd(
                f"l{l}_mix", kind, dx, dxb, w["wqkv_e"], w["wout_e"], n_heads, n_sb, s2, wf=w["wf"], bf=w["bf"],
                after=token)
        else:
            dx, dxb, dg, grads["dwqkv_o"], grads["dwout_o"], _, _ = _mixer_bwd(
                f"l{l}_mix", kind, dx, dxb, w["wqkv_o"], w["wout_o"], n_heads, n_sb, s2, rope=rope, after=token)
        grads["dnorm"][l][1] = dg
        if emit:
            token = emit(("mix", l), grads, dx)
        dx, dxb, dg, grads["dwgu1"][l], grads["dwd1"][l] = _ffn_bwd(
            f"l{l}_ffn1", dx, dxb, w["wgu1"][l], w["wd1"][l], fs, s1, after=token)
        grads["dnorm"][l][0] = dg
        if emit:
            token = emit(("ffn1", l), grads, dx)
    return loss, dx, grads


def _cast_into(name, shard, layer, chip, full_shape, place, full=None):
    R, C = shard.shape[-2:]
    tr = _row_tile(R, 512, step=16)
    if layer is None:
        in_spec = pl.BlockSpec((tr, C), lambda i, k: (i, 0))
    else:
        in_spec = pl.BlockSpec((None, tr, C), lambda i, k: (layer, i, 0))
    lead = (None,) * (len(full_shape) - 2)
    out_spec = pl.BlockSpec(lead + (tr, C), lambda i, k: place(i, k[0]))

    def body(*refs):
        k_ref, w_ref = refs[:2]
        o_ref = refs[-1]
        o_ref[...] = w_ref[...].astype(BF16)

    in_specs = [in_spec] + ([_ANY] if full is not None else [])
    args = (chip, shard) + ((full,) if full is not None else ())
    grid_spec = pltpu.PrefetchScalarGridSpec(num_scalar_prefetch=1, grid=(R // tr,), in_specs=in_specs, out_specs=out_spec)
    return pl.pallas_call(
        body, name=name, grid_spec=grid_spec, out_shape=jax.ShapeDtypeStruct(full_shape, BF16),
        input_output_aliases={2: 0} if full is not None else {},
        compiler_params=pltpu.CompilerParams(dimension_semantics=("arbitrary",)),
    )(*args)


def _region_shape(grad, kind):
    if kind == "lead":
        return grad.shape[1] // N_CORES, grad.shape[2]
    rows, cols = grad.shape
    if kind == "cols":
        return rows // N_CORES, cols // N_CHIPS
    return rows // (N_CHIPS * N_CORES), cols


def _region_add(name, grad, kind, landed, core):
    rh, cw = _region_shape(grad, kind)
    tr = _row_tile(rh, 256, step=16)
    nrb = rh // tr
    if kind == "cols":
        g_spec = pl.BlockSpec((tr, cw), lambda k, r, c: (c[0] * nrb + r, k))
    elif kind == "rows":
        g_spec = pl.BlockSpec((tr, cw), lambda k, r, c: ((N_CORES * k + c[0]) * nrb + r, 0))
    else:
        g_spec = pl.BlockSpec((None, tr, cw), lambda k, r, c: (k, c[0] * nrb + r, 0))
    l_spec = pl.BlockSpec((None, tr, cw), lambda k, r, c: (k, r, 0))

    def body(c_ref, g_ref, l_ref, o_ref):
        o_ref[...] = (g_ref[...].astype(F32) + l_ref[...].astype(F32)).astype(BF16)

    grid_spec = pltpu.PrefetchScalarGridSpec(
        num_scalar_prefetch=1, grid=(N_CHIPS, nrb), in_specs=[g_spec, l_spec], out_specs=l_spec)
    return pl.pallas_call(
        body, name=name, grid_spec=grid_spec, out_shape=jax.ShapeDtypeStruct(landed.shape, BF16),
        compiler_params=pltpu.CompilerParams(dimension_semantics=("parallel", "parallel"),
                                             vmem_limit_bytes=_vmem_limit(3 * _nbytes((tr, cw), F32))),
    )(core, grad, landed)


def _chip_sum(name, pair, landed, pos):
    _, rh, cw = pair.shape
    tr = _row_tile(rh, max(16, 2**20 // (cw * 4)), step=16)
    nrb = rh // tr

    def body(p_ref, own_ref, l_ref, o_ref):
        acc = own_ref[...].astype(F32)
        for s in range(N_CHIPS - 1):
            acc = acc + l_ref[s].astype(F32)
        o_ref[...] = acc

    grid_spec = pltpu.PrefetchScalarGridSpec(
        num_scalar_prefetch=1, grid=(nrb,),
        in_specs=[pl.BlockSpec((None, tr, cw), lambda r, p: (p[0], r, 0)),
                  pl.BlockSpec((N_CHIPS - 1, tr, cw), lambda r, p: (0, r, 0))],
        out_specs=pl.BlockSpec((tr, cw), lambda r, p: (p[1] * nrb + r, 0)))
    return pl.pallas_call(
        body, name=name, grid_spec=grid_spec, out_shape=jax.ShapeDtypeStruct((N_CORES * rh, cw), F32),
        compiler_params=pltpu.CompilerParams(dimension_semantics=("arbitrary",)),
    )(pos, pair, landed)


def _sum_leading(name, parts):
    n, R, C = parts.shape
    tr = _row_tile(R, max(8, (2**20 // (C * 4)) // 8 * 8))

    def body(p_ref, o_ref):
        acc = p_ref[0]
        for s in range(1, n):
            acc = acc + p_ref[s]
        o_ref[...] = acc

    return pl.pallas_call(
        body, name=name, grid=(R // tr,),
        in_specs=[pl.BlockSpec((n, tr, C), lambda i: (0, i, 0))],
        out_specs=pl.BlockSpec((tr, C), lambda i: (i, 0)),
        out_shape=jax.ShapeDtypeStruct((R, C), F32),
        compiler_params=pltpu.CompilerParams(dimension_semantics=("parallel",)),
    )(parts)


def _adamw(name, w, g, m, v):
    shape = w.shape
    to2d = lambda t: t.reshape(-1, shape[-1]) if t.ndim > 1 else t.reshape(1, -1)
    w2, g2, m2, v2 = (to2d(t) for t in (w, g, m, v))
    R, C = w2.shape
    tr = _row_tile(R, 256)

    def body(w_ref, g_ref, m_ref, v_ref, d_ref, nm_ref, nv_ref):
        gv = g_ref[...]
        nm = ADAM_B1 * m_ref[...] + (1.0 - ADAM_B1) * gv
        nv = ADAM_B2 * v_ref[...] + (1.0 - ADAM_B2) * (gv * gv)
        m_hat = nm / (1.0 - ADAM_B1 ** ADAM_STEP)
        v_hat = nv / (1.0 - ADAM_B2 ** ADAM_STEP)
        d_ref[...] = -ADAM_LR * (m_hat / (jnp.sqrt(v_hat) + ADAM_EPS) + ADAM_WD * w_ref[...])
        nm_ref[...] = nm
        nv_ref[...] = nv

    spec = pl.BlockSpec((tr, C), lambda i: (i, 0))
    sds = jax.ShapeDtypeStruct((R, C), F32)
    d, nm, nv = pl.pallas_call(
        body, name=name, grid=(R // tr,), in_specs=[spec] * 4, out_specs=[spec] * 3, out_shape=[sds] * 3,
        compiler_params=pltpu.CompilerParams(dimension_semantics=("parallel",),
                                             vmem_limit_bytes=_vmem_limit(7 * _nbytes((tr, C), F32))),
    )(w2, g2, m2, v2)
    return d.reshape(shape), nm.reshape(shape), nv.reshape(shape)


_ANY = pl.BlockSpec(memory_space=pl.ANY)


def _mesh_pos():
    return lax.axis_index("x"), lax.axis_index("y"), lax.axis_index("c")


def _other_chips(x, y):
    return [(1 - x, y), (x, 1 - y), (1 - x, 1 - y)]


def _gather_over_chips(name, fulls, views):
    n = len(views)
    nf = len(fulls)

    def body(*refs):
        full = refs[nf:2 * nf]
        ici_send, ici_recv, d2d_send, d2d_recv = refs[2 * nf:]
        x, y, c = _mesh_pos()
        chips = _other_chips(x, y)
        mine = 2 * x + y
        sibling = (x, y, 1 - c)

        def ici(a, p, k):
            i, view, _ = views[a]
            part = view(full[i], k, c)
            return pltpu.make_async_remote_copy(
                src_ref=part, dst_ref=part, send_sem=ici_send.at[a, p], recv_sem=ici_recv.at[a, p],
                device_id=(*chips[p], c), device_id_type=MESH)

        def d2d(a, p, h):
            i, view, _ = views[a]
            px, py = chips[p]
            part = view(full[i], 2 * px + py, h)
            return pltpu.make_async_remote_copy(
                src_ref=part, dst_ref=part, send_sem=d2d_send.at[a, p], recv_sem=d2d_recv.at[a, p],
                device_id=sibling, device_id_type=MESH)

        sends = [ici(a, p, mine) for a in range(n) for p in range(3)]
        for cp in sends:
            cp.start()
        passed = []
        for a in range(n):
            for p, (px, py) in enumerate(chips):
                ici(a, p, 2 * px + py).wait_recv()
                if views[a][2]:
                    fwd = d2d(a, p, c)
                    fwd.start()
                    passed.append(fwd)
        for a in range(n):
            if views[a][2]:
                for p in range(3):
                    d2d(a, p, 1 - c).wait_recv()
        for cp in sends + passed:
            cp.wait_send()

    return pl.pallas_call(
        body, name=name, in_specs=[_ANY] * nf, out_specs=[_ANY] * nf,
        out_shape=[jax.ShapeDtypeStruct(f.shape, f.dtype) for f in fulls],
        input_output_aliases={i: i for i in range(nf)},
        scratch_shapes=[pltpu.SemaphoreType.DMA((n, 3))] * 4,
        compiler_params=pltpu.CompilerParams(has_side_effects=True),
    )(*fulls)


_HBM = pl.BlockSpec(memory_space=pltpu.HBM)
_SEM = pl.BlockSpec(memory_space=pltpu.SEMAPHORE)


def _in_hbm(arrays):
    return [pltpu.with_memory_space_constraint(a, pltpu.HBM) for a in arrays]


def _gather_start(name, fulls, views):
    nf = len(fulls)
    ng = 1 + max(g for _, _, g in views)

    def body(*refs):
        full = refs[nf:2 * nf]
        send_sems, recv_sems = refs[2 * nf:2 * nf + ng], refs[2 * nf + ng:]
        x, y, c = _mesh_pos()
        chips = _other_chips(x, y)
        for i, view, g in views:
            part = view(full[i], 2 * x + y, c)
            for px, py in chips:
                pltpu.make_async_remote_copy(
                    src_ref=part, dst_ref=part, send_sem=send_sems[g], recv_sem=recv_sems[g],
                    device_id=(px, py, c), device_id_type=MESH).start()

    outs = pl.pallas_call(
        body, name=name, in_specs=[_HBM] * nf, out_specs=[_HBM] * nf + [_SEM] * (2 * ng),
        out_shape=[pltpu.HBM(f.shape, f.dtype) for f in fulls] + [pltpu.SemaphoreType.DMA(())] * (2 * ng),
        input_output_aliases={i: i for i in range(nf)},
        compiler_params=pltpu.CompilerParams(has_side_effects=pltpu.SideEffectType.DATAFLOW_SIDE_EFFECTING),
    )(*_in_hbm(fulls))
    return list(outs[:nf]), list(outs[nf:nf + ng]), list(outs[nf + ng:])


def _gather_wait(name, fulls, views, send_sem, recv_sem, after):
    nf = len(fulls)

    def body(*refs):
        send_ref, recv_ref = refs[nf], refs[nf + 1]
        full = refs[nf + 3:]
        x, y, c = _mesh_pos()
        copies = [pltpu.make_async_remote_copy(
            src_ref=view(full[i], 2 * x + y, c), dst_ref=view(full[i], 2 * px + py, c),
            send_sem=send_ref, recv_sem=recv_ref, device_id=(px, py, c), device_id_type=MESH)
            for i, view in views for px, py in _other_chips(x, y)]
        for cp in copies:
            cp.wait_send()
        for cp in copies:
            cp.wait_recv()

    outs = pl.pallas_call(
        body, name=name, in_specs=[_HBM] * nf + [_SEM, _SEM, _ANY], out_specs=[_HBM] * nf,
        out_shape=[pltpu.HBM(f.shape, f.dtype) for f in fulls],
        input_output_aliases={i: i for i in range(nf)},
        compiler_params=pltpu.CompilerParams(has_side_effects=pltpu.SideEffectType.DATAFLOW_SIDE_EFFECTING),
    )(*fulls, send_sem, recv_sem, after)
    return list(outs)


def _forward_to_sibling(name, fulls, views):
    n, nf = len(views), len(fulls)

    def body(*refs):
        full = refs[nf:2 * nf]
        send_sems, recv_sems = refs[2 * nf:]
        x, y, c = _mesh_pos()
        chips = _other_chips(x, y)

        def copy(a, p, h):
            i, view = views[a]
            px, py = chips[p]
            part = view(full[i], 2 * px + py, h)
            return pltpu.make_async_remote_copy(
                src_ref=part, dst_ref=part, send_sem=send_sems.at[a, p], recv_sem=recv_sems.at[a, p],
                device_id=(x, y, 1 - c), device_id_type=MESH)

        sends = [copy(a, p, c) for a in range(n) for p in range(3)]
        for cp in sends:
            cp.start()
        for a in range(n):
            for p in range(3):
                copy(a, p, 1 - c).wait_recv()
        for cp in sends:
            cp.wait_send()

    return pl.pallas_call(
        body, name=name, in_specs=[_ANY] * nf, out_specs=[_ANY] * nf,
        out_shape=[jax.ShapeDtypeStruct(f.shape, f.dtype) for f in fulls],
        input_output_aliases={i: i for i in range(nf)},
        scratch_shapes=[pltpu.SemaphoreType.DMA((n, 3))] * 2,
        compiler_params=pltpu.CompilerParams(has_side_effects=True),
    )(*fulls)


def _region_view(ref, kind, k, c):
    if kind == "lead":
        rh = ref.shape[1] // N_CORES
        return ref.at[k, pl.ds(pl.multiple_of(c * rh, 8), rh), :]
    rows, cols = ref.shape
    if kind == "cols":
        rh, cw = rows // N_CORES, cols // N_CHIPS
        return ref.at[pl.ds(pl.multiple_of(c * rh, 8), rh), pl.ds(k * cw, cw)]
    rh = rows // (N_CHIPS * N_CORES)
    return ref.at[pl.ds(pl.multiple_of((N_CORES * k + c) * rh, 8), rh), :]


def _send_to_sibling(name, grads, kinds):
    n = len(grads)
    shapes = [jax.ShapeDtypeStruct((N_CHIPS,) + _region_shape(g, kd), g.dtype) for g, kd in zip(grads, kinds)]

    def body(*refs):
        g_ref, land = refs[:n], refs[n:2 * n]
        send_sems, recv_sems = refs[2 * n:]
        x, y, c = _mesh_pos()
        copies = []
        for a in range(n):
            for k in range(N_CHIPS):
                cp = pltpu.make_async_remote_copy(
                    src_ref=_region_view(g_ref[a], kinds[a], k, 1 - c), dst_ref=land[a].at[k],
                    send_sem=send_sems.at[a, k], recv_sem=recv_sems.at[a, k],
                    device_id=(x, y, 1 - c), device_id_type=MESH)
                cp.start()
                copies.append(cp)
        for cp in copies:
            cp.wait_recv()
        for cp in copies:
            cp.wait_send()

    return pl.pallas_call(
        body, name=name, in_specs=[_ANY] * n, out_specs=[_ANY] * n, out_shape=shapes,
        scratch_shapes=[pltpu.SemaphoreType.DMA((n, N_CHIPS)), pltpu.SemaphoreType.DMA((n, N_CHIPS))],
        compiler_params=pltpu.CompilerParams(has_side_effects=True),
    )(*grads)


def _scatter_over_chips(name, pair_sums):
    n = len(pair_sums)

    def body(*refs):
        p_ref, land = refs[:n], refs[n:2 * n]
        send_sems, recv_sems = refs[2 * n:]
        x, y, c = _mesh_pos()
        chips = _other_chips(x, y)
        sends = []
        for a in range(n):
            for p, (px, py) in enumerate(chips):
                cp = pltpu.make_async_remote_copy(
                    src_ref=p_ref[a].at[2 * px + py], dst_ref=land[a].at[p], send_sem=send_sems.at[a, p],
                    recv_sem=recv_sems.at[a, p], device_id=(px, py, c), device_id_type=MESH)
                cp.start()
                sends.append(cp)
        for cp in sends:
            cp.wait_recv()
        for cp in sends:
            cp.wait_send()

    return pl.pallas_call(
        body, name=name, in_specs=[_ANY] * n, out_specs=[_ANY] * n,
        out_shape=[jax.ShapeDtypeStruct((N_CHIPS - 1,) + p.shape[1:], p.dtype) for p in pair_sums],
        scratch_shapes=[pltpu.SemaphoreType.DMA((n, 3)), pltpu.SemaphoreType.DMA((n, 3))],
        compiler_params=pltpu.CompilerParams(has_side_effects=True),
    )(*pair_sums)


def _scatter_start(name, pair_sums):
    n = len(pair_sums)
    lands = [lax.empty((N_CHIPS - 1,) + p.shape[1:], p.dtype) for p in pair_sums]

    def body(*refs):
        p_ref, land = refs[2 * n:3 * n], refs[3 * n:4 * n]
        send_sem, recv_sem, token = refs[4 * n:]
        x, y, c = _mesh_pos()
        for a in range(n):
            for p, (px, py) in enumerate(_other_chips(x, y)):
                pltpu.make_async_remote_copy(
                    src_ref=p_ref[a].at[2 * px + py], dst_ref=land[a].at[p], send_sem=send_sem, recv_sem=recv_sem,
                    device_id=(px, py, c), device_id_type=MESH).start()
        token[...] = jnp.zeros_like(token)

    outs = pl.pallas_call(
        body, name=name, in_specs=[_HBM] * (2 * n),
        out_specs=[_HBM] * (2 * n) + [_SEM, _SEM, pl.BlockSpec(memory_space=pltpu.VMEM)],
        out_shape=[pltpu.HBM(t.shape, t.dtype) for t in list(pair_sums) + lands]
        + [pltpu.SemaphoreType.DMA(()), pltpu.SemaphoreType.DMA(()), jax.ShapeDtypeStruct((8, LANES), F32)],
        input_output_aliases={i: i for i in range(2 * n)},
        compiler_params=pltpu.CompilerParams(has_side_effects=pltpu.SideEffectType.DATAFLOW_SIDE_EFFECTING),
    )(*_in_hbm(list(pair_sums) + lands))
    return list(outs[:n]), list(outs[n:2 * n]), outs[2 * n], outs[2 * n + 1], outs[2 * n + 2]


def _scatter_wait(name, pair_sums, lands, send_sem, recv_sem, after):
    n = len(pair_sums)

    def body(*refs):
        send_ref, recv_ref = refs[2 * n], refs[2 * n + 1]
        p_ref, land = refs[2 * n + 3:3 * n + 3], refs[3 * n + 3:]
        x, y, c = _mesh_pos()
        copies = [pltpu.make_async_remote_copy(
            src_ref=p_ref[a].at[2 * px + py], dst_ref=land[a].at[p], send_sem=send_ref, recv_sem=recv_ref,
            device_id=(px, py, c), device_id_type=MESH)
            for a in range(n) for p, (px, py) in enumerate(_other_chips(x, y))]
        for cp in copies:
            cp.wait_send()
        for cp in copies:
            cp.wait_recv()

    outs = pl.pallas_call(
        body, name=name, in_specs=[_HBM] * (2 * n) + [_SEM, _SEM, _ANY], out_specs=[_HBM] * (2 * n),
        out_shape=[pltpu.HBM(t.shape, t.dtype) for t in list(pair_sums) + list(lands)],
        input_output_aliases={i: i for i in range(2 * n)},
        compiler_params=pltpu.CompilerParams(has_side_effects=pltpu.SideEffectType.DATAFLOW_SIDE_EFFECTING),
    )(*pair_sums, *lands, send_sem, recv_sem, after)
    return list(outs[:n]), list(outs[n:])


def _swap_halves(name, shards):
    n = len(shards)

    def body(*refs):
        out = refs[n:2 * n]
        send_sems, recv_sems = refs[2 * n:]
        x, y, c = _mesh_pos()
        sends = []
        for a in range(n):
            rh = out[a].shape[0] // N_CORES
            mine = out[a].at[pl.ds(pl.multiple_of(c * rh, 8), rh), :]
            cp = pltpu.make_async_remote_copy(
                src_ref=mine, dst_ref=mine, send_sem=send_sems.at[a], recv_sem=recv_sems.at[a],
                device_id=(x, y, 1 - c), device_id_type=MESH)
            cp.start()
            sends.append(cp)
        for a in range(n):
            rh = out[a].shape[0] // N_CORES
            theirs = out[a].at[pl.ds(pl.multiple_of((1 - c) * rh, 8), rh), :]
            pltpu.make_async_remote_copy(
                src_ref=theirs, dst_ref=theirs, send_sem=send_sems.at[a], recv_sem=recv_sems.at[a],
                device_id=(x, y, 1 - c), device_id_type=MESH).wait_recv()
        for cp in sends:
            cp.wait_send()

    return pl.pallas_call(
        body, name=name, in_specs=[_ANY] * n, out_specs=[_ANY] * n,
        out_shape=[jax.ShapeDtypeStruct(s.shape, s.dtype) for s in shards],
        input_output_aliases={i: i for i in range(n)},
        scratch_shapes=[pltpu.SemaphoreType.DMA((n,)), pltpu.SemaphoreType.DMA((n,))],
        compiler_params=pltpu.CompilerParams(has_side_effects=True),
    )(*shards)


def _gather_all_devices(name, block):
    R, C = block.shape
    ndev = N_CHIPS * N_CORES

    def body(b_ref, out_ref, send_sems, recv_sems, local_sem):
        x, y, c = _mesh_pos()
        mine = 4 * x + 2 * y + c
        own = pltpu.make_async_copy(b_ref, out_ref.at[mine], local_sem)
        own.start()
        sends = []
        for mask in range(1, ndev):
            fx, fy, fc = (mask >> 2) & 1, (mask >> 1) & 1, mask & 1
            px, py, pc = x ^ fx, y ^ fy, c ^ fc
            cp = pltpu.make_async_remote_copy(
                src_ref=b_ref, dst_ref=out_ref.at[mine], send_sem=send_sems.at[mask - 1],
                recv_sem=recv_sems.at[mask - 1], device_id=(px, py, pc), device_id_type=MESH)
            cp.start()
            sends.append(cp)
        for mask in range(1, ndev):
            fx, fy, fc = (mask >> 2) & 1, (mask >> 1) & 1, mask & 1
            px, py, pc = x ^ fx, y ^ fy, c ^ fc
            pltpu.make_async_remote_copy(
                src_ref=b_ref, dst_ref=out_ref.at[4 * px + 2 * py + pc], send_sem=send_sems.at[mask - 1],
                recv_sem=recv_sems.at[mask - 1], device_id=(px, py, pc), device_id_type=MESH).wait_recv()
        for cp in sends:
            cp.wait_send()
        own.wait()

    return pl.pallas_call(
        body, name=name, in_specs=[_ANY], out_specs=_ANY,
        out_shape=jax.ShapeDtypeStruct((ndev, R, C), F32),
        scratch_shapes=[pltpu.SemaphoreType.DMA((ndev - 1,)), pltpu.SemaphoreType.DMA((ndev - 1,)),
                        pltpu.SemaphoreType.DMA(())],
        compiler_params=pltpu.CompilerParams(has_side_effects=True),
    )(block)


def _reduce_scatter(grads, kinds, pos):
    landed = _send_to_sibling("rs_pair_send", grads, kinds)
    pair = [_region_add(f"rs_pair_add_{a}", g, kd, ld, pos[1:]) for a, (g, kd, ld) in enumerate(zip(grads, kinds, landed))]
    parts = _scatter_over_chips("rs_chip_send", pair)
    shards = [_chip_sum(f"rs_chip_add_{a}", p, ld, pos) for a, (p, ld) in enumerate(zip(pair, parts))]
    return _swap_halves("rs_swap_halves", shards)


def kernel(x, norm_g, ffn1_w_gate, ffn1_w_up, ffn1_w_down, ffn2_w_gate, ffn2_w_up, ffn2_w_down, even_w_in, even_b_forget, even_w_out, odd_w_qkv, odd_w_out, final_norm_g, loss_target, m_norm_g, m_ffn1_w_gate, m_ffn1_w_up, m_ffn1_w_down, m_ffn2_w_gate, m_ffn2_w_up, m_ffn2_w_down, m_even_w_in, m_even_b_forget, m_even_w_out, m_odd_w_qkv, m_odd_w_out, m_final_norm_g, v_norm_g, v_ffn1_w_gate, v_ffn1_w_up, v_ffn1_w_down, v_ffn2_w_gate, v_ffn2_w_up, v_ffn2_w_down, v_even_w_in, v_even_b_forget, v_even_w_out, v_odd_w_qkv, v_odd_w_out, v_final_norm_g):
    _, S, D = x.shape
    L = norm_g.shape[0]
    assert L == 2 and even_w_in.shape[0] == 1 and odd_w_qkv.shape[0] == 1
    fs = ffn1_w_gate.shape[2]
    F = N_CHIPS * fs
    wc = even_w_in.shape[2]
    n_heads = D // HEAD_DIM
    n_fox = N_CHIPS * wc - 3 * D
    n_sb = n_heads - n_fox
    qs = odd_w_qkv.shape[2]
    os_ = even_w_out.shape[1]
    ns = norm_g.shape[2]
    xi, yi, ci = _mesh_pos()
    chip = 2 * xi + yi

    pos = jnp.stack([chip, ci]).astype(jnp.int32)
    kchip = pos[:1]
    lane = lambda start, size: pl.ds(pl.multiple_of(start, LANES), size)
    sub = lambda start, size: pl.ds(pl.multiple_of(start, 16), size)
    gate_view = lambda r, k, h: r.at[sub(h * (D // 2), D // 2), lane(k * 2 * fs, fs)]
    up_view = lambda r, k, h: r.at[sub(h * (D // 2), D // 2), lane(k * 2 * fs + fs, fs)]
    down_view = lambda r, k, h: r.at[sub(k * fs + h * (fs // 2), fs // 2), :]
    out_view = lambda r, k, h: r.at[sub(k * os_ + h * (os_ // 2), os_ // 2), :]
    tr_d = _row_tile(fs, 512, step=16)
    tr_o = _row_tile(os_, 512, step=16)
    ffn_w = {"ffn1": (ffn1_w_gate, ffn1_w_up, ffn1_w_down), "ffn2": (ffn2_w_gate, ffn2_w_up, ffn2_w_down)}
    win_view = lambda r, k, h: r.at[k, sub(h * (D // 2), D // 2), :]
    qkv_view = lambda r, k, h: r.at[sub(h * (D // 2), D // 2), lane(k * qs, qs)]
    fulls, views, groups = [], [], {}
    for l in range(L):
        for blk in ("ffn1", "mix", "ffn2"):
            o, v0 = len(fulls), len(views)
            if blk == "mix" and l == 0:
                fulls += [_cast_into("cast_win", even_w_in, 0, kchip, (N_CHIPS, D, wc), lambda i, k: (k, i, 0)),
                          _cast_into("cast_wout_e", even_w_out, 0, kchip, (D, D), lambda i, k: (k * (os_ // tr_o) + i, 0))]
                views += [(o, win_view), (o + 1, out_view)]
            elif blk == "mix":
                fulls += [_cast_into("cast_wqkv_o", odd_w_qkv, 0, kchip, (D, N_CHIPS * qs), lambda i, k: (i, k)),
                          _cast_into("cast_wout_o", odd_w_out, 0, kchip, (D, D), lambda i, k: (k * (os_ // tr_o) + i, 0))]
                views += [(o, qkv_view), (o + 1, out_view)]
            else:
                wg, wu, wd = ffn_w[blk]
                t = f"cast_{blk}_l{l}"
                gu = _cast_into(t + "_gate", wg, l, kchip, (D, 2 * F), lambda i, k: (i, 2 * k))
                gu = _cast_into(t + "_up", wu, l, kchip, (D, 2 * F), lambda i, k: (i, 2 * k + 1), full=gu)
                dn = _cast_into(t + "_down", wd, l, kchip, (F, D), lambda i, k: (k * (fs // tr_d) + i, 0))
                fulls += [gu, dn]
                views += [(o, gate_view), (o, up_view), (o + 1, down_view)]
            gid = len(groups)
            views[v0:] = [(i, view, gid) for i, view in views[v0:]]
            groups[(blk, l)] = (gid, list(range(o, len(fulls))), list(range(v0, len(views))))
    norm_own = lax.dynamic_update_slice(jnp.zeros((L, 3, N_CHIPS * ns), F32), norm_g, (0, 0, chip * ns))
    (norm_full,) = _gather_over_chips("gather_norm", [norm_own], [(0, lambda r, k, h: r.at[:, :, lane(k * ns, ns)], False)])
    started, send_sems, recv_sems = _gather_start("gather_start", fulls, views)

    def fetch(block, after):
        if block == "norm_g":
            return norm_full
        if block == "final_g":
            return final_norm_g[None, :]
        gid, arrays, rows = groups[block]
        tag = f"{block[0]}_l{block[1]}"
        local = [(views[a][0] - arrays[0], views[a][1]) for a in rows]
        got = _gather_wait("gather_wait_" + tag, [started[i] for i in arrays], local, send_sems[gid], recv_sems[gid],
                           x if after is None else after)
        got = _forward_to_sibling("gather_pass_" + tag, got, local)
        if block[0] != "mix":
            return got
        if block[1] == 1:
            return {"wqkv_o": got[0], "wout_o": got[1]}
        win = jnp.concatenate([got[0][k] for k in range(N_CHIPS)], axis=1)
        return {"wqkv_e": win[:, :3 * D], "wf": jnp.pad(win[:, 3 * D:], ((0, 0), (0, LANES - n_fox))),
                "bf": jnp.pad(even_b_forget, ((0, 0), (0, LANES - n_fox))), "wout_e": got[1]}

    pending, shards = [], {}

    def finish(after):
        block, pair, lands, ssem, rsem = pending.pop(0)
        tag = f"{block[0]}_l{block[1]}"
        pair, lands = _scatter_wait("rs_chip_wait_" + tag, pair, lands, ssem, rsem, after)
        shards[block] = [_chip_sum(f"rs_chip_add_{tag}_{a}", p, ld, pos) for a, (p, ld) in enumerate(zip(pair, lands))]

    def emit(block, g, dx):
        blk, l = block
        tag = f"{blk}_l{l}"
        if blk == "mix" and l == 0:
            dwin = jnp.concatenate([g["dwqkv_e"], g["dwf"][:, :n_fox]], axis=1)
            mats = [jnp.stack([dwin[:, k * wc:(k + 1) * wc] for k in range(N_CHIPS)]), g["dwout_e"]]
            kinds = ["lead", "rows"]
        elif blk == "mix":
            mats, kinds = [g["dwqkv_o"], g["dwout_o"]], ["cols", "rows"]
        else:
            f = blk[-1]
            mats, kinds = [g["dwgu" + f][l], g["dwd" + f][l]], ["cols", "rows"]
        landed = _send_to_sibling("rs_pair_send_" + tag, mats, kinds)
        pair = [_region_add(f"rs_pair_add_{tag}_{a}", m, kd, ld, pos[1:])
                for a, (m, kd, ld) in enumerate(zip(mats, kinds, landed))]
        pair, lands, ssem, rsem, token = _scatter_start("rs_chip_start_" + tag, pair)
        if pending:
            finish(token)
        pending.append((block, pair, lands, ssem, rsem))
        return token

    loss_vec, grad_x, g = _local_step(x[0], loss_target[0], fetch, fs, n_heads, n_sb, emit=emit)
    finish(grad_x)
    order = [(b, l) for b in ("ffn1", "ffn2", "mix") for l in range(L)]
    red = _swap_halves("rs_swap_halves", [s for b in order for s in shards[b]])
    red = {b: red[2 * i:2 * i + 2] for i, b in enumerate(order)}
    gu1, gd1 = [red[("ffn1", l)][0] for l in range(L)], [red[("ffn1", l)][1] for l in range(L)]
    gu2, gd2 = [red[("ffn2", l)][0] for l in range(L)], [red[("ffn2", l)][1] for l in range(L)]
    (g_win, g_wout_e), (g_qkv_o, g_wout_o) = red[("mix", 0)], red[("mix", 1)]

    small_rows = [g["dnorm"][l][i] for l in range(L) for i in range(3)] + [
        g["dfinal"], jnp.pad(g["db"], ((0, 0), (0, D - LANES))), jnp.pad(loss_vec, ((0, 0), (0, D - LANES)))]
    small = jnp.concatenate(small_rows + [jnp.zeros((16 - len(small_rows), D), F32)], axis=0)
    small_sum = _sum_leading("small_sum", _gather_all_devices("small_gather", small))
    loss = small_sum[3 * L + 2, 0]
    g_norm = lax.dynamic_slice_in_dim(small_sum[:3 * L].reshape(L, 3, D), chip * ns, ns, axis=2)
    g_final = small_sum[3 * L]
    g_bf = small_sum[3 * L + 1, :n_fox][None, :]

    grads = [
        g_norm,
        jnp.stack([t[:, :fs] for t in gu1]), jnp.stack([t[:, fs:] for t in gu1]), jnp.stack(gd1),
        jnp.stack([t[:, :fs] for t in gu2]), jnp.stack([t[:, fs:] for t in gu2]), jnp.stack(gd2),
        g_win[None], g_bf, g_wout_e[None], g_qkv_o[None], g_wout_o[None], g_final]
    weights = [norm_g, ffn1_w_gate, ffn1_w_up, ffn1_w_down, ffn2_w_gate, ffn2_w_up, ffn2_w_down,
               even_w_in, even_b_forget, even_w_out, odd_w_qkv, odd_w_out, final_norm_g]
    ms = [m_norm_g, m_ffn1_w_gate, m_ffn1_w_up, m_ffn1_w_down, m_ffn2_w_gate, m_ffn2_w_up, m_ffn2_w_down,
          m_even_w_in, m_even_b_forget, m_even_w_out, m_odd_w_qkv, m_odd_w_out, m_final_norm_g]
    vs = [v_norm_g, v_ffn1_w_gate, v_ffn1_w_up, v_ffn1_w_down, v_ffn2_w_gate, v_ffn2_w_up, v_ffn2_w_down,
          v_even_w_in, v_even_b_forget, v_even_w_out, v_odd_w_qkv, v_odd_w_out, v_final_norm_g]
    deltas, new_ms, new_vs = [], [], []
    for i, (wt, gt, mt, vt) in enumerate(zip(weights, grads, ms, vs)):
        d, nm, nv = _adamw(f"adamw_{i}", wt, gt, mt, vt)
        deltas.append(d)
        new_ms.append(nm)
        new_vs.append(nv)
    return (loss, grad_x[None], *grads, *deltas, *new_ms, *new_vs)
```

```python
import functools
import math

import jax
import jax.numpy as jnp
from jax import lax
from jax.experimental import pallas as pl
from jax.experimental.pallas import tpu as pltpu

F32 = jnp.float32
BF16 = jnp.bfloat16

HEAD_DIM = 128
ROPE_DIMS = 32
ROPE_THETA = 500000.0
DILATED_PATTERNS = ((128, 1), (512, 4), (2048, 16))
RMS_EPS = 1e-6
NEG_INF = -1e30
ADAM_LR = 0.001
ADAM_B1 = 0.9
ADAM_B2 = 0.999
ADAM_EPS = 1e-08
ADAM_WD = 0.01
ADAM_STEP = 10

N_CHIPS = 4
N_CORES = 2
LANES = 128
BLK = 128
VMEM_BYTES_V7X = 64 * 2**20
MESH = pl.DeviceIdType.MESH


def _vmem_limit(block_bytes, scratch_bytes=0):
    need = 2 * block_bytes + scratch_bytes + 12 * 2**20
    return int(min(need, VMEM_BYTES_V7X - 6 * 2**20))


def _nbytes(shape, dtype):
    return math.prod(shape) * jnp.dtype(dtype).itemsize


def _tile(dim, target):
    best = None
    for t in range(LANES, min(dim, target) + 1, LANES):
        if dim % t == 0:
            best = t
    assert best is not None, (dim, target)
    return best


def _row_tile(rows, target, step=8):
    if rows <= target:
        return rows
    best = None
    for t in range(step, target + 1, step):
        if rows % t == 0:
            best = t
    assert best is not None, (rows, target)
    return best


def _mm(name, a, b, mode, out_dtype, res=None, alpha=1.0, after=None, tm_target=1024, tn_target=1536, tk_target=2048):
    a3 = a.ndim == 3
    b3 = b.ndim == 3
    if mode == "nn":
        assert not a3 and not b3
        (M, K), (K2, N) = a.shape, b.shape
    elif mode == "nt":
        assert not b3
        if a3:
            P, M, Kp = a.shape
            K = P * Kp
        else:
            M, K = a.shape
        N, K2 = b.shape
    else:
        assert mode == "tn" and not a3
        K, M = a.shape
        if b3:
            P, K2, Np = b.shape
            N = P * Np
        else:
            K2, N = b.shape
    assert K == K2, (name, a.shape, b.shape)
    tm = _tile(M, tm_target)
    tn = _tile(Np if b3 else N, tn_target)
    tk = _tile(Kp if a3 else K, tk_target)
    nk = K // tk
    grid = (M // tm, N // tn, nk)

    if mode == "nn":
        a_spec = pl.BlockSpec((tm, tk), lambda i, j, k: (i, k))
        b_spec = pl.BlockSpec((tk, tn), lambda i, j, k: (k, j))
        dims = (((1,), (0,)), ((), ()))
    elif mode == "nt":
        if a3:
            nkp = Kp // tk
            a_spec = pl.BlockSpec((None, tm, tk), lambda i, j, k: (k // nkp, i, k % nkp))
        else:
            a_spec = pl.BlockSpec((tm, tk), lambda i, j, k: (i, k))
        b_spec = pl.BlockSpec((tn, tk), lambda i, j, k: (j, k))
        dims = (((1,), (1,)), ((), ()))
    else:
        a_spec = pl.BlockSpec((tk, tm), lambda i, j, k: (k, i))
        if b3:
            njp = Np // tn
            b_spec = pl.BlockSpec((None, tk, tn), lambda i, j, k: (j // njp, k, j % njp))
        else:
            b_spec = pl.BlockSpec((tk, tn), lambda i, j, k: (k, j))
        dims = (((0,), (0,)), ((), ()))
    o_spec = pl.BlockSpec((tm, tn), lambda i, j, k: (i, j))
    has_res = res is not None

    def finish(y, r_ref, o_ref):
        if alpha != 1.0:
            y = y * alpha
        if has_res:
            y = r_ref[...] + y
        o_ref[...] = y.astype(o_ref.dtype)

    n_in = 2 + has_res + (after is not None)

    def body(*refs):
        a_ref, b_ref = refs[:2]
        r_ref = refs[2] if has_res else None
        o_ref = refs[n_in]
        part = lax.dot_general(a_ref[...], b_ref[...], dims, preferred_element_type=F32)
        if nk == 1:
            finish(part, r_ref, o_ref)
            return
        acc_ref = refs[-1]
        k = pl.program_id(2)

        @pl.when(k == 0)
        def _():
            acc_ref[...] = part

        @pl.when(k > 0)
        def _():
            acc_ref[...] += part

        @pl.when(k == nk - 1)
        def _():
            finish(acc_ref[...], r_ref, o_ref)

    in_specs = [a_spec, b_spec] + ([o_spec] if has_res else []) + ([_ANY] if after is not None else [])
    args = (a, b) + ((res,) if has_res else ()) + ((after,) if after is not None else ())
    blk = (_nbytes((tm, tk), a.dtype) + _nbytes((tk, tn), b.dtype) + _nbytes((tm, tn), out_dtype)
           + (_nbytes((tm, tn), F32) if has_res else 0))
    return pl.pallas_call(
        body, name=name, grid=grid, in_specs=in_specs, out_specs=o_spec,
        out_shape=jax.ShapeDtypeStruct((M, N), out_dtype),
        scratch_shapes=[pltpu.VMEM((tm, tn), F32)] if nk > 1 else [],
        compiler_params=pltpu.CompilerParams(
            dimension_semantics=("parallel", "parallel", "arbitrary"),
            vmem_limit_bytes=_vmem_limit(blk, 2 * _nbytes((tm, tn), F32))),
    )(*args)


def _rms_fwd(name, x, g):
    S, D = x.shape
    tr = _row_tile(S, 256)

    def body(x_ref, g_ref, n_ref):
        xv = x_ref[...]
        r = lax.rsqrt(jnp.mean(xv * xv, axis=-1, keepdims=True) + RMS_EPS)
        n_ref[...] = (xv * r * g_ref[...]).astype(BF16)

    return pl.pallas_call(
        body, name=name, grid=(S // tr,),
        in_specs=[pl.BlockSpec((tr, D), lambda i: (i, 0)), pl.BlockSpec((1, D), lambda i: (0, 0))],
        out_specs=pl.BlockSpec((tr, D), lambda i: (i, 0)),
        out_shape=jax.ShapeDtypeStruct((S, D), BF16),
        compiler_params=pltpu.CompilerParams(dimension_semantics=("parallel",)),
    )(x, g)


def _rms_bwd(name, dn, x, g, dres):
    S, D = x.shape
    tr = _row_tile(S, 256)

    def body(dn_ref, x_ref, g_ref, dres_ref, dx_ref, dxb_ref, dg_ref):
        i = pl.program_id(0)
        xv = x_ref[...]
        dnv = dn_ref[...]
        r = lax.rsqrt(jnp.mean(xv * xv, axis=-1, keepdims=True) + RMS_EPS)
        u = dnv * g_ref[...]
        dot = jnp.mean(u * xv, axis=-1, keepdims=True)
        dx = dres_ref[...] + r * u - xv * (r * r * r * dot)
        dx_ref[...] = dx
        dxb_ref[...] = dx.astype(BF16)

        @pl.when(i == 0)
        def _():
            dg_ref[...] = jnp.zeros_like(dg_ref)

        dg_ref[...] += jnp.sum(dnv * xv * r, axis=0, keepdims=True)

    row = pl.BlockSpec((tr, D), lambda i: (i, 0))
    vec = pl.BlockSpec((1, D), lambda i: (0, 0))
    return pl.pallas_call(
        body, name=name, grid=(S // tr,),
        in_specs=[row, row, vec, row], out_specs=[row, row, vec],
        out_shape=[jax.ShapeDtypeStruct((S, D), F32), jax.ShapeDtypeStruct((S, D), BF16),
                   jax.ShapeDtypeStruct((1, D), F32)],
        compiler_params=pltpu.CompilerParams(dimension_semantics=("arbitrary",)),
    )(dn, x, g, dres)


def _loss_head(name, x, g, target):
    S, D = x.shape
    tr = _row_tile(S, 256)

    def body(x_ref, g_ref, t_ref, dx_ref, dxb_ref, dg_ref, loss_ref):
        i = pl.program_id(0)
        xv = x_ref[...]
        gv = g_ref[...]
        r = lax.rsqrt(jnp.mean(xv * xv, axis=-1, keepdims=True) + RMS_EPS)
        diff = xv * r * gv - t_ref[...]
        part = 0.5 * jnp.sum(jnp.mean(diff * diff, axis=-1, keepdims=True), axis=0, keepdims=True)
        dy = diff * (1.0 / D)
        u = dy * gv
        dot = jnp.mean(u * xv, axis=-1, keepdims=True)
        dx = r * u - xv * (r * r * r * dot)
        dx_ref[...] = dx
        dxb_ref[...] = dx.astype(BF16)

        @pl.when(i == 0)
        def _():
            dg_ref[...] = jnp.zeros_like(dg_ref)
            loss_ref[...] = jnp.zeros_like(loss_ref)

        dg_ref[...] += jnp.sum(dy * xv * r, axis=0, keepdims=True)
        loss_ref[...] += jnp.broadcast_to(part, loss_ref.shape)

    row = pl.BlockSpec((tr, D), lambda i: (i, 0))
    vec = pl.BlockSpec((1, D), lambda i: (0, 0))
    lvec = pl.BlockSpec((1, LANES), lambda i: (0, 0))
    return pl.pallas_call(
        body, name=name, grid=(S // tr,),
        in_specs=[row, vec, row], out_specs=[row, row, vec, lvec],
        out_shape=[jax.ShapeDtypeStruct((S, D), F32), jax.ShapeDtypeStruct((S, D), BF16),
                   jax.ShapeDtypeStruct((1, D), F32), jax.ShapeDtypeStruct((1, LANES), F32)],
        compiler_params=pltpu.CompilerParams(dimension_semantics=("arbitrary",)),
    )(x, g, target)


def _swiglu_fwd(name, gu, fs):
    S, two_f = gu.shape
    nslab = two_f // (2 * fs)
    tr = _row_tile(S, 256)

    def body(gu_ref, h_ref):
        gv = gu_ref[:, :fs]
        uv = gu_ref[:, fs:]
        h_ref[...] = (gv * jax.nn.sigmoid(gv) * uv).astype(BF16)

    return pl.pallas_call(
        body, name=name, grid=(S // tr, nslab),
        in_specs=[pl.BlockSpec((tr, 2 * fs), lambda i, k: (i, k))],
        out_specs=pl.BlockSpec((tr, fs), lambda i, k: (i, k)),
        out_shape=jax.ShapeDtypeStruct((S, nslab * fs), BF16),
        compiler_params=pltpu.CompilerParams(dimension_semantics=("parallel", "parallel")),
    )(gu)


def _swiglu_bwd(name, dh, gu, fs):
    S, two_f = gu.shape
    nslab = two_f // (2 * fs)
    tr = _row_tile(S, 256)

    def body(dh_ref, gu_ref, o_ref):
        gv = gu_ref[:, :fs]
        uv = gu_ref[:, fs:]
        dhv = dh_ref[...]
        sg = jax.nn.sigmoid(gv)
        silu = gv * sg
        o_ref[:, :fs] = (dhv * uv * (sg + silu * (1.0 - sg))).astype(BF16)
        o_ref[:, fs:] = (dhv * silu).astype(BF16)

    return pl.pallas_call(
        body, name=name, grid=(S // tr, nslab),
        in_specs=[pl.BlockSpec((tr, fs), lambda i, k: (i, k)), pl.BlockSpec((tr, 2 * fs), lambda i, k: (i, k))],
        out_specs=pl.BlockSpec((tr, 2 * fs), lambda i, k: (i, k)),
        out_shape=jax.ShapeDtypeStruct((S, two_f), BF16),
        compiler_params=pltpu.CompilerParams(dimension_semantics=("parallel", "parallel")),
    )(dh, gu)


def _ffn_up(name, n, wgu, fs, tm_target=512):
    S, D = n.shape
    nslab = wgu.shape[1] // (2 * fs)
    tm = _tile(S, tm_target)

    def body(n_ref, w_ref, gu_ref, h_ref):
        y = jnp.dot(n_ref[...], w_ref[...], preferred_element_type=F32)
        gu_ref[...] = y
        gv = y[:, :fs]
        h_ref[...] = (gv * jax.nn.sigmoid(gv) * y[:, fs:]).astype(BF16)

    blk = _nbytes((tm, D), BF16) + _nbytes((D, 2 * fs), BF16) + _nbytes((tm, 2 * fs), F32) + _nbytes((tm, fs), BF16)
    return pl.pallas_call(
        body, name=name, grid=(nslab, S // tm),
        in_specs=[pl.BlockSpec((tm, D), lambda k, i: (i, 0)), pl.BlockSpec((D, 2 * fs), lambda k, i: (0, k))],
        out_specs=[pl.BlockSpec((tm, 2 * fs), lambda k, i: (i, k)), pl.BlockSpec((tm, fs), lambda k, i: (i, k))],
        out_shape=[jax.ShapeDtypeStruct((S, nslab * 2 * fs), F32), jax.ShapeDtypeStruct((S, nslab * fs), BF16)],
        compiler_params=pltpu.CompilerParams(dimension_semantics=("parallel", "parallel"),
                                             vmem_limit_bytes=_vmem_limit(blk, _nbytes((tm, 2 * fs), F32))),
    )(n, wgu)


def _ffn_dact(name, dyb, wd, gu, fs, alpha, after=None, tm_target=512):
    S, D = dyb.shape
    nslab = wd.shape[0] // fs
    tm = _tile(S, tm_target)

    def body(*refs):
        d_ref, w_ref, gu_ref = refs[:3]
        o_ref = refs[-1]
        dhv = _dot_nt(d_ref[...], w_ref[...]) * alpha
        gv = gu_ref[:, :fs]
        uv = gu_ref[:, fs:]
        sg = jax.nn.sigmoid(gv)
        silu = gv * sg
        o_ref[:, :fs] = (dhv * uv * (sg + silu * (1.0 - sg))).astype(BF16)
        o_ref[:, fs:] = (dhv * silu).astype(BF16)

    in_specs = [pl.BlockSpec((tm, D), lambda k, i: (i, 0)), pl.BlockSpec((fs, D), lambda k, i: (k, 0)),
                pl.BlockSpec((tm, 2 * fs), lambda k, i: (i, k))] + ([_ANY] if after is not None else [])
    args = (dyb, wd, gu) + ((after,) if after is not None else ())
    blk = _nbytes((tm, D), BF16) + _nbytes((fs, D), BF16) + _nbytes((tm, 2 * fs), F32) + _nbytes((tm, 2 * fs), BF16)
    return pl.pallas_call(
        body, name=name, grid=(nslab, S // tm), in_specs=in_specs,
        out_specs=pl.BlockSpec((tm, 2 * fs), lambda k, i: (i, k)),
        out_shape=jax.ShapeDtypeStruct((S, nslab * 2 * fs), BF16),
        compiler_params=pltpu.CompilerParams(dimension_semantics=("parallel", "parallel"),
                                             vmem_limit_bytes=_vmem_limit(blk, 2 * _nbytes((tm, fs), F32))),
    )(*args)


def _tri_rows(r0, nrows, ncols, lower):
    row = lax.broadcasted_iota(jnp.int32, (nrows, ncols), 0) + r0
    col = lax.broadcasted_iota(jnp.int32, (nrows, ncols), 1)
    return jnp.where((col <= row) if lower else (col >= row), 1.0, 0.0).astype(F32)


def _gate_fwd(name, hf, b):
    S = hf.shape[0]
    tb = _row_tile(S, 256)

    def body(hf_ref, b_ref, cf_ref, cft_ref, lf_ref):
        zz = hf_ref[...] + b_ref[...]
        lf_ref[...] = jnp.minimum(zz, 0.0) - jnp.log1p(jnp.exp(-jnp.abs(zz)))

        def blk(i, c):
            r0 = pl.multiple_of(i * tb, tb)
            tri = _tri_rows(r0, tb, S, True)
            cf_ref[pl.ds(r0, tb), :] = jnp.dot(tri, lf_ref[...], precision=lax.Precision.HIGHEST,
                                               preferred_element_type=F32)
            return c

        lax.fori_loop(0, S // tb, blk, 0)
        cft_ref[...] = cf_ref[...].T

    full = pl.BlockSpec((S, LANES), lambda: (0, 0))
    return pl.pallas_call(
        body, name=name, in_specs=[full, pl.BlockSpec((1, LANES), lambda: (0, 0))],
        out_specs=[full, pl.BlockSpec((LANES, S), lambda: (0, 0))],
        out_shape=[jax.ShapeDtypeStruct((S, LANES), F32), jax.ShapeDtypeStruct((LANES, S), F32)],
        scratch_shapes=[pltpu.VMEM((S, LANES), F32)],
    )(hf, b)


def _gate_bwd(name, dcft, drow, hf, b):
    S = hf.shape[0]
    tb = _row_tile(S, 256)

    def body(dcft_ref, drow_ref, hf_ref, b_ref, dhf_ref, db_ref, dcf_ref, dlf_ref):
        dcf_ref[...] = dcft_ref[...].T + drow_ref[...]

        def blk(i, c):
            r0 = pl.multiple_of(i * tb, tb)
            tri = _tri_rows(r0, tb, S, False)
            dlf_ref[pl.ds(r0, tb), :] = jnp.dot(tri, dcf_ref[...], precision=lax.Precision.HIGHEST,
                                                preferred_element_type=F32)
            return c

        lax.fori_loop(0, S // tb, blk, 0)
        zz = hf_ref[...] + b_ref[...]
        dhf = dlf_ref[...] * jax.nn.sigmoid(-zz)
        dhf_ref[...] = dhf.astype(BF16)
        db_ref[...] = jnp.sum(dhf, axis=0, keepdims=True)

    full = pl.BlockSpec((S, LANES), lambda: (0, 0))
    vec = pl.BlockSpec((1, LANES), lambda: (0, 0))
    return pl.pallas_call(
        body, name=name, in_specs=[pl.BlockSpec((LANES, S), lambda: (0, 0)), full, full, vec],
        out_specs=[full, vec],
        out_shape=[jax.ShapeDtypeStruct((S, LANES), BF16), jax.ShapeDtypeStruct((1, LANES), F32)],
        scratch_shapes=[pltpu.VMEM((S, LANES), F32), pltpu.VMEM((S, LANES), F32)],
    )(dcft, drow, hf, b)


def _rope_tables(S):
    half = ROPE_DIMS // 2
    freqs = ROPE_THETA ** (-jnp.arange(half, dtype=F32) / half)
    ang = jnp.arange(S, dtype=F32)[:, None] * freqs[None, :]
    cos, sin = jnp.cos(ang), jnp.sin(ang)
    pad = HEAD_DIM - ROPE_DIMS
    c = jnp.concatenate([cos, cos, jnp.ones((S, pad), F32)], axis=1)
    s = jnp.concatenate([-sin, sin, jnp.zeros((S, pad), F32)], axis=1)
    return c, s


def _rope_swap(x):
    half = ROPE_DIMS // 2
    lane = lax.broadcasted_iota(jnp.int32, x.shape, 1)
    upper = jnp.where(lane < ROPE_DIMS, pltpu.roll(x, half, 1), 0.0)
    return jnp.where(lane < half, pltpu.roll(x, HEAD_DIM - half, 1), upper)


def _rope(x, c, s):
    return x * c + _rope_swap(x) * s


def _rope_t(dy, c, s):
    return dy * c + _rope_swap(dy * s)


def _split_dot(x, t):
    hi = x.astype(BF16)
    lo = (x - hi.astype(F32)).astype(BF16)
    return (jnp.dot(hi, t, preferred_element_type=F32) + jnp.dot(lo, t, preferred_element_type=F32))


_NT = (((1,), (1,)), ((), ()))
_TN = (((0,), (0,)), ((), ()))


def _dot_nt(a, b):
    return lax.dot_general(a, b, _NT, preferred_element_type=F32)


def _dot_tn(a, b):
    return lax.dot_general(a, b, _TN, preferred_element_type=F32)


def _blk(i):
    return pl.ds(pl.multiple_of(i * BLK, BLK), BLK)


def _delta(i, j):
    row = lax.broadcasted_iota(jnp.int32, (BLK, BLK), 0)
    col = lax.broadcasted_iota(jnp.int32, (BLK, BLK), 1)
    return (row - col) + (i - j) * BLK


def _dilated_mult(delta):
    c = jnp.zeros(delta.shape, F32)
    for window, dil in DILATED_PATTERNS:
        ok = (delta >= 0) & (delta <= window) & ((delta & (dil - 1)) == 0)
        c = c + jnp.where(ok, 1.0, 0.0)
    return c


def _sb_terms(z, mask, t_ex, run):
    t = jnp.log1p(jnp.exp(-jnp.abs(z)))
    lsig = jnp.minimum(z, 0.0) - t
    m = jnp.where(mask, -(jnp.maximum(z, 0.0) + t), 0.0)
    after = _split_dot(m, t_ex)
    a = jnp.where(mask, jnp.exp(lsig + after + run), 0.0)
    return a, m, lsig


def _attn_fwd(name, hq, layer_kind, n_heads, n_sb, cf=None, cft=None, rope_c=None, rope_s=None):
    S = hq.shape[0]
    D = n_heads * HEAD_DIM
    nq = S // BLK
    scale = HEAD_DIM ** -0.5
    even = layer_kind == "even"

    def body(*refs):
        if even:
            q_ref, k_ref, v_ref, cf_ref, cft_ref, o_ref, ob_ref, lse_ref, qs, ks, vs = refs
        else:
            q_ref, k_ref, v_ref, c_ref, s_ref, o_ref, ob_ref, lse_ref, qs, ks, vs = refs
        h = pl.program_id(0)
        if even:
            qs[...] = q_ref[...].astype(BF16)
            ks[...] = k_ref[...].astype(BF16)
        else:
            qs[...] = _rope(q_ref[...], c_ref[...], s_ref[...]).astype(BF16)
            ks[...] = _rope(k_ref[...], c_ref[...], s_ref[...]).astype(BF16)
        vs[...] = v_ref[...].astype(BF16)

        def softmax_head(hh):
            def qblock(i, carry):
                qi = qs[_blk(i), :]
                if even:
                    lane = lax.broadcasted_iota(jnp.int32, (BLK, LANES), 1)
                    cfq = jnp.sum(jnp.where(lane == hh, cf_ref[_blk(i), :], 0.0), axis=1, keepdims=True)

                def kblock(j, c):
                    m_run, l_run, acc = c
                    z = _dot_nt(qi, ks[_blk(j), :]) * scale
                    delta = _delta(i, j)
                    if even:
                        z = z + cfq - cft_ref[hh, :, _blk(j)]
                        ok = delta >= 0
                    else:
                        mult = _dilated_mult(delta)
                        ok = mult > 0.0
                    z = jnp.where(ok, z, NEG_INF)
                    m_new = jnp.maximum(m_run, jnp.max(z, axis=1, keepdims=True))
                    p = jnp.exp(z - m_new)
                    if not even:
                        p = p * mult
                    alpha = jnp.exp(m_run - m_new)
                    l_new = alpha * l_run + jnp.sum(p, axis=1, keepdims=True)
                    acc = alpha * acc + jnp.dot(p.astype(BF16), vs[_blk(j), :], preferred_element_type=F32)
                    return m_new, l_new, acc

                init = (jnp.full((BLK, 1), NEG_INF, F32), jnp.zeros((BLK, 1), F32), jnp.zeros((BLK, HEAD_DIM), F32))
                m_run, l_run, acc = lax.fori_loop(0, i + 1, kblock, init)
                o = acc / l_run
                o_ref[_blk(i), :] = o
                ob_ref[_blk(i), :] = o.astype(BF16)
                lse_ref[_blk(i), :] = jnp.broadcast_to(m_run + jnp.log(l_run), (BLK, HEAD_DIM))
                return carry

            lax.fori_loop(0, nq, qblock, 0)

        def sb_head():
            row = lax.broadcasted_iota(jnp.int32, (BLK, BLK), 0)
            col = lax.broadcasted_iota(jnp.int32, (BLK, BLK), 1)
            t_ex = jnp.where(row > col, 1.0, 0.0).astype(BF16)

            def qblock(i, carry):
                qi = qs[_blk(i), :]

                def kblock(jj, c):
                    run, acc = c
                    j = i - jj
                    z = _dot_nt(qi, ks[_blk(j), :]) * scale
                    a, m, _ = _sb_terms(z, _delta(i, j) > 0, t_ex, run)
                    acc = acc + jnp.dot(a.astype(BF16), vs[_blk(j), :], preferred_element_type=F32)
                    return run + jnp.sum(m, axis=1, keepdims=True), acc

                init = (jnp.zeros((BLK, 1), F32), jnp.zeros((BLK, HEAD_DIM), F32))
                _, acc = lax.fori_loop(0, i + 1, kblock, init)
                o_ref[_blk(i), :] = acc
                ob_ref[_blk(i), :] = acc.astype(BF16)
                lse_ref[_blk(i), :] = jnp.zeros((BLK, HEAD_DIM), F32)
                return carry

            lax.fori_loop(0, nq, qblock, 0)

        if even:
            @pl.when(h < n_sb)
            def _():
                sb_head()

            @pl.when(h >= n_sb)
            def _():
                softmax_head(h - n_sb)
        else:
            softmax_head(h)

    head = lambda off: pl.BlockSpec((S, HEAD_DIM), lambda h, off=off: (0, off + h))
    full = pl.BlockSpec((S, LANES), lambda h: (0, 0))
    if even:
        extra_specs = [full, pl.BlockSpec(cft.shape, lambda h: (0, 0, 0))]
        extra = (cf, cft)
    else:
        extra_specs = [full, full]
        extra = (rope_c, rope_s)
    blk_bytes = 8 * _nbytes((S, HEAD_DIM), F32)
    return pl.pallas_call(
        body, name=name, grid=(n_heads,),
        in_specs=[head(0), head(n_heads), head(2 * n_heads)] + extra_specs,
        out_specs=[head(0), head(0), head(0)],
        out_shape=[jax.ShapeDtypeStruct((S, D), F32), jax.ShapeDtypeStruct((S, D), BF16),
                   jax.ShapeDtypeStruct((S, D), F32)],
        scratch_shapes=[pltpu.VMEM((S, HEAD_DIM), BF16)] * 3,
        compiler_params=pltpu.CompilerParams(dimension_semantics=("arbitrary",),
                                             vmem_limit_bytes=_vmem_limit(blk_bytes, 3 * _nbytes((S, HEAD_DIM), BF16))),
    )(hq, hq, hq, *extra)


def _attn_bwd(name, hq, do, o, lse, layer_kind, n_heads, n_sb, cf=None, cft=None, rope_c=None, rope_s=None):
    S = hq.shape[0]
    D = n_heads * HEAD_DIM
    nq = S // BLK
    scale = HEAD_DIM ** -0.5
    even = layer_kind == "even"

    def body(*refs):
        if even:
            (q_ref, k_ref, v_ref, do_ref, o_ref, lse_ref, cf_ref, cft_ref,
             dh_ref, dcft_ref, drow_ref, qs, ks, vs, dos, dq_acc, dk_acc, dv_acc) = refs
        else:
            (q_ref, k_ref, v_ref, do_ref, o_ref, lse_ref, c_ref, s_ref,
             dh_ref, qs, ks, vs, dos, dq_acc, dk_acc, dv_acc) = refs
        h = pl.program_id(0)
        if even:
            qs[...] = q_ref[...].astype(BF16)
            ks[...] = k_ref[...].astype(BF16)

            @pl.when(h == 0)
            def _():
                dcft_ref[...] = jnp.zeros_like(dcft_ref)
                drow_ref[...] = jnp.zeros_like(drow_ref)
        else:
            qs[...] = _rope(q_ref[...], c_ref[...], s_ref[...]).astype(BF16)
            ks[...] = _rope(k_ref[...], c_ref[...], s_ref[...]).astype(BF16)
        vs[...] = v_ref[...].astype(BF16)
        dos[...] = do_ref[...].astype(BF16)
        dk_acc[...] = jnp.zeros_like(dk_acc)
        dv_acc[...] = jnp.zeros_like(dv_acc)

        def softmax_head(hh):
            def qblock(i, carry):
                qi = qs[_blk(i), :]
                doi = dos[_blk(i), :]
                dvec = jnp.sum(do_ref[_blk(i), :] * o_ref[_blk(i), :], axis=1, keepdims=True)
                lse_i = jnp.max(lse_ref[_blk(i), :], axis=1, keepdims=True)
                if even:
                    lane = lax.broadcasted_iota(jnp.int32, (BLK, LANES), 1)
                    cfq = jnp.sum(jnp.where(lane == hh, cf_ref[_blk(i), :], 0.0), axis=1, keepdims=True)

                def kblock(j, c):
                    dq, ds_rows = c
                    kj = ks[_blk(j), :]
                    z = _dot_nt(qi, kj) * scale
                    delta = _delta(i, j)
                    if even:
                        z = z + cfq - cft_ref[hh, :, _blk(j)]
                        ok = delta >= 0
                    else:
                        mult = _dilated_mult(delta)
                        ok = mult > 0.0
                    p = jnp.exp(jnp.where(ok, z, NEG_INF) - lse_i)
                    if not even:
                        p = p * mult
                    dp = _dot_nt(doi, vs[_blk(j), :])
                    ds = p * (dp - dvec)
                    dsb = (ds * scale).astype(BF16)
                    dk_acc[_blk(j), :] += _dot_tn(dsb, qi)
                    dv_acc[_blk(j), :] += _dot_tn(p.astype(BF16), doi)
                    if even:
                        dcft_ref[hh, :, _blk(j)] += -jnp.sum(ds, axis=0, keepdims=True)
                    return (dq + jnp.dot(dsb, kj, preferred_element_type=F32),
                            ds_rows + jnp.sum(ds, axis=1, keepdims=True))

                dq, ds_rows = lax.fori_loop(0, i + 1, kblock,
                                            (jnp.zeros((BLK, HEAD_DIM), F32), jnp.zeros((BLK, 1), F32)))
                dq_acc[_blk(i), :] = dq
                if even:
                    drow_ref[_blk(i), :] += jnp.where(lane == hh, ds_rows, 0.0)
                return carry

            lax.fori_loop(0, nq, qblock, 0)

        def sb_head():
            row = lax.broadcasted_iota(jnp.int32, (BLK, BLK), 0)
            col = lax.broadcasted_iota(jnp.int32, (BLK, BLK), 1)
            t_ex = jnp.where(row > col, 1.0, 0.0).astype(BF16)
            t_in = jnp.where(row >= col, 1.0, 0.0).astype(BF16)

            def qblock(i, carry):
                qi = qs[_blk(i), :]
                doi = dos[_blk(i), :]

                def e_total(jj, c):
                    run, tot = c
                    j = i - jj
                    z = _dot_nt(qi, ks[_blk(j), :]) * scale
                    a, m, _ = _sb_terms(z, _delta(i, j) > 0, t_ex, run)
                    e = _dot_nt(doi, vs[_blk(j), :]) * a
                    return run + jnp.sum(m, axis=1, keepdims=True), tot + jnp.sum(e, axis=1, keepdims=True)

                zero = jnp.zeros((BLK, 1), F32)
                _, e_tot = lax.fori_loop(0, i + 1, e_total, (zero, zero))

                def kblock(jj, c):
                    run, e_run, dq = c
                    j = i - jj
                    kj = ks[_blk(j), :]
                    z = _dot_nt(qi, kj) * scale
                    mask = _delta(i, j) > 0
                    a, m, lsig = _sb_terms(z, mask, t_ex, run)
                    sig = jnp.exp(lsig)
                    e = _dot_nt(doi, vs[_blk(j), :]) * a
                    e_before = e_tot - (_split_dot(e, t_in) + e_run)
                    dz = jnp.where(mask, e * (1.0 - sig) - sig * e_before, 0.0)
                    dzb = (dz * scale).astype(BF16)
                    dk_acc[_blk(j), :] += _dot_tn(dzb, qi)
                    dv_acc[_blk(j), :] += _dot_tn(a.astype(BF16), doi)
                    return (run + jnp.sum(m, axis=1, keepdims=True), e_run + jnp.sum(e, axis=1, keepdims=True),
                            dq + jnp.dot(dzb, kj, preferred_element_type=F32))

                _, _, dq = lax.fori_loop(0, i + 1, kblock, (zero, zero, jnp.zeros((BLK, HEAD_DIM), F32)))
                dq_acc[_blk(i), :] = dq
                return carry

            lax.fori_loop(0, nq, qblock, 0)

        if even:
            @pl.when(h < n_sb)
            def _():
                sb_head()

            @pl.when(h >= n_sb)
            def _():
                softmax_head(h - n_sb)

            dh_ref[0] = dq_acc[...].astype(BF16)
            dh_ref[1] = dk_acc[...].astype(BF16)
        else:
            softmax_head(h)
            dh_ref[0] = _rope_t(dq_acc[...], c_ref[...], s_ref[...]).astype(BF16)
            dh_ref[1] = _rope_t(dk_acc[...], c_ref[...], s_ref[...]).astype(BF16)
        dh_ref[2] = dv_acc[...].astype(BF16)

    head = lambda off: pl.BlockSpec((S, HEAD_DIM), lambda h, off=off: (0, off + h))
    full = pl.BlockSpec((S, LANES), lambda h: (0, 0))
    tfull = pl.BlockSpec((n_heads - n_sb, 1, S), lambda h: (0, 0, 0))
    dh_spec = pl.BlockSpec((3, S, HEAD_DIM), lambda h: (0, 0, h))
    dh_shape = jax.ShapeDtypeStruct((3, S, D), BF16)
    if even:
        extra_specs, extra = [full, tfull], (cf, cft)
        out_specs = [dh_spec, tfull, full]
        out_shape = [dh_shape, jax.ShapeDtypeStruct((n_heads - n_sb, 1, S), F32),
                     jax.ShapeDtypeStruct((S, LANES), F32)]
    else:
        extra_specs, extra = [full, full], (rope_c, rope_s)
        out_specs = [dh_spec]
        out_shape = [dh_shape]
    blk_bytes = 10 * _nbytes((S, HEAD_DIM), F32)
    scratch_bytes = 4 * _nbytes((S, HEAD_DIM), BF16) + 3 * _nbytes((S, HEAD_DIM), F32)
    return pl.pallas_call(
        body, name=name, grid=(n_heads,),
        in_specs=[head(0), head(n_heads), head(2 * n_heads), head(0), head(0), head(0)] + extra_specs,
        out_specs=out_specs, out_shape=out_shape,
        scratch_shapes=[pltpu.VMEM((S, HEAD_DIM), BF16)] * 4 + [pltpu.VMEM((S, HEAD_DIM), F32)] * 3,
        compiler_params=pltpu.CompilerParams(dimension_semantics=("arbitrary",),
                                             vmem_limit_bytes=_vmem_limit(blk_bytes, scratch_bytes)),
    )(hq, hq, hq, do, o, lse, *extra)


def _query_block(S):
    return min(512, S)


def _offsets(d, bq):
    row = jnp.arange(bq, dtype=jnp.int32)[:, None]
    col = jnp.arange(BLK, dtype=jnp.int32)[None, :]
    return d * BLK + row - col


def _causal_tables(bq, strict):
    r = bq // BLK
    tabs = []
    for d in range(-(r - 1), 1):
        delta = _offsets(d, bq)
        tabs.append(jnp.where((delta > 0) if strict else (delta >= 0), 1.0, 0.0))
    tabs.append(jnp.ones((bq, BLK), F32))
    return jnp.stack(tabs).astype(F32)


def _dilated_tables(bq):
    r = bq // BLK
    limit = sorted(w for w, _ in DILATED_PATTERNS)[-2]
    assert all(BLK % dil == 0 for _, dil in DILATED_PATTERNS)
    d_far = -(-(limit + BLK) // BLK)
    tabs = []
    for d in range(-(r - 1), d_far + 1):
        mult = _dilated_mult(_offsets(d, bq))
        tabs.append(jnp.where(mult > 0, jnp.log(jnp.maximum(mult, 1.0)), NEG_INF))
    return jnp.stack(tabs).astype(F32)


def _qblk(i, bq):
    return pl.ds(pl.multiple_of(i * bq, bq), bq)


def _sb_block(z, valid, t_ex, run):
    t = jnp.log1p(jnp.exp(-jnp.abs(z)))
    lsig = jnp.minimum(z, 0.0) - t
    m = -(jnp.maximum(z, 0.0) + t) * valid
    after = _split_dot(m, t_ex)
    a = jnp.exp(lsig + after + run) * valid
    return a, m, lsig


def _attn_fwd_wide(name, hq, layer_kind, n_heads, n_sb, cf=None, cft=None, rope_c=None, rope_s=None):
    S = hq.shape[0]
    D = n_heads * HEAD_DIM
    bq = _query_block(S)
    r = bq // BLK
    nq = S // bq
    scale = HEAD_DIM ** -0.5
    even = layer_kind == "even"
    if even:
        tabs = (jnp.where(_causal_tables(bq, False) > 0, 0.0, NEG_INF), _causal_tables(bq, True))
    else:
        tabs = (_dilated_tables(bq),)
    n_tab = tabs[0].shape[0]

    def body(*refs):
        if even:
            q_ref, k_ref, v_ref, cf_ref, cft_ref, bias_ref, valid_ref, o_ref, ob_ref, lse_ref, qs, ks, vs = refs
        else:
            q_ref, k_ref, v_ref, c_ref, s_ref, bias_ref, o_ref, ob_ref, lse_ref, qs, ks, vs = refs
        h = pl.program_id(0)
        if even:
            qs[...] = q_ref[...].astype(BF16)
            ks[...] = k_ref[...].astype(BF16)
        else:
            qs[...] = _rope(q_ref[...], c_ref[...], s_ref[...]).astype(BF16)
            ks[...] = _rope(k_ref[...], c_ref[...], s_ref[...]).astype(BF16)
        vs[...] = v_ref[...].astype(BF16)

        def softmax_head(hh):
            def qblock(i, carry):
                qi = qs[_qblk(i, bq), :]
                if even:
                    lane = lax.broadcasted_iota(jnp.int32, (bq, LANES), 1)
                    cfq = jnp.sum(jnp.where(lane == hh, cf_ref[_qblk(i, bq), :], 0.0), axis=1, keepdims=True)

                def kblock(j, c):
                    m_run, l_run, acc = c
                    z = _dot_nt(qi, ks[_blk(j), :]) * scale + bias_ref[jnp.minimum(r * i - j + (r - 1), n_tab - 1)]
                    if even:
                        z = z + (cfq - cft_ref[hh, :, _blk(j)])
                    m_new = jnp.maximum(m_run, jnp.max(z, axis=1, keepdims=True))
                    p = jnp.exp(z - m_new)
                    alpha = jnp.exp(m_run - m_new)
                    l_new = alpha * l_run + jnp.sum(p, axis=1, keepdims=True)
                    acc = alpha * acc + jnp.dot(p.astype(BF16), vs[_blk(j), :], preferred_element_type=F32)
                    return m_new, l_new, acc

                init = (jnp.full((bq, 1), NEG_INF, F32), jnp.zeros((bq, 1), F32), jnp.zeros((bq, HEAD_DIM), F32))
                m_run, l_run, acc = lax.fori_loop(0, r * (i + 1), kblock, init)
                o = acc / l_run
                o_ref[_qblk(i, bq), :] = o
                ob_ref[_qblk(i, bq), :] = o.astype(BF16)
                lse_ref[_qblk(i, bq), :] = jnp.broadcast_to(m_run + jnp.log(l_run), (bq, HEAD_DIM))
                return carry

            lax.fori_loop(0, nq, qblock, 0)

        def sb_head():
            row = lax.broadcasted_iota(jnp.int32, (BLK, BLK), 0)
            col = lax.broadcasted_iota(jnp.int32, (BLK, BLK), 1)
            t_ex = jnp.where(row > col, 1.0, 0.0).astype(BF16)

            def qblock(i, carry):
                qi = qs[_qblk(i, bq), :]

                def kblock(jj, c):
                    run, acc, rest = c
                    j = r * (i + 1) - 1 - jj
                    z = _dot_nt(qi, ks[_blk(j), :]) * scale
                    a, m, _ = _sb_block(z, valid_ref[jnp.minimum(r * i - j + (r - 1), r)], t_ex, run)
                    vj = vs[_blk(j), :]
                    hi = a.astype(BF16)
                    lo = (a - hi.astype(F32)).astype(BF16)
                    acc = acc + jnp.dot(hi, vj, preferred_element_type=F32)
                    rest = rest + jnp.dot(lo, vj, preferred_element_type=F32)
                    return run + jnp.sum(m, axis=1, keepdims=True), acc, rest

                zero = jnp.zeros((bq, HEAD_DIM), F32)
                _, acc, rest = lax.fori_loop(0, r * (i + 1), kblock, (jnp.zeros((bq, 1), F32), zero, zero))
                o_ref[_qblk(i, bq), :] = acc + rest
                ob_ref[_qblk(i, bq), :] = acc.astype(BF16)
                lse_ref[_qblk(i, bq), :] = jnp.zeros((bq, HEAD_DIM), F32)
                return carry

            lax.fori_loop(0, nq, qblock, 0)

        if even:
            @pl.when(h < n_sb)
            def _():
                sb_head()

            @pl.when(h >= n_sb)
            def _():
                softmax_head(h - n_sb)
        else:
            softmax_head(h)

    head = lambda off: pl.BlockSpec((S, HEAD_DIM), lambda h, off=off: (0, off + h))
    full = pl.BlockSpec((S, LANES), lambda h: (0, 0))
    tab_specs = [pl.BlockSpec(t.shape, lambda h: (0, 0, 0)) for t in tabs]
    if even:
        extra_specs = [full, pl.BlockSpec(cft.shape, lambda h: (0, 0, 0))] + tab_specs
        extra = (cf, cft) + tabs
    else:
        extra_specs = [full, full] + tab_specs
        extra = (rope_c, rope_s) + tabs
    blk_bytes = 8 * _nbytes((S, HEAD_DIM), F32) + sum(_nbytes(t.shape, F32) for t in tabs)
    return pl.pallas_call(
        body, name=name, grid=(n_heads,),
        in_specs=[head(0), head(n_heads), head(2 * n_heads)] + extra_specs,
        out_specs=[head(0), head(0), head(0)],
        out_shape=[jax.ShapeDtypeStruct((S, D), F32), jax.ShapeDtypeStruct((S, D), BF16),
                   jax.ShapeDtypeStruct((S, D), F32)],
        scratch_shapes=[pltpu.VMEM((S, HEAD_DIM), BF16)] * 3,
        compiler_params=pltpu.CompilerParams(dimension_semantics=("arbitrary",),
                                             vmem_limit_bytes=_vmem_limit(blk_bytes, 3 * _nbytes((S, HEAD_DIM), BF16))),
    )(hq, hq, hq, *extra)


def _attn_bwd_wide(name, hq, do, o, lse, layer_kind, n_heads, n_sb, cf=None, cft=None, rope_c=None, rope_s=None):
    S = hq.shape[0]
    D = n_heads * HEAD_DIM
    bq = _query_block(S)
    r = bq // BLK
    nq = S // bq
    scale = HEAD_DIM ** -0.5
    even = layer_kind == "even"
    if even:
        tabs = (jnp.where(_causal_tables(bq, False) > 0, 0.0, NEG_INF), _causal_tables(bq, True))
    else:
        tabs = (_dilated_tables(bq),)
    n_tab = tabs[0].shape[0]

    def body(*refs):
        if even:
            (q_ref, k_ref, v_ref, do_ref, o_ref, lse_ref, cf_ref, cft_ref, bias_ref, valid_ref,
             dh_ref, dcft_ref, drow_ref, qs, ks, vs, dos, dq_acc, dk_acc, dv_acc) = refs
        else:
            (q_ref, k_ref, v_ref, do_ref, o_ref, lse_ref, c_ref, s_ref, bias_ref,
             dh_ref, qs, ks, vs, dos, dq_acc, dk_acc, dv_acc) = refs
        h = pl.program_id(0)
        if even:
            qs[...] = q_ref[...].astype(BF16)
            ks[...] = k_ref[...].astype(BF16)

            @pl.when(h == 0)
            def _():
                dcft_ref[...] = jnp.zeros_like(dcft_ref)
                drow_ref[...] = jnp.zeros_like(drow_ref)
        else:
            qs[...] = _rope(q_ref[...], c_ref[...], s_ref[...]).astype(BF16)
            ks[...] = _rope(k_ref[...], c_ref[...], s_ref[...]).astype(BF16)
        vs[...] = v_ref[...].astype(BF16)
        dos[...] = do_ref[...].astype(BF16)
        dk_acc[...] = jnp.zeros_like(dk_acc)
        dv_acc[...] = jnp.zeros_like(dv_acc)

        def softmax_head(hh):
            def qblock(i, carry):
                qi = qs[_qblk(i, bq), :]
                doi = dos[_qblk(i, bq), :]
                dvec = jnp.sum(do_ref[_qblk(i, bq), :] * o_ref[_qblk(i, bq), :], axis=1, keepdims=True)
                lse_i = jnp.max(lse_ref[_qblk(i, bq), :], axis=1, keepdims=True)
                if even:
                    lane = lax.broadcasted_iota(jnp.int32, (bq, LANES), 1)
                    cfq = jnp.sum(jnp.where(lane == hh, cf_ref[_qblk(i, bq), :], 0.0), axis=1, keepdims=True)

                def kblock(j, c):
                    dq, ds_rows = c
                    kj = ks[_blk(j), :]
                    z = _dot_nt(qi, kj) * scale + bias_ref[jnp.minimum(r * i - j + (r - 1), n_tab - 1)]
                    if even:
                        z = z + (cfq - cft_ref[hh, :, _blk(j)])
                    p = jnp.exp(z - lse_i)
                    dp = _dot_nt(doi, vs[_blk(j), :])
                    ds = p * (dp - dvec)
                    dsb = (ds * scale).astype(BF16)
                    dk_acc[_blk(j), :] += _dot_tn(dsb, qi)
                    dv_acc[_blk(j), :] += _dot_tn(p.astype(BF16), doi)
                    if even:
                        dcft_ref[hh, :, _blk(j)] += -jnp.sum(ds, axis=0, keepdims=True)
                    return (dq + jnp.dot(dsb, kj, preferred_element_type=F32),
                            ds_rows + jnp.sum(ds, axis=1, keepdims=True))

                dq, ds_rows = lax.fori_loop(0, r * (i + 1), kblock,
                                            (jnp.zeros((bq, HEAD_DIM), F32), jnp.zeros((bq, 1), F32)))
                dq_acc[_qblk(i, bq), :] = dq
                if even:
                    drow_ref[_qblk(i, bq), :] += jnp.where(lane == hh, ds_rows, 0.0)
                return carry

            lax.fori_loop(0, nq, qblock, 0)

        def sb_head():
            row = lax.broadcasted_iota(jnp.int32, (BLK, BLK), 0)
            col = lax.broadcasted_iota(jnp.int32, (BLK, BLK), 1)
            t_ex = jnp.where(row > col, 1.0, 0.0).astype(BF16)
            t_in = jnp.where(row >= col, 1.0, 0.0).astype(BF16)

            def qblock(i, carry):
                qi = qs[_qblk(i, bq), :]
                doi = dos[_qblk(i, bq), :]
                nkb = r * (i + 1)
                e_tot = jnp.sum(doi.astype(F32) * o_ref[_qblk(i, bq), :], axis=1, keepdims=True)
                zero = jnp.zeros((bq, 1), F32)

                def kblock(jj, c):
                    run, e_run, dq = c
                    j = nkb - 1 - jj
                    kj = ks[_blk(j), :]
                    z = _dot_nt(qi, kj) * scale
                    valid = valid_ref[jnp.minimum(r * i - j + (r - 1), r)]
                    a, m, lsig = _sb_block(z, valid, t_ex, run)
                    sig = jnp.exp(lsig)
                    e = _dot_nt(doi, vs[_blk(j), :]) * a
                    e_before = e_tot - (_split_dot(e, t_in) + e_run)
                    dz = (e * (1.0 - sig) - sig * e_before) * valid
                    dzb = (dz * scale).astype(BF16)
                    dk_acc[_blk(j), :] += _dot_tn(dzb, qi)
                    dv_acc[_blk(j), :] += _dot_tn(a.astype(BF16), doi)
                    return (run + jnp.sum(m, axis=1, keepdims=True), e_run + jnp.sum(e, axis=1, keepdims=True),
                            dq + jnp.dot(dzb, kj, preferred_element_type=F32))

                _, _, dq = lax.fori_loop(0, nkb, kblock, (zero, zero, jnp.zeros((bq, HEAD_DIM), F32)))
                dq_acc[_qblk(i, bq), :] = dq
                return carry

            lax.fori_loop(0, nq, qblock, 0)

        if even:
            @pl.when(h < n_sb)
            def _():
                sb_head()

            @pl.when(h >= n_sb)
            def _():
                softmax_head(h - n_sb)

            dh_ref[0] = dq_acc[...].astype(BF16)
            dh_ref[1] = dk_acc[...].astype(BF16)
        else:
            softmax_head(h)
            dh_ref[0] = _rope_t(dq_acc[...], c_ref[...], s_ref[...]).astype(BF16)
            dh_ref[1] = _rope_t(dk_acc[...], c_ref[...], s_ref[...]).astype(BF16)
        dh_ref[2] = dv_acc[...].astype(BF16)

    head = lambda off: pl.BlockSpec((S, HEAD_DIM), lambda h, off=off: (0, off + h))
    full = pl.BlockSpec((S, LANES), lambda h: (0, 0))
    tfull = pl.BlockSpec((n_heads - n_sb, 1, S), lambda h: (0, 0, 0))
    tab_specs = [pl.BlockSpec(t.shape, lambda h: (0, 0, 0)) for t in tabs]
    dh_spec = pl.BlockSpec((3, S, HEAD_DIM), lambda h: (0, 0, h))
    dh_shape = jax.ShapeDtypeStruct((3, S, D), BF16)
    if even:
        extra_specs, extra = [full, tfull] + tab_specs, (cf, cft) + tabs
        out_specs = [dh_spec, tfull, full]
        out_shape = [dh_shape, jax.ShapeDtypeStruct((n_heads - n_sb, 1, S), F32),
                     jax.ShapeDtypeStruct((S, LANES), F32)]
    else:
        extra_specs, extra = [full, full] + tab_specs, (rope_c, rope_s) + tabs
        out_specs = [dh_spec]
        out_shape = [dh_shape]
    blk_bytes = 10 * _nbytes((S, HEAD_DIM), F32) + sum(_nbytes(t.shape, F32) for t in tabs)
    scratch_bytes = 4 * _nbytes((S, HEAD_DIM), BF16) + 3 * _nbytes((S, HEAD_DIM), F32)
    return pl.pallas_call(
        body, name=name, grid=(n_heads,),
        in_specs=[head(0), head(n_heads), head(2 * n_heads), head(0), head(0), head(0)] + extra_specs,
        out_specs=out_specs, out_shape=out_shape,
        scratch_shapes=[pltpu.VMEM((S, HEAD_DIM), BF16)] * 4 + [pltpu.VMEM((S, HEAD_DIM), F32)] * 3,
        compiler_params=pltpu.CompilerParams(dimension_semantics=("arbitrary",),
                                             vmem_limit_bytes=_vmem_limit(blk_bytes, scratch_bytes)),
    )(hq, hq, hq, do, o, lse, *extra)


def _ffn_fwd(tag, x, g, wgu, wd, fs):
    n = _rms_fwd(tag + "_norm", x, g)
    gu, h = _ffn_up(tag + "_gu", n, wgu, fs)
    y = _mm(tag + "_down", h, wd, "nn", F32, res=x, alpha=0.5)
    return y, (x, g, n, gu, h)


def _ffn_bwd(tag, dx, dxb, wgu, wd, fs, saved, after=None, emit=None):
    x, g, n, gu, h = saved
    dgu = _ffn_dact(tag + "_dgu", dxb, wd, gu, fs, 0.5, after=after)
    dwd = _mm(tag + "_dwd", h, dxb, "tn", BF16, alpha=0.5)
    dwgu = _mm(tag + "_dwgu", n, dgu, "tn", BF16)
    token = emit(dwgu, dwd) if emit else None
    dn = _mm(tag + "_dn", dgu, wgu, "nt", F32, after=token)
    dx_in, dxb_in, dg = _rms_bwd(tag + "_dnorm", dn, x, g, dx)
    return dx_in, dxb_in, dg, dwgu, dwd, token


def _mixer_fwd(tag, kind, x, g, wqkv, wout, n_heads, n_sb, wf=None, bf=None, rope=None):
    n = _rms_fwd(tag + "_norm", x, g)
    hq = _mm(tag + "_qkv", n, wqkv, "nn", F32)
    if kind == "even":
        hf = _mm(tag + "_gate", n, wf, "nn", F32)
        cf, cft = _gate_fwd(tag + "_cumgate", hf, bf)
        cft = cft[:n_heads - n_sb].reshape(n_heads - n_sb, 1, -1)
        o, ob, lse = _attn_fwd_wide(tag + "_attn", hq, kind, n_heads, n_sb, cf=cf, cft=cft)
    else:
        hf = cf = cft = None
        o, ob, lse = _attn_fwd_wide(tag + "_attn", hq, kind, n_heads, n_sb, rope_c=rope[0], rope_s=rope[1])
    y = _mm(tag + "_out", ob, wout, "nn", F32, res=x)
    return y, (x, g, n, hq, hf, cf, cft, o, ob, lse)


def _mixer_bwd(tag, kind, dx, dxb, wqkv, wout, n_heads, n_sb, saved, wf=None, bf=None, rope=None, after=None,
               emit=None):
    x, g, n, hq, hf, cf, cft, o, ob, lse = saved
    do = _mm(tag + "_do", dxb, wout, "nt", F32, after=after)
    dwout = _mm(tag + "_dwout", ob, dxb, "tn", BF16)
    if kind == "even":
        dh3, dcft, drow = _attn_bwd_wide(tag + "_dattn", hq, do, o, lse, kind, n_heads, n_sb, cf=cf, cft=cft)
    else:
        (dh3,) = _attn_bwd_wide(tag + "_dattn", hq, do, o, lse, kind, n_heads, n_sb, rope_c=rope[0], rope_s=rope[1])
    dwqkv = _mm(tag + "_dwqkv", n, dh3, "tn", BF16)
    dwf = db = dhf = None
    if kind == "even":
        n_fox = n_heads - n_sb
        dcft = jnp.pad(dcft.reshape(n_fox, -1), ((0, LANES - n_fox), (0, 0)))
        dhf, db = _gate_bwd(tag + "_dcumgate", dcft, drow, hf, bf)
        dwf = _mm(tag + "_dwf", n, dhf, "tn", BF16)
    token = emit(dwqkv, dwout, dwf) if emit else None
    dn = _mm(tag + "_dn", dh3, wqkv, "nt", F32, after=token)
    if kind == "even":
        dn = _mm(tag + "_dn_gate", dhf, wf, "nt", F32, res=dn)
    dx_in, dxb_in, dg = _rms_bwd(tag + "_dnorm", dn, x, g, dx)
    return dx_in, dxb_in, dg, dwqkv, dwout, dwf, db, token


def _local_step(x, target, w, fs, n_heads, n_sb, emit=None):
    S, D = x.shape
    rope = _rope_tables(S)
    kinds = ("even", "odd")
    saved = []
    h = x
    if callable(w):
        fetch, w = w, {"norm_g": w("norm_g", None), "final_g": w("final_g", None),
                       "wgu1": [None, None], "wd1": [None, None], "wgu2": [None, None], "wd2": [None, None]}
    else:
        fetch = None
    for l, kind in enumerate(kinds):
        ng = [w["norm_g"][l, i][None, :] for i in range(3)]
        if fetch:
            w["wgu1"][l], w["wd1"][l] = fetch(("ffn1", l), h)
        h, s1 = _ffn_fwd(f"l{l}_ffn1", h, ng[0], w["wgu1"][l], w["wd1"][l], fs)
        if fetch:
            w.update(fetch(("mix", l), h))
        if kind == "even":
            h, s2 = _mixer_fwd(f"l{l}_mix", kind, h, ng[1], w["wqkv_e"], w["wout_e"], n_heads, n_sb,
                               wf=w["wf"], bf=w["bf"])
        else:
            h, s2 = _mixer_fwd(f"l{l}_mix", kind, h, ng[1], w["wqkv_o"], w["wout_o"], n_heads, n_sb, rope=rope)
        if fetch:
            w["wgu2"][l], w["wd2"][l] = fetch(("ffn2", l), h)
        h, s3 = _ffn_fwd(f"l{l}_ffn2", h, ng[2], w["wgu2"][l], w["wd2"][l], fs)
        saved.append((s1, s2, s3))

    dx, dxb, dfinal, loss = _loss_head("loss_head", h, w["final_g"], target)
    grads = {"dfinal": dfinal, "dnorm": [[None] * 3 for _ in kinds],
             "dwgu1": [None, None], "dwd1": [None, None], "dwgu2": [None, None], "dwd2": [None, None]}
    hand = lambda block: (lambda *mats: emit(block, mats)) if emit else None
    token = None
    for l in (1, 0):
        kind = kinds[l]
        s1, s2, s3 = saved[l]
        dx, dxb, dg, grads["dwgu2"][l], grads["dwd2"][l], token = _ffn_bwd(
            f"l{l}_ffn2", dx, dxb, w["wgu2"][l], w["wd2"][l], fs, s3, after=token, emit=hand(("ffn2", l)))
        grads["dnorm"][l][2] = dg
        if kind == "even":
            dx, dxb, dg, grads["dwqkv_e"], grads["dwout_e"], grads["dwf"], grads["db"], token = _mixer_bwd(
                f"l{l}_mix", kind, dx, dxb, w["wqkv_e"], w["wout_e"], n_heads, n_sb, s2, wf=w["wf"], bf=w["bf"],
                after=token, emit=hand(("mix", l)))
        else:
            dx, dxb, dg, grads["dwqkv_o"], grads["dwout_o"], _, _, token = _mixer_bwd(
                f"l{l}_mix", kind, dx, dxb, w["wqkv_o"], w["wout_o"], n_heads, n_sb, s2, rope=rope,
                after=token, emit=hand(("mix", l)))
        grads["dnorm"][l][1] = dg
        dx, dxb, dg, grads["dwgu1"][l], grads["dwd1"][l], token = _ffn_bwd(
            f"l{l}_ffn1", dx, dxb, w["wgu1"][l], w["wd1"][l], fs, s1, after=token, emit=hand(("ffn1", l)))
        grads["dnorm"][l][0] = dg
    return loss, dx, grads


def _cast_into(name, shard, layer, chip, full_shape, place, full=None):
    R, C = shard.shape[-2:]
    tr = _row_tile(R, 512, step=16)
    if layer is None:
        in_spec = pl.BlockSpec((tr, C), lambda i, k: (i, 0))
    else:
        in_spec = pl.BlockSpec((None, tr, C), lambda i, k: (layer, i, 0))
    lead = (None,) * (len(full_shape) - 2)
    out_spec = pl.BlockSpec(lead + (tr, C), lambda i, k: place(i, k[0]))

    def body(*refs):
        k_ref, w_ref = refs[:2]
        o_ref = refs[-1]
        o_ref[...] = w_ref[...].astype(BF16)

    in_specs = [in_spec] + ([_ANY] if full is not None else [])
    args = (chip, shard) + ((full,) if full is not None else ())
    grid_spec = pltpu.PrefetchScalarGridSpec(num_scalar_prefetch=1, grid=(R // tr,), in_specs=in_specs, out_specs=out_spec)
    return pl.pallas_call(
        body, name=name, grid_spec=grid_spec, out_shape=jax.ShapeDtypeStruct(full_shape, BF16),
        input_output_aliases={2: 0} if full is not None else {},
        compiler_params=pltpu.CompilerParams(dimension_semantics=("arbitrary",)),
    )(*args)


def _region_shape(grad, kind):
    if kind == "lead":
        return grad.shape[1] // N_CORES, grad.shape[2]
    rows, cols = grad.shape
    if kind == "cols":
        return rows // N_CORES, cols // N_CHIPS
    return rows // (N_CHIPS * N_CORES), cols


def _region_add(name, grad, kind, landed, core):
    rh, cw = _region_shape(grad, kind)
    tr = _row_tile(rh, 256, step=16)
    nrb = rh // tr
    if kind == "cols":
        g_spec = pl.BlockSpec((tr, cw), lambda k, r, c: (c[0] * nrb + r, k))
    elif kind == "rows":
        g_spec = pl.BlockSpec((tr, cw), lambda k, r, c: ((N_CORES * k + c[0]) * nrb + r, 0))
    else:
        g_spec = pl.BlockSpec((None, tr, cw), lambda k, r, c: (k, c[0] * nrb + r, 0))
    l_spec = pl.BlockSpec((None, tr, cw), lambda k, r, c: (k, r, 0))

    def body(c_ref, g_ref, l_ref, o_ref):
        o_ref[...] = (g_ref[...].astype(F32) + l_ref[...].astype(F32)).astype(BF16)

    grid_spec = pltpu.PrefetchScalarGridSpec(
        num_scalar_prefetch=1, grid=(N_CHIPS, nrb), in_specs=[g_spec, l_spec], out_specs=l_spec)
    return pl.pallas_call(
        body, name=name, grid_spec=grid_spec, out_shape=jax.ShapeDtypeStruct(landed.shape, BF16),
        compiler_params=pltpu.CompilerParams(dimension_semantics=("parallel", "parallel"),
                                             vmem_limit_bytes=_vmem_limit(3 * _nbytes((tr, cw), F32))),
    )(core, grad, landed)


def _chip_sum(name, pair, landed, pos):
    _, rh, cw = pair.shape
    tr = _row_tile(rh, max(16, 2**20 // (cw * 4)), step=16)
    nrb = rh // tr

    def body(p_ref, own_ref, l_ref, o_ref):
        acc = own_ref[...].astype(F32)
        for s in range(N_CHIPS - 1):
            acc = acc + l_ref[s].astype(F32)
        o_ref[...] = acc

    grid_spec = pltpu.PrefetchScalarGridSpec(
        num_scalar_prefetch=1, grid=(nrb,),
        in_specs=[pl.BlockSpec((None, tr, cw), lambda r, p: (p[0], r, 0)),
                  pl.BlockSpec((N_CHIPS - 1, tr, cw), lambda r, p: (0, r, 0))],
        out_specs=pl.BlockSpec((tr, cw), lambda r, p: (p[1] * nrb + r, 0)))
    return pl.pallas_call(
        body, name=name, grid_spec=grid_spec, out_shape=jax.ShapeDtypeStruct((N_CORES * rh, cw), F32),
        compiler_params=pltpu.CompilerParams(dimension_semantics=("arbitrary",)),
    )(pos, pair, landed)


def _sum_leading(name, parts):
    n, R, C = parts.shape
    tr = _row_tile(R, max(8, (2**20 // (C * 4)) // 8 * 8))

    def body(p_ref, o_ref):
        acc = p_ref[0]
        for s in range(1, n):
            acc = acc + p_ref[s]
        o_ref[...] = acc

    return pl.pallas_call(
        body, name=name, grid=(R // tr,),
        in_specs=[pl.BlockSpec((n, tr, C), lambda i: (0, i, 0))],
        out_specs=pl.BlockSpec((tr, C), lambda i: (i, 0)),
        out_shape=jax.ShapeDtypeStruct((R, C), F32),
        compiler_params=pltpu.CompilerParams(dimension_semantics=("parallel",)),
    )(parts)


def _adamw(name, w, g, m, v):
    shape = w.shape
    to2d = lambda t: t.reshape(-1, shape[-1]) if t.ndim > 1 else t.reshape(1, -1)
    w2, g2, m2, v2 = (to2d(t) for t in (w, g, m, v))
    R, C = w2.shape
    tr = _row_tile(R, 256)

    def body(w_ref, g_ref, m_ref, v_ref, d_ref, nm_ref, nv_ref):
        gv = g_ref[...]
        nm = ADAM_B1 * m_ref[...] + (1.0 - ADAM_B1) * gv
        nv = ADAM_B2 * v_ref[...] + (1.0 - ADAM_B2) * (gv * gv)
        m_hat = nm / (1.0 - ADAM_B1 ** ADAM_STEP)
        v_hat = nv / (1.0 - ADAM_B2 ** ADAM_STEP)
        d_ref[...] = -ADAM_LR * (m_hat / (jnp.sqrt(v_hat) + ADAM_EPS) + ADAM_WD * w_ref[...])
        nm_ref[...] = nm
        nv_ref[...] = nv

    spec = pl.BlockSpec((tr, C), lambda i: (i, 0))
    sds = jax.ShapeDtypeStruct((R, C), F32)
    d, nm, nv = pl.pallas_call(
        body, name=name, grid=(R // tr,), in_specs=[spec] * 4, out_specs=[spec] * 3, out_shape=[sds] * 3,
        compiler_params=pltpu.CompilerParams(dimension_semantics=("parallel",),
                                             vmem_limit_bytes=_vmem_limit(7 * _nbytes((tr, C), F32))),
    )(w2, g2, m2, v2)
    return d.reshape(shape), nm.reshape(shape), nv.reshape(shape)


_ANY = pl.BlockSpec(memory_space=pl.ANY)


def _mesh_pos():
    return lax.axis_index("x"), lax.axis_index("y"), lax.axis_index("c")


def _other_chips(x, y):
    return [(1 - x, y), (x, 1 - y), (1 - x, 1 - y)]


def _gather_over_chips(name, fulls, views):
    n = len(views)
    nf = len(fulls)

    def body(*refs):
        full = refs[nf:2 * nf]
        ici_send, ici_recv, d2d_send, d2d_recv = refs[2 * nf:]
        x, y, c = _mesh_pos()
        chips = _other_chips(x, y)
        mine = 2 * x + y
        sibling = (x, y, 1 - c)

        def ici(a, p, k):
            i, view, _ = views[a]
            part = view(full[i], k, c)
            return pltpu.make_async_remote_copy(
                src_ref=part, dst_ref=part, send_sem=ici_send.at[a, p], recv_sem=ici_recv.at[a, p],
                device_id=(*chips[p], c), device_id_type=MESH)

        def d2d(a, p, h):
            i, view, _ = views[a]
            px, py = chips[p]
            part = view(full[i], 2 * px + py, h)
            return pltpu.make_async_remote_copy(
                src_ref=part, dst_ref=part, send_sem=d2d_send.at[a, p], recv_sem=d2d_recv.at[a, p],
                device_id=sibling, device_id_type=MESH)

        sends = [ici(a, p, mine) for a in range(n) for p in range(3)]
        for cp in sends:
            cp.start()
        passed = []
        for a in range(n):
            for p, (px, py) in enumerate(chips):
                ici(a, p, 2 * px + py).wait_recv()
                if views[a][2]:
                    fwd = d2d(a, p, c)
                    fwd.start()
                    passed.append(fwd)
        for a in range(n):
            if views[a][2]:
                for p in range(3):
                    d2d(a, p, 1 - c).wait_recv()
        for cp in sends + passed:
            cp.wait_send()

    return pl.pallas_call(
        body, name=name, in_specs=[_ANY] * nf, out_specs=[_ANY] * nf,
        out_shape=[jax.ShapeDtypeStruct(f.shape, f.dtype) for f in fulls],
        input_output_aliases={i: i for i in range(nf)},
        scratch_shapes=[pltpu.SemaphoreType.DMA((n, 3))] * 4,
        compiler_params=pltpu.CompilerParams(has_side_effects=True),
    )(*fulls)


_HBM = pl.BlockSpec(memory_space=pltpu.HBM)
_SEM = pl.BlockSpec(memory_space=pltpu.SEMAPHORE)


def _in_hbm(arrays):
    return [pltpu.with_memory_space_constraint(a, pltpu.HBM) for a in arrays]


def _gather_start(name, fulls, views):
    nf = len(fulls)
    ng = 1 + max(g for _, _, g in views)

    def body(*refs):
        full = refs[nf:2 * nf]
        send_sems, recv_sems = refs[2 * nf:2 * nf + ng], refs[2 * nf + ng:]
        x, y, c = _mesh_pos()
        chips = _other_chips(x, y)
        for i, view, g in views:
            part = view(full[i], 2 * x + y, c)
            for px, py in chips:
                pltpu.make_async_remote_copy(
                    src_ref=part, dst_ref=part, send_sem=send_sems[g], recv_sem=recv_sems[g],
                    device_id=(px, py, c), device_id_type=MESH).start()

    outs = pl.pallas_call(
        body, name=name, in_specs=[_HBM] * nf, out_specs=[_HBM] * nf + [_SEM] * (2 * ng),
        out_shape=[pltpu.HBM(f.shape, f.dtype) for f in fulls] + [pltpu.SemaphoreType.DMA(())] * (2 * ng),
        input_output_aliases={i: i for i in range(nf)},
        compiler_params=pltpu.CompilerParams(has_side_effects=pltpu.SideEffectType.DATAFLOW_SIDE_EFFECTING),
    )(*_in_hbm(fulls))
    return list(outs[:nf]), list(outs[nf:nf + ng]), list(outs[nf + ng:])


def _gather_wait(name, fulls, views, send_sem, recv_sem, after):
    nf = len(fulls)

    def body(*refs):
        send_ref, recv_ref = refs[nf], refs[nf + 1]
        full = refs[nf + 3:]
        x, y, c = _mesh_pos()
        copies = [pltpu.make_async_remote_copy(
            src_ref=view(full[i], 2 * x + y, c), dst_ref=view(full[i], 2 * px + py, c),
            send_sem=send_ref, recv_sem=recv_ref, device_id=(px, py, c), device_id_type=MESH)
            for i, view in views for px, py in _other_chips(x, y)]
        for cp in copies:
            cp.wait_send()
        for cp in copies:
            cp.wait_recv()

    outs = pl.pallas_call(
        body, name=name, in_specs=[_HBM] * nf + [_SEM, _SEM, _ANY], out_specs=[_HBM] * nf,
        out_shape=[pltpu.HBM(f.shape, f.dtype) for f in fulls],
        input_output_aliases={i: i for i in range(nf)},
        compiler_params=pltpu.CompilerParams(has_side_effects=pltpu.SideEffectType.DATAFLOW_SIDE_EFFECTING),
    )(*fulls, send_sem, recv_sem, after)
    return list(outs)


def _forward_to_sibling(name, fulls, views):
    n, nf = len(views), len(fulls)

    def body(*refs):
        full = refs[nf:2 * nf]
        send_sems, recv_sems = refs[2 * nf:]
        x, y, c = _mesh_pos()
        chips = _other_chips(x, y)

        def copy(a, p, h):
            i, view = views[a]
            px, py = chips[p]
            part = view(full[i], 2 * px + py, h)
            return pltpu.make_async_remote_copy(
                src_ref=part, dst_ref=part, send_sem=send_sems.at[a, p], recv_sem=recv_sems.at[a, p],
                device_id=(x, y, 1 - c), device_id_type=MESH)

        sends = [copy(a, p, c) for a in range(n) for p in range(3)]
        for cp in sends:
            cp.start()
        for a in range(n):
            for p in range(3):
                copy(a, p, 1 - c).wait_recv()
        for cp in sends:
            cp.wait_send()

    return pl.pallas_call(
        body, name=name, in_specs=[_ANY] * nf, out_specs=[_ANY] * nf,
        out_shape=[jax.ShapeDtypeStruct(f.shape, f.dtype) for f in fulls],
        input_output_aliases={i: i for i in range(nf)},
        scratch_shapes=[pltpu.SemaphoreType.DMA((n, 3))] * 2,
        compiler_params=pltpu.CompilerParams(has_side_effects=True),
    )(*fulls)


def _region_view(ref, kind, k, c):
    if kind == "lead":
        rh = ref.shape[1] // N_CORES
        return ref.at[k, pl.ds(pl.multiple_of(c * rh, 8), rh), :]
    rows, cols = ref.shape
    if kind == "cols":
        rh, cw = rows // N_CORES, cols // N_CHIPS
        return ref.at[pl.ds(pl.multiple_of(c * rh, 8), rh), pl.ds(k * cw, cw)]
    rh = rows // (N_CHIPS * N_CORES)
    return ref.at[pl.ds(pl.multiple_of((N_CORES * k + c) * rh, 8), rh), :]


def _send_to_sibling(name, grads, kinds):
    n = len(grads)
    shapes = [jax.ShapeDtypeStruct((N_CHIPS,) + _region_shape(g, kd), g.dtype) for g, kd in zip(grads, kinds)]

    def body(*refs):
        g_ref, land = refs[:n], refs[n:2 * n]
        send_sems, recv_sems = refs[2 * n:]
        x, y, c = _mesh_pos()
        copies = []
        for a in range(n):
            for k in range(N_CHIPS):
                cp = pltpu.make_async_remote_copy(
                    src_ref=_region_view(g_ref[a], kinds[a], k, 1 - c), dst_ref=land[a].at[k],
                    send_sem=send_sems.at[a, k], recv_sem=recv_sems.at[a, k],
                    device_id=(x, y, 1 - c), device_id_type=MESH)
                cp.start()
                copies.append(cp)
        for cp in copies:
            cp.wait_recv()
        for cp in copies:
            cp.wait_send()

    return pl.pallas_call(
        body, name=name, in_specs=[_ANY] * n, out_specs=[_ANY] * n, out_shape=shapes,
        scratch_shapes=[pltpu.SemaphoreType.DMA((n, N_CHIPS)), pltpu.SemaphoreType.DMA((n, N_CHIPS))],
        compiler_params=pltpu.CompilerParams(has_side_effects=True),
    )(*grads)


def _scatter_over_chips(name, pair_sums):
    n = len(pair_sums)

    def body(*refs):
        p_ref, land = refs[:n], refs[n:2 * n]
        send_sems, recv_sems = refs[2 * n:]
        x, y, c = _mesh_pos()
        chips = _other_chips(x, y)
        sends = []
        for a in range(n):
            for p, (px, py) in enumerate(chips):
                cp = pltpu.make_async_remote_copy(
                    src_ref=p_ref[a].at[2 * px + py], dst_ref=land[a].at[p], send_sem=send_sems.at[a, p],
                    recv_sem=recv_sems.at[a, p], device_id=(px, py, c), device_id_type=MESH)
                cp.start()
                sends.append(cp)
        for cp in sends:
            cp.wait_recv()
        for cp in sends:
            cp.wait_send()

    return pl.pallas_call(
        body, name=name, in_specs=[_ANY] * n, out_specs=[_ANY] * n,
        out_shape=[jax.ShapeDtypeStruct((N_CHIPS - 1,) + p.shape[1:], p.dtype) for p in pair_sums],
        scratch_shapes=[pltpu.SemaphoreType.DMA((n, 3)), pltpu.SemaphoreType.DMA((n, 3))],
        compiler_params=pltpu.CompilerParams(has_side_effects=True),
    )(*pair_sums)


def _scatter_start(name, pair_sums):
    n = len(pair_sums)
    lands = [lax.empty((N_CHIPS - 1,) + p.shape[1:], p.dtype) for p in pair_sums]

    def body(*refs):
        p_ref, land = refs[2 * n:3 * n], refs[3 * n:4 * n]
        send_sem, recv_sem, token = refs[4 * n:]
        x, y, c = _mesh_pos()
        for a in range(n):
            for p, (px, py) in enumerate(_other_chips(x, y)):
                pltpu.make_async_remote_copy(
                    src_ref=p_ref[a].at[2 * px + py], dst_ref=land[a].at[p], send_sem=send_sem, recv_sem=recv_sem,
                    device_id=(px, py, c), device_id_type=MESH).start()
        token[...] = jnp.zeros_like(token)

    outs = pl.pallas_call(
        body, name=name, in_specs=[_HBM] * (2 * n),
        out_specs=[_HBM] * (2 * n) + [_SEM, _SEM, pl.BlockSpec(memory_space=pltpu.VMEM)],
        out_shape=[pltpu.HBM(t.shape, t.dtype) for t in list(pair_sums) + lands]
        + [pltpu.SemaphoreType.DMA(()), pltpu.SemaphoreType.DMA(()), jax.ShapeDtypeStruct((8, LANES), F32)],
        input_output_aliases={i: i for i in range(2 * n)},
        compiler_params=pltpu.CompilerParams(has_side_effects=pltpu.SideEffectType.DATAFLOW_SIDE_EFFECTING),
    )(*_in_hbm(list(pair_sums) + lands))
    return list(outs[:n]), list(outs[n:2 * n]), outs[2 * n], outs[2 * n + 1], outs[2 * n + 2]


def _scatter_wait(name, pair_sums, lands, send_sem, recv_sem, after):
    n = len(pair_sums)

    def body(*refs):
        send_ref, recv_ref = refs[2 * n], refs[2 * n + 1]
        p_ref, land = refs[2 * n + 3:3 * n + 3], refs[3 * n + 3:]
        x, y, c = _mesh_pos()
        copies = [pltpu.make_async_remote_copy(
            src_ref=p_ref[a].at[2 * px + py], dst_ref=land[a].at[p], send_sem=send_ref, recv_sem=recv_ref,
            device_id=(px, py, c), device_id_type=MESH)
            for a in range(n) for p, (px, py) in enumerate(_other_chips(x, y))]
        for cp in copies:
            cp.wait_send()
        for cp in copies:
            cp.wait_recv()

    outs = pl.pallas_call(
        body, name=name, in_specs=[_HBM] * (2 * n) + [_SEM, _SEM, _ANY], out_specs=[_HBM] * (2 * n),
        out_shape=[pltpu.HBM(t.shape, t.dtype) for t in list(pair_sums) + list(lands)],
        input_output_aliases={i: i for i in range(2 * n)},
        compiler_params=pltpu.CompilerParams(has_side_effects=pltpu.SideEffectType.DATAFLOW_SIDE_EFFECTING),
    )(*pair_sums, *lands, send_sem, recv_sem, after)
    return list(outs[:n]), list(outs[n:])


def _swap_halves(name, shards):
    n = len(shards)

    def body(*refs):
        out = refs[n:2 * n]
        send_sems, recv_sems = refs[2 * n:]
        x, y, c = _mesh_pos()
        sends = []
        for a in range(n):
            rh = out[a].shape[0] // N_CORES
            mine = out[a].at[pl.ds(pl.multiple_of(c * rh, 8), rh), :]
            cp = pltpu.make_async_remote_copy(
                src_ref=mine, dst_ref=mine, send_sem=send_sems.at[a], recv_sem=recv_sems.at[a],
                device_id=(x, y, 1 - c), device_id_type=MESH)
            cp.start()
            sends.append(cp)
        for a in range(n):
            rh = out[a].shape[0] // N_CORES
            theirs = out[a].at[pl.ds(pl.multiple_of((1 - c) * rh, 8), rh), :]
            pltpu.make_async_remote_copy(
                src_ref=theirs, dst_ref=theirs, send_sem=send_sems.at[a], recv_sem=recv_sems.at[a],
                device_id=(x, y, 1 - c), device_id_type=MESH).wait_recv()
        for cp in sends:
            cp.wait_send()

    return pl.pallas_call(
        body, name=name, in_specs=[_ANY] * n, out_specs=[_ANY] * n,
        out_shape=[jax.ShapeDtypeStruct(s.shape, s.dtype) for s in shards],
        input_output_aliases={i: i for i in range(n)},
        scratch_shapes=[pltpu.SemaphoreType.DMA((n,)), pltpu.SemaphoreType.DMA((n,))],
        compiler_params=pltpu.CompilerParams(has_side_effects=True),
    )(*shards)


def _gather_all_devices(name, block):
    R, C = block.shape
    ndev = N_CHIPS * N_CORES

    def body(b_ref, out_ref, send_sems, recv_sems, local_sem):
        x, y, c = _mesh_pos()
        mine = 4 * x + 2 * y + c
        own = pltpu.make_async_copy(b_ref, out_ref.at[mine], local_sem)
        own.start()
        sends = []
        for mask in range(1, ndev):
            fx, fy, fc = (mask >> 2) & 1, (mask >> 1) & 1, mask & 1
            px, py, pc = x ^ fx, y ^ fy, c ^ fc
            cp = pltpu.make_async_remote_copy(
                src_ref=b_ref, dst_ref=out_ref.at[mine], send_sem=send_sems.at[mask - 1],
                recv_sem=recv_sems.at[mask - 1], device_id=(px, py, pc), device_id_type=MESH)
            cp.start()
            sends.append(cp)
        for mask in range(1, ndev):
            fx, fy, fc = (mask >> 2) & 1, (mask >> 1) & 1, mask & 1
            px, py, pc = x ^ fx, y ^ fy, c ^ fc
            pltpu.make_async_remote_copy(
                src_ref=b_ref, dst_ref=out_ref.at[4 * px + 2 * py + pc], send_sem=send_sems.at[mask - 1],
                recv_sem=recv_sems.at[mask - 1], device_id=(px, py, pc), device_id_type=MESH).wait_recv()
        for cp in sends:
            cp.wait_send()
        own.wait()

    return pl.pallas_call(
        body, name=name, in_specs=[_ANY], out_specs=_ANY,
        out_shape=jax.ShapeDtypeStruct((ndev, R, C), F32),
        scratch_shapes=[pltpu.SemaphoreType.DMA((ndev - 1,)), pltpu.SemaphoreType.DMA((ndev - 1,)),
                        pltpu.SemaphoreType.DMA(())],
        compiler_params=pltpu.CompilerParams(has_side_effects=True),
    )(block)


def _reduce_scatter(grads, kinds, pos):
    landed = _send_to_sibling("rs_pair_send", grads, kinds)
    pair = [_region_add(f"rs_pair_add_{a}", g, kd, ld, pos[1:]) for a, (g, kd, ld) in enumerate(zip(grads, kinds, landed))]
    parts = _scatter_over_chips("rs_chip_send", pair)
    shards = [_chip_sum(f"rs_chip_add_{a}", p, ld, pos) for a, (p, ld) in enumerate(zip(pair, parts))]
    return _swap_halves("rs_swap_halves", shards)


def kernel(x, norm_g, ffn1_w_gate, ffn1_w_up, ffn1_w_down, ffn2_w_gate, ffn2_w_up, ffn2_w_down, even_w_in, even_b_forget, even_w_out, odd_w_qkv, odd_w_out, final_norm_g, loss_target, m_norm_g, m_ffn1_w_gate, m_ffn1_w_up, m_ffn1_w_down, m_ffn2_w_gate, m_ffn2_w_up, m_ffn2_w_down, m_even_w_in, m_even_b_forget, m_even_w_out, m_odd_w_qkv, m_odd_w_out, m_final_norm_g, v_norm_g, v_ffn1_w_gate, v_ffn1_w_up, v_ffn1_w_down, v_ffn2_w_gate, v_ffn2_w_up, v_ffn2_w_down, v_even_w_in, v_even_b_forget, v_even_w_out, v_odd_w_qkv, v_odd_w_out, v_final_norm_g):
    _, S, D = x.shape
    L = norm_g.shape[0]
    assert L == 2 and even_w_in.shape[0] == 1 and odd_w_qkv.shape[0] == 1
    fs = ffn1_w_gate.shape[2]
    F = N_CHIPS * fs
    wc = even_w_in.shape[2]
    n_heads = D // HEAD_DIM
    n_fox = N_CHIPS * wc - 3 * D
    n_sb = n_heads - n_fox
    qs = odd_w_qkv.shape[2]
    os_ = even_w_out.shape[1]
    ns = norm_g.shape[2]
    xi, yi, ci = _mesh_pos()
    chip = 2 * xi + yi

    pos = jnp.stack([chip, ci]).astype(jnp.int32)
    kchip = pos[:1]
    lane = lambda start, size: pl.ds(pl.multiple_of(start, LANES), size)
    sub = lambda start, size: pl.ds(pl.multiple_of(start, 16), size)
    gate_view = lambda r, k, h: r.at[sub(h * (D // 2), D // 2), lane(k * 2 * fs, fs)]
    up_view = lambda r, k, h: r.at[sub(h * (D // 2), D // 2), lane(k * 2 * fs + fs, fs)]
    down_view = lambda r, k, h: r.at[sub(k * fs + h * (fs // 2), fs // 2), :]
    out_view = lambda r, k, h: r.at[sub(k * os_ + h * (os_ // 2), os_ // 2), :]
    tr_d = _row_tile(fs, 512, step=16)
    tr_o = _row_tile(os_, 512, step=16)
    ffn_w = {"ffn1": (ffn1_w_gate, ffn1_w_up, ffn1_w_down), "ffn2": (ffn2_w_gate, ffn2_w_up, ffn2_w_down)}
    win_view = lambda r, k, h: r.at[k, sub(h * (D // 2), D // 2), :]
    qkv_view = lambda r, k, h: r.at[sub(h * (D // 2), D // 2), lane(k * qs, qs)]
    fulls, views, groups = [], [], {}
    for l in range(L):
        for blk in ("ffn1", "mix", "ffn2"):
            o, v0 = len(fulls), len(views)
            if blk == "mix" and l == 0:
                fulls += [_cast_into("cast_win", even_w_in, 0, kchip, (N_CHIPS, D, wc), lambda i, k: (k, i, 0)),
                          _cast_into("cast_wout_e", even_w_out, 0, kchip, (D, D), lambda i, k: (k * (os_ // tr_o) + i, 0))]
                views += [(o, win_view), (o + 1, out_view)]
            elif blk == "mix":
                fulls += [_cast_into("cast_wqkv_o", odd_w_qkv, 0, kchip, (D, N_CHIPS * qs), lambda i, k: (i, k)),
                          _cast_into("cast_wout_o", odd_w_out, 0, kchip, (D, D), lambda i, k: (k * (os_ // tr_o) + i, 0))]
                views += [(o, qkv_view), (o + 1, out_view)]
            else:
                wg, wu, wd = ffn_w[blk]
                t = f"cast_{blk}_l{l}"
                gu = _cast_into(t + "_gate", wg, l, kchip, (D, 2 * F), lambda i, k: (i, 2 * k))
                gu = _cast_into(t + "_up", wu, l, kchip, (D, 2 * F), lambda i, k: (i, 2 * k + 1), full=gu)
                dn = _cast_into(t + "_down", wd, l, kchip, (F, D), lambda i, k: (k * (fs // tr_d) + i, 0))
                fulls += [gu, dn]
                views += [(o, gate_view), (o, up_view), (o + 1, down_view)]
            gid = len(groups)
            views[v0:] = [(i, view, gid) for i, view in views[v0:]]
            groups[(blk, l)] = (gid, list(range(o, len(fulls))), list(range(v0, len(views))))
    norm_own = lax.dynamic_update_slice(jnp.zeros((L, 3, N_CHIPS * ns), F32), norm_g, (0, 0, chip * ns))
    (norm_full,) = _gather_over_chips("gather_norm", [norm_own], [(0, lambda r, k, h: r.at[:, :, lane(k * ns, ns)], False)])
    started, send_sems, recv_sems = _gather_start("gather_start", fulls, views)

    def fetch(block, after):
        if block == "norm_g":
            return norm_full
        if block == "final_g":
            return final_norm_g[None, :]
        gid, arrays, rows = groups[block]
        tag = f"{block[0]}_l{block[1]}"
        local = [(views[a][0] - arrays[0], views[a][1]) for a in rows]
        got = _gather_wait("gather_wait_" + tag, [started[i] for i in arrays], local, send_sems[gid], recv_sems[gid],
                           x if after is None else after)
        got = _forward_to_sibling("gather_pass_" + tag, got, local)
        if block[0] != "mix":
            return got
        if block[1] == 1:
            return {"wqkv_o": got[0], "wout_o": got[1]}
        win = jnp.concatenate([got[0][k] for k in range(N_CHIPS)], axis=1)
        return {"wqkv_e": win[:, :3 * D], "wf": jnp.pad(win[:, 3 * D:], ((0, 0), (0, LANES - n_fox))),
                "bf": jnp.pad(even_b_forget, ((0, 0), (0, LANES - n_fox))), "wout_e": got[1]}

    pending, shards = [], {}

    def finish(after):
        block, pair, lands, ssem, rsem = pending.pop(0)
        tag = f"{block[0]}_l{block[1]}"
        pair, lands = _scatter_wait("rs_chip_wait_" + tag, pair, lands, ssem, rsem, after)
        shards[block] = [_chip_sum(f"rs_chip_add_{tag}_{a}", p, ld, pos) for a, (p, ld) in enumerate(zip(pair, lands))]

    def emit(block, mats):
        blk, l = block
        tag = f"{blk}_l{l}"
        kinds = ["cols", "rows"]
        if blk == "mix" and l == 0:
            dwqkv, dwout, dwf = mats
            dwin = jnp.concatenate([dwqkv, dwf[:, :n_fox]], axis=1)
            mats = [jnp.stack([dwin[:, k * wc:(k + 1) * wc] for k in range(N_CHIPS)]), dwout]
            kinds = ["lead", "rows"]
        elif blk == "mix":
            mats = list(mats[:2])
        else:
            mats = list(mats)
        landed = _send_to_sibling("rs_pair_send_" + tag, mats, kinds)
        pair = [_region_add(f"rs_pair_add_{tag}_{a}", m, kd, ld, pos[1:])
                for a, (m, kd, ld) in enumerate(zip(mats, kinds, landed))]
        pair, lands, ssem, rsem, token = _scatter_start("rs_chip_start_" + tag, pair)
        if pending:
            finish(token)
        pending.append((block, pair, lands, ssem, rsem))
        return token

    loss_vec, grad_x, g = _local_step(x[0], loss_target[0], fetch, fs, n_heads, n_sb, emit=emit)
    finish(grad_x)
    order = [(b, l) for b in ("ffn1", "ffn2", "mix") for l in range(L)]
    red = _swap_halves("rs_swap_halves", [s for b in order for s in shards[b]])
    red = {b: red[2 * i:2 * i + 2] for i, b in enumerate(order)}
    gu1, gd1 = [red[("ffn1", l)][0] for l in range(L)], [red[("ffn1", l)][1] for l in range(L)]
    gu2, gd2 = [red[("ffn2", l)][0] for l in range(L)], [red[("ffn2", l)][1] for l in range(L)]
    (g_win, g_wout_e), (g_qkv_o, g_wout_o) = red[("mix", 0)], red[("mix", 1)]

    small_rows = [g["dnorm"][l][i] for l in range(L) for i in range(3)] + [
        g["dfinal"], jnp.pad(g["db"], ((0, 0), (0, D - LANES))), jnp.pad(loss_vec, ((0, 0), (0, D - LANES)))]
    small = jnp.concatenate(small_rows + [jnp.zeros((16 - len(small_rows), D), F32)], axis=0)
    small_sum = _sum_leading("small_sum", _gather_all_devices("small_gather", small))
    loss = small_sum[3 * L + 2, 0]
    g_norm = lax.dynamic_slice_in_dim(small_sum[:3 * L].reshape(L, 3, D), chip * ns, ns, axis=2)
    g_final = small_sum[3 * L]
    g_bf = small_sum[3 * L + 1, :n_fox][None, :]

    grads = [
        g_norm,
        jnp.stack([t[:, :fs] for t in gu1]), jnp.stack([t[:, fs:] for t in gu1]), jnp.stack(gd1),
        jnp.stack([t[:, :fs] for t in gu2]), jnp.stack([t[:, fs:] for t in gu2]), jnp.stack(gd2),
        g_win[None], g_bf, g_wout_e[None], g_qkv_o[None], g_wout_o[None], g_final]
    weights = [norm_g, ffn1_w_gate, ffn1_w_up, ffn1_w_down, ffn2_w_gate, ffn2_w_up, ffn2_w_down,
               even_w_in, even_b_forget, even_w_out, odd_w_qkv, odd_w_out, final_norm_g]
    ms = [m_norm_g, m_ffn1_w_gate, m_ffn1_w_up, m_ffn1_w_down, m_ffn2_w_gate, m_ffn2_w_up, m_ffn2_w_down,
          m_even_w_in, m_even_b_forget, m_even_w_out, m_odd_w_qkv, m_odd_w_out, m_final_norm_g]
    vs = [v_norm_g, v_ffn1_w_gate, v_ffn1_w_up, v_ffn1_w_down, v_ffn2_w_gate, v_ffn2_w_up, v_ffn2_w_down,
          v_even_w_in, v_even_b_forget, v_even_w_out, v_odd_w_qkv, v_odd_w_out, v_final_norm_g]
    deltas, new_ms, new_vs = [], [], []
    for i, (wt, gt, mt, vt) in enumerate(zip(weights, grads, ms, vs)):
        d, nm, nv = _adamw(f"adamw_{i}", wt, gt, mt, vt)
        deltas.append(d)
        new_ms.append(nm)
        new_vs.append(nv)
    return (loss, grad_x[None], *grads, *deltas, *new_ms, *new_vs)
```

```python
import functools
import math

import jax
import jax.numpy as jnp
from jax import lax
from jax.experimental import pallas as pl
from jax.experimental.pallas import tpu as pltpu

F32 = jnp.float32
BF16 = jnp.bfloat16

HEAD_DIM = 128
ROPE_DIMS = 32
ROPE_THETA = 500000.0
DILATED_PATTERNS = ((128, 1), (512, 4), (2048, 16))
RMS_EPS = 1e-6
NEG_INF = -1e30
ADAM_LR = 0.001
ADAM_B1 = 0.9
ADAM_B2 = 0.999
ADAM_EPS = 1e-08
ADAM_WD = 0.01
ADAM_STEP = 10

N_CHIPS = 4
N_CORES = 2
LANES = 128
BLK = 256
VMEM_BYTES_V7X = 64 * 2**20
MESH = pl.DeviceIdType.MESH


def _vmem_limit(block_bytes, scratch_bytes=0):
    need = 2 * block_bytes + scratch_bytes + 12 * 2**20
    return int(min(need, VMEM_BYTES_V7X - 6 * 2**20))


def _nbytes(shape, dtype):
    return math.prod(shape) * jnp.dtype(dtype).itemsize


def _tile(dim, target):
    best = None
    for t in range(LANES, min(dim, target) + 1, LANES):
        if dim % t == 0:
            best = t
    assert best is not None, (dim, target)
    return best


def _row_tile(rows, target, step=8):
    if rows <= target:
        return rows
    best = None
    for t in range(step, target + 1, step):
        if rows % t == 0:
            best = t
    assert best is not None, (rows, target)
    return best


def _mm(name, a, b, mode, out_dtype, res=None, alpha=1.0, after=None, tm_target=1024, tn_target=1536, tk_target=2048):
    a3 = a.ndim == 3
    b3 = b.ndim == 3
    if mode == "nn":
        assert not a3 and not b3
        (M, K), (K2, N) = a.shape, b.shape
    elif mode == "nt":
        assert not b3
        if a3:
            P, M, Kp = a.shape
            K = P * Kp
        else:
            M, K = a.shape
        N, K2 = b.shape
    else:
        assert mode == "tn" and not a3
        K, M = a.shape
        if b3:
            P, K2, Np = b.shape
            N = P * Np
        else:
            K2, N = b.shape
    assert K == K2, (name, a.shape, b.shape)
    tm = _tile(M, tm_target)
    tn = _tile(Np if b3 else N, tn_target)
    tk = _tile(Kp if a3 else K, tk_target)
    nk = K // tk
    grid = (M // tm, N // tn, nk)

    if mode == "nn":
        a_spec = pl.BlockSpec((tm, tk), lambda i, j, k: (i, k))
        b_spec = pl.BlockSpec((tk, tn), lambda i, j, k: (k, j))
        dims = (((1,), (0,)), ((), ()))
    elif mode == "nt":
        if a3:
            nkp = Kp // tk
            a_spec = pl.BlockSpec((None, tm, tk), lambda i, j, k: (k // nkp, i, k % nkp))
        else:
            a_spec = pl.BlockSpec((tm, tk), lambda i, j, k: (i, k))
        b_spec = pl.BlockSpec((tn, tk), lambda i, j, k: (j, k))
        dims = (((1,), (1,)), ((), ()))
    else:
        a_spec = pl.BlockSpec((tk, tm), lambda i, j, k: (k, i))
        if b3:
            njp = Np // tn
            b_spec = pl.BlockSpec((None, tk, tn), lambda i, j, k: (j // njp, k, j % njp))
        else:
            b_spec = pl.BlockSpec((tk, tn), lambda i, j, k: (k, j))
        dims = (((0,), (0,)), ((), ()))
    o_spec = pl.BlockSpec((tm, tn), lambda i, j, k: (i, j))
    has_res = res is not None

    def finish(y, r_ref, o_ref):
        if alpha != 1.0:
            y = y * alpha
        if has_res:
            y = r_ref[...] + y
        o_ref[...] = y.astype(o_ref.dtype)

    n_in = 2 + has_res + (after is not None)

    def body(*refs):
        a_ref, b_ref = refs[:2]
        r_ref = refs[2] if has_res else None
        o_ref = refs[n_in]
        part = lax.dot_general(a_ref[...], b_ref[...], dims, preferred_element_type=F32)
        if nk == 1:
            finish(part, r_ref, o_ref)
            return
        acc_ref = refs[-1]
        k = pl.program_id(2)

        @pl.when(k == 0)
        def _():
            acc_ref[...] = part

        @pl.when(k > 0)
        def _():
            acc_ref[...] += part

        @pl.when(k == nk - 1)
        def _():
            finish(acc_ref[...], r_ref, o_ref)

    in_specs = [a_spec, b_spec] + ([o_spec] if has_res else []) + ([_ANY] if after is not None else [])
    args = (a, b) + ((res,) if has_res else ()) + ((after,) if after is not None else ())
    blk = (_nbytes((tm, tk), a.dtype) + _nbytes((tk, tn), b.dtype) + _nbytes((tm, tn), out_dtype)
           + (_nbytes((tm, tn), F32) if has_res else 0))
    return pl.pallas_call(
        body, name=name, grid=grid, in_specs=in_specs, out_specs=o_spec,
        out_shape=jax.ShapeDtypeStruct((M, N), out_dtype),
        scratch_shapes=[pltpu.VMEM((tm, tn), F32)] if nk > 1 else [],
        compiler_params=pltpu.CompilerParams(
            dimension_semantics=("parallel", "parallel", "arbitrary"),
            vmem_limit_bytes=_vmem_limit(blk, 2 * _nbytes((tm, tn), F32))),
    )(*args)


def _rms_fwd(name, x, g):
    S, D = x.shape
    tr = _row_tile(S, 256)

    def body(x_ref, g_ref, n_ref):
        xv = x_ref[...]
        r = lax.rsqrt(jnp.mean(xv * xv, axis=-1, keepdims=True) + RMS_EPS)
        n_ref[...] = (xv * r * g_ref[...]).astype(BF16)

    return pl.pallas_call(
        body, name=name, grid=(S // tr,),
        in_specs=[pl.BlockSpec((tr, D), lambda i: (i, 0)), pl.BlockSpec((1, D), lambda i: (0, 0))],
        out_specs=pl.BlockSpec((tr, D), lambda i: (i, 0)),
        out_shape=jax.ShapeDtypeStruct((S, D), BF16),
        compiler_params=pltpu.CompilerParams(dimension_semantics=("parallel",)),
    )(x, g)


def _rms_bwd(name, dn, x, g, dres):
    S, D = x.shape
    tr = _row_tile(S, 256)

    def body(dn_ref, x_ref, g_ref, dres_ref, dx_ref, dxb_ref, dg_ref):
        i = pl.program_id(0)
        xv = x_ref[...]
        dnv = dn_ref[...]
        r = lax.rsqrt(jnp.mean(xv * xv, axis=-1, keepdims=True) + RMS_EPS)
        u = dnv * g_ref[...]
        dot = jnp.mean(u * xv, axis=-1, keepdims=True)
        dx = dres_ref[...] + r * u - xv * (r * r * r * dot)
        dx_ref[...] = dx
        dxb_ref[...] = dx.astype(BF16)

        @pl.when(i == 0)
        def _():
            dg_ref[...] = jnp.zeros_like(dg_ref)

        dg_ref[...] += jnp.sum(dnv * xv * r, axis=0, keepdims=True)

    row = pl.BlockSpec((tr, D), lambda i: (i, 0))
    vec = pl.BlockSpec((1, D), lambda i: (0, 0))
    return pl.pallas_call(
        body, name=name, grid=(S // tr,),
        in_specs=[row, row, vec, row], out_specs=[row, row, vec],
        out_shape=[jax.ShapeDtypeStruct((S, D), F32), jax.ShapeDtypeStruct((S, D), BF16),
                   jax.ShapeDtypeStruct((1, D), F32)],
        compiler_params=pltpu.CompilerParams(dimension_semantics=("arbitrary",)),
    )(dn, x, g, dres)


def _loss_head(name, x, g, target):
    S, D = x.shape
    tr = _row_tile(S, 256)

    def body(x_ref, g_ref, t_ref, dx_ref, dxb_ref, dg_ref, loss_ref):
        i = pl.program_id(0)
        xv = x_ref[...]
        gv = g_ref[...]
        r = lax.rsqrt(jnp.mean(xv * xv, axis=-1, keepdims=True) + RMS_EPS)
        diff = xv * r * gv - t_ref[...]
        part = 0.5 * jnp.sum(jnp.mean(diff * diff, axis=-1, keepdims=True), axis=0, keepdims=True)
        dy = diff * (1.0 / D)
        u = dy * gv
        dot = jnp.mean(u * xv, axis=-1, keepdims=True)
        dx = r * u - xv * (r * r * r * dot)
        dx_ref[...] = dx
        dxb_ref[...] = dx.astype(BF16)

        @pl.when(i == 0)
        def _():
            dg_ref[...] = jnp.zeros_like(dg_ref)
            loss_ref[...] = jnp.zeros_like(loss_ref)

        dg_ref[...] += jnp.sum(dy * xv * r, axis=0, keepdims=True)
        loss_ref[...] += jnp.broadcast_to(part, loss_ref.shape)

    row = pl.BlockSpec((tr, D), lambda i: (i, 0))
    vec = pl.BlockSpec((1, D), lambda i: (0, 0))
    lvec = pl.BlockSpec((1, LANES), lambda i: (0, 0))
    return pl.pallas_call(
        body, name=name, grid=(S // tr,),
        in_specs=[row, vec, row], out_specs=[row, row, vec, lvec],
        out_shape=[jax.ShapeDtypeStruct((S, D), F32), jax.ShapeDtypeStruct((S, D), BF16),
                   jax.ShapeDtypeStruct((1, D), F32), jax.ShapeDtypeStruct((1, LANES), F32)],
        compiler_params=pltpu.CompilerParams(dimension_semantics=("arbitrary",)),
    )(x, g, target)


def _swiglu_fwd(name, gu, fs):
    S, two_f = gu.shape
    nslab = two_f // (2 * fs)
    tr = _row_tile(S, 256)

    def body(gu_ref, h_ref):
        gv = gu_ref[:, :fs]
        uv = gu_ref[:, fs:]
        h_ref[...] = (gv * jax.nn.sigmoid(gv) * uv).astype(BF16)

    return pl.pallas_call(
        body, name=name, grid=(S // tr, nslab),
        in_specs=[pl.BlockSpec((tr, 2 * fs), lambda i, k: (i, k))],
        out_specs=pl.BlockSpec((tr, fs), lambda i, k: (i, k)),
        out_shape=jax.ShapeDtypeStruct((S, nslab * fs), BF16),
        compiler_params=pltpu.CompilerParams(dimension_semantics=("parallel", "parallel")),
    )(gu)


def _swiglu_bwd(name, dh, gu, fs):
    S, two_f = gu.shape
    nslab = two_f // (2 * fs)
    tr = _row_tile(S, 256)

    def body(dh_ref, gu_ref, o_ref):
        gv = gu_ref[:, :fs]
        uv = gu_ref[:, fs:]
        dhv = dh_ref[...]
        sg = jax.nn.sigmoid(gv)
        silu = gv * sg
        o_ref[:, :fs] = (dhv * uv * (sg + silu * (1.0 - sg))).astype(BF16)
        o_ref[:, fs:] = (dhv * silu).astype(BF16)

    return pl.pallas_call(
        body, name=name, grid=(S // tr, nslab),
        in_specs=[pl.BlockSpec((tr, fs), lambda i, k: (i, k)), pl.BlockSpec((tr, 2 * fs), lambda i, k: (i, k))],
        out_specs=pl.BlockSpec((tr, 2 * fs), lambda i, k: (i, k)),
        out_shape=jax.ShapeDtypeStruct((S, two_f), BF16),
        compiler_params=pltpu.CompilerParams(dimension_semantics=("parallel", "parallel")),
    )(dh, gu)


def _ffn_up(name, n, wgu, fs, tm_target=512):
    S, D = n.shape
    nslab = wgu.shape[1] // (2 * fs)
    tm = _tile(S, tm_target)

    def body(n_ref, w_ref, gu_ref, h_ref):
        y = jnp.dot(n_ref[...], w_ref[...], preferred_element_type=F32)
        gu_ref[...] = y
        gv = y[:, :fs]
        h_ref[...] = (gv * jax.nn.sigmoid(gv) * y[:, fs:]).astype(BF16)

    blk = _nbytes((tm, D), BF16) + _nbytes((D, 2 * fs), BF16) + _nbytes((tm, 2 * fs), F32) + _nbytes((tm, fs), BF16)
    return pl.pallas_call(
        body, name=name, grid=(nslab, S // tm),
        in_specs=[pl.BlockSpec((tm, D), lambda k, i: (i, 0)), pl.BlockSpec((D, 2 * fs), lambda k, i: (0, k))],
        out_specs=[pl.BlockSpec((tm, 2 * fs), lambda k, i: (i, k)), pl.BlockSpec((tm, fs), lambda k, i: (i, k))],
        out_shape=[jax.ShapeDtypeStruct((S, nslab * 2 * fs), F32), jax.ShapeDtypeStruct((S, nslab * fs), BF16)],
        compiler_params=pltpu.CompilerParams(dimension_semantics=("parallel", "parallel"),
                                             vmem_limit_bytes=_vmem_limit(blk, _nbytes((tm, 2 * fs), F32))),
    )(n, wgu)


def _ffn_dact(name, dyb, wd, gu, fs, alpha, after=None, tm_target=512):
    S, D = dyb.shape
    nslab = wd.shape[0] // fs
    tm = _tile(S, tm_target)

    def body(*refs):
        d_ref, w_ref, gu_ref = refs[:3]
        o_ref = refs[-1]
        dhv = _dot_nt(d_ref[...], w_ref[...]) * alpha
        gv = gu_ref[:, :fs]
        uv = gu_ref[:, fs:]
        sg = jax.nn.sigmoid(gv)
        silu = gv * sg
        o_ref[:, :fs] = (dhv * uv * (sg + silu * (1.0 - sg))).astype(BF16)
        o_ref[:, fs:] = (dhv * silu).astype(BF16)

    in_specs = [pl.BlockSpec((tm, D), lambda k, i: (i, 0)), pl.BlockSpec((fs, D), lambda k, i: (k, 0)),
                pl.BlockSpec((tm, 2 * fs), lambda k, i: (i, k))] + ([_ANY] if after is not None else [])
    args = (dyb, wd, gu) + ((after,) if after is not None else ())
    blk = _nbytes((tm, D), BF16) + _nbytes((fs, D), BF16) + _nbytes((tm, 2 * fs), F32) + _nbytes((tm, 2 * fs), BF16)
    return pl.pallas_call(
        body, name=name, grid=(nslab, S // tm), in_specs=in_specs,
        out_specs=pl.BlockSpec((tm, 2 * fs), lambda k, i: (i, k)),
        out_shape=jax.ShapeDtypeStruct((S, nslab * 2 * fs), BF16),
        compiler_params=pltpu.CompilerParams(dimension_semantics=("parallel", "parallel"),
                                             vmem_limit_bytes=_vmem_limit(blk, 2 * _nbytes((tm, fs), F32))),
    )(*args)


def _tri_rows(r0, nrows, ncols, lower):
    row = lax.broadcasted_iota(jnp.int32, (nrows, ncols), 0) + r0
    col = lax.broadcasted_iota(jnp.int32, (nrows, ncols), 1)
    return jnp.where((col <= row) if lower else (col >= row), 1.0, 0.0).astype(F32)


def _gate_fwd(name, hf, b):
    S = hf.shape[0]
    tb = _row_tile(S, 256)

    def body(hf_ref, b_ref, cf_ref, cft_ref, lf_ref):
        zz = hf_ref[...] + b_ref[...]
        lf_ref[...] = jnp.minimum(zz, 0.0) - jnp.log1p(jnp.exp(-jnp.abs(zz)))

        def blk(i, c):
            r0 = pl.multiple_of(i * tb, tb)
            tri = _tri_rows(r0, tb, S, True)
            cf_ref[pl.ds(r0, tb), :] = jnp.dot(tri, lf_ref[...], precision=lax.Precision.HIGHEST,
                                               preferred_element_type=F32)
            return c

        lax.fori_loop(0, S // tb, blk, 0)
        cft_ref[...] = cf_ref[...].T

    full = pl.BlockSpec((S, LANES), lambda: (0, 0))
    return pl.pallas_call(
        body, name=name, in_specs=[full, pl.BlockSpec((1, LANES), lambda: (0, 0))],
        out_specs=[full, pl.BlockSpec((LANES, S), lambda: (0, 0))],
        out_shape=[jax.ShapeDtypeStruct((S, LANES), F32), jax.ShapeDtypeStruct((LANES, S), F32)],
        scratch_shapes=[pltpu.VMEM((S, LANES), F32)],
    )(hf, b)


def _gate_bwd(name, dcft, drow, hf, b):
    S = hf.shape[0]
    tb = _row_tile(S, 256)

    def body(dcft_ref, drow_ref, hf_ref, b_ref, dhf_ref, db_ref, dcf_ref, dlf_ref):
        dcf_ref[...] = dcft_ref[...].T + drow_ref[...]

        def blk(i, c):
            r0 = pl.multiple_of(i * tb, tb)
            tri = _tri_rows(r0, tb, S, False)
            dlf_ref[pl.ds(r0, tb), :] = jnp.dot(tri, dcf_ref[...], precision=lax.Precision.HIGHEST,
                                                preferred_element_type=F32)
            return c

        lax.fori_loop(0, S // tb, blk, 0)
        zz = hf_ref[...] + b_ref[...]
        dhf = dlf_ref[...] * jax.nn.sigmoid(-zz)
        dhf_ref[...] = dhf.astype(BF16)
        db_ref[...] = jnp.sum(dhf, axis=0, keepdims=True)

    full = pl.BlockSpec((S, LANES), lambda: (0, 0))
    vec = pl.BlockSpec((1, LANES), lambda: (0, 0))
    return pl.pallas_call(
        body, name=name, in_specs=[pl.BlockSpec((LANES, S), lambda: (0, 0)), full, full, vec],
        out_specs=[full, vec],
        out_shape=[jax.ShapeDtypeStruct((S, LANES), BF16), jax.ShapeDtypeStruct((1, LANES), F32)],
        scratch_shapes=[pltpu.VMEM((S, LANES), F32), pltpu.VMEM((S, LANES), F32)],
    )(dcft, drow, hf, b)


def _rope_tables(S):
    half = ROPE_DIMS // 2
    freqs = ROPE_THETA ** (-jnp.arange(half, dtype=F32) / half)
    ang = jnp.arange(S, dtype=F32)[:, None] * freqs[None, :]
    cos, sin = jnp.cos(ang), jnp.sin(ang)
    pad = HEAD_DIM - ROPE_DIMS
    c = jnp.concatenate([cos, cos, jnp.ones((S, pad), F32)], axis=1)
    s = jnp.concatenate([-sin, sin, jnp.zeros((S, pad), F32)], axis=1)
    return c, s


def _rope_swap(x):
    half = ROPE_DIMS // 2
    lane = lax.broadcasted_iota(jnp.int32, x.shape, 1)
    upper = jnp.where(lane < ROPE_DIMS, pltpu.roll(x, half, 1), 0.0)
    return jnp.where(lane < half, pltpu.roll(x, HEAD_DIM - half, 1), upper)


def _rope(x, c, s):
    return x * c + _rope_swap(x) * s


def _rope_t(dy, c, s):
    return dy * c + _rope_swap(dy * s)


def _split_dot(x, t):
    hi = x.astype(BF16)
    lo = (x - hi.astype(F32)).astype(BF16)
    return (jnp.dot(hi, t, preferred_element_type=F32) + jnp.dot(lo, t, preferred_element_type=F32))


_NT = (((1,), (1,)), ((), ()))
_TN = (((0,), (0,)), ((), ()))


def _dot_nt(a, b):
    return lax.dot_general(a, b, _NT, preferred_element_type=F32)


def _dot_tn(a, b):
    return lax.dot_general(a, b, _TN, preferred_element_type=F32)


def _blk(i):
    return pl.ds(pl.multiple_of(i * BLK, BLK), BLK)


def _delta(i, j):
    row = lax.broadcasted_iota(jnp.int32, (BLK, BLK), 0)
    col = lax.broadcasted_iota(jnp.int32, (BLK, BLK), 1)
    return (row - col) + (i - j) * BLK


def _dilated_mult(delta):
    c = jnp.zeros(delta.shape, F32)
    for window, dil in DILATED_PATTERNS:
        ok = (delta >= 0) & (delta <= window) & ((delta & (dil - 1)) == 0)
        c = c + jnp.where(ok, 1.0, 0.0)
    return c


def _sb_terms(z, mask, t_ex, run):
    t = jnp.log1p(jnp.exp(-jnp.abs(z)))
    lsig = jnp.minimum(z, 0.0) - t
    m = jnp.where(mask, -(jnp.maximum(z, 0.0) + t), 0.0)
    after = _split_dot(m, t_ex)
    a = jnp.where(mask, jnp.exp(lsig + after + run), 0.0)
    return a, m, lsig


def _attn_fwd(name, hq, layer_kind, n_heads, n_sb, cf=None, cft=None, rope_c=None, rope_s=None):
    S = hq.shape[0]
    D = n_heads * HEAD_DIM
    nq = S // BLK
    scale = HEAD_DIM ** -0.5
    even = layer_kind == "even"

    def body(*refs):
        if even:
            q_ref, k_ref, v_ref, cf_ref, cft_ref, o_ref, ob_ref, lse_ref, qs, ks, vs = refs
        else:
            q_ref, k_ref, v_ref, c_ref, s_ref, o_ref, ob_ref, lse_ref, qs, ks, vs = refs
        h = pl.program_id(0)
        if even:
            qs[...] = q_ref[...].astype(BF16)
            ks[...] = k_ref[...].astype(BF16)
        else:
            qs[...] = _rope(q_ref[...], c_ref[...], s_ref[...]).astype(BF16)
            ks[...] = _rope(k_ref[...], c_ref[...], s_ref[...]).astype(BF16)
        vs[...] = v_ref[...].astype(BF16)

        def softmax_head(hh):
            def qblock(i, carry):
                qi = qs[_blk(i), :]
                if even:
                    lane = lax.broadcasted_iota(jnp.int32, (BLK, LANES), 1)
                    cfq = jnp.sum(jnp.where(lane == hh, cf_ref[_blk(i), :], 0.0), axis=1, keepdims=True)

                def kblock(j, c):
                    m_run, l_run, acc = c
                    z = _dot_nt(qi, ks[_blk(j), :]) * scale
                    delta = _delta(i, j)
                    if even:
                        z = z + cfq - cft_ref[hh, :, _blk(j)]
                        ok = delta >= 0
                    else:
                        mult = _dilated_mult(delta)
                        ok = mult > 0.0
                    z = jnp.where(ok, z, NEG_INF)
                    m_new = jnp.maximum(m_run, jnp.max(z, axis=1, keepdims=True))
                    p = jnp.exp(z - m_new)
                    if not even:
                        p = p * mult
                    alpha = jnp.exp(m_run - m_new)
                    l_new = alpha * l_run + jnp.sum(p, axis=1, keepdims=True)
                    acc = alpha * acc + jnp.dot(p.astype(BF16), vs[_blk(j), :], preferred_element_type=F32)
                    return m_new, l_new, acc

                init = (jnp.full((BLK, 1), NEG_INF, F32), jnp.zeros((BLK, 1), F32), jnp.zeros((BLK, HEAD_DIM), F32))
                m_run, l_run, acc = lax.fori_loop(0, i + 1, kblock, init)
                o = acc / l_run
                o_ref[_blk(i), :] = o
                ob_ref[_blk(i), :] = o.astype(BF16)
                lse_ref[_blk(i), :] = jnp.broadcast_to(m_run + jnp.log(l_run), (BLK, HEAD_DIM))
                return carry

            lax.fori_loop(0, nq, qblock, 0)

        def sb_head():
            row = lax.broadcasted_iota(jnp.int32, (BLK, BLK), 0)
            col = lax.broadcasted_iota(jnp.int32, (BLK, BLK), 1)
            t_ex = jnp.where(row > col, 1.0, 0.0).astype(BF16)

            def qblock(i, carry):
                qi = qs[_blk(i), :]

                def kblock(jj, c):
                    run, acc = c
                    j = i - jj
                    z = _dot_nt(qi, ks[_blk(j), :]) * scale
                    a, m, _ = _sb_terms(z, _delta(i, j) > 0, t_ex, run)
                    acc = acc + jnp.dot(a.astype(BF16), vs[_blk(j), :], preferred_element_type=F32)
                    return run + jnp.sum(m, axis=1, keepdims=True), acc

                init = (jnp.zeros((BLK, 1), F32), jnp.zeros((BLK, HEAD_DIM), F32))
                _, acc = lax.fori_loop(0, i + 1, kblock, init)
                o_ref[_blk(i), :] = acc
                ob_ref[_blk(i), :] = acc.astype(BF16)
                lse_ref[_blk(i), :] = jnp.zeros((BLK, HEAD_DIM), F32)
                return carry

            lax.fori_loop(0, nq, qblock, 0)

        if even:
            @pl.when(h < n_sb)
            def _():
                sb_head()

            @pl.when(h >= n_sb)
            def _():
                softmax_head(h - n_sb)
        else:
            softmax_head(h)

    head = lambda off: pl.BlockSpec((S, HEAD_DIM), lambda h, off=off: (0, off + h))
    full = pl.BlockSpec((S, LANES), lambda h: (0, 0))
    if even:
        extra_specs = [full, pl.BlockSpec(cft.shape, lambda h: (0, 0, 0))]
        extra = (cf, cft)
    else:
        extra_specs = [full, full]
        extra = (rope_c, rope_s)
    blk_bytes = 8 * _nbytes((S, HEAD_DIM), F32)
    return pl.pallas_call(
        body, name=name, grid=(n_heads,),
        in_specs=[head(0), head(n_heads), head(2 * n_heads)] + extra_specs,
        out_specs=[head(0), head(0), head(0)],
        out_shape=[jax.ShapeDtypeStruct((S, D), F32), jax.ShapeDtypeStruct((S, D), BF16),
                   jax.ShapeDtypeStruct((S, D), F32)],
        scratch_shapes=[pltpu.VMEM((S, HEAD_DIM), BF16)] * 3,
        compiler_params=pltpu.CompilerParams(dimension_semantics=("arbitrary",),
                                             vmem_limit_bytes=_vmem_limit(blk_bytes, 3 * _nbytes((S, HEAD_DIM), BF16))),
    )(hq, hq, hq, *extra)


def _attn_bwd(name, hq, do, o, lse, layer_kind, n_heads, n_sb, cf=None, cft=None, rope_c=None, rope_s=None):
    S = hq.shape[0]
    D = n_heads * HEAD_DIM
    nq = S // BLK
    scale = HEAD_DIM ** -0.5
    even = layer_kind == "even"

    def body(*refs):
        if even:
            (q_ref, k_ref, v_ref, do_ref, o_ref, lse_ref, cf_ref, cft_ref,
             dh_ref, dcft_ref, drow_ref, qs, ks, vs, dos, dq_acc, dk_acc, dv_acc) = refs
        else:
            (q_ref, k_ref, v_ref, do_ref, o_ref, lse_ref, c_ref, s_ref,
             dh_ref, qs, ks, vs, dos, dq_acc, dk_acc, dv_acc) = refs
        h = pl.program_id(0)
        if even:
            qs[...] = q_ref[...].astype(BF16)
            ks[...] = k_ref[...].astype(BF16)

            @pl.when(h == 0)
            def _():
                dcft_ref[...] = jnp.zeros_like(dcft_ref)
                drow_ref[...] = jnp.zeros_like(drow_ref)
        else:
            qs[...] = _rope(q_ref[...], c_ref[...], s_ref[...]).astype(BF16)
            ks[...] = _rope(k_ref[...], c_ref[...], s_ref[...]).astype(BF16)
        vs[...] = v_ref[...].astype(BF16)
        dos[...] = do_ref[...].astype(BF16)
        dk_acc[...] = jnp.zeros_like(dk_acc)
        dv_acc[...] = jnp.zeros_like(dv_acc)

        def softmax_head(hh):
            def qblock(i, carry):
                qi = qs[_blk(i), :]
                doi = dos[_blk(i), :]
                dvec = jnp.sum(do_ref[_blk(i), :] * o_ref[_blk(i), :], axis=1, keepdims=True)
                lse_i = jnp.max(lse_ref[_blk(i), :], axis=1, keepdims=True)
                if even:
                    lane = lax.broadcasted_iota(jnp.int32, (BLK, LANES), 1)
                    cfq = jnp.sum(jnp.where(lane == hh, cf_ref[_blk(i), :], 0.0), axis=1, keepdims=True)

                def kblock(j, c):
                    dq, ds_rows = c
                    kj = ks[_blk(j), :]
                    z = _dot_nt(qi, kj) * scale
                    delta = _delta(i, j)
                    if even:
                        z = z + cfq - cft_ref[hh, :, _blk(j)]
                        ok = delta >= 0
                    else:
                        mult = _dilated_mult(delta)
                        ok = mult > 0.0
                    p = jnp.exp(jnp.where(ok, z, NEG_INF) - lse_i)
                    if not even:
                        p = p * mult
                    dp = _dot_nt(doi, vs[_blk(j), :])
                    ds = p * (dp - dvec)
                    dsb = (ds * scale).astype(BF16)
                    dk_acc[_blk(j), :] += _dot_tn(dsb, qi)
                    dv_acc[_blk(j), :] += _dot_tn(p.astype(BF16), doi)
                    if even:
                        dcft_ref[hh, :, _blk(j)] += -jnp.sum(ds, axis=0, keepdims=True)
                    return (dq + jnp.dot(dsb, kj, preferred_element_type=F32),
                            ds_rows + jnp.sum(ds, axis=1, keepdims=True))

                dq, ds_rows = lax.fori_loop(0, i + 1, kblock,
                                            (jnp.zeros((BLK, HEAD_DIM), F32), jnp.zeros((BLK, 1), F32)))
                dq_acc[_blk(i), :] = dq
                if even:
                    drow_ref[_blk(i), :] += jnp.where(lane == hh, ds_rows, 0.0)
                return carry

            lax.fori_loop(0, nq, qblock, 0)

        def sb_head():
            row = lax.broadcasted_iota(jnp.int32, (BLK, BLK), 0)
            col = lax.broadcasted_iota(jnp.int32, (BLK, BLK), 1)
            t_ex = jnp.where(row > col, 1.0, 0.0).astype(BF16)
            t_in = jnp.where(row >= col, 1.0, 0.0).astype(BF16)

            def qblock(i, carry):
                qi = qs[_blk(i), :]
                doi = dos[_blk(i), :]

                def e_total(jj, c):
                    run, tot = c
                    j = i - jj
                    z = _dot_nt(qi, ks[_blk(j), :]) * scale
                    a, m, _ = _sb_terms(z, _delta(i, j) > 0, t_ex, run)
                    e = _dot_nt(doi, vs[_blk(j), :]) * a
                    return run + jnp.sum(m, axis=1, keepdims=True), tot + jnp.sum(e, axis=1, keepdims=True)

                zero = jnp.zeros((BLK, 1), F32)
                _, e_tot = lax.fori_loop(0, i + 1, e_total, (zero, zero))

                def kblock(jj, c):
                    run, e_run, dq = c
                    j = i - jj
                    kj = ks[_blk(j), :]
                    z = _dot_nt(qi, kj) * scale
                    mask = _delta(i, j) > 0
                    a, m, lsig = _sb_terms(z, mask, t_ex, run)
                    sig = jnp.exp(lsig)
                    e = _dot_nt(doi, vs[_blk(j), :]) * a
                    e_before = e_tot - (_split_dot(e, t_in) + e_run)
                    dz = jnp.where(mask, e * (1.0 - sig) - sig * e_before, 0.0)
                    dzb = (dz * scale).astype(BF16)
                    dk_acc[_blk(j), :] += _dot_tn(dzb, qi)
                    dv_acc[_blk(j), :] += _dot_tn(a.astype(BF16), doi)
                    return (run + jnp.sum(m, axis=1, keepdims=True), e_run + jnp.sum(e, axis=1, keepdims=True),
                            dq + jnp.dot(dzb, kj, preferred_element_type=F32))

                _, _, dq = lax.fori_loop(0, i + 1, kblock, (zero, zero, jnp.zeros((BLK, HEAD_DIM), F32)))
                dq_acc[_blk(i), :] = dq
                return carry

            lax.fori_loop(0, nq, qblock, 0)

        if even:
            @pl.when(h < n_sb)
            def _():
                sb_head()

            @pl.when(h >= n_sb)
            def _():
                softmax_head(h - n_sb)

            dh_ref[0] = dq_acc[...].astype(BF16)
            dh_ref[1] = dk_acc[...].astype(BF16)
        else:
            softmax_head(h)
            dh_ref[0] = _rope_t(dq_acc[...], c_ref[...], s_ref[...]).astype(BF16)
            dh_ref[1] = _rope_t(dk_acc[...], c_ref[...], s_ref[...]).astype(BF16)
        dh_ref[2] = dv_acc[...].astype(BF16)

    head = lambda off: pl.BlockSpec((S, HEAD_DIM), lambda h, off=off: (0, off + h))
    full = pl.BlockSpec((S, LANES), lambda h: (0, 0))
    tfull = pl.BlockSpec((n_heads - n_sb, 1, S), lambda h: (0, 0, 0))
    dh_spec = pl.BlockSpec((3, S, HEAD_DIM), lambda h: (0, 0, h))
    dh_shape = jax.ShapeDtypeStruct((3, S, D), BF16)
    if even:
        extra_specs, extra = [full, tfull], (cf, cft)
        out_specs = [dh_spec, tfull, full]
        out_shape = [dh_shape, jax.ShapeDtypeStruct((n_heads - n_sb, 1, S), F32),
                     jax.ShapeDtypeStruct((S, LANES), F32)]
    else:
        extra_specs, extra = [full, full], (rope_c, rope_s)
        out_specs = [dh_spec]
        out_shape = [dh_shape]
    blk_bytes = 10 * _nbytes((S, HEAD_DIM), F32)
    scratch_bytes = 4 * _nbytes((S, HEAD_DIM), BF16) + 3 * _nbytes((S, HEAD_DIM), F32)
    return pl.pallas_call(
        body, name=name, grid=(n_heads,),
        in_specs=[head(0), head(n_heads), head(2 * n_heads), head(0), head(0), head(0)] + extra_specs,
        out_specs=out_specs, out_shape=out_shape,
        scratch_shapes=[pltpu.VMEM((S, HEAD_DIM), BF16)] * 4 + [pltpu.VMEM((S, HEAD_DIM), F32)] * 3,
        compiler_params=pltpu.CompilerParams(dimension_semantics=("arbitrary",),
                                             vmem_limit_bytes=_vmem_limit(blk_bytes, scratch_bytes)),
    )(hq, hq, hq, do, o, lse, *extra)


def _query_block(S):
    return min(512, S)


def _offsets(d, bq):
    row = jnp.arange(bq, dtype=jnp.int32)[:, None]
    col = jnp.arange(BLK, dtype=jnp.int32)[None, :]
    return d * BLK + row - col


def _causal_tables(bq, strict):
    r = bq // BLK
    tabs = []
    for d in range(-(r - 1), 1):
        delta = _offsets(d, bq)
        tabs.append(jnp.where((delta > 0) if strict else (delta >= 0), 1.0, 0.0))
    tabs.append(jnp.ones((bq, BLK), F32))
    return jnp.stack(tabs).astype(F32)


def _dilated_tables(bq):
    r = bq // BLK
    limit = sorted(w for w, _ in DILATED_PATTERNS)[-2]
    assert all(BLK % dil == 0 for _, dil in DILATED_PATTERNS)
    d_far = -(-(limit + BLK) // BLK)
    tabs = []
    for d in range(-(r - 1), d_far + 1):
        mult = _dilated_mult(_offsets(d, bq))
        tabs.append(jnp.where(mult > 0, jnp.log(jnp.maximum(mult, 1.0)), NEG_INF))
    return jnp.stack(tabs).astype(F32)


def _qblk(i, bq):
    return pl.ds(pl.multiple_of(i * bq, bq), bq)


def _sb_block(z, valid, t_ex, run):
    t = jnp.log1p(jnp.exp(-jnp.abs(z)))
    lsig = jnp.minimum(z, 0.0) - t
    m = -(jnp.maximum(z, 0.0) + t) * valid
    after = _split_dot(m, t_ex)
    a = jnp.exp(lsig + after + run) * valid
    return a, m, lsig


def _attn_fwd_wide(name, hq, layer_kind, n_heads, n_sb, cf=None, cft=None, rope_c=None, rope_s=None):
    S = hq.shape[0]
    D = n_heads * HEAD_DIM
    bq = _query_block(S)
    r = bq // BLK
    nq = S // bq
    scale = HEAD_DIM ** -0.5
    even = layer_kind == "even"
    if even:
        tabs = (jnp.where(_causal_tables(bq, False) > 0, 0.0, NEG_INF), _causal_tables(bq, True))
    else:
        tabs = (_dilated_tables(bq),)
    n_tab = tabs[0].shape[0]

    def body(*refs):
        if even:
            q_ref, k_ref, v_ref, cf_ref, cft_ref, bias_ref, valid_ref, o_ref, ob_ref, lse_ref, qs, ks, vs = refs
        else:
            q_ref, k_ref, v_ref, c_ref, s_ref, bias_ref, o_ref, ob_ref, lse_ref, qs, ks, vs = refs
        h = pl.program_id(0)
        if even:
            qs[...] = q_ref[...].astype(BF16)
            ks[...] = k_ref[...].astype(BF16)
        else:
            qs[...] = _rope(q_ref[...], c_ref[...], s_ref[...]).astype(BF16)
            ks[...] = _rope(k_ref[...], c_ref[...], s_ref[...]).astype(BF16)
        vs[...] = v_ref[...].astype(BF16)

        def softmax_head(hh):
            def qblock(i, carry):
                qi = qs[_qblk(i, bq), :]
                if even:
                    lane = lax.broadcasted_iota(jnp.int32, (bq, LANES), 1)
                    cfq = jnp.sum(jnp.where(lane == hh, cf_ref[_qblk(i, bq), :], 0.0), axis=1, keepdims=True)

                def kblock(j, c):
                    m_run, l_run, acc = c
                    z = _dot_nt(qi, ks[_blk(j), :]) * scale + bias_ref[jnp.minimum(r * i - j + (r - 1), n_tab - 1)]
                    if even:
                        z = z + (cfq - cft_ref[hh, :, _blk(j)])
                    m_new = jnp.maximum(m_run, jnp.max(z, axis=1, keepdims=True))
                    p = jnp.exp(z - m_new)
                    alpha = jnp.exp(m_run - m_new)
                    l_new = alpha * l_run + jnp.sum(p, axis=1, keepdims=True)
                    acc = alpha * acc + jnp.dot(p.astype(BF16), vs[_blk(j), :], preferred_element_type=F32)
                    return m_new, l_new, acc

                init = (jnp.full((bq, 1), NEG_INF, F32), jnp.zeros((bq, 1), F32), jnp.zeros((bq, HEAD_DIM), F32))
                m_run, l_run, acc = lax.fori_loop(0, r * (i + 1), kblock, init)
                o = acc / l_run
                o_ref[_qblk(i, bq), :] = o
                ob_ref[_qblk(i, bq), :] = o.astype(BF16)
                lse_ref[_qblk(i, bq), :] = jnp.broadcast_to(m_run + jnp.log(l_run), (bq, HEAD_DIM))
                return carry

            lax.fori_loop(0, nq, qblock, 0)

        def sb_head():
            row = lax.broadcasted_iota(jnp.int32, (BLK, BLK), 0)
            col = lax.broadcasted_iota(jnp.int32, (BLK, BLK), 1)
            t_ex = jnp.where(row > col, 1.0, 0.0).astype(BF16)

            def qblock(i, carry):
                qi = qs[_qblk(i, bq), :]

                def kblock(jj, c):
                    run, acc, rest = c
                    j = r * (i + 1) - 1 - jj
                    z = _dot_nt(qi, ks[_blk(j), :]) * scale
                    a, m, _ = _sb_block(z, valid_ref[jnp.minimum(r * i - j + (r - 1), r)], t_ex, run)
                    vj = vs[_blk(j), :]
                    hi = a.astype(BF16)
                    lo = (a - hi.astype(F32)).astype(BF16)
                    acc = acc + jnp.dot(hi, vj, preferred_element_type=F32)
                    rest = rest + jnp.dot(lo, vj, preferred_element_type=F32)
                    return run + jnp.sum(m, axis=1, keepdims=True), acc, rest

                zero = jnp.zeros((bq, HEAD_DIM), F32)
                _, acc, rest = lax.fori_loop(0, r * (i + 1), kblock, (jnp.zeros((bq, 1), F32), zero, zero))
                o_ref[_qblk(i, bq), :] = acc + rest
                ob_ref[_qblk(i, bq), :] = acc.astype(BF16)
                lse_ref[_qblk(i, bq), :] = jnp.zeros((bq, HEAD_DIM), F32)
                return carry

            lax.fori_loop(0, nq, qblock, 0)

        if even:
            @pl.when(h < n_sb)
            def _():
                sb_head()

            @pl.when(h >= n_sb)
            def _():
                softmax_head(h - n_sb)
        else:
            softmax_head(h)

    head = lambda off: pl.BlockSpec((S, HEAD_DIM), lambda h, off=off: (0, off + h))
    full = pl.BlockSpec((S, LANES), lambda h: (0, 0))
    tab_specs = [pl.BlockSpec(t.shape, lambda h: (0, 0, 0)) for t in tabs]
    if even:
        extra_specs = [full, pl.BlockSpec(cft.shape, lambda h: (0, 0, 0))] + tab_specs
        extra = (cf, cft) + tabs
    else:
        extra_specs = [full, full] + tab_specs
        extra = (rope_c, rope_s) + tabs
    blk_bytes = 8 * _nbytes((S, HEAD_DIM), F32) + sum(_nbytes(t.shape, F32) for t in tabs)
    return pl.pallas_call(
        body, name=name, grid=(n_heads,),
        in_specs=[head(0), head(n_heads), head(2 * n_heads)] + extra_specs,
        out_specs=[head(0), head(0), head(0)],
        out_shape=[jax.ShapeDtypeStruct((S, D), F32), jax.ShapeDtypeStruct((S, D), BF16),
                   jax.ShapeDtypeStruct((S, D), F32)],
        scratch_shapes=[pltpu.VMEM((S, HEAD_DIM), BF16)] * 3,
        compiler_params=pltpu.CompilerParams(dimension_semantics=("arbitrary",),
                                             vmem_limit_bytes=_vmem_limit(blk_bytes, 3 * _nbytes((S, HEAD_DIM), BF16))),
    )(hq, hq, hq, *extra)


def _attn_bwd_wide(name, hq, do, o, lse, layer_kind, n_heads, n_sb, cf=None, cft=None, rope_c=None, rope_s=None):
    S = hq.shape[0]
    D = n_heads * HEAD_DIM
    bq = _query_block(S)
    r = bq // BLK
    nq = S // bq
    scale = HEAD_DIM ** -0.5
    even = layer_kind == "even"
    if even:
        tabs = (jnp.where(_causal_tables(bq, False) > 0, 0.0, NEG_INF), _causal_tables(bq, True))
    else:
        tabs = (_dilated_tables(bq),)
    n_tab = tabs[0].shape[0]

    def body(*refs):
        if even:
            (q_ref, k_ref, v_ref, do_ref, o_ref, lse_ref, cf_ref, cft_ref, bias_ref, valid_ref,
             dh_ref, dcft_ref, drow_ref, qs, ks, vs, dos, dq_acc, dk_acc, dv_acc) = refs
        else:
            (q_ref, k_ref, v_ref, do_ref, o_ref, lse_ref, c_ref, s_ref, bias_ref,
             dh_ref, qs, ks, vs, dos, dq_acc, dk_acc, dv_acc) = refs
        h = pl.program_id(0)
        if even:
            qs[...] = q_ref[...].astype(BF16)
            ks[...] = k_ref[...].astype(BF16)

            @pl.when(h == 0)
            def _():
                dcft_ref[...] = jnp.zeros_like(dcft_ref)
                drow_ref[...] = jnp.zeros_like(drow_ref)
        else:
            qs[...] = _rope(q_ref[...], c_ref[...], s_ref[...]).astype(BF16)
            ks[...] = _rope(k_ref[...], c_ref[...], s_ref[...]).astype(BF16)
        vs[...] = v_ref[...].astype(BF16)
        dos[...] = do_ref[...].astype(BF16)
        dk_acc[...] = jnp.zeros_like(dk_acc)
        dv_acc[...] = jnp.zeros_like(dv_acc)

        def softmax_head(hh):
            def qblock(i, carry):
                qi = qs[_qblk(i, bq), :]
                doi = dos[_qblk(i, bq), :]
                dvec = jnp.sum(do_ref[_qblk(i, bq), :] * o_ref[_qblk(i, bq), :], axis=1, keepdims=True)
                lse_i = jnp.max(lse_ref[_qblk(i, bq), :], axis=1, keepdims=True)
                if even:
                    lane = lax.broadcasted_iota(jnp.int32, (bq, LANES), 1)
                    cfq = jnp.sum(jnp.where(lane == hh, cf_ref[_qblk(i, bq), :], 0.0), axis=1, keepdims=True)

                def kblock(j, c):
                    dq, ds_rows = c
                    kj = ks[_blk(j), :]
                    z = _dot_nt(qi, kj) * scale + bias_ref[jnp.minimum(r * i - j + (r - 1), n_tab - 1)]
                    if even:
                        z = z + (cfq - cft_ref[hh, :, _blk(j)])
                    p = jnp.exp(z - lse_i)
                    dp = _dot_nt(doi, vs[_blk(j), :])
                    ds = p * (dp - dvec)
                    dsb = (ds * scale).astype(BF16)
                    dk_acc[_blk(j), :] += _dot_tn(dsb, qi)
                    dv_acc[_blk(j), :] += _dot_tn(p.astype(BF16), doi)
                    if even:
                        dcft_ref[hh, :, _blk(j)] += -jnp.sum(ds, axis=0, keepdims=True)
                    return (dq + jnp.dot(dsb, kj, preferred_element_type=F32),
                            ds_rows + jnp.sum(ds, axis=1, keepdims=True))

                dq, ds_rows = lax.fori_loop(0, r * (i + 1), kblock,
                                            (jnp.zeros((bq, HEAD_DIM), F32), jnp.zeros((bq, 1), F32)))
                dq_acc[_qblk(i, bq), :] = dq
                if even:
                    drow_ref[_qblk(i, bq), :] += jnp.where(lane == hh, ds_rows, 0.0)
                return carry

            lax.fori_loop(0, nq, qblock, 0)

        def sb_head():
            row = lax.broadcasted_iota(jnp.int32, (BLK, BLK), 0)
            col = lax.broadcasted_iota(jnp.int32, (BLK, BLK), 1)
            t_ex = jnp.where(row > col, 1.0, 0.0).astype(BF16)
            t_in = jnp.where(row >= col, 1.0, 0.0).astype(BF16)

            def qblock(i, carry):
                qi = qs[_qblk(i, bq), :]
                doi = dos[_qblk(i, bq), :]
                nkb = r * (i + 1)
                e_tot = jnp.sum(doi.astype(F32) * o_ref[_qblk(i, bq), :], axis=1, keepdims=True)
                zero = jnp.zeros((bq, 1), F32)

                def kblock(jj, c):
                    run, e_run, dq = c
                    j = nkb - 1 - jj
                    kj = ks[_blk(j), :]
                    z = _dot_nt(qi, kj) * scale
                    valid = valid_ref[jnp.minimum(r * i - j + (r - 1), r)]
                    a, m, lsig = _sb_block(z, valid, t_ex, run)
                    sig = jnp.exp(lsig)
                    e = _dot_nt(doi, vs[_blk(j), :]) * a
                    e_before = e_tot - (_split_dot(e, t_in) + e_run)
                    dz = (e * (1.0 - sig) - sig * e_before) * valid
                    dzb = (dz * scale).astype(BF16)
                    dk_acc[_blk(j), :] += _dot_tn(dzb, qi)
                    dv_acc[_blk(j), :] += _dot_tn(a.astype(BF16), doi)
                    return (run + jnp.sum(m, axis=1, keepdims=True), e_run + jnp.sum(e, axis=1, keepdims=True),
                            dq + jnp.dot(dzb, kj, preferred_element_type=F32))

                _, _, dq = lax.fori_loop(0, nkb, kblock, (zero, zero, jnp.zeros((bq, HEAD_DIM), F32)))
                dq_acc[_qblk(i, bq), :] = dq
                return carry

            lax.fori_loop(0, nq, qblock, 0)

        if even:
            @pl.when(h < n_sb)
            def _():
                sb_head()

            @pl.when(h >= n_sb)
            def _():
                softmax_head(h - n_sb)

            dh_ref[0] = dq_acc[...].astype(BF16)
            dh_ref[1] = dk_acc[...].astype(BF16)
        else:
            softmax_head(h)
            dh_ref[0] = _rope_t(dq_acc[...], c_ref[...], s_ref[...]).astype(BF16)
            dh_ref[1] = _rope_t(dk_acc[...], c_ref[...], s_ref[...]).astype(BF16)
        dh_ref[2] = dv_acc[...].astype(BF16)

    head = lambda off: pl.BlockSpec((S, HEAD_DIM), lambda h, off=off: (0, off + h))
    full = pl.BlockSpec((S, LANES), lambda h: (0, 0))
    tfull = pl.BlockSpec((n_heads - n_sb, 1, S), lambda h: (0, 0, 0))
    tab_specs = [pl.BlockSpec(t.shape, lambda h: (0, 0, 0)) for t in tabs]
    dh_spec = pl.BlockSpec((3, S, HEAD_DIM), lambda h: (0, 0, h))
    dh_shape = jax.ShapeDtypeStruct((3, S, D), BF16)
    if even:
        extra_specs, extra = [full, tfull] + tab_specs, (cf, cft) + tabs
        out_specs = [dh_spec, tfull, full]
        out_shape = [dh_shape, jax.ShapeDtypeStruct((n_heads - n_sb, 1, S), F32),
                     jax.ShapeDtypeStruct((S, LANES), F32)]
    else:
        extra_specs, extra = [full, full] + tab_specs, (rope_c, rope_s) + tabs
        out_specs = [dh_spec]
        out_shape = [dh_shape]
    blk_bytes = 10 * _nbytes((S, HEAD_DIM), F32) + sum(_nbytes(t.shape, F32) for t in tabs)
    scratch_bytes = 4 * _nbytes((S, HEAD_DIM), BF16) + 3 * _nbytes((S, HEAD_DIM), F32)
    return pl.pallas_call(
        body, name=name, grid=(n_heads,),
        in_specs=[head(0), head(n_heads), head(2 * n_heads), head(0), head(0), head(0)] + extra_specs,
        out_specs=out_specs, out_shape=out_shape,
        scratch_shapes=[pltpu.VMEM((S, HEAD_DIM), BF16)] * 4 + [pltpu.VMEM((S, HEAD_DIM), F32)] * 3,
        compiler_params=pltpu.CompilerParams(dimension_semantics=("arbitrary",),
                                             vmem_limit_bytes=_vmem_limit(blk_bytes, scratch_bytes)),
    )(hq, hq, hq, do, o, lse, *extra)


def _ffn_fwd(tag, x, g, wgu, wd, fs):
    n = _rms_fwd(tag + "_norm", x, g)
    gu, h = _ffn_up(tag + "_gu", n, wgu, fs)
    y = _mm(tag + "_down", h, wd, "nn", F32, res=x, alpha=0.5)
    return y, (x, g, n, gu, h)


def _ffn_bwd(tag, dx, dxb, wgu, wd, fs, saved, after=None, emit=None):
    x, g, n, gu, h = saved
    dgu = _ffn_dact(tag + "_dgu", dxb, wd, gu, fs, 0.5, after=after)
    dwd = _mm(tag + "_dwd", h, dxb, "tn", BF16, alpha=0.5)
    dwgu = _mm(tag + "_dwgu", n, dgu, "tn", BF16)
    token = emit(dwgu, dwd) if emit else None
    dn = _mm(tag + "_dn", dgu, wgu, "nt", F32, after=token)
    dx_in, dxb_in, dg = _rms_bwd(tag + "_dnorm", dn, x, g, dx)
    return dx_in, dxb_in, dg, dwgu, dwd, token


def _mixer_fwd(tag, kind, x, g, wqkv, wout, n_heads, n_sb, wf=None, bf=None, rope=None):
    n = _rms_fwd(tag + "_norm", x, g)
    hq = _mm(tag + "_qkv", n, wqkv, "nn", F32)
    if kind == "even":
        hf = _mm(tag + "_gate", n, wf, "nn", F32)
        cf, cft = _gate_fwd(tag + "_cumgate", hf, bf)
        cft = cft[:n_heads - n_sb].reshape(n_heads - n_sb, 1, -1)
        o, ob, lse = _attn_fwd_wide(tag + "_attn", hq, kind, n_heads, n_sb, cf=cf, cft=cft)
    else:
        hf = cf = cft = None
        o, ob, lse = _attn_fwd_wide(tag + "_attn", hq, kind, n_heads, n_sb, rope_c=rope[0], rope_s=rope[1])
    y = _mm(tag + "_out", ob, wout, "nn", F32, res=x)
    return y, (x, g, n, hq, hf, cf, cft, o, ob, lse)


def _mixer_bwd(tag, kind, dx, dxb, wqkv, wout, n_heads, n_sb, saved, wf=None, bf=None, rope=None, after=None,
               emit=None):
    x, g, n, hq, hf, cf, cft, o, ob, lse = saved
    do = _mm(tag + "_do", dxb, wout, "nt", F32, after=after)
    dwout = _mm(tag + "_dwout", ob, dxb, "tn", BF16)
    if kind == "even":
        dh3, dcft, drow = _attn_bwd_wide(tag + "_dattn", hq, do, o, lse, kind, n_heads, n_sb, cf=cf, cft=cft)
    else:
        (dh3,) = _attn_bwd_wide(tag + "_dattn", hq, do, o, lse, kind, n_heads, n_sb, rope_c=rope[0], rope_s=rope[1])
    dwqkv = _mm(tag + "_dwqkv", n, dh3, "tn", BF16)
    dwf = db = dhf = None
    if kind == "even":
        n_fox = n_heads - n_sb
        dcft = jnp.pad(dcft.reshape(n_fox, -1), ((0, LANES - n_fox), (0, 0)))
        dhf, db = _gate_bwd(tag + "_dcumgate", dcft, drow, hf, bf)
        dwf = _mm(tag + "_dwf", n, dhf, "tn", BF16)
    token = emit(dwqkv, dwout, dwf) if emit else None
    dn = _mm(tag + "_dn", dh3, wqkv, "nt", F32, after=token)
    if kind == "even":
        dn = _mm(tag + "_dn_gate", dhf, wf, "nt", F32, res=dn)
    dx_in, dxb_in, dg = _rms_bwd(tag + "_dnorm", dn, x, g, dx)
    return dx_in, dxb_in, dg, dwqkv, dwout, dwf, db, token


def _local_step(x, target, w, fs, n_heads, n_sb, emit=None):
    S, D = x.shape
    rope = _rope_tables(S)
    kinds = ("even", "odd")
    saved = []
    h = x
    if callable(w):
        fetch, w = w, {"norm_g": w("norm_g", None), "final_g": w("final_g", None),
                       "wgu1": [None, None], "wd1": [None, None], "wgu2": [None, None], "wd2": [None, None]}
    else:
        fetch = None
    for l, kind in enumerate(kinds):
        ng = [w["norm_g"][l, i][None, :] for i in range(3)]
        if fetch:
            w["wgu1"][l], w["wd1"][l] = fetch(("ffn1", l), h)
        h, s1 = _ffn_fwd(f"l{l}_ffn1", h, ng[0], w["wgu1"][l], w["wd1"][l], fs)
        if fetch:
            w.update(fetch(("mix", l), h))
        if kind == "even":
            h, s2 = _mixer_fwd(f"l{l}_mix", kind, h, ng[1], w["wqkv_e"], w["wout_e"], n_heads, n_sb,
                               wf=w["wf"], bf=w["bf"])
        else:
            h, s2 = _mixer_fwd(f"l{l}_mix", kind, h, ng[1], w["wqkv_o"], w["wout_o"], n_heads, n_sb, rope=rope)
        if fetch:
            w["wgu2"][l], w["wd2"][l] = fetch(("ffn2", l), h)
        h, s3 = _ffn_fwd(f"l{l}_ffn2", h, ng[2], w["wgu2"][l], w["wd2"][l], fs)
        saved.append((s1, s2, s3))

    dx, dxb, dfinal, loss = _loss_head("loss_head", h, w["final_g"], target)
    grads = {"dfinal": dfinal, "dnorm": [[None] * 3 for _ in kinds],
             "dwgu1": [None, None], "dwd1": [None, None], "dwgu2": [None, None], "dwd2": [None, None]}
    hand = lambda block: (lambda *mats: emit(block, mats)) if emit else None
    token = None
    for l in (1, 0):
        kind = kinds[l]
        s1, s2, s3 = saved[l]
        dx, dxb, dg, grads["dwgu2"][l], grads["dwd2"][l], token = _ffn_bwd(
            f"l{l}_ffn2", dx, dxb, w["wgu2"][l], w["wd2"][l], fs, s3, after=token, emit=hand(("ffn2", l)))
        grads["dnorm"][l][2] = dg
        if kind == "even":
            dx, dxb, dg, grads["dwqkv_e"], grads["dwout_e"], grads["dwf"], grads["db"], token = _mixer_bwd(
                f"l{l}_mix", kind, dx, dxb, w["wqkv_e"], w["wout_e"], n_heads, n_sb, s2, wf=w["wf"], bf=w["bf"],
                after=token, emit=hand(("mix", l)))
        else:
            dx, dxb, dg, grads["dwqkv_o"], grads["dwout_o"], _, _, token = _mixer_bwd(
                f"l{l}_mix", kind, dx, dxb, w["wqkv_o"], w["wout_o"], n_heads, n_sb, s2, rope=rope,
                after=token, emit=hand(("mix", l)))
        grads["dnorm"][l][1] = dg
        dx, dxb, dg, grads["dwgu1"][l], grads["dwd1"][l], token = _ffn_bwd(
            f"l{l}_ffn1", dx, dxb, w["wgu1"][l], w["wd1"][l], fs, s1, after=token, emit=hand(("ffn1", l)))
        grads["dnorm"][l][0] = dg
    return loss, dx, grads


def _cast_into(name, shard, layer, chip, full_shape, place, full=None):
    R, C = shard.shape[-2:]
    tr = _row_tile(R, 512, step=16)
    if layer is None:
        in_spec = pl.BlockSpec((tr, C), lambda i, k: (i, 0))
    else:
        in_spec = pl.BlockSpec((None, tr, C), lambda i, k: (layer, i, 0))
    lead = (None,) * (len(full_shape) - 2)
    out_spec = pl.BlockSpec(lead + (tr, C), lambda i, k: place(i, k[0]))

    def body(*refs):
        k_ref, w_ref = refs[:2]
        o_ref = refs[-1]
        o_ref[...] = w_ref[...].astype(BF16)

    in_specs = [in_spec] + ([_ANY] if full is not None else [])
    args = (chip, shard) + ((full,) if full is not None else ())
    grid_spec = pltpu.PrefetchScalarGridSpec(num_scalar_prefetch=1, grid=(R // tr,), in_specs=in_specs, out_specs=out_spec)
    return pl.pallas_call(
        body, name=name, grid_spec=grid_spec, out_shape=jax.ShapeDtypeStruct(full_shape, BF16),
        input_output_aliases={2: 0} if full is not None else {},
        compiler_params=pltpu.CompilerParams(dimension_semantics=("arbitrary",)),
    )(*args)


def _region_shape(grad, kind):
    if kind == "lead":
        return grad.shape[1] // N_CORES, grad.shape[2]
    rows, cols = grad.shape
    if kind == "cols":
        return rows // N_CORES, cols // N_CHIPS
    return rows // (N_CHIPS * N_CORES), cols


def _region_add(name, grad, kind, landed, core):
    rh, cw = _region_shape(grad, kind)
    tr = _row_tile(rh, 256, step=16)
    nrb = rh // tr
    if kind == "cols":
        g_spec = pl.BlockSpec((tr, cw), lambda k, r, c: (c[0] * nrb + r, k))
    elif kind == "rows":
        g_spec = pl.BlockSpec((tr, cw), lambda k, r, c: ((N_CORES * k + c[0]) * nrb + r, 0))
    else:
        g_spec = pl.BlockSpec((None, tr, cw), lambda k, r, c: (k, c[0] * nrb + r, 0))
    l_spec = pl.BlockSpec((None, tr, cw), lambda k, r, c: (k, r, 0))

    def body(c_ref, g_ref, l_ref, o_ref):
        o_ref[...] = (g_ref[...].astype(F32) + l_ref[...].astype(F32)).astype(BF16)

    grid_spec = pltpu.PrefetchScalarGridSpec(
        num_scalar_prefetch=1, grid=(N_CHIPS, nrb), in_specs=[g_spec, l_spec], out_specs=l_spec)
    return pl.pallas_call(
        body, name=name, grid_spec=grid_spec, out_shape=jax.ShapeDtypeStruct(landed.shape, BF16),
        compiler_params=pltpu.CompilerParams(dimension_semantics=("parallel", "parallel"),
                                             vmem_limit_bytes=_vmem_limit(3 * _nbytes((tr, cw), F32))),
    )(core, grad, landed)


def _chip_sum(name, pair, landed, pos):
    _, rh, cw = pair.shape
    tr = _row_tile(rh, max(16, 2**20 // (cw * 4)), step=16)
    nrb = rh // tr

    def body(p_ref, own_ref, l_ref, o_ref):
        acc = own_ref[...].astype(F32)
        for s in range(N_CHIPS - 1):
            acc = acc + l_ref[s].astype(F32)
        o_ref[...] = acc

    grid_spec = pltpu.PrefetchScalarGridSpec(
        num_scalar_prefetch=1, grid=(nrb,),
        in_specs=[pl.BlockSpec((None, tr, cw), lambda r, p: (p[0], r, 0)),
                  pl.BlockSpec((N_CHIPS - 1, tr, cw), lambda r, p: (0, r, 0))],
        out_specs=pl.BlockSpec((tr, cw), lambda r, p: (p[1] * nrb + r, 0)))
    return pl.pallas_call(
        body, name=name, grid_spec=grid_spec, out_shape=jax.ShapeDtypeStruct((N_CORES * rh, cw), F32),
        compiler_params=pltpu.CompilerParams(dimension_semantics=("arbitrary",)),
    )(pos, pair, landed)


def _sum_leading(name, parts):
    n, R, C = parts.shape
    tr = _row_tile(R, max(8, (2**20 // (C * 4)) // 8 * 8))

    def body(p_ref, o_ref):
        acc = p_ref[0]
        for s in range(1, n):
            acc = acc + p_ref[s]
        o_ref[...] = acc

    return pl.pallas_call(
        body, name=name, grid=(R // tr,),
        in_specs=[pl.BlockSpec((n, tr, C), lambda i: (0, i, 0))],
        out_specs=pl.BlockSpec((tr, C), lambda i: (i, 0)),
        out_shape=jax.ShapeDtypeStruct((R, C), F32),
        compiler_params=pltpu.CompilerParams(dimension_semantics=("parallel",)),
    )(parts)


def _adamw(name, w, g, m, v):
    shape = w.shape
    to2d = lambda t: t.reshape(-1, shape[-1]) if t.ndim > 1 else t.reshape(1, -1)
    w2, g2, m2, v2 = (to2d(t) for t in (w, g, m, v))
    R, C = w2.shape
    tr = _row_tile(R, 256)

    def body(w_ref, g_ref, m_ref, v_ref, d_ref, nm_ref, nv_ref):
        gv = g_ref[...]
        nm = ADAM_B1 * m_ref[...] + (1.0 - ADAM_B1) * gv
        nv = ADAM_B2 * v_ref[...] + (1.0 - ADAM_B2) * (gv * gv)
        m_hat = nm / (1.0 - ADAM_B1 ** ADAM_STEP)
        v_hat = nv / (1.0 - ADAM_B2 ** ADAM_STEP)
        d_ref[...] = -ADAM_LR * (m_hat / (jnp.sqrt(v_hat) + ADAM_EPS) + ADAM_WD * w_ref[...])
        nm_ref[...] = nm
        nv_ref[...] = nv

    spec = pl.BlockSpec((tr, C), lambda i: (i, 0))
    sds = jax.ShapeDtypeStruct((R, C), F32)
    d, nm, nv = pl.pallas_call(
        body, name=name, grid=(R // tr,), in_specs=[spec] * 4, out_specs=[spec] * 3, out_shape=[sds] * 3,
        compiler_params=pltpu.CompilerParams(dimension_semantics=("parallel",),
                                             vmem_limit_bytes=_vmem_limit(7 * _nbytes((tr, C), F32))),
    )(w2, g2, m2, v2)
    return d.reshape(shape), nm.reshape(shape), nv.reshape(shape)


_ANY = pl.BlockSpec(memory_space=pl.ANY)


def _mesh_pos():
    return lax.axis_index("x"), lax.axis_index("y"), lax.axis_index("c")


def _other_chips(x, y):
    return [(1 - x, y), (x, 1 - y), (1 - x, 1 - y)]


def _gather_over_chips(name, fulls, views):
    n = len(views)
    nf = len(fulls)

    def body(*refs):
        full = refs[nf:2 * nf]
        ici_send, ici_recv, d2d_send, d2d_recv = refs[2 * nf:]
        x, y, c = _mesh_pos()
        chips = _other_chips(x, y)
        mine = 2 * x + y
        sibling = (x, y, 1 - c)

        def ici(a, p, k):
            i, view, _ = views[a]
            part = view(full[i], k, c)
            return pltpu.make_async_remote_copy(
                src_ref=part, dst_ref=part, send_sem=ici_send.at[a, p], recv_sem=ici_recv.at[a, p],
                device_id=(*chips[p], c), device_id_type=MESH)

        def d2d(a, p, h):
            i, view, _ = views[a]
            px, py = chips[p]
            part = view(full[i], 2 * px + py, h)
            return pltpu.make_async_remote_copy(
                src_ref=part, dst_ref=part, send_sem=d2d_send.at[a, p], recv_sem=d2d_recv.at[a, p],
                device_id=sibling, device_id_type=MESH)

        sends = [ici(a, p, mine) for a in range(n) for p in range(3)]
        for cp in sends:
            cp.start()
        passed = []
        for a in range(n):
            for p, (px, py) in enumerate(chips):
                ici(a, p, 2 * px + py).wait_recv()
                if views[a][2]:
                    fwd = d2d(a, p, c)
                    fwd.start()
                    passed.append(fwd)
        for a in range(n):
            if views[a][2]:
                for p in range(3):
                    d2d(a, p, 1 - c).wait_recv()
        for cp in sends + passed:
            cp.wait_send()

    return pl.pallas_call(
        body, name=name, in_specs=[_ANY] * nf, out_specs=[_ANY] * nf,
        out_shape=[jax.ShapeDtypeStruct(f.shape, f.dtype) for f in fulls],
        input_output_aliases={i: i for i in range(nf)},
        scratch_shapes=[pltpu.SemaphoreType.DMA((n, 3))] * 4,
        compiler_params=pltpu.CompilerParams(has_side_effects=True),
    )(*fulls)


_HBM = pl.BlockSpec(memory_space=pltpu.HBM)
_SEM = pl.BlockSpec(memory_space=pltpu.SEMAPHORE)


def _in_hbm(arrays):
    return [pltpu.with_memory_space_constraint(a, pltpu.HBM) for a in arrays]


def _gather_start(name, fulls, views):
    nf = len(fulls)
    ng = 1 + max(g for _, _, g in views)

    def body(*refs):
        full = refs[nf:2 * nf]
        send_sems, recv_sems = refs[2 * nf:2 * nf + ng], refs[2 * nf + ng:]
        x, y, c = _mesh_pos()
        chips = _other_chips(x, y)
        for i, view, g in views:
            part = view(full[i], 2 * x + y, c)
            for px, py in chips:
                pltpu.make_async_remote_copy(
                    src_ref=part, dst_ref=part, send_sem=send_sems[g], recv_sem=recv_sems[g],
                    device_id=(px, py, c), device_id_type=MESH).start()

    outs = pl.pallas_call(
        body, name=name, in_specs=[_HBM] * nf, out_specs=[_HBM] * nf + [_SEM] * (2 * ng),
        out_shape=[pltpu.HBM(f.shape, f.dtype) for f in fulls] + [pltpu.SemaphoreType.DMA(())] * (2 * ng),
        input_output_aliases={i: i for i in range(nf)},
        compiler_params=pltpu.CompilerParams(has_side_effects=pltpu.SideEffectType.DATAFLOW_SIDE_EFFECTING),
    )(*_in_hbm(fulls))
    return list(outs[:nf]), list(outs[nf:nf + ng]), list(outs[nf + ng:])


def _gather_wait(name, fulls, views, send_sem, recv_sem, after):
    nf = len(fulls)

    def body(*refs):
        send_ref, recv_ref = refs[nf], refs[nf + 1]
        full = refs[nf + 3:]
        x, y, c = _mesh_pos()
        copies = [pltpu.make_async_remote_copy(
            src_ref=view(full[i], 2 * x + y, c), dst_ref=view(full[i], 2 * px + py, c),
            send_sem=send_ref, recv_sem=recv_ref, device_id=(px, py, c), device_id_type=MESH)
            for i, view in views for px, py in _other_chips(x, y)]
        for cp in copies:
            cp.wait_send()
        for cp in copies:
            cp.wait_recv()

    outs = pl.pallas_call(
        body, name=name, in_specs=[_HBM] * nf + [_SEM, _SEM, _ANY], out_specs=[_HBM] * nf,
        out_shape=[pltpu.HBM(f.shape, f.dtype) for f in fulls],
        input_output_aliases={i: i for i in range(nf)},
        compiler_params=pltpu.CompilerParams(has_side_effects=pltpu.SideEffectType.DATAFLOW_SIDE_EFFECTING),
    )(*fulls, send_sem, recv_sem, after)
    return list(outs)


def _forward_to_sibling(name, fulls, views):
    n, nf = len(views), len(fulls)

    def body(*refs):
        full = refs[nf:2 * nf]
        send_sems, recv_sems = refs[2 * nf:]
        x, y, c = _mesh_pos()
        chips = _other_chips(x, y)

        def copy(a, p, h):
            i, view = views[a]
            px, py = chips[p]
            part = view(full[i], 2 * px + py, h)
            return pltpu.make_async_remote_copy(
                src_ref=part, dst_ref=part, send_sem=send_sems.at[a, p], recv_sem=recv_sems.at[a, p],
                device_id=(x, y, 1 - c), device_id_type=MESH)

        sends = [copy(a, p, c) for a in range(n) for p in range(3)]
        for cp in sends:
            cp.start()
        for a in range(n):
            for p in range(3):
                copy(a, p, 1 - c).wait_recv()
        for cp in sends:
            cp.wait_send()

    return pl.pallas_call(
        body, name=name, in_specs=[_ANY] * nf, out_specs=[_ANY] * nf,
        out_shape=[jax.ShapeDtypeStruct(f.shape, f.dtype) for f in fulls],
        input_output_aliases={i: i for i in range(nf)},
        scratch_shapes=[pltpu.SemaphoreType.DMA((n, 3))] * 2,
        compiler_params=pltpu.CompilerParams(has_side_effects=True),
    )(*fulls)


def _region_view(ref, kind, k, c):
    if kind == "lead":
        rh = ref.shape[1] // N_CORES
        return ref.at[k, pl.ds(pl.multiple_of(c * rh, 8), rh), :]
    rows, cols = ref.shape
    if kind == "cols":
        rh, cw = rows // N_CORES, cols // N_CHIPS
        return ref.at[pl.ds(pl.multiple_of(c * rh, 8), rh), pl.ds(k * cw, cw)]
    rh = rows // (N_CHIPS * N_CORES)
    return ref.at[pl.ds(pl.multiple_of((N_CORES * k + c) * rh, 8), rh), :]


def _send_to_sibling(name, grads, kinds):
    n = len(grads)
    shapes = [jax.ShapeDtypeStruct((N_CHIPS,) + _region_shape(g, kd), g.dtype) for g, kd in zip(grads, kinds)]

    def body(*refs):
        g_ref, land = refs[:n], refs[n:2 * n]
        send_sems, recv_sems = refs[2 * n:]
        x, y, c = _mesh_pos()
        copies = []
        for a in range(n):
            for k in range(N_CHIPS):
                cp = pltpu.make_async_remote_copy(
                    src_ref=_region_view(g_ref[a], kinds[a], k, 1 - c), dst_ref=land[a].at[k],
                    send_sem=send_sems.at[a, k], recv_sem=recv_sems.at[a, k],
                    device_id=(x, y, 1 - c), device_id_type=MESH)
                cp.start()
                copies.append(cp)
        for cp in copies:
            cp.wait_recv()
        for cp in copies:
            cp.wait_send()

    return pl.pallas_call(
        body, name=name, in_specs=[_ANY] * n, out_specs=[_ANY] * n, out_shape=shapes,
        scratch_shapes=[pltpu.SemaphoreType.DMA((n, N_CHIPS)), pltpu.SemaphoreType.DMA((n, N_CHIPS))],
        compiler_params=pltpu.CompilerParams(has_side_effects=True),
    )(*grads)


def _scatter_over_chips(name, pair_sums):
    n = len(pair_sums)

    def body(*refs):
        p_ref, land = refs[:n], refs[n:2 * n]
        send_sems, recv_sems = refs[2 * n:]
        x, y, c = _mesh_pos()
        chips = _other_chips(x, y)
        sends = []
        for a in range(n):
            for p, (px, py) in enumerate(chips):
                cp = pltpu.make_async_remote_copy(
                    src_ref=p_ref[a].at[2 * px + py], dst_ref=land[a].at[p], send_sem=send_sems.at[a, p],
                    recv_sem=recv_sems.at[a, p], device_id=(px, py, c), device_id_type=MESH)
                cp.start()
                sends.append(cp)
        for cp in sends:
            cp.wait_recv()
        for cp in sends:
            cp.wait_send()

    return pl.pallas_call(
        body, name=name, in_specs=[_ANY] * n, out_specs=[_ANY] * n,
        out_shape=[jax.ShapeDtypeStruct((N_CHIPS - 1,) + p.shape[1:], p.dtype) for p in pair_sums],
        scratch_shapes=[pltpu.SemaphoreType.DMA((n, 3)), pltpu.SemaphoreType.DMA((n, 3))],
        compiler_params=pltpu.CompilerParams(has_side_effects=True),
    )(*pair_sums)


def _scatter_start(name, pair_sums):
    n = len(pair_sums)
    lands = [lax.empty((N_CHIPS - 1,) + p.shape[1:], p.dtype) for p in pair_sums]

    def body(*refs):
        p_ref, land = refs[2 * n:3 * n], refs[3 * n:4 * n]
        send_sem, recv_sem, token = refs[4 * n:]
        x, y, c = _mesh_pos()
        for a in range(n):
            for p, (px, py) in enumerate(_other_chips(x, y)):
                pltpu.make_async_remote_copy(
                    src_ref=p_ref[a].at[2 * px + py], dst_ref=land[a].at[p], send_sem=send_sem, recv_sem=recv_sem,
                    device_id=(px, py, c), device_id_type=MESH).start()
        token[...] = jnp.zeros_like(token)

    outs = pl.pallas_call(
        body, name=name, in_specs=[_HBM] * (2 * n),
        out_specs=[_HBM] * (2 * n) + [_SEM, _SEM, pl.BlockSpec(memory_space=pltpu.VMEM)],
        out_shape=[pltpu.HBM(t.shape, t.dtype) for t in list(pair_sums) + lands]
        + [pltpu.SemaphoreType.DMA(()), pltpu.SemaphoreType.DMA(()), jax.ShapeDtypeStruct((8, LANES), F32)],
        input_output_aliases={i: i for i in range(2 * n)},
        compiler_params=pltpu.CompilerParams(has_side_effects=pltpu.SideEffectType.DATAFLOW_SIDE_EFFECTING),
    )(*_in_hbm(list(pair_sums) + lands))
    return list(outs[:n]), list(outs[n:2 * n]), outs[2 * n], outs[2 * n + 1], outs[2 * n + 2]


def _scatter_wait(name, pair_sums, lands, send_sem, recv_sem, after):
    n = len(pair_sums)

    def body(*refs):
        send_ref, recv_ref = refs[2 * n], refs[2 * n + 1]
        p_ref, land = refs[2 * n + 3:3 * n + 3], refs[3 * n + 3:]
        x, y, c = _mesh_pos()
        copies = [pltpu.make_async_remote_copy(
            src_ref=p_ref[a].at[2 * px + py], dst_ref=land[a].at[p], send_sem=send_ref, recv_sem=recv_ref,
            device_id=(px, py, c), device_id_type=MESH)
            for a in range(n) for p, (px, py) in enumerate(_other_chips(x, y))]
        for cp in copies:
            cp.wait_send()
        for cp in copies:
            cp.wait_recv()

    outs = pl.pallas_call(
        body, name=name, in_specs=[_HBM] * (2 * n) + [_SEM, _SEM, _ANY], out_specs=[_HBM] * (2 * n),
        out_shape=[pltpu.HBM(t.shape, t.dtype) for t in list(pair_sums) + list(lands)],
        input_output_aliases={i: i for i in range(2 * n)},
        compiler_params=pltpu.CompilerParams(has_side_effects=pltpu.SideEffectType.DATAFLOW_SIDE_EFFECTING),
    )(*pair_sums, *lands, send_sem, recv_sem, after)
    return list(outs[:n]), list(outs[n:])


def _swap_halves(name, shards):
    n = len(shards)

    def body(*refs):
        out = refs[n:2 * n]
        send_sems, recv_sems = refs[2 * n:]
        x, y, c = _mesh_pos()
        sends = []
        for a in range(n):
            rh = out[a].shape[0] // N_CORES
            mine = out[a].at[pl.ds(pl.multiple_of(c * rh, 8), rh), :]
            cp = pltpu.make_async_remote_copy(
                src_ref=mine, dst_ref=mine, send_sem=send_sems.at[a], recv_sem=recv_sems.at[a],
                device_id=(x, y, 1 - c), device_id_type=MESH)
            cp.start()
            sends.append(cp)
        for a in range(n):
            rh = out[a].shape[0] // N_CORES
            theirs = out[a].at[pl.ds(pl.multiple_of((1 - c) * rh, 8), rh), :]
            pltpu.make_async_remote_copy(
                src_ref=theirs, dst_ref=theirs, send_sem=send_sems.at[a], recv_sem=recv_sems.at[a],
                device_id=(x, y, 1 - c), device_id_type=MESH).wait_recv()
        for cp in sends:
            cp.wait_send()

    return pl.pallas_call(
        body, name=name, in_specs=[_ANY] * n, out_specs=[_ANY] * n,
        out_shape=[jax.ShapeDtypeStruct(s.shape, s.dtype) for s in shards],
        input_output_aliases={i: i for i in range(n)},
        scratch_shapes=[pltpu.SemaphoreType.DMA((n,)), pltpu.SemaphoreType.DMA((n,))],
        compiler_params=pltpu.CompilerParams(has_side_effects=True),
    )(*shards)


def _gather_all_devices(name, block):
    R, C = block.shape
    ndev = N_CHIPS * N_CORES

    def body(b_ref, out_ref, send_sems, recv_sems, local_sem):
        x, y, c = _mesh_pos()
        mine = 4 * x + 2 * y + c
        own = pltpu.make_async_copy(b_ref, out_ref.at[mine], local_sem)
        own.start()
        sends = []
        for mask in range(1, ndev):
            fx, fy, fc = (mask >> 2) & 1, (mask >> 1) & 1, mask & 1
            px, py, pc = x ^ fx, y ^ fy, c ^ fc
            cp = pltpu.make_async_remote_copy(
                src_ref=b_ref, dst_ref=out_ref.at[mine], send_sem=send_sems.at[mask - 1],
                recv_sem=recv_sems.at[mask - 1], device_id=(px, py, pc), device_id_type=MESH)
            cp.start()
            sends.append(cp)
        for mask in range(1, ndev):
            fx, fy, fc = (mask >> 2) & 1, (mask >> 1) & 1, mask & 1
            px, py, pc = x ^ fx, y ^ fy, c ^ fc
            pltpu.make_async_remote_copy(
                src_ref=b_ref, dst_ref=out_ref.at[4 * px + 2 * py + pc], send_sem=send_sems.at[mask - 1],
                recv_sem=recv_sems.at[mask - 1], device_id=(px, py, pc), device_id_type=MESH).wait_recv()
        for cp in sends:
            cp.wait_send()
        own.wait()

    return pl.pallas_call(
        body, name=name, in_specs=[_ANY], out_specs=_ANY,
        out_shape=jax.ShapeDtypeStruct((ndev, R, C), F32),
        scratch_shapes=[pltpu.SemaphoreType.DMA((ndev - 1,)), pltpu.SemaphoreType.DMA((ndev - 1,)),
                        pltpu.SemaphoreType.DMA(())],
        compiler_params=pltpu.CompilerParams(has_side_effects=True),
    )(block)


def _reduce_scatter(grads, kinds, pos):
    landed = _send_to_sibling("rs_pair_send", grads, kinds)
    pair = [_region_add(f"rs_pair_add_{a}", g, kd, ld, pos[1:]) for a, (g, kd, ld) in enumerate(zip(grads, kinds, landed))]
    parts = _scatter_over_chips("rs_chip_send", pair)
    shards = [_chip_sum(f"rs_chip_add_{a}", p, ld, pos) for a, (p, ld) in enumerate(zip(pair, parts))]
    return _swap_halves("rs_swap_halves", shards)


def kernel(x, norm_g, ffn1_w_gate, ffn1_w_up, ffn1_w_down, ffn2_w_gate, ffn2_w_up, ffn2_w_down, even_w_in, even_b_forget, even_w_out, odd_w_qkv, odd_w_out, final_norm_g, loss_target, m_norm_g, m_ffn1_w_gate, m_ffn1_w_up, m_ffn1_w_down, m_ffn2_w_gate, m_ffn2_w_up, m_ffn2_w_down, m_even_w_in, m_even_b_forget, m_even_w_out, m_odd_w_qkv, m_odd_w_out, m_final_norm_g, v_norm_g, v_ffn1_w_gate, v_ffn1_w_up, v_ffn1_w_down, v_ffn2_w_gate, v_ffn2_w_up, v_ffn2_w_down, v_even_w_in, v_even_b_forget, v_even_w_out, v_odd_w_qkv, v_odd_w_out, v_final_norm_g):
    _, S, D = x.shape
    L = norm_g.shape[0]
    assert L == 2 and even_w_in.shape[0] == 1 and odd_w_qkv.shape[0] == 1
    fs = ffn1_w_gate.shape[2]
    F = N_CHIPS * fs
    wc = even_w_in.shape[2]
    n_heads = D // HEAD_DIM
    n_fox = N_CHIPS * wc - 3 * D
    n_sb = n_heads - n_fox
    qs = odd_w_qkv.shape[2]
    os_ = even_w_out.shape[1]
    ns = norm_g.shape[2]
    xi, yi, ci = _mesh_pos()
    chip = 2 * xi + yi

    pos = jnp.stack([chip, ci]).astype(jnp.int32)
    kchip = pos[:1]
    lane = lambda start, size: pl.ds(pl.multiple_of(start, LANES), size)
    sub = lambda start, size: pl.ds(pl.multiple_of(start, 16), size)
    gate_view = lambda r, k, h: r.at[sub(h * (D // 2), D // 2), lane(k * 2 * fs, fs)]
    up_view = lambda r, k, h: r.at[sub(h * (D // 2), D // 2), lane(k * 2 * fs + fs, fs)]
    down_view = lambda r, k, h: r.at[sub(k * fs + h * (fs // 2), fs // 2), :]
    out_view = lambda r, k, h: r.at[sub(k * os_ + h * (os_ // 2), os_ // 2), :]
    tr_d = _row_tile(fs, 512, step=16)
    tr_o = _row_tile(os_, 512, step=16)
    ffn_w = {"ffn1": (ffn1_w_gate, ffn1_w_up, ffn1_w_down), "ffn2": (ffn2_w_gate, ffn2_w_up, ffn2_w_down)}
    win_view = lambda r, k, h: r.at[k, sub(h * (D // 2), D // 2), :]
    qkv_view = lambda r, k, h: r.at[sub(h * (D // 2), D // 2), lane(k * qs, qs)]
    fulls, views, groups = [], [], {}
    for l in range(L):
        for blk in ("ffn1", "mix", "ffn2"):
            o, v0 = len(fulls), len(views)
            if blk == "mix" and l == 0:
                fulls += [_cast_into("cast_win", even_w_in, 0, kchip, (N_CHIPS, D, wc), lambda i, k: (k, i, 0)),
                          _cast_into("cast_wout_e", even_w_out, 0, kchip, (D, D), lambda i, k: (k * (os_ // tr_o) + i, 0))]
                views += [(o, win_view), (o + 1, out_view)]
            elif blk == "mix":
                fulls += [_cast_into("cast_wqkv_o", odd_w_qkv, 0, kchip, (D, N_CHIPS * qs), lambda i, k: (i, k)),
                          _cast_into("cast_wout_o", odd_w_out, 0, kchip, (D, D), lambda i, k: (k * (os_ // tr_o) + i, 0))]
                views += [(o, qkv_view), (o + 1, out_view)]
            else:
                wg, wu, wd = ffn_w[blk]
                t = f"cast_{blk}_l{l}"
                gu = _cast_into(t + "_gate", wg, l, kchip, (D, 2 * F), lambda i, k: (i, 2 * k))
                gu = _cast_into(t + "_up", wu, l, kchip, (D, 2 * F), lambda i, k: (i, 2 * k + 1), full=gu)
                dn = _cast_into(t + "_down", wd, l, kchip, (F, D), lambda i, k: (k * (fs // tr_d) + i, 0))
                fulls += [gu, dn]
                views += [(o, gate_view), (o, up_view), (o + 1, down_view)]
            gid = len(groups)
            views[v0:] = [(i, view, gid) for i, view in views[v0:]]
            groups[(blk, l)] = (gid, list(range(o, len(fulls))), list(range(v0, len(views))))
    norm_own = lax.dynamic_update_slice(jnp.zeros((L, 3, N_CHIPS * ns), F32), norm_g, (0, 0, chip * ns))
    (norm_full,) = _gather_over_chips("gather_norm", [norm_own], [(0, lambda r, k, h: r.at[:, :, lane(k * ns, ns)], False)])
    started, send_sems, recv_sems = _gather_start("gather_start", fulls, views)

    def fetch(block, after):
        if block == "norm_g":
            return norm_full
        if block == "final_g":
            return final_norm_g[None, :]
        gid, arrays, rows = groups[block]
        tag = f"{block[0]}_l{block[1]}"
        local = [(views[a][0] - arrays[0], views[a][1]) for a in rows]
        got = _gather_wait("gather_wait_" + tag, [started[i] for i in arrays], local, send_sems[gid], recv_sems[gid],
                           x if after is None else after)
        got = _forward_to_sibling("gather_pass_" + tag, got, local)
        if block[0] != "mix":
            return got
        if block[1] == 1:
            return {"wqkv_o": got[0], "wout_o": got[1]}
        win = jnp.concatenate([got[0][k] for k in range(N_CHIPS)], axis=1)
        return {"wqkv_e": win[:, :3 * D], "wf": jnp.pad(win[:, 3 * D:], ((0, 0), (0, LANES - n_fox))),
                "bf": jnp.pad(even_b_forget, ((0, 0), (0, LANES - n_fox))), "wout_e": got[1]}

    pending, shards = [], {}

    def finish(after):
        block, pair, lands, ssem, rsem = pending.pop(0)
        tag = f"{block[0]}_l{block[1]}"
        pair, lands = _scatter_wait("rs_chip_wait_" + tag, pair, lands, ssem, rsem, after)
        shards[block] = [_chip_sum(f"rs_chip_add_{tag}_{a}", p, ld, pos) for a, (p, ld) in enumerate(zip(pair, lands))]

    def emit(block, mats):
        blk, l = block
        tag = f"{blk}_l{l}"
        kinds = ["cols", "rows"]
        if blk == "mix" and l == 0:
            dwqkv, dwout, dwf = mats
            dwin = jnp.concatenate([dwqkv, dwf[:, :n_fox]], axis=1)
            mats = [jnp.stack([dwin[:, k * wc:(k + 1) * wc] for k in range(N_CHIPS)]), dwout]
            kinds = ["lead", "rows"]
        elif blk == "mix":
            mats = list(mats[:2])
        else:
            mats = list(mats)
        landed = _send_to_sibling("rs_pair_send_" + tag, mats, kinds)
        pair = [_region_add(f"rs_pair_add_{tag}_{a}", m, kd, ld, pos[1:])
                for a, (m, kd, ld) in enumerate(zip(mats, kinds, landed))]
        pair, lands, ssem, rsem, token = _scatter_start("rs_chip_start_" + tag, pair)
        if pending:
            finish(token)
        pending.append((block, pair, lands, ssem, rsem))
        return token

    loss_vec, grad_x, g = _local_step(x[0], loss_target[0], fetch, fs, n_heads, n_sb, emit=emit)
    finish(grad_x)
    order = [(b, l) for b in ("ffn1", "ffn2", "mix") for l in range(L)]
    red = _swap_halves("rs_swap_halves", [s for b in order for s in shards[b]])
    red = {b: red[2 * i:2 * i + 2] for i, b in enumerate(order)}
    gu1, gd1 = [red[("ffn1", l)][0] for l in range(L)], [red[("ffn1", l)][1] for l in range(L)]
    gu2, gd2 = [red[("ffn2", l)][0] for l in range(L)], [red[("ffn2", l)][1] for l in range(L)]
    (g_win, g_wout_e), (g_qkv_o, g_wout_o) = red[("mix", 0)], red[("mix", 1)]

    small_rows = [g["dnorm"][l][i] for l in range(L) for i in range(3)] + [
        g["dfinal"], jnp.pad(g["db"], ((0, 0), (0, D - LANES))), jnp.pad(loss_vec, ((0, 0), (0, D - LANES)))]
    small = jnp.concatenate(small_rows + [jnp.zeros((16 - len(small_rows), D), F32)], axis=0)
    small_sum = _sum_leading("small_sum", _gather_all_devices("small_gather", small))
    loss = small_sum[3 * L + 2, 0]
    g_norm = lax.dynamic_slice_in_dim(small_sum[:3 * L].reshape(L, 3, D), chip * ns, ns, axis=2)
    g_final = small_sum[3 * L]
    g_bf = small_sum[3 * L + 1, :n_fox][None, :]

    grads = [
        g_norm,
        jnp.stack([t[:, :fs] for t in gu1]), jnp.stack([t[:, fs:] for t in gu1]), jnp.stack(gd1),
        jnp.stack([t[:, :fs] for t in gu2]), jnp.stack([t[:, fs:] for t in gu2]), jnp.stack(gd2),
        g_win[None], g_bf, g_wout_e[None], g_qkv_o[None], g_wout_o[None], g_final]
    weights = [norm_g, ffn1_w_gate, ffn1_w_up, ffn1_w_down, ffn2_w_gate, ffn2_w_up, ffn2_w_down,
               even_w_in, even_b_forget, even_w_out, odd_w_qkv, odd_w_out, final_norm_g]
    ms = [m_norm_g, m_ffn1_w_gate, m_ffn1_w_up, m_ffn1_w_down, m_ffn2_w_gate, m_ffn2_w_up, m_ffn2_w_down,
          m_even_w_in, m_even_b_forget, m_even_w_out, m_odd_w_qkv, m_odd_w_out, m_final_norm_g]
    vs = [v_norm_g, v_ffn1_w_gate, v_ffn1_w_up, v_ffn1_w_down, v_ffn2_w_gate, v_ffn2_w_up, v_ffn2_w_down,
          v_even_w_in, v_even_b_forget, v_even_w_out, v_odd_w_qkv, v_odd_w_out, v_final_norm_g]
    deltas, new_ms, new_vs = [], [], []
    for i, (wt, gt, mt, vt) in enumerate(zip(weights, grads, ms, vs)):
        d, nm, nv = _adamw(f"adamw_{i}", wt, gt, mt, vt)
        deltas.append(d)
        new_ms.append(nm)
        new_vs.append(nv)
    return (loss, grad_x[None], *grads, *deltas, *new_ms, *new_vs)
```

```python
import functools
import math

import jax
import jax.numpy as jnp
from jax import lax
from jax.experimental import pallas as pl
from jax.experimental.pallas import tpu as pltpu

F32 = jnp.float32
BF16 = jnp.bfloat16

HEAD_DIM = 128
ROPE_DIMS = 32
ROPE_THETA = 500000.0
DILATED_PATTERNS = ((128, 1), (512, 4), (2048, 16))
RMS_EPS = 1e-6
NEG_INF = -1e30
ADAM_LR = 0.001
ADAM_B1 = 0.9
ADAM_B2 = 0.999
ADAM_EPS = 1e-08
ADAM_WD = 0.01
ADAM_STEP = 10

N_CHIPS = 4
N_CORES = 2
LANES = 128
BLK = 256
VMEM_BYTES_V7X = 64 * 2**20
MESH = pl.DeviceIdType.MESH


def _vmem_limit(block_bytes, scratch_bytes=0):
    need = 2 * block_bytes + scratch_bytes + 12 * 2**20
    return int(min(need, VMEM_BYTES_V7X - 6 * 2**20))


def _nbytes(shape, dtype):
    return math.prod(shape) * jnp.dtype(dtype).itemsize


def _tile(dim, target):
    best = None
    for t in range(LANES, min(dim, target) + 1, LANES):
        if dim % t == 0:
            best = t
    assert best is not None, (dim, target)
    return best


def _row_tile(rows, target, step=8):
    if rows <= target:
        return rows
    best = None
    for t in range(step, target + 1, step):
        if rows % t == 0:
            best = t
    assert best is not None, (rows, target)
    return best


def _mm(name, a, b, mode, out_dtype, res=None, alpha=1.0, after=None, tm_target=1024, tn_target=1536, tk_target=2048):
    a3 = a.ndim == 3
    b3 = b.ndim == 3
    if mode == "nn":
        assert not a3 and not b3
        (M, K), (K2, N) = a.shape, b.shape
    elif mode == "nt":
        assert not b3
        if a3:
            P, M, Kp = a.shape
            K = P * Kp
        else:
            M, K = a.shape
        N, K2 = b.shape
    else:
        assert mode == "tn" and not a3
        K, M = a.shape
        if b3:
            P, K2, Np = b.shape
            N = P * Np
        else:
            K2, N = b.shape
    assert K == K2, (name, a.shape, b.shape)
    tm = _tile(M, tm_target)
    tn = _tile(Np if b3 else N, tn_target)
    tk = _tile(Kp if a3 else K, tk_target)
    nk = K // tk
    grid = (M // tm, N // tn, nk)

    if mode == "nn":
        a_spec = pl.BlockSpec((tm, tk), lambda i, j, k: (i, k))
        b_spec = pl.BlockSpec((tk, tn), lambda i, j, k: (k, j))
        dims = (((1,), (0,)), ((), ()))
    elif mode == "nt":
        if a3:
            nkp = Kp // tk
            a_spec = pl.BlockSpec((None, tm, tk), lambda i, j, k: (k // nkp, i, k % nkp))
        else:
            a_spec = pl.BlockSpec((tm, tk), lambda i, j, k: (i, k))
        b_spec = pl.BlockSpec((tn, tk), lambda i, j, k: (j, k))
        dims = (((1,), (1,)), ((), ()))
    else:
        a_spec = pl.BlockSpec((tk, tm), lambda i, j, k: (k, i))
        if b3:
            njp = Np // tn
            b_spec = pl.BlockSpec((None, tk, tn), lambda i, j, k: (j // njp, k, j % njp))
        else:
            b_spec = pl.BlockSpec((tk, tn), lambda i, j, k: (k, j))
        dims = (((0,), (0,)), ((), ()))
    o_spec = pl.BlockSpec((tm, tn), lambda i, j, k: (i, j))
    has_res = res is not None

    def finish(y, r_ref, o_ref):
        if alpha != 1.0:
            y = y * alpha
        if has_res:
            y = r_ref[...] + y
        o_ref[...] = y.astype(o_ref.dtype)

    n_in = 2 + has_res + (after is not None)

    def body(*refs):
        a_ref, b_ref = refs[:2]
        r_ref = refs[2] if has_res else None
        o_ref = refs[n_in]
        part = lax.dot_general(a_ref[...], b_ref[...], dims, preferred_element_type=F32)
        if nk == 1:
            finish(part, r_ref, o_ref)
            return
        acc_ref = refs[-1]
        k = pl.program_id(2)

        @pl.when(k == 0)
        def _():
            acc_ref[...] = part

        @pl.when(k > 0)
        def _():
            acc_ref[...] += part

        @pl.when(k == nk - 1)
        def _():
            finish(acc_ref[...], r_ref, o_ref)

    in_specs = [a_spec, b_spec] + ([o_spec] if has_res else []) + ([_ANY] if after is not None else [])
    args = (a, b) + ((res,) if has_res else ()) + ((after,) if after is not None else ())
    blk = (_nbytes((tm, tk), a.dtype) + _nbytes((tk, tn), b.dtype) + _nbytes((tm, tn), out_dtype)
           + (_nbytes((tm, tn), F32) if has_res else 0))
    return pl.pallas_call(
        body, name=name, grid=grid, in_specs=in_specs, out_specs=o_spec,
        out_shape=jax.ShapeDtypeStruct((M, N), out_dtype),
        scratch_shapes=[pltpu.VMEM((tm, tn), F32)] if nk > 1 else [],
        compiler_params=pltpu.CompilerParams(
            dimension_semantics=("parallel", "parallel", "arbitrary"),
            vmem_limit_bytes=_vmem_limit(blk, 2 * _nbytes((tm, tn), F32))),
    )(*args)


def _rms_fwd(name, x, g):
    S, D = x.shape
    tr = _row_tile(S, 256)

    def body(x_ref, g_ref, n_ref):
        xv = x_ref[...]
        r = lax.rsqrt(jnp.mean(xv * xv, axis=-1, keepdims=True) + RMS_EPS)
        n_ref[...] = (xv * r * g_ref[...]).astype(BF16)

    return pl.pallas_call(
        body, name=name, grid=(S // tr,),
        in_specs=[pl.BlockSpec((tr, D), lambda i: (i, 0)), pl.BlockSpec((1, D), lambda i: (0, 0))],
        out_specs=pl.BlockSpec((tr, D), lambda i: (i, 0)),
        out_shape=jax.ShapeDtypeStruct((S, D), BF16),
        compiler_params=pltpu.CompilerParams(dimension_semantics=("parallel",)),
    )(x, g)


def _rms_bwd(name, dn, x, g, dres):
    S, D = x.shape
    tr = _row_tile(S, 256)

    def body(dn_ref, x_ref, g_ref, dres_ref, dx_ref, dxb_ref, dg_ref):
        i = pl.program_id(0)
        xv = x_ref[...]
        dnv = dn_ref[...]
        r = lax.rsqrt(jnp.mean(xv * xv, axis=-1, keepdims=True) + RMS_EPS)
        u = dnv * g_ref[...]
        dot = jnp.mean(u * xv, axis=-1, keepdims=True)
        dx = dres_ref[...] + r * u - xv * (r * r * r * dot)
        dx_ref[...] = dx
        dxb_ref[...] = dx.astype(BF16)

        @pl.when(i == 0)
        def _():
            dg_ref[...] = jnp.zeros_like(dg_ref)

        dg_ref[...] += jnp.sum(dnv * xv * r, axis=0, keepdims=True)

    row = pl.BlockSpec((tr, D), lambda i: (i, 0))
    vec = pl.BlockSpec((1, D), lambda i: (0, 0))
    return pl.pallas_call(
        body, name=name, grid=(S // tr,),
        in_specs=[row, row, vec, row], out_specs=[row, row, vec],
        out_shape=[jax.ShapeDtypeStruct((S, D), F32), jax.ShapeDtypeStruct((S, D), BF16),
                   jax.ShapeDtypeStruct((1, D), F32)],
        compiler_params=pltpu.CompilerParams(dimension_semantics=("arbitrary",)),
    )(dn, x, g, dres)


def _loss_head(name, x, g, target):
    S, D = x.shape
    tr = _row_tile(S, 256)

    def body(x_ref, g_ref, t_ref, dx_ref, dxb_ref, dg_ref, loss_ref):
        i = pl.program_id(0)
        xv = x_ref[...]
        gv = g_ref[...]
        r = lax.rsqrt(jnp.mean(xv * xv, axis=-1, keepdims=True) + RMS_EPS)
        diff = xv * r * gv - t_ref[...]
        part = 0.5 * jnp.sum(jnp.mean(diff * diff, axis=-1, keepdims=True), axis=0, keepdims=True)
        dy = diff * (1.0 / D)
        u = dy * gv
        dot = jnp.mean(u * xv, axis=-1, keepdims=True)
        dx = r * u - xv * (r * r * r * dot)
        dx_ref[...] = dx
        dxb_ref[...] = dx.astype(BF16)

        @pl.when(i == 0)
        def _():
            dg_ref[...] = jnp.zeros_like(dg_ref)
            loss_ref[...] = jnp.zeros_like(loss_ref)

        dg_ref[...] += jnp.sum(dy * xv * r, axis=0, keepdims=True)
        loss_ref[...] += jnp.broadcast_to(part, loss_ref.shape)

    row = pl.BlockSpec((tr, D), lambda i: (i, 0))
    vec = pl.BlockSpec((1, D), lambda i: (0, 0))
    lvec = pl.BlockSpec((1, LANES), lambda i: (0, 0))
    return pl.pallas_call(
        body, name=name, grid=(S // tr,),
        in_specs=[row, vec, row], out_specs=[row, row, vec, lvec],
        out_shape=[jax.ShapeDtypeStruct((S, D), F32), jax.ShapeDtypeStruct((S, D), BF16),
                   jax.ShapeDtypeStruct((1, D), F32), jax.ShapeDtypeStruct((1, LANES), F32)],
        compiler_params=pltpu.CompilerParams(dimension_semantics=("arbitrary",)),
    )(x, g, target)


def _ffn_up(name, n, wgu, fs, tm_target=512):
    S, D = n.shape
    nslab = wgu.shape[1] // (2 * fs)
    tm = _tile(S, tm_target)

    def body(n_ref, w_ref, gu_ref, h_ref):
        y = jnp.dot(n_ref[...], w_ref[...], preferred_element_type=F32)
        gu_ref[...] = y
        gv = y[:, :fs]
        h_ref[...] = (gv * jax.nn.sigmoid(gv) * y[:, fs:]).astype(BF16)

    blk = _nbytes((tm, D), BF16) + _nbytes((D, 2 * fs), BF16) + _nbytes((tm, 2 * fs), F32) + _nbytes((tm, fs), BF16)
    return pl.pallas_call(
        body, name=name, grid=(nslab, S // tm),
        in_specs=[pl.BlockSpec((tm, D), lambda k, i: (i, 0)), pl.BlockSpec((D, 2 * fs), lambda k, i: (0, k))],
        out_specs=[pl.BlockSpec((tm, 2 * fs), lambda k, i: (i, k)), pl.BlockSpec((tm, fs), lambda k, i: (i, k))],
        out_shape=[jax.ShapeDtypeStruct((S, nslab * 2 * fs), F32), jax.ShapeDtypeStruct((S, nslab * fs), BF16)],
        compiler_params=pltpu.CompilerParams(dimension_semantics=("parallel", "parallel"),
                                             vmem_limit_bytes=_vmem_limit(blk, _nbytes((tm, 2 * fs), F32))),
    )(n, wgu)


def _ffn_dact(name, dyb, wd, gu, fs, alpha, after=None, tm_target=512):
    S, D = dyb.shape
    nslab = wd.shape[0] // fs
    tm = _tile(S, tm_target)

    def body(*refs):
        d_ref, w_ref, gu_ref = refs[:3]
        o_ref = refs[-1]
        dhv = _dot_nt(d_ref[...], w_ref[...]) * alpha
        gv = gu_ref[:, :fs]
        uv = gu_ref[:, fs:]
        sg = jax.nn.sigmoid(gv)
        silu = gv * sg
        o_ref[:, :fs] = (dhv * uv * (sg + silu * (1.0 - sg))).astype(BF16)
        o_ref[:, fs:] = (dhv * silu).astype(BF16)

    in_specs = [pl.BlockSpec((tm, D), lambda k, i: (i, 0)), pl.BlockSpec((fs, D), lambda k, i: (k, 0)),
                pl.BlockSpec((tm, 2 * fs), lambda k, i: (i, k))] + ([_ANY] if after is not None else [])
    args = (dyb, wd, gu) + ((after,) if after is not None else ())
    blk = _nbytes((tm, D), BF16) + _nbytes((fs, D), BF16) + _nbytes((tm, 2 * fs), F32) + _nbytes((tm, 2 * fs), BF16)
    return pl.pallas_call(
        body, name=name, grid=(nslab, S // tm), in_specs=in_specs,
        out_specs=pl.BlockSpec((tm, 2 * fs), lambda k, i: (i, k)),
        out_shape=jax.ShapeDtypeStruct((S, nslab * 2 * fs), BF16),
        compiler_params=pltpu.CompilerParams(dimension_semantics=("parallel", "parallel"),
                                             vmem_limit_bytes=_vmem_limit(blk, 2 * _nbytes((tm, fs), F32))),
    )(*args)


def _tri_rows(r0, nrows, ncols, lower):
    row = lax.broadcasted_iota(jnp.int32, (nrows, ncols), 0) + r0
    col = lax.broadcasted_iota(jnp.int32, (nrows, ncols), 1)
    return jnp.where((col <= row) if lower else (col >= row), 1.0, 0.0).astype(F32)


def _gate_fwd(name, hf, b):
    S = hf.shape[0]
    tb = _row_tile(S, 256)

    def body(hf_ref, b_ref, cf_ref, cft_ref, lf_ref):
        zz = hf_ref[...] + b_ref[...]
        lf_ref[...] = jnp.minimum(zz, 0.0) - jnp.log1p(jnp.exp(-jnp.abs(zz)))

        def blk(i, c):
            r0 = pl.multiple_of(i * tb, tb)
            tri = _tri_rows(r0, tb, S, True)
            cf_ref[pl.ds(r0, tb), :] = jnp.dot(tri, lf_ref[...], precision=lax.Precision.HIGHEST,
                                               preferred_element_type=F32)
            return c

        lax.fori_loop(0, S // tb, blk, 0)
        cft_ref[...] = cf_ref[...].T

    full = pl.BlockSpec((S, LANES), lambda: (0, 0))
    return pl.pallas_call(
        body, name=name, in_specs=[full, pl.BlockSpec((1, LANES), lambda: (0, 0))],
        out_specs=[full, pl.BlockSpec((LANES, S), lambda: (0, 0))],
        out_shape=[jax.ShapeDtypeStruct((S, LANES), F32), jax.ShapeDtypeStruct((LANES, S), F32)],
        scratch_shapes=[pltpu.VMEM((S, LANES), F32)],
    )(hf, b)


def _gate_bwd(name, dcft, drow, hf, b):
    S = hf.shape[0]
    tb = _row_tile(S, 256)

    def body(dcft_ref, drow_ref, hf_ref, b_ref, dhf_ref, db_ref, dcf_ref, dlf_ref):
        dcf_ref[...] = dcft_ref[...].T + drow_ref[...]

        def blk(i, c):
            r0 = pl.multiple_of(i * tb, tb)
            tri = _tri_rows(r0, tb, S, False)
            dlf_ref[pl.ds(r0, tb), :] = jnp.dot(tri, dcf_ref[...], precision=lax.Precision.HIGHEST,
                                                preferred_element_type=F32)
            return c

        lax.fori_loop(0, S // tb, blk, 0)
        zz = hf_ref[...] + b_ref[...]
        dhf = dlf_ref[...] * jax.nn.sigmoid(-zz)
        dhf_ref[...] = dhf.astype(BF16)
        db_ref[...] = jnp.sum(dhf, axis=0, keepdims=True)

    full = pl.BlockSpec((S, LANES), lambda: (0, 0))
    vec = pl.BlockSpec((1, LANES), lambda: (0, 0))
    return pl.pallas_call(
        body, name=name, in_specs=[pl.BlockSpec((LANES, S), lambda: (0, 0)), full, full, vec],
        out_specs=[full, vec],
        out_shape=[jax.ShapeDtypeStruct((S, LANES), BF16), jax.ShapeDtypeStruct((1, LANES), F32)],
        scratch_shapes=[pltpu.VMEM((S, LANES), F32), pltpu.VMEM((S, LANES), F32)],
    )(dcft, drow, hf, b)


def _rope_tables(S):
    half = ROPE_DIMS // 2
    freqs = ROPE_THETA ** (-jnp.arange(half, dtype=F32) / half)
    ang = jnp.arange(S, dtype=F32)[:, None] * freqs[None, :]
    cos, sin = jnp.cos(ang), jnp.sin(ang)
    pad = HEAD_DIM - ROPE_DIMS
    c = jnp.concatenate([cos, cos, jnp.ones((S, pad), F32)], axis=1)
    s = jnp.concatenate([-sin, sin, jnp.zeros((S, pad), F32)], axis=1)
    return c, s


def _rope_swap(x):
    half = ROPE_DIMS // 2
    lane = lax.broadcasted_iota(jnp.int32, x.shape, 1)
    upper = jnp.where(lane < ROPE_DIMS, pltpu.roll(x, half, 1), 0.0)
    return jnp.where(lane < half, pltpu.roll(x, HEAD_DIM - half, 1), upper)


def _rope(x, c, s):
    return x * c + _rope_swap(x) * s


def _rope_t(dy, c, s):
    return dy * c + _rope_swap(dy * s)


def _split_dot(x, t):
    hi = x.astype(BF16)
    lo = (x - hi.astype(F32)).astype(BF16)
    return (jnp.dot(hi, t, preferred_element_type=F32) + jnp.dot(lo, t, preferred_element_type=F32))


_NT = (((1,), (1,)), ((), ()))
_TN = (((0,), (0,)), ((), ()))


def _dot_nt(a, b):
    return lax.dot_general(a, b, _NT, preferred_element_type=F32)


def _dot_tn(a, b):
    return lax.dot_general(a, b, _TN, preferred_element_type=F32)


def _blk(i):
    return pl.ds(pl.multiple_of(i * BLK, BLK), BLK)


def _delta(i, j):
    row = lax.broadcasted_iota(jnp.int32, (BLK, BLK), 0)
    col = lax.broadcasted_iota(jnp.int32, (BLK, BLK), 1)
    return (row - col) + (i - j) * BLK


def _dilated_mult(delta):
    c = jnp.zeros(delta.shape, F32)
    for window, dil in DILATED_PATTERNS:
        ok = (delta >= 0) & (delta <= window) & ((delta & (dil - 1)) == 0)
        c = c + jnp.where(ok, 1.0, 0.0)
    return c


def _sb_terms(z, mask, t_ex, run):
    t = jnp.log1p(jnp.exp(-jnp.abs(z)))
    lsig = jnp.minimum(z, 0.0) - t
    m = jnp.where(mask, -(jnp.maximum(z, 0.0) + t), 0.0)
    after = _split_dot(m, t_ex)
    a = jnp.where(mask, jnp.exp(lsig + after + run), 0.0)
    return a, m, lsig


def _attn_fwd(name, hq, layer_kind, n_heads, n_sb, cf=None, cft=None, rope_c=None, rope_s=None):
    S = hq.shape[0]
    D = n_heads * HEAD_DIM
    nq = S // BLK
    scale = HEAD_DIM ** -0.5
    even = layer_kind == "even"

    def body(*refs):
        if even:
            q_ref, k_ref, v_ref, cf_ref, cft_ref, o_ref, ob_ref, lse_ref, qs, ks, vs = refs
        else:
            q_ref, k_ref, v_ref, c_ref, s_ref, o_ref, ob_ref, lse_ref, qs, ks, vs = refs
        h = pl.program_id(0)
        if even:
            qs[...] = q_ref[...].astype(BF16)
            ks[...] = k_ref[...].astype(BF16)
        else:
            qs[...] = _rope(q_ref[...], c_ref[...], s_ref[...]).astype(BF16)
            ks[...] = _rope(k_ref[...], c_ref[...], s_ref[...]).astype(BF16)
        vs[...] = v_ref[...].astype(BF16)

        def softmax_head(hh):
            def qblock(i, carry):
                qi = qs[_blk(i), :]
                if even:
                    lane = lax.broadcasted_iota(jnp.int32, (BLK, LANES), 1)
                    cfq = jnp.sum(jnp.where(lane == hh, cf_ref[_blk(i), :], 0.0), axis=1, keepdims=True)

                def kblock(j, c):
                    m_run, l_run, acc = c
                    z = _dot_nt(qi, ks[_blk(j), :]) * scale
                    delta = _delta(i, j)
                    if even:
                        z = z + cfq - cft_ref[hh, :, _blk(j)]
                        ok = delta >= 0
                    else:
                        mult = _dilated_mult(delta)
                        ok = mult > 0.0
                    z = jnp.where(ok, z, NEG_INF)
                    m_new = jnp.maximum(m_run, jnp.max(z, axis=1, keepdims=True))
                    p = jnp.exp(z - m_new)
                    if not even:
                        p = p * mult
                    alpha = jnp.exp(m_run - m_new)
                    l_new = alpha * l_run + jnp.sum(p, axis=1, keepdims=True)
                    acc = alpha * acc + jnp.dot(p.astype(BF16), vs[_blk(j), :], preferred_element_type=F32)
                    return m_new, l_new, acc

                init = (jnp.full((BLK, 1), NEG_INF, F32), jnp.zeros((BLK, 1), F32), jnp.zeros((BLK, HEAD_DIM), F32))
                m_run, l_run, acc = lax.fori_loop(0, i + 1, kblock, init)
                o = acc / l_run
                o_ref[_blk(i), :] = o
                ob_ref[_blk(i), :] = o.astype(BF16)
                lse_ref[_blk(i), :] = jnp.broadcast_to(m_run + jnp.log(l_run), (BLK, HEAD_DIM))
                return carry

            lax.fori_loop(0, nq, qblock, 0)

        def sb_head():
            row = lax.broadcasted_iota(jnp.int32, (BLK, BLK), 0)
            col = lax.broadcasted_iota(jnp.int32, (BLK, BLK), 1)
            t_ex = jnp.where(row > col, 1.0, 0.0).astype(BF16)

            def qblock(i, carry):
                qi = qs[_blk(i), :]

                def kblock(jj, c):
                    run, acc = c
                    j = i - jj
                    z = _dot_nt(qi, ks[_blk(j), :]) * scale
                    a, m, _ = _sb_terms(z, _delta(i, j) > 0, t_ex, run)
                    acc = acc + jnp.dot(a.astype(BF16), vs[_blk(j), :], preferred_element_type=F32)
                    return run + jnp.sum(m, axis=1, keepdims=True), acc

                init = (jnp.zeros((BLK, 1), F32), jnp.zeros((BLK, HEAD_DIM), F32))
                _, acc = lax.fori_loop(0, i + 1, kblock, init)
                o_ref[_blk(i), :] = acc
                ob_ref[_blk(i), :] = acc.astype(BF16)
                lse_ref[_blk(i), :] = jnp.zeros((BLK, HEAD_DIM), F32)
                return carry

            lax.fori_loop(0, nq, qblock, 0)

        if even:
            @pl.when(h < n_sb)
            def _():
                sb_head()

            @pl.when(h >= n_sb)
            def _():
                softmax_head(h - n_sb)
        else:
            softmax_head(h)

    head = lambda off: pl.BlockSpec((S, HEAD_DIM), lambda h, off=off: (0, off + h))
    full = pl.BlockSpec((S, LANES), lambda h: (0, 0))
    if even:
        extra_specs = [full, pl.BlockSpec(cft.shape, lambda h: (0, 0, 0))]
        extra = (cf, cft)
    else:
        extra_specs = [full, full]
        extra = (rope_c, rope_s)
    blk_bytes = 8 * _nbytes((S, HEAD_DIM), F32)
    return pl.pallas_call(
        body, name=name, grid=(n_heads,),
        in_specs=[head(0), head(n_heads), head(2 * n_heads)] + extra_specs,
        out_specs=[head(0), head(0), head(0)],
        out_shape=[jax.ShapeDtypeStruct((S, D), F32), jax.ShapeDtypeStruct((S, D), BF16),
                   jax.ShapeDtypeStruct((S, D), F32)],
        scratch_shapes=[pltpu.VMEM((S, HEAD_DIM), BF16)] * 3,
        compiler_params=pltpu.CompilerParams(dimension_semantics=("arbitrary",),
                                             vmem_limit_bytes=_vmem_limit(blk_bytes, 3 * _nbytes((S, HEAD_DIM), BF16))),
    )(hq, hq, hq, *extra)


def _attn_bwd(name, hq, do, o, lse, layer_kind, n_heads, n_sb, cf=None, cft=None, rope_c=None, rope_s=None):
    S = hq.shape[0]
    D = n_heads * HEAD_DIM
    nq = S // BLK
    scale = HEAD_DIM ** -0.5
    even = layer_kind == "even"

    def body(*refs):
        if even:
            (q_ref, k_ref, v_ref, do_ref, o_ref, lse_ref, cf_ref, cft_ref,
             dh_ref, dcft_ref, drow_ref, qs, ks, vs, dos, dq_acc, dk_acc, dv_acc) = refs
        else:
            (q_ref, k_ref, v_ref, do_ref, o_ref, lse_ref, c_ref, s_ref,
             dh_ref, qs, ks, vs, dos, dq_acc, dk_acc, dv_acc) = refs
        h = pl.program_id(0)
        if even:
            qs[...] = q_ref[...].astype(BF16)
            ks[...] = k_ref[...].astype(BF16)

            @pl.when(h == 0)
            def _():
                dcft_ref[...] = jnp.zeros_like(dcft_ref)
                drow_ref[...] = jnp.zeros_like(drow_ref)
        else:
            qs[...] = _rope(q_ref[...], c_ref[...], s_ref[...]).astype(BF16)
            ks[...] = _rope(k_ref[...], c_ref[...], s_ref[...]).astype(BF16)
        vs[...] = v_ref[...].astype(BF16)
        dos[...] = do_ref[...].astype(BF16)
        dk_acc[...] = jnp.zeros_like(dk_acc)
        dv_acc[...] = jnp.zeros_like(dv_acc)

        def softmax_head(hh):
            def qblock(i, carry):
                qi = qs[_blk(i), :]
                doi = dos[_blk(i), :]
                dvec = jnp.sum(do_ref[_blk(i), :] * o_ref[_blk(i), :], axis=1, keepdims=True)
                lse_i = jnp.max(lse_ref[_blk(i), :], axis=1, keepdims=True)
                if even:
                    lane = lax.broadcasted_iota(jnp.int32, (BLK, LANES), 1)
                    cfq = jnp.sum(jnp.where(lane == hh, cf_ref[_blk(i), :], 0.0), axis=1, keepdims=True)

                def kblock(j, c):
                    dq, ds_rows = c
                    kj = ks[_blk(j), :]
                    z = _dot_nt(qi, kj) * scale
                    delta = _delta(i, j)
                    if even:
                        z = z + cfq - cft_ref[hh, :, _blk(j)]
                        ok = delta >= 0
                    else:
                        mult = _dilated_mult(delta)
                        ok = mult > 0.0
                    p = jnp.exp(jnp.where(ok, z, NEG_INF) - lse_i)
                    if not even:
                        p = p * mult
                    dp = _dot_nt(doi, vs[_blk(j), :])
                    ds = p * (dp - dvec)
                    dsb = (ds * scale).astype(BF16)
                    dk_acc[_blk(j), :] += _dot_tn(dsb, qi)
                    dv_acc[_blk(j), :] += _dot_tn(p.astype(BF16), doi)
                    if even:
                        dcft_ref[hh, :, _blk(j)] += -jnp.sum(ds, axis=0, keepdims=True)
                    return (dq + jnp.dot(dsb, kj, preferred_element_type=F32),
                            ds_rows + jnp.sum(ds, axis=1, keepdims=True))

                dq, ds_rows = lax.fori_loop(0, i + 1, kblock,
                                            (jnp.zeros((BLK, HEAD_DIM), F32), jnp.zeros((BLK, 1), F32)))
                dq_acc[_blk(i), :] = dq
                if even:
                    drow_ref[_blk(i), :] += jnp.where(lane == hh, ds_rows, 0.0)
                return carry

            lax.fori_loop(0, nq, qblock, 0)

        def sb_head():
            row = lax.broadcasted_iota(jnp.int32, (BLK, BLK), 0)
            col = lax.broadcasted_iota(jnp.int32, (BLK, BLK), 1)
            t_ex = jnp.where(row > col, 1.0, 0.0).astype(BF16)
            t_in = jnp.where(row >= col, 1.0, 0.0).astype(BF16)

            def qblock(i, carry):
                qi = qs[_blk(i), :]
                doi = dos[_blk(i), :]

                def e_total(jj, c):
                    run, tot = c
                    j = i - jj
                    z = _dot_nt(qi, ks[_blk(j), :]) * scale
                    a, m, _ = _sb_terms(z, _delta(i, j) > 0, t_ex, run)
                    e = _dot_nt(doi, vs[_blk(j), :]) * a
                    return run + jnp.sum(m, axis=1, keepdims=True), tot + jnp.sum(e, axis=1, keepdims=True)

                zero = jnp.zeros((BLK, 1), F32)
                _, e_tot = lax.fori_loop(0, i + 1, e_total, (zero, zero))

                def kblock(jj, c):
                    run, e_run, dq = c
                    j = i - jj
                    kj = ks[_blk(j), :]
                    z = _dot_nt(qi, kj) * scale
                    mask = _delta(i, j) > 0
                    a, m, lsig = _sb_terms(z, mask, t_ex, run)
                    sig = jnp.exp(lsig)
                    e = _dot_nt(doi, vs[_blk(j), :]) * a
                    e_before = e_tot - (_split_dot(e, t_in) + e_run)
                    dz = jnp.where(mask, e * (1.0 - sig) - sig * e_before, 0.0)
                    dzb = (dz * scale).astype(BF16)
                    dk_acc[_blk(j), :] += _dot_tn(dzb, qi)
                    dv_acc[_blk(j), :] += _dot_tn(a.astype(BF16), doi)
                    return (run + jnp.sum(m, axis=1, keepdims=True), e_run + jnp.sum(e, axis=1, keepdims=True),
                            dq + jnp.dot(dzb, kj, preferred_element_type=F32))

                _, _, dq = lax.fori_loop(0, i + 1, kblock, (zero, zero, jnp.zeros((BLK, HEAD_DIM), F32)))
                dq_acc[_blk(i), :] = dq
                return carry

            lax.fori_loop(0, nq, qblock, 0)

        if even:
            @pl.when(h < n_sb)
            def _():
                sb_head()

            @pl.when(h >= n_sb)
            def _():
                softmax_head(h - n_sb)

            dh_ref[0] = dq_acc[...].astype(BF16)
            dh_ref[1] = dk_acc[...].astype(BF16)
        else:
            softmax_head(h)
            dh_ref[0] = _rope_t(dq_acc[...], c_ref[...], s_ref[...]).astype(BF16)
            dh_ref[1] = _rope_t(dk_acc[...], c_ref[...], s_ref[...]).astype(BF16)
        dh_ref[2] = dv_acc[...].astype(BF16)

    head = lambda off: pl.BlockSpec((S, HEAD_DIM), lambda h, off=off: (0, off + h))
    full = pl.BlockSpec((S, LANES), lambda h: (0, 0))
    tfull = pl.BlockSpec((n_heads - n_sb, 1, S), lambda h: (0, 0, 0))
    dh_spec = pl.BlockSpec((3, S, HEAD_DIM), lambda h: (0, 0, h))
    dh_shape = jax.ShapeDtypeStruct((3, S, D), BF16)
    if even:
        extra_specs, extra = [full, tfull], (cf, cft)
        out_specs = [dh_spec, tfull, full]
        out_shape = [dh_shape, jax.ShapeDtypeStruct((n_heads - n_sb, 1, S), F32),
                     jax.ShapeDtypeStruct((S, LANES), F32)]
    else:
        extra_specs, extra = [full, full], (rope_c, rope_s)
        out_specs = [dh_spec]
        out_shape = [dh_shape]
    blk_bytes = 10 * _nbytes((S, HEAD_DIM), F32)
    scratch_bytes = 4 * _nbytes((S, HEAD_DIM), BF16) + 3 * _nbytes((S, HEAD_DIM), F32)
    return pl.pallas_call(
        body, name=name, grid=(n_heads,),
        in_specs=[head(0), head(n_heads), head(2 * n_heads), head(0), head(0), head(0)] + extra_specs,
        out_specs=out_specs, out_shape=out_shape,
        scratch_shapes=[pltpu.VMEM((S, HEAD_DIM), BF16)] * 4 + [pltpu.VMEM((S, HEAD_DIM), F32)] * 3,
        compiler_params=pltpu.CompilerParams(dimension_semantics=("arbitrary",),
                                             vmem_limit_bytes=_vmem_limit(blk_bytes, scratch_bytes)),
    )(hq, hq, hq, do, o, lse, *extra)


def _query_block(S):
    return min(512, S)


def _offsets(d, bq):
    row = jnp.arange(bq, dtype=jnp.int32)[:, None]
    col = jnp.arange(BLK, dtype=jnp.int32)[None, :]
    return d * BLK + row - col


def _causal_tables(bq, strict):
    r = bq // BLK
    tabs = []
    for d in range(-(r - 1), 1):
        delta = _offsets(d, bq)
        tabs.append(jnp.where((delta > 0) if strict else (delta >= 0), 1.0, 0.0))
    tabs.append(jnp.ones((bq, BLK), F32))
    return jnp.stack(tabs).astype(F32)


def _dilated_tables(bq):
    r = bq // BLK
    limit = sorted(w for w, _ in DILATED_PATTERNS)[-2]
    assert all(BLK % dil == 0 for _, dil in DILATED_PATTERNS)
    d_far = -(-(limit + BLK) // BLK)
    tabs = []
    for d in range(-(r - 1), d_far + 1):
        mult = _dilated_mult(_offsets(d, bq))
        tabs.append(jnp.where(mult > 0, jnp.log(jnp.maximum(mult, 1.0)), NEG_INF))
    return jnp.stack(tabs).astype(F32)


def _qblk(i, bq):
    return pl.ds(pl.multiple_of(i * bq, bq), bq)


def _sb_block(z, valid, t_ex, run):
    t = jnp.log1p(jnp.exp(-jnp.abs(z)))
    lsig = jnp.minimum(z, 0.0) - t
    m = -(jnp.maximum(z, 0.0) + t) * valid
    after = _split_dot(m, t_ex)
    a = jnp.exp(lsig + after + run) * valid
    return a, m, lsig


def _attn_fwd_wide(name, hq, layer_kind, n_heads, n_sb, cf=None, cft=None, rope_c=None, rope_s=None):
    S = hq.shape[0]
    D = n_heads * HEAD_DIM
    bq = _query_block(S)
    r = bq // BLK
    nq = S // bq
    scale = HEAD_DIM ** -0.5
    even = layer_kind == "even"
    if even:
        tabs = (jnp.where(_causal_tables(bq, False) > 0, 0.0, NEG_INF), _causal_tables(bq, True))
    else:
        tabs = (_dilated_tables(bq),)
    n_tab = tabs[0].shape[0]

    def body(*refs):
        if even:
            q_ref, k_ref, v_ref, cf_ref, cft_ref, bias_ref, valid_ref, o_ref, ob_ref, lse_ref, qs, ks, vs = refs
        else:
            q_ref, k_ref, v_ref, c_ref, s_ref, bias_ref, o_ref, ob_ref, lse_ref, qs, ks, vs = refs
        h = pl.program_id(0)
        if even:
            qs[...] = q_ref[...].astype(BF16)
            ks[...] = k_ref[...].astype(BF16)
        else:
            qs[...] = _rope(q_ref[...], c_ref[...], s_ref[...]).astype(BF16)
            ks[...] = _rope(k_ref[...], c_ref[...], s_ref[...]).astype(BF16)
        vs[...] = v_ref[...].astype(BF16)

        def softmax_head(hh):
            def qblock(i, carry):
                qi = qs[_qblk(i, bq), :]
                if even:
                    lane = lax.broadcasted_iota(jnp.int32, (bq, LANES), 1)
                    cfq = jnp.sum(jnp.where(lane == hh, cf_ref[_qblk(i, bq), :], 0.0), axis=1, keepdims=True)

                def kblock(j, c):
                    m_run, l_run, acc = c
                    z = _dot_nt(qi, ks[_blk(j), :]) * scale + bias_ref[jnp.minimum(r * i - j + (r - 1), n_tab - 1)]
                    if even:
                        z = z + (cfq - cft_ref[hh, :, _blk(j)])
                    m_new = jnp.maximum(m_run, jnp.max(z, axis=1, keepdims=True))
                    p = jnp.exp(z - m_new)
                    alpha = jnp.exp(m_run - m_new)
                    l_new = alpha * l_run + jnp.sum(p, axis=1, keepdims=True)
                    acc = alpha * acc + jnp.dot(p.astype(BF16), vs[_blk(j), :], preferred_element_type=F32)
                    return m_new, l_new, acc

                init = (jnp.full((bq, 1), NEG_INF, F32), jnp.zeros((bq, 1), F32), jnp.zeros((bq, HEAD_DIM), F32))
                m_run, l_run, acc = lax.fori_loop(0, r * (i + 1), kblock, init)
                o = acc / l_run
                o_ref[_qblk(i, bq), :] = o
                ob_ref[_qblk(i, bq), :] = o.astype(BF16)
                lse_ref[_qblk(i, bq), :] = jnp.broadcast_to(m_run + jnp.log(l_run), (bq, HEAD_DIM))
                return carry

            lax.fori_loop(0, nq, qblock, 0)

        def sb_head():
            row = lax.broadcasted_iota(jnp.int32, (BLK, BLK), 0)
            col = lax.broadcasted_iota(jnp.int32, (BLK, BLK), 1)
            t_ex = jnp.where(row > col, 1.0, 0.0).astype(BF16)

            def qblock(i, carry):
                qi = qs[_qblk(i, bq), :]

                def kblock(jj, c):
                    run, acc, rest = c
                    j = r * (i + 1) - 1 - jj
                    z = _dot_nt(qi, ks[_blk(j), :]) * scale
                    a, m, _ = _sb_block(z, valid_ref[jnp.minimum(r * i - j + (r - 1), r)], t_ex, run)
                    vj = vs[_blk(j), :]
                    hi = a.astype(BF16)
                    lo = (a - hi.astype(F32)).astype(BF16)
                    acc = acc + jnp.dot(hi, vj, preferred_element_type=F32)
                    rest = rest + jnp.dot(lo, vj, preferred_element_type=F32)
                    return run + jnp.sum(m, axis=1, keepdims=True), acc, rest

                zero = jnp.zeros((bq, HEAD_DIM), F32)
                _, acc, rest = lax.fori_loop(0, r * (i + 1), kblock, (jnp.zeros((bq, 1), F32), zero, zero))
                o_ref[_qblk(i, bq), :] = acc + rest
                ob_ref[_qblk(i, bq), :] = acc.astype(BF16)
                lse_ref[_qblk(i, bq), :] = jnp.zeros((bq, HEAD_DIM), F32)
                return carry

            lax.fori_loop(0, nq, qblock, 0)

        if even:
            @pl.when(h < n_sb)
            def _():
                sb_head()

            @pl.when(h >= n_sb)
            def _():
                softmax_head(h - n_sb)
        else:
            softmax_head(h)

    head = lambda off: pl.BlockSpec((S, HEAD_DIM), lambda h, off=off: (0, off + h))
    full = pl.BlockSpec((S, LANES), lambda h: (0, 0))
    tab_specs = [pl.BlockSpec(t.shape, lambda h: (0, 0, 0)) for t in tabs]
    if even:
        extra_specs = [full, pl.BlockSpec(cft.shape, lambda h: (0, 0, 0))] + tab_specs
        extra = (cf, cft) + tabs
    else:
        extra_specs = [full, full] + tab_specs
        extra = (rope_c, rope_s) + tabs
    blk_bytes = 8 * _nbytes((S, HEAD_DIM), F32) + sum(_nbytes(t.shape, F32) for t in tabs)
    return pl.pallas_call(
        body, name=name, grid=(n_heads,),
        in_specs=[head(0), head(n_heads), head(2 * n_heads)] + extra_specs,
        out_specs=[head(0), head(0), head(0)],
        out_shape=[jax.ShapeDtypeStruct((S, D), F32), jax.ShapeDtypeStruct((S, D), BF16),
                   jax.ShapeDtypeStruct((S, D), F32)],
        scratch_shapes=[pltpu.VMEM((S, HEAD_DIM), BF16)] * 3,
        compiler_params=pltpu.CompilerParams(dimension_semantics=("arbitrary",),
                                             vmem_limit_bytes=_vmem_limit(blk_bytes, 3 * _nbytes((S, HEAD_DIM), BF16))),
    )(hq, hq, hq, *extra)


def _attn_bwd_wide(name, hq, do, o, lse, layer_kind, n_heads, n_sb, cf=None, cft=None, rope_c=None, rope_s=None):
    S = hq.shape[0]
    D = n_heads * HEAD_DIM
    bq = _query_block(S)
    r = bq // BLK
    nq = S // bq
    scale = HEAD_DIM ** -0.5
    even = layer_kind == "even"
    if even:
        tabs = (jnp.where(_causal_tables(bq, False) > 0, 0.0, NEG_INF), _causal_tables(bq, True))
    else:
        tabs = (_dilated_tables(bq),)
    n_tab = tabs[0].shape[0]

    def body(*refs):
        if even:
            (q_ref, k_ref, v_ref, do_ref, o_ref, lse_ref, cf_ref, cft_ref, bias_ref, valid_ref,
             dh_ref, dcft_ref, drow_ref, qs, ks, vs, dos, dq_acc, dk_acc, dv_acc) = refs
        else:
            (q_ref, k_ref, v_ref, do_ref, o_ref, lse_ref, c_ref, s_ref, bias_ref,
             dh_ref, qs, ks, vs, dos, dq_acc, dk_acc, dv_acc) = refs
        h = pl.program_id(0)
        if even:
            qs[...] = q_ref[...].astype(BF16)
            ks[...] = k_ref[...].astype(BF16)

            @pl.when(h == 0)
            def _():
                dcft_ref[...] = jnp.zeros_like(dcft_ref)
                drow_ref[...] = jnp.zeros_like(drow_ref)
        else:
            qs[...] = _rope(q_ref[...], c_ref[...], s_ref[...]).astype(BF16)
            ks[...] = _rope(k_ref[...], c_ref[...], s_ref[...]).astype(BF16)
        vs[...] = v_ref[...].astype(BF16)
        dos[...] = do_ref[...].astype(BF16)
        dk_acc[...] = jnp.zeros_like(dk_acc)
        dv_acc[...] = jnp.zeros_like(dv_acc)

        def softmax_head(hh):
            def qblock(i, carry):
                qi = qs[_qblk(i, bq), :]
                doi = dos[_qblk(i, bq), :]
                dvec = jnp.sum(do_ref[_qblk(i, bq), :] * o_ref[_qblk(i, bq), :], axis=1, keepdims=True)
                lse_i = jnp.max(lse_ref[_qblk(i, bq), :], axis=1, keepdims=True)
                if even:
                    lane = lax.broadcasted_iota(jnp.int32, (bq, LANES), 1)
                    cfq = jnp.sum(jnp.where(lane == hh, cf_ref[_qblk(i, bq), :], 0.0), axis=1, keepdims=True)

                def kblock(j, c):
                    dq, ds_rows = c
                    kj = ks[_blk(j), :]
                    z = _dot_nt(qi, kj) * scale + bias_ref[jnp.minimum(r * i - j + (r - 1), n_tab - 1)]
                    if even:
                        z = z + (cfq - cft_ref[hh, :, _blk(j)])
                    p = jnp.exp(z - lse_i)
                    dp = _dot_nt(doi, vs[_blk(j), :])
                    ds = p * (dp - dvec)
                    dsb = (ds * scale).astype(BF16)
                    dk_acc[_blk(j), :] += _dot_tn(dsb, qi)
                    dv_acc[_blk(j), :] += _dot_tn(p.astype(BF16), doi)
                    if even:
                        dcft_ref[hh, :, _blk(j)] += -jnp.sum(ds, axis=0, keepdims=True)
                    return (dq + jnp.dot(dsb, kj, preferred_element_type=F32),
                            ds_rows + jnp.sum(ds, axis=1, keepdims=True))

                dq, ds_rows = lax.fori_loop(0, r * (i + 1), kblock,
                                            (jnp.zeros((bq, HEAD_DIM), F32), jnp.zeros((bq, 1), F32)))
                dq_acc[_qblk(i, bq), :] = dq
                if even:
                    drow_ref[_qblk(i, bq), :] += jnp.where(lane == hh, ds_rows, 0.0)
                return carry

            lax.fori_loop(0, nq, qblock, 0)

        def sb_head():
            row = lax.broadcasted_iota(jnp.int32, (BLK, BLK), 0)
            col = lax.broadcasted_iota(jnp.int32, (BLK, BLK), 1)
            t_ex = jnp.where(row > col, 1.0, 0.0).astype(BF16)
            t_in = jnp.where(row >= col, 1.0, 0.0).astype(BF16)

            def qblock(i, carry):
                qi = qs[_qblk(i, bq), :]
                doi = dos[_qblk(i, bq), :]
                nkb = r * (i + 1)
                e_tot = jnp.sum(doi.astype(F32) * o_ref[_qblk(i, bq), :], axis=1, keepdims=True)
                zero = jnp.zeros((bq, 1), F32)

                def kblock(jj, c):
                    run, e_run, dq = c
                    j = nkb - 1 - jj
                    kj = ks[_blk(j), :]
                    z = _dot_nt(qi, kj) * scale
                    valid = valid_ref[jnp.minimum(r * i - j + (r - 1), r)]
                    a, m, lsig = _sb_block(z, valid, t_ex, run)
                    sig = jnp.exp(lsig)
                    e = _dot_nt(doi, vs[_blk(j), :]) * a
                    e_before = e_tot - (_split_dot(e, t_in) + e_run)
                    dz = (e * (1.0 - sig) - sig * e_before) * valid
                    dzb = (dz * scale).astype(BF16)
                    dk_acc[_blk(j), :] += _dot_tn(dzb, qi)
                    dv_acc[_blk(j), :] += _dot_tn(a.astype(BF16), doi)
                    return (run + jnp.sum(m, axis=1, keepdims=True), e_run + jnp.sum(e, axis=1, keepdims=True),
                            dq + jnp.dot(dzb, kj, preferred_element_type=F32))

                _, _, dq = lax.fori_loop(0, nkb, kblock, (zero, zero, jnp.zeros((bq, HEAD_DIM), F32)))
                dq_acc[_qblk(i, bq), :] = dq
                return carry

            lax.fori_loop(0, nq, qblock, 0)

        if even:
            @pl.when(h < n_sb)
            def _():
                sb_head()

            @pl.when(h >= n_sb)
            def _():
                softmax_head(h - n_sb)

            dh_ref[0] = dq_acc[...].astype(BF16)
            dh_ref[1] = dk_acc[...].astype(BF16)
        else:
            softmax_head(h)
            dh_ref[0] = _rope_t(dq_acc[...], c_ref[...], s_ref[...]).astype(BF16)
            dh_ref[1] = _rope_t(dk_acc[...], c_ref[...], s_ref[...]).astype(BF16)
        dh_ref[2] = dv_acc[...].astype(BF16)

    head = lambda off: pl.BlockSpec((S, HEAD_DIM), lambda h, off=off: (0, off + h))
    full = pl.BlockSpec((S, LANES), lambda h: (0, 0))
    tfull = pl.BlockSpec((n_heads - n_sb, 1, S), lambda h: (0, 0, 0))
    tab_specs = [pl.BlockSpec(t.shape, lambda h: (0, 0, 0)) for t in tabs]
    dh_spec = pl.BlockSpec((3, S, HEAD_DIM), lambda h: (0, 0, h))
    dh_shape = jax.ShapeDtypeStruct((3, S, D), BF16)
    if even:
        extra_specs, extra = [full, tfull] + tab_specs, (cf, cft) + tabs
        out_specs = [dh_spec, tfull, full]
        out_shape = [dh_shape, jax.ShapeDtypeStruct((n_heads - n_sb, 1, S), F32),
                     jax.ShapeDtypeStruct((S, LANES), F32)]
    else:
        extra_specs, extra = [full, full] + tab_specs, (rope_c, rope_s) + tabs
        out_specs = [dh_spec]
        out_shape = [dh_shape]
    blk_bytes = 10 * _nbytes((S, HEAD_DIM), F32) + sum(_nbytes(t.shape, F32) for t in tabs)
    scratch_bytes = 4 * _nbytes((S, HEAD_DIM), BF16) + 3 * _nbytes((S, HEAD_DIM), F32)
    return pl.pallas_call(
        body, name=name, grid=(n_heads,),
        in_specs=[head(0), head(n_heads), head(2 * n_heads), head(0), head(0), head(0)] + extra_specs,
        out_specs=out_specs, out_shape=out_shape,
        scratch_shapes=[pltpu.VMEM((S, HEAD_DIM), BF16)] * 4 + [pltpu.VMEM((S, HEAD_DIM), F32)] * 3,
        compiler_params=pltpu.CompilerParams(dimension_semantics=("arbitrary",),
                                             vmem_limit_bytes=_vmem_limit(blk_bytes, scratch_bytes)),
    )(hq, hq, hq, do, o, lse, *extra)


def _ffn_fwd(tag, x, g, wgu, wd, fs):
    n = _rms_fwd(tag + "_norm", x, g)
    gu, h = _ffn_up(tag + "_gu", n, wgu, fs)
    y = _mm(tag + "_down", h, wd, "nn", F32, res=x, alpha=0.5)
    return y, (x, g, n, gu, h)


def _ffn_bwd(tag, dx, dxb, wgu, wd, fs, saved, after=None, emit=None):
    x, g, n, gu, h = saved
    dgu = _ffn_dact(tag + "_dgu", dxb, wd, gu, fs, 0.5, after=after)
    dwd = _mm(tag + "_dwd", h, dxb, "tn", BF16, alpha=0.5)
    dwgu = _mm(tag + "_dwgu", n, dgu, "tn", BF16)
    token = emit(dwgu, dwd) if emit else None
    dn = _mm(tag + "_dn", dgu, wgu, "nt", F32, after=token)
    dx_in, dxb_in, dg = _rms_bwd(tag + "_dnorm", dn, x, g, dx)
    return dx_in, dxb_in, dg, dwgu, dwd, token


def _mixer_fwd(tag, kind, x, g, wqkv, wout, n_heads, n_sb, wf=None, bf=None, rope=None):
    n = _rms_fwd(tag + "_norm", x, g)
    hq = _mm(tag + "_qkv", n, wqkv, "nn", F32)
    if kind == "even":
        hf = _mm(tag + "_gate", n, wf, "nn", F32)
        cf, cft = _gate_fwd(tag + "_cumgate", hf, bf)
        cft = cft[:n_heads - n_sb].reshape(n_heads - n_sb, 1, -1)
        o, ob, lse = _attn_fwd_wide(tag + "_attn", hq, kind, n_heads, n_sb, cf=cf, cft=cft)
    else:
        hf = cf = cft = None
        o, ob, lse = _attn_fwd_wide(tag + "_attn", hq, kind, n_heads, n_sb, rope_c=rope[0], rope_s=rope[1])
    y = _mm(tag + "_out", ob, wout, "nn", F32, res=x)
    return y, (x, g, n, hq, hf, cf, cft, o, ob, lse)


def _mixer_bwd(tag, kind, dx, dxb, wqkv, wout, n_heads, n_sb, saved, wf=None, bf=None, rope=None, after=None,
               emit=None):
    x, g, n, hq, hf, cf, cft, o, ob, lse = saved
    do = _mm(tag + "_do", dxb, wout, "nt", F32, after=after)
    dwout = _mm(tag + "_dwout", ob, dxb, "tn", BF16)
    if kind == "even":
        dh3, dcft, drow = _attn_bwd_wide(tag + "_dattn", hq, do, o, lse, kind, n_heads, n_sb, cf=cf, cft=cft)
    else:
        (dh3,) = _attn_bwd_wide(tag + "_dattn", hq, do, o, lse, kind, n_heads, n_sb, rope_c=rope[0], rope_s=rope[1])
    dwqkv = _mm(tag + "_dwqkv", n, dh3, "tn", BF16)
    dwf = db = dhf = None
    if kind == "even":
        n_fox = n_heads - n_sb
        dcft = jnp.pad(dcft.reshape(n_fox, -1), ((0, LANES - n_fox), (0, 0)))
        dhf, db = _gate_bwd(tag + "_dcumgate", dcft, drow, hf, bf)
        dwf = _mm(tag + "_dwf", n, dhf, "tn", BF16)
    token = emit(dwqkv, dwout, dwf) if emit else None
    dn = _mm(tag + "_dn", dh3, wqkv, "nt", F32, after=token)
    if kind == "even":
        dn = _mm(tag + "_dn_gate", dhf, wf, "nt", F32, res=dn)
    dx_in, dxb_in, dg = _rms_bwd(tag + "_dnorm", dn, x, g, dx)
    return dx_in, dxb_in, dg, dwqkv, dwout, dwf, db, token


def _local_step(x, target, w, fs, n_heads, n_sb, emit=None):
    S, D = x.shape
    rope = _rope_tables(S)
    kinds = ("even", "odd")
    saved = []
    h = x
    if callable(w):
        fetch, w = w, {"norm_g": w("norm_g", None), "final_g": w("final_g", None),
                       "wgu1": [None, None], "wd1": [None, None], "wgu2": [None, None], "wd2": [None, None]}
    else:
        fetch = None
    for l, kind in enumerate(kinds):
        ng = [w["norm_g"][l, i][None, :] for i in range(3)]
        if fetch:
            w["wgu1"][l], w["wd1"][l] = fetch(("ffn1", l), h)
        h, s1 = _ffn_fwd(f"l{l}_ffn1", h, ng[0], w["wgu1"][l], w["wd1"][l], fs)
        if fetch:
            w.update(fetch(("mix", l), h))
        if kind == "even":
            h, s2 = _mixer_fwd(f"l{l}_mix", kind, h, ng[1], w["wqkv_e"], w["wout_e"], n_heads, n_sb,
                               wf=w["wf"], bf=w["bf"])
        else:
            h, s2 = _mixer_fwd(f"l{l}_mix", kind, h, ng[1], w["wqkv_o"], w["wout_o"], n_heads, n_sb, rope=rope)
        if fetch:
            w["wgu2"][l], w["wd2"][l] = fetch(("ffn2", l), h)
        h, s3 = _ffn_fwd(f"l{l}_ffn2", h, ng[2], w["wgu2"][l], w["wd2"][l], fs)
        saved.append((s1, s2, s3))

    dx, dxb, dfinal, loss = _loss_head("loss_head", h, w["final_g"], target)
    grads = {"dfinal": dfinal, "dnorm": [[None] * 3 for _ in kinds],
             "dwgu1": [None, None], "dwd1": [None, None], "dwgu2": [None, None], "dwd2": [None, None]}
    hand = lambda block: (lambda *mats: emit(block, mats)) if emit else None
    token = None
    for l in (1, 0):
        kind = kinds[l]
        s1, s2, s3 = saved[l]
        dx, dxb, dg, grads["dwgu2"][l], grads["dwd2"][l], token = _ffn_bwd(
            f"l{l}_ffn2", dx, dxb, w["wgu2"][l], w["wd2"][l], fs, s3, after=token, emit=hand(("ffn2", l)))
        grads["dnorm"][l][2] = dg
        if kind == "even":
            dx, dxb, dg, grads["dwqkv_e"], grads["dwout_e"], grads["dwf"], grads["db"], token = _mixer_bwd(
                f"l{l}_mix", kind, dx, dxb, w["wqkv_e"], w["wout_e"], n_heads, n_sb, s2, wf=w["wf"], bf=w["bf"],
                after=token, emit=hand(("mix", l)))
        else:
            dx, dxb, dg, grads["dwqkv_o"], grads["dwout_o"], _, _, token = _mixer_bwd(
                f"l{l}_mix", kind, dx, dxb, w["wqkv_o"], w["wout_o"], n_heads, n_sb, s2, rope=rope,
                after=token, emit=hand(("mix", l)))
        grads["dnorm"][l][1] = dg
        dx, dxb, dg, grads["dwgu1"][l], grads["dwd1"][l], token = _ffn_bwd(
            f"l{l}_ffn1", dx, dxb, w["wgu1"][l], w["wd1"][l], fs, s1, after=token, emit=hand(("ffn1", l)))
        grads["dnorm"][l][0] = dg
    return loss, dx, grads


def _cast_into(name, shard, layer, chip, full_shape, place, full=None):
    R, C = shard.shape[-2:]
    tr = _row_tile(R, 512, step=16)
    if layer is None:
        in_spec = pl.BlockSpec((tr, C), lambda i, k: (i, 0))
    else:
        in_spec = pl.BlockSpec((None, tr, C), lambda i, k: (layer, i, 0))
    lead = (None,) * (len(full_shape) - 2)
    out_spec = pl.BlockSpec(lead + (tr, C), lambda i, k: place(i, k[0]))

    def body(*refs):
        k_ref, w_ref = refs[:2]
        o_ref = refs[-1]
        o_ref[...] = w_ref[...].astype(BF16)

    in_specs = [in_spec] + ([_ANY] if full is not None else [])
    args = (chip, shard) + ((full,) if full is not None else ())
    grid_spec = pltpu.PrefetchScalarGridSpec(num_scalar_prefetch=1, grid=(R // tr,), in_specs=in_specs, out_specs=out_spec)
    return pl.pallas_call(
        body, name=name, grid_spec=grid_spec, out_shape=jax.ShapeDtypeStruct(full_shape, BF16),
        input_output_aliases={2: 0} if full is not None else {},
        compiler_params=pltpu.CompilerParams(dimension_semantics=("arbitrary",)),
    )(*args)


def _region_shape(grad, kind):
    if kind == "lead":
        return grad.shape[1] // N_CORES, grad.shape[2]
    rows, cols = grad.shape
    if kind == "cols":
        return rows // N_CORES, cols // N_CHIPS
    return rows // (N_CHIPS * N_CORES), cols


def _region_add(name, grad, kind, landed, core):
    rh, cw = _region_shape(grad, kind)
    tr = _row_tile(rh, 256, step=16)
    nrb = rh // tr
    if kind == "cols":
        g_spec = pl.BlockSpec((tr, cw), lambda k, r, c: (c[0] * nrb + r, k))
    elif kind == "rows":
        g_spec = pl.BlockSpec((tr, cw), lambda k, r, c: ((N_CORES * k + c[0]) * nrb + r, 0))
    else:
        g_spec = pl.BlockSpec((None, tr, cw), lambda k, r, c: (k, c[0] * nrb + r, 0))
    l_spec = pl.BlockSpec((None, tr, cw), lambda k, r, c: (k, r, 0))

    def body(c_ref, g_ref, l_ref, o_ref):
        o_ref[...] = (g_ref[...].astype(F32) + l_ref[...].astype(F32)).astype(BF16)

    grid_spec = pltpu.PrefetchScalarGridSpec(
        num_scalar_prefetch=1, grid=(N_CHIPS, nrb), in_specs=[g_spec, l_spec], out_specs=l_spec)
    return pl.pallas_call(
        body, name=name, grid_spec=grid_spec, out_shape=jax.ShapeDtypeStruct(landed.shape, BF16),
        compiler_params=pltpu.CompilerParams(dimension_semantics=("parallel", "parallel"),
                                             vmem_limit_bytes=_vmem_limit(3 * _nbytes((tr, cw), F32))),
    )(core, grad, landed)


def _chip_sum(name, pair, landed, pos):
    _, rh, cw = pair.shape
    tr = _row_tile(rh, max(16, 2**20 // (cw * 4)), step=16)
    nrb = rh // tr

    def body(p_ref, own_ref, l_ref, o_ref):
        acc = own_ref[...].astype(F32)
        for s in range(N_CHIPS - 1):
            acc = acc + l_ref[s].astype(F32)
        o_ref[...] = acc

    grid_spec = pltpu.PrefetchScalarGridSpec(
        num_scalar_prefetch=1, grid=(nrb,),
        in_specs=[pl.BlockSpec((None, tr, cw), lambda r, p: (p[0], r, 0)),
                  pl.BlockSpec((N_CHIPS - 1, tr, cw), lambda r, p: (0, r, 0))],
        out_specs=pl.BlockSpec((tr, cw), lambda r, p: (p[1] * nrb + r, 0)))
    return pl.pallas_call(
        body, name=name, grid_spec=grid_spec, out_shape=jax.ShapeDtypeStruct((N_CORES * rh, cw), F32),
        compiler_params=pltpu.CompilerParams(dimension_semantics=("arbitrary",)),
    )(pos, pair, landed)


def _sum_leading(name, parts):
    n, R, C = parts.shape
    tr = _row_tile(R, max(8, (2**20 // (C * 4)) // 8 * 8))

    def body(p_ref, o_ref):
        acc = p_ref[0]
        for s in range(1, n):
            acc = acc + p_ref[s]
        o_ref[...] = acc

    return pl.pallas_call(
        body, name=name, grid=(R // tr,),
        in_specs=[pl.BlockSpec((n, tr, C), lambda i: (0, i, 0))],
        out_specs=pl.BlockSpec((tr, C), lambda i: (i, 0)),
        out_shape=jax.ShapeDtypeStruct((R, C), F32),
        compiler_params=pltpu.CompilerParams(dimension_semantics=("parallel",)),
    )(parts)


def _adamw(name, w, g, m, v):
    shape = w.shape
    to2d = lambda t: t.reshape(-1, shape[-1]) if t.ndim > 1 else t.reshape(1, -1)
    w2, g2, m2, v2 = (to2d(t) for t in (w, g, m, v))
    R, C = w2.shape
    tr = _row_tile(R, 256)

    def body(w_ref, g_ref, m_ref, v_ref, d_ref, nm_ref, nv_ref):
        gv = g_ref[...]
        nm = ADAM_B1 * m_ref[...] + (1.0 - ADAM_B1) * gv
        nv = ADAM_B2 * v_ref[...] + (1.0 - ADAM_B2) * (gv * gv)
        m_hat = nm / (1.0 - ADAM_B1 ** ADAM_STEP)
        v_hat = nv / (1.0 - ADAM_B2 ** ADAM_STEP)
        d_ref[...] = -ADAM_LR * (m_hat / (jnp.sqrt(v_hat) + ADAM_EPS) + ADAM_WD * w_ref[...])
        nm_ref[...] = nm
        nv_ref[...] = nv

    spec = pl.BlockSpec((tr, C), lambda i: (i, 0))
    sds = jax.ShapeDtypeStruct((R, C), F32)
    d, nm, nv = pl.pallas_call(
        body, name=name, grid=(R // tr,), in_specs=[spec] * 4, out_specs=[spec] * 3, out_shape=[sds] * 3,
        compiler_params=pltpu.CompilerParams(dimension_semantics=("parallel",),
                                             vmem_limit_bytes=_vmem_limit(7 * _nbytes((tr, C), F32))),
    )(w2, g2, m2, v2)
    return d.reshape(shape), nm.reshape(shape), nv.reshape(shape)


def _adamw_layer(name, w, m, v, g_src, g_col, layer, outs=None):
    L, R, C = w.shape
    tr = _row_tile(R, 256)

    def body(*refs):
        w_ref, m_ref, v_ref, g_ref = refs[:4]
        go_ref, d_ref, nm_ref, nv_ref = refs[-4:]
        gv = g_ref[...]
        nm = ADAM_B1 * m_ref[...] + (1.0 - ADAM_B1) * gv
        nv = ADAM_B2 * v_ref[...] + (1.0 - ADAM_B2) * (gv * gv)
        m_hat = nm / (1.0 - ADAM_B1 ** ADAM_STEP)
        v_hat = nv / (1.0 - ADAM_B2 ** ADAM_STEP)
        go_ref[...] = gv
        d_ref[...] = -ADAM_LR * (m_hat / (jnp.sqrt(v_hat) + ADAM_EPS) + ADAM_WD * w_ref[...])
        nm_ref[...] = nm
        nv_ref[...] = nv

    stacked = pl.BlockSpec((None, tr, C), lambda i: (layer, i, 0))
    in_specs = [stacked] * 3 + [pl.BlockSpec((tr, C), lambda i: (i, g_col))] + ([_ANY] * 4 if outs else [])
    sds = jax.ShapeDtypeStruct((L, R, C), F32)
    return pl.pallas_call(
        body, name=name, grid=(R // tr,), in_specs=in_specs, out_specs=[stacked] * 4, out_shape=[sds] * 4,
        input_output_aliases={4 + i: i for i in range(4)} if outs else {},
        compiler_params=pltpu.CompilerParams(dimension_semantics=("parallel",),
                                             vmem_limit_bytes=_vmem_limit(8 * _nbytes((tr, C), F32))),
    )(w, m, v, g_src, *(outs or ()))


_ANY = pl.BlockSpec(memory_space=pl.ANY)


def _mesh_pos():
    return lax.axis_index("x"), lax.axis_index("y"), lax.axis_index("c")


def _other_chips(x, y):
    return [(1 - x, y), (x, 1 - y), (1 - x, 1 - y)]


def _gather_over_chips(name, fulls, views):
    n = len(views)
    nf = len(fulls)

    def body(*refs):
        full = refs[nf:2 * nf]
        ici_send, ici_recv, d2d_send, d2d_recv = refs[2 * nf:]
        x, y, c = _mesh_pos()
        chips = _other_chips(x, y)
        mine = 2 * x + y
        sibling = (x, y, 1 - c)

        def ici(a, p, k):
            i, view, _ = views[a]
            part = view(full[i], k, c)
            return pltpu.make_async_remote_copy(
                src_ref=part, dst_ref=part, send_sem=ici_send.at[a, p], recv_sem=ici_recv.at[a, p],
                device_id=(*chips[p], c), device_id_type=MESH)

        def d2d(a, p, h):
            i, view, _ = views[a]
            px, py = chips[p]
            part = view(full[i], 2 * px + py, h)
            return pltpu.make_async_remote_copy(
                src_ref=part, dst_ref=part, send_sem=d2d_send.at[a, p], recv_sem=d2d_recv.at[a, p],
                device_id=sibling, device_id_type=MESH)

        sends = [ici(a, p, mine) for a in range(n) for p in range(3)]
        for cp in sends:
            cp.start()
        passed = []
        for a in range(n):
            for p, (px, py) in enumerate(chips):
                ici(a, p, 2 * px + py).wait_recv()
                if views[a][2]:
                    fwd = d2d(a, p, c)
                    fwd.start()
                    passed.append(fwd)
        for a in range(n):
            if views[a][2]:
                for p in range(3):
                    d2d(a, p, 1 - c).wait_recv()
        for cp in sends + passed:
            cp.wait_send()

    return pl.pallas_call(
        body, name=name, in_specs=[_ANY] * nf, out_specs=[_ANY] * nf,
        out_shape=[jax.ShapeDtypeStruct(f.shape, f.dtype) for f in fulls],
        input_output_aliases={i: i for i in range(nf)},
        scratch_shapes=[pltpu.SemaphoreType.DMA((n, 3))] * 4,
        compiler_params=pltpu.CompilerParams(has_side_effects=True),
    )(*fulls)


_HBM = pl.BlockSpec(memory_space=pltpu.HBM)
_SEM = pl.BlockSpec(memory_space=pltpu.SEMAPHORE)


def _in_hbm(arrays):
    return [pltpu.with_memory_space_constraint(a, pltpu.HBM) for a in arrays]


def _gather_start(name, fulls, views):
    nf = len(fulls)
    ng = 1 + max(g for _, _, g in views)

    def body(*refs):
        full = refs[nf:2 * nf]
        send_sems, recv_sems = refs[2 * nf:2 * nf + ng], refs[2 * nf + ng:]
        x, y, c = _mesh_pos()
        chips = _other_chips(x, y)
        for i, view, g in views:
            part = view(full[i], 2 * x + y, c)
            for px, py in chips:
                pltpu.make_async_remote_copy(
                    src_ref=part, dst_ref=part, send_sem=send_sems[g], recv_sem=recv_sems[g],
                    device_id=(px, py, c), device_id_type=MESH).start()

    outs = pl.pallas_call(
        body, name=name, in_specs=[_HBM] * nf, out_specs=[_HBM] * nf + [_SEM] * (2 * ng),
        out_shape=[pltpu.HBM(f.shape, f.dtype) for f in fulls] + [pltpu.SemaphoreType.DMA(())] * (2 * ng),
        input_output_aliases={i: i for i in range(nf)},
        compiler_params=pltpu.CompilerParams(has_side_effects=pltpu.SideEffectType.DATAFLOW_SIDE_EFFECTING),
    )(*_in_hbm(fulls))
    return list(outs[:nf]), list(outs[nf:nf + ng]), list(outs[nf + ng:])


def _gather_wait(name, fulls, views, send_sem, recv_sem, after):
    nf = len(fulls)

    def body(*refs):
        send_ref, recv_ref = refs[nf], refs[nf + 1]
        full = refs[nf + 3:]
        x, y, c = _mesh_pos()
        copies = [pltpu.make_async_remote_copy(
            src_ref=view(full[i], 2 * x + y, c), dst_ref=view(full[i], 2 * px + py, c),
            send_sem=send_ref, recv_sem=recv_ref, device_id=(px, py, c), device_id_type=MESH)
            for i, view in views for px, py in _other_chips(x, y)]
        for cp in copies:
            cp.wait_send()
        for cp in copies:
            cp.wait_recv()

    outs = pl.pallas_call(
        body, name=name, in_specs=[_HBM] * nf + [_SEM, _SEM, _ANY], out_specs=[_HBM] * nf,
        out_shape=[pltpu.HBM(f.shape, f.dtype) for f in fulls],
        input_output_aliases={i: i for i in range(nf)},
        compiler_params=pltpu.CompilerParams(has_side_effects=pltpu.SideEffectType.DATAFLOW_SIDE_EFFECTING),
    )(*fulls, send_sem, recv_sem, after)
    return list(outs)


def _forward_to_sibling(name, fulls, views):
    n, nf = len(views), len(fulls)

    def body(*refs):
        full = refs[nf:2 * nf]
        send_sems, recv_sems = refs[2 * nf:]
        x, y, c = _mesh_pos()
        chips = _other_chips(x, y)

        def copy(a, p, h):
            i, view = views[a]
            px, py = chips[p]
            part = view(full[i], 2 * px + py, h)
            return pltpu.make_async_remote_copy(
                src_ref=part, dst_ref=part, send_sem=send_sems.at[a, p], recv_sem=recv_sems.at[a, p],
                device_id=(x, y, 1 - c), device_id_type=MESH)

        sends = [copy(a, p, c) for a in range(n) for p in range(3)]
        for cp in sends:
            cp.start()
        for a in range(n):
            for p in range(3):
                copy(a, p, 1 - c).wait_recv()
        for cp in sends:
            cp.wait_send()

    return pl.pallas_call(
        body, name=name, in_specs=[_ANY] * nf, out_specs=[_ANY] * nf,
        out_shape=[jax.ShapeDtypeStruct(f.shape, f.dtype) for f in fulls],
        input_output_aliases={i: i for i in range(nf)},
        scratch_shapes=[pltpu.SemaphoreType.DMA((n, 3))] * 2,
        compiler_params=pltpu.CompilerParams(has_side_effects=True),
    )(*fulls)


def _region_view(ref, kind, k, c):
    if kind == "lead":
        rh = ref.shape[1] // N_CORES
        return ref.at[k, pl.ds(pl.multiple_of(c * rh, 8), rh), :]
    rows, cols = ref.shape
    if kind == "cols":
        rh, cw = rows // N_CORES, cols // N_CHIPS
        return ref.at[pl.ds(pl.multiple_of(c * rh, 8), rh), pl.ds(k * cw, cw)]
    rh = rows // (N_CHIPS * N_CORES)
    return ref.at[pl.ds(pl.multiple_of((N_CORES * k + c) * rh, 8), rh), :]


def _send_to_sibling(name, grads, kinds):
    n = len(grads)
    shapes = [jax.ShapeDtypeStruct((N_CHIPS,) + _region_shape(g, kd), g.dtype) for g, kd in zip(grads, kinds)]

    def body(*refs):
        g_ref, land = refs[:n], refs[n:2 * n]
        send_sems, recv_sems = refs[2 * n:]
        x, y, c = _mesh_pos()
        copies = []
        for a in range(n):
            for k in range(N_CHIPS):
                cp = pltpu.make_async_remote_copy(
                    src_ref=_region_view(g_ref[a], kinds[a], k, 1 - c), dst_ref=land[a].at[k],
                    send_sem=send_sems.at[a, k], recv_sem=recv_sems.at[a, k],
                    device_id=(x, y, 1 - c), device_id_type=MESH)
                cp.start()
                copies.append(cp)
        for cp in copies:
            cp.wait_recv()
        for cp in copies:
            cp.wait_send()

    return pl.pallas_call(
        body, name=name, in_specs=[_ANY] * n, out_specs=[_ANY] * n, out_shape=shapes,
        scratch_shapes=[pltpu.SemaphoreType.DMA((n, N_CHIPS)), pltpu.SemaphoreType.DMA((n, N_CHIPS))],
        compiler_params=pltpu.CompilerParams(has_side_effects=True),
    )(*grads)


def _scatter_over_chips(name, pair_sums):
    n = len(pair_sums)

    def body(*refs):
        p_ref, land = refs[:n], refs[n:2 * n]
        send_sems, recv_sems = refs[2 * n:]
        x, y, c = _mesh_pos()
        chips = _other_chips(x, y)
        sends = []
        for a in range(n):
            for p, (px, py) in enumerate(chips):
                cp = pltpu.make_async_remote_copy(
                    src_ref=p_ref[a].at[2 * px + py], dst_ref=land[a].at[p], send_sem=send_sems.at[a, p],
                    recv_sem=recv_sems.at[a, p], device_id=(px, py, c), device_id_type=MESH)
                cp.start()
                sends.append(cp)
        for cp in sends:
            cp.wait_recv()
        for cp in sends:
            cp.wait_send()

    return pl.pallas_call(
        body, name=name, in_specs=[_ANY] * n, out_specs=[_ANY] * n,
        out_shape=[jax.ShapeDtypeStruct((N_CHIPS - 1,) + p.shape[1:], p.dtype) for p in pair_sums],
        scratch_shapes=[pltpu.SemaphoreType.DMA((n, 3)), pltpu.SemaphoreType.DMA((n, 3))],
        compiler_params=pltpu.CompilerParams(has_side_effects=True),
    )(*pair_sums)


def _scatter_start(name, pair_sums):
    n = len(pair_sums)
    lands = [lax.empty((N_CHIPS - 1,) + p.shape[1:], p.dtype) for p in pair_sums]

    def body(*refs):
        p_ref, land = refs[2 * n:3 * n], refs[3 * n:4 * n]
        send_sem, recv_sem, token = refs[4 * n:]
        x, y, c = _mesh_pos()
        for a in range(n):
            for p, (px, py) in enumerate(_other_chips(x, y)):
                pltpu.make_async_remote_copy(
                    src_ref=p_ref[a].at[2 * px + py], dst_ref=land[a].at[p], send_sem=send_sem, recv_sem=recv_sem,
                    device_id=(px, py, c), device_id_type=MESH).start()
        token[...] = jnp.zeros_like(token)

    outs = pl.pallas_call(
        body, name=name, in_specs=[_HBM] * (2 * n),
        out_specs=[_HBM] * (2 * n) + [_SEM, _SEM, pl.BlockSpec(memory_space=pltpu.VMEM)],
        out_shape=[pltpu.HBM(t.shape, t.dtype) for t in list(pair_sums) + lands]
        + [pltpu.SemaphoreType.DMA(()), pltpu.SemaphoreType.DMA(()), jax.ShapeDtypeStruct((8, LANES), F32)],
        input_output_aliases={i: i for i in range(2 * n)},
        compiler_params=pltpu.CompilerParams(has_side_effects=pltpu.SideEffectType.DATAFLOW_SIDE_EFFECTING),
    )(*_in_hbm(list(pair_sums) + lands))
    return list(outs[:n]), list(outs[n:2 * n]), outs[2 * n], outs[2 * n + 1], outs[2 * n + 2]


def _scatter_wait(name, pair_sums, lands, send_sem, recv_sem, after):
    n = len(pair_sums)

    def body(*refs):
        send_ref, recv_ref = refs[2 * n], refs[2 * n + 1]
        p_ref, land = refs[2 * n + 3:3 * n + 3], refs[3 * n + 3:]
        x, y, c = _mesh_pos()
        copies = [pltpu.make_async_remote_copy(
            src_ref=p_ref[a].at[2 * px + py], dst_ref=land[a].at[p], send_sem=send_ref, recv_sem=recv_ref,
            device_id=(px, py, c), device_id_type=MESH)
            for a in range(n) for p, (px, py) in enumerate(_other_chips(x, y))]
        for cp in copies:
            cp.wait_send()
        for cp in copies:
            cp.wait_recv()

    outs = pl.pallas_call(
        body, name=name, in_specs=[_HBM] * (2 * n) + [_SEM, _SEM, _ANY], out_specs=[_HBM] * (2 * n),
        out_shape=[pltpu.HBM(t.shape, t.dtype) for t in list(pair_sums) + list(lands)],
        input_output_aliases={i: i for i in range(2 * n)},
        compiler_params=pltpu.CompilerParams(has_side_effects=pltpu.SideEffectType.DATAFLOW_SIDE_EFFECTING),
    )(*pair_sums, *lands, send_sem, recv_sem, after)
    return list(outs[:n]), list(outs[n:])


def _swap_halves(name, shards):
    n = len(shards)

    def body(*refs):
        out = refs[n:2 * n]
        send_sems, recv_sems = refs[2 * n:]
        x, y, c = _mesh_pos()
        sends = []
        for a in range(n):
            rh = out[a].shape[0] // N_CORES
            mine = out[a].at[pl.ds(pl.multiple_of(c * rh, 8), rh), :]
            cp = pltpu.make_async_remote_copy(
                src_ref=mine, dst_ref=mine, send_sem=send_sems.at[a], recv_sem=recv_sems.at[a],
                device_id=(x, y, 1 - c), device_id_type=MESH)
            cp.start()
            sends.append(cp)
        for a in range(n):
            rh = out[a].shape[0] // N_CORES
            theirs = out[a].at[pl.ds(pl.multiple_of((1 - c) * rh, 8), rh), :]
            pltpu.make_async_remote_copy(
                src_ref=theirs, dst_ref=theirs, send_sem=send_sems.at[a], recv_sem=recv_sems.at[a],
                device_id=(x, y, 1 - c), device_id_type=MESH).wait_recv()
        for cp in sends:
            cp.wait_send()

    return pl.pallas_call(
        body, name=name, in_specs=[_ANY] * n, out_specs=[_ANY] * n,
        out_shape=[jax.ShapeDtypeStruct(s.shape, s.dtype) for s in shards],
        input_output_aliases={i: i for i in range(n)},
        scratch_shapes=[pltpu.SemaphoreType.DMA((n,)), pltpu.SemaphoreType.DMA((n,))],
        compiler_params=pltpu.CompilerParams(has_side_effects=True),
    )(*shards)


def _gather_all_devices(name, block):
    R, C = block.shape
    ndev = N_CHIPS * N_CORES

    def body(b_ref, out_ref, send_sems, recv_sems, local_sem):
        x, y, c = _mesh_pos()
        mine = 4 * x + 2 * y + c
        own = pltpu.make_async_copy(b_ref, out_ref.at[mine], local_sem)
        own.start()
        sends = []
        for mask in range(1, ndev):
            fx, fy, fc = (mask >> 2) & 1, (mask >> 1) & 1, mask & 1
            px, py, pc = x ^ fx, y ^ fy, c ^ fc
            cp = pltpu.make_async_remote_copy(
                src_ref=b_ref, dst_ref=out_ref.at[mine], send_sem=send_sems.at[mask - 1],
                recv_sem=recv_sems.at[mask - 1], device_id=(px, py, pc), device_id_type=MESH)
            cp.start()
            sends.append(cp)
        for mask in range(1, ndev):
            fx, fy, fc = (mask >> 2) & 1, (mask >> 1) & 1, mask & 1
            px, py, pc = x ^ fx, y ^ fy, c ^ fc
            pltpu.make_async_remote_copy(
                src_ref=b_ref, dst_ref=out_ref.at[4 * px + 2 * py + pc], send_sem=send_sems.at[mask - 1],
                recv_sem=recv_sems.at[mask - 1], device_id=(px, py, pc), device_id_type=MESH).wait_recv()
        for cp in sends:
            cp.wait_send()
        own.wait()

    return pl.pallas_call(
        body, name=name, in_specs=[_ANY], out_specs=_ANY,
        out_shape=jax.ShapeDtypeStruct((ndev, R, C), F32),
        scratch_shapes=[pltpu.SemaphoreType.DMA((ndev - 1,)), pltpu.SemaphoreType.DMA((ndev - 1,)),
                        pltpu.SemaphoreType.DMA(())],
        compiler_params=pltpu.CompilerParams(has_side_effects=True),
    )(block)


def _reduce_scatter(grads, kinds, pos):
    landed = _send_to_sibling("rs_pair_send", grads, kinds)
    pair = [_region_add(f"rs_pair_add_{a}", g, kd, ld, pos[1:]) for a, (g, kd, ld) in enumerate(zip(grads, kinds, landed))]
    parts = _scatter_over_chips("rs_chip_send", pair)
    shards = [_chip_sum(f"rs_chip_add_{a}", p, ld, pos) for a, (p, ld) in enumerate(zip(pair, parts))]
    return _swap_halves("rs_swap_halves", shards)


def kernel(x, norm_g, ffn1_w_gate, ffn1_w_up, ffn1_w_down, ffn2_w_gate, ffn2_w_up, ffn2_w_down, even_w_in, even_b_forget, even_w_out, odd_w_qkv, odd_w_out, final_norm_g, loss_target, m_norm_g, m_ffn1_w_gate, m_ffn1_w_up, m_ffn1_w_down, m_ffn2_w_gate, m_ffn2_w_up, m_ffn2_w_down, m_even_w_in, m_even_b_forget, m_even_w_out, m_odd_w_qkv, m_odd_w_out, m_final_norm_g, v_norm_g, v_ffn1_w_gate, v_ffn1_w_up, v_ffn1_w_down, v_ffn2_w_gate, v_ffn2_w_up, v_ffn2_w_down, v_even_w_in, v_even_b_forget, v_even_w_out, v_odd_w_qkv, v_odd_w_out, v_final_norm_g):
    _, S, D = x.shape
    L = norm_g.shape[0]
    assert L == 2 and even_w_in.shape[0] == 1 and odd_w_qkv.shape[0] == 1
    fs = ffn1_w_gate.shape[2]
    F = N_CHIPS * fs
    wc = even_w_in.shape[2]
    n_heads = D // HEAD_DIM
    n_fox = N_CHIPS * wc - 3 * D
    n_sb = n_heads - n_fox
    qs = odd_w_qkv.shape[2]
    os_ = even_w_out.shape[1]
    ns = norm_g.shape[2]
    xi, yi, ci = _mesh_pos()
    chip = 2 * xi + yi

    pos = jnp.stack([chip, ci]).astype(jnp.int32)
    kchip = pos[:1]
    lane = lambda start, size: pl.ds(pl.multiple_of(start, LANES), size)
    sub = lambda start, size: pl.ds(pl.multiple_of(start, 16), size)
    gate_view = lambda r, k, h: r.at[sub(h * (D // 2), D // 2), lane(k * 2 * fs, fs)]
    up_view = lambda r, k, h: r.at[sub(h * (D // 2), D // 2), lane(k * 2 * fs + fs, fs)]
    down_view = lambda r, k, h: r.at[sub(k * fs + h * (fs // 2), fs // 2), :]
    out_view = lambda r, k, h: r.at[sub(k * os_ + h * (os_ // 2), os_ // 2), :]
    tr_d = _row_tile(fs, 512, step=16)
    tr_o = _row_tile(os_, 512, step=16)
    ffn_w = {"ffn1": (ffn1_w_gate, ffn1_w_up, ffn1_w_down), "ffn2": (ffn2_w_gate, ffn2_w_up, ffn2_w_down)}
    win_view = lambda r, k, h: r.at[k, sub(h * (D // 2), D // 2), :]
    qkv_view = lambda r, k, h: r.at[sub(h * (D // 2), D // 2), lane(k * qs, qs)]
    fulls, views, groups = [], [], {}
    for l in range(L):
        for blk in ("ffn1", "mix", "ffn2"):
            o, v0 = len(fulls), len(views)
            if blk == "mix" and l == 0:
                fulls += [_cast_into("cast_win", even_w_in, 0, kchip, (N_CHIPS, D, wc), lambda i, k: (k, i, 0)),
                          _cast_into("cast_wout_e", even_w_out, 0, kchip, (D, D), lambda i, k: (k * (os_ // tr_o) + i, 0))]
                views += [(o, win_view), (o + 1, out_view)]
            elif blk == "mix":
                fulls += [_cast_into("cast_wqkv_o", odd_w_qkv, 0, kchip, (D, N_CHIPS * qs), lambda i, k: (i, k)),
                          _cast_into("cast_wout_o", odd_w_out, 0, kchip, (D, D), lambda i, k: (k * (os_ // tr_o) + i, 0))]
                views += [(o, qkv_view), (o + 1, out_view)]
            else:
                wg, wu, wd = ffn_w[blk]
                t = f"cast_{blk}_l{l}"
                gu = _cast_into(t + "_gate", wg, l, kchip, (D, 2 * F), lambda i, k: (i, 2 * k))
                gu = _cast_into(t + "_up", wu, l, kchip, (D, 2 * F), lambda i, k: (i, 2 * k + 1), full=gu)
                dn = _cast_into(t + "_down", wd, l, kchip, (F, D), lambda i, k: (k * (fs // tr_d) + i, 0))
                fulls += [gu, dn]
                views += [(o, gate_view), (o, up_view), (o + 1, down_view)]
            gid = len(groups)
            views[v0:] = [(i, view, gid) for i, view in views[v0:]]
            groups[(blk, l)] = (gid, list(range(o, len(fulls))), list(range(v0, len(views))))
    norm_own = lax.dynamic_update_slice(jnp.zeros((L, 3, N_CHIPS * ns), F32), norm_g, (0, 0, chip * ns))
    (norm_full,) = _gather_over_chips("gather_norm", [norm_own], [(0, lambda r, k, h: r.at[:, :, lane(k * ns, ns)], False)])
    started, send_sems, recv_sems = _gather_start("gather_start", fulls, views)

    def fetch(block, after):
        if block == "norm_g":
            return norm_full
        if block == "final_g":
            return final_norm_g[None, :]
        gid, arrays, rows = groups[block]
        tag = f"{block[0]}_l{block[1]}"
        local = [(views[a][0] - arrays[0], views[a][1]) for a in rows]
        got = _gather_wait("gather_wait_" + tag, [started[i] for i in arrays], local, send_sems[gid], recv_sems[gid],
                           x if after is None else after)
        got = _forward_to_sibling("gather_pass_" + tag, got, local)
        if block[0] != "mix":
            return got
        if block[1] == 1:
            return {"wqkv_o": got[0], "wout_o": got[1]}
        win = jnp.concatenate([got[0][k] for k in range(N_CHIPS)], axis=1)
        return {"wqkv_e": win[:, :3 * D], "wf": jnp.pad(win[:, 3 * D:], ((0, 0), (0, LANES - n_fox))),
                "bf": jnp.pad(even_b_forget, ((0, 0), (0, LANES - n_fox))), "wout_e": got[1]}

    pending, shards = [], {}

    def finish(after):
        block, pair, lands, ssem, rsem = pending.pop(0)
        tag = f"{block[0]}_l{block[1]}"
        pair, lands = _scatter_wait("rs_chip_wait_" + tag, pair, lands, ssem, rsem, after)
        shards[block] = [_chip_sum(f"rs_chip_add_{tag}_{a}", p, ld, pos) for a, (p, ld) in enumerate(zip(pair, lands))]

    def emit(block, mats):
        blk, l = block
        tag = f"{blk}_l{l}"
        kinds = ["cols", "rows"]
        if blk == "mix" and l == 0:
            dwqkv, dwout, dwf = mats
            dwin = jnp.concatenate([dwqkv, dwf[:, :n_fox]], axis=1)
            mats = [jnp.stack([dwin[:, k * wc:(k + 1) * wc] for k in range(N_CHIPS)]), dwout]
            kinds = ["lead", "rows"]
        elif blk == "mix":
            mats = list(mats[:2])
        else:
            mats = list(mats)
        landed = _send_to_sibling("rs_pair_send_" + tag, mats, kinds)
        pair = [_region_add(f"rs_pair_add_{tag}_{a}", m, kd, ld, pos[1:])
                for a, (m, kd, ld) in enumerate(zip(mats, kinds, landed))]
        pair, lands, ssem, rsem, token = _scatter_start("rs_chip_start_" + tag, pair)
        if pending:
            finish(token)
        pending.append((block, pair, lands, ssem, rsem))
        return token

    loss_vec, grad_x, g = _local_step(x[0], loss_target[0], fetch, fs, n_heads, n_sb, emit=emit)
    finish(grad_x)
    order = [(b, l) for b in ("ffn1", "ffn2", "mix") for l in range(L)]
    red = _swap_halves("rs_swap_halves", [s for b in order for s in shards[b]])
    red = {b: red[2 * i:2 * i + 2] for i, b in enumerate(order)}
    gu1, gd1 = [red[("ffn1", l)][0] for l in range(L)], [red[("ffn1", l)][1] for l in range(L)]
    gu2, gd2 = [red[("ffn2", l)][0] for l in range(L)], [red[("ffn2", l)][1] for l in range(L)]
    (g_win, g_wout_e), (g_qkv_o, g_wout_o) = red[("mix", 0)], red[("mix", 1)]

    small_rows = [g["dnorm"][l][i] for l in range(L) for i in range(3)] + [
        g["dfinal"], jnp.pad(g["db"], ((0, 0), (0, D - LANES))), jnp.pad(loss_vec, ((0, 0), (0, D - LANES)))]
    small = jnp.concatenate(small_rows + [jnp.zeros((16 - len(small_rows), D), F32)], axis=0)
    small_sum = _sum_leading("small_sum", _gather_all_devices("small_gather", small))
    loss = small_sum[3 * L + 2, 0]
    g_norm = lax.dynamic_slice_in_dim(small_sum[:3 * L].reshape(L, 3, D), chip * ns, ns, axis=2)
    g_final = small_sum[3 * L]
    g_bf = small_sum[3 * L + 1, :n_fox][None, :]

    def matrix(tag, wt, mt, vt, srcs):
        outs = None
        for l, (g_src, g_col) in enumerate(srcs):
            outs = _adamw_layer(f"adamw_{tag}_l{l}", wt, mt, vt, g_src, g_col, l, outs)
        return tuple(outs)

    def small(tag, wt, gt, mt, vt):
        return (gt,) + _adamw("adamw_" + tag, wt, gt, mt, vt)

    results = [
        small("norm_g", norm_g, g_norm, m_norm_g, v_norm_g),
        matrix("ffn1_gate", ffn1_w_gate, m_ffn1_w_gate, v_ffn1_w_gate, [(t, 0) for t in gu1]),
        matrix("ffn1_up", ffn1_w_up, m_ffn1_w_up, v_ffn1_w_up, [(t, 1) for t in gu1]),
        matrix("ffn1_down", ffn1_w_down, m_ffn1_w_down, v_ffn1_w_down, [(t, 0) for t in gd1]),
        matrix("ffn2_gate", ffn2_w_gate, m_ffn2_w_gate, v_ffn2_w_gate, [(t, 0) for t in gu2]),
        matrix("ffn2_up", ffn2_w_up, m_ffn2_w_up, v_ffn2_w_up, [(t, 1) for t in gu2]),
        matrix("ffn2_down", ffn2_w_down, m_ffn2_w_down, v_ffn2_w_down, [(t, 0) for t in gd2]),
        matrix("even_w_in", even_w_in, m_even_w_in, v_even_w_in, [(g_win, 0)]),
        small("b_forget", even_b_forget, g_bf, m_even_b_forget, v_even_b_forget),
        matrix("even_w_out", even_w_out, m_even_w_out, v_even_w_out, [(g_wout_e, 0)]),
        matrix("odd_w_qkv", odd_w_qkv, m_odd_w_qkv, v_odd_w_qkv, [(g_qkv_o, 0)]),
        matrix("odd_w_out", odd_w_out, m_odd_w_out, v_odd_w_out, [(g_wout_o, 0)]),
        small("final_norm_g", final_norm_g, g_final, m_final_norm_g, v_final_norm_g)]
    return (loss, grad_x[None], *[r[0] for r in results], *[r[1] for r in results],
            *[r[2] for r in results], *[r[3] for r in results])
```

```python
import functools
import math

import jax
import jax.numpy as jnp
from jax import lax
from jax.experimental import pallas as pl
from jax.experimental.pallas import tpu as pltpu

F32 = jnp.float32
BF16 = jnp.bfloat16

HEAD_DIM = 128
ROPE_DIMS = 32
ROPE_THETA = 500000.0
DILATED_PATTERNS = ((128, 1), (512, 4), (2048, 16))
RMS_EPS = 1e-6
NEG_INF = -1e30
ADAM_LR = 0.001
ADAM_B1 = 0.9
ADAM_B2 = 0.999
ADAM_EPS = 1e-08
ADAM_WD = 0.01
ADAM_STEP = 10

N_CHIPS = 4
N_CORES = 2
LANES = 128
BLK = 256
VMEM_BYTES_V7X = 64 * 2**20
MESH = pl.DeviceIdType.MESH


def _vmem_limit(block_bytes, scratch_bytes=0):
    need = 2 * block_bytes + scratch_bytes + 12 * 2**20
    return int(min(need, VMEM_BYTES_V7X - 6 * 2**20))


def _nbytes(shape, dtype):
    return math.prod(shape) * jnp.dtype(dtype).itemsize


def _tile(dim, target):
    best = None
    for t in range(LANES, min(dim, target) + 1, LANES):
        if dim % t == 0:
            best = t
    assert best is not None, (dim, target)
    return best


def _row_tile(rows, target, step=8):
    if rows <= target:
        return rows
    best = None
    for t in range(step, target + 1, step):
        if rows % t == 0:
            best = t
    assert best is not None, (rows, target)
    return best


def _mm(name, a, b, mode, out_dtype, res=None, alpha=1.0, after=None, tm_target=1024, tn_target=1536, tk_target=2048):
    a3 = a.ndim == 3
    b3 = b.ndim == 3
    if mode == "nn":
        assert not a3 and not b3
        (M, K), (K2, N) = a.shape, b.shape
    elif mode == "nt":
        assert not b3
        if a3:
            P, M, Kp = a.shape
            K = P * Kp
        else:
            M, K = a.shape
        N, K2 = b.shape
    else:
        assert mode == "tn" and not a3
        K, M = a.shape
        if b3:
            P, K2, Np = b.shape
            N = P * Np
        else:
            K2, N = b.shape
    assert K == K2, (name, a.shape, b.shape)
    tm = _tile(M, tm_target)
    tn = _tile(Np if b3 else N, tn_target)
    tk = _tile(Kp if a3 else K, tk_target)
    nk = K // tk
    grid = (M // tm, N // tn, nk)

    if mode == "nn":
        a_spec = pl.BlockSpec((tm, tk), lambda i, j, k: (i, k))
        b_spec = pl.BlockSpec((tk, tn), lambda i, j, k: (k, j))
        dims = (((1,), (0,)), ((), ()))
    elif mode == "nt":
        if a3:
            nkp = Kp // tk
            a_spec = pl.BlockSpec((None, tm, tk), lambda i, j, k: (k // nkp, i, k % nkp))
        else:
            a_spec = pl.BlockSpec((tm, tk), lambda i, j, k: (i, k))
        b_spec = pl.BlockSpec((tn, tk), lambda i, j, k: (j, k))
        dims = (((1,), (1,)), ((), ()))
    else:
        a_spec = pl.BlockSpec((tk, tm), lambda i, j, k: (k, i))
        if b3:
            njp = Np // tn
            b_spec = pl.BlockSpec((None, tk, tn), lambda i, j, k: (j // njp, k, j % njp))
        else:
            b_spec = pl.BlockSpec((tk, tn), lambda i, j, k: (k, j))
        dims = (((0,), (0,)), ((), ()))
    o_spec = pl.BlockSpec((tm, tn), lambda i, j, k: (i, j))
    has_res = res is not None

    def finish(y, r_ref, o_ref):
        if alpha != 1.0:
            y = y * alpha
        if has_res:
            y = r_ref[...] + y
        o_ref[...] = y.astype(o_ref.dtype)

    n_in = 2 + has_res + (after is not None)

    def body(*refs):
        a_ref, b_ref = refs[:2]
        r_ref = refs[2] if has_res else None
        o_ref = refs[n_in]
        part = lax.dot_general(a_ref[...], b_ref[...], dims, preferred_element_type=F32)
        if nk == 1:
            finish(part, r_ref, o_ref)
            return
        acc_ref = refs[-1]
        k = pl.program_id(2)

        @pl.when(k == 0)
        def _():
            acc_ref[...] = part

        @pl.when(k > 0)
        def _():
            acc_ref[...] += part

        @pl.when(k == nk - 1)
        def _():
            finish(acc_ref[...], r_ref, o_ref)

    in_specs = [a_spec, b_spec] + ([o_spec] if has_res else []) + ([_ANY] if after is not None else [])
    args = (a, b) + ((res,) if has_res else ()) + ((after,) if after is not None else ())
    blk = (_nbytes((tm, tk), a.dtype) + _nbytes((tk, tn), b.dtype) + _nbytes((tm, tn), out_dtype)
           + (_nbytes((tm, tn), F32) if has_res else 0))
    return pl.pallas_call(
        body, name=name, grid=grid, in_specs=in_specs, out_specs=o_spec,
        out_shape=jax.ShapeDtypeStruct((M, N), out_dtype),
        scratch_shapes=[pltpu.VMEM((tm, tn), F32)] if nk > 1 else [],
        compiler_params=pltpu.CompilerParams(
            dimension_semantics=("parallel", "parallel", "arbitrary"),
            vmem_limit_bytes=_vmem_limit(blk, 2 * _nbytes((tm, tn), F32))),
    )(*args)


def _rms_fwd(name, x, g):
    S, D = x.shape
    tr = _row_tile(S, 256)

    def body(x_ref, g_ref, n_ref):
        xv = x_ref[...]
        r = lax.rsqrt(jnp.mean(xv * xv, axis=-1, keepdims=True) + RMS_EPS)
        n_ref[...] = (xv * r * g_ref[...]).astype(BF16)

    return pl.pallas_call(
        body, name=name, grid=(S // tr,),
        in_specs=[pl.BlockSpec((tr, D), lambda i: (i, 0)), pl.BlockSpec((1, D), lambda i: (0, 0))],
        out_specs=pl.BlockSpec((tr, D), lambda i: (i, 0)),
        out_shape=jax.ShapeDtypeStruct((S, D), BF16),
        compiler_params=pltpu.CompilerParams(dimension_semantics=("parallel",)),
    )(x, g)


def _rms_bwd(name, dn, x, g, dres):
    S, D = x.shape
    tr = _row_tile(S, 256)

    def body(dn_ref, x_ref, g_ref, dres_ref, dx_ref, dxb_ref, dg_ref):
        i = pl.program_id(0)
        xv = x_ref[...]
        dnv = dn_ref[...]
        r = lax.rsqrt(jnp.mean(xv * xv, axis=-1, keepdims=True) + RMS_EPS)
        u = dnv * g_ref[...]
        dot = jnp.mean(u * xv, axis=-1, keepdims=True)
        dx = dres_ref[...] + r * u - xv * (r * r * r * dot)
        dx_ref[...] = dx
        dxb_ref[...] = dx.astype(BF16)

        @pl.when(i == 0)
        def _():
            dg_ref[...] = jnp.zeros_like(dg_ref)

        dg_ref[...] += jnp.sum(dnv * xv * r, axis=0, keepdims=True)

    row = pl.BlockSpec((tr, D), lambda i: (i, 0))
    vec = pl.BlockSpec((1, D), lambda i: (0, 0))
    return pl.pallas_call(
        body, name=name, grid=(S // tr,),
        in_specs=[row, row, vec, row], out_specs=[row, row, vec],
        out_shape=[jax.ShapeDtypeStruct((S, D), F32), jax.ShapeDtypeStruct((S, D), BF16),
                   jax.ShapeDtypeStruct((1, D), F32)],
        compiler_params=pltpu.CompilerParams(dimension_semantics=("arbitrary",)),
    )(dn, x, g, dres)


def _loss_head(name, x, g, target):
    S, D = x.shape
    tr = _row_tile(S, 256)

    def body(x_ref, g_ref, t_ref, dx_ref, dxb_ref, dg_ref, loss_ref):
        i = pl.program_id(0)
        xv = x_ref[...]
        gv = g_ref[...]
        r = lax.rsqrt(jnp.mean(xv * xv, axis=-1, keepdims=True) + RMS_EPS)
        diff = xv * r * gv - t_ref[...]
        part = 0.5 * jnp.sum(jnp.mean(diff * diff, axis=-1, keepdims=True), axis=0, keepdims=True)
        dy = diff * (1.0 / D)
        u = dy * gv
        dot = jnp.mean(u * xv, axis=-1, keepdims=True)
        dx = r * u - xv * (r * r * r * dot)
        dx_ref[...] = dx
        dxb_ref[...] = dx.astype(BF16)

        @pl.when(i == 0)
        def _():
            dg_ref[...] = jnp.zeros_like(dg_ref)
            loss_ref[...] = jnp.zeros_like(loss_ref)

        dg_ref[...] += jnp.sum(dy * xv * r, axis=0, keepdims=True)
        loss_ref[...] += jnp.broadcast_to(part, loss_ref.shape)

    row = pl.BlockSpec((tr, D), lambda i: (i, 0))
    vec = pl.BlockSpec((1, D), lambda i: (0, 0))
    lvec = pl.BlockSpec((1, LANES), lambda i: (0, 0))
    return pl.pallas_call(
        body, name=name, grid=(S // tr,),
        in_specs=[row, vec, row], out_specs=[row, row, vec, lvec],
        out_shape=[jax.ShapeDtypeStruct((S, D), F32), jax.ShapeDtypeStruct((S, D), BF16),
                   jax.ShapeDtypeStruct((1, D), F32), jax.ShapeDtypeStruct((1, LANES), F32)],
        compiler_params=pltpu.CompilerParams(dimension_semantics=("arbitrary",)),
    )(x, g, target)


def _ffn_up(name, n, wgu, fs, tm_target=512):
    S, D = n.shape
    nslab = wgu.shape[1] // (2 * fs)
    tm = _tile(S, tm_target)

    def body(n_ref, w_ref, gu_ref, h_ref):
        y = jnp.dot(n_ref[...], w_ref[...], preferred_element_type=F32)
        gu_ref[...] = y
        gv = y[:, :fs]
        h_ref[...] = (gv * jax.nn.sigmoid(gv) * y[:, fs:]).astype(BF16)

    blk = _nbytes((tm, D), BF16) + _nbytes((D, 2 * fs), BF16) + _nbytes((tm, 2 * fs), F32) + _nbytes((tm, fs), BF16)
    return pl.pallas_call(
        body, name=name, grid=(nslab, S // tm),
        in_specs=[pl.BlockSpec((tm, D), lambda k, i: (i, 0)), pl.BlockSpec((D, 2 * fs), lambda k, i: (0, k))],
        out_specs=[pl.BlockSpec((tm, 2 * fs), lambda k, i: (i, k)), pl.BlockSpec((tm, fs), lambda k, i: (i, k))],
        out_shape=[jax.ShapeDtypeStruct((S, nslab * 2 * fs), F32), jax.ShapeDtypeStruct((S, nslab * fs), BF16)],
        compiler_params=pltpu.CompilerParams(dimension_semantics=("parallel", "parallel"),
                                             vmem_limit_bytes=_vmem_limit(blk, _nbytes((tm, 2 * fs), F32))),
    )(n, wgu)


def _ffn_dact(name, dyb, wd, gu, fs, alpha, after=None, tm_target=512):
    S, D = dyb.shape
    nslab = wd.shape[0] // fs
    tm = _tile(S, tm_target)

    def body(*refs):
        d_ref, w_ref, gu_ref = refs[:3]
        o_ref = refs[-1]
        dhv = _dot_nt(d_ref[...], w_ref[...]) * alpha
        gv = gu_ref[:, :fs]
        uv = gu_ref[:, fs:]
        sg = jax.nn.sigmoid(gv)
        silu = gv * sg
        o_ref[:, :fs] = (dhv * uv * (sg + silu * (1.0 - sg))).astype(BF16)
        o_ref[:, fs:] = (dhv * silu).astype(BF16)

    in_specs = [pl.BlockSpec((tm, D), lambda k, i: (i, 0)), pl.BlockSpec((fs, D), lambda k, i: (k, 0)),
                pl.BlockSpec((tm, 2 * fs), lambda k, i: (i, k))] + ([_ANY] if after is not None else [])
    args = (dyb, wd, gu) + ((after,) if after is not None else ())
    blk = _nbytes((tm, D), BF16) + _nbytes((fs, D), BF16) + _nbytes((tm, 2 * fs), F32) + _nbytes((tm, 2 * fs), BF16)
    return pl.pallas_call(
        body, name=name, grid=(nslab, S // tm), in_specs=in_specs,
        out_specs=pl.BlockSpec((tm, 2 * fs), lambda k, i: (i, k)),
        out_shape=jax.ShapeDtypeStruct((S, nslab * 2 * fs), BF16),
        compiler_params=pltpu.CompilerParams(dimension_semantics=("parallel", "parallel"),
                                             vmem_limit_bytes=_vmem_limit(blk, 2 * _nbytes((tm, fs), F32))),
    )(*args)


def _tri_rows(r0, nrows, ncols, lower):
    row = lax.broadcasted_iota(jnp.int32, (nrows, ncols), 0) + r0
    col = lax.broadcasted_iota(jnp.int32, (nrows, ncols), 1)
    return jnp.where((col <= row) if lower else (col >= row), 1.0, 0.0).astype(F32)


def _gate_fwd(name, hf, b):
    S = hf.shape[0]
    tb = _row_tile(S, 256)

    def body(hf_ref, b_ref, cf_ref, cft_ref, lf_ref):
        zz = hf_ref[...] + b_ref[...]
        lf_ref[...] = jnp.minimum(zz, 0.0) - jnp.log1p(jnp.exp(-jnp.abs(zz)))

        def blk(i, c):
            r0 = pl.multiple_of(i * tb, tb)
            tri = _tri_rows(r0, tb, S, True)
            cf_ref[pl.ds(r0, tb), :] = jnp.dot(tri, lf_ref[...], precision=lax.Precision.HIGHEST,
                                               preferred_element_type=F32)
            return c

        lax.fori_loop(0, S // tb, blk, 0)
        cft_ref[...] = cf_ref[...].T

    full = pl.BlockSpec((S, LANES), lambda: (0, 0))
    return pl.pallas_call(
        body, name=name, in_specs=[full, pl.BlockSpec((1, LANES), lambda: (0, 0))],
        out_specs=[full, pl.BlockSpec((LANES, S), lambda: (0, 0))],
        out_shape=[jax.ShapeDtypeStruct((S, LANES), F32), jax.ShapeDtypeStruct((LANES, S), F32)],
        scratch_shapes=[pltpu.VMEM((S, LANES), F32)],
    )(hf, b)


def _gate_bwd(name, dcft, drow, hf, b):
    S = hf.shape[0]
    tb = _row_tile(S, 256)

    def body(dcft_ref, drow_ref, hf_ref, b_ref, dhf_ref, db_ref, dcf_ref, dlf_ref):
        dcf_ref[...] = dcft_ref[...].T + drow_ref[...]

        def blk(i, c):
            r0 = pl.multiple_of(i * tb, tb)
            tri = _tri_rows(r0, tb, S, False)
            dlf_ref[pl.ds(r0, tb), :] = jnp.dot(tri, dcf_ref[...], precision=lax.Precision.HIGHEST,
                                                preferred_element_type=F32)
            return c

        lax.fori_loop(0, S // tb, blk, 0)
        zz = hf_ref[...] + b_ref[...]
        dhf = dlf_ref[...] * jax.nn.sigmoid(-zz)
        dhf_ref[...] = dhf.astype(BF16)
        db_ref[...] = jnp.sum(dhf, axis=0, keepdims=True)

    full = pl.BlockSpec((S, LANES), lambda: (0, 0))
    vec = pl.BlockSpec((1, LANES), lambda: (0, 0))
    return pl.pallas_call(
        body, name=name, in_specs=[pl.BlockSpec((LANES, S), lambda: (0, 0)), full, full, vec],
        out_specs=[full, vec],
        out_shape=[jax.ShapeDtypeStruct((S, LANES), BF16), jax.ShapeDtypeStruct((1, LANES), F32)],
        scratch_shapes=[pltpu.VMEM((S, LANES), F32), pltpu.VMEM((S, LANES), F32)],
    )(dcft, drow, hf, b)


def _rope_tables(S):
    half = ROPE_DIMS // 2
    freqs = ROPE_THETA ** (-jnp.arange(half, dtype=F32) / half)
    ang = jnp.arange(S, dtype=F32)[:, None] * freqs[None, :]
    cos, sin = jnp.cos(ang), jnp.sin(ang)
    pad = HEAD_DIM - ROPE_DIMS
    c = jnp.concatenate([cos, cos, jnp.ones((S, pad), F32)], axis=1)
    s = jnp.concatenate([-sin, sin, jnp.zeros((S, pad), F32)], axis=1)
    return c, s


def _rope_swap(x):
    half = ROPE_DIMS // 2
    lane = lax.broadcasted_iota(jnp.int32, x.shape, 1)
    upper = jnp.where(lane < ROPE_DIMS, pltpu.roll(x, half, 1), 0.0)
    return jnp.where(lane < half, pltpu.roll(x, HEAD_DIM - half, 1), upper)


def _rope(x, c, s):
    return x * c + _rope_swap(x) * s


def _rope_t(dy, c, s):
    return dy * c + _rope_swap(dy * s)


def _split_dot(x, t):
    hi = x.astype(BF16)
    lo = (x - hi.astype(F32)).astype(BF16)
    return (jnp.dot(hi, t, preferred_element_type=F32) + jnp.dot(lo, t, preferred_element_type=F32))


_NT = (((1,), (1,)), ((), ()))
_TN = (((0,), (0,)), ((), ()))


def _dot_nt(a, b):
    return lax.dot_general(a, b, _NT, preferred_element_type=F32)


def _dot_tn(a, b):
    return lax.dot_general(a, b, _TN, preferred_element_type=F32)


def _blk(i):
    return pl.ds(pl.multiple_of(i * BLK, BLK), BLK)


def _delta(i, j):
    row = lax.broadcasted_iota(jnp.int32, (BLK, BLK), 0)
    col = lax.broadcasted_iota(jnp.int32, (BLK, BLK), 1)
    return (row - col) + (i - j) * BLK


def _dilated_mult(delta):
    c = jnp.zeros(delta.shape, F32)
    for window, dil in DILATED_PATTERNS:
        ok = (delta >= 0) & (delta <= window) & ((delta & (dil - 1)) == 0)
        c = c + jnp.where(ok, 1.0, 0.0)
    return c


def _sb_terms(z, mask, t_ex, run):
    t = jnp.log1p(jnp.exp(-jnp.abs(z)))
    lsig = jnp.minimum(z, 0.0) - t
    m = jnp.where(mask, -(jnp.maximum(z, 0.0) + t), 0.0)
    after = _split_dot(m, t_ex)
    a = jnp.where(mask, jnp.exp(lsig + after + run), 0.0)
    return a, m, lsig


def _attn_fwd(name, hq, layer_kind, n_heads, n_sb, cf=None, cft=None, rope_c=None, rope_s=None):
    S = hq.shape[0]
    D = n_heads * HEAD_DIM
    nq = S // BLK
    scale = HEAD_DIM ** -0.5
    even = layer_kind == "even"

    def body(*refs):
        if even:
            q_ref, k_ref, v_ref, cf_ref, cft_ref, o_ref, ob_ref, lse_ref, qs, ks, vs = refs
        else:
            q_ref, k_ref, v_ref, c_ref, s_ref, o_ref, ob_ref, lse_ref, qs, ks, vs = refs
        h = pl.program_id(0)
        if even:
            qs[...] = q_ref[...].astype(BF16)
            ks[...] = k_ref[...].astype(BF16)
        else:
            qs[...] = _rope(q_ref[...], c_ref[...], s_ref[...]).astype(BF16)
            ks[...] = _rope(k_ref[...], c_ref[...], s_ref[...]).astype(BF16)
        vs[...] = v_ref[...].astype(BF16)

        def softmax_head(hh):
            def qblock(i, carry):
                qi = qs[_blk(i), :]
                if even:
                    lane = lax.broadcasted_iota(jnp.int32, (BLK, LANES), 1)
                    cfq = jnp.sum(jnp.where(lane == hh, cf_ref[_blk(i), :], 0.0), axis=1, keepdims=True)

                def kblock(j, c):
                    m_run, l_run, acc = c
                    z = _dot_nt(qi, ks[_blk(j), :]) * scale
                    delta = _delta(i, j)
                    if even:
                        z = z + cfq - cft_ref[hh, :, _blk(j)]
                        ok = delta >= 0
                    else:
                        mult = _dilated_mult(delta)
                        ok = mult > 0.0
                    z = jnp.where(ok, z, NEG_INF)
                    m_new = jnp.maximum(m_run, jnp.max(z, axis=1, keepdims=True))
                    p = jnp.exp(z - m_new)
                    if not even:
                        p = p * mult
                    alpha = jnp.exp(m_run - m_new)
                    l_new = alpha * l_run + jnp.sum(p, axis=1, keepdims=True)
                    acc = alpha * acc + jnp.dot(p.astype(BF16), vs[_blk(j), :], preferred_element_type=F32)
                    return m_new, l_new, acc

                init = (jnp.full((BLK, 1), NEG_INF, F32), jnp.zeros((BLK, 1), F32), jnp.zeros((BLK, HEAD_DIM), F32))
                m_run, l_run, acc = lax.fori_loop(0, i + 1, kblock, init)
                o = acc / l_run
                o_ref[_blk(i), :] = o
                ob_ref[_blk(i), :] = o.astype(BF16)
                lse_ref[_blk(i), :] = jnp.broadcast_to(m_run + jnp.log(l_run), (BLK, HEAD_DIM))
                return carry

            lax.fori_loop(0, nq, qblock, 0)

        def sb_head():
            row = lax.broadcasted_iota(jnp.int32, (BLK, BLK), 0)
            col = lax.broadcasted_iota(jnp.int32, (BLK, BLK), 1)
            t_ex = jnp.where(row > col, 1.0, 0.0).astype(BF16)

            def qblock(i, carry):
                qi = qs[_blk(i), :]

                def kblock(jj, c):
                    run, acc = c
                    j = i - jj
                    z = _dot_nt(qi, ks[_blk(j), :]) * scale
                    a, m, _ = _sb_terms(z, _delta(i, j) > 0, t_ex, run)
                    acc = acc + jnp.dot(a.astype(BF16), vs[_blk(j), :], preferred_element_type=F32)
                    return run + jnp.sum(m, axis=1, keepdims=True), acc

                init = (jnp.zeros((BLK, 1), F32), jnp.zeros((BLK, HEAD_DIM), F32))
                _, acc = lax.fori_loop(0, i + 1, kblock, init)
                o_ref[_blk(i), :] = acc
                ob_ref[_blk(i), :] = acc.astype(BF16)
                lse_ref[_blk(i), :] = jnp.zeros((BLK, HEAD_DIM), F32)
                return carry

            lax.fori_loop(0, nq, qblock, 0)

        if even:
            @pl.when(h < n_sb)
            def _():
                sb_head()

            @pl.when(h >= n_sb)
            def _():
                softmax_head(h - n_sb)
        else:
            softmax_head(h)

    head = lambda off: pl.BlockSpec((S, HEAD_DIM), lambda h, off=off: (0, off + h))
    full = pl.BlockSpec((S, LANES), lambda h: (0, 0))
    if even:
        extra_specs = [full, pl.BlockSpec(cft.shape, lambda h: (0, 0, 0))]
        extra = (cf, cft)
    else:
        extra_specs = [full, full]
        extra = (rope_c, rope_s)
    blk_bytes = 8 * _nbytes((S, HEAD_DIM), F32)
    return pl.pallas_call(
        body, name=name, grid=(n_heads,),
        in_specs=[head(0), head(n_heads), head(2 * n_heads)] + extra_specs,
        out_specs=[head(0), head(0), head(0)],
        out_shape=[jax.ShapeDtypeStruct((S, D), F32), jax.ShapeDtypeStruct((S, D), BF16),
                   jax.ShapeDtypeStruct((S, D), F32)],
        scratch_shapes=[pltpu.VMEM((S, HEAD_DIM), BF16)] * 3,
        compiler_params=pltpu.CompilerParams(dimension_semantics=("arbitrary",),
                                             vmem_limit_bytes=_vmem_limit(blk_bytes, 3 * _nbytes((S, HEAD_DIM), BF16))),
    )(hq, hq, hq, *extra)


def _attn_bwd(name, hq, do, o, lse, layer_kind, n_heads, n_sb, cf=None, cft=None, rope_c=None, rope_s=None):
    S = hq.shape[0]
    D = n_heads * HEAD_DIM
    nq = S // BLK
    scale = HEAD_DIM ** -0.5
    even = layer_kind == "even"

    def body(*refs):
        if even:
            (q_ref, k_ref, v_ref, do_ref, o_ref, lse_ref, cf_ref, cft_ref,
             dh_ref, dcft_ref, drow_ref, qs, ks, vs, dos, dq_acc, dk_acc, dv_acc) = refs
        else:
            (q_ref, k_ref, v_ref, do_ref, o_ref, lse_ref, c_ref, s_ref,
             dh_ref, qs, ks, vs, dos, dq_acc, dk_acc, dv_acc) = refs
        h = pl.program_id(0)
        if even:
            qs[...] = q_ref[...].astype(BF16)
            ks[...] = k_ref[...].astype(BF16)

            @pl.when(h == 0)
            def _():
                dcft_ref[...] = jnp.zeros_like(dcft_ref)
                drow_ref[...] = jnp.zeros_like(drow_ref)
        else:
            qs[...] = _rope(q_ref[...], c_ref[...], s_ref[...]).astype(BF16)
            ks[...] = _rope(k_ref[...], c_ref[...], s_ref[...]).astype(BF16)
        vs[...] = v_ref[...].astype(BF16)
        dos[...] = do_ref[...].astype(BF16)
        dk_acc[...] = jnp.zeros_like(dk_acc)
        dv_acc[...] = jnp.zeros_like(dv_acc)

        def softmax_head(hh):
            def qblock(i, carry):
                qi = qs[_blk(i), :]
                doi = dos[_blk(i), :]
                dvec = jnp.sum(do_ref[_blk(i), :] * o_ref[_blk(i), :], axis=1, keepdims=True)
                lse_i = jnp.max(lse_ref[_blk(i), :], axis=1, keepdims=True)
                if even:
                    lane = lax.broadcasted_iota(jnp.int32, (BLK, LANES), 1)
                    cfq = jnp.sum(jnp.where(lane == hh, cf_ref[_blk(i), :], 0.0), axis=1, keepdims=True)

                def kblock(j, c):
                    dq, ds_rows = c
                    kj = ks[_blk(j), :]
                    z = _dot_nt(qi, kj) * scale
                    delta = _delta(i, j)
                    if even:
                        z = z + cfq - cft_ref[hh, :, _blk(j)]
                        ok = delta >= 0
                    else:
                        mult = _dilated_mult(delta)
                        ok = mult > 0.0
                    p = jnp.exp(jnp.where(ok, z, NEG_INF) - lse_i)
                    if not even:
                        p = p * mult
                    dp = _dot_nt(doi, vs[_blk(j), :])
                    ds = p * (dp - dvec)
                    dsb = (ds * scale).astype(BF16)
                    dk_acc[_blk(j), :] += _dot_tn(dsb, qi)
                    dv_acc[_blk(j), :] += _dot_tn(p.astype(BF16), doi)
                    if even:
                        dcft_ref[hh, :, _blk(j)] += -jnp.sum(ds, axis=0, keepdims=True)
                    return (dq + jnp.dot(dsb, kj, preferred_element_type=F32),
                            ds_rows + jnp.sum(ds, axis=1, keepdims=True))

                dq, ds_rows = lax.fori_loop(0, i + 1, kblock,
                                            (jnp.zeros((BLK, HEAD_DIM), F32), jnp.zeros((BLK, 1), F32)))
                dq_acc[_blk(i), :] = dq
                if even:
                    drow_ref[_blk(i), :] += jnp.where(lane == hh, ds_rows, 0.0)
                return carry

            lax.fori_loop(0, nq, qblock, 0)

        def sb_head():
            row = lax.broadcasted_iota(jnp.int32, (BLK, BLK), 0)
            col = lax.broadcasted_iota(jnp.int32, (BLK, BLK), 1)
            t_ex = jnp.where(row > col, 1.0, 0.0).astype(BF16)
            t_in = jnp.where(row >= col, 1.0, 0.0).astype(BF16)

            def qblock(i, carry):
                qi = qs[_blk(i), :]
                doi = dos[_blk(i), :]

                def e_total(jj, c):
                    run, tot = c
                    j = i - jj
                    z = _dot_nt(qi, ks[_blk(j), :]) * scale
                    a, m, _ = _sb_terms(z, _delta(i, j) > 0, t_ex, run)
                    e = _dot_nt(doi, vs[_blk(j), :]) * a
                    return run + jnp.sum(m, axis=1, keepdims=True), tot + jnp.sum(e, axis=1, keepdims=True)

                zero = jnp.zeros((BLK, 1), F32)
                _, e_tot = lax.fori_loop(0, i + 1, e_total, (zero, zero))

                def kblock(jj, c):
                    run, e_run, dq = c
                    j = i - jj
                    kj = ks[_blk(j), :]
                    z = _dot_nt(qi, kj) * scale
                    mask = _delta(i, j) > 0
                    a, m, lsig = _sb_terms(z, mask, t_ex, run)
                    sig = jnp.exp(lsig)
                    e = _dot_nt(doi, vs[_blk(j), :]) * a
                    e_before = e_tot - (_split_dot(e, t_in) + e_run)
                    dz = jnp.where(mask, e * (1.0 - sig) - sig * e_before, 0.0)
                    dzb = (dz * scale).astype(BF16)
                    dk_acc[_blk(j), :] += _dot_tn(dzb, qi)
                    dv_acc[_blk(j), :] += _dot_tn(a.astype(BF16), doi)
                    return (run + jnp.sum(m, axis=1, keepdims=True), e_run + jnp.sum(e, axis=1, keepdims=True),
                            dq + jnp.dot(dzb, kj, preferred_element_type=F32))

                _, _, dq = lax.fori_loop(0, i + 1, kblock, (zero, zero, jnp.zeros((BLK, HEAD_DIM), F32)))
                dq_acc[_blk(i), :] = dq
                return carry

            lax.fori_loop(0, nq, qblock, 0)

        if even:
            @pl.when(h < n_sb)
            def _():
                sb_head()

            @pl.when(h >= n_sb)
            def _():
                softmax_head(h - n_sb)

            dh_ref[0] = dq_acc[...].astype(BF16)
            dh_ref[1] = dk_acc[...].astype(BF16)
        else:
            softmax_head(h)
            dh_ref[0] = _rope_t(dq_acc[...], c_ref[...], s_ref[...]).astype(BF16)
            dh_ref[1] = _rope_t(dk_acc[...], c_ref[...], s_ref[...]).astype(BF16)
        dh_ref[2] = dv_acc[...].astype(BF16)

    head = lambda off: pl.BlockSpec((S, HEAD_DIM), lambda h, off=off: (0, off + h))
    full = pl.BlockSpec((S, LANES), lambda h: (0, 0))
    tfull = pl.BlockSpec((n_heads - n_sb, 1, S), lambda h: (0, 0, 0))
    dh_spec = pl.BlockSpec((3, S, HEAD_DIM), lambda h: (0, 0, h))
    dh_shape = jax.ShapeDtypeStruct((3, S, D), BF16)
    if even:
        extra_specs, extra = [full, tfull], (cf, cft)
        out_specs = [dh_spec, tfull, full]
        out_shape = [dh_shape, jax.ShapeDtypeStruct((n_heads - n_sb, 1, S), F32),
                     jax.ShapeDtypeStruct((S, LANES), F32)]
    else:
        extra_specs, extra = [full, full], (rope_c, rope_s)
        out_specs = [dh_spec]
        out_shape = [dh_shape]
    blk_bytes = 10 * _nbytes((S, HEAD_DIM), F32)
    scratch_bytes = 4 * _nbytes((S, HEAD_DIM), BF16) + 3 * _nbytes((S, HEAD_DIM), F32)
    return pl.pallas_call(
        body, name=name, grid=(n_heads,),
        in_specs=[head(0), head(n_heads), head(2 * n_heads), head(0), head(0), head(0)] + extra_specs,
        out_specs=out_specs, out_shape=out_shape,
        scratch_shapes=[pltpu.VMEM((S, HEAD_DIM), BF16)] * 4 + [pltpu.VMEM((S, HEAD_DIM), F32)] * 3,
        compiler_params=pltpu.CompilerParams(dimension_semantics=("arbitrary",),
                                             vmem_limit_bytes=_vmem_limit(blk_bytes, scratch_bytes)),
    )(hq, hq, hq, do, o, lse, *extra)


def _query_block(S):
    return min(512, S)


def _offsets(d, bq):
    row = jnp.arange(bq, dtype=jnp.int32)[:, None]
    col = jnp.arange(BLK, dtype=jnp.int32)[None, :]
    return d * BLK + row - col


def _causal_tables(bq, strict):
    r = bq // BLK
    tabs = []
    for d in range(-(r - 1), 1):
        delta = _offsets(d, bq)
        tabs.append(jnp.where((delta > 0) if strict else (delta >= 0), 1.0, 0.0))
    tabs.append(jnp.ones((bq, BLK), F32))
    return jnp.stack(tabs).astype(F32)


def _dilated_tables(bq):
    r = bq // BLK
    limit = sorted(w for w, _ in DILATED_PATTERNS)[-2]
    assert all(BLK % dil == 0 for _, dil in DILATED_PATTERNS)
    d_far = -(-(limit + BLK) // BLK)
    tabs = []
    for d in range(-(r - 1), d_far + 1):
        mult = _dilated_mult(_offsets(d, bq))
        tabs.append(jnp.where(mult > 0, jnp.log(jnp.maximum(mult, 1.0)), NEG_INF))
    return jnp.stack(tabs).astype(F32)


def _qblk(i, bq):
    return pl.ds(pl.multiple_of(i * bq, bq), bq)


def _sb_block(z, valid, t_ex, run):
    t = jnp.log1p(jnp.exp(-jnp.abs(z)))
    lsig = jnp.minimum(z, 0.0) - t
    m = -(jnp.maximum(z, 0.0) + t) * valid
    after = _split_dot(m, t_ex)
    a = jnp.exp(lsig + after + run) * valid
    return a, m, lsig


def _attn_fwd_wide(name, hq, layer_kind, n_heads, n_sb, cf=None, cft=None, rope_c=None, rope_s=None):
    S = hq.shape[0]
    D = n_heads * HEAD_DIM
    bq = _query_block(S)
    r = bq // BLK
    nq = S // bq
    scale = HEAD_DIM ** -0.5
    even = layer_kind == "even"
    if even:
        tabs = (jnp.where(_causal_tables(bq, False) > 0, 0.0, NEG_INF), _causal_tables(bq, True))
    else:
        tabs = (_dilated_tables(bq),)
    n_tab = tabs[0].shape[0]

    def body(*refs):
        if even:
            q_ref, k_ref, v_ref, cf_ref, cft_ref, bias_ref, valid_ref, o_ref, ob_ref, lse_ref, qs, ks, vs = refs
        else:
            q_ref, k_ref, v_ref, c_ref, s_ref, bias_ref, o_ref, ob_ref, lse_ref, qs, ks, vs = refs
        h = pl.program_id(0)
        if even:
            qs[...] = q_ref[...].astype(BF16)
            ks[...] = k_ref[...].astype(BF16)
        else:
            qs[...] = _rope(q_ref[...], c_ref[...], s_ref[...]).astype(BF16)
            ks[...] = _rope(k_ref[...], c_ref[...], s_ref[...]).astype(BF16)
        vs[...] = v_ref[...].astype(BF16)

        def softmax_head(hh):
            def qblock(i, carry):
                qi = qs[_qblk(i, bq), :]
                if even:
                    lane = lax.broadcasted_iota(jnp.int32, (bq, LANES), 1)
                    cfq = jnp.sum(jnp.where(lane == hh, cf_ref[_qblk(i, bq), :], 0.0), axis=1, keepdims=True)

                def kblock(j, c):
                    m_run, l_run, acc = c
                    z = _dot_nt(qi, ks[_blk(j), :]) * scale + bias_ref[jnp.minimum(r * i - j + (r - 1), n_tab - 1)]
                    if even:
                        z = z + (cfq - cft_ref[hh, :, _blk(j)])
                    m_new = jnp.maximum(m_run, jnp.max(z, axis=1, keepdims=True))
                    p = jnp.exp(z - m_new)
                    alpha = jnp.exp(m_run - m_new)
                    l_new = alpha * l_run + jnp.sum(p, axis=1, keepdims=True)
                    acc = alpha * acc + jnp.dot(p.astype(BF16), vs[_blk(j), :], preferred_element_type=F32)
                    return m_new, l_new, acc

                init = (jnp.full((bq, 1), NEG_INF, F32), jnp.zeros((bq, 1), F32), jnp.zeros((bq, HEAD_DIM), F32))
                m_run, l_run, acc = lax.fori_loop(0, r * (i + 1), kblock, init)
                o = acc / l_run
                o_ref[_qblk(i, bq), :] = o
                ob_ref[_qblk(i, bq), :] = o.astype(BF16)
                lse_ref[_qblk(i, bq), :] = jnp.broadcast_to(m_run + jnp.log(l_run), (bq, HEAD_DIM))
                return carry

            lax.fori_loop(0, nq, qblock, 0)

        def sb_head():
            row = lax.broadcasted_iota(jnp.int32, (BLK, BLK), 0)
            col = lax.broadcasted_iota(jnp.int32, (BLK, BLK), 1)
            t_ex = jnp.where(row > col, 1.0, 0.0).astype(BF16)

            def qblock(i, carry):
                qi = qs[_qblk(i, bq), :]

                def kblock(jj, c):
                    run, acc, rest = c
                    j = r * (i + 1) - 1 - jj
                    z = _dot_nt(qi, ks[_blk(j), :]) * scale
                    a, m, _ = _sb_block(z, valid_ref[jnp.minimum(r * i - j + (r - 1), r)], t_ex, run)
                    vj = vs[_blk(j), :]
                    hi = a.astype(BF16)
                    lo = (a - hi.astype(F32)).astype(BF16)
                    acc = acc + jnp.dot(hi, vj, preferred_element_type=F32)
                    rest = rest + jnp.dot(lo, vj, preferred_element_type=F32)
                    return run + jnp.sum(m, axis=1, keepdims=True), acc, rest

                zero = jnp.zeros((bq, HEAD_DIM), F32)
                _, acc, rest = lax.fori_loop(0, r * (i + 1), kblock, (jnp.zeros((bq, 1), F32), zero, zero))
                o_ref[_qblk(i, bq), :] = acc + rest
                ob_ref[_qblk(i, bq), :] = acc.astype(BF16)
                lse_ref[_qblk(i, bq), :] = jnp.zeros((bq, HEAD_DIM), F32)
                return carry

            lax.fori_loop(0, nq, qblock, 0)

        if even:
            @pl.when(h < n_sb)
            def _():
                sb_head()

            @pl.when(h >= n_sb)
            def _():
                softmax_head(h - n_sb)
        else:
            softmax_head(h)

    head = lambda off: pl.BlockSpec((S, HEAD_DIM), lambda h, off=off: (0, off + h))
    full = pl.BlockSpec((S, LANES), lambda h: (0, 0))
    tab_specs = [pl.BlockSpec(t.shape, lambda h: (0, 0, 0)) for t in tabs]
    if even:
        extra_specs = [full, pl.BlockSpec(cft.shape, lambda h: (0, 0, 0))] + tab_specs
        extra = (cf, cft) + tabs
    else:
        extra_specs = [full, full] + tab_specs
        extra = (rope_c, rope_s) + tabs
    blk_bytes = 8 * _nbytes((S, HEAD_DIM), F32) + sum(_nbytes(t.shape, F32) for t in tabs)
    return pl.pallas_call(
        body, name=name, grid=(n_heads,),
        in_specs=[head(0), head(n_heads), head(2 * n_heads)] + extra_specs,
        out_specs=[head(0), head(0), head(0)],
        out_shape=[jax.ShapeDtypeStruct((S, D), F32), jax.ShapeDtypeStruct((S, D), BF16),
                   jax.ShapeDtypeStruct((S, D), F32)],
        scratch_shapes=[pltpu.VMEM((S, HEAD_DIM), BF16)] * 3,
        compiler_params=pltpu.CompilerParams(dimension_semantics=("arbitrary",),
                                             vmem_limit_bytes=_vmem_limit(blk_bytes, 3 * _nbytes((S, HEAD_DIM), BF16))),
    )(hq, hq, hq, *extra)


def _attn_bwd_wide(name, hq, do, o, lse, layer_kind, n_heads, n_sb, cf=None, cft=None, rope_c=None, rope_s=None):
    S = hq.shape[0]
    D = n_heads * HEAD_DIM
    bq = _query_block(S)
    r = bq // BLK
    nq = S // bq
    scale = HEAD_DIM ** -0.5
    even = layer_kind == "even"
    if even:
        tabs = (jnp.where(_causal_tables(bq, False) > 0, 0.0, NEG_INF), _causal_tables(bq, True))
    else:
        tabs = (_dilated_tables(bq),)
    n_tab = tabs[0].shape[0]

    def body(*refs):
        if even:
            (q_ref, k_ref, v_ref, do_ref, o_ref, lse_ref, cf_ref, cft_ref, bias_ref, valid_ref,
             dh_ref, dcft_ref, drow_ref, qs, ks, vs, dos, dq_acc, dk_acc, dv_acc) = refs
        else:
            (q_ref, k_ref, v_ref, do_ref, o_ref, lse_ref, c_ref, s_ref, bias_ref,
             dh_ref, qs, ks, vs, dos, dq_acc, dk_acc, dv_acc) = refs
        h = pl.program_id(0)
        if even:
            qs[...] = q_ref[...].astype(BF16)
            ks[...] = k_ref[...].astype(BF16)

            @pl.when(h == 0)
            def _():
                dcft_ref[...] = jnp.zeros_like(dcft_ref)
                drow_ref[...] = jnp.zeros_like(drow_ref)
        else:
            qs[...] = _rope(q_ref[...], c_ref[...], s_ref[...]).astype(BF16)
            ks[...] = _rope(k_ref[...], c_ref[...], s_ref[...]).astype(BF16)
        vs[...] = v_ref[...].astype(BF16)
        dos[...] = do_ref[...].astype(BF16)
        dk_acc[...] = jnp.zeros_like(dk_acc)
        dv_acc[...] = jnp.zeros_like(dv_acc)

        def softmax_head(hh):
            def qblock(i, carry):
                qi = qs[_qblk(i, bq), :]
                doi = dos[_qblk(i, bq), :]
                dvec = jnp.sum(do_ref[_qblk(i, bq), :] * o_ref[_qblk(i, bq), :], axis=1, keepdims=True)
                lse_i = jnp.max(lse_ref[_qblk(i, bq), :], axis=1, keepdims=True)
                if even:
                    lane = lax.broadcasted_iota(jnp.int32, (bq, LANES), 1)
                    cfq = jnp.sum(jnp.where(lane == hh, cf_ref[_qblk(i, bq), :], 0.0), axis=1, keepdims=True)

                def kblock(j, c):
                    dq, ds_rows = c
                    kj = ks[_blk(j), :]
                    z = _dot_nt(qi, kj) * scale + bias_ref[jnp.minimum(r * i - j + (r - 1), n_tab - 1)]
                    if even:
                        z = z + (cfq - cft_ref[hh, :, _blk(j)])
                    p = jnp.exp(z - lse_i)
                    dp = _dot_nt(doi, vs[_blk(j), :])
                    ds = p * (dp - dvec)
                    dsb = (ds * scale).astype(BF16)
                    dk_acc[_blk(j), :] += _dot_tn(dsb, qi)
                    dv_acc[_blk(j), :] += _dot_tn(p.astype(BF16), doi)
                    if even:
                        dcft_ref[hh, :, _blk(j)] += -jnp.sum(ds, axis=0, keepdims=True)
                    return (dq + jnp.dot(dsb, kj, preferred_element_type=F32),
                            ds_rows + jnp.sum(ds, axis=1, keepdims=True))

                dq, ds_rows = lax.fori_loop(0, r * (i + 1), kblock,
                                            (jnp.zeros((bq, HEAD_DIM), F32), jnp.zeros((bq, 1), F32)))
                dq_acc[_qblk(i, bq), :] = dq
                if even:
                    drow_ref[_qblk(i, bq), :] += jnp.where(lane == hh, ds_rows, 0.0)
                return carry

            lax.fori_loop(0, nq, qblock, 0)

        def sb_head():
            row = lax.broadcasted_iota(jnp.int32, (BLK, BLK), 0)
            col = lax.broadcasted_iota(jnp.int32, (BLK, BLK), 1)
            t_ex = jnp.where(row > col, 1.0, 0.0).astype(BF16)
            t_in = jnp.where(row >= col, 1.0, 0.0).astype(BF16)

            def qblock(i, carry):
                qi = qs[_qblk(i, bq), :]
                doi = dos[_qblk(i, bq), :]
                nkb = r * (i + 1)
                e_tot = jnp.sum(doi.astype(F32) * o_ref[_qblk(i, bq), :], axis=1, keepdims=True)
                zero = jnp.zeros((bq, 1), F32)

                def kblock(jj, c):
                    run, e_run, dq = c
                    j = nkb - 1 - jj
                    kj = ks[_blk(j), :]
                    z = _dot_nt(qi, kj) * scale
                    valid = valid_ref[jnp.minimum(r * i - j + (r - 1), r)]
                    a, m, lsig = _sb_block(z, valid, t_ex, run)
                    sig = jnp.exp(lsig)
                    e = _dot_nt(doi, vs[_blk(j), :]) * a
                    e_before = e_tot - (_split_dot(e, t_in) + e_run)
                    dz = (e * (1.0 - sig) - sig * e_before) * valid
                    dzb = (dz * scale).astype(BF16)
                    dk_acc[_blk(j), :] += _dot_tn(dzb, qi)
                    dv_acc[_blk(j), :] += _dot_tn(a.astype(BF16), doi)
                    return (run + jnp.sum(m, axis=1, keepdims=True), e_run + jnp.sum(e, axis=1, keepdims=True),
                            dq + jnp.dot(dzb, kj, preferred_element_type=F32))

                _, _, dq = lax.fori_loop(0, nkb, kblock, (zero, zero, jnp.zeros((bq, HEAD_DIM), F32)))
                dq_acc[_qblk(i, bq), :] = dq
                return carry

            lax.fori_loop(0, nq, qblock, 0)

        if even:
            @pl.when(h < n_sb)
            def _():
                sb_head()

            @pl.when(h >= n_sb)
            def _():
                softmax_head(h - n_sb)

            dh_ref[0] = dq_acc[...].astype(BF16)
            dh_ref[1] = dk_acc[...].astype(BF16)
        else:
            softmax_head(h)
            dh_ref[0] = _rope_t(dq_acc[...], c_ref[...], s_ref[...]).astype(BF16)
            dh_ref[1] = _rope_t(dk_acc[...], c_ref[...], s_ref[...]).astype(BF16)
        dh_ref[2] = dv_acc[...].astype(BF16)

    head = lambda off: pl.BlockSpec((S, HEAD_DIM), lambda h, off=off: (0, off + h))
    full = pl.BlockSpec((S, LANES), lambda h: (0, 0))
    tfull = pl.BlockSpec((n_heads - n_sb, 1, S), lambda h: (0, 0, 0))
    tab_specs = [pl.BlockSpec(t.shape, lambda h: (0, 0, 0)) for t in tabs]
    dh_spec = pl.BlockSpec((3, S, HEAD_DIM), lambda h: (0, 0, h))
    dh_shape = jax.ShapeDtypeStruct((3, S, D), BF16)
    if even:
        extra_specs, extra = [full, tfull] + tab_specs, (cf, cft) + tabs
        out_specs = [dh_spec, tfull, full]
        out_shape = [dh_shape, jax.ShapeDtypeStruct((n_heads - n_sb, 1, S), F32),
                     jax.ShapeDtypeStruct((S, LANES), F32)]
    else:
        extra_specs, extra = [full, full] + tab_specs, (rope_c, rope_s) + tabs
        out_specs = [dh_spec]
        out_shape = [dh_shape]
    blk_bytes = 10 * _nbytes((S, HEAD_DIM), F32) + sum(_nbytes(t.shape, F32) for t in tabs)
    scratch_bytes = 4 * _nbytes((S, HEAD_DIM), BF16) + 3 * _nbytes((S, HEAD_DIM), F32)
    return pl.pallas_call(
        body, name=name, grid=(n_heads,),
        in_specs=[head(0), head(n_heads), head(2 * n_heads), head(0), head(0), head(0)] + extra_specs,
        out_specs=out_specs, out_shape=out_shape,
        scratch_shapes=[pltpu.VMEM((S, HEAD_DIM), BF16)] * 4 + [pltpu.VMEM((S, HEAD_DIM), F32)] * 3,
        compiler_params=pltpu.CompilerParams(dimension_semantics=("arbitrary",),
                                             vmem_limit_bytes=_vmem_limit(blk_bytes, scratch_bytes)),
    )(hq, hq, hq, do, o, lse, *extra)


def _ffn_fwd(tag, x, g, wgu, wd, fs):
    n = _rms_fwd(tag + "_norm", x, g)
    gu, h = _ffn_up(tag + "_gu", n, wgu, fs)
    y = _mm(tag + "_down", h, wd, "nn", F32, res=x, alpha=0.5)
    return y, (x, g, n, gu, h)


def _ffn_bwd(tag, dx, dxb, wgu, wd, fs, saved, after=None, emit=None):
    x, g, n, gu, h = saved
    dgu = _ffn_dact(tag + "_dgu", dxb, wd, gu, fs, 0.5, after=after)
    dwd = _mm(tag + "_dwd", h, dxb, "tn", BF16, alpha=0.5)
    dwgu = _mm(tag + "_dwgu", n, dgu, "tn", BF16)
    token = emit(dwgu, dwd) if emit else None
    dn = _mm(tag + "_dn", dgu, wgu, "nt", F32, after=token)
    dx_in, dxb_in, dg = _rms_bwd(tag + "_dnorm", dn, x, g, dx)
    return dx_in, dxb_in, dg, dwgu, dwd, token


def _mixer_fwd(tag, kind, x, g, wqkv, wout, n_heads, n_sb, wf=None, bf=None, rope=None):
    n = _rms_fwd(tag + "_norm", x, g)
    hq = _mm(tag + "_qkv", n, wqkv, "nn", F32)
    if kind == "even":
        hf = _mm(tag + "_gate", n, wf, "nn", F32)
        cf, cft = _gate_fwd(tag + "_cumgate", hf, bf)
        cft = cft[:n_heads - n_sb].reshape(n_heads - n_sb, 1, -1)
        o, ob, lse = _attn_fwd_wide(tag + "_attn", hq, kind, n_heads, n_sb, cf=cf, cft=cft)
    else:
        hf = cf = cft = None
        o, ob, lse = _attn_fwd_wide(tag + "_attn", hq, kind, n_heads, n_sb, rope_c=rope[0], rope_s=rope[1])
    y = _mm(tag + "_out", ob, wout, "nn", F32, res=x)
    return y, (x, g, n, hq, hf, cf, cft, o, ob, lse)


def _mixer_bwd(tag, kind, dx, dxb, wqkv, wout, n_heads, n_sb, saved, wf=None, bf=None, rope=None, after=None,
               emit=None):
    x, g, n, hq, hf, cf, cft, o, ob, lse = saved
    do = _mm(tag + "_do", dxb, wout, "nt", F32, after=after)
    dwout = _mm(tag + "_dwout", ob, dxb, "tn", BF16)
    if kind == "even":
        dh3, dcft, drow = _attn_bwd_wide(tag + "_dattn", hq, do, o, lse, kind, n_heads, n_sb, cf=cf, cft=cft)
    else:
        (dh3,) = _attn_bwd_wide(tag + "_dattn", hq, do, o, lse, kind, n_heads, n_sb, rope_c=rope[0], rope_s=rope[1])
    dwqkv = _mm(tag + "_dwqkv", n, dh3, "tn", BF16)
    dwf = db = dhf = None
    if kind == "even":
        n_fox = n_heads - n_sb
        dcft = jnp.pad(dcft.reshape(n_fox, -1), ((0, LANES - n_fox), (0, 0)))
        dhf, db = _gate_bwd(tag + "_dcumgate", dcft, drow, hf, bf)
        dwf = _mm(tag + "_dwf", n, dhf, "tn", BF16)
    token = emit(dwqkv, dwout, dwf) if emit else None
    dn = _mm(tag + "_dn", dh3, wqkv, "nt", F32, after=token)
    if kind == "even":
        dn = _mm(tag + "_dn_gate", dhf, wf, "nt", F32, res=dn)
    dx_in, dxb_in, dg = _rms_bwd(tag + "_dnorm", dn, x, g, dx)
    return dx_in, dxb_in, dg, dwqkv, dwout, dwf, db, token


def _local_step(x, target, w, fs, n_heads, n_sb, emit=None):
    S, D = x.shape
    rope = _rope_tables(S)
    kinds = ("even", "odd")
    saved = []
    h = x
    if callable(w):
        fetch, w = w, {"norm_g": w("norm_g", None), "final_g": w("final_g", None),
                       "wgu1": [None, None], "wd1": [None, None], "wgu2": [None, None], "wd2": [None, None]}
    else:
        fetch = None
    for l, kind in enumerate(kinds):
        ng = [w["norm_g"][l, i][None, :] for i in range(3)]
        if fetch:
            w["wgu1"][l], w["wd1"][l] = fetch(("ffn1", l), h)
        h, s1 = _ffn_fwd(f"l{l}_ffn1", h, ng[0], w["wgu1"][l], w["wd1"][l], fs)
        if fetch:
            w.update(fetch(("mix", l), h))
        if kind == "even":
            h, s2 = _mixer_fwd(f"l{l}_mix", kind, h, ng[1], w["wqkv_e"], w["wout_e"], n_heads, n_sb,
                               wf=w["wf"], bf=w["bf"])
        else:
            h, s2 = _mixer_fwd(f"l{l}_mix", kind, h, ng[1], w["wqkv_o"], w["wout_o"], n_heads, n_sb, rope=rope)
        if fetch:
            w["wgu2"][l], w["wd2"][l] = fetch(("ffn2", l), h)
        h, s3 = _ffn_fwd(f"l{l}_ffn2", h, ng[2], w["wgu2"][l], w["wd2"][l], fs)
        saved.append((s1, s2, s3))

    dx, dxb, dfinal, loss = _loss_head("loss_head", h, w["final_g"], target)
    grads = {"dfinal": dfinal, "dnorm": [[None] * 3 for _ in kinds],
             "dwgu1": [None, None], "dwd1": [None, None], "dwgu2": [None, None], "dwd2": [None, None]}
    hand = lambda block: (lambda *mats: emit(block, mats)) if emit else None
    token = None
    for l in (1, 0):
        kind = kinds[l]
        s1, s2, s3 = saved[l]
        dx, dxb, dg, grads["dwgu2"][l], grads["dwd2"][l], token = _ffn_bwd(
            f"l{l}_ffn2", dx, dxb, w["wgu2"][l], w["wd2"][l], fs, s3, after=token, emit=hand(("ffn2", l)))
        grads["dnorm"][l][2] = dg
        if kind == "even":
            dx, dxb, dg, grads["dwqkv_e"], grads["dwout_e"], grads["dwf"], grads["db"], token = _mixer_bwd(
                f"l{l}_mix", kind, dx, dxb, w["wqkv_e"], w["wout_e"], n_heads, n_sb, s2, wf=w["wf"], bf=w["bf"],
                after=token, emit=hand(("mix", l)))
        else:
            dx, dxb, dg, grads["dwqkv_o"], grads["dwout_o"], _, _, token = _mixer_bwd(
                f"l{l}_mix", kind, dx, dxb, w["wqkv_o"], w["wout_o"], n_heads, n_sb, s2, rope=rope,
                after=token, emit=hand(("mix", l)))
        grads["dnorm"][l][1] = dg
        dx, dxb, dg, grads["dwgu1"][l], grads["dwd1"][l], token = _ffn_bwd(
            f"l{l}_ffn1", dx, dxb, w["wgu1"][l], w["wd1"][l], fs, s1, after=token, emit=hand(("ffn1", l)))
        grads["dnorm"][l][0] = dg
    return loss, dx, grads


def _cast_into(name, shard, layer, chip, full_shape, place, full=None):
    R, C = shard.shape[-2:]
    tr = _row_tile(R, 512, step=16)
    if layer is None:
        in_spec = pl.BlockSpec((tr, C), lambda i, k: (i, 0))
    else:
        in_spec = pl.BlockSpec((None, tr, C), lambda i, k: (layer, i, 0))
    lead = (None,) * (len(full_shape) - 2)
    out_spec = pl.BlockSpec(lead + (tr, C), lambda i, k: place(i, k[0]))

    def body(*refs):
        k_ref, w_ref = refs[:2]
        o_ref = refs[-1]
        o_ref[...] = w_ref[...].astype(BF16)

    in_specs = [in_spec] + ([_ANY] if full is not None else [])
    args = (chip, shard) + ((full,) if full is not None else ())
    grid_spec = pltpu.PrefetchScalarGridSpec(num_scalar_prefetch=1, grid=(R // tr,), in_specs=in_specs, out_specs=out_spec)
    return pl.pallas_call(
        body, name=name, grid_spec=grid_spec, out_shape=jax.ShapeDtypeStruct(full_shape, BF16),
        input_output_aliases={2: 0} if full is not None else {},
        compiler_params=pltpu.CompilerParams(dimension_semantics=("arbitrary",)),
    )(*args)


def _region_shape(grad, kind):
    if kind == "lead":
        return grad.shape[1] // N_CORES, grad.shape[2]
    rows, cols = grad.shape
    if kind == "cols":
        return rows // N_CORES, cols // N_CHIPS
    return rows // (N_CHIPS * N_CORES), cols


def _region_add(name, grad, kind, landed, core):
    rh, cw = _region_shape(grad, kind)
    tr = _row_tile(rh, 256, step=16)
    nrb = rh // tr
    if kind == "cols":
        g_spec = pl.BlockSpec((tr, cw), lambda k, r, c: (c[0] * nrb + r, k))
    elif kind == "rows":
        g_spec = pl.BlockSpec((tr, cw), lambda k, r, c: ((N_CORES * k + c[0]) * nrb + r, 0))
    else:
        g_spec = pl.BlockSpec((None, tr, cw), lambda k, r, c: (k, c[0] * nrb + r, 0))
    l_spec = pl.BlockSpec((None, tr, cw), lambda k, r, c: (k, r, 0))

    def body(c_ref, g_ref, l_ref, o_ref):
        o_ref[...] = (g_ref[...].astype(F32) + l_ref[...].astype(F32)).astype(BF16)

    grid_spec = pltpu.PrefetchScalarGridSpec(
        num_scalar_prefetch=1, grid=(N_CHIPS, nrb), in_specs=[g_spec, l_spec], out_specs=l_spec)
    return pl.pallas_call(
        body, name=name, grid_spec=grid_spec, out_shape=jax.ShapeDtypeStruct(landed.shape, BF16),
        compiler_params=pltpu.CompilerParams(dimension_semantics=("parallel", "parallel"),
                                             vmem_limit_bytes=_vmem_limit(3 * _nbytes((tr, cw), F32))),
    )(core, grad, landed)


def _chip_sum(name, pair, landed, pos):
    _, rh, cw = pair.shape
    tr = _row_tile(rh, max(16, 2**20 // (cw * 4)), step=16)
    nrb = rh // tr

    def body(p_ref, own_ref, l_ref, o_ref):
        acc = own_ref[...].astype(F32)
        for s in range(N_CHIPS - 1):
            acc = acc + l_ref[s].astype(F32)
        o_ref[...] = acc

    grid_spec = pltpu.PrefetchScalarGridSpec(
        num_scalar_prefetch=1, grid=(nrb,),
        in_specs=[pl.BlockSpec((None, tr, cw), lambda r, p: (p[0], r, 0)),
                  pl.BlockSpec((N_CHIPS - 1, tr, cw), lambda r, p: (0, r, 0))],
        out_specs=pl.BlockSpec((tr, cw), lambda r, p: (p[1] * nrb + r, 0)))
    return pl.pallas_call(
        body, name=name, grid_spec=grid_spec, out_shape=jax.ShapeDtypeStruct((N_CORES * rh, cw), F32),
        compiler_params=pltpu.CompilerParams(dimension_semantics=("arbitrary",)),
    )(pos, pair, landed)


def _sum_leading(name, parts):
    n, R, C = parts.shape
    tr = _row_tile(R, max(8, (2**20 // (C * 4)) // 8 * 8))

    def body(p_ref, o_ref):
        acc = p_ref[0]
        for s in range(1, n):
            acc = acc + p_ref[s]
        o_ref[...] = acc

    return pl.pallas_call(
        body, name=name, grid=(R // tr,),
        in_specs=[pl.BlockSpec((n, tr, C), lambda i: (0, i, 0))],
        out_specs=pl.BlockSpec((tr, C), lambda i: (i, 0)),
        out_shape=jax.ShapeDtypeStruct((R, C), F32),
        compiler_params=pltpu.CompilerParams(dimension_semantics=("parallel",)),
    )(parts)


def _adamw(name, w, g, m, v):
    shape = w.shape
    to2d = lambda t: t.reshape(-1, shape[-1]) if t.ndim > 1 else t.reshape(1, -1)
    w2, g2, m2, v2 = (to2d(t) for t in (w, g, m, v))
    R, C = w2.shape
    tr = _row_tile(R, 256)

    def body(w_ref, g_ref, m_ref, v_ref, d_ref, nm_ref, nv_ref):
        gv = g_ref[...]
        nm = ADAM_B1 * m_ref[...] + (1.0 - ADAM_B1) * gv
        nv = ADAM_B2 * v_ref[...] + (1.0 - ADAM_B2) * (gv * gv)
        m_hat = nm / (1.0 - ADAM_B1 ** ADAM_STEP)
        v_hat = nv / (1.0 - ADAM_B2 ** ADAM_STEP)
        d_ref[...] = -ADAM_LR * (m_hat / (jnp.sqrt(v_hat) + ADAM_EPS) + ADAM_WD * w_ref[...])
        nm_ref[...] = nm
        nv_ref[...] = nv

    spec = pl.BlockSpec((tr, C), lambda i: (i, 0))
    sds = jax.ShapeDtypeStruct((R, C), F32)
    d, nm, nv = pl.pallas_call(
        body, name=name, grid=(R // tr,), in_specs=[spec] * 4, out_specs=[spec] * 3, out_shape=[sds] * 3,
        compiler_params=pltpu.CompilerParams(dimension_semantics=("parallel",),
                                             vmem_limit_bytes=_vmem_limit(7 * _nbytes((tr, C), F32))),
    )(w2, g2, m2, v2)
    return d.reshape(shape), nm.reshape(shape), nv.reshape(shape)


def _adamw_layer(name, w, m, v, g_src, g_col, layer, outs=None):
    L, R, C = w.shape
    tr = _row_tile(R, 256)

    def body(*refs):
        w_ref, m_ref, v_ref, g_ref = refs[:4]
        go_ref, d_ref, nm_ref, nv_ref = refs[-4:]
        gv = g_ref[...]
        nm = ADAM_B1 * m_ref[...] + (1.0 - ADAM_B1) * gv
        nv = ADAM_B2 * v_ref[...] + (1.0 - ADAM_B2) * (gv * gv)
        m_hat = nm / (1.0 - ADAM_B1 ** ADAM_STEP)
        v_hat = nv / (1.0 - ADAM_B2 ** ADAM_STEP)
        go_ref[...] = gv
        d_ref[...] = -ADAM_LR * (m_hat / (jnp.sqrt(v_hat) + ADAM_EPS) + ADAM_WD * w_ref[...])
        nm_ref[...] = nm
        nv_ref[...] = nv

    stacked = pl.BlockSpec((None, tr, C), lambda i: (layer, i, 0))
    in_specs = [stacked] * 3 + [pl.BlockSpec((tr, C), lambda i: (i, g_col))] + ([_ANY] * 4 if outs else [])
    sds = jax.ShapeDtypeStruct((L, R, C), F32)
    return pl.pallas_call(
        body, name=name, grid=(R // tr,), in_specs=in_specs, out_specs=[stacked] * 4, out_shape=[sds] * 4,
        input_output_aliases={4 + i: i for i in range(4)} if outs else {},
        compiler_params=pltpu.CompilerParams(dimension_semantics=("parallel",),
                                             vmem_limit_bytes=_vmem_limit(8 * _nbytes((tr, C), F32))),
    )(w, m, v, g_src, *(outs or ()))


_ANY = pl.BlockSpec(memory_space=pl.ANY)


def _mesh_pos():
    return lax.axis_index("x"), lax.axis_index("y"), lax.axis_index("c")


def _other_chips(x, y):
    return [(1 - x, y), (x, 1 - y), (1 - x, 1 - y)]


def _gather_over_chips(name, fulls, views):
    n = len(views)
    nf = len(fulls)

    def body(*refs):
        full = refs[nf:2 * nf]
        ici_send, ici_recv, d2d_send, d2d_recv = refs[2 * nf:]
        x, y, c = _mesh_pos()
        chips = _other_chips(x, y)
        mine = 2 * x + y
        sibling = (x, y, 1 - c)

        def ici(a, p, k):
            i, view, _ = views[a]
            part = view(full[i], k, c)
            return pltpu.make_async_remote_copy(
                src_ref=part, dst_ref=part, send_sem=ici_send.at[a, p], recv_sem=ici_recv.at[a, p],
                device_id=(*chips[p], c), device_id_type=MESH)

        def d2d(a, p, h):
            i, view, _ = views[a]
            px, py = chips[p]
            part = view(full[i], 2 * px + py, h)
            return pltpu.make_async_remote_copy(
                src_ref=part, dst_ref=part, send_sem=d2d_send.at[a, p], recv_sem=d2d_recv.at[a, p],
                device_id=sibling, device_id_type=MESH)

        sends = [ici(a, p, mine) for a in range(n) for p in range(3)]
        for cp in sends:
            cp.start()
        passed = []
        for a in range(n):
            for p, (px, py) in enumerate(chips):
                ici(a, p, 2 * px + py).wait_recv()
                if views[a][2]:
                    fwd = d2d(a, p, c)
                    fwd.start()
                    passed.append(fwd)
        for a in range(n):
            if views[a][2]:
                for p in range(3):
                    d2d(a, p, 1 - c).wait_recv()
        for cp in sends + passed:
            cp.wait_send()

    return pl.pallas_call(
        body, name=name, in_specs=[_ANY] * nf, out_specs=[_ANY] * nf,
        out_shape=[jax.ShapeDtypeStruct(f.shape, f.dtype) for f in fulls],
        input_output_aliases={i: i for i in range(nf)},
        scratch_shapes=[pltpu.SemaphoreType.DMA((n, 3))] * 4,
        compiler_params=pltpu.CompilerParams(has_side_effects=True),
    )(*fulls)


_HBM = pl.BlockSpec(memory_space=pltpu.HBM)
_SEM = pl.BlockSpec(memory_space=pltpu.SEMAPHORE)


def _in_hbm(arrays):
    return [pltpu.with_memory_space_constraint(a, pltpu.HBM) for a in arrays]


def _gather_start(name, fulls, views, after):
    nf = len(fulls)
    ng = 1 + max(g for _, _, g in views)

    def body(*refs):
        full = refs[nf + 1:2 * nf + 1]
        send_sems, recv_sems = refs[2 * nf + 1:2 * nf + 1 + ng], refs[2 * nf + 1 + ng:]
        x, y, c = _mesh_pos()
        chips = _other_chips(x, y)
        for i, view, g in views:
            part = view(full[i], 2 * x + y, c)
            for px, py in chips:
                pltpu.make_async_remote_copy(
                    src_ref=part, dst_ref=part, send_sem=send_sems[g], recv_sem=recv_sems[g],
                    device_id=(px, py, c), device_id_type=MESH).start()

    outs = pl.pallas_call(
        body, name=name, in_specs=[_HBM] * nf + [_ANY], out_specs=[_HBM] * nf + [_SEM] * (2 * ng),
        out_shape=[pltpu.HBM(f.shape, f.dtype) for f in fulls] + [pltpu.SemaphoreType.DMA(())] * (2 * ng),
        input_output_aliases={i: i for i in range(nf)},
        compiler_params=pltpu.CompilerParams(has_side_effects=pltpu.SideEffectType.DATAFLOW_SIDE_EFFECTING),
    )(*_in_hbm(fulls), after)
    return list(outs[:nf]), list(outs[nf:nf + ng]), list(outs[nf + ng:])


def _gather_wait(name, fulls, views, send_sem, recv_sem, after):
    nf = len(fulls)

    def body(*refs):
        send_ref, recv_ref = refs[nf], refs[nf + 1]
        full = refs[nf + 3:]
        x, y, c = _mesh_pos()
        copies = [pltpu.make_async_remote_copy(
            src_ref=view(full[i], 2 * x + y, c), dst_ref=view(full[i], 2 * px + py, c),
            send_sem=send_ref, recv_sem=recv_ref, device_id=(px, py, c), device_id_type=MESH)
            for i, view in views for px, py in _other_chips(x, y)]
        for cp in copies:
            cp.wait_send()
        for cp in copies:
            cp.wait_recv()

    outs = pl.pallas_call(
        body, name=name, in_specs=[_HBM] * nf + [_SEM, _SEM, _ANY], out_specs=[_HBM] * nf,
        out_shape=[pltpu.HBM(f.shape, f.dtype) for f in fulls],
        input_output_aliases={i: i for i in range(nf)},
        compiler_params=pltpu.CompilerParams(has_side_effects=pltpu.SideEffectType.DATAFLOW_SIDE_EFFECTING),
    )(*fulls, send_sem, recv_sem, after)
    return list(outs)


def _forward_to_sibling(name, fulls, views):
    n, nf = len(views), len(fulls)

    def body(*refs):
        full = refs[nf:2 * nf]
        send_sems, recv_sems = refs[2 * nf:]
        x, y, c = _mesh_pos()
        chips = _other_chips(x, y)

        def copy(a, p, h):
            i, view = views[a]
            px, py = chips[p]
            part = view(full[i], 2 * px + py, h)
            return pltpu.make_async_remote_copy(
                src_ref=part, dst_ref=part, send_sem=send_sems.at[a, p], recv_sem=recv_sems.at[a, p],
                device_id=(x, y, 1 - c), device_id_type=MESH)

        sends = [copy(a, p, c) for a in range(n) for p in range(3)]
        for cp in sends:
            cp.start()
        for a in range(n):
            for p in range(3):
                copy(a, p, 1 - c).wait_recv()
        for cp in sends:
            cp.wait_send()

    return pl.pallas_call(
        body, name=name, in_specs=[_ANY] * nf, out_specs=[_ANY] * nf,
        out_shape=[jax.ShapeDtypeStruct(f.shape, f.dtype) for f in fulls],
        input_output_aliases={i: i for i in range(nf)},
        scratch_shapes=[pltpu.SemaphoreType.DMA((n, 3))] * 2,
        compiler_params=pltpu.CompilerParams(has_side_effects=True),
    )(*fulls)


def _region_view(ref, kind, k, c):
    if kind == "lead":
        rh = ref.shape[1] // N_CORES
        return ref.at[k, pl.ds(pl.multiple_of(c * rh, 8), rh), :]
    rows, cols = ref.shape
    if kind == "cols":
        rh, cw = rows // N_CORES, cols // N_CHIPS
        return ref.at[pl.ds(pl.multiple_of(c * rh, 8), rh), pl.ds(k * cw, cw)]
    rh = rows // (N_CHIPS * N_CORES)
    return ref.at[pl.ds(pl.multiple_of((N_CORES * k + c) * rh, 8), rh), :]


def _send_to_sibling(name, grads, kinds):
    n = len(grads)
    shapes = [jax.ShapeDtypeStruct((N_CHIPS,) + _region_shape(g, kd), g.dtype) for g, kd in zip(grads, kinds)]

    def body(*refs):
        g_ref, land = refs[:n], refs[n:2 * n]
        send_sems, recv_sems = refs[2 * n:]
        x, y, c = _mesh_pos()
        copies = []
        for a in range(n):
            for k in range(N_CHIPS):
                cp = pltpu.make_async_remote_copy(
                    src_ref=_region_view(g_ref[a], kinds[a], k, 1 - c), dst_ref=land[a].at[k],
                    send_sem=send_sems.at[a, k], recv_sem=recv_sems.at[a, k],
                    device_id=(x, y, 1 - c), device_id_type=MESH)
                cp.start()
                copies.append(cp)
        for cp in copies:
            cp.wait_recv()
        for cp in copies:
            cp.wait_send()

    return pl.pallas_call(
        body, name=name, in_specs=[_ANY] * n, out_specs=[_ANY] * n, out_shape=shapes,
        scratch_shapes=[pltpu.SemaphoreType.DMA((n, N_CHIPS)), pltpu.SemaphoreType.DMA((n, N_CHIPS))],
        compiler_params=pltpu.CompilerParams(has_side_effects=True),
    )(*grads)


def _scatter_over_chips(name, pair_sums):
    n = len(pair_sums)

    def body(*refs):
        p_ref, land = refs[:n], refs[n:2 * n]
        send_sems, recv_sems = refs[2 * n:]
        x, y, c = _mesh_pos()
        chips = _other_chips(x, y)
        sends = []
        for a in range(n):
            for p, (px, py) in enumerate(chips):
                cp = pltpu.make_async_remote_copy(
                    src_ref=p_ref[a].at[2 * px + py], dst_ref=land[a].at[p], send_sem=send_sems.at[a, p],
                    recv_sem=recv_sems.at[a, p], device_id=(px, py, c), device_id_type=MESH)
                cp.start()
                sends.append(cp)
        for cp in sends:
            cp.wait_recv()
        for cp in sends:
            cp.wait_send()

    return pl.pallas_call(
        body, name=name, in_specs=[_ANY] * n, out_specs=[_ANY] * n,
        out_shape=[jax.ShapeDtypeStruct((N_CHIPS - 1,) + p.shape[1:], p.dtype) for p in pair_sums],
        scratch_shapes=[pltpu.SemaphoreType.DMA((n, 3)), pltpu.SemaphoreType.DMA((n, 3))],
        compiler_params=pltpu.CompilerParams(has_side_effects=True),
    )(*pair_sums)


def _scatter_start(name, pair_sums):
    n = len(pair_sums)
    lands = [lax.empty((N_CHIPS - 1,) + p.shape[1:], p.dtype) for p in pair_sums]

    def body(*refs):
        p_ref, land = refs[2 * n:3 * n], refs[3 * n:4 * n]
        send_sem, recv_sem, token = refs[4 * n:]
        x, y, c = _mesh_pos()
        for a in range(n):
            for p, (px, py) in enumerate(_other_chips(x, y)):
                pltpu.make_async_remote_copy(
                    src_ref=p_ref[a].at[2 * px + py], dst_ref=land[a].at[p], send_sem=send_sem, recv_sem=recv_sem,
                    device_id=(px, py, c), device_id_type=MESH).start()
        token[...] = jnp.zeros_like(token)

    outs = pl.pallas_call(
        body, name=name, in_specs=[_HBM] * (2 * n),
        out_specs=[_HBM] * (2 * n) + [_SEM, _SEM, pl.BlockSpec(memory_space=pltpu.VMEM)],
        out_shape=[pltpu.HBM(t.shape, t.dtype) for t in list(pair_sums) + lands]
        + [pltpu.SemaphoreType.DMA(()), pltpu.SemaphoreType.DMA(()), jax.ShapeDtypeStruct((8, LANES), F32)],
        input_output_aliases={i: i for i in range(2 * n)},
        compiler_params=pltpu.CompilerParams(has_side_effects=pltpu.SideEffectType.DATAFLOW_SIDE_EFFECTING),
    )(*_in_hbm(list(pair_sums) + lands))
    return list(outs[:n]), list(outs[n:2 * n]), outs[2 * n], outs[2 * n + 1], outs[2 * n + 2]


def _scatter_wait(name, pair_sums, lands, send_sem, recv_sem, after):
    n = len(pair_sums)

    def body(*refs):
        send_ref, recv_ref = refs[2 * n], refs[2 * n + 1]
        p_ref, land = refs[2 * n + 3:3 * n + 3], refs[3 * n + 3:]
        x, y, c = _mesh_pos()
        copies = [pltpu.make_async_remote_copy(
            src_ref=p_ref[a].at[2 * px + py], dst_ref=land[a].at[p], send_sem=send_ref, recv_sem=recv_ref,
            device_id=(px, py, c), device_id_type=MESH)
            for a in range(n) for p, (px, py) in enumerate(_other_chips(x, y))]
        for cp in copies:
            cp.wait_send()
        for cp in copies:
            cp.wait_recv()

    outs = pl.pallas_call(
        body, name=name, in_specs=[_HBM] * (2 * n) + [_SEM, _SEM, _ANY], out_specs=[_HBM] * (2 * n),
        out_shape=[pltpu.HBM(t.shape, t.dtype) for t in list(pair_sums) + list(lands)],
        input_output_aliases={i: i for i in range(2 * n)},
        compiler_params=pltpu.CompilerParams(has_side_effects=pltpu.SideEffectType.DATAFLOW_SIDE_EFFECTING),
    )(*pair_sums, *lands, send_sem, recv_sem, after)
    return list(outs[:n]), list(outs[n:])


def _swap_halves(name, shards):
    n = len(shards)

    def body(*refs):
        out = refs[n:2 * n]
        send_sems, recv_sems = refs[2 * n:]
        x, y, c = _mesh_pos()
        sends = []
        for a in range(n):
            rh = out[a].shape[0] // N_CORES
            mine = out[a].at[pl.ds(pl.multiple_of(c * rh, 8), rh), :]
            cp = pltpu.make_async_remote_copy(
                src_ref=mine, dst_ref=mine, send_sem=send_sems.at[a], recv_sem=recv_sems.at[a],
                device_id=(x, y, 1 - c), device_id_type=MESH)
            cp.start()
            sends.append(cp)
        for a in range(n):
            rh = out[a].shape[0] // N_CORES
            theirs = out[a].at[pl.ds(pl.multiple_of((1 - c) * rh, 8), rh), :]
            pltpu.make_async_remote_copy(
                src_ref=theirs, dst_ref=theirs, send_sem=send_sems.at[a], recv_sem=recv_sems.at[a],
                device_id=(x, y, 1 - c), device_id_type=MESH).wait_recv()
        for cp in sends:
            cp.wait_send()

    return pl.pallas_call(
        body, name=name, in_specs=[_ANY] * n, out_specs=[_ANY] * n,
        out_shape=[jax.ShapeDtypeStruct(s.shape, s.dtype) for s in shards],
        input_output_aliases={i: i for i in range(n)},
        scratch_shapes=[pltpu.SemaphoreType.DMA((n,)), pltpu.SemaphoreType.DMA((n,))],
        compiler_params=pltpu.CompilerParams(has_side_effects=True),
    )(*shards)


def _gather_all_devices(name, block):
    R, C = block.shape
    ndev = N_CHIPS * N_CORES

    def body(b_ref, out_ref, send_sems, recv_sems, local_sem):
        x, y, c = _mesh_pos()
        mine = 4 * x + 2 * y + c
        own = pltpu.make_async_copy(b_ref, out_ref.at[mine], local_sem)
        own.start()
        sends = []
        for mask in range(1, ndev):
            fx, fy, fc = (mask >> 2) & 1, (mask >> 1) & 1, mask & 1
            px, py, pc = x ^ fx, y ^ fy, c ^ fc
            cp = pltpu.make_async_remote_copy(
                src_ref=b_ref, dst_ref=out_ref.at[mine], send_sem=send_sems.at[mask - 1],
                recv_sem=recv_sems.at[mask - 1], device_id=(px, py, pc), device_id_type=MESH)
            cp.start()
            sends.append(cp)
        for mask in range(1, ndev):
            fx, fy, fc = (mask >> 2) & 1, (mask >> 1) & 1, mask & 1
            px, py, pc = x ^ fx, y ^ fy, c ^ fc
            pltpu.make_async_remote_copy(
                src_ref=b_ref, dst_ref=out_ref.at[4 * px + 2 * py + pc], send_sem=send_sems.at[mask - 1],
                recv_sem=recv_sems.at[mask - 1], device_id=(px, py, pc), device_id_type=MESH).wait_recv()
        for cp in sends:
            cp.wait_send()
        own.wait()

    return pl.pallas_call(
        body, name=name, in_specs=[_ANY], out_specs=_ANY,
        out_shape=jax.ShapeDtypeStruct((ndev, R, C), F32),
        scratch_shapes=[pltpu.SemaphoreType.DMA((ndev - 1,)), pltpu.SemaphoreType.DMA((ndev - 1,)),
                        pltpu.SemaphoreType.DMA(())],
        compiler_params=pltpu.CompilerParams(has_side_effects=True),
    )(block)


def _reduce_scatter(grads, kinds, pos):
    landed = _send_to_sibling("rs_pair_send", grads, kinds)
    pair = [_region_add(f"rs_pair_add_{a}", g, kd, ld, pos[1:]) for a, (g, kd, ld) in enumerate(zip(grads, kinds, landed))]
    parts = _scatter_over_chips("rs_chip_send", pair)
    shards = [_chip_sum(f"rs_chip_add_{a}", p, ld, pos) for a, (p, ld) in enumerate(zip(pair, parts))]
    return _swap_halves("rs_swap_halves", shards)


def kernel(x, norm_g, ffn1_w_gate, ffn1_w_up, ffn1_w_down, ffn2_w_gate, ffn2_w_up, ffn2_w_down, even_w_in, even_b_forget, even_w_out, odd_w_qkv, odd_w_out, final_norm_g, loss_target, m_norm_g, m_ffn1_w_gate, m_ffn1_w_up, m_ffn1_w_down, m_ffn2_w_gate, m_ffn2_w_up, m_ffn2_w_down, m_even_w_in, m_even_b_forget, m_even_w_out, m_odd_w_qkv, m_odd_w_out, m_final_norm_g, v_norm_g, v_ffn1_w_gate, v_ffn1_w_up, v_ffn1_w_down, v_ffn2_w_gate, v_ffn2_w_up, v_ffn2_w_down, v_even_w_in, v_even_b_forget, v_even_w_out, v_odd_w_qkv, v_odd_w_out, v_final_norm_g):
    _, S, D = x.shape
    L = norm_g.shape[0]
    assert L == 2 and even_w_in.shape[0] == 1 and odd_w_qkv.shape[0] == 1
    fs = ffn1_w_gate.shape[2]
    F = N_CHIPS * fs
    wc = even_w_in.shape[2]
    n_heads = D // HEAD_DIM
    n_fox = N_CHIPS * wc - 3 * D
    n_sb = n_heads - n_fox
    qs = odd_w_qkv.shape[2]
    os_ = even_w_out.shape[1]
    ns = norm_g.shape[2]
    xi, yi, ci = _mesh_pos()
    chip = 2 * xi + yi

    pos = jnp.stack([chip, ci]).astype(jnp.int32)
    kchip = pos[:1]
    lane = lambda start, size: pl.ds(pl.multiple_of(start, LANES), size)
    sub = lambda start, size: pl.ds(pl.multiple_of(start, 16), size)
    gate_view = lambda r, k, h: r.at[sub(h * (D // 2), D // 2), lane(k * 2 * fs, fs)]
    up_view = lambda r, k, h: r.at[sub(h * (D // 2), D // 2), lane(k * 2 * fs + fs, fs)]
    down_view = lambda r, k, h: r.at[sub(k * fs + h * (fs // 2), fs // 2), :]
    out_view = lambda r, k, h: r.at[sub(k * os_ + h * (os_ // 2), os_ // 2), :]
    tr_d = _row_tile(fs, 512, step=16)
    tr_o = _row_tile(os_, 512, step=16)
    ffn_w = {"ffn1": (ffn1_w_gate, ffn1_w_up, ffn1_w_down), "ffn2": (ffn2_w_gate, ffn2_w_up, ffn2_w_down)}
    win_view = lambda r, k, h: r.at[k, sub(h * (D // 2), D // 2), :]
    qkv_view = lambda r, k, h: r.at[sub(h * (D // 2), D // 2), lane(k * qs, qs)]
    fulls, views, groups = [], [], {}
    for l in range(L):
        for blk in ("ffn1", "mix", "ffn2"):
            o, v0 = len(fulls), len(views)
            if blk == "mix" and l == 0:
                fulls += [_cast_into("cast_win", even_w_in, 0, kchip, (N_CHIPS, D, wc), lambda i, k: (k, i, 0)),
                          _cast_into("cast_wout_e", even_w_out, 0, kchip, (D, D), lambda i, k: (k * (os_ // tr_o) + i, 0))]
                views += [(o, win_view), (o + 1, out_view)]
            elif blk == "mix":
                fulls += [_cast_into("cast_wqkv_o", odd_w_qkv, 0, kchip, (D, N_CHIPS * qs), lambda i, k: (i, k)),
                          _cast_into("cast_wout_o", odd_w_out, 0, kchip, (D, D), lambda i, k: (k * (os_ // tr_o) + i, 0))]
                views += [(o, qkv_view), (o + 1, out_view)]
            else:
                wg, wu, wd = ffn_w[blk]
                t = f"cast_{blk}_l{l}"
                gu = _cast_into(t + "_gate", wg, l, kchip, (D, 2 * F), lambda i, k: (i, 2 * k))
                gu = _cast_into(t + "_up", wu, l, kchip, (D, 2 * F), lambda i, k: (i, 2 * k + 1), full=gu)
                dn = _cast_into(t + "_down", wd, l, kchip, (F, D), lambda i, k: (k * (fs // tr_d) + i, 0))
                fulls += [gu, dn]
                views += [(o, gate_view), (o, up_view), (o + 1, down_view)]
            gid = len(groups)
            views[v0:] = [(i, view, gid) for i, view in views[v0:]]
            groups[(blk, l)] = (gid, list(range(o, len(fulls))), list(range(v0, len(views))))
    norm_own = lax.dynamic_update_slice(jnp.zeros((L, 3, N_CHIPS * ns), F32), norm_g, (0, 0, chip * ns))
    (norm_full,) = _gather_over_chips("gather_norm", [norm_own], [(0, lambda r, k, h: r.at[:, :, lane(k * ns, ns)], False)])
    started, send_sems, recv_sems = _gather_start("gather_start", fulls, views, norm_full)

    def fetch(block, after):
        if block == "norm_g":
            return norm_full
        if block == "final_g":
            return final_norm_g[None, :]
        gid, arrays, rows = groups[block]
        tag = f"{block[0]}_l{block[1]}"
        local = [(views[a][0] - arrays[0], views[a][1]) for a in rows]
        got = _gather_wait("gather_wait_" + tag, [started[i] for i in arrays], local, send_sems[gid], recv_sems[gid],
                           x if after is None else after)
        got = _forward_to_sibling("gather_pass_" + tag, got, local)
        if block[0] != "mix":
            return got
        if block[1] == 1:
            return {"wqkv_o": got[0], "wout_o": got[1]}
        win = jnp.concatenate([got[0][k] for k in range(N_CHIPS)], axis=1)
        return {"wqkv_e": win[:, :3 * D], "wf": jnp.pad(win[:, 3 * D:], ((0, 0), (0, LANES - n_fox))),
                "bf": jnp.pad(even_b_forget, ((0, 0), (0, LANES - n_fox))), "wout_e": got[1]}

    pending, shards = [], {}

    def finish(after):
        block, pair, lands, ssem, rsem = pending.pop(0)
        tag = f"{block[0]}_l{block[1]}"
        pair, lands = _scatter_wait("rs_chip_wait_" + tag, pair, lands, ssem, rsem, after)
        shards[block] = [_chip_sum(f"rs_chip_add_{tag}_{a}", p, ld, pos) for a, (p, ld) in enumerate(zip(pair, lands))]

    def emit(block, mats):
        blk, l = block
        tag = f"{blk}_l{l}"
        kinds = ["cols", "rows"]
        if blk == "mix" and l == 0:
            dwqkv, dwout, dwf = mats
            dwin = jnp.concatenate([dwqkv, dwf[:, :n_fox]], axis=1)
            mats = [jnp.stack([dwin[:, k * wc:(k + 1) * wc] for k in range(N_CHIPS)]), dwout]
            kinds = ["lead", "rows"]
        elif blk == "mix":
            mats = list(mats[:2])
        else:
            mats = list(mats)
        landed = _send_to_sibling("rs_pair_send_" + tag, mats, kinds)
        pair = [_region_add(f"rs_pair_add_{tag}_{a}", m, kd, ld, pos[1:])
                for a, (m, kd, ld) in enumerate(zip(mats, kinds, landed))]
        pair, lands, ssem, rsem, token = _scatter_start("rs_chip_start_" + tag, pair)
        if pending:
            finish(token)
        pending.append((block, pair, lands, ssem, rsem))
        return token

    loss_vec, grad_x, g = _local_step(x[0], loss_target[0], fetch, fs, n_heads, n_sb, emit=emit)
    finish(grad_x)
    order = [(b, l) for b in ("ffn1", "ffn2", "mix") for l in range(L)]
    red = _swap_halves("rs_swap_halves", [s for b in order for s in shards[b]])
    red = {b: red[2 * i:2 * i + 2] for i, b in enumerate(order)}
    gu1, gd1 = [red[("ffn1", l)][0] for l in range(L)], [red[("ffn1", l)][1] for l in range(L)]
    gu2, gd2 = [red[("ffn2", l)][0] for l in range(L)], [red[("ffn2", l)][1] for l in range(L)]
    (g_win, g_wout_e), (g_qkv_o, g_wout_o) = red[("mix", 0)], red[("mix", 1)]

    small_rows = [g["dnorm"][l][i] for l in range(L) for i in range(3)] + [
        g["dfinal"], jnp.pad(g["db"], ((0, 0), (0, D - LANES))), jnp.pad(loss_vec, ((0, 0), (0, D - LANES)))]
    small = jnp.concatenate(small_rows + [jnp.zeros((16 - len(small_rows), D), F32)], axis=0)
    small_sum = _sum_leading("small_sum", _gather_all_devices("small_gather", small))
    loss = small_sum[3 * L + 2, 0]
    g_norm = lax.dynamic_slice_in_dim(small_sum[:3 * L].reshape(L, 3, D), chip * ns, ns, axis=2)
    g_final = small_sum[3 * L]
    g_bf = small_sum[3 * L + 1, :n_fox][None, :]

    def matrix(tag, wt, mt, vt, srcs):
        outs = None
        for l, (g_src, g_col) in enumerate(srcs):
            outs = _adamw_layer(f"adamw_{tag}_l{l}", wt, mt, vt, g_src, g_col, l, outs)
        return tuple(outs)

    def small(tag, wt, gt, mt, vt):
        return (gt,) + _adamw("adamw_" + tag, wt, gt, mt, vt)

    results = [
        small("norm_g", norm_g, g_norm, m_norm_g, v_norm_g),
        matrix("ffn1_gate", ffn1_w_gate, m_ffn1_w_gate, v_ffn1_w_gate, [(t, 0) for t in gu1]),
        matrix("ffn1_up", ffn1_w_up, m_ffn1_w_up, v_ffn1_w_up, [(t, 1) for t in gu1]),
        matrix("ffn1_down", ffn1_w_down, m_ffn1_w_down, v_ffn1_w_down, [(t, 0) for t in gd1]),
        matrix("ffn2_gate", ffn2_w_gate, m_ffn2_w_gate, v_ffn2_w_gate, [(t, 0) for t in gu2]),
        matrix("ffn2_up", ffn2_w_up, m_ffn2_w_up, v_ffn2_w_up, [(t, 1) for t in gu2]),
        matrix("ffn2_down", ffn2_w_down, m_ffn2_w_down, v_ffn2_w_down, [(t, 0) for t in gd2]),
        matrix("even_w_in", even_w_in, m_even_w_in, v_even_w_in, [(g_win, 0)]),
        small("b_forget", even_b_forget, g_bf, m_even_b_forget, v_even_b_forget),
        matrix("even_w_out", even_w_out, m_even_w_out, v_even_w_out, [(g_wout_e, 0)]),
        matrix("odd_w_qkv", odd_w_qkv, m_odd_w_qkv, v_odd_w_qkv, [(g_qkv_o, 0)]),
        matrix("odd_w_out", odd_w_out, m_odd_w_out, v_odd_w_out, [(g_wout_o, 0)]),
        small("final_norm_g", final_norm_g, g_final, m_final_norm_g, v_final_norm_g)]
    return (loss, grad_x[None], *[r[0] for r in results], *[r[1] for r in results],
            *[r[2] for r in results], *[r[3] for r in results])
```

```python
import math

import jax
import jax.numpy as jnp
from jax import lax
from jax.experimental import pallas as pl
from jax.experimental.pallas import tpu as pltpu

F32 = jnp.float32
BF16 = jnp.bfloat16

HEAD_DIM = 128
ROPE_DIMS = 32
ROPE_THETA = 500000.0
DILATED_PATTERNS = ((128, 1), (512, 4), (2048, 16))
RMS_EPS = 1e-6
NEG_INF = -1e30
ADAM_LR = 0.001
ADAM_B1 = 0.9
ADAM_B2 = 0.999
ADAM_EPS = 1e-08
ADAM_WD = 0.01
ADAM_STEP = 10

N_CHIPS = 4
N_CORES = 2
LANES = 128
BLK = 256
VMEM_BYTES_V7X = 64 * 2**20
MESH = pl.DeviceIdType.MESH


def _vmem_limit(block_bytes, scratch_bytes=0):
    need = 2 * block_bytes + scratch_bytes + 12 * 2**20
    return int(min(need, VMEM_BYTES_V7X - 6 * 2**20))


def _nbytes(shape, dtype):
    return math.prod(shape) * jnp.dtype(dtype).itemsize


def _tile(dim, target):
    best = None
    for t in range(LANES, min(dim, target) + 1, LANES):
        if dim % t == 0:
            best = t
    assert best is not None, (dim, target)
    return best


def _row_tile(rows, target, step=8):
    if rows <= target:
        return rows
    best = None
    for t in range(step, target + 1, step):
        if rows % t == 0:
            best = t
    assert best is not None, (rows, target)
    return best


def _mm(name, a, b, mode, out_dtype, res=None, alpha=1.0, after=None, tm_target=1024, tn_target=1536, tk_target=2048):
    a3 = a.ndim == 3
    b3 = b.ndim == 3
    if mode == "nn":
        assert not a3 and not b3
        (M, K), (K2, N) = a.shape, b.shape
    elif mode == "nt":
        assert not b3
        if a3:
            P, M, Kp = a.shape
            K = P * Kp
        else:
            M, K = a.shape
        N, K2 = b.shape
    else:
        assert mode == "tn" and not a3
        K, M = a.shape
        if b3:
            P, K2, Np = b.shape
            N = P * Np
        else:
            K2, N = b.shape
    assert K == K2, (name, a.shape, b.shape)
    tm = _tile(M, tm_target)
    tn = _tile(Np if b3 else N, tn_target)
    tk = _tile(Kp if a3 else K, tk_target)
    nk = K // tk
    grid = (M // tm, N // tn, nk)

    if mode == "nn":
        a_spec = pl.BlockSpec((tm, tk), lambda i, j, k: (i, k))
        b_spec = pl.BlockSpec((tk, tn), lambda i, j, k: (k, j))
        dims = (((1,), (0,)), ((), ()))
    elif mode == "nt":
        if a3:
            nkp = Kp // tk
            a_spec = pl.BlockSpec((None, tm, tk), lambda i, j, k: (k // nkp, i, k % nkp))
        else:
            a_spec = pl.BlockSpec((tm, tk), lambda i, j, k: (i, k))
        b_spec = pl.BlockSpec((tn, tk), lambda i, j, k: (j, k))
        dims = (((1,), (1,)), ((), ()))
    else:
        a_spec = pl.BlockSpec((tk, tm), lambda i, j, k: (k, i))
        if b3:
            njp = Np // tn
            b_spec = pl.BlockSpec((None, tk, tn), lambda i, j, k: (j // njp, k, j % njp))
        else:
            b_spec = pl.BlockSpec((tk, tn), lambda i, j, k: (k, j))
        dims = (((0,), (0,)), ((), ()))
    o_spec = pl.BlockSpec((tm, tn), lambda i, j, k: (i, j))
    has_res = res is not None

    def finish(y, r_ref, o_ref):
        if alpha != 1.0:
            y = y * alpha
        if has_res:
            y = r_ref[...] + y
        o_ref[...] = y.astype(o_ref.dtype)

    n_in = 2 + has_res + (after is not None)

    def body(*refs):
        a_ref, b_ref = refs[:2]
        r_ref = refs[2] if has_res else None
        o_ref = refs[n_in]
        part = lax.dot_general(a_ref[...], b_ref[...], dims, preferred_element_type=F32)
        if nk == 1:
            finish(part, r_ref, o_ref)
            return
        acc_ref = refs[-1]
        k = pl.program_id(2)

        @pl.when(k == 0)
        def _():
            acc_ref[...] = part

        @pl.when(k > 0)
        def _():
            acc_ref[...] += part

        @pl.when(k == nk - 1)
        def _():
            finish(acc_ref[...], r_ref, o_ref)

    in_specs = [a_spec, b_spec] + ([o_spec] if has_res else []) + ([_ANY] if after is not None else [])
    args = (a, b) + ((res,) if has_res else ()) + ((after,) if after is not None else ())
    blk = (_nbytes((tm, tk), a.dtype) + _nbytes((tk, tn), b.dtype) + _nbytes((tm, tn), out_dtype)
           + (_nbytes((tm, tn), F32) if has_res else 0))
    return pl.pallas_call(
        body, name=name, grid=grid, in_specs=in_specs, out_specs=o_spec,
        out_shape=jax.ShapeDtypeStruct((M, N), out_dtype),
        scratch_shapes=[pltpu.VMEM((tm, tn), F32)] if nk > 1 else [],
        compiler_params=pltpu.CompilerParams(
            dimension_semantics=("parallel", "parallel", "arbitrary"),
            vmem_limit_bytes=_vmem_limit(blk, 2 * _nbytes((tm, tn), F32))),
    )(*args)


def _rms_fwd(name, x, g):
    S, D = x.shape
    tr = _row_tile(S, 256)

    def body(x_ref, g_ref, n_ref):
        xv = x_ref[...]
        r = lax.rsqrt(jnp.mean(xv * xv, axis=-1, keepdims=True) + RMS_EPS)
        n_ref[...] = (xv * r * g_ref[...]).astype(BF16)

    return pl.pallas_call(
        body, name=name, grid=(S // tr,),
        in_specs=[pl.BlockSpec((tr, D), lambda i: (i, 0)), pl.BlockSpec((1, D), lambda i: (0, 0))],
        out_specs=pl.BlockSpec((tr, D), lambda i: (i, 0)),
        out_shape=jax.ShapeDtypeStruct((S, D), BF16),
        compiler_params=pltpu.CompilerParams(dimension_semantics=("parallel",)),
    )(x, g)


def _rms_bwd(name, dn, x, g, dres):
    S, D = x.shape
    tr = _row_tile(S, 256)

    def body(dn_ref, x_ref, g_ref, dres_ref, dx_ref, dxb_ref, dg_ref):
        i = pl.program_id(0)
        xv = x_ref[...]
        dnv = dn_ref[...]
        r = lax.rsqrt(jnp.mean(xv * xv, axis=-1, keepdims=True) + RMS_EPS)
        u = dnv * g_ref[...]
        dot = jnp.mean(u * xv, axis=-1, keepdims=True)
        dx = dres_ref[...] + r * u - xv * (r * r * r * dot)
        dx_ref[...] = dx
        dxb_ref[...] = dx.astype(BF16)

        @pl.when(i == 0)
        def _():
            dg_ref[...] = jnp.zeros_like(dg_ref)

        dg_ref[...] += jnp.sum(dnv * xv * r, axis=0, keepdims=True)

    row = pl.BlockSpec((tr, D), lambda i: (i, 0))
    vec = pl.BlockSpec((1, D), lambda i: (0, 0))
    return pl.pallas_call(
        body, name=name, grid=(S // tr,),
        in_specs=[row, row, vec, row], out_specs=[row, row, vec],
        out_shape=[jax.ShapeDtypeStruct((S, D), F32), jax.ShapeDtypeStruct((S, D), BF16),
                   jax.ShapeDtypeStruct((1, D), F32)],
        compiler_params=pltpu.CompilerParams(dimension_semantics=("arbitrary",)),
    )(dn, x, g, dres)


def _loss_head(name, x, g, target):
    S, D = x.shape
    tr = _row_tile(S, 256)

    def body(x_ref, g_ref, t_ref, dx_ref, dxb_ref, dg_ref, loss_ref):
        i = pl.program_id(0)
        xv = x_ref[...]
        gv = g_ref[...]
        r = lax.rsqrt(jnp.mean(xv * xv, axis=-1, keepdims=True) + RMS_EPS)
        diff = xv * r * gv - t_ref[...]
        part = 0.5 * jnp.sum(jnp.mean(diff * diff, axis=-1, keepdims=True), axis=0, keepdims=True)
        dy = diff * (1.0 / D)
        u = dy * gv
        dot = jnp.mean(u * xv, axis=-1, keepdims=True)
        dx = r * u - xv * (r * r * r * dot)
        dx_ref[...] = dx
        dxb_ref[...] = dx.astype(BF16)

        @pl.when(i == 0)
        def _():
            dg_ref[...] = jnp.zeros_like(dg_ref)
            loss_ref[...] = jnp.zeros_like(loss_ref)

        dg_ref[...] += jnp.sum(dy * xv * r, axis=0, keepdims=True)
        loss_ref[...] += jnp.broadcast_to(part, loss_ref.shape)

    row = pl.BlockSpec((tr, D), lambda i: (i, 0))
    vec = pl.BlockSpec((1, D), lambda i: (0, 0))
    lvec = pl.BlockSpec((1, LANES), lambda i: (0, 0))
    return pl.pallas_call(
        body, name=name, grid=(S // tr,),
        in_specs=[row, vec, row], out_specs=[row, row, vec, lvec],
        out_shape=[jax.ShapeDtypeStruct((S, D), F32), jax.ShapeDtypeStruct((S, D), BF16),
                   jax.ShapeDtypeStruct((1, D), F32), jax.ShapeDtypeStruct((1, LANES), F32)],
        compiler_params=pltpu.CompilerParams(dimension_semantics=("arbitrary",)),
    )(x, g, target)


def _ffn_up(name, n, wgu, fs, tm_target=512):
    S, D = n.shape
    nslab = wgu.shape[1] // (2 * fs)
    tm = _tile(S, tm_target)

    def body(n_ref, w_ref, gu_ref, h_ref):
        y = jnp.dot(n_ref[...], w_ref[...], preferred_element_type=F32)
        gu_ref[...] = y
        gv = y[:, :fs]
        h_ref[...] = (gv * jax.nn.sigmoid(gv) * y[:, fs:]).astype(BF16)

    blk = _nbytes((tm, D), BF16) + _nbytes((D, 2 * fs), BF16) + _nbytes((tm, 2 * fs), F32) + _nbytes((tm, fs), BF16)
    return pl.pallas_call(
        body, name=name, grid=(nslab, S // tm),
        in_specs=[pl.BlockSpec((tm, D), lambda k, i: (i, 0)), pl.BlockSpec((D, 2 * fs), lambda k, i: (0, k))],
        out_specs=[pl.BlockSpec((tm, 2 * fs), lambda k, i: (i, k)), pl.BlockSpec((tm, fs), lambda k, i: (i, k))],
        out_shape=[jax.ShapeDtypeStruct((S, nslab * 2 * fs), F32), jax.ShapeDtypeStruct((S, nslab * fs), BF16)],
        compiler_params=pltpu.CompilerParams(dimension_semantics=("parallel", "parallel"),
                                             vmem_limit_bytes=_vmem_limit(blk, _nbytes((tm, 2 * fs), F32))),
    )(n, wgu)


def _ffn_dact(name, dyb, wd, gu, fs, alpha, after=None, tm_target=512):
    S, D = dyb.shape
    nslab = wd.shape[0] // fs
    tm = _tile(S, tm_target)

    def body(*refs):
        d_ref, w_ref, gu_ref = refs[:3]
        o_ref = refs[-1]
        dhv = _dot_nt(d_ref[...], w_ref[...]) * alpha
        gv = gu_ref[:, :fs]
        uv = gu_ref[:, fs:]
        sg = jax.nn.sigmoid(gv)
        silu = gv * sg
        o_ref[:, :fs] = (dhv * uv * (sg + silu * (1.0 - sg))).astype(BF16)
        o_ref[:, fs:] = (dhv * silu).astype(BF16)

    in_specs = [pl.BlockSpec((tm, D), lambda k, i: (i, 0)), pl.BlockSpec((fs, D), lambda k, i: (k, 0)),
                pl.BlockSpec((tm, 2 * fs), lambda k, i: (i, k))] + ([_ANY] if after is not None else [])
    args = (dyb, wd, gu) + ((after,) if after is not None else ())
    blk = _nbytes((tm, D), BF16) + _nbytes((fs, D), BF16) + _nbytes((tm, 2 * fs), F32) + _nbytes((tm, 2 * fs), BF16)
    return pl.pallas_call(
        body, name=name, grid=(nslab, S // tm), in_specs=in_specs,
        out_specs=pl.BlockSpec((tm, 2 * fs), lambda k, i: (i, k)),
        out_shape=jax.ShapeDtypeStruct((S, nslab * 2 * fs), BF16),
        compiler_params=pltpu.CompilerParams(dimension_semantics=("parallel", "parallel"),
                                             vmem_limit_bytes=_vmem_limit(blk, 2 * _nbytes((tm, fs), F32))),
    )(*args)


def _tri_rows(r0, nrows, ncols, lower):
    row = lax.broadcasted_iota(jnp.int32, (nrows, ncols), 0) + r0
    col = lax.broadcasted_iota(jnp.int32, (nrows, ncols), 1)
    return jnp.where((col <= row) if lower else (col >= row), 1.0, 0.0).astype(F32)


def _gate_fwd(name, hf, b):
    S = hf.shape[0]
    tb = _row_tile(S, 256)

    def body(hf_ref, b_ref, cf_ref, cft_ref, lf_ref):
        zz = hf_ref[...] + b_ref[...]
        lf_ref[...] = jnp.minimum(zz, 0.0) - jnp.log1p(jnp.exp(-jnp.abs(zz)))

        def blk(i, c):
            r0 = pl.multiple_of(i * tb, tb)
            tri = _tri_rows(r0, tb, S, True)
            cf_ref[pl.ds(r0, tb), :] = jnp.dot(tri, lf_ref[...], precision=lax.Precision.HIGHEST,
                                               preferred_element_type=F32)
            return c

        lax.fori_loop(0, S // tb, blk, 0)
        cft_ref[...] = cf_ref[...].T

    full = pl.BlockSpec((S, LANES), lambda: (0, 0))
    return pl.pallas_call(
        body, name=name, in_specs=[full, pl.BlockSpec((1, LANES), lambda: (0, 0))],
        out_specs=[full, pl.BlockSpec((LANES, S), lambda: (0, 0))],
        out_shape=[jax.ShapeDtypeStruct((S, LANES), F32), jax.ShapeDtypeStruct((LANES, S), F32)],
        scratch_shapes=[pltpu.VMEM((S, LANES), F32)],
    )(hf, b)


def _gate_bwd(name, dcft, drow, hf, b):
    S = hf.shape[0]
    tb = _row_tile(S, 256)

    def body(dcft_ref, drow_ref, hf_ref, b_ref, dhf_ref, db_ref, dcf_ref, dlf_ref):
        dcf_ref[...] = dcft_ref[...].T + drow_ref[...]

        def blk(i, c):
            r0 = pl.multiple_of(i * tb, tb)
            tri = _tri_rows(r0, tb, S, False)
            dlf_ref[pl.ds(r0, tb), :] = jnp.dot(tri, dcf_ref[...], precision=lax.Precision.HIGHEST,
                                                preferred_element_type=F32)
            return c

        lax.fori_loop(0, S // tb, blk, 0)
        zz = hf_ref[...] + b_ref[...]
        dhf = dlf_ref[...] * jax.nn.sigmoid(-zz)
        dhf_ref[...] = dhf.astype(BF16)
        db_ref[...] = jnp.sum(dhf, axis=0, keepdims=True)

    full = pl.BlockSpec((S, LANES), lambda: (0, 0))
    vec = pl.BlockSpec((1, LANES), lambda: (0, 0))
    return pl.pallas_call(
        body, name=name, in_specs=[pl.BlockSpec((LANES, S), lambda: (0, 0)), full, full, vec],
        out_specs=[full, vec],
        out_shape=[jax.ShapeDtypeStruct((S, LANES), BF16), jax.ShapeDtypeStruct((1, LANES), F32)],
        scratch_shapes=[pltpu.VMEM((S, LANES), F32), pltpu.VMEM((S, LANES), F32)],
    )(dcft, drow, hf, b)


def _rope_tables(S):
    half = ROPE_DIMS // 2
    freqs = ROPE_THETA ** (-jnp.arange(half, dtype=F32) / half)
    ang = jnp.arange(S, dtype=F32)[:, None] * freqs[None, :]
    cos, sin = jnp.cos(ang), jnp.sin(ang)
    pad = HEAD_DIM - ROPE_DIMS
    c = jnp.concatenate([cos, cos, jnp.ones((S, pad), F32)], axis=1)
    s = jnp.concatenate([-sin, sin, jnp.zeros((S, pad), F32)], axis=1)
    return c, s


def _rope_swap(x):
    half = ROPE_DIMS // 2
    lane = lax.broadcasted_iota(jnp.int32, x.shape, 1)
    upper = jnp.where(lane < ROPE_DIMS, pltpu.roll(x, half, 1), 0.0)
    return jnp.where(lane < half, pltpu.roll(x, HEAD_DIM - half, 1), upper)


def _rope(x, c, s):
    return x * c + _rope_swap(x) * s


def _rope_t(dy, c, s):
    return dy * c + _rope_swap(dy * s)


def _split_dot(x, t):
    hi = x.astype(BF16)
    lo = (x - hi.astype(F32)).astype(BF16)
    return (jnp.dot(hi, t, preferred_element_type=F32) + jnp.dot(lo, t, preferred_element_type=F32))


_NT = (((1,), (1,)), ((), ()))
_TN = (((0,), (0,)), ((), ()))


def _dot_nt(a, b):
    return lax.dot_general(a, b, _NT, preferred_element_type=F32)


def _dot_tn(a, b):
    return lax.dot_general(a, b, _TN, preferred_element_type=F32)


def _blk(i):
    return pl.ds(pl.multiple_of(i * BLK, BLK), BLK)


def _dilated_mult(delta):
    c = jnp.zeros(delta.shape, F32)
    for window, dil in DILATED_PATTERNS:
        ok = (delta >= 0) & (delta <= window) & ((delta & (dil - 1)) == 0)
        c = c + jnp.where(ok, 1.0, 0.0)
    return c


def _query_block(S):
    return min(512, S)


def _offsets(d, bq):
    row = jnp.arange(bq, dtype=jnp.int32)[:, None]
    col = jnp.arange(BLK, dtype=jnp.int32)[None, :]
    return d * BLK + row - col


def _causal_tables(bq, strict):
    r = bq // BLK
    tabs = []
    for d in range(-(r - 1), 1):
        delta = _offsets(d, bq)
        tabs.append(jnp.where((delta > 0) if strict else (delta >= 0), 1.0, 0.0))
    tabs.append(jnp.ones((bq, BLK), F32))
    return jnp.stack(tabs).astype(F32)


def _dilated_tables(bq):
    r = bq // BLK
    limit = sorted(w for w, _ in DILATED_PATTERNS)[-2]
    assert all(BLK % dil == 0 for _, dil in DILATED_PATTERNS)
    d_far = -(-(limit + BLK) // BLK)
    tabs = []
    for d in range(-(r - 1), d_far + 1):
        mult = _dilated_mult(_offsets(d, bq))
        tabs.append(jnp.where(mult > 0, jnp.log(jnp.maximum(mult, 1.0)), NEG_INF))
    return jnp.stack(tabs).astype(F32)


def _qblk(i, bq):
    return pl.ds(pl.multiple_of(i * bq, bq), bq)


def _sb_block(z, valid, t_ex, run):
    t = jnp.log1p(jnp.exp(-jnp.abs(z)))
    lsig = jnp.minimum(z, 0.0) - t
    m = -(jnp.maximum(z, 0.0) + t) * valid
    after = _split_dot(m, t_ex)
    a = jnp.exp(lsig + after + run) * valid
    return a, m, lsig


def _attn_fwd_wide(name, hq, layer_kind, n_heads, n_sb, cf=None, cft=None, rope_c=None, rope_s=None):
    S = hq.shape[0]
    D = n_heads * HEAD_DIM
    bq = _query_block(S)
    r = bq // BLK
    nq = S // bq
    scale = HEAD_DIM ** -0.5
    even = layer_kind == "even"
    if even:
        tabs = (jnp.where(_causal_tables(bq, False) > 0, 0.0, NEG_INF), _causal_tables(bq, True))
    else:
        tabs = (_dilated_tables(bq),)
    n_tab = tabs[0].shape[0]

    def body(*refs):
        if even:
            q_ref, k_ref, v_ref, cf_ref, cft_ref, bias_ref, valid_ref, o_ref, ob_ref, lse_ref, qs, ks, vs = refs
        else:
            q_ref, k_ref, v_ref, c_ref, s_ref, bias_ref, o_ref, ob_ref, lse_ref, qs, ks, vs = refs
        h = pl.program_id(0)
        if even:
            qs[...] = q_ref[...].astype(BF16)
            ks[...] = k_ref[...].astype(BF16)
        else:
            qs[...] = _rope(q_ref[...], c_ref[...], s_ref[...]).astype(BF16)
            ks[...] = _rope(k_ref[...], c_ref[...], s_ref[...]).astype(BF16)
        vs[...] = v_ref[...].astype(BF16)

        def softmax_head(hh):
            def qblock(i, carry):
                qi = qs[_qblk(i, bq), :]
                if even:
                    lane = lax.broadcasted_iota(jnp.int32, (bq, LANES), 1)
                    cfq = jnp.sum(jnp.where(lane == hh, cf_ref[_qblk(i, bq), :], 0.0), axis=1, keepdims=True)

                def kblock(j, c):
                    m_run, l_run, acc = c
                    z = _dot_nt(qi, ks[_blk(j), :]) * scale + bias_ref[jnp.minimum(r * i - j + (r - 1), n_tab - 1)]
                    if even:
                        z = z + (cfq - cft_ref[hh, :, _blk(j)])
                    m_new = jnp.maximum(m_run, jnp.max(z, axis=1, keepdims=True))
                    p = jnp.exp(z - m_new)
                    alpha = jnp.exp(m_run - m_new)
                    l_new = alpha * l_run + jnp.sum(p, axis=1, keepdims=True)
                    acc = alpha * acc + jnp.dot(p.astype(BF16), vs[_blk(j), :], preferred_element_type=F32)
                    return m_new, l_new, acc

                init = (jnp.full((bq, 1), NEG_INF, F32), jnp.zeros((bq, 1), F32), jnp.zeros((bq, HEAD_DIM), F32))
                m_run, l_run, acc = lax.fori_loop(0, r * (i + 1), kblock, init)
                o = acc / l_run
                o_ref[_qblk(i, bq), :] = o
                ob_ref[_qblk(i, bq), :] = o.astype(BF16)
                lse_ref[_qblk(i, bq), :] = jnp.broadcast_to(m_run + jnp.log(l_run), (bq, HEAD_DIM))
                return carry

            lax.fori_loop(0, nq, qblock, 0)

        def sb_head():
            row = lax.broadcasted_iota(jnp.int32, (BLK, BLK), 0)
            col = lax.broadcasted_iota(jnp.int32, (BLK, BLK), 1)
            t_ex = jnp.where(row > col, 1.0, 0.0).astype(BF16)

            def qblock(i, carry):
                qi = qs[_qblk(i, bq), :]

                def kblock(jj, c):
                    run, acc, rest = c
                    j = r * (i + 1) - 1 - jj
                    z = _dot_nt(qi, ks[_blk(j), :]) * scale
                    a, m, _ = _sb_block(z, valid_ref[jnp.minimum(r * i - j + (r - 1), r)], t_ex, run)
                    vj = vs[_blk(j), :]
                    hi = a.astype(BF16)
                    lo = (a - hi.astype(F32)).astype(BF16)
                    acc = acc + jnp.dot(hi, vj, preferred_element_type=F32)
                    rest = rest + jnp.dot(lo, vj, preferred_element_type=F32)
                    return run + jnp.sum(m, axis=1, keepdims=True), acc, rest

                zero = jnp.zeros((bq, HEAD_DIM), F32)
                _, acc, rest = lax.fori_loop(0, r * (i + 1), kblock, (jnp.zeros((bq, 1), F32), zero, zero))
                o_ref[_qblk(i, bq), :] = acc + rest
                ob_ref[_qblk(i, bq), :] = acc.astype(BF16)
                lse_ref[_qblk(i, bq), :] = jnp.zeros((bq, HEAD_DIM), F32)
                return carry

            lax.fori_loop(0, nq, qblock, 0)

        if even:
            @pl.when(h < n_sb)
            def _():
                sb_head()

            @pl.when(h >= n_sb)
            def _():
                softmax_head(h - n_sb)
        else:
            softmax_head(h)

    head = lambda off: pl.BlockSpec((S, HEAD_DIM), lambda h, off=off: (0, off + h))
    full = pl.BlockSpec((S, LANES), lambda h: (0, 0))
    tab_specs = [pl.BlockSpec(t.shape, lambda h: (0, 0, 0)) for t in tabs]
    if even:
        extra_specs = [full, pl.BlockSpec(cft.shape, lambda h: (0, 0, 0))] + tab_specs
        extra = (cf, cft) + tabs
    else:
        extra_specs = [full, full] + tab_specs
        extra = (rope_c, rope_s) + tabs
    blk_bytes = 8 * _nbytes((S, HEAD_DIM), F32) + sum(_nbytes(t.shape, F32) for t in tabs)
    return pl.pallas_call(
        body, name=name, grid=(n_heads,),
        in_specs=[head(0), head(n_heads), head(2 * n_heads)] + extra_specs,
        out_specs=[head(0), head(0), head(0)],
        out_shape=[jax.ShapeDtypeStruct((S, D), F32), jax.ShapeDtypeStruct((S, D), BF16),
                   jax.ShapeDtypeStruct((S, D), F32)],
        scratch_shapes=[pltpu.VMEM((S, HEAD_DIM), BF16)] * 3,
        compiler_params=pltpu.CompilerParams(dimension_semantics=("arbitrary",),
                                             vmem_limit_bytes=_vmem_limit(blk_bytes, 3 * _nbytes((S, HEAD_DIM), BF16))),
    )(hq, hq, hq, *extra)


def _attn_bwd_wide(name, hq, do, o, lse, layer_kind, n_heads, n_sb, cf=None, cft=None, rope_c=None, rope_s=None):
    S = hq.shape[0]
    D = n_heads * HEAD_DIM
    bq = _query_block(S)
    r = bq // BLK
    nq = S // bq
    scale = HEAD_DIM ** -0.5
    even = layer_kind == "even"
    if even:
        tabs = (jnp.where(_causal_tables(bq, False) > 0, 0.0, NEG_INF), _causal_tables(bq, True))
    else:
        tabs = (_dilated_tables(bq),)
    n_tab = tabs[0].shape[0]

    def body(*refs):
        if even:
            (q_ref, k_ref, v_ref, do_ref, o_ref, lse_ref, cf_ref, cft_ref, bias_ref, valid_ref,
             dh_ref, dcft_ref, drow_ref, qs, ks, vs, dos, dq_acc, dk_acc, dv_acc) = refs
        else:
            (q_ref, k_ref, v_ref, do_ref, o_ref, lse_ref, c_ref, s_ref, bias_ref,
             dh_ref, qs, ks, vs, dos, dq_acc, dk_acc, dv_acc) = refs
        h = pl.program_id(0)
        if even:
            qs[...] = q_ref[...].astype(BF16)
            ks[...] = k_ref[...].astype(BF16)

            @pl.when(h == 0)
            def _():
                dcft_ref[...] = jnp.zeros_like(dcft_ref)
                drow_ref[...] = jnp.zeros_like(drow_ref)
        else:
            qs[...] = _rope(q_ref[...], c_ref[...], s_ref[...]).astype(BF16)
            ks[...] = _rope(k_ref[...], c_ref[...], s_ref[...]).astype(BF16)
        vs[...] = v_ref[...].astype(BF16)
        dos[...] = do_ref[...].astype(BF16)
        dk_acc[...] = jnp.zeros_like(dk_acc)
        dv_acc[...] = jnp.zeros_like(dv_acc)

        def softmax_head(hh):
            def qblock(i, carry):
                qi = qs[_qblk(i, bq), :]
                doi = dos[_qblk(i, bq), :]
                dvec = jnp.sum(do_ref[_qblk(i, bq), :] * o_ref[_qblk(i, bq), :], axis=1, keepdims=True)
                lse_i = jnp.max(lse_ref[_qblk(i, bq), :], axis=1, keepdims=True)
                if even:
                    lane = lax.broadcasted_iota(jnp.int32, (bq, LANES), 1)
                    cfq = jnp.sum(jnp.where(lane == hh, cf_ref[_qblk(i, bq), :], 0.0), axis=1, keepdims=True)

                def kblock(j, c):
                    dq, ds_rows = c
                    kj = ks[_blk(j), :]
                    z = _dot_nt(qi, kj) * scale + bias_ref[jnp.minimum(r * i - j + (r - 1), n_tab - 1)]
                    if even:
                        z = z + (cfq - cft_ref[hh, :, _blk(j)])
                    p = jnp.exp(z - lse_i)
                    dp = _dot_nt(doi, vs[_blk(j), :])
                    ds = p * (dp - dvec)
                    dsb = (ds * scale).astype(BF16)
                    dk_acc[_blk(j), :] += _dot_tn(dsb, qi)
                    dv_acc[_blk(j), :] += _dot_tn(p.astype(BF16), doi)
                    if even:
                        dcft_ref[hh, :, _blk(j)] += -jnp.sum(ds, axis=0, keepdims=True)
                    return (dq + jnp.dot(dsb, kj, preferred_element_type=F32),
                            ds_rows + jnp.sum(ds, axis=1, keepdims=True))

                dq, ds_rows = lax.fori_loop(0, r * (i + 1), kblock,
                                            (jnp.zeros((bq, HEAD_DIM), F32), jnp.zeros((bq, 1), F32)))
                dq_acc[_qblk(i, bq), :] = dq
                if even:
                    drow_ref[_qblk(i, bq), :] += jnp.where(lane == hh, ds_rows, 0.0)
                return carry

            lax.fori_loop(0, nq, qblock, 0)

        def sb_head():
            row = lax.broadcasted_iota(jnp.int32, (BLK, BLK), 0)
            col = lax.broadcasted_iota(jnp.int32, (BLK, BLK), 1)
            t_ex = jnp.where(row > col, 1.0, 0.0).astype(BF16)
            t_in = jnp.where(row >= col, 1.0, 0.0).astype(BF16)

            def qblock(i, carry):
                qi = qs[_qblk(i, bq), :]
                doi = dos[_qblk(i, bq), :]
                nkb = r * (i + 1)
                e_tot = jnp.sum(doi.astype(F32) * o_ref[_qblk(i, bq), :], axis=1, keepdims=True)
                zero = jnp.zeros((bq, 1), F32)

                def kblock(jj, c):
                    run, e_run, dq = c
                    j = nkb - 1 - jj
                    kj = ks[_blk(j), :]
                    z = _dot_nt(qi, kj) * scale
                    valid = valid_ref[jnp.minimum(r * i - j + (r - 1), r)]
                    a, m, lsig = _sb_block(z, valid, t_ex, run)
                    sig = jnp.exp(lsig)
                    e = _dot_nt(doi, vs[_blk(j), :]) * a
                    e_before = e_tot - (_split_dot(e, t_in) + e_run)
                    dz = (e * (1.0 - sig) - sig * e_before) * valid
                    dzb = (dz * scale).astype(BF16)
                    dk_acc[_blk(j), :] += _dot_tn(dzb, qi)
                    dv_acc[_blk(j), :] += _dot_tn(a.astype(BF16), doi)
                    return (run + jnp.sum(m, axis=1, keepdims=True), e_run + jnp.sum(e, axis=1, keepdims=True),
                            dq + jnp.dot(dzb, kj, preferred_element_type=F32))

                _, _, dq = lax.fori_loop(0, nkb, kblock, (zero, zero, jnp.zeros((bq, HEAD_DIM), F32)))
                dq_acc[_qblk(i, bq), :] = dq
                return carry

            lax.fori_loop(0, nq, qblock, 0)

        if even:
            @pl.when(h < n_sb)
            def _():
                sb_head()

            @pl.when(h >= n_sb)
            def _():
                softmax_head(h - n_sb)

            dh_ref[0] = dq_acc[...].astype(BF16)
            dh_ref[1] = dk_acc[...].astype(BF16)
        else:
            softmax_head(h)
            dh_ref[0] = _rope_t(dq_acc[...], c_ref[...], s_ref[...]).astype(BF16)
            dh_ref[1] = _rope_t(dk_acc[...], c_ref[...], s_ref[...]).astype(BF16)
        dh_ref[2] = dv_acc[...].astype(BF16)

    head = lambda off: pl.BlockSpec((S, HEAD_DIM), lambda h, off=off: (0, off + h))
    full = pl.BlockSpec((S, LANES), lambda h: (0, 0))
    tfull = pl.BlockSpec((n_heads - n_sb, 1, S), lambda h: (0, 0, 0))
    tab_specs = [pl.BlockSpec(t.shape, lambda h: (0, 0, 0)) for t in tabs]
    dh_spec = pl.BlockSpec((3, S, HEAD_DIM), lambda h: (0, 0, h))
    dh_shape = jax.ShapeDtypeStruct((3, S, D), BF16)
    if even:
        extra_specs, extra = [full, tfull] + tab_specs, (cf, cft) + tabs
        out_specs = [dh_spec, tfull, full]
        out_shape = [dh_shape, jax.ShapeDtypeStruct((n_heads - n_sb, 1, S), F32),
                     jax.ShapeDtypeStruct((S, LANES), F32)]
    else:
        extra_specs, extra = [full, full] + tab_specs, (rope_c, rope_s) + tabs
        out_specs = [dh_spec]
        out_shape = [dh_shape]
    blk_bytes = 10 * _nbytes((S, HEAD_DIM), F32) + sum(_nbytes(t.shape, F32) for t in tabs)
    scratch_bytes = 4 * _nbytes((S, HEAD_DIM), BF16) + 3 * _nbytes((S, HEAD_DIM), F32)
    return pl.pallas_call(
        body, name=name, grid=(n_heads,),
        in_specs=[head(0), head(n_heads), head(2 * n_heads), head(0), head(0), head(0)] + extra_specs,
        out_specs=out_specs, out_shape=out_shape,
        scratch_shapes=[pltpu.VMEM((S, HEAD_DIM), BF16)] * 4 + [pltpu.VMEM((S, HEAD_DIM), F32)] * 3,
        compiler_params=pltpu.CompilerParams(dimension_semantics=("arbitrary",),
                                             vmem_limit_bytes=_vmem_limit(blk_bytes, scratch_bytes)),
    )(hq, hq, hq, do, o, lse, *extra)


def _ffn_fwd(tag, x, g, wgu, wd, fs):
    n = _rms_fwd(tag + "_norm", x, g)
    gu, h = _ffn_up(tag + "_gu", n, wgu, fs)
    y = _mm(tag + "_down", h, wd, "nn", F32, res=x, alpha=0.5)
    return y, (x, g, n, gu, h)


def _ffn_bwd(tag, dx, dxb, wgu, wd, fs, saved, after=None, emit=None):
    x, g, n, gu, h = saved
    dgu = _ffn_dact(tag + "_dgu", dxb, wd, gu, fs, 0.5, after=after)
    dwd = _mm(tag + "_dwd", h, dxb, "tn", BF16, alpha=0.5)
    dwgu = _mm(tag + "_dwgu", n, dgu, "tn", BF16)
    token = emit(dwgu, dwd) if emit else None
    dn = _mm(tag + "_dn", dgu, wgu, "nt", F32, after=token)
    dx_in, dxb_in, dg = _rms_bwd(tag + "_dnorm", dn, x, g, dx)
    return dx_in, dxb_in, dg, dwgu, dwd, token


def _mixer_fwd(tag, kind, x, g, wqkv, wout, n_heads, n_sb, wf=None, bf=None, rope=None):
    n = _rms_fwd(tag + "_norm", x, g)
    hq = _mm(tag + "_qkv", n, wqkv, "nn", F32)
    if kind == "even":
        hf = _mm(tag + "_gate", n, wf, "nn", F32)
        cf, cft = _gate_fwd(tag + "_cumgate", hf, bf)
        cft = cft[:n_heads - n_sb].reshape(n_heads - n_sb, 1, -1)
        o, ob, lse = _attn_fwd_wide(tag + "_attn", hq, kind, n_heads, n_sb, cf=cf, cft=cft)
    else:
        hf = cf = cft = None
        o, ob, lse = _attn_fwd_wide(tag + "_attn", hq, kind, n_heads, n_sb, rope_c=rope[0], rope_s=rope[1])
    y = _mm(tag + "_out", ob, wout, "nn", F32, res=x)
    return y, (x, g, n, hq, hf, cf, cft, o, ob, lse)


def _mixer_bwd(tag, kind, dx, dxb, wqkv, wout, n_heads, n_sb, saved, wf=None, bf=None, rope=None, after=None,
               emit=None):
    x, g, n, hq, hf, cf, cft, o, ob, lse = saved
    do = _mm(tag + "_do", dxb, wout, "nt", F32, after=after)
    dwout = _mm(tag + "_dwout", ob, dxb, "tn", BF16)
    if kind == "even":
        dh3, dcft, drow = _attn_bwd_wide(tag + "_dattn", hq, do, o, lse, kind, n_heads, n_sb, cf=cf, cft=cft)
    else:
        (dh3,) = _attn_bwd_wide(tag + "_dattn", hq, do, o, lse, kind, n_heads, n_sb, rope_c=rope[0], rope_s=rope[1])
    dwqkv = _mm(tag + "_dwqkv", n, dh3, "tn", BF16)
    dwf = db = dhf = None
    if kind == "even":
        n_fox = n_heads - n_sb
        dcft = jnp.pad(dcft.reshape(n_fox, -1), ((0, LANES - n_fox), (0, 0)))
        dhf, db = _gate_bwd(tag + "_dcumgate", dcft, drow, hf, bf)
        dwf = _mm(tag + "_dwf", n, dhf, "tn", BF16)
    token = emit(dwqkv, dwout, dwf) if emit else None
    dn = _mm(tag + "_dn", dh3, wqkv, "nt", F32, after=token)
    if kind == "even":
        dn = _mm(tag + "_dn_gate", dhf, wf, "nt", F32, res=dn)
    dx_in, dxb_in, dg = _rms_bwd(tag + "_dnorm", dn, x, g, dx)
    return dx_in, dxb_in, dg, dwqkv, dwout, dwf, db, token


def _local_step(x, target, w, fs, n_heads, n_sb, emit=None):
    S, D = x.shape
    rope = _rope_tables(S)
    kinds = ("even", "odd")
    saved = []
    h = x
    if callable(w):
        fetch, w = w, {"norm_g": w("norm_g", None), "final_g": w("final_g", None),
                       "wgu1": [None, None], "wd1": [None, None], "wgu2": [None, None], "wd2": [None, None]}
    else:
        fetch = None
    for l, kind in enumerate(kinds):
        ng = [w["norm_g"][l, i][None, :] for i in range(3)]
        if fetch:
            w["wgu1"][l], w["wd1"][l] = fetch(("ffn1", l), h)
        h, s1 = _ffn_fwd(f"l{l}_ffn1", h, ng[0], w["wgu1"][l], w["wd1"][l], fs)
        if fetch:
            w.update(fetch(("mix", l), h))
        if kind == "even":
            h, s2 = _mixer_fwd(f"l{l}_mix", kind, h, ng[1], w["wqkv_e"], w["wout_e"], n_heads, n_sb,
                               wf=w["wf"], bf=w["bf"])
        else:
            h, s2 = _mixer_fwd(f"l{l}_mix", kind, h, ng[1], w["wqkv_o"], w["wout_o"], n_heads, n_sb, rope=rope)
        if fetch:
            w["wgu2"][l], w["wd2"][l] = fetch(("ffn2", l), h)
        h, s3 = _ffn_fwd(f"l{l}_ffn2", h, ng[2], w["wgu2"][l], w["wd2"][l], fs)
        saved.append((s1, s2, s3))

    dx, dxb, dfinal, loss = _loss_head("loss_head", h, w["final_g"], target)
    grads = {"dfinal": dfinal, "dnorm": [[None] * 3 for _ in kinds],
             "dwgu1": [None, None], "dwd1": [None, None], "dwgu2": [None, None], "dwd2": [None, None]}
    hand = lambda block: (lambda *mats: emit(block, mats)) if emit else None
    token = None
    for l in (1, 0):
        kind = kinds[l]
        s1, s2, s3 = saved[l]
        dx, dxb, dg, grads["dwgu2"][l], grads["dwd2"][l], token = _ffn_bwd(
            f"l{l}_ffn2", dx, dxb, w["wgu2"][l], w["wd2"][l], fs, s3, after=token, emit=hand(("ffn2", l)))
        grads["dnorm"][l][2] = dg
        if kind == "even":
            dx, dxb, dg, grads["dwqkv_e"], grads["dwout_e"], grads["dwf"], grads["db"], token = _mixer_bwd(
                f"l{l}_mix", kind, dx, dxb, w["wqkv_e"], w["wout_e"], n_heads, n_sb, s2, wf=w["wf"], bf=w["bf"],
                after=token, emit=hand(("mix", l)))
        else:
            dx, dxb, dg, grads["dwqkv_o"], grads["dwout_o"], _, _, token = _mixer_bwd(
                f"l{l}_mix", kind, dx, dxb, w["wqkv_o"], w["wout_o"], n_heads, n_sb, s2, rope=rope,
                after=token, emit=hand(("mix", l)))
        grads["dnorm"][l][1] = dg
        dx, dxb, dg, grads["dwgu1"][l], grads["dwd1"][l], token = _ffn_bwd(
            f"l{l}_ffn1", dx, dxb, w["wgu1"][l], w["wd1"][l], fs, s1, after=token, emit=hand(("ffn1", l)))
        grads["dnorm"][l][0] = dg
    return loss, dx, grads


def _cast_into(name, shard, layer, chip, full_shape, place, full=None):
    R, C = shard.shape[-2:]
    tr = _row_tile(R, 512, step=16)
    if layer is None:
        in_spec = pl.BlockSpec((tr, C), lambda i, k: (i, 0))
    else:
        in_spec = pl.BlockSpec((None, tr, C), lambda i, k: (layer, i, 0))
    lead = (None,) * (len(full_shape) - 2)
    out_spec = pl.BlockSpec(lead + (tr, C), lambda i, k: place(i, k[0]))

    def body(*refs):
        k_ref, w_ref = refs[:2]
        o_ref = refs[-1]
        o_ref[...] = w_ref[...].astype(BF16)

    in_specs = [in_spec] + ([_ANY] if full is not None else [])
    args = (chip, shard) + ((full,) if full is not None else ())
    grid_spec = pltpu.PrefetchScalarGridSpec(num_scalar_prefetch=1, grid=(R // tr,), in_specs=in_specs, out_specs=out_spec)
    return pl.pallas_call(
        body, name=name, grid_spec=grid_spec, out_shape=jax.ShapeDtypeStruct(full_shape, BF16),
        input_output_aliases={2: 0} if full is not None else {},
        compiler_params=pltpu.CompilerParams(dimension_semantics=("arbitrary",)),
    )(*args)


def _region_shape(grad, kind):
    if kind == "lead":
        return grad.shape[1] // N_CORES, grad.shape[2]
    rows, cols = grad.shape
    if kind == "cols":
        return rows // N_CORES, cols // N_CHIPS
    return rows // (N_CHIPS * N_CORES), cols


def _region_add(name, grad, kind, landed, core):
    rh, cw = _region_shape(grad, kind)
    tr = _row_tile(rh, 256, step=16)
    nrb = rh // tr
    if kind == "cols":
        g_spec = pl.BlockSpec((tr, cw), lambda k, r, c: (c[0] * nrb + r, k))
    elif kind == "rows":
        g_spec = pl.BlockSpec((tr, cw), lambda k, r, c: ((N_CORES * k + c[0]) * nrb + r, 0))
    else:
        g_spec = pl.BlockSpec((None, tr, cw), lambda k, r, c: (k, c[0] * nrb + r, 0))
    l_spec = pl.BlockSpec((None, tr, cw), lambda k, r, c: (k, r, 0))

    def body(c_ref, g_ref, l_ref, o_ref):
        o_ref[...] = (g_ref[...].astype(F32) + l_ref[...].astype(F32)).astype(BF16)

    grid_spec = pltpu.PrefetchScalarGridSpec(
        num_scalar_prefetch=1, grid=(N_CHIPS, nrb), in_specs=[g_spec, l_spec], out_specs=l_spec)
    return pl.pallas_call(
        body, name=name, grid_spec=grid_spec, out_shape=jax.ShapeDtypeStruct(landed.shape, BF16),
        compiler_params=pltpu.CompilerParams(dimension_semantics=("parallel", "parallel"),
                                             vmem_limit_bytes=_vmem_limit(3 * _nbytes((tr, cw), F32))),
    )(core, grad, landed)


def _chip_sum(name, pair, landed, pos):
    _, rh, cw = pair.shape
    tr = _row_tile(rh, max(16, 2**20 // (cw * 4)), step=16)
    nrb = rh // tr

    def body(p_ref, own_ref, l_ref, o_ref):
        acc = own_ref[...].astype(F32)
        for s in range(N_CHIPS - 1):
            acc = acc + l_ref[s].astype(F32)
        o_ref[...] = acc

    grid_spec = pltpu.PrefetchScalarGridSpec(
        num_scalar_prefetch=1, grid=(nrb,),
        in_specs=[pl.BlockSpec((None, tr, cw), lambda r, p: (p[0], r, 0)),
                  pl.BlockSpec((N_CHIPS - 1, tr, cw), lambda r, p: (0, r, 0))],
        out_specs=pl.BlockSpec((tr, cw), lambda r, p: (p[1] * nrb + r, 0)))
    return pl.pallas_call(
        body, name=name, grid_spec=grid_spec, out_shape=jax.ShapeDtypeStruct((N_CORES * rh, cw), F32),
        compiler_params=pltpu.CompilerParams(dimension_semantics=("arbitrary",)),
    )(pos, pair, landed)


def _sum_leading(name, parts):
    n, R, C = parts.shape
    tr = _row_tile(R, max(8, (2**20 // (C * 4)) // 8 * 8))

    def body(p_ref, o_ref):
        acc = p_ref[0]
        for s in range(1, n):
            acc = acc + p_ref[s]
        o_ref[...] = acc

    return pl.pallas_call(
        body, name=name, grid=(R // tr,),
        in_specs=[pl.BlockSpec((n, tr, C), lambda i: (0, i, 0))],
        out_specs=pl.BlockSpec((tr, C), lambda i: (i, 0)),
        out_shape=jax.ShapeDtypeStruct((R, C), F32),
        compiler_params=pltpu.CompilerParams(dimension_semantics=("parallel",)),
    )(parts)


def _adamw(name, w, g, m, v):
    shape = w.shape
    to2d = lambda t: t.reshape(-1, shape[-1]) if t.ndim > 1 else t.reshape(1, -1)
    w2, g2, m2, v2 = (to2d(t) for t in (w, g, m, v))
    R, C = w2.shape
    tr = _row_tile(R, 256)

    def body(w_ref, g_ref, m_ref, v_ref, d_ref, nm_ref, nv_ref):
        gv = g_ref[...]
        nm = ADAM_B1 * m_ref[...] + (1.0 - ADAM_B1) * gv
        nv = ADAM_B2 * v_ref[...] + (1.0 - ADAM_B2) * (gv * gv)
        m_hat = nm / (1.0 - ADAM_B1 ** ADAM_STEP)
        v_hat = nv / (1.0 - ADAM_B2 ** ADAM_STEP)
        d_ref[...] = -ADAM_LR * (m_hat / (jnp.sqrt(v_hat) + ADAM_EPS) + ADAM_WD * w_ref[...])
        nm_ref[...] = nm
        nv_ref[...] = nv

    spec = pl.BlockSpec((tr, C), lambda i: (i, 0))
    sds = jax.ShapeDtypeStruct((R, C), F32)
    d, nm, nv = pl.pallas_call(
        body, name=name, grid=(R // tr,), in_specs=[spec] * 4, out_specs=[spec] * 3, out_shape=[sds] * 3,
        compiler_params=pltpu.CompilerParams(dimension_semantics=("parallel",),
                                             vmem_limit_bytes=_vmem_limit(7 * _nbytes((tr, C), F32))),
    )(w2, g2, m2, v2)
    return d.reshape(shape), nm.reshape(shape), nv.reshape(shape)


_ANY = pl.BlockSpec(memory_space=pl.ANY)


def _mesh_pos():
    return lax.axis_index("x"), lax.axis_index("y"), lax.axis_index("c")


def _other_chips(x, y):
    return [(1 - x, y), (x, 1 - y), (1 - x, 1 - y)]


def _gather_over_chips(name, fulls, views):
    n = len(views)
    nf = len(fulls)

    def body(*refs):
        full = refs[nf:2 * nf]
        ici_send, ici_recv, d2d_send, d2d_recv = refs[2 * nf:]
        x, y, c = _mesh_pos()
        chips = _other_chips(x, y)
        mine = 2 * x + y
        sibling = (x, y, 1 - c)

        def ici(a, p, k):
            i, view, _ = views[a]
            part = view(full[i], k, c)
            return pltpu.make_async_remote_copy(
                src_ref=part, dst_ref=part, send_sem=ici_send.at[a, p], recv_sem=ici_recv.at[a, p],
                device_id=(*chips[p], c), device_id_type=MESH)

        def d2d(a, p, h):
            i, view, _ = views[a]
            px, py = chips[p]
            part = view(full[i], 2 * px + py, h)
            return pltpu.make_async_remote_copy(
                src_ref=part, dst_ref=part, send_sem=d2d_send.at[a, p], recv_sem=d2d_recv.at[a, p],
                device_id=sibling, device_id_type=MESH)

        sends = [ici(a, p, mine) for a in range(n) for p in range(3)]
        for cp in sends:
            cp.start()
        passed = []
        for a in range(n):
            for p, (px, py) in enumerate(chips):
                ici(a, p, 2 * px + py).wait_recv()
                if views[a][2]:
                    fwd = d2d(a, p, c)
                    fwd.start()
                    passed.append(fwd)
        for a in range(n):
            if views[a][2]:
                for p in range(3):
                    d2d(a, p, 1 - c).wait_recv()
        for cp in sends + passed:
            cp.wait_send()

    return pl.pallas_call(
        body, name=name, in_specs=[_ANY] * nf, out_specs=[_ANY] * nf,
        out_shape=[jax.ShapeDtypeStruct(f.shape, f.dtype) for f in fulls],
        input_output_aliases={i: i for i in range(nf)},
        scratch_shapes=[pltpu.SemaphoreType.DMA((n, 3))] * 4,
        compiler_params=pltpu.CompilerParams(has_side_effects=True),
    )(*fulls)


_HBM = pl.BlockSpec(memory_space=pltpu.HBM)
_SEM = pl.BlockSpec(memory_space=pltpu.SEMAPHORE)


def _in_hbm(arrays):
    return [pltpu.with_memory_space_constraint(a, pltpu.HBM) for a in arrays]


def _gather_start(name, fulls, views, after):
    nf = len(fulls)
    ng = 1 + max(g for _, _, g in views)

    def body(*refs):
        full = refs[nf + 1:2 * nf + 1]
        send_sems, recv_sems = refs[2 * nf + 1:2 * nf + 1 + ng], refs[2 * nf + 1 + ng:]
        x, y, c = _mesh_pos()
        chips = _other_chips(x, y)
        for i, view, g in views:
            part = view(full[i], 2 * x + y, c)
            for px, py in chips:
                pltpu.make_async_remote_copy(
                    src_ref=part, dst_ref=part, send_sem=send_sems[g], recv_sem=recv_sems[g],
                    device_id=(px, py, c), device_id_type=MESH).start()

    outs = pl.pallas_call(
        body, name=name, in_specs=[_HBM] * nf + [_ANY], out_specs=[_HBM] * nf + [_SEM] * (2 * ng),
        out_shape=[pltpu.HBM(f.shape, f.dtype) for f in fulls] + [pltpu.SemaphoreType.DMA(())] * (2 * ng),
        input_output_aliases={i: i for i in range(nf)},
        compiler_params=pltpu.CompilerParams(has_side_effects=pltpu.SideEffectType.DATAFLOW_SIDE_EFFECTING),
    )(*_in_hbm(fulls), after)
    return list(outs[:nf]), list(outs[nf:nf + ng]), list(outs[nf + ng:])


def _gather_wait(name, fulls, views, send_sem, recv_sem, after):
    nf = len(fulls)

    def body(*refs):
        send_ref, recv_ref = refs[nf], refs[nf + 1]
        full = refs[nf + 3:]
        x, y, c = _mesh_pos()
        copies = [pltpu.make_async_remote_copy(
            src_ref=view(full[i], 2 * x + y, c), dst_ref=view(full[i], 2 * px + py, c),
            send_sem=send_ref, recv_sem=recv_ref, device_id=(px, py, c), device_id_type=MESH)
            for i, view in views for px, py in _other_chips(x, y)]
        for cp in copies:
            cp.wait_send()
        for cp in copies:
            cp.wait_recv()

    outs = pl.pallas_call(
        body, name=name, in_specs=[_HBM] * nf + [_SEM, _SEM, _ANY], out_specs=[_HBM] * nf,
        out_shape=[pltpu.HBM(f.shape, f.dtype) for f in fulls],
        input_output_aliases={i: i for i in range(nf)},
        compiler_params=pltpu.CompilerParams(has_side_effects=pltpu.SideEffectType.DATAFLOW_SIDE_EFFECTING),
    )(*fulls, send_sem, recv_sem, after)
    return list(outs)


def _forward_to_sibling(name, fulls, views):
    n, nf = len(views), len(fulls)

    def body(*refs):
        full = refs[nf:2 * nf]
        send_sems, recv_sems = refs[2 * nf:]
        x, y, c = _mesh_pos()
        chips = _other_chips(x, y)

        def copy(a, p, h):
            i, view = views[a]
            px, py = chips[p]
            part = view(full[i], 2 * px + py, h)
            return pltpu.make_async_remote_copy(
                src_ref=part, dst_ref=part, send_sem=send_sems.at[a, p], recv_sem=recv_sems.at[a, p],
                device_id=(x, y, 1 - c), device_id_type=MESH)

        sends = [copy(a, p, c) for a in range(n) for p in range(3)]
        for cp in sends:
            cp.start()
        for a in range(n):
            for p in range(3):
                copy(a, p, 1 - c).wait_recv()
        for cp in sends:
            cp.wait_send()

    return pl.pallas_call(
        body, name=name, in_specs=[_ANY] * nf, out_specs=[_ANY] * nf,
        out_shape=[jax.ShapeDtypeStruct(f.shape, f.dtype) for f in fulls],
        input_output_aliases={i: i for i in range(nf)},
        scratch_shapes=[pltpu.SemaphoreType.DMA((n, 3))] * 2,
        compiler_params=pltpu.CompilerParams(has_side_effects=True),
    )(*fulls)


def _region_view(ref, kind, k, c):
    if kind == "lead":
        rh = ref.shape[1] // N_CORES
        return ref.at[k, pl.ds(pl.multiple_of(c * rh, 8), rh), :]
    rows, cols = ref.shape
    if kind == "cols":
        rh, cw = rows // N_CORES, cols // N_CHIPS
        return ref.at[pl.ds(pl.multiple_of(c * rh, 8), rh), pl.ds(k * cw, cw)]
    rh = rows // (N_CHIPS * N_CORES)
    return ref.at[pl.ds(pl.multiple_of((N_CORES * k + c) * rh, 8), rh), :]


def _send_to_sibling(name, grads, kinds):
    n = len(grads)
    shapes = [jax.ShapeDtypeStruct((N_CHIPS,) + _region_shape(g, kd), g.dtype) for g, kd in zip(grads, kinds)]

    def body(*refs):
        g_ref, land = refs[:n], refs[n:2 * n]
        send_sems, recv_sems = refs[2 * n:]
        x, y, c = _mesh_pos()
        copies = []
        for a in range(n):
            for k in range(N_CHIPS):
                cp = pltpu.make_async_remote_copy(
                    src_ref=_region_view(g_ref[a], kinds[a], k, 1 - c), dst_ref=land[a].at[k],
                    send_sem=send_sems.at[a, k], recv_sem=recv_sems.at[a, k],
                    device_id=(x, y, 1 - c), device_id_type=MESH)
                cp.start()
                copies.append(cp)
        for cp in copies:
            cp.wait_recv()
        for cp in copies:
            cp.wait_send()

    return pl.pallas_call(
        body, name=name, in_specs=[_ANY] * n, out_specs=[_ANY] * n, out_shape=shapes,
        scratch_shapes=[pltpu.SemaphoreType.DMA((n, N_CHIPS)), pltpu.SemaphoreType.DMA((n, N_CHIPS))],
        compiler_params=pltpu.CompilerParams(has_side_effects=True),
    )(*grads)


def _scatter_start(name, pair_sums):
    n = len(pair_sums)
    lands = [lax.empty((N_CHIPS - 1,) + p.shape[1:], p.dtype) for p in pair_sums]

    def body(*refs):
        p_ref, land = refs[2 * n:3 * n], refs[3 * n:4 * n]
        send_sem, recv_sem, token = refs[4 * n:]
        x, y, c = _mesh_pos()
        for a in range(n):
            for p, (px, py) in enumerate(_other_chips(x, y)):
                pltpu.make_async_remote_copy(
                    src_ref=p_ref[a].at[2 * px + py], dst_ref=land[a].at[p], send_sem=send_sem, recv_sem=recv_sem,
                    device_id=(px, py, c), device_id_type=MESH).start()
        token[...] = jnp.zeros_like(token)

    outs = pl.pallas_call(
        body, name=name, in_specs=[_HBM] * (2 * n),
        out_specs=[_HBM] * (2 * n) + [_SEM, _SEM, pl.BlockSpec(memory_space=pltpu.VMEM)],
        out_shape=[pltpu.HBM(t.shape, t.dtype) for t in list(pair_sums) + lands]
        + [pltpu.SemaphoreType.DMA(()), pltpu.SemaphoreType.DMA(()), jax.ShapeDtypeStruct((8, LANES), F32)],
        input_output_aliases={i: i for i in range(2 * n)},
        compiler_params=pltpu.CompilerParams(has_side_effects=pltpu.SideEffectType.DATAFLOW_SIDE_EFFECTING),
    )(*_in_hbm(list(pair_sums) + lands))
    return list(outs[:n]), list(outs[n:2 * n]), outs[2 * n], outs[2 * n + 1], outs[2 * n + 2]


def _scatter_wait(name, pair_sums, lands, send_sem, recv_sem, after):
    n = len(pair_sums)

    def body(*refs):
        send_ref, recv_ref = refs[2 * n], refs[2 * n + 1]
        p_ref, land = refs[2 * n + 3:3 * n + 3], refs[3 * n + 3:]
        x, y, c = _mesh_pos()
        copies = [pltpu.make_async_remote_copy(
            src_ref=p_ref[a].at[2 * px + py], dst_ref=land[a].at[p], send_sem=send_ref, recv_sem=recv_ref,
            device_id=(px, py, c), device_id_type=MESH)
            for a in range(n) for p, (px, py) in enumerate(_other_chips(x, y))]
        for cp in copies:
            cp.wait_send()
        for cp in copies:
            cp.wait_recv()

    outs = pl.pallas_call(
        body, name=name, in_specs=[_HBM] * (2 * n) + [_SEM, _SEM, _ANY], out_specs=[_HBM] * (2 * n),
        out_shape=[pltpu.HBM(t.shape, t.dtype) for t in list(pair_sums) + list(lands)],
        input_output_aliases={i: i for i in range(2 * n)},
        compiler_params=pltpu.CompilerParams(has_side_effects=pltpu.SideEffectType.DATAFLOW_SIDE_EFFECTING),
    )(*pair_sums, *lands, send_sem, recv_sem, after)
    return list(outs[:n]), list(outs[n:])


def _swap_halves(name, shards):
    n = len(shards)

    def body(*refs):
        out = refs[n:2 * n]
        send_sems, recv_sems = refs[2 * n:]
        x, y, c = _mesh_pos()
        sends = []
        for a in range(n):
            rh = out[a].shape[0] // N_CORES
            mine = out[a].at[pl.ds(pl.multiple_of(c * rh, 8), rh), :]
            cp = pltpu.make_async_remote_copy(
                src_ref=mine, dst_ref=mine, send_sem=send_sems.at[a], recv_sem=recv_sems.at[a],
                device_id=(x, y, 1 - c), device_id_type=MESH)
            cp.start()
            sends.append(cp)
        for a in range(n):
            rh = out[a].shape[0] // N_CORES
            theirs = out[a].at[pl.ds(pl.multiple_of((1 - c) * rh, 8), rh), :]
            pltpu.make_async_remote_copy(
                src_ref=theirs, dst_ref=theirs, send_sem=send_sems.at[a], recv_sem=recv_sems.at[a],
                device_id=(x, y, 1 - c), device_id_type=MESH).wait_recv()
        for cp in sends:
            cp.wait_send()

    return pl.pallas_call(
        body, name=name, in_specs=[_ANY] * n, out_specs=[_ANY] * n,
        out_shape=[jax.ShapeDtypeStruct(s.shape, s.dtype) for s in shards],
        input_output_aliases={i: i for i in range(n)},
        scratch_shapes=[pltpu.SemaphoreType.DMA((n,)), pltpu.SemaphoreType.DMA((n,))],
        compiler_params=pltpu.CompilerParams(has_side_effects=True),
    )(*shards)


def _gather_all_devices(name, block):
    R, C = block.shape
    ndev = N_CHIPS * N_CORES

    def body(b_ref, out_ref, send_sems, recv_sems, local_sem):
        x, y, c = _mesh_pos()
        mine = 4 * x + 2 * y + c
        own = pltpu.make_async_copy(b_ref, out_ref.at[mine], local_sem)
        own.start()
        sends = []
        for mask in range(1, ndev):
            fx, fy, fc = (mask >> 2) & 1, (mask >> 1) & 1, mask & 1
            px, py, pc = x ^ fx, y ^ fy, c ^ fc
            cp = pltpu.make_async_remote_copy(
                src_ref=b_ref, dst_ref=out_ref.at[mine], send_sem=send_sems.at[mask - 1],
                recv_sem=recv_sems.at[mask - 1], device_id=(px, py, pc), device_id_type=MESH)
            cp.start()
            sends.append(cp)
        for mask in range(1, ndev):
            fx, fy, fc = (mask >> 2) & 1, (mask >> 1) & 1, mask & 1
            px, py, pc = x ^ fx, y ^ fy, c ^ fc
            pltpu.make_async_remote_copy(
                src_ref=b_ref, dst_ref=out_ref.at[4 * px + 2 * py + pc], send_sem=send_sems.at[mask - 1],
                recv_sem=recv_sems.at[mask - 1], device_id=(px, py, pc), device_id_type=MESH).wait_recv()
        for cp in sends:
            cp.wait_send()
        own.wait()

    return pl.pallas_call(
        body, name=name, in_specs=[_ANY], out_specs=_ANY,
        out_shape=jax.ShapeDtypeStruct((ndev, R, C), F32),
        scratch_shapes=[pltpu.SemaphoreType.DMA((ndev - 1,)), pltpu.SemaphoreType.DMA((ndev - 1,)),
                        pltpu.SemaphoreType.DMA(())],
        compiler_params=pltpu.CompilerParams(has_side_effects=True),
    )(block)


def kernel(x, norm_g, ffn1_w_gate, ffn1_w_up, ffn1_w_down, ffn2_w_gate, ffn2_w_up, ffn2_w_down, even_w_in, even_b_forget, even_w_out, odd_w_qkv, odd_w_out, final_norm_g, loss_target, m_norm_g, m_ffn1_w_gate, m_ffn1_w_up, m_ffn1_w_down, m_ffn2_w_gate, m_ffn2_w_up, m_ffn2_w_down, m_even_w_in, m_even_b_forget, m_even_w_out, m_odd_w_qkv, m_odd_w_out, m_final_norm_g, v_norm_g, v_ffn1_w_gate, v_ffn1_w_up, v_ffn1_w_down, v_ffn2_w_gate, v_ffn2_w_up, v_ffn2_w_down, v_even_w_in, v_even_b_forget, v_even_w_out, v_odd_w_qkv, v_odd_w_out, v_final_norm_g):
    _, S, D = x.shape
    L = norm_g.shape[0]
    assert L == 2 and even_w_in.shape[0] == 1 and odd_w_qkv.shape[0] == 1
    fs = ffn1_w_gate.shape[2]
    F = N_CHIPS * fs
    wc = even_w_in.shape[2]
    n_heads = D // HEAD_DIM
    n_fox = N_CHIPS * wc - 3 * D
    n_sb = n_heads - n_fox
    qs = odd_w_qkv.shape[2]
    os_ = even_w_out.shape[1]
    ns = norm_g.shape[2]
    xi, yi, ci = _mesh_pos()
    chip = 2 * xi + yi

    pos = jnp.stack([chip, ci]).astype(jnp.int32)
    kchip = pos[:1]
    lane = lambda start, size: pl.ds(pl.multiple_of(start, LANES), size)
    sub = lambda start, size: pl.ds(pl.multiple_of(start, 16), size)
    gate_view = lambda r, k, h: r.at[sub(h * (D // 2), D // 2), lane(k * 2 * fs, fs)]
    up_view = lambda r, k, h: r.at[sub(h * (D // 2), D // 2), lane(k * 2 * fs + fs, fs)]
    down_view = lambda r, k, h: r.at[sub(k * fs + h * (fs // 2), fs // 2), :]
    out_view = lambda r, k, h: r.at[sub(k * os_ + h * (os_ // 2), os_ // 2), :]
    tr_d = _row_tile(fs, 512, step=16)
    tr_o = _row_tile(os_, 512, step=16)
    ffn_w = {"ffn1": (ffn1_w_gate, ffn1_w_up, ffn1_w_down), "ffn2": (ffn2_w_gate, ffn2_w_up, ffn2_w_down)}
    win_view = lambda r, k, h: r.at[k, sub(h * (D // 2), D // 2), :]
    qkv_view = lambda r, k, h: r.at[sub(h * (D // 2), D // 2), lane(k * qs, qs)]
    fulls, views, groups = [], [], {}
    for l in range(L):
        for blk in ("ffn1", "mix", "ffn2"):
            o, v0 = len(fulls), len(views)
            if blk == "mix" and l == 0:
                fulls += [_cast_into("cast_win", even_w_in, 0, kchip, (N_CHIPS, D, wc), lambda i, k: (k, i, 0)),
                          _cast_into("cast_wout_e", even_w_out, 0, kchip, (D, D), lambda i, k: (k * (os_ // tr_o) + i, 0))]
                views += [(o, win_view), (o + 1, out_view)]
            elif blk == "mix":
                fulls += [_cast_into("cast_wqkv_o", odd_w_qkv, 0, kchip, (D, N_CHIPS * qs), lambda i, k: (i, k)),
                          _cast_into("cast_wout_o", odd_w_out, 0, kchip, (D, D), lambda i, k: (k * (os_ // tr_o) + i, 0))]
                views += [(o, qkv_view), (o + 1, out_view)]
            else:
                wg, wu, wd = ffn_w[blk]
                t = f"cast_{blk}_l{l}"
                gu = _cast_into(t + "_gate", wg, l, kchip, (D, 2 * F), lambda i, k: (i, 2 * k))
                gu = _cast_into(t + "_up", wu, l, kchip, (D, 2 * F), lambda i, k: (i, 2 * k + 1), full=gu)
                dn = _cast_into(t + "_down", wd, l, kchip, (F, D), lambda i, k: (k * (fs // tr_d) + i, 0))
                fulls += [gu, dn]
                views += [(o, gate_view), (o, up_view), (o + 1, down_view)]
            gid = len(groups)
            views[v0:] = [(i, view, gid) for i, view in views[v0:]]
            groups[(blk, l)] = (gid, list(range(o, len(fulls))), list(range(v0, len(views))))
    norm_own = lax.dynamic_update_slice(jnp.zeros((L, 3, N_CHIPS * ns), F32), norm_g, (0, 0, chip * ns))
    (norm_full,) = _gather_over_chips("gather_norm", [norm_own], [(0, lambda r, k, h: r.at[:, :, lane(k * ns, ns)], False)])
    started, send_sems, recv_sems = _gather_start("gather_start", fulls, views, norm_full)

    def fetch(block, after):
        if block == "norm_g":
            return norm_full
        if block == "final_g":
            return final_norm_g[None, :]
        gid, arrays, rows = groups[block]
        tag = f"{block[0]}_l{block[1]}"
        local = [(views[a][0] - arrays[0], views[a][1]) for a in rows]
        got = _gather_wait("gather_wait_" + tag, [started[i] for i in arrays], local, send_sems[gid], recv_sems[gid],
                           x if after is None else after)
        got = _forward_to_sibling("gather_pass_" + tag, got, local)
        if block[0] != "mix":
            return got
        if block[1] == 1:
            return {"wqkv_o": got[0], "wout_o": got[1]}
        win = jnp.concatenate([got[0][k] for k in range(N_CHIPS)], axis=1)
        return {"wqkv_e": win[:, :3 * D], "wf": jnp.pad(win[:, 3 * D:], ((0, 0), (0, LANES - n_fox))),
                "bf": jnp.pad(even_b_forget, ((0, 0), (0, LANES - n_fox))), "wout_e": got[1]}

    pending, shards = [], {}

    def finish(after):
        block, pair, lands, ssem, rsem = pending.pop(0)
        tag = f"{block[0]}_l{block[1]}"
        pair, lands = _scatter_wait("rs_chip_wait_" + tag, pair, lands, ssem, rsem, after)
        shards[block] = [_chip_sum(f"rs_chip_add_{tag}_{a}", p, ld, pos) for a, (p, ld) in enumerate(zip(pair, lands))]

    def emit(block, mats):
        blk, l = block
        tag = f"{blk}_l{l}"
        kinds = ["cols", "rows"]
        if blk == "mix" and l == 0:
            dwqkv, dwout, dwf = mats
            dwin = jnp.concatenate([dwqkv, dwf[:, :n_fox]], axis=1)
            mats = [jnp.stack([dwin[:, k * wc:(k + 1) * wc] for k in range(N_CHIPS)]), dwout]
            kinds = ["lead", "rows"]
        elif blk == "mix":
            mats = list(mats[:2])
        else:
            mats = list(mats)
        landed = _send_to_sibling("rs_pair_send_" + tag, mats, kinds)
        pair = [_region_add(f"rs_pair_add_{tag}_{a}", m, kd, ld, pos[1:])
                for a, (m, kd, ld) in enumerate(zip(mats, kinds, landed))]
        pair, lands, ssem, rsem, token = _scatter_start("rs_chip_start_" + tag, pair)
        if pending:
            finish(token)
        pending.append((block, pair, lands, ssem, rsem))
        return token

    loss_vec, grad_x, g = _local_step(x[0], loss_target[0], fetch, fs, n_heads, n_sb, emit=emit)
    finish(grad_x)
    order = [(b, l) for b in ("ffn1", "ffn2", "mix") for l in range(L)]
    red = _swap_halves("rs_swap_halves", [s for b in order for s in shards[b]])
    red = {b: red[2 * i:2 * i + 2] for i, b in enumerate(order)}
    gu1, gd1 = [red[("ffn1", l)][0] for l in range(L)], [red[("ffn1", l)][1] for l in range(L)]
    gu2, gd2 = [red[("ffn2", l)][0] for l in range(L)], [red[("ffn2", l)][1] for l in range(L)]
    (g_win, g_wout_e), (g_qkv_o, g_wout_o) = red[("mix", 0)], red[("mix", 1)]

    small_rows = [g["dnorm"][l][i] for l in range(L) for i in range(3)] + [
        g["dfinal"], jnp.pad(g["db"], ((0, 0), (0, D - LANES))), jnp.pad(loss_vec, ((0, 0), (0, D - LANES)))]
    small = jnp.concatenate(small_rows + [jnp.zeros((16 - len(small_rows), D), F32)], axis=0)
    small_sum = _sum_leading("small_sum", _gather_all_devices("small_gather", small))
    loss = small_sum[3 * L + 2, 0]
    g_norm = lax.dynamic_slice_in_dim(small_sum[:3 * L].reshape(L, 3, D), chip * ns, ns, axis=2)
    g_final = small_sum[3 * L]
    g_bf = small_sum[3 * L + 1, :n_fox][None, :]

    grads = [
        g_norm,
        jnp.stack([t[:, :fs] for t in gu1]), jnp.stack([t[:, fs:] for t in gu1]), jnp.stack(gd1),
        jnp.stack([t[:, :fs] for t in gu2]), jnp.stack([t[:, fs:] for t in gu2]), jnp.stack(gd2),
        g_win[None], g_bf, g_wout_e[None], g_qkv_o[None], g_wout_o[None], g_final]
    weights = [norm_g, ffn1_w_gate, ffn1_w_up, ffn1_w_down, ffn2_w_gate, ffn2_w_up, ffn2_w_down,
               even_w_in, even_b_forget, even_w_out, odd_w_qkv, odd_w_out, final_norm_g]
    ms = [m_norm_g, m_ffn1_w_gate, m_ffn1_w_up, m_ffn1_w_down, m_ffn2_w_gate, m_ffn2_w_up, m_ffn2_w_down,
          m_even_w_in, m_even_b_forget, m_even_w_out, m_odd_w_qkv, m_odd_w_out, m_final_norm_g]
    vs = [v_norm_g, v_ffn1_w_gate, v_ffn1_w_up, v_ffn1_w_down, v_ffn2_w_gate, v_ffn2_w_up, v_ffn2_w_down,
          v_even_w_in, v_even_b_forget, v_even_w_out, v_odd_w_qkv, v_odd_w_out, v_final_norm_g]
    deltas, new_ms, new_vs = [], [], []
    for i, (wt, gt, mt, vt) in enumerate(zip(weights, grads, ms, vs)):
        d, nm, nv = _adamw(f"adamw_{i}", wt, gt, mt, vt)
        deltas.append(d)
        new_ms.append(nm)
        new_vs.append(nv)
    return (loss, grad_x[None], *grads, *deltas, *new_ms, *new_vs)
```

```python
import math

import jax
import jax.numpy as jnp
from jax import lax
from jax.experimental import pallas as pl
from jax.experimental.pallas import tpu as pltpu

F32 = jnp.float32
BF16 = jnp.bfloat16

HEAD_DIM = 128
ROPE_DIMS = 32
ROPE_THETA = 500000.0
DILATED_PATTERNS = ((128, 1), (512, 4), (2048, 16))
RMS_EPS = 1e-6
NEG_INF = -1e30
ADAM_LR = 0.001
ADAM_B1 = 0.9
ADAM_B2 = 0.999
ADAM_EPS = 1e-08
ADAM_WD = 0.01
ADAM_STEP = 10

N_CHIPS = 4
N_CORES = 2
LANES = 128
BLK = 256
VMEM_BYTES_V7X = 64 * 2**20
MESH = pl.DeviceIdType.MESH


def _vmem_limit(block_bytes, scratch_bytes=0):
    need = 2 * block_bytes + scratch_bytes + 12 * 2**20
    return int(min(need, VMEM_BYTES_V7X - 6 * 2**20))


def _nbytes(shape, dtype):
    return math.prod(shape) * jnp.dtype(dtype).itemsize


def _tile(dim, target):
    best = None
    for t in range(LANES, min(dim, target) + 1, LANES):
        if dim % t == 0:
            best = t
    assert best is not None, (dim, target)
    return best


def _row_tile(rows, target, step=8):
    if rows <= target:
        return rows
    best = None
    for t in range(step, target + 1, step):
        if rows % t == 0:
            best = t
    assert best is not None, (rows, target)
    return best


def _mm(name, a, b, mode, out_dtype, res=None, alpha=1.0, after=None, tm_target=1024, tn_target=1536, tk_target=2048):
    a3 = a.ndim == 3
    b3 = b.ndim == 3
    if mode == "nn":
        assert not a3 and not b3
        (M, K), (K2, N) = a.shape, b.shape
    elif mode == "nt":
        assert not b3
        if a3:
            P, M, Kp = a.shape
            K = P * Kp
        else:
            M, K = a.shape
        N, K2 = b.shape
    else:
        assert mode == "tn" and not a3
        K, M = a.shape
        if b3:
            P, K2, Np = b.shape
            N = P * Np
        else:
            K2, N = b.shape
    assert K == K2, (name, a.shape, b.shape)
    tm = _tile(M, tm_target)
    tn = _tile(Np if b3 else N, tn_target)
    tk = _tile(Kp if a3 else K, tk_target)
    nk = K // tk
    grid = (M // tm, N // tn, nk)

    if mode == "nn":
        a_spec = pl.BlockSpec((tm, tk), lambda i, j, k: (i, k))
        b_spec = pl.BlockSpec((tk, tn), lambda i, j, k: (k, j))
        dims = (((1,), (0,)), ((), ()))
    elif mode == "nt":
        if a3:
            nkp = Kp // tk
            a_spec = pl.BlockSpec((None, tm, tk), lambda i, j, k: (k // nkp, i, k % nkp))
        else:
            a_spec = pl.BlockSpec((tm, tk), lambda i, j, k: (i, k))
        b_spec = pl.BlockSpec((tn, tk), lambda i, j, k: (j, k))
        dims = (((1,), (1,)), ((), ()))
    else:
        a_spec = pl.BlockSpec((tk, tm), lambda i, j, k: (k, i))
        if b3:
            njp = Np // tn
            b_spec = pl.BlockSpec((None, tk, tn), lambda i, j, k: (j // njp, k, j % njp))
        else:
            b_spec = pl.BlockSpec((tk, tn), lambda i, j, k: (k, j))
        dims = (((0,), (0,)), ((), ()))
    o_spec = pl.BlockSpec((tm, tn), lambda i, j, k: (i, j))
    has_res = res is not None

    def finish(y, r_ref, o_ref):
        if alpha != 1.0:
            y = y * alpha
        if has_res:
            y = r_ref[...] + y
        o_ref[...] = y.astype(o_ref.dtype)

    n_in = 2 + has_res + (after is not None)

    def body(*refs):
        a_ref, b_ref = refs[:2]
        r_ref = refs[2] if has_res else None
        o_ref = refs[n_in]
        part = lax.dot_general(a_ref[...], b_ref[...], dims, preferred_element_type=F32)
        if nk == 1:
            finish(part, r_ref, o_ref)
            return
        acc_ref = refs[-1]
        k = pl.program_id(2)

        @pl.when(k == 0)
        def _():
            acc_ref[...] = part

        @pl.when(k > 0)
        def _():
            acc_ref[...] += part

        @pl.when(k == nk - 1)
        def _():
            finish(acc_ref[...], r_ref, o_ref)

    in_specs = [a_spec, b_spec] + ([o_spec] if has_res else []) + ([_ANY] if after is not None else [])
    args = (a, b) + ((res,) if has_res else ()) + ((after,) if after is not None else ())
    blk = (_nbytes((tm, tk), a.dtype) + _nbytes((tk, tn), b.dtype) + _nbytes((tm, tn), out_dtype)
           + (_nbytes((tm, tn), F32) if has_res else 0))
    return pl.pallas_call(
        body, name=name, grid=grid, in_specs=in_specs, out_specs=o_spec,
        out_shape=jax.ShapeDtypeStruct((M, N), out_dtype),
        scratch_shapes=[pltpu.VMEM((tm, tn), F32)] if nk > 1 else [],
        compiler_params=pltpu.CompilerParams(
            dimension_semantics=("parallel", "parallel", "arbitrary"),
            vmem_limit_bytes=_vmem_limit(blk, 2 * _nbytes((tm, tn), F32))),
    )(*args)


def _rms_fwd(name, x, g):
    S, D = x.shape
    tr = _row_tile(S, 256)

    def body(x_ref, g_ref, n_ref):
        xv = x_ref[...]
        r = lax.rsqrt(jnp.mean(xv * xv, axis=-1, keepdims=True) + RMS_EPS)
        n_ref[...] = (xv * r * g_ref[...]).astype(BF16)

    return pl.pallas_call(
        body, name=name, grid=(S // tr,),
        in_specs=[pl.BlockSpec((tr, D), lambda i: (i, 0)), pl.BlockSpec((1, D), lambda i: (0, 0))],
        out_specs=pl.BlockSpec((tr, D), lambda i: (i, 0)),
        out_shape=jax.ShapeDtypeStruct((S, D), BF16),
        compiler_params=pltpu.CompilerParams(dimension_semantics=("parallel",)),
    )(x, g)


def _rms_bwd(name, dn, x, g, dres):
    S, D = x.shape
    tr = _row_tile(S, 256)

    def body(dn_ref, x_ref, g_ref, dres_ref, dx_ref, dxb_ref, dg_ref):
        i = pl.program_id(0)
        xv = x_ref[...]
        dnv = dn_ref[...]
        r = lax.rsqrt(jnp.mean(xv * xv, axis=-1, keepdims=True) + RMS_EPS)
        u = dnv * g_ref[...]
        dot = jnp.mean(u * xv, axis=-1, keepdims=True)
        dx = dres_ref[...] + r * u - xv * (r * r * r * dot)
        dx_ref[...] = dx
        dxb_ref[...] = dx.astype(BF16)

        @pl.when(i == 0)
        def _():
            dg_ref[...] = jnp.zeros_like(dg_ref)

        dg_ref[...] += jnp.sum(dnv * xv * r, axis=0, keepdims=True)

    row = pl.BlockSpec((tr, D), lambda i: (i, 0))
    vec = pl.BlockSpec((1, D), lambda i: (0, 0))
    return pl.pallas_call(
        body, name=name, grid=(S // tr,),
        in_specs=[row, row, vec, row], out_specs=[row, row, vec],
        out_shape=[jax.ShapeDtypeStruct((S, D), F32), jax.ShapeDtypeStruct((S, D), BF16),
                   jax.ShapeDtypeStruct((1, D), F32)],
        compiler_params=pltpu.CompilerParams(dimension_semantics=("arbitrary",)),
    )(dn, x, g, dres)


def _loss_head(name, x, g, target):
    S, D = x.shape
    tr = _row_tile(S, 256)

    def body(x_ref, g_ref, t_ref, dx_ref, dxb_ref, dg_ref, loss_ref):
        i = pl.program_id(0)
        xv = x_ref[...]
        gv = g_ref[...]
        r = lax.rsqrt(jnp.mean(xv * xv, axis=-1, keepdims=True) + RMS_EPS)
        diff = xv * r * gv - t_ref[...]
        part = 0.5 * jnp.sum(jnp.mean(diff * diff, axis=-1, keepdims=True), axis=0, keepdims=True)
        dy = diff * (1.0 / D)
        u = dy * gv
        dot = jnp.mean(u * xv, axis=-1, keepdims=True)
        dx = r * u - xv * (r * r * r * dot)
        dx_ref[...] = dx
        dxb_ref[...] = dx.astype(BF16)

        @pl.when(i == 0)
        def _():
            dg_ref[...] = jnp.zeros_like(dg_ref)
            loss_ref[...] = jnp.zeros_like(loss_ref)

        dg_ref[...] += jnp.sum(dy * xv * r, axis=0, keepdims=True)
        loss_ref[...] += jnp.broadcast_to(part, loss_ref.shape)

    row = pl.BlockSpec((tr, D), lambda i: (i, 0))
    vec = pl.BlockSpec((1, D), lambda i: (0, 0))
    lvec = pl.BlockSpec((1, LANES), lambda i: (0, 0))
    return pl.pallas_call(
        body, name=name, grid=(S // tr,),
        in_specs=[row, vec, row], out_specs=[row, row, vec, lvec],
        out_shape=[jax.ShapeDtypeStruct((S, D), F32), jax.ShapeDtypeStruct((S, D), BF16),
                   jax.ShapeDtypeStruct((1, D), F32), jax.ShapeDtypeStruct((1, LANES), F32)],
        compiler_params=pltpu.CompilerParams(dimension_semantics=("arbitrary",)),
    )(x, g, target)


def _ffn_up(name, n, wgu, fs, tm_target=512):
    S, D = n.shape
    nslab = wgu.shape[1] // (2 * fs)
    tm = _tile(S, tm_target)

    def body(n_ref, w_ref, gu_ref, h_ref):
        y = jnp.dot(n_ref[...], w_ref[...], preferred_element_type=F32)
        gu_ref[...] = y
        gv = y[:, :fs]
        h_ref[...] = (gv * jax.nn.sigmoid(gv) * y[:, fs:]).astype(BF16)

    blk = _nbytes((tm, D), BF16) + _nbytes((D, 2 * fs), BF16) + _nbytes((tm, 2 * fs), F32) + _nbytes((tm, fs), BF16)
    return pl.pallas_call(
        body, name=name, grid=(nslab, S // tm),
        in_specs=[pl.BlockSpec((tm, D), lambda k, i: (i, 0)), pl.BlockSpec((D, 2 * fs), lambda k, i: (0, k))],
        out_specs=[pl.BlockSpec((tm, 2 * fs), lambda k, i: (i, k)), pl.BlockSpec((tm, fs), lambda k, i: (i, k))],
        out_shape=[jax.ShapeDtypeStruct((S, nslab * 2 * fs), F32), jax.ShapeDtypeStruct((S, nslab * fs), BF16)],
        compiler_params=pltpu.CompilerParams(dimension_semantics=("parallel", "parallel"),
                                             vmem_limit_bytes=_vmem_limit(blk, _nbytes((tm, 2 * fs), F32))),
    )(n, wgu)


def _ffn_dact(name, dyb, wd, gu, fs, alpha, after=None, tm_target=512):
    S, D = dyb.shape
    nslab = wd.shape[0] // fs
    tm = _tile(S, tm_target)

    def body(*refs):
        d_ref, w_ref, gu_ref = refs[:3]
        o_ref = refs[-1]
        dhv = _dot_nt(d_ref[...], w_ref[...]) * alpha
        gv = gu_ref[:, :fs]
        uv = gu_ref[:, fs:]
        sg = jax.nn.sigmoid(gv)
        silu = gv * sg
        o_ref[:, :fs] = (dhv * uv * (sg + silu * (1.0 - sg))).astype(BF16)
        o_ref[:, fs:] = (dhv * silu).astype(BF16)

    in_specs = [pl.BlockSpec((tm, D), lambda k, i: (i, 0)), pl.BlockSpec((fs, D), lambda k, i: (k, 0)),
                pl.BlockSpec((tm, 2 * fs), lambda k, i: (i, k))] + ([_ANY] if after is not None else [])
    args = (dyb, wd, gu) + ((after,) if after is not None else ())
    blk = _nbytes((tm, D), BF16) + _nbytes((fs, D), BF16) + _nbytes((tm, 2 * fs), F32) + _nbytes((tm, 2 * fs), BF16)
    return pl.pallas_call(
        body, name=name, grid=(nslab, S // tm), in_specs=in_specs,
        out_specs=pl.BlockSpec((tm, 2 * fs), lambda k, i: (i, k)),
        out_shape=jax.ShapeDtypeStruct((S, nslab * 2 * fs), BF16),
        compiler_params=pltpu.CompilerParams(dimension_semantics=("parallel", "parallel"),
                                             vmem_limit_bytes=_vmem_limit(blk, 2 * _nbytes((tm, fs), F32))),
    )(*args)


def _tri_rows(r0, nrows, ncols, lower):
    row = lax.broadcasted_iota(jnp.int32, (nrows, ncols), 0) + r0
    col = lax.broadcasted_iota(jnp.int32, (nrows, ncols), 1)
    return jnp.where((col <= row) if lower else (col >= row), 1.0, 0.0).astype(F32)


def _gate_fwd(name, hf, b):
    S = hf.shape[0]
    tb = _row_tile(S, 256)

    def body(hf_ref, b_ref, cf_ref, cft_ref, lf_ref):
        zz = hf_ref[...] + b_ref[...]
        lf_ref[...] = jnp.minimum(zz, 0.0) - jnp.log1p(jnp.exp(-jnp.abs(zz)))

        def blk(i, c):
            r0 = pl.multiple_of(i * tb, tb)
            tri = _tri_rows(r0, tb, S, True)
            cf_ref[pl.ds(r0, tb), :] = jnp.dot(tri, lf_ref[...], precision=lax.Precision.HIGHEST,
                                               preferred_element_type=F32)
            return c

        lax.fori_loop(0, S // tb, blk, 0)
        cft_ref[...] = cf_ref[...].T

    full = pl.BlockSpec((S, LANES), lambda: (0, 0))
    return pl.pallas_call(
        body, name=name, in_specs=[full, pl.BlockSpec((1, LANES), lambda: (0, 0))],
        out_specs=[full, pl.BlockSpec((LANES, S), lambda: (0, 0))],
        out_shape=[jax.ShapeDtypeStruct((S, LANES), F32), jax.ShapeDtypeStruct((LANES, S), F32)],
        scratch_shapes=[pltpu.VMEM((S, LANES), F32)],
    )(hf, b)


def _gate_bwd(name, dcft, drow, hf, b):
    S = hf.shape[0]
    tb = _row_tile(S, 256)

    def body(dcft_ref, drow_ref, hf_ref, b_ref, dhf_ref, db_ref, dcf_ref, dlf_ref):
        dcf_ref[...] = dcft_ref[...].T + drow_ref[...]

        def blk(i, c):
            r0 = pl.multiple_of(i * tb, tb)
            tri = _tri_rows(r0, tb, S, False)
            dlf_ref[pl.ds(r0, tb), :] = jnp.dot(tri, dcf_ref[...], precision=lax.Precision.HIGHEST,
                                                preferred_element_type=F32)
            return c

        lax.fori_loop(0, S // tb, blk, 0)
        zz = hf_ref[...] + b_ref[...]
        dhf = dlf_ref[...] * jax.nn.sigmoid(-zz)
        dhf_ref[...] = dhf.astype(BF16)
        db_ref[...] = jnp.sum(dhf, axis=0, keepdims=True)

    full = pl.BlockSpec((S, LANES), lambda: (0, 0))
    vec = pl.BlockSpec((1, LANES), lambda: (0, 0))
    return pl.pallas_call(
        body, name=name, in_specs=[pl.BlockSpec((LANES, S), lambda: (0, 0)), full, full, vec],
        out_specs=[full, vec],
        out_shape=[jax.ShapeDtypeStruct((S, LANES), BF16), jax.ShapeDtypeStruct((1, LANES), F32)],
        scratch_shapes=[pltpu.VMEM((S, LANES), F32), pltpu.VMEM((S, LANES), F32)],
    )(dcft, drow, hf, b)


def _rope_tables(S):
    half = ROPE_DIMS // 2
    freqs = ROPE_THETA ** (-jnp.arange(half, dtype=F32) / half)
    ang = jnp.arange(S, dtype=F32)[:, None] * freqs[None, :]
    cos, sin = jnp.cos(ang), jnp.sin(ang)
    pad = HEAD_DIM - ROPE_DIMS
    c = jnp.concatenate([cos, cos, jnp.ones((S, pad), F32)], axis=1)
    s = jnp.concatenate([-sin, sin, jnp.zeros((S, pad), F32)], axis=1)
    return c, s


def _rope_swap(x):
    half = ROPE_DIMS // 2
    lane = lax.broadcasted_iota(jnp.int32, x.shape, 1)
    upper = jnp.where(lane < ROPE_DIMS, pltpu.roll(x, half, 1), 0.0)
    return jnp.where(lane < half, pltpu.roll(x, HEAD_DIM - half, 1), upper)


def _rope(x, c, s):
    return x * c + _rope_swap(x) * s


def _rope_t(dy, c, s):
    return dy * c + _rope_swap(dy * s)


def _split_dot(x, t):
    hi = x.astype(BF16)
    lo = (x - hi.astype(F32)).astype(BF16)
    return (jnp.dot(hi, t, preferred_element_type=F32) + jnp.dot(lo, t, preferred_element_type=F32))


_NT = (((1,), (1,)), ((), ()))
_TN = (((0,), (0,)), ((), ()))


def _dot_nt(a, b):
    return lax.dot_general(a, b, _NT, preferred_element_type=F32)


def _dot_tn(a, b):
    return lax.dot_general(a, b, _TN, preferred_element_type=F32)


def _blk(i):
    return pl.ds(pl.multiple_of(i * BLK, BLK), BLK)


def _dilated_mult(delta):
    c = jnp.zeros(delta.shape, F32)
    for window, dil in DILATED_PATTERNS:
        ok = (delta >= 0) & (delta <= window) & ((delta & (dil - 1)) == 0)
        c = c + jnp.where(ok, 1.0, 0.0)
    return c


def _query_block(S):
    return min(512, S)


def _offsets(d, bq):
    row = jnp.arange(bq, dtype=jnp.int32)[:, None]
    col = jnp.arange(BLK, dtype=jnp.int32)[None, :]
    return d * BLK + row - col


def _causal_tables(bq, strict):
    r = bq // BLK
    tabs = []
    for d in range(-(r - 1), 1):
        delta = _offsets(d, bq)
        tabs.append(jnp.where((delta > 0) if strict else (delta >= 0), 1.0, 0.0))
    tabs.append(jnp.ones((bq, BLK), F32))
    return jnp.stack(tabs).astype(F32)


def _dilated_tables(bq):
    r = bq // BLK
    limit = sorted(w for w, _ in DILATED_PATTERNS)[-2]
    assert all(BLK % dil == 0 for _, dil in DILATED_PATTERNS)
    d_far = -(-(limit + BLK) // BLK)
    tabs = []
    for d in range(-(r - 1), d_far + 1):
        mult = _dilated_mult(_offsets(d, bq))
        tabs.append(jnp.where(mult > 0, jnp.log(jnp.maximum(mult, 1.0)), NEG_INF))
    return jnp.stack(tabs).astype(F32)


def _qblk(i, bq):
    return pl.ds(pl.multiple_of(i * bq, bq), bq)


def _sb_block(z, valid, t_ex, run):
    t = jnp.log1p(jnp.exp(-jnp.abs(z)))
    lsig = jnp.minimum(z, 0.0) - t
    m = -(jnp.maximum(z, 0.0) + t) * valid
    after = _split_dot(m, t_ex)
    a = jnp.exp(lsig + after + run) * valid
    return a, m, lsig


def _attn_fwd_wide(name, hq, layer_kind, n_heads, n_sb, cf=None, cft=None, rope_c=None, rope_s=None):
    S = hq.shape[0]
    D = n_heads * HEAD_DIM
    bq = _query_block(S)
    r = bq // BLK
    nq = S // bq
    scale = HEAD_DIM ** -0.5
    even = layer_kind == "even"
    if even:
        tabs = (jnp.where(_causal_tables(bq, False) > 0, 0.0, NEG_INF), _causal_tables(bq, True))
    else:
        tabs = (_dilated_tables(bq),)
    n_tab = tabs[0].shape[0]

    def body(*refs):
        if even:
            q_ref, k_ref, v_ref, cf_ref, cft_ref, bias_ref, valid_ref, o_ref, ob_ref, lse_ref, qs, ks, vs = refs
        else:
            q_ref, k_ref, v_ref, c_ref, s_ref, bias_ref, o_ref, ob_ref, lse_ref, qs, ks, vs = refs
        h = pl.program_id(0)
        if even:
            qs[...] = q_ref[...].astype(BF16)
            ks[...] = k_ref[...].astype(BF16)
        else:
            qs[...] = _rope(q_ref[...], c_ref[...], s_ref[...]).astype(BF16)
            ks[...] = _rope(k_ref[...], c_ref[...], s_ref[...]).astype(BF16)
        vs[...] = v_ref[...].astype(BF16)

        def softmax_head(hh):
            def qblock(i, carry):
                qi = qs[_qblk(i, bq), :]
                if even:
                    lane = lax.broadcasted_iota(jnp.int32, (bq, LANES), 1)
                    cfq = jnp.sum(jnp.where(lane == hh, cf_ref[_qblk(i, bq), :], 0.0), axis=1, keepdims=True)

                def kblock(j, c):
                    m_run, l_run, acc = c
                    z = _dot_nt(qi, ks[_blk(j), :]) * scale + bias_ref[jnp.minimum(r * i - j + (r - 1), n_tab - 1)]
                    if even:
                        z = z + (cfq - cft_ref[hh, :, _blk(j)])
                    m_new = jnp.maximum(m_run, jnp.max(z, axis=1, keepdims=True))
                    p = jnp.exp(z - m_new)
                    alpha = jnp.exp(m_run - m_new)
                    l_new = alpha * l_run + jnp.sum(p, axis=1, keepdims=True)
                    acc = alpha * acc + jnp.dot(p.astype(BF16), vs[_blk(j), :], preferred_element_type=F32)
                    return m_new, l_new, acc

                init = (jnp.full((bq, 1), NEG_INF, F32), jnp.zeros((bq, 1), F32), jnp.zeros((bq, HEAD_DIM), F32))
                m_run, l_run, acc = lax.fori_loop(0, r * (i + 1), kblock, init)
                o = acc / l_run
                o_ref[_qblk(i, bq), :] = o
                ob_ref[_qblk(i, bq), :] = o.astype(BF16)
                lse_ref[_qblk(i, bq), :] = jnp.broadcast_to(m_run + jnp.log(l_run), (bq, HEAD_DIM))
                return carry

            lax.fori_loop(0, nq, qblock, 0)

        def sb_head():
            row = lax.broadcasted_iota(jnp.int32, (BLK, BLK), 0)
            col = lax.broadcasted_iota(jnp.int32, (BLK, BLK), 1)
            t_ex = jnp.where(row > col, 1.0, 0.0).astype(BF16)

            def qblock(i, carry):
                qi = qs[_qblk(i, bq), :]

                def kblock(jj, c):
                    run, acc, rest = c
                    j = r * (i + 1) - 1 - jj
                    z = _dot_nt(qi, ks[_blk(j), :]) * scale
                    a, m, _ = _sb_block(z, valid_ref[jnp.minimum(r * i - j + (r - 1), r)], t_ex, run)
                    vj = vs[_blk(j), :]
                    hi = a.astype(BF16)
                    lo = (a - hi.astype(F32)).astype(BF16)
                    acc = acc + jnp.dot(hi, vj, preferred_element_type=F32)
                    rest = rest + jnp.dot(lo, vj, preferred_element_type=F32)
                    return run + jnp.sum(m, axis=1, keepdims=True), acc, rest

                zero = jnp.zeros((bq, HEAD_DIM), F32)
                _, acc, rest = lax.fori_loop(0, r * (i + 1), kblock, (jnp.zeros((bq, 1), F32), zero, zero))
                o_ref[_qblk(i, bq), :] = acc + rest
                ob_ref[_qblk(i, bq), :] = acc.astype(BF16)
                lse_ref[_qblk(i, bq), :] = jnp.zeros((bq, HEAD_DIM), F32)
                return carry

            lax.fori_loop(0, nq, qblock, 0)

        if even:
            @pl.when(h < n_sb)
            def _():
                sb_head()

            @pl.when(h >= n_sb)
            def _():
                softmax_head(h - n_sb)
        else:
            softmax_head(h)

    head = lambda off: pl.BlockSpec((S, HEAD_DIM), lambda h, off=off: (0, off + h))
    full = pl.BlockSpec((S, LANES), lambda h: (0, 0))
    tab_specs = [pl.BlockSpec(t.shape, lambda h: (0, 0, 0)) for t in tabs]
    if even:
        extra_specs = [full, pl.BlockSpec(cft.shape, lambda h: (0, 0, 0))] + tab_specs
        extra = (cf, cft) + tabs
    else:
        extra_specs = [full, full] + tab_specs
        extra = (rope_c, rope_s) + tabs
    blk_bytes = 8 * _nbytes((S, HEAD_DIM), F32) + sum(_nbytes(t.shape, F32) for t in tabs)
    return pl.pallas_call(
        body, name=name, grid=(n_heads,),
        in_specs=[head(0), head(n_heads), head(2 * n_heads)] + extra_specs,
        out_specs=[head(0), head(0), head(0)],
        out_shape=[jax.ShapeDtypeStruct((S, D), F32), jax.ShapeDtypeStruct((S, D), BF16),
                   jax.ShapeDtypeStruct((S, D), F32)],
        scratch_shapes=[pltpu.VMEM((S, HEAD_DIM), BF16)] * 3,
        compiler_params=pltpu.CompilerParams(dimension_semantics=("arbitrary",),
                                             vmem_limit_bytes=_vmem_limit(blk_bytes, 3 * _nbytes((S, HEAD_DIM), BF16))),
    )(hq, hq, hq, *extra)


def _attn_bwd_wide(name, hq, do, o, lse, layer_kind, n_heads, n_sb, cf=None, cft=None, rope_c=None, rope_s=None):
    S = hq.shape[0]
    D = n_heads * HEAD_DIM
    bq = _query_block(S)
    r = bq // BLK
    nq = S // bq
    scale = HEAD_DIM ** -0.5
    even = layer_kind == "even"
    if even:
        tabs = (jnp.where(_causal_tables(bq, False) > 0, 0.0, NEG_INF), _causal_tables(bq, True))
    else:
        tabs = (_dilated_tables(bq),)
    n_tab = tabs[0].shape[0]

    def body(*refs):
        if even:
            (q_ref, k_ref, v_ref, do_ref, o_ref, lse_ref, cf_ref, cft_ref, bias_ref, valid_ref,
             dh_ref, dcft_ref, drow_ref, qs, ks, vs, dos, dq_acc, dk_acc, dv_acc) = refs
        else:
            (q_ref, k_ref, v_ref, do_ref, o_ref, lse_ref, c_ref, s_ref, bias_ref,
             dh_ref, qs, ks, vs, dos, dq_acc, dk_acc, dv_acc) = refs
        h = pl.program_id(0)
        if even:
            qs[...] = q_ref[...].astype(BF16)
            ks[...] = k_ref[...].astype(BF16)

            @pl.when(h == 0)
            def _():
                dcft_ref[...] = jnp.zeros_like(dcft_ref)
                drow_ref[...] = jnp.zeros_like(drow_ref)
        else:
            qs[...] = _rope(q_ref[...], c_ref[...], s_ref[...]).astype(BF16)
            ks[...] = _rope(k_ref[...], c_ref[...], s_ref[...]).astype(BF16)
        vs[...] = v_ref[...].astype(BF16)
        dos[...] = do_ref[...].astype(BF16)
        dk_acc[...] = jnp.zeros_like(dk_acc)
        dv_acc[...] = jnp.zeros_like(dv_acc)

        def softmax_head(hh):
            def qblock(i, carry):
                qi = qs[_qblk(i, bq), :]
                doi = dos[_qblk(i, bq), :]
                dvec = jnp.sum(do_ref[_qblk(i, bq), :] * o_ref[_qblk(i, bq), :], axis=1, keepdims=True)
                lse_i = jnp.max(lse_ref[_qblk(i, bq), :], axis=1, keepdims=True)
                if even:
                    lane = lax.broadcasted_iota(jnp.int32, (bq, LANES), 1)
                    cfq = jnp.sum(jnp.where(lane == hh, cf_ref[_qblk(i, bq), :], 0.0), axis=1, keepdims=True)

                def kblock(j, c):
                    dq, ds_rows = c
                    kj = ks[_blk(j), :]
                    z = _dot_nt(qi, kj) * scale + bias_ref[jnp.minimum(r * i - j + (r - 1), n_tab - 1)]
                    if even:
                        z = z + (cfq - cft_ref[hh, :, _blk(j)])
                    p = jnp.exp(z - lse_i)
                    dp = _dot_nt(doi, vs[_blk(j), :])
                    ds = p * (dp - dvec)
                    dsb = (ds * scale).astype(BF16)
                    dk_acc[_blk(j), :] += _dot_tn(dsb, qi)
                    dv_acc[_blk(j), :] += _dot_tn(p.astype(BF16), doi)
                    if even:
                        dcft_ref[hh, :, _blk(j)] += -jnp.sum(ds, axis=0, keepdims=True)
                    return (dq + jnp.dot(dsb, kj, preferred_element_type=F32),
                            ds_rows + jnp.sum(ds, axis=1, keepdims=True))

                dq, ds_rows = lax.fori_loop(0, r * (i + 1), kblock,
                                            (jnp.zeros((bq, HEAD_DIM), F32), jnp.zeros((bq, 1), F32)))
                dq_acc[_qblk(i, bq), :] = dq
                if even:
                    drow_ref[_qblk(i, bq), :] += jnp.where(lane == hh, ds_rows, 0.0)
                return carry

            lax.fori_loop(0, nq, qblock, 0)

        def sb_head():
            row = lax.broadcasted_iota(jnp.int32, (BLK, BLK), 0)
            col = lax.broadcasted_iota(jnp.int32, (BLK, BLK), 1)
            t_ex = jnp.where(row > col, 1.0, 0.0).astype(BF16)
            t_in = jnp.where(row >= col, 1.0, 0.0).astype(BF16)

            def qblock(i, carry):
                qi = qs[_qblk(i, bq), :]
                doi = dos[_qblk(i, bq), :]
                nkb = r * (i + 1)
                e_tot = jnp.sum(doi.astype(F32) * o_ref[_qblk(i, bq), :], axis=1, keepdims=True)
                zero = jnp.zeros((bq, 1), F32)

                def kblock(jj, c):
                    run, e_run, dq = c
                    j = nkb - 1 - jj
                    kj = ks[_blk(j), :]
                    z = _dot_nt(qi, kj) * scale
                    valid = valid_ref[jnp.minimum(r * i - j + (r - 1), r)]
                    a, m, lsig = _sb_block(z, valid, t_ex, run)
                    sig = jnp.exp(lsig)
                    e = _dot_nt(doi, vs[_blk(j), :]) * a
                    e_before = e_tot - (_split_dot(e, t_in) + e_run)
                    dz = (e * (1.0 - sig) - sig * e_before) * valid
                    dzb = (dz * scale).astype(BF16)
                    dk_acc[_blk(j), :] += _dot_tn(dzb, qi)
                    dv_acc[_blk(j), :] += _dot_tn(a.astype(BF16), doi)
                    return (run + jnp.sum(m, axis=1, keepdims=True), e_run + jnp.sum(e, axis=1, keepdims=True),
                            dq + jnp.dot(dzb, kj, preferred_element_type=F32))

                _, _, dq = lax.fori_loop(0, nkb, kblock, (zero, zero, jnp.zeros((bq, HEAD_DIM), F32)))
                dq_acc[_qblk(i, bq), :] = dq
                return carry

            lax.fori_loop(0, nq, qblock, 0)

        if even:
            @pl.when(h < n_sb)
            def _():
                sb_head()

            @pl.when(h >= n_sb)
            def _():
                softmax_head(h - n_sb)

            dh_ref[0] = dq_acc[...].astype(BF16)
            dh_ref[1] = dk_acc[...].astype(BF16)
        else:
            softmax_head(h)
            dh_ref[0] = _rope_t(dq_acc[...], c_ref[...], s_ref[...]).astype(BF16)
            dh_ref[1] = _rope_t(dk_acc[...], c_ref[...], s_ref[...]).astype(BF16)
        dh_ref[2] = dv_acc[...].astype(BF16)

    head = lambda off: pl.BlockSpec((S, HEAD_DIM), lambda h, off=off: (0, off + h))
    full = pl.BlockSpec((S, LANES), lambda h: (0, 0))
    tfull = pl.BlockSpec((n_heads - n_sb, 1, S), lambda h: (0, 0, 0))
    tab_specs = [pl.BlockSpec(t.shape, lambda h: (0, 0, 0)) for t in tabs]
    dh_spec = pl.BlockSpec((3, S, HEAD_DIM), lambda h: (0, 0, h))
    dh_shape = jax.ShapeDtypeStruct((3, S, D), BF16)
    if even:
        extra_specs, extra = [full, tfull] + tab_specs, (cf, cft) + tabs
        out_specs = [dh_spec, tfull, full]
        out_shape = [dh_shape, jax.ShapeDtypeStruct((n_heads - n_sb, 1, S), F32),
                     jax.ShapeDtypeStruct((S, LANES), F32)]
    else:
        extra_specs, extra = [full, full] + tab_specs, (rope_c, rope_s) + tabs
        out_specs = [dh_spec]
        out_shape = [dh_shape]
    blk_bytes = 10 * _nbytes((S, HEAD_DIM), F32) + sum(_nbytes(t.shape, F32) for t in tabs)
    scratch_bytes = 4 * _nbytes((S, HEAD_DIM), BF16) + 3 * _nbytes((S, HEAD_DIM), F32)
    return pl.pallas_call(
        body, name=name, grid=(n_heads,),
        in_specs=[head(0), head(n_heads), head(2 * n_heads), head(0), head(0), head(0)] + extra_specs,
        out_specs=out_specs, out_shape=out_shape,
        scratch_shapes=[pltpu.VMEM((S, HEAD_DIM), BF16)] * 4 + [pltpu.VMEM((S, HEAD_DIM), F32)] * 3,
        compiler_params=pltpu.CompilerParams(dimension_semantics=("arbitrary",),
                                             vmem_limit_bytes=_vmem_limit(blk_bytes, scratch_bytes)),
    )(hq, hq, hq, do, o, lse, *extra)


def _ffn_fwd(tag, x, g, wgu, wd, fs):
    n = _rms_fwd(tag + "_norm", x, g)
    gu, h = _ffn_up(tag + "_gu", n, wgu, fs)
    if callable(wd):
        wd = wd(h)
    y = _mm(tag + "_down", h, wd, "nn", F32, res=x, alpha=0.5)
    return y, (x, g, n, gu, h), wd


def _ffn_bwd(tag, dx, dxb, wgu, wd, fs, saved, after=None, emit=None):
    x, g, n, gu, h = saved
    dgu = _ffn_dact(tag + "_dgu", dxb, wd, gu, fs, 0.5, after=after)
    dwd = _mm(tag + "_dwd", h, dxb, "tn", BF16, alpha=0.5)
    dwgu = _mm(tag + "_dwgu", n, dgu, "tn", BF16)
    token = emit(dwgu, dwd) if emit else None
    dn = _mm(tag + "_dn", dgu, wgu, "nt", F32, after=token)
    dx_in, dxb_in, dg = _rms_bwd(tag + "_dnorm", dn, x, g, dx)
    return dx_in, dxb_in, dg, dwgu, dwd, token


def _mixer_fwd(tag, kind, x, g, wqkv, wout, n_heads, n_sb, wf=None, bf=None, rope=None):
    n = _rms_fwd(tag + "_norm", x, g)
    hq = _mm(tag + "_qkv", n, wqkv, "nn", F32)
    if kind == "even":
        hf = _mm(tag + "_gate", n, wf, "nn", F32)
        cf, cft = _gate_fwd(tag + "_cumgate", hf, bf)
        cft = cft[:n_heads - n_sb].reshape(n_heads - n_sb, 1, -1)
        o, ob, lse = _attn_fwd_wide(tag + "_attn", hq, kind, n_heads, n_sb, cf=cf, cft=cft)
    else:
        hf = cf = cft = None
        o, ob, lse = _attn_fwd_wide(tag + "_attn", hq, kind, n_heads, n_sb, rope_c=rope[0], rope_s=rope[1])
    y = _mm(tag + "_out", ob, wout, "nn", F32, res=x)
    return y, (x, g, n, hq, hf, cf, cft, o, ob, lse)


def _mixer_bwd(tag, kind, dx, dxb, wqkv, wout, n_heads, n_sb, saved, wf=None, bf=None, rope=None, after=None,
               emit=None):
    x, g, n, hq, hf, cf, cft, o, ob, lse = saved
    do = _mm(tag + "_do", dxb, wout, "nt", F32, after=after)
    dwout = _mm(tag + "_dwout", ob, dxb, "tn", BF16)
    if kind == "even":
        dh3, dcft, drow = _attn_bwd_wide(tag + "_dattn", hq, do, o, lse, kind, n_heads, n_sb, cf=cf, cft=cft)
    else:
        (dh3,) = _attn_bwd_wide(tag + "_dattn", hq, do, o, lse, kind, n_heads, n_sb, rope_c=rope[0], rope_s=rope[1])
    dwqkv = _mm(tag + "_dwqkv", n, dh3, "tn", BF16)
    dwf = db = dhf = None
    if kind == "even":
        n_fox = n_heads - n_sb
        dcft = jnp.pad(dcft.reshape(n_fox, -1), ((0, LANES - n_fox), (0, 0)))
        dhf, db = _gate_bwd(tag + "_dcumgate", dcft, drow, hf, bf)
        dwf = _mm(tag + "_dwf", n, dhf, "tn", BF16)
    token = emit(dwqkv, dwout, dwf) if emit else None
    dn = _mm(tag + "_dn", dh3, wqkv, "nt", F32, after=token)
    if kind == "even":
        dn = _mm(tag + "_dn_gate", dhf, wf, "nt", F32, res=dn)
    dx_in, dxb_in, dg = _rms_bwd(tag + "_dnorm", dn, x, g, dx)
    return dx_in, dxb_in, dg, dwqkv, dwout, dwf, db, token


def _local_step(x, target, w, fs, n_heads, n_sb, emit=None):
    S, D = x.shape
    rope = _rope_tables(S)
    kinds = ("even", "odd")
    saved = []
    h = x
    if callable(w):
        fetch, w = w, {"norm_g": w("norm_g", None), "final_g": w("final_g", None),
                       "wgu1": [None, None], "wd1": [None, None], "wgu2": [None, None], "wd2": [None, None]}
    else:
        fetch = None
    for l, kind in enumerate(kinds):
        ng = [w["norm_g"][l, i][None, :] for i in range(3)]
        if fetch:
            w["wgu1"][l], w["wd1"][l] = fetch(("ffn1", l), h)
        h, s1, wd = _ffn_fwd(f"l{l}_ffn1", h, ng[0], w["wgu1"][l], w["wd1"][l], fs)
        if fetch:
            w["wd1"][l] = wd
        if fetch:
            w.update(fetch(("mix", l), h))
        if kind == "even":
            h, s2 = _mixer_fwd(f"l{l}_mix", kind, h, ng[1], w["wqkv_e"], w["wout_e"], n_heads, n_sb,
                               wf=w["wf"], bf=w["bf"])
        else:
            h, s2 = _mixer_fwd(f"l{l}_mix", kind, h, ng[1], w["wqkv_o"], w["wout_o"], n_heads, n_sb, rope=rope)
        if fetch:
            w["wgu2"][l], w["wd2"][l] = fetch(("ffn2", l), h)
        h, s3, _ = _ffn_fwd(f"l{l}_ffn2", h, ng[2], w["wgu2"][l], w["wd2"][l], fs)
        saved.append((s1, s2, s3))

    dx, dxb, dfinal, loss = _loss_head("loss_head", h, w["final_g"], target)
    grads = {"dfinal": dfinal, "dnorm": [[None] * 3 for _ in kinds],
             "dwgu1": [None, None], "dwd1": [None, None], "dwgu2": [None, None], "dwd2": [None, None]}
    hand = lambda block: (lambda *mats: emit(block, mats)) if emit else None
    token = None
    for l in (1, 0):
        kind = kinds[l]
        s1, s2, s3 = saved[l]
        dx, dxb, dg, grads["dwgu2"][l], grads["dwd2"][l], token = _ffn_bwd(
            f"l{l}_ffn2", dx, dxb, w["wgu2"][l], w["wd2"][l], fs, s3, after=token, emit=hand(("ffn2", l)))
        grads["dnorm"][l][2] = dg
        if kind == "even":
            dx, dxb, dg, grads["dwqkv_e"], grads["dwout_e"], grads["dwf"], grads["db"], token = _mixer_bwd(
                f"l{l}_mix", kind, dx, dxb, w["wqkv_e"], w["wout_e"], n_heads, n_sb, s2, wf=w["wf"], bf=w["bf"],
                after=token, emit=hand(("mix", l)))
        else:
            dx, dxb, dg, grads["dwqkv_o"], grads["dwout_o"], _, _, token = _mixer_bwd(
                f"l{l}_mix", kind, dx, dxb, w["wqkv_o"], w["wout_o"], n_heads, n_sb, s2, rope=rope,
                after=token, emit=hand(("mix", l)))
        grads["dnorm"][l][1] = dg
        dx, dxb, dg, grads["dwgu1"][l], grads["dwd1"][l], token = _ffn_bwd(
            f"l{l}_ffn1", dx, dxb, w["wgu1"][l], w["wd1"][l], fs, s1, after=token, emit=hand(("ffn1", l)))
        grads["dnorm"][l][0] = dg
    return loss, dx, grads


def _cast_into(name, shard, layer, chip, full_shape, place, full=None, after=None):
    R, C = shard.shape[-2:]
    tr = _row_tile(R, 512, step=16)
    if layer is None:
        in_spec = pl.BlockSpec((tr, C), lambda i, k: (i, 0))
    else:
        in_spec = pl.BlockSpec((None, tr, C), lambda i, k: (layer, i, 0))
    lead = (None,) * (len(full_shape) - 2)
    out_spec = pl.BlockSpec(lead + (tr, C), lambda i, k: place(i, k[0]))

    def body(*refs):
        k_ref, w_ref = refs[:2]
        o_ref = refs[-1]
        o_ref[...] = w_ref[...].astype(BF16)

    in_specs = [in_spec] + ([_ANY] if full is not None else []) + ([_ANY] if after is not None else [])
    args = (chip, shard) + ((full,) if full is not None else ()) + ((after,) if after is not None else ())
    grid_spec = pltpu.PrefetchScalarGridSpec(num_scalar_prefetch=1, grid=(R // tr,), in_specs=in_specs, out_specs=out_spec)
    return pl.pallas_call(
        body, name=name, grid_spec=grid_spec, out_shape=jax.ShapeDtypeStruct(full_shape, BF16),
        input_output_aliases={2: 0} if full is not None else {},
        compiler_params=pltpu.CompilerParams(dimension_semantics=("arbitrary",)),
    )(*args)


def _region_shape(grad, kind):
    if kind == "lead":
        return grad.shape[1] // N_CORES, grad.shape[2]
    rows, cols = grad.shape
    if kind == "cols":
        return rows // N_CORES, cols // N_CHIPS
    return rows // (N_CHIPS * N_CORES), cols


def _region_add(name, grad, kind, landed, core):
    rh, cw = _region_shape(grad, kind)
    tr = _row_tile(rh, 256, step=16)
    nrb = rh // tr
    if kind == "cols":
        g_spec = pl.BlockSpec((tr, cw), lambda k, r, c: (c[0] * nrb + r, k))
    elif kind == "rows":
        g_spec = pl.BlockSpec((tr, cw), lambda k, r, c: ((N_CORES * k + c[0]) * nrb + r, 0))
    else:
        g_spec = pl.BlockSpec((None, tr, cw), lambda k, r, c: (k, c[0] * nrb + r, 0))
    l_spec = pl.BlockSpec((None, tr, cw), lambda k, r, c: (k, r, 0))

    def body(c_ref, g_ref, l_ref, o_ref):
        o_ref[...] = (g_ref[...].astype(F32) + l_ref[...].astype(F32)).astype(BF16)

    grid_spec = pltpu.PrefetchScalarGridSpec(
        num_scalar_prefetch=1, grid=(N_CHIPS, nrb), in_specs=[g_spec, l_spec], out_specs=l_spec)
    return pl.pallas_call(
        body, name=name, grid_spec=grid_spec, out_shape=jax.ShapeDtypeStruct(landed.shape, BF16),
        compiler_params=pltpu.CompilerParams(dimension_semantics=("parallel", "parallel"),
                                             vmem_limit_bytes=_vmem_limit(3 * _nbytes((tr, cw), F32))),
    )(core, grad, landed)


def _chip_sum(name, pair, landed, pos):
    _, rh, cw = pair.shape
    tr = _row_tile(rh, max(16, 2**20 // (cw * 4)), step=16)
    nrb = rh // tr

    def body(p_ref, own_ref, l_ref, o_ref):
        acc = own_ref[...].astype(F32)
        for s in range(N_CHIPS - 1):
            acc = acc + l_ref[s].astype(F32)
        o_ref[...] = acc

    grid_spec = pltpu.PrefetchScalarGridSpec(
        num_scalar_prefetch=1, grid=(nrb,),
        in_specs=[pl.BlockSpec((None, tr, cw), lambda r, p: (p[0], r, 0)),
                  pl.BlockSpec((N_CHIPS - 1, tr, cw), lambda r, p: (0, r, 0))],
        out_specs=pl.BlockSpec((tr, cw), lambda r, p: (p[1] * nrb + r, 0)))
    return pl.pallas_call(
        body, name=name, grid_spec=grid_spec, out_shape=jax.ShapeDtypeStruct((N_CORES * rh, cw), F32),
        compiler_params=pltpu.CompilerParams(dimension_semantics=("arbitrary",)),
    )(pos, pair, landed)


def _sum_leading(name, parts):
    n, R, C = parts.shape
    tr = _row_tile(R, max(8, (2**20 // (C * 4)) // 8 * 8))

    def body(p_ref, o_ref):
        acc = p_ref[0]
        for s in range(1, n):
            acc = acc + p_ref[s]
        o_ref[...] = acc

    return pl.pallas_call(
        body, name=name, grid=(R // tr,),
        in_specs=[pl.BlockSpec((n, tr, C), lambda i: (0, i, 0))],
        out_specs=pl.BlockSpec((tr, C), lambda i: (i, 0)),
        out_shape=jax.ShapeDtypeStruct((R, C), F32),
        compiler_params=pltpu.CompilerParams(dimension_semantics=("parallel",)),
    )(parts)


def _adamw(name, w, g, m, v):
    shape = w.shape
    to2d = lambda t: t.reshape(-1, shape[-1]) if t.ndim > 1 else t.reshape(1, -1)
    w2, g2, m2, v2 = (to2d(t) for t in (w, g, m, v))
    R, C = w2.shape
    tr = _row_tile(R, 256)

    def body(w_ref, g_ref, m_ref, v_ref, d_ref, nm_ref, nv_ref):
        gv = g_ref[...]
        nm = ADAM_B1 * m_ref[...] + (1.0 - ADAM_B1) * gv
        nv = ADAM_B2 * v_ref[...] + (1.0 - ADAM_B2) * (gv * gv)
        m_hat = nm / (1.0 - ADAM_B1 ** ADAM_STEP)
        v_hat = nv / (1.0 - ADAM_B2 ** ADAM_STEP)
        d_ref[...] = -ADAM_LR * (m_hat / (jnp.sqrt(v_hat) + ADAM_EPS) + ADAM_WD * w_ref[...])
        nm_ref[...] = nm
        nv_ref[...] = nv

    spec = pl.BlockSpec((tr, C), lambda i: (i, 0))
    sds = jax.ShapeDtypeStruct((R, C), F32)
    d, nm, nv = pl.pallas_call(
        body, name=name, grid=(R // tr,), in_specs=[spec] * 4, out_specs=[spec] * 3, out_shape=[sds] * 3,
        compiler_params=pltpu.CompilerParams(dimension_semantics=("parallel",),
                                             vmem_limit_bytes=_vmem_limit(7 * _nbytes((tr, C), F32))),
    )(w2, g2, m2, v2)
    return d.reshape(shape), nm.reshape(shape), nv.reshape(shape)


_ANY = pl.BlockSpec(memory_space=pl.ANY)


def _mesh_pos():
    return lax.axis_index("x"), lax.axis_index("y"), lax.axis_index("c")


def _other_chips(x, y):
    return [(1 - x, y), (x, 1 - y), (1 - x, 1 - y)]


def _gather_over_chips(name, fulls, views):
    n = len(views)
    nf = len(fulls)

    def body(*refs):
        full = refs[nf:2 * nf]
        ici_send, ici_recv, d2d_send, d2d_recv = refs[2 * nf:]
        x, y, c = _mesh_pos()
        chips = _other_chips(x, y)
        mine = 2 * x + y
        sibling = (x, y, 1 - c)

        def ici(a, p, k):
            i, view, _ = views[a]
            part = view(full[i], k, c)
            return pltpu.make_async_remote_copy(
                src_ref=part, dst_ref=part, send_sem=ici_send.at[a, p], recv_sem=ici_recv.at[a, p],
                device_id=(*chips[p], c), device_id_type=MESH)

        def d2d(a, p, h):
            i, view, _ = views[a]
            px, py = chips[p]
            part = view(full[i], 2 * px + py, h)
            return pltpu.make_async_remote_copy(
                src_ref=part, dst_ref=part, send_sem=d2d_send.at[a, p], recv_sem=d2d_recv.at[a, p],
                device_id=sibling, device_id_type=MESH)

        sends = [ici(a, p, mine) for a in range(n) for p in range(3)]
        for cp in sends:
            cp.start()
        passed = []
        for a in range(n):
            for p, (px, py) in enumerate(chips):
                ici(a, p, 2 * px + py).wait_recv()
                if views[a][2]:
                    fwd = d2d(a, p, c)
                    fwd.start()
                    passed.append(fwd)
        for a in range(n):
            if views[a][2]:
                for p in range(3):
                    d2d(a, p, 1 - c).wait_recv()
        for cp in sends + passed:
            cp.wait_send()

    return pl.pallas_call(
        body, name=name, in_specs=[_ANY] * nf, out_specs=[_ANY] * nf,
        out_shape=[jax.ShapeDtypeStruct(f.shape, f.dtype) for f in fulls],
        input_output_aliases={i: i for i in range(nf)},
        scratch_shapes=[pltpu.SemaphoreType.DMA((n, 3))] * 4,
        compiler_params=pltpu.CompilerParams(has_side_effects=True),
    )(*fulls)


_HBM = pl.BlockSpec(memory_space=pltpu.HBM)
_SEM = pl.BlockSpec(memory_space=pltpu.SEMAPHORE)


def _in_hbm(arrays):
    return [pltpu.with_memory_space_constraint(a, pltpu.HBM) for a in arrays]


def _gather_start(name, fulls, views, after):
    nf = len(fulls)
    ng = 1 + max(g for _, _, g in views)

    def body(*refs):
        full = refs[nf + 1:2 * nf + 1]
        send_sems, recv_sems = refs[2 * nf + 1:2 * nf + 1 + ng], refs[2 * nf + 1 + ng:]
        x, y, c = _mesh_pos()
        chips = _other_chips(x, y)
        for i, view, g in views:
            part = view(full[i], 2 * x + y, c)
            for px, py in chips:
                pltpu.make_async_remote_copy(
                    src_ref=part, dst_ref=part, send_sem=send_sems[g], recv_sem=recv_sems[g],
                    device_id=(px, py, c), device_id_type=MESH).start()

    outs = pl.pallas_call(
        body, name=name, in_specs=[_HBM] * nf + [_ANY], out_specs=[_HBM] * nf + [_SEM] * (2 * ng),
        out_shape=[pltpu.HBM(f.shape, f.dtype) for f in fulls] + [pltpu.SemaphoreType.DMA(())] * (2 * ng),
        input_output_aliases={i: i for i in range(nf)},
        compiler_params=pltpu.CompilerParams(has_side_effects=pltpu.SideEffectType.DATAFLOW_SIDE_EFFECTING),
    )(*_in_hbm(fulls), after)
    return list(outs[:nf]), list(outs[nf:nf + ng]), list(outs[nf + ng:])


def _gather_wait(name, fulls, views, send_sem, recv_sem, after):
    nf = len(fulls)

    def body(*refs):
        send_ref, recv_ref = refs[nf], refs[nf + 1]
        full = refs[nf + 3:]
        x, y, c = _mesh_pos()
        copies = [pltpu.make_async_remote_copy(
            src_ref=view(full[i], 2 * x + y, c), dst_ref=view(full[i], 2 * px + py, c),
            send_sem=send_ref, recv_sem=recv_ref, device_id=(px, py, c), device_id_type=MESH)
            for i, view in views for px, py in _other_chips(x, y)]
        for cp in copies:
            cp.wait_send()
        for cp in copies:
            cp.wait_recv()

    outs = pl.pallas_call(
        body, name=name, in_specs=[_HBM] * nf + [_SEM, _SEM, _ANY], out_specs=[_HBM] * nf,
        out_shape=[pltpu.HBM(f.shape, f.dtype) for f in fulls],
        input_output_aliases={i: i for i in range(nf)},
        compiler_params=pltpu.CompilerParams(has_side_effects=pltpu.SideEffectType.DATAFLOW_SIDE_EFFECTING),
    )(*fulls, send_sem, recv_sem, after)
    return list(outs)


def _forward_to_sibling(name, fulls, views):
    n, nf = len(views), len(fulls)

    def body(*refs):
        full = refs[nf:2 * nf]
        send_sems, recv_sems = refs[2 * nf:]
        x, y, c = _mesh_pos()
        chips = _other_chips(x, y)

        def copy(a, p, h):
            i, view = views[a]
            px, py = chips[p]
            part = view(full[i], 2 * px + py, h)
            return pltpu.make_async_remote_copy(
                src_ref=part, dst_ref=part, send_sem=send_sems.at[a, p], recv_sem=recv_sems.at[a, p],
                device_id=(x, y, 1 - c), device_id_type=MESH)

        sends = [copy(a, p, c) for a in range(n) for p in range(3)]
        for cp in sends:
            cp.start()
        for a in range(n):
            for p in range(3):
                copy(a, p, 1 - c).wait_recv()
        for cp in sends:
            cp.wait_send()

    return pl.pallas_call(
        body, name=name, in_specs=[_ANY] * nf, out_specs=[_ANY] * nf,
        out_shape=[jax.ShapeDtypeStruct(f.shape, f.dtype) for f in fulls],
        input_output_aliases={i: i for i in range(nf)},
        scratch_shapes=[pltpu.SemaphoreType.DMA((n, 3))] * 2,
        compiler_params=pltpu.CompilerParams(has_side_effects=True),
    )(*fulls)


def _region_view(ref, kind, k, c):
    if kind == "lead":
        rh = ref.shape[1] // N_CORES
        return ref.at[k, pl.ds(pl.multiple_of(c * rh, 8), rh), :]
    rows, cols = ref.shape
    if kind == "cols":
        rh, cw = rows // N_CORES, cols // N_CHIPS
        return ref.at[pl.ds(pl.multiple_of(c * rh, 8), rh), pl.ds(k * cw, cw)]
    rh = rows // (N_CHIPS * N_CORES)
    return ref.at[pl.ds(pl.multiple_of((N_CORES * k + c) * rh, 8), rh), :]


def _send_to_sibling(name, grads, kinds):
    n = len(grads)
    shapes = [jax.ShapeDtypeStruct((N_CHIPS,) + _region_shape(g, kd), g.dtype) for g, kd in zip(grads, kinds)]

    def body(*refs):
        g_ref, land = refs[:n], refs[n:2 * n]
        send_sems, recv_sems = refs[2 * n:]
        x, y, c = _mesh_pos()
        copies = []
        for a in range(n):
            for k in range(N_CHIPS):
                cp = pltpu.make_async_remote_copy(
                    src_ref=_region_view(g_ref[a], kinds[a], k, 1 - c), dst_ref=land[a].at[k],
                    send_sem=send_sems.at[a, k], recv_sem=recv_sems.at[a, k],
                    device_id=(x, y, 1 - c), device_id_type=MESH)
                cp.start()
                copies.append(cp)
        for cp in copies:
            cp.wait_recv()
        for cp in copies:
            cp.wait_send()

    return pl.pallas_call(
        body, name=name, in_specs=[_ANY] * n, out_specs=[_ANY] * n, out_shape=shapes,
        scratch_shapes=[pltpu.SemaphoreType.DMA((n, N_CHIPS)), pltpu.SemaphoreType.DMA((n, N_CHIPS))],
        compiler_params=pltpu.CompilerParams(has_side_effects=True),
    )(*grads)


def _scatter_start(name, pair_sums):
    n = len(pair_sums)
    lands = [lax.empty((N_CHIPS - 1,) + p.shape[1:], p.dtype) for p in pair_sums]

    def body(*refs):
        p_ref, land = refs[2 * n:3 * n], refs[3 * n:4 * n]
        send_sem, recv_sem, token = refs[4 * n:]
        x, y, c = _mesh_pos()
        for a in range(n):
            for p, (px, py) in enumerate(_other_chips(x, y)):
                pltpu.make_async_remote_copy(
                    src_ref=p_ref[a].at[2 * px + py], dst_ref=land[a].at[p], send_sem=send_sem, recv_sem=recv_sem,
                    device_id=(px, py, c), device_id_type=MESH).start()
        token[...] = jnp.zeros_like(token)

    outs = pl.pallas_call(
        body, name=name, in_specs=[_HBM] * (2 * n),
        out_specs=[_HBM] * (2 * n) + [_SEM, _SEM, pl.BlockSpec(memory_space=pltpu.VMEM)],
        out_shape=[pltpu.HBM(t.shape, t.dtype) for t in list(pair_sums) + lands]
        + [pltpu.SemaphoreType.DMA(()), pltpu.SemaphoreType.DMA(()), jax.ShapeDtypeStruct((8, LANES), F32)],
        input_output_aliases={i: i for i in range(2 * n)},
        compiler_params=pltpu.CompilerParams(has_side_effects=pltpu.SideEffectType.DATAFLOW_SIDE_EFFECTING),
    )(*_in_hbm(list(pair_sums) + lands))
    return list(outs[:n]), list(outs[n:2 * n]), outs[2 * n], outs[2 * n + 1], outs[2 * n + 2]


def _scatter_wait(name, pair_sums, lands, send_sem, recv_sem, after):
    n = len(pair_sums)

    def body(*refs):
        send_ref, recv_ref = refs[2 * n], refs[2 * n + 1]
        p_ref, land = refs[2 * n + 3:3 * n + 3], refs[3 * n + 3:]
        x, y, c = _mesh_pos()
        copies = [pltpu.make_async_remote_copy(
            src_ref=p_ref[a].at[2 * px + py], dst_ref=land[a].at[p], send_sem=send_ref, recv_sem=recv_ref,
            device_id=(px, py, c), device_id_type=MESH)
            for a in range(n) for p, (px, py) in enumerate(_other_chips(x, y))]
        for cp in copies:
            cp.wait_send()
        for cp in copies:
            cp.wait_recv()

    outs = pl.pallas_call(
        body, name=name, in_specs=[_HBM] * (2 * n) + [_SEM, _SEM, _ANY], out_specs=[_HBM] * (2 * n),
        out_shape=[pltpu.HBM(t.shape, t.dtype) for t in list(pair_sums) + list(lands)],
        input_output_aliases={i: i for i in range(2 * n)},
        compiler_params=pltpu.CompilerParams(has_side_effects=pltpu.SideEffectType.DATAFLOW_SIDE_EFFECTING),
    )(*pair_sums, *lands, send_sem, recv_sem, after)
    return list(outs[:n]), list(outs[n:])


def _swap_halves(name, shards):
    n = len(shards)

    def body(*refs):
        out = refs[n:2 * n]
        send_sems, recv_sems = refs[2 * n:]
        x, y, c = _mesh_pos()
        sends = []
        for a in range(n):
            rh = out[a].shape[0] // N_CORES
            mine = out[a].at[pl.ds(pl.multiple_of(c * rh, 8), rh), :]
            cp = pltpu.make_async_remote_copy(
                src_ref=mine, dst_ref=mine, send_sem=send_sems.at[a], recv_sem=recv_sems.at[a],
                device_id=(x, y, 1 - c), device_id_type=MESH)
            cp.start()
            sends.append(cp)
        for a in range(n):
            rh = out[a].shape[0] // N_CORES
            theirs = out[a].at[pl.ds(pl.multiple_of((1 - c) * rh, 8), rh), :]
            pltpu.make_async_remote_copy(
                src_ref=theirs, dst_ref=theirs, send_sem=send_sems.at[a], recv_sem=recv_sems.at[a],
                device_id=(x, y, 1 - c), device_id_type=MESH).wait_recv()
        for cp in sends:
            cp.wait_send()

    return pl.pallas_call(
        body, name=name, in_specs=[_ANY] * n, out_specs=[_ANY] * n,
        out_shape=[jax.ShapeDtypeStruct(s.shape, s.dtype) for s in shards],
        input_output_aliases={i: i for i in range(n)},
        scratch_shapes=[pltpu.SemaphoreType.DMA((n,)), pltpu.SemaphoreType.DMA((n,))],
        compiler_params=pltpu.CompilerParams(has_side_effects=True),
    )(*shards)


def _gather_all_devices(name, block):
    R, C = block.shape
    ndev = N_CHIPS * N_CORES

    def body(b_ref, out_ref, send_sems, recv_sems, local_sem):
        x, y, c = _mesh_pos()
        mine = 4 * x + 2 * y + c
        own = pltpu.make_async_copy(b_ref, out_ref.at[mine], local_sem)
        own.start()
        sends = []
        for mask in range(1, ndev):
            fx, fy, fc = (mask >> 2) & 1, (mask >> 1) & 1, mask & 1
            px, py, pc = x ^ fx, y ^ fy, c ^ fc
            cp = pltpu.make_async_remote_copy(
                src_ref=b_ref, dst_ref=out_ref.at[mine], send_sem=send_sems.at[mask - 1],
                recv_sem=recv_sems.at[mask - 1], device_id=(px, py, pc), device_id_type=MESH)
            cp.start()
            sends.append(cp)
        for mask in range(1, ndev):
            fx, fy, fc = (mask >> 2) & 1, (mask >> 1) & 1, mask & 1
            px, py, pc = x ^ fx, y ^ fy, c ^ fc
            pltpu.make_async_remote_copy(
                src_ref=b_ref, dst_ref=out_ref.at[4 * px + 2 * py + pc], send_sem=send_sems.at[mask - 1],
                recv_sem=recv_sems.at[mask - 1], device_id=(px, py, pc), device_id_type=MESH).wait_recv()
        for cp in sends:
            cp.wait_send()
        own.wait()

    return pl.pallas_call(
        body, name=name, in_specs=[_ANY], out_specs=_ANY,
        out_shape=jax.ShapeDtypeStruct((ndev, R, C), F32),
        scratch_shapes=[pltpu.SemaphoreType.DMA((ndev - 1,)), pltpu.SemaphoreType.DMA((ndev - 1,)),
                        pltpu.SemaphoreType.DMA(())],
        compiler_params=pltpu.CompilerParams(has_side_effects=True),
    )(block)


def kernel(x, norm_g, ffn1_w_gate, ffn1_w_up, ffn1_w_down, ffn2_w_gate, ffn2_w_up, ffn2_w_down, even_w_in, even_b_forget, even_w_out, odd_w_qkv, odd_w_out, final_norm_g, loss_target, m_norm_g, m_ffn1_w_gate, m_ffn1_w_up, m_ffn1_w_down, m_ffn2_w_gate, m_ffn2_w_up, m_ffn2_w_down, m_even_w_in, m_even_b_forget, m_even_w_out, m_odd_w_qkv, m_odd_w_out, m_final_norm_g, v_norm_g, v_ffn1_w_gate, v_ffn1_w_up, v_ffn1_w_down, v_ffn2_w_gate, v_ffn2_w_up, v_ffn2_w_down, v_even_w_in, v_even_b_forget, v_even_w_out, v_odd_w_qkv, v_odd_w_out, v_final_norm_g):
    _, S, D = x.shape
    L = norm_g.shape[0]
    assert L == 2 and even_w_in.shape[0] == 1 and odd_w_qkv.shape[0] == 1
    fs = ffn1_w_gate.shape[2]
    F = N_CHIPS * fs
    wc = even_w_in.shape[2]
    n_heads = D // HEAD_DIM
    n_fox = N_CHIPS * wc - 3 * D
    n_sb = n_heads - n_fox
    qs = odd_w_qkv.shape[2]
    os_ = even_w_out.shape[1]
    ns = norm_g.shape[2]
    xi, yi, ci = _mesh_pos()
    chip = 2 * xi + yi

    pos = jnp.stack([chip, ci]).astype(jnp.int32)
    kchip = pos[:1]
    lane = lambda start, size: pl.ds(pl.multiple_of(start, LANES), size)
    sub = lambda start, size: pl.ds(pl.multiple_of(start, 16), size)
    gate_view = lambda r, k, h: r.at[sub(h * (D // 2), D // 2), lane(k * 2 * fs, fs)]
    up_view = lambda r, k, h: r.at[sub(h * (D // 2), D // 2), lane(k * 2 * fs + fs, fs)]
    down_view = lambda r, k, h: r.at[sub(k * fs + h * (fs // 2), fs // 2), :]
    out_view = lambda r, k, h: r.at[sub(k * os_ + h * (os_ // 2), os_ // 2), :]
    tr_d = _row_tile(fs, 512, step=16)
    tr_o = _row_tile(os_, 512, step=16)
    ffn_w = {"ffn1": (ffn1_w_gate, ffn1_w_up, ffn1_w_down), "ffn2": (ffn2_w_gate, ffn2_w_up, ffn2_w_down)}
    win_view = lambda r, k, h: r.at[k, sub(h * (D // 2), D // 2), :]
    qkv_view = lambda r, k, h: r.at[sub(h * (D // 2), D // 2), lane(k * qs, qs)]
    norm_own = lax.dynamic_update_slice(jnp.zeros((L, 3, N_CHIPS * ns), F32), norm_g, (0, 0, chip * ns))
    (norm_full,) = _gather_over_chips("gather_norm", [norm_own], [(0, lambda r, k, h: r.at[:, :, lane(k * ns, ns)], False)])
    first = None
    fulls, views, groups = [], [], {}
    for l in range(L):
        for blk in ("ffn1", "mix", "ffn2"):
            tok = first[0][0] if first else None
            o, v0 = len(fulls), len(views)
            if blk == "mix" and l == 0:
                fulls += [_cast_into("cast_win", even_w_in, 0, kchip, (N_CHIPS, D, wc), lambda i, k: (k, i, 0), after=tok),
                          _cast_into("cast_wout_e", even_w_out, 0, kchip, (D, D), lambda i, k: (k * (os_ // tr_o) + i, 0),
                                     after=tok)]
                views += [(o, win_view), (o + 1, out_view)]
            elif blk == "mix":
                fulls += [_cast_into("cast_wqkv_o", odd_w_qkv, 0, kchip, (D, N_CHIPS * qs), lambda i, k: (i, k), after=tok),
                          _cast_into("cast_wout_o", odd_w_out, 0, kchip, (D, D), lambda i, k: (k * (os_ // tr_o) + i, 0),
                                     after=tok)]
                views += [(o, qkv_view), (o + 1, out_view)]
            else:
                wg, wu, wd = ffn_w[blk]
                t = f"cast_{blk}_l{l}"
                gu = _cast_into(t + "_gate", wg, l, kchip, (D, 2 * F), lambda i, k: (i, 2 * k), after=tok)
                gu = _cast_into(t + "_up", wu, l, kchip, (D, 2 * F), lambda i, k: (i, 2 * k + 1), full=gu)
                if first is None:
                    first = _gather_start("gather_start_first", [gu], [(0, gate_view, 0), (0, up_view, 0)], norm_full)
                    tok = first[0][0]
                    dn = _cast_into(t + "_down", wd, l, kchip, (F, D), lambda i, k: (k * (fs // tr_d) + i, 0), after=tok)
                    fulls += [dn]
                    views += [(o, down_view)]
                else:
                    dn = _cast_into(t + "_down", wd, l, kchip, (F, D), lambda i, k: (k * (fs // tr_d) + i, 0))
                    fulls += [gu, dn]
                    views += [(o, gate_view), (o, up_view), (o + 1, down_view)]
            gid = len(groups)
            views[v0:] = [(i, view, gid) for i, view in views[v0:]]
            groups[(blk, l)] = (gid, list(range(o, len(fulls))), list(range(v0, len(views))))
    started, send_sems, recv_sems = _gather_start("gather_start", fulls, views, first[0][0])

    def arrive(tag, arrays, local, ssem, rsem, after):
        got = _gather_wait("gather_wait_" + tag, arrays, local, ssem, rsem, after)
        return _forward_to_sibling("gather_pass_" + tag, got, local)

    def fetch(block, after):
        if block == "norm_g":
            return norm_full
        if block == "final_g":
            return final_norm_g[None, :]
        gid, arrays, rows = groups[block]
        tag = f"{block[0]}_l{block[1]}"
        local = [(views[a][0] - arrays[0], views[a][1]) for a in rows]
        rest = lambda a: arrive(tag, [started[i] for i in arrays], local, send_sems[gid], recv_sems[gid], a)
        if block == ("ffn1", 0):
            (wgu,) = arrive(tag + "_gu", first[0], [(0, gate_view), (0, up_view)], first[1][0], first[2][0], after)
            return wgu, lambda a: rest(a)[0]
        got = rest(after)
        if block[0] != "mix":
            return got
        if block[1] == 1:
            return {"wqkv_o": got[0], "wout_o": got[1]}
        win = jnp.concatenate([got[0][k] for k in range(N_CHIPS)], axis=1)
        return {"wqkv_e": win[:, :3 * D], "wf": jnp.pad(win[:, 3 * D:], ((0, 0), (0, LANES - n_fox))),
                "bf": jnp.pad(even_b_forget, ((0, 0), (0, LANES - n_fox))), "wout_e": got[1]}

    pending, shards = [], {}

    def finish(after):
        block, pair, lands, ssem, rsem = pending.pop(0)
        tag = f"{block[0]}_l{block[1]}"
        pair, lands = _scatter_wait("rs_chip_wait_" + tag, pair, lands, ssem, rsem, after)
        shards[block] = [_chip_sum(f"rs_chip_add_{tag}_{a}", p, ld, pos) for a, (p, ld) in enumerate(zip(pair, lands))]

    def emit(block, mats):
        blk, l = block
        tag = f"{blk}_l{l}"
        kinds = ["cols", "rows"]
        if blk == "mix" and l == 0:
            dwqkv, dwout, dwf = mats
            dwin = jnp.concatenate([dwqkv, dwf[:, :n_fox]], axis=1)
            mats = [jnp.stack([dwin[:, k * wc:(k + 1) * wc] for k in range(N_CHIPS)]), dwout]
            kinds = ["lead", "rows"]
        elif blk == "mix":
            mats = list(mats[:2])
        else:
            mats = list(mats)
        landed = _send_to_sibling("rs_pair_send_" + tag, mats, kinds)
        pair = [_region_add(f"rs_pair_add_{tag}_{a}", m, kd, ld, pos[1:])
                for a, (m, kd, ld) in enumerate(zip(mats, kinds, landed))]
        pair, lands, ssem, rsem, token = _scatter_start("rs_chip_start_" + tag, pair)
        if pending:
            finish(token)
        pending.append((block, pair, lands, ssem, rsem))
        return token

    loss_vec, grad_x, g = _local_step(x[0], loss_target[0], fetch, fs, n_heads, n_sb, emit=emit)
    finish(grad_x)
    order = [(b, l) for b in ("ffn1", "ffn2", "mix") for l in range(L)]
    red = _swap_halves("rs_swap_halves", [s for b in order for s in shards[b]])
    red = {b: red[2 * i:2 * i + 2] for i, b in enumerate(order)}
    gu1, gd1 = [red[("ffn1", l)][0] for l in range(L)], [red[("ffn1", l)][1] for l in range(L)]
    gu2, gd2 = [red[("ffn2", l)][0] for l in range(L)], [red[("ffn2", l)][1] for l in range(L)]
    (g_win, g_wout_e), (g_qkv_o, g_wout_o) = red[("mix", 0)], red[("mix", 1)]

    small_rows = [g["dnorm"][l][i] for l in range(L) for i in range(3)] + [
        g["dfinal"], jnp.pad(g["db"], ((0, 0), (0, D - LANES))), jnp.pad(loss_vec, ((0, 0), (0, D - LANES)))]
    small = jnp.concatenate(small_rows + [jnp.zeros((16 - len(small_rows), D), F32)], axis=0)
    small_sum = _sum_leading("small_sum", _gather_all_devices("small_gather", small))
    loss = small_sum[3 * L + 2, 0]
    g_norm = lax.dynamic_slice_in_dim(small_sum[:3 * L].reshape(L, 3, D), chip * ns, ns, axis=2)
    g_final = small_sum[3 * L]
    g_bf = small_sum[3 * L + 1, :n_fox][None, :]

    grads = [
        g_norm,
        jnp.stack([t[:, :fs] for t in gu1]), jnp.stack([t[:, fs:] for t in gu1]), jnp.stack(gd1),
        jnp.stack([t[:, :fs] for t in gu2]), jnp.stack([t[:, fs:] for t in gu2]), jnp.stack(gd2),
        g_win[None], g_bf, g_wout_e[None], g_qkv_o[None], g_wout_o[None], g_final]
    weights = [norm_g, ffn1_w_gate, ffn1_w_up, ffn1_w_down, ffn2_w_gate, ffn2_w_up, ffn2_w_down,
               even_w_in, even_b_forget, even_w_out, odd_w_qkv, odd_w_out, final_norm_g]
    ms = [m_norm_g, m_ffn1_w_gate, m_ffn1_w_up, m_ffn1_w_down, m_ffn2_w_gate, m_ffn2_w_up, m_ffn2_w_down,
          m_even_w_in, m_even_b_forget, m_even_w_out, m_odd_w_qkv, m_odd_w_out, m_final_norm_g]
    vs = [v_norm_g, v_ffn1_w_gate, v_ffn1_w_up, v_ffn1_w_down, v_ffn2_w_gate, v_ffn2_w_up, v_ffn2_w_down,
          v_even_w_in, v_even_b_forget, v_even_w_out, v_odd_w_qkv, v_odd_w_out, v_final_norm_g]
    deltas, new_ms, new_vs = [], [], []
    for i, (wt, gt, mt, vt) in enumerate(zip(weights, grads, ms, vs)):
        d, nm, nv = _adamw(f"adamw_{i}", wt, gt, mt, vt)
        deltas.append(d)
        new_ms.append(nm)
        new_vs.append(nv)
    return (loss, grad_x[None], *grads, *deltas, *new_ms, *new_vs)
```

```python
import math

import jax
import jax.numpy as jnp
from jax import lax
from jax.experimental import pallas as pl
from jax.experimental.pallas import tpu as pltpu

F32 = jnp.float32
BF16 = jnp.bfloat16

HEAD_DIM = 128
ROPE_DIMS = 32
ROPE_THETA = 500000.0
DILATED_PATTERNS = ((128, 1), (512, 4), (2048, 16))
RMS_EPS = 1e-6
NEG_INF = -1e30
ADAM_LR = 0.001
ADAM_B1 = 0.9
ADAM_B2 = 0.999
ADAM_EPS = 1e-08
ADAM_WD = 0.01
ADAM_STEP = 10

N_CHIPS = 4
N_CORES = 2
LANES = 128
BLK = 256
VMEM_BYTES_V7X = 64 * 2**20
MESH = pl.DeviceIdType.MESH


def _vmem_limit(block_bytes, scratch_bytes=0):
    need = 2 * block_bytes + scratch_bytes + 12 * 2**20
    return int(min(need, VMEM_BYTES_V7X - 6 * 2**20))


def _nbytes(shape, dtype):
    return math.prod(shape) * jnp.dtype(dtype).itemsize


def _tile(dim, target):
    best = None
    for t in range(LANES, min(dim, target) + 1, LANES):
        if dim % t == 0:
            best = t
    assert best is not None, (dim, target)
    return best


def _row_tile(rows, target, step=8):
    if rows <= target:
        return rows
    best = None
    for t in range(step, target + 1, step):
        if rows % t == 0:
            best = t
    assert best is not None, (rows, target)
    return best


def _mm(name, a, b, mode, out_dtype, res=None, alpha=1.0, after=None, tm_target=1024, tn_target=1536, tk_target=2048):
    a3 = a.ndim == 3
    b3 = b.ndim == 3
    if mode == "nn":
        assert not a3 and not b3
        (M, K), (K2, N) = a.shape, b.shape
    elif mode == "nt":
        assert not b3
        if a3:
            P, M, Kp = a.shape
            K = P * Kp
        else:
            M, K = a.shape
        N, K2 = b.shape
    else:
        assert mode == "tn" and not a3
        K, M = a.shape
        if b3:
            P, K2, Np = b.shape
            N = P * Np
        else:
            K2, N = b.shape
    assert K == K2, (name, a.shape, b.shape)
    tm = _tile(M, tm_target)
    tn = _tile(Np if b3 else N, tn_target)
    tk = _tile(Kp if a3 else K, tk_target)
    nk = K // tk
    grid = (M // tm, N // tn, nk)

    if mode == "nn":
        a_spec = pl.BlockSpec((tm, tk), lambda i, j, k: (i, k))
        b_spec = pl.BlockSpec((tk, tn), lambda i, j, k: (k, j))
        dims = (((1,), (0,)), ((), ()))
    elif mode == "nt":
        if a3:
            nkp = Kp // tk
            a_spec = pl.BlockSpec((None, tm, tk), lambda i, j, k: (k // nkp, i, k % nkp))
        else:
            a_spec = pl.BlockSpec((tm, tk), lambda i, j, k: (i, k))
        b_spec = pl.BlockSpec((tn, tk), lambda i, j, k: (j, k))
        dims = (((1,), (1,)), ((), ()))
    else:
        a_spec = pl.BlockSpec((tk, tm), lambda i, j, k: (k, i))
        if b3:
            njp = Np // tn
            b_spec = pl.BlockSpec((None, tk, tn), lambda i, j, k: (j // njp, k, j % njp))
        else:
            b_spec = pl.BlockSpec((tk, tn), lambda i, j, k: (k, j))
        dims = (((0,), (0,)), ((), ()))
    o_spec = pl.BlockSpec((tm, tn), lambda i, j, k: (i, j))
    has_res = res is not None

    def finish(y, r_ref, o_ref):
        if alpha != 1.0:
            y = y * alpha
        if has_res:
            y = r_ref[...] + y
        o_ref[...] = y.astype(o_ref.dtype)

    n_in = 2 + has_res + (after is not None)

    def body(*refs):
        a_ref, b_ref = refs[:2]
        r_ref = refs[2] if has_res else None
        o_ref = refs[n_in]
        part = lax.dot_general(a_ref[...], b_ref[...], dims, preferred_element_type=F32)
        if nk == 1:
            finish(part, r_ref, o_ref)
            return
        acc_ref = refs[-1]
        k = pl.program_id(2)

        @pl.when(k == 0)
        def _():
            acc_ref[...] = part

        @pl.when(k > 0)
        def _():
            acc_ref[...] += part

        @pl.when(k == nk - 1)
        def _():
            finish(acc_ref[...], r_ref, o_ref)

    in_specs = [a_spec, b_spec] + ([o_spec] if has_res else []) + ([_ANY] if after is not None else [])
    args = (a, b) + ((res,) if has_res else ()) + ((after,) if after is not None else ())
    blk = (_nbytes((tm, tk), a.dtype) + _nbytes((tk, tn), b.dtype) + _nbytes((tm, tn), out_dtype)
           + (_nbytes((tm, tn), F32) if has_res else 0))
    return pl.pallas_call(
        body, name=name, grid=grid, in_specs=in_specs, out_specs=o_spec,
        out_shape=jax.ShapeDtypeStruct((M, N), out_dtype),
        scratch_shapes=[pltpu.VMEM((tm, tn), F32)] if nk > 1 else [],
        compiler_params=pltpu.CompilerParams(
            dimension_semantics=("parallel", "parallel", "arbitrary"),
            vmem_limit_bytes=_vmem_limit(blk, 2 * _nbytes((tm, tn), F32))),
    )(*args)


def _rms_fwd(name, x, g):
    S, D = x.shape
    tr = _row_tile(S, 256)

    def body(x_ref, g_ref, n_ref):
        xv = x_ref[...]
        r = lax.rsqrt(jnp.mean(xv * xv, axis=-1, keepdims=True) + RMS_EPS)
        n_ref[...] = (xv * r * g_ref[...]).astype(BF16)

    return pl.pallas_call(
        body, name=name, grid=(S // tr,),
        in_specs=[pl.BlockSpec((tr, D), lambda i: (i, 0)), pl.BlockSpec((1, D), lambda i: (0, 0))],
        out_specs=pl.BlockSpec((tr, D), lambda i: (i, 0)),
        out_shape=jax.ShapeDtypeStruct((S, D), BF16),
        compiler_params=pltpu.CompilerParams(dimension_semantics=("parallel",)),
    )(x, g)


def _rms_bwd(name, dn, x, g, dres):
    S, D = x.shape
    tr = _row_tile(S, 256)

    def body(dn_ref, x_ref, g_ref, dres_ref, dx_ref, dxb_ref, dg_ref):
        i = pl.program_id(0)
        xv = x_ref[...]
        dnv = dn_ref[...]
        r = lax.rsqrt(jnp.mean(xv * xv, axis=-1, keepdims=True) + RMS_EPS)
        u = dnv * g_ref[...]
        dot = jnp.mean(u * xv, axis=-1, keepdims=True)
        dx = dres_ref[...] + r * u - xv * (r * r * r * dot)
        dx_ref[...] = dx
        dxb_ref[...] = dx.astype(BF16)

        @pl.when(i == 0)
        def _():
            dg_ref[...] = jnp.zeros_like(dg_ref)

        dg_ref[...] += jnp.sum(dnv * xv * r, axis=0, keepdims=True)

    row = pl.BlockSpec((tr, D), lambda i: (i, 0))
    vec = pl.BlockSpec((1, D), lambda i: (0, 0))
    return pl.pallas_call(
        body, name=name, grid=(S // tr,),
        in_specs=[row, row, vec, row], out_specs=[row, row, vec],
        out_shape=[jax.ShapeDtypeStruct((S, D), F32), jax.ShapeDtypeStruct((S, D), BF16),
                   jax.ShapeDtypeStruct((1, D), F32)],
        compiler_params=pltpu.CompilerParams(dimension_semantics=("arbitrary",)),
    )(dn, x, g, dres)


def _loss_head(name, x, g, target):
    S, D = x.shape
    tr = _row_tile(S, 256)

    def body(x_ref, g_ref, t_ref, dx_ref, dxb_ref, dg_ref, loss_ref):
        i = pl.program_id(0)
        xv = x_ref[...]
        gv = g_ref[...]
        r = lax.rsqrt(jnp.mean(xv * xv, axis=-1, keepdims=True) + RMS_EPS)
        diff = xv * r * gv - t_ref[...]
        part = 0.5 * jnp.sum(jnp.mean(diff * diff, axis=-1, keepdims=True), axis=0, keepdims=True)
        dy = diff * (1.0 / D)
        u = dy * gv
        dot = jnp.mean(u * xv, axis=-1, keepdims=True)
        dx = r * u - xv * (r * r * r * dot)
        dx_ref[...] = dx
        dxb_ref[...] = dx.astype(BF16)

        @pl.when(i == 0)
        def _():
            dg_ref[...] = jnp.zeros_like(dg_ref)
            loss_ref[...] = jnp.zeros_like(loss_ref)

        dg_ref[...] += jnp.sum(dy * xv * r, axis=0, keepdims=True)
        loss_ref[...] += jnp.broadcast_to(part, loss_ref.shape)

    row = pl.BlockSpec((tr, D), lambda i: (i, 0))
    vec = pl.BlockSpec((1, D), lambda i: (0, 0))
    lvec = pl.BlockSpec((1, LANES), lambda i: (0, 0))
    return pl.pallas_call(
        body, name=name, grid=(S // tr,),
        in_specs=[row, vec, row], out_specs=[row, row, vec, lvec],
        out_shape=[jax.ShapeDtypeStruct((S, D), F32), jax.ShapeDtypeStruct((S, D), BF16),
                   jax.ShapeDtypeStruct((1, D), F32), jax.ShapeDtypeStruct((1, LANES), F32)],
        compiler_params=pltpu.CompilerParams(dimension_semantics=("arbitrary",)),
    )(x, g, target)


def _ffn_up(name, n, wgu, fs, tm_target=512):
    S, D = n.shape
    nslab = wgu.shape[1] // (2 * fs)
    tm = _tile(S, tm_target)

    def body(n_ref, w_ref, gu_ref, h_ref):
        y = jnp.dot(n_ref[...], w_ref[...], preferred_element_type=F32)
        gu_ref[...] = y
        gv = y[:, :fs]
        h_ref[...] = (gv * jax.nn.sigmoid(gv) * y[:, fs:]).astype(BF16)

    blk = _nbytes((tm, D), BF16) + _nbytes((D, 2 * fs), BF16) + _nbytes((tm, 2 * fs), F32) + _nbytes((tm, fs), BF16)
    return pl.pallas_call(
        body, name=name, grid=(nslab, S // tm),
        in_specs=[pl.BlockSpec((tm, D), lambda k, i: (i, 0)), pl.BlockSpec((D, 2 * fs), lambda k, i: (0, k))],
        out_specs=[pl.BlockSpec((tm, 2 * fs), lambda k, i: (i, k)), pl.BlockSpec((tm, fs), lambda k, i: (i, k))],
        out_shape=[jax.ShapeDtypeStruct((S, nslab * 2 * fs), F32), jax.ShapeDtypeStruct((S, nslab * fs), BF16)],
        compiler_params=pltpu.CompilerParams(dimension_semantics=("parallel", "parallel"),
                                             vmem_limit_bytes=_vmem_limit(blk, _nbytes((tm, 2 * fs), F32))),
    )(n, wgu)


def _ffn_dact(name, dyb, wd, gu, fs, alpha, after=None, tm_target=512):
    S, D = dyb.shape
    nslab = wd.shape[0] // fs
    tm = _tile(S, tm_target)

    def body(*refs):
        d_ref, w_ref, gu_ref = refs[:3]
        o_ref = refs[-1]
        dhv = _dot_nt(d_ref[...], w_ref[...]) * alpha
        gv = gu_ref[:, :fs]
        uv = gu_ref[:, fs:]
        sg = jax.nn.sigmoid(gv)
        silu = gv * sg
        o_ref[:, :fs] = (dhv * uv * (sg + silu * (1.0 - sg))).astype(BF16)
        o_ref[:, fs:] = (dhv * silu).astype(BF16)

    in_specs = [pl.BlockSpec((tm, D), lambda k, i: (i, 0)), pl.BlockSpec((fs, D), lambda k, i: (k, 0)),
                pl.BlockSpec((tm, 2 * fs), lambda k, i: (i, k))] + ([_ANY] if after is not None else [])
    args = (dyb, wd, gu) + ((after,) if after is not None else ())
    blk = _nbytes((tm, D), BF16) + _nbytes((fs, D), BF16) + _nbytes((tm, 2 * fs), F32) + _nbytes((tm, 2 * fs), BF16)
    return pl.pallas_call(
        body, name=name, grid=(nslab, S // tm), in_specs=in_specs,
        out_specs=pl.BlockSpec((tm, 2 * fs), lambda k, i: (i, k)),
        out_shape=jax.ShapeDtypeStruct((S, nslab * 2 * fs), BF16),
        compiler_params=pltpu.CompilerParams(dimension_semantics=("parallel", "parallel"),
                                             vmem_limit_bytes=_vmem_limit(blk, 2 * _nbytes((tm, fs), F32))),
    )(*args)


def _tri_rows(r0, nrows, ncols, lower):
    row = lax.broadcasted_iota(jnp.int32, (nrows, ncols), 0) + r0
    col = lax.broadcasted_iota(jnp.int32, (nrows, ncols), 1)
    return jnp.where((col <= row) if lower else (col >= row), 1.0, 0.0).astype(F32)


def _gate_fwd(name, hf, b):
    S = hf.shape[0]
    tb = _row_tile(S, 256)

    def body(hf_ref, b_ref, cf_ref, cft_ref, lf_ref):
        zz = hf_ref[...] + b_ref[...]
        lf_ref[...] = jnp.minimum(zz, 0.0) - jnp.log1p(jnp.exp(-jnp.abs(zz)))

        def blk(i, c):
            r0 = pl.multiple_of(i * tb, tb)
            tri = _tri_rows(r0, tb, S, True)
            cf_ref[pl.ds(r0, tb), :] = jnp.dot(tri, lf_ref[...], precision=lax.Precision.HIGHEST,
                                               preferred_element_type=F32)
            return c

        lax.fori_loop(0, S // tb, blk, 0)
        cft_ref[...] = cf_ref[...].T

    full = pl.BlockSpec((S, LANES), lambda: (0, 0))
    return pl.pallas_call(
        body, name=name, in_specs=[full, pl.BlockSpec((1, LANES), lambda: (0, 0))],
        out_specs=[full, pl.BlockSpec((LANES, S), lambda: (0, 0))],
        out_shape=[jax.ShapeDtypeStruct((S, LANES), F32), jax.ShapeDtypeStruct((LANES, S), F32)],
        scratch_shapes=[pltpu.VMEM((S, LANES), F32)],
    )(hf, b)


def _gate_bwd(name, dcft, drow, hf, b):
    S = hf.shape[0]
    tb = _row_tile(S, 256)

    def body(dcft_ref, drow_ref, hf_ref, b_ref, dhf_ref, db_ref, dcf_ref, dlf_ref):
        dcf_ref[...] = dcft_ref[...].T + drow_ref[...]

        def blk(i, c):
            r0 = pl.multiple_of(i * tb, tb)
            tri = _tri_rows(r0, tb, S, False)
            dlf_ref[pl.ds(r0, tb), :] = jnp.dot(tri, dcf_ref[...], precision=lax.Precision.HIGHEST,
                                                preferred_element_type=F32)
            return c

        lax.fori_loop(0, S // tb, blk, 0)
        zz = hf_ref[...] + b_ref[...]
        dhf = dlf_ref[...] * jax.nn.sigmoid(-zz)
        dhf_ref[...] = dhf.astype(BF16)
        db_ref[...] = jnp.sum(dhf, axis=0, keepdims=True)

    full = pl.BlockSpec((S, LANES), lambda: (0, 0))
    vec = pl.BlockSpec((1, LANES), lambda: (0, 0))
    return pl.pallas_call(
        body, name=name, in_specs=[pl.BlockSpec((LANES, S), lambda: (0, 0)), full, full, vec],
        out_specs=[full, vec],
        out_shape=[jax.ShapeDtypeStruct((S, LANES), BF16), jax.ShapeDtypeStruct((1, LANES), F32)],
        scratch_shapes=[pltpu.VMEM((S, LANES), F32), pltpu.VMEM((S, LANES), F32)],
    )(dcft, drow, hf, b)


def _rope_tables(S):
    half = ROPE_DIMS // 2
    freqs = ROPE_THETA ** (-jnp.arange(half, dtype=F32) / half)
    ang = jnp.arange(S, dtype=F32)[:, None] * freqs[None, :]
    cos, sin = jnp.cos(ang), jnp.sin(ang)
    pad = HEAD_DIM - ROPE_DIMS
    c = jnp.concatenate([cos, cos, jnp.ones((S, pad), F32)], axis=1)
    s = jnp.concatenate([-sin, sin, jnp.zeros((S, pad), F32)], axis=1)
    return c, s


def _rope_swap(x):
    half = ROPE_DIMS // 2
    lane = lax.broadcasted_iota(jnp.int32, x.shape, 1)
    upper = jnp.where(lane < ROPE_DIMS, pltpu.roll(x, half, 1), 0.0)
    return jnp.where(lane < half, pltpu.roll(x, HEAD_DIM - half, 1), upper)


def _rope(x, c, s):
    return x * c + _rope_swap(x) * s


def _rope_t(dy, c, s):
    return dy * c + _rope_swap(dy * s)


def _split_dot(x, t):
    hi = x.astype(BF16)
    lo = (x - hi.astype(F32)).astype(BF16)
    return (jnp.dot(hi, t, preferred_element_type=F32) + jnp.dot(lo, t, preferred_element_type=F32))


_NT = (((1,), (1,)), ((), ()))
_TN = (((0,), (0,)), ((), ()))


def _dot_nt(a, b):
    return lax.dot_general(a, b, _NT, preferred_element_type=F32)


def _dot_tn(a, b):
    return lax.dot_general(a, b, _TN, preferred_element_type=F32)


def _blk(i):
    return pl.ds(pl.multiple_of(i * BLK, BLK), BLK)


def _dilated_mult(delta):
    c = jnp.zeros(delta.shape, F32)
    for window, dil in DILATED_PATTERNS:
        ok = (delta >= 0) & (delta <= window) & ((delta & (dil - 1)) == 0)
        c = c + jnp.where(ok, 1.0, 0.0)
    return c


def _query_block(S):
    return min(512, S)


def _offsets(d, bq):
    row = jnp.arange(bq, dtype=jnp.int32)[:, None]
    col = jnp.arange(BLK, dtype=jnp.int32)[None, :]
    return d * BLK + row - col


def _causal_tables(bq, strict):
    r = bq // BLK
    tabs = []
    for d in range(-(r - 1), 1):
        delta = _offsets(d, bq)
        tabs.append(jnp.where((delta > 0) if strict else (delta >= 0), 1.0, 0.0))
    tabs.append(jnp.ones((bq, BLK), F32))
    return jnp.stack(tabs).astype(F32)


def _dilated_tables(bq):
    r = bq // BLK
    limit = sorted(w for w, _ in DILATED_PATTERNS)[-2]
    assert all(BLK % dil == 0 for _, dil in DILATED_PATTERNS)
    d_far = -(-(limit + BLK) // BLK)
    tabs = []
    for d in range(-(r - 1), d_far + 1):
        mult = _dilated_mult(_offsets(d, bq))
        tabs.append(jnp.where(mult > 0, jnp.log(jnp.maximum(mult, 1.0)), NEG_INF))
    return jnp.stack(tabs).astype(F32)


def _qblk(i, bq):
    return pl.ds(pl.multiple_of(i * bq, bq), bq)


def _sb_block(z, valid, t_ex, run):
    t = jnp.log1p(jnp.exp(-jnp.abs(z)))
    lsig = jnp.minimum(z, 0.0) - t
    m = -(jnp.maximum(z, 0.0) + t) * valid
    after = _split_dot(m, t_ex)
    a = jnp.exp(lsig + after + run) * valid
    return a, m, lsig


def _attn_fwd_wide(name, hq, layer_kind, n_heads, n_sb, cf=None, cft=None, rope_c=None, rope_s=None):
    S = hq.shape[0]
    D = n_heads * HEAD_DIM
    bq = _query_block(S)
    r = bq // BLK
    nq = S // bq
    scale = HEAD_DIM ** -0.5
    even = layer_kind == "even"
    if even:
        tabs = (jnp.where(_causal_tables(bq, False) > 0, 0.0, NEG_INF), _causal_tables(bq, True))
    else:
        tabs = (_dilated_tables(bq),)
    n_tab = tabs[0].shape[0]

    def body(*refs):
        if even:
            q_ref, k_ref, v_ref, cf_ref, cft_ref, bias_ref, valid_ref, o_ref, ob_ref, lse_ref, qs, ks, vs = refs
        else:
            q_ref, k_ref, v_ref, c_ref, s_ref, bias_ref, o_ref, ob_ref, lse_ref, qs, ks, vs = refs
        h = pl.program_id(0)
        if even:
            qs[...] = q_ref[...].astype(BF16)
            ks[...] = k_ref[...].astype(BF16)
        else:
            qs[...] = _rope(q_ref[...], c_ref[...], s_ref[...]).astype(BF16)
            ks[...] = _rope(k_ref[...], c_ref[...], s_ref[...]).astype(BF16)
        vs[...] = v_ref[...].astype(BF16)

        def softmax_head(hh):
            def qblock(i, carry):
                qi = qs[_qblk(i, bq), :]
                if even:
                    lane = lax.broadcasted_iota(jnp.int32, (bq, LANES), 1)
                    cfq = jnp.sum(jnp.where(lane == hh, cf_ref[_qblk(i, bq), :], 0.0), axis=1, keepdims=True)

                def kblock(j, c):
                    m_run, l_run, acc = c
                    z = _dot_nt(qi, ks[_blk(j), :]) * scale + bias_ref[jnp.minimum(r * i - j + (r - 1), n_tab - 1)]
                    if even:
                        z = z + (cfq - cft_ref[hh, :, _blk(j)])
                    m_new = jnp.maximum(m_run, jnp.max(z, axis=1, keepdims=True))
                    p = jnp.exp(z - m_new)
                    alpha = jnp.exp(m_run - m_new)
                    l_new = alpha * l_run + jnp.sum(p, axis=1, keepdims=True)
                    acc = alpha * acc + jnp.dot(p.astype(BF16), vs[_blk(j), :], preferred_element_type=F32)
                    return m_new, l_new, acc

                init = (jnp.full((bq, 1), NEG_INF, F32), jnp.zeros((bq, 1), F32), jnp.zeros((bq, HEAD_DIM), F32))
                m_run, l_run, acc = lax.fori_loop(0, r * (i + 1), kblock, init)
                o = acc / l_run
                o_ref[_qblk(i, bq), :] = o
                ob_ref[_qblk(i, bq), :] = o.astype(BF16)
                lse_ref[_qblk(i, bq), :] = jnp.broadcast_to(m_run + jnp.log(l_run), (bq, HEAD_DIM))
                return carry

            lax.fori_loop(0, nq, qblock, 0)

        def sb_head():
            row = lax.broadcasted_iota(jnp.int32, (BLK, BLK), 0)
            col = lax.broadcasted_iota(jnp.int32, (BLK, BLK), 1)
            t_ex = jnp.where(row > col, 1.0, 0.0).astype(BF16)

            def qblock(i, carry):
                qi = qs[_qblk(i, bq), :]

                def kblock(jj, c):
                    run, acc, rest = c
                    j = r * (i + 1) - 1 - jj
                    z = _dot_nt(qi, ks[_blk(j), :]) * scale
                    a, m, _ = _sb_block(z, valid_ref[jnp.minimum(r * i - j + (r - 1), r)], t_ex, run)
                    vj = vs[_blk(j), :]
                    hi = a.astype(BF16)
                    lo = (a - hi.astype(F32)).astype(BF16)
                    acc = acc + jnp.dot(hi, vj, preferred_element_type=F32)
                    rest = rest + jnp.dot(lo, vj, preferred_element_type=F32)
                    return run + jnp.sum(m, axis=1, keepdims=True), acc, rest

                zero = jnp.zeros((bq, HEAD_DIM), F32)
                _, acc, rest = lax.fori_loop(0, r * (i + 1), kblock, (jnp.zeros((bq, 1), F32), zero, zero))
                o_ref[_qblk(i, bq), :] = acc + rest
                ob_ref[_qblk(i, bq), :] = acc.astype(BF16)
                lse_ref[_qblk(i, bq), :] = jnp.zeros((bq, HEAD_DIM), F32)
                return carry

            lax.fori_loop(0, nq, qblock, 0)

        if even:
            @pl.when(h < n_sb)
            def _():
                sb_head()

            @pl.when(h >= n_sb)
            def _():
                softmax_head(h - n_sb)
        else:
            softmax_head(h)

    head = lambda off: pl.BlockSpec((S, HEAD_DIM), lambda h, off=off: (0, off + h))
    full = pl.BlockSpec((S, LANES), lambda h: (0, 0))
    tab_specs = [pl.BlockSpec(t.shape, lambda h: (0, 0, 0)) for t in tabs]
    if even:
        extra_specs = [full, pl.BlockSpec(cft.shape, lambda h: (0, 0, 0))] + tab_specs
        extra = (cf, cft) + tabs
    else:
        extra_specs = [full, full] + tab_specs
        extra = (rope_c, rope_s) + tabs
    blk_bytes = 8 * _nbytes((S, HEAD_DIM), F32) + sum(_nbytes(t.shape, F32) for t in tabs)
    return pl.pallas_call(
        body, name=name, grid=(n_heads,),
        in_specs=[head(0), head(n_heads), head(2 * n_heads)] + extra_specs,
        out_specs=[head(0), head(0), head(0)],
        out_shape=[jax.ShapeDtypeStruct((S, D), F32), jax.ShapeDtypeStruct((S, D), BF16),
                   jax.ShapeDtypeStruct((S, D), F32)],
        scratch_shapes=[pltpu.VMEM((S, HEAD_DIM), BF16)] * 3,
        compiler_params=pltpu.CompilerParams(dimension_semantics=("arbitrary",),
                                             vmem_limit_bytes=_vmem_limit(blk_bytes, 3 * _nbytes((S, HEAD_DIM), BF16))),
    )(hq, hq, hq, *extra)


def _attn_bwd_wide(name, hq, do, o, lse, layer_kind, n_heads, n_sb, cf=None, cft=None, rope_c=None, rope_s=None):
    S = hq.shape[0]
    D = n_heads * HEAD_DIM
    bq = _query_block(S)
    r = bq // BLK
    nq = S // bq
    scale = HEAD_DIM ** -0.5
    even = layer_kind == "even"
    if even:
        tabs = (jnp.where(_causal_tables(bq, False) > 0, 0.0, NEG_INF), _causal_tables(bq, True))
    else:
        tabs = (_dilated_tables(bq),)
    n_tab = tabs[0].shape[0]

    def body(*refs):
        if even:
            (q_ref, k_ref, v_ref, do_ref, o_ref, lse_ref, cf_ref, cft_ref, bias_ref, valid_ref,
             dh_ref, dcft_ref, drow_ref, qs, ks, vs, dos, dq_acc, dk_acc, dv_acc) = refs
        else:
            (q_ref, k_ref, v_ref, do_ref, o_ref, lse_ref, c_ref, s_ref, bias_ref,
             dh_ref, qs, ks, vs, dos, dq_acc, dk_acc, dv_acc) = refs
        h = pl.program_id(0)
        if even:
            qs[...] = q_ref[...].astype(BF16)
            ks[...] = k_ref[...].astype(BF16)

            @pl.when(h == 0)
            def _():
                dcft_ref[...] = jnp.zeros_like(dcft_ref)
                drow_ref[...] = jnp.zeros_like(drow_ref)
        else:
            qs[...] = _rope(q_ref[...], c_ref[...], s_ref[...]).astype(BF16)
            ks[...] = _rope(k_ref[...], c_ref[...], s_ref[...]).astype(BF16)
        vs[...] = v_ref[...].astype(BF16)
        dos[...] = do_ref[...].astype(BF16)
        dk_acc[...] = jnp.zeros_like(dk_acc)
        dv_acc[...] = jnp.zeros_like(dv_acc)

        def softmax_head(hh):
            def qblock(i, carry):
                qi = qs[_qblk(i, bq), :]
                doi = dos[_qblk(i, bq), :]
                dvec = jnp.sum(do_ref[_qblk(i, bq), :] * o_ref[_qblk(i, bq), :], axis=1, keepdims=True)
                lse_i = jnp.max(lse_ref[_qblk(i, bq), :], axis=1, keepdims=True)
                if even:
                    lane = lax.broadcasted_iota(jnp.int32, (bq, LANES), 1)
                    cfq = jnp.sum(jnp.where(lane == hh, cf_ref[_qblk(i, bq), :], 0.0), axis=1, keepdims=True)

                def kblock(j, c):
                    dq, ds_rows = c
                    kj = ks[_blk(j), :]
                    z = _dot_nt(qi, kj) * scale + bias_ref[jnp.minimum(r * i - j + (r - 1), n_tab - 1)]
                    if even:
                        z = z + (cfq - cft_ref[hh, :, _blk(j)])
                    p = jnp.exp(z - lse_i)
                    dp = _dot_nt(doi, vs[_blk(j), :])
                    ds = p * (dp - dvec)
                    dsb = (ds * scale).astype(BF16)
                    dk_acc[_blk(j), :] += _dot_tn(dsb, qi)
                    dv_acc[_blk(j), :] += _dot_tn(p.astype(BF16), doi)
                    if even:
                        dcft_ref[hh, :, _blk(j)] += -jnp.sum(ds, axis=0, keepdims=True)
                    return (dq + jnp.dot(dsb, kj, preferred_element_type=F32),
                            ds_rows + jnp.sum(ds, axis=1, keepdims=True))

                dq, ds_rows = lax.fori_loop(0, r * (i + 1), kblock,
                                            (jnp.zeros((bq, HEAD_DIM), F32), jnp.zeros((bq, 1), F32)))
                dq_acc[_qblk(i, bq), :] = dq
                if even:
                    drow_ref[_qblk(i, bq), :] += jnp.where(lane == hh, ds_rows, 0.0)
                return carry

            lax.fori_loop(0, nq, qblock, 0)

        def sb_head():
            row = lax.broadcasted_iota(jnp.int32, (BLK, BLK), 0)
            col = lax.broadcasted_iota(jnp.int32, (BLK, BLK), 1)
            t_ex = jnp.where(row > col, 1.0, 0.0).astype(BF16)
            t_in = jnp.where(row >= col, 1.0, 0.0).astype(BF16)

            def qblock(i, carry):
                qi = qs[_qblk(i, bq), :]
                doi = dos[_qblk(i, bq), :]
                nkb = r * (i + 1)
                e_tot = jnp.sum(doi.astype(F32) * o_ref[_qblk(i, bq), :], axis=1, keepdims=True)
                zero = jnp.zeros((bq, 1), F32)

                def kblock(jj, c):
                    run, e_run, dq = c
                    j = nkb - 1 - jj
                    kj = ks[_blk(j), :]
                    z = _dot_nt(qi, kj) * scale
                    valid = valid_ref[jnp.minimum(r * i - j + (r - 1), r)]
                    a, m, lsig = _sb_block(z, valid, t_ex, run)
                    sig = jnp.exp(lsig)
                    e = _dot_nt(doi, vs[_blk(j), :]) * a
                    e_before = e_tot - (_split_dot(e, t_in) + e_run)
                    dz = (e * (1.0 - sig) - sig * e_before) * valid
                    dzb = (dz * scale).astype(BF16)
                    dk_acc[_blk(j), :] += _dot_tn(dzb, qi)
                    dv_acc[_blk(j), :] += _dot_tn(a.astype(BF16), doi)
                    return (run + jnp.sum(m, axis=1, keepdims=True), e_run + jnp.sum(e, axis=1, keepdims=True),
                            dq + jnp.dot(dzb, kj, preferred_element_type=F32))

                _, _, dq = lax.fori_loop(0, nkb, kblock, (zero, zero, jnp.zeros((bq, HEAD_DIM), F32)))
                dq_acc[_qblk(i, bq), :] = dq
                return carry

            lax.fori_loop(0, nq, qblock, 0)

        if even:
            @pl.when(h < n_sb)
            def _():
                sb_head()

            @pl.when(h >= n_sb)
            def _():
                softmax_head(h - n_sb)

            dh_ref[0] = dq_acc[...].astype(BF16)
            dh_ref[1] = dk_acc[...].astype(BF16)
        else:
            softmax_head(h)
            dh_ref[0] = _rope_t(dq_acc[...], c_ref[...], s_ref[...]).astype(BF16)
            dh_ref[1] = _rope_t(dk_acc[...], c_ref[...], s_ref[...]).astype(BF16)
        dh_ref[2] = dv_acc[...].astype(BF16)

    head = lambda off: pl.BlockSpec((S, HEAD_DIM), lambda h, off=off: (0, off + h))
    full = pl.BlockSpec((S, LANES), lambda h: (0, 0))
    tfull = pl.BlockSpec((n_heads - n_sb, 1, S), lambda h: (0, 0, 0))
    tab_specs = [pl.BlockSpec(t.shape, lambda h: (0, 0, 0)) for t in tabs]
    dh_spec = pl.BlockSpec((3, S, HEAD_DIM), lambda h: (0, 0, h))
    dh_shape = jax.ShapeDtypeStruct((3, S, D), BF16)
    if even:
        extra_specs, extra = [full, tfull] + tab_specs, (cf, cft) + tabs
        out_specs = [dh_spec, tfull, full]
        out_shape = [dh_shape, jax.ShapeDtypeStruct((n_heads - n_sb, 1, S), F32),
                     jax.ShapeDtypeStruct((S, LANES), F32)]
    else:
        extra_specs, extra = [full, full] + tab_specs, (rope_c, rope_s) + tabs
        out_specs = [dh_spec]
        out_shape = [dh_shape]
    blk_bytes = 10 * _nbytes((S, HEAD_DIM), F32) + sum(_nbytes(t.shape, F32) for t in tabs)
    scratch_bytes = 4 * _nbytes((S, HEAD_DIM), BF16) + 3 * _nbytes((S, HEAD_DIM), F32)
    return pl.pallas_call(
        body, name=name, grid=(n_heads,),
        in_specs=[head(0), head(n_heads), head(2 * n_heads), head(0), head(0), head(0)] + extra_specs,
        out_specs=out_specs, out_shape=out_shape,
        scratch_shapes=[pltpu.VMEM((S, HEAD_DIM), BF16)] * 4 + [pltpu.VMEM((S, HEAD_DIM), F32)] * 3,
        compiler_params=pltpu.CompilerParams(dimension_semantics=("arbitrary",),
                                             vmem_limit_bytes=_vmem_limit(blk_bytes, scratch_bytes)),
    )(hq, hq, hq, do, o, lse, *extra)


def _ffn_fwd(tag, x, g, wgu, wd, fs):
    n = _rms_fwd(tag + "_norm", x, g)
    gu, h = _ffn_up(tag + "_gu", n, wgu, fs)
    if callable(wd):
        wd = wd(h)
    y = _mm(tag + "_down", h, wd, "nn", F32, res=x, alpha=0.5)
    return y, (x, g, n, gu, h), wd


def _ffn_bwd(tag, dx, dxb, wgu, wd, fs, saved, after=None, emit=None):
    x, g, n, gu, h = saved
    dgu = _ffn_dact(tag + "_dgu", dxb, wd, gu, fs, 0.5, after=after)
    dwd = _mm(tag + "_dwd", h, dxb, "tn", BF16, alpha=0.5)
    dwgu = _mm(tag + "_dwgu", n, dgu, "tn", BF16)
    token = emit(dwgu, dwd) if emit else None
    dn = _mm(tag + "_dn", dgu, wgu, "nt", F32, after=token)
    dx_in, dxb_in, dg = _rms_bwd(tag + "_dnorm", dn, x, g, dx)
    return dx_in, dxb_in, dg, dwgu, dwd, token


def _mixer_fwd(tag, kind, x, g, wqkv, wout, n_heads, n_sb, wf=None, bf=None, rope=None):
    n = _rms_fwd(tag + "_norm", x, g)
    hq = _mm(tag + "_qkv", n, wqkv, "nn", F32)
    if kind == "even":
        hf = _mm(tag + "_gate", n, wf, "nn", F32)
        cf, cft = _gate_fwd(tag + "_cumgate", hf, bf)
        cft = cft[:n_heads - n_sb].reshape(n_heads - n_sb, 1, -1)
        o, ob, lse = _attn_fwd_wide(tag + "_attn", hq, kind, n_heads, n_sb, cf=cf, cft=cft)
    else:
        hf = cf = cft = None
        o, ob, lse = _attn_fwd_wide(tag + "_attn", hq, kind, n_heads, n_sb, rope_c=rope[0], rope_s=rope[1])
    y = _mm(tag + "_out", ob, wout, "nn", F32, res=x)
    return y, (x, g, n, hq, hf, cf, cft, o, ob, lse)


def _mixer_bwd(tag, kind, dx, dxb, wqkv, wout, n_heads, n_sb, saved, wf=None, bf=None, rope=None, after=None,
               emit=None):
    x, g, n, hq, hf, cf, cft, o, ob, lse = saved
    do = _mm(tag + "_do", dxb, wout, "nt", F32, after=after)
    dwout = _mm(tag + "_dwout", ob, dxb, "tn", BF16)
    if kind == "even":
        dh3, dcft, drow = _attn_bwd_wide(tag + "_dattn", hq, do, o, lse, kind, n_heads, n_sb, cf=cf, cft=cft)
    else:
        (dh3,) = _attn_bwd_wide(tag + "_dattn", hq, do, o, lse, kind, n_heads, n_sb, rope_c=rope[0], rope_s=rope[1])
    dwqkv = _mm(tag + "_dwqkv", n, dh3, "tn", BF16)
    dwf = db = dhf = None
    if kind == "even":
        n_fox = n_heads - n_sb
        dcft = jnp.pad(dcft.reshape(n_fox, -1), ((0, LANES - n_fox), (0, 0)))
        dhf, db = _gate_bwd(tag + "_dcumgate", dcft, drow, hf, bf)
        dwf = _mm(tag + "_dwf", n, dhf, "tn", BF16)
    token = emit(dwqkv, dwout, dwf) if emit else None
    dn = _mm(tag + "_dn", dh3, wqkv, "nt", F32, after=token)
    if kind == "even":
        dn = _mm(tag + "_dn_gate", dhf, wf, "nt", F32, res=dn)
    dx_in, dxb_in, dg = _rms_bwd(tag + "_dnorm", dn, x, g, dx)
    return dx_in, dxb_in, dg, dwqkv, dwout, dwf, db, token


def _local_step(x, target, w, fs, n_heads, n_sb, emit=None):
    S, D = x.shape
    rope = _rope_tables(S)
    kinds = ("even", "odd")
    saved = []
    h = x
    if callable(w):
        fetch, w = w, {"norm_g": w("norm_g", None), "final_g": w("final_g", None),
                       "wgu1": [None, None], "wd1": [None, None], "wgu2": [None, None], "wd2": [None, None]}
    else:
        fetch = None
    for l, kind in enumerate(kinds):
        ng = [w["norm_g"][l, i][None, :] for i in range(3)]
        if fetch:
            w["wgu1"][l], w["wd1"][l] = fetch(("ffn1", l), h)
        h, s1, wd = _ffn_fwd(f"l{l}_ffn1", h, ng[0], w["wgu1"][l], w["wd1"][l], fs)
        if fetch:
            w["wd1"][l] = wd
        if fetch:
            w.update(fetch(("mix", l), h))
        if kind == "even":
            h, s2 = _mixer_fwd(f"l{l}_mix", kind, h, ng[1], w["wqkv_e"], w["wout_e"], n_heads, n_sb,
                               wf=w["wf"], bf=w["bf"])
        else:
            h, s2 = _mixer_fwd(f"l{l}_mix", kind, h, ng[1], w["wqkv_o"], w["wout_o"], n_heads, n_sb, rope=rope)
        if fetch:
            w["wgu2"][l], w["wd2"][l] = fetch(("ffn2", l), h)
        h, s3, _ = _ffn_fwd(f"l{l}_ffn2", h, ng[2], w["wgu2"][l], w["wd2"][l], fs)
        saved.append((s1, s2, s3))

    dx, dxb, dfinal, loss = _loss_head("loss_head", h, w["final_g"], target)
    grads = {"dfinal": dfinal, "dnorm": [[None] * 3 for _ in kinds],
             "dwgu1": [None, None], "dwd1": [None, None], "dwgu2": [None, None], "dwd2": [None, None]}
    hand = lambda block: (lambda *mats: emit(block, mats)) if emit else None
    token = None
    for l in (1, 0):
        kind = kinds[l]
        s1, s2, s3 = saved[l]
        dx, dxb, dg, grads["dwgu2"][l], grads["dwd2"][l], token = _ffn_bwd(
            f"l{l}_ffn2", dx, dxb, w["wgu2"][l], w["wd2"][l], fs, s3, after=token, emit=hand(("ffn2", l)))
        grads["dnorm"][l][2] = dg
        if kind == "even":
            dx, dxb, dg, grads["dwqkv_e"], grads["dwout_e"], grads["dwf"], grads["db"], token = _mixer_bwd(
                f"l{l}_mix", kind, dx, dxb, w["wqkv_e"], w["wout_e"], n_heads, n_sb, s2, wf=w["wf"], bf=w["bf"],
                after=token, emit=hand(("mix", l)))
        else:
            dx, dxb, dg, grads["dwqkv_o"], grads["dwout_o"], _, _, token = _mixer_bwd(
                f"l{l}_mix", kind, dx, dxb, w["wqkv_o"], w["wout_o"], n_heads, n_sb, s2, rope=rope,
                after=token, emit=hand(("mix", l)))
        grads["dnorm"][l][1] = dg
        dx, dxb, dg, grads["dwgu1"][l], grads["dwd1"][l], token = _ffn_bwd(
            f"l{l}_ffn1", dx, dxb, w["wgu1"][l], w["wd1"][l], fs, s1, after=token, emit=hand(("ffn1", l)))
        grads["dnorm"][l][0] = dg
    return loss, dx, grads


def _cast_into(name, shard, layer, chip, full_shape, place, full=None, after=None):
    R, C = shard.shape[-2:]
    tr = _row_tile(R, 512, step=16)
    if layer is None:
        in_spec = pl.BlockSpec((tr, C), lambda i, k: (i, 0))
    else:
        in_spec = pl.BlockSpec((None, tr, C), lambda i, k: (layer, i, 0))
    lead = (None,) * (len(full_shape) - 2)
    out_spec = pl.BlockSpec(lead + (tr, C), lambda i, k: place(i, k[0]))

    def body(*refs):
        k_ref, w_ref = refs[:2]
        o_ref = refs[-1]
        o_ref[...] = w_ref[...].astype(BF16)

    in_specs = [in_spec] + ([_ANY] if full is not None else []) + ([_ANY] if after is not None else [])
    args = (chip, shard) + ((full,) if full is not None else ()) + ((after,) if after is not None else ())
    grid_spec = pltpu.PrefetchScalarGridSpec(num_scalar_prefetch=1, grid=(R // tr,), in_specs=in_specs, out_specs=out_spec)
    return pl.pallas_call(
        body, name=name, grid_spec=grid_spec, out_shape=jax.ShapeDtypeStruct(full_shape, BF16),
        input_output_aliases={2: 0} if full is not None else {},
        compiler_params=pltpu.CompilerParams(dimension_semantics=("arbitrary",)),
    )(*args)


def _region_shape(grad, kind):
    if kind == "lead":
        return grad.shape[1] // N_CORES, grad.shape[2]
    rows, cols = grad.shape
    if kind == "cols":
        return rows // N_CORES, cols // N_CHIPS
    return rows // (N_CHIPS * N_CORES), cols


def _region_add(name, grad, kind, landed, core):
    rh, cw = _region_shape(grad, kind)
    tr = _row_tile(rh, 256, step=16)
    nrb = rh // tr
    if kind == "cols":
        g_spec = pl.BlockSpec((tr, cw), lambda k, r, c: (c[0] * nrb + r, k))
    elif kind == "rows":
        g_spec = pl.BlockSpec((tr, cw), lambda k, r, c: ((N_CORES * k + c[0]) * nrb + r, 0))
    else:
        g_spec = pl.BlockSpec((None, tr, cw), lambda k, r, c: (k, c[0] * nrb + r, 0))
    l_spec = pl.BlockSpec((None, tr, cw), lambda k, r, c: (k, r, 0))

    def body(c_ref, g_ref, l_ref, o_ref):
        o_ref[...] = (g_ref[...].astype(F32) + l_ref[...].astype(F32)).astype(BF16)

    grid_spec = pltpu.PrefetchScalarGridSpec(
        num_scalar_prefetch=1, grid=(N_CHIPS, nrb), in_specs=[g_spec, l_spec], out_specs=l_spec)
    return pl.pallas_call(
        body, name=name, grid_spec=grid_spec, out_shape=jax.ShapeDtypeStruct(landed.shape, BF16),
        compiler_params=pltpu.CompilerParams(dimension_semantics=("parallel", "parallel"),
                                             vmem_limit_bytes=_vmem_limit(3 * _nbytes((tr, cw), F32))),
    )(core, grad, landed)


def _chip_sum(name, pair, landed, pos):
    _, rh, cw = pair.shape
    tr = _row_tile(rh, max(16, 2**20 // (cw * 4)), step=16)
    nrb = rh // tr

    def body(p_ref, own_ref, l_ref, o_ref):
        acc = own_ref[...].astype(F32)
        for s in range(N_CHIPS - 1):
            acc = acc + l_ref[s].astype(F32)
        o_ref[...] = acc

    grid_spec = pltpu.PrefetchScalarGridSpec(
        num_scalar_prefetch=1, grid=(nrb,),
        in_specs=[pl.BlockSpec((None, tr, cw), lambda r, p: (p[0], r, 0)),
                  pl.BlockSpec((N_CHIPS - 1, tr, cw), lambda r, p: (0, r, 0))],
        out_specs=pl.BlockSpec((tr, cw), lambda r, p: (p[1] * nrb + r, 0)))
    return pl.pallas_call(
        body, name=name, grid_spec=grid_spec, out_shape=jax.ShapeDtypeStruct((N_CORES * rh, cw), F32),
        compiler_params=pltpu.CompilerParams(dimension_semantics=("arbitrary",)),
    )(pos, pair, landed)


def _sum_leading(name, parts):
    n, R, C = parts.shape
    tr = _row_tile(R, max(8, (2**20 // (C * 4)) // 8 * 8))

    def body(p_ref, o_ref):
        acc = p_ref[0]
        for s in range(1, n):
            acc = acc + p_ref[s]
        o_ref[...] = acc

    return pl.pallas_call(
        body, name=name, grid=(R // tr,),
        in_specs=[pl.BlockSpec((n, tr, C), lambda i: (0, i, 0))],
        out_specs=pl.BlockSpec((tr, C), lambda i: (i, 0)),
        out_shape=jax.ShapeDtypeStruct((R, C), F32),
        compiler_params=pltpu.CompilerParams(dimension_semantics=("parallel",)),
    )(parts)


def _adamw(name, w, g, m, v):
    shape = w.shape
    to2d = lambda t: t.reshape(-1, shape[-1]) if t.ndim > 1 else t.reshape(1, -1)
    w2, g2, m2, v2 = (to2d(t) for t in (w, g, m, v))
    R, C = w2.shape
    tr = _row_tile(R, 256)

    def body(w_ref, g_ref, m_ref, v_ref, d_ref, nm_ref, nv_ref):
        gv = g_ref[...]
        nm = ADAM_B1 * m_ref[...] + (1.0 - ADAM_B1) * gv
        nv = ADAM_B2 * v_ref[...] + (1.0 - ADAM_B2) * (gv * gv)
        m_hat = nm / (1.0 - ADAM_B1 ** ADAM_STEP)
        v_hat = nv / (1.0 - ADAM_B2 ** ADAM_STEP)
        d_ref[...] = -ADAM_LR * (m_hat / (jnp.sqrt(v_hat) + ADAM_EPS) + ADAM_WD * w_ref[...])
        nm_ref[...] = nm
        nv_ref[...] = nv

    spec = pl.BlockSpec((tr, C), lambda i: (i, 0))
    sds = jax.ShapeDtypeStruct((R, C), F32)
    d, nm, nv = pl.pallas_call(
        body, name=name, grid=(R // tr,), in_specs=[spec] * 4, out_specs=[spec] * 3, out_shape=[sds] * 3,
        compiler_params=pltpu.CompilerParams(dimension_semantics=("parallel",),
                                             vmem_limit_bytes=_vmem_limit(7 * _nbytes((tr, C), F32))),
    )(w2, g2, m2, v2)
    return d.reshape(shape), nm.reshape(shape), nv.reshape(shape)


_ANY = pl.BlockSpec(memory_space=pl.ANY)


def _mesh_pos():
    return lax.axis_index("x"), lax.axis_index("y"), lax.axis_index("c")


def _other_chips(x, y):
    return [(1 - x, y), (x, 1 - y), (1 - x, 1 - y)]


def _gather_over_chips(name, fulls, views):
    n = len(views)
    nf = len(fulls)

    def body(*refs):
        full = refs[nf:2 * nf]
        ici_send, ici_recv, d2d_send, d2d_recv = refs[2 * nf:]
        x, y, c = _mesh_pos()
        chips = _other_chips(x, y)
        mine = 2 * x + y
        sibling = (x, y, 1 - c)

        def ici(a, p, k):
            i, view, _ = views[a]
            part = view(full[i], k, c)
            return pltpu.make_async_remote_copy(
                src_ref=part, dst_ref=part, send_sem=ici_send.at[a, p], recv_sem=ici_recv.at[a, p],
                device_id=(*chips[p], c), device_id_type=MESH)

        def d2d(a, p, h):
            i, view, _ = views[a]
            px, py = chips[p]
            part = view(full[i], 2 * px + py, h)
            return pltpu.make_async_remote_copy(
                src_ref=part, dst_ref=part, send_sem=d2d_send.at[a, p], recv_sem=d2d_recv.at[a, p],
                device_id=sibling, device_id_type=MESH)

        sends = [ici(a, p, mine) for a in range(n) for p in range(3)]
        for cp in sends:
            cp.start()
        passed = []
        for a in range(n):
            for p, (px, py) in enumerate(chips):
                ici(a, p, 2 * px + py).wait_recv()
                if views[a][2]:
                    fwd = d2d(a, p, c)
                    fwd.start()
                    passed.append(fwd)
        for a in range(n):
            if views[a][2]:
                for p in range(3):
                    d2d(a, p, 1 - c).wait_recv()
        for cp in sends + passed:
            cp.wait_send()

    return pl.pallas_call(
        body, name=name, in_specs=[_ANY] * nf, out_specs=[_ANY] * nf,
        out_shape=[jax.ShapeDtypeStruct(f.shape, f.dtype) for f in fulls],
        input_output_aliases={i: i for i in range(nf)},
        scratch_shapes=[pltpu.SemaphoreType.DMA((n, 3))] * 4,
        compiler_params=pltpu.CompilerParams(has_side_effects=True),
    )(*fulls)


_HBM = pl.BlockSpec(memory_space=pltpu.HBM)
_SEM = pl.BlockSpec(memory_space=pltpu.SEMAPHORE)


def _in_hbm(arrays):
    return [pltpu.with_memory_space_constraint(a, pltpu.HBM) for a in arrays]


def _gather_start(name, fulls, views, after):
    nf = len(fulls)
    ng = 1 + max(g for _, _, g in views)

    def body(*refs):
        full = refs[nf + 1:2 * nf + 1]
        send_sems, recv_sems = refs[2 * nf + 1:2 * nf + 1 + ng], refs[2 * nf + 1 + ng:]
        x, y, c = _mesh_pos()
        chips = _other_chips(x, y)
        for i, view, g in views:
            part = view(full[i], 2 * x + y, c)
            for px, py in chips:
                pltpu.make_async_remote_copy(
                    src_ref=part, dst_ref=part, send_sem=send_sems[g], recv_sem=recv_sems[g],
                    device_id=(px, py, c), device_id_type=MESH).start()

    outs = pl.pallas_call(
        body, name=name, in_specs=[_HBM] * nf + [_ANY], out_specs=[_HBM] * nf + [_SEM] * (2 * ng),
        out_shape=[pltpu.HBM(f.shape, f.dtype) for f in fulls] + [pltpu.SemaphoreType.DMA(())] * (2 * ng),
        input_output_aliases={i: i for i in range(nf)},
        compiler_params=pltpu.CompilerParams(has_side_effects=pltpu.SideEffectType.DATAFLOW_SIDE_EFFECTING),
    )(*_in_hbm(fulls), after)
    return list(outs[:nf]), list(outs[nf:nf + ng]), list(outs[nf + ng:])


def _gather_wait(name, fulls, views, send_sem, recv_sem, after):
    nf = len(fulls)

    def body(*refs):
        send_ref, recv_ref = refs[nf], refs[nf + 1]
        full = refs[nf + 3:]
        x, y, c = _mesh_pos()
        copies = [pltpu.make_async_remote_copy(
            src_ref=view(full[i], 2 * x + y, c), dst_ref=view(full[i], 2 * px + py, c),
            send_sem=send_ref, recv_sem=recv_ref, device_id=(px, py, c), device_id_type=MESH)
            for i, view in views for px, py in _other_chips(x, y)]
        for cp in copies:
            cp.wait_send()
        for cp in copies:
            cp.wait_recv()

    outs = pl.pallas_call(
        body, name=name, in_specs=[_HBM] * nf + [_SEM, _SEM, _ANY], out_specs=[_HBM] * nf,
        out_shape=[pltpu.HBM(f.shape, f.dtype) for f in fulls],
        input_output_aliases={i: i for i in range(nf)},
        compiler_params=pltpu.CompilerParams(has_side_effects=pltpu.SideEffectType.DATAFLOW_SIDE_EFFECTING),
    )(*fulls, send_sem, recv_sem, after)
    return list(outs)


def _forward_to_sibling(name, fulls, views):
    n, nf = len(views), len(fulls)

    def body(*refs):
        full = refs[nf:2 * nf]
        send_sems, recv_sems = refs[2 * nf:]
        x, y, c = _mesh_pos()
        chips = _other_chips(x, y)

        def copy(a, p, h):
            i, view = views[a]
            px, py = chips[p]
            part = view(full[i], 2 * px + py, h)
            return pltpu.make_async_remote_copy(
                src_ref=part, dst_ref=part, send_sem=send_sems.at[a, p], recv_sem=recv_sems.at[a, p],
                device_id=(x, y, 1 - c), device_id_type=MESH)

        sends = [copy(a, p, c) for a in range(n) for p in range(3)]
        for cp in sends:
            cp.start()
        for a in range(n):
            for p in range(3):
                copy(a, p, 1 - c).wait_recv()
        for cp in sends:
            cp.wait_send()

    return pl.pallas_call(
        body, name=name, in_specs=[_ANY] * nf, out_specs=[_ANY] * nf,
        out_shape=[jax.ShapeDtypeStruct(f.shape, f.dtype) for f in fulls],
        input_output_aliases={i: i for i in range(nf)},
        scratch_shapes=[pltpu.SemaphoreType.DMA((n, 3))] * 2,
        compiler_params=pltpu.CompilerParams(has_side_effects=True),
    )(*fulls)


def _region_view(ref, kind, k, c):
    if kind == "lead":
        rh = ref.shape[1] // N_CORES
        return ref.at[k, pl.ds(pl.multiple_of(c * rh, 8), rh), :]
    rows, cols = ref.shape
    if kind == "cols":
        rh, cw = rows // N_CORES, cols // N_CHIPS
        return ref.at[pl.ds(pl.multiple_of(c * rh, 8), rh), pl.ds(k * cw, cw)]
    rh = rows // (N_CHIPS * N_CORES)
    return ref.at[pl.ds(pl.multiple_of((N_CORES * k + c) * rh, 8), rh), :]


def _send_to_sibling(name, grads, kinds):
    n = len(grads)
    shapes = [jax.ShapeDtypeStruct((N_CHIPS,) + _region_shape(g, kd), g.dtype) for g, kd in zip(grads, kinds)]

    def body(*refs):
        g_ref, land = refs[:n], refs[n:2 * n]
        send_sems, recv_sems = refs[2 * n:]
        x, y, c = _mesh_pos()
        copies = []
        for a in range(n):
            for k in range(N_CHIPS):
                cp = pltpu.make_async_remote_copy(
                    src_ref=_region_view(g_ref[a], kinds[a], k, 1 - c), dst_ref=land[a].at[k],
                    send_sem=send_sems.at[a, k], recv_sem=recv_sems.at[a, k],
                    device_id=(x, y, 1 - c), device_id_type=MESH)
                cp.start()
                copies.append(cp)
        for cp in copies:
            cp.wait_recv()
        for cp in copies:
            cp.wait_send()

    return pl.pallas_call(
        body, name=name, in_specs=[_ANY] * n, out_specs=[_ANY] * n, out_shape=shapes,
        scratch_shapes=[pltpu.SemaphoreType.DMA((n, N_CHIPS)), pltpu.SemaphoreType.DMA((n, N_CHIPS))],
        compiler_params=pltpu.CompilerParams(has_side_effects=True),
    )(*grads)


def _scatter_start(name, pair_sums):
    n = len(pair_sums)
    lands = [lax.empty((N_CHIPS - 1,) + p.shape[1:], p.dtype) for p in pair_sums]

    def body(*refs):
        p_ref, land = refs[2 * n:3 * n], refs[3 * n:4 * n]
        send_sem, recv_sem, token = refs[4 * n:]
        x, y, c = _mesh_pos()
        for a in range(n):
            for p, (px, py) in enumerate(_other_chips(x, y)):
                pltpu.make_async_remote_copy(
                    src_ref=p_ref[a].at[2 * px + py], dst_ref=land[a].at[p], send_sem=send_sem, recv_sem=recv_sem,
                    device_id=(px, py, c), device_id_type=MESH).start()
        token[...] = jnp.zeros_like(token)

    outs = pl.pallas_call(
        body, name=name, in_specs=[_HBM] * (2 * n),
        out_specs=[_HBM] * (2 * n) + [_SEM, _SEM, pl.BlockSpec(memory_space=pltpu.VMEM)],
        out_shape=[pltpu.HBM(t.shape, t.dtype) for t in list(pair_sums) + lands]
        + [pltpu.SemaphoreType.DMA(()), pltpu.SemaphoreType.DMA(()), jax.ShapeDtypeStruct((8, LANES), F32)],
        input_output_aliases={i: i for i in range(2 * n)},
        compiler_params=pltpu.CompilerParams(has_side_effects=pltpu.SideEffectType.DATAFLOW_SIDE_EFFECTING),
    )(*_in_hbm(list(pair_sums) + lands))
    return list(outs[:n]), list(outs[n:2 * n]), outs[2 * n], outs[2 * n + 1], outs[2 * n + 2]


def _scatter_wait(name, pair_sums, lands, send_sem, recv_sem, after):
    n = len(pair_sums)

    def body(*refs):
        send_ref, recv_ref = refs[2 * n], refs[2 * n + 1]
        p_ref, land = refs[2 * n + 3:3 * n + 3], refs[3 * n + 3:]
        x, y, c = _mesh_pos()
        copies = [pltpu.make_async_remote_copy(
            src_ref=p_ref[a].at[2 * px + py], dst_ref=land[a].at[p], send_sem=send_ref, recv_sem=recv_ref,
            device_id=(px, py, c), device_id_type=MESH)
            for a in range(n) for p, (px, py) in enumerate(_other_chips(x, y))]
        for cp in copies:
            cp.wait_send()
        for cp in copies:
            cp.wait_recv()

    outs = pl.pallas_call(
        body, name=name, in_specs=[_HBM] * (2 * n) + [_SEM, _SEM, _ANY], out_specs=[_HBM] * (2 * n),
        out_shape=[pltpu.HBM(t.shape, t.dtype) for t in list(pair_sums) + list(lands)],
        input_output_aliases={i: i for i in range(2 * n)},
        compiler_params=pltpu.CompilerParams(has_side_effects=pltpu.SideEffectType.DATAFLOW_SIDE_EFFECTING),
    )(*pair_sums, *lands, send_sem, recv_sem, after)
    return list(outs[:n]), list(outs[n:])


def _half_rows(ref, c):
    rh = ref.shape[0] // N_CORES
    return ref.at[pl.ds(pl.multiple_of(c * rh, 8), rh), :]


def _swap_start(name, shards):
    n = len(shards)

    def body(*refs):
        out = refs[n:2 * n]
        send_sem, recv_sem, token = refs[2 * n:]
        x, y, c = _mesh_pos()
        for a in range(n):
            mine = _half_rows(out[a], c)
            pltpu.make_async_remote_copy(
                src_ref=mine, dst_ref=mine, send_sem=send_sem, recv_sem=recv_sem,
                device_id=(x, y, 1 - c), device_id_type=MESH).start()
        token[...] = jnp.zeros_like(token)

    outs = pl.pallas_call(
        body, name=name, in_specs=[_HBM] * n,
        out_specs=[_HBM] * n + [_SEM, _SEM, pl.BlockSpec(memory_space=pltpu.VMEM)],
        out_shape=[pltpu.HBM(s.shape, s.dtype) for s in shards]
        + [pltpu.SemaphoreType.DMA(()), pltpu.SemaphoreType.DMA(()), jax.ShapeDtypeStruct((8, LANES), F32)],
        input_output_aliases={i: i for i in range(n)},
        compiler_params=pltpu.CompilerParams(has_side_effects=pltpu.SideEffectType.DATAFLOW_SIDE_EFFECTING),
    )(*_in_hbm(shards))
    return list(outs[:n]), outs[n], outs[n + 1], outs[n + 2]


def _swap_wait(name, shards, send_sem, recv_sem, after):
    n = len(shards)

    def body(*refs):
        send_ref, recv_ref = refs[n], refs[n + 1]
        out = refs[n + 3:]
        x, y, c = _mesh_pos()
        copies = [pltpu.make_async_remote_copy(
            src_ref=_half_rows(out[a], c), dst_ref=_half_rows(out[a], 1 - c), send_sem=send_ref, recv_sem=recv_ref,
            device_id=(x, y, 1 - c), device_id_type=MESH) for a in range(n)]
        for cp in copies:
            cp.wait_send()
        for cp in copies:
            cp.wait_recv()

    outs = pl.pallas_call(
        body, name=name, in_specs=[_HBM] * n + [_SEM, _SEM, _ANY], out_specs=[_HBM] * n,
        out_shape=[pltpu.HBM(s.shape, s.dtype) for s in shards],
        input_output_aliases={i: i for i in range(n)},
        compiler_params=pltpu.CompilerParams(has_side_effects=pltpu.SideEffectType.DATAFLOW_SIDE_EFFECTING),
    )(*shards, send_sem, recv_sem, after)
    return list(outs)


def _gather_all_devices(name, block):
    R, C = block.shape
    ndev = N_CHIPS * N_CORES

    def body(b_ref, out_ref, send_sems, recv_sems, local_sem):
        x, y, c = _mesh_pos()
        mine = 4 * x + 2 * y + c
        own = pltpu.make_async_copy(b_ref, out_ref.at[mine], local_sem)
        own.start()
        sends = []
        for mask in range(1, ndev):
            fx, fy, fc = (mask >> 2) & 1, (mask >> 1) & 1, mask & 1
            px, py, pc = x ^ fx, y ^ fy, c ^ fc
            cp = pltpu.make_async_remote_copy(
                src_ref=b_ref, dst_ref=out_ref.at[mine], send_sem=send_sems.at[mask - 1],
                recv_sem=recv_sems.at[mask - 1], device_id=(px, py, pc), device_id_type=MESH)
            cp.start()
            sends.append(cp)
        for mask in range(1, ndev):
            fx, fy, fc = (mask >> 2) & 1, (mask >> 1) & 1, mask & 1
            px, py, pc = x ^ fx, y ^ fy, c ^ fc
            pltpu.make_async_remote_copy(
                src_ref=b_ref, dst_ref=out_ref.at[4 * px + 2 * py + pc], send_sem=send_sems.at[mask - 1],
                recv_sem=recv_sems.at[mask - 1], device_id=(px, py, pc), device_id_type=MESH).wait_recv()
        for cp in sends:
            cp.wait_send()
        own.wait()

    return pl.pallas_call(
        body, name=name, in_specs=[_ANY], out_specs=_ANY,
        out_shape=jax.ShapeDtypeStruct((ndev, R, C), F32),
        scratch_shapes=[pltpu.SemaphoreType.DMA((ndev - 1,)), pltpu.SemaphoreType.DMA((ndev - 1,)),
                        pltpu.SemaphoreType.DMA(())],
        compiler_params=pltpu.CompilerParams(has_side_effects=True),
    )(block)


def kernel(x, norm_g, ffn1_w_gate, ffn1_w_up, ffn1_w_down, ffn2_w_gate, ffn2_w_up, ffn2_w_down, even_w_in, even_b_forget, even_w_out, odd_w_qkv, odd_w_out, final_norm_g, loss_target, m_norm_g, m_ffn1_w_gate, m_ffn1_w_up, m_ffn1_w_down, m_ffn2_w_gate, m_ffn2_w_up, m_ffn2_w_down, m_even_w_in, m_even_b_forget, m_even_w_out, m_odd_w_qkv, m_odd_w_out, m_final_norm_g, v_norm_g, v_ffn1_w_gate, v_ffn1_w_up, v_ffn1_w_down, v_ffn2_w_gate, v_ffn2_w_up, v_ffn2_w_down, v_even_w_in, v_even_b_forget, v_even_w_out, v_odd_w_qkv, v_odd_w_out, v_final_norm_g):
    _, S, D = x.shape
    L = norm_g.shape[0]
    assert L == 2 and even_w_in.shape[0] == 1 and odd_w_qkv.shape[0] == 1
    fs = ffn1_w_gate.shape[2]
    F = N_CHIPS * fs
    wc = even_w_in.shape[2]
    n_heads = D // HEAD_DIM
    n_fox = N_CHIPS * wc - 3 * D
    n_sb = n_heads - n_fox
    qs = odd_w_qkv.shape[2]
    os_ = even_w_out.shape[1]
    ns = norm_g.shape[2]
    xi, yi, ci = _mesh_pos()
    chip = 2 * xi + yi

    pos = jnp.stack([chip, ci]).astype(jnp.int32)
    kchip = pos[:1]
    lane = lambda start, size: pl.ds(pl.multiple_of(start, LANES), size)
    sub = lambda start, size: pl.ds(pl.multiple_of(start, 16), size)
    gate_view = lambda r, k, h: r.at[sub(h * (D // 2), D // 2), lane(k * 2 * fs, fs)]
    up_view = lambda r, k, h: r.at[sub(h * (D // 2), D // 2), lane(k * 2 * fs + fs, fs)]
    down_view = lambda r, k, h: r.at[sub(k * fs + h * (fs // 2), fs // 2), :]
    out_view = lambda r, k, h: r.at[sub(k * os_ + h * (os_ // 2), os_ // 2), :]
    tr_d = _row_tile(fs, 512, step=16)
    tr_o = _row_tile(os_, 512, step=16)
    ffn_w = {"ffn1": (ffn1_w_gate, ffn1_w_up, ffn1_w_down), "ffn2": (ffn2_w_gate, ffn2_w_up, ffn2_w_down)}
    win_view = lambda r, k, h: r.at[k, sub(h * (D // 2), D // 2), :]
    qkv_view = lambda r, k, h: r.at[sub(h * (D // 2), D // 2), lane(k * qs, qs)]
    norm_own = lax.dynamic_update_slice(jnp.zeros((L, 3, N_CHIPS * ns), F32), norm_g, (0, 0, chip * ns))
    (norm_full,) = _gather_over_chips("gather_norm", [norm_own], [(0, lambda r, k, h: r.at[:, :, lane(k * ns, ns)], False)])
    first = None
    fulls, views, groups = [], [], {}
    for l in range(L):
        for blk in ("ffn1", "mix", "ffn2"):
            tok = first[0][0] if first else None
            o, v0 = len(fulls), len(views)
            if blk == "mix" and l == 0:
                fulls += [_cast_into("cast_win", even_w_in, 0, kchip, (N_CHIPS, D, wc), lambda i, k: (k, i, 0), after=tok),
                          _cast_into("cast_wout_e", even_w_out, 0, kchip, (D, D), lambda i, k: (k * (os_ // tr_o) + i, 0),
                                     after=tok)]
                views += [(o, win_view), (o + 1, out_view)]
            elif blk == "mix":
                fulls += [_cast_into("cast_wqkv_o", odd_w_qkv, 0, kchip, (D, N_CHIPS * qs), lambda i, k: (i, k), after=tok),
                          _cast_into("cast_wout_o", odd_w_out, 0, kchip, (D, D), lambda i, k: (k * (os_ // tr_o) + i, 0),
                                     after=tok)]
                views += [(o, qkv_view), (o + 1, out_view)]
            else:
                wg, wu, wd = ffn_w[blk]
                t = f"cast_{blk}_l{l}"
                gu = _cast_into(t + "_gate", wg, l, kchip, (D, 2 * F), lambda i, k: (i, 2 * k), after=tok)
                gu = _cast_into(t + "_up", wu, l, kchip, (D, 2 * F), lambda i, k: (i, 2 * k + 1), full=gu)
                if first is None:
                    first = _gather_start("gather_start_first", [gu], [(0, gate_view, 0), (0, up_view, 0)], norm_full)
                    tok = first[0][0]
                    dn = _cast_into(t + "_down", wd, l, kchip, (F, D), lambda i, k: (k * (fs // tr_d) + i, 0), after=tok)
                    fulls += [dn]
                    views += [(o, down_view)]
                else:
                    dn = _cast_into(t + "_down", wd, l, kchip, (F, D), lambda i, k: (k * (fs // tr_d) + i, 0))
                    fulls += [gu, dn]
                    views += [(o, gate_view), (o, up_view), (o + 1, down_view)]
            gid = len(groups)
            views[v0:] = [(i, view, gid) for i, view in views[v0:]]
            groups[(blk, l)] = (gid, list(range(o, len(fulls))), list(range(v0, len(views))))
    started, send_sems, recv_sems = _gather_start("gather_start", fulls, views, first[0][0])

    def arrive(tag, arrays, local, ssem, rsem, after):
        got = _gather_wait("gather_wait_" + tag, arrays, local, ssem, rsem, after)
        return _forward_to_sibling("gather_pass_" + tag, got, local)

    def fetch(block, after):
        if block == "norm_g":
            return norm_full
        if block == "final_g":
            return final_norm_g[None, :]
        gid, arrays, rows = groups[block]
        tag = f"{block[0]}_l{block[1]}"
        local = [(views[a][0] - arrays[0], views[a][1]) for a in rows]
        rest = lambda a: arrive(tag, [started[i] for i in arrays], local, send_sems[gid], recv_sems[gid], a)
        if block == ("ffn1", 0):
            (wgu,) = arrive(tag + "_gu", first[0], [(0, gate_view), (0, up_view)], first[1][0], first[2][0], after)
            return wgu, lambda a: rest(a)[0]
        got = rest(after)
        if block[0] != "mix":
            return got
        if block[1] == 1:
            return {"wqkv_o": got[0], "wout_o": got[1]}
        win = jnp.concatenate([got[0][k] for k in range(N_CHIPS)], axis=1)
        return {"wqkv_e": win[:, :3 * D], "wf": jnp.pad(win[:, 3 * D:], ((0, 0), (0, LANES - n_fox))),
                "bf": jnp.pad(even_b_forget, ((0, 0), (0, LANES - n_fox))), "wout_e": got[1]}

    pending, swaps = [], []

    def finish(after):
        block, pair, lands, ssem, rsem = pending.pop(0)
        tag = f"{block[0]}_l{block[1]}"
        pair, lands = _scatter_wait("rs_chip_wait_" + tag, pair, lands, ssem, rsem, after)
        sums = [_chip_sum(f"rs_chip_add_{tag}_{a}", p, ld, pos) for a, (p, ld) in enumerate(zip(pair, lands))]
        sums, ssem, rsem, token = _swap_start("rs_swap_start_" + tag, sums)
        swaps.append((block, sums, ssem, rsem))
        return token

    def emit(block, mats):
        blk, l = block
        tag = f"{blk}_l{l}"
        kinds = ["cols", "rows"]
        if blk == "mix" and l == 0:
            dwqkv, dwout, dwf = mats
            dwin = jnp.concatenate([dwqkv, dwf[:, :n_fox]], axis=1)
            mats = [jnp.stack([dwin[:, k * wc:(k + 1) * wc] for k in range(N_CHIPS)]), dwout]
            kinds = ["lead", "rows"]
        elif blk == "mix":
            mats = list(mats[:2])
        else:
            mats = list(mats)
        landed = _send_to_sibling("rs_pair_send_" + tag, mats, kinds)
        pair = [_region_add(f"rs_pair_add_{tag}_{a}", m, kd, ld, pos[1:])
                for a, (m, kd, ld) in enumerate(zip(mats, kinds, landed))]
        pair, lands, ssem, rsem, token = _scatter_start("rs_chip_start_" + tag, pair)
        if pending:
            token = finish(token)
        pending.append((block, pair, lands, ssem, rsem))
        return token

    loss_vec, grad_x, g = _local_step(x[0], loss_target[0], fetch, fs, n_heads, n_sb, emit=emit)
    token = finish(grad_x)
    red = {block: _swap_wait(f"rs_swap_wait_{block[0]}_l{block[1]}", sums, ssem, rsem, token)
           for block, sums, ssem, rsem in swaps}
    gu1, gd1 = [red[("ffn1", l)][0] for l in range(L)], [red[("ffn1", l)][1] for l in range(L)]
    gu2, gd2 = [red[("ffn2", l)][0] for l in range(L)], [red[("ffn2", l)][1] for l in range(L)]
    (g_win, g_wout_e), (g_qkv_o, g_wout_o) = red[("mix", 0)], red[("mix", 1)]

    small_rows = [g["dnorm"][l][i] for l in range(L) for i in range(3)] + [
        g["dfinal"], jnp.pad(g["db"], ((0, 0), (0, D - LANES))), jnp.pad(loss_vec, ((0, 0), (0, D - LANES)))]
    small = jnp.concatenate(small_rows + [jnp.zeros((16 - len(small_rows), D), F32)], axis=0)
    small_sum = _sum_leading("small_sum", _gather_all_devices("small_gather", small))
    loss = small_sum[3 * L + 2, 0]
    g_norm = lax.dynamic_slice_in_dim(small_sum[:3 * L].reshape(L, 3, D), chip * ns, ns, axis=2)
    g_final = small_sum[3 * L]
    g_bf = small_sum[3 * L + 1, :n_fox][None, :]

    grads = [
        g_norm,
        jnp.stack([t[:, :fs] for t in gu1]), jnp.stack([t[:, fs:] for t in gu1]), jnp.stack(gd1),
        jnp.stack([t[:, :fs] for t in gu2]), jnp.stack([t[:, fs:] for t in gu2]), jnp.stack(gd2),
        g_win[None], g_bf, g_wout_e[None], g_qkv_o[None], g_wout_o[None], g_final]
    weights = [norm_g, ffn1_w_gate, ffn1_w_up, ffn1_w_down, ffn2_w_gate, ffn2_w_up, ffn2_w_down,
               even_w_in, even_b_forget, even_w_out, odd_w_qkv, odd_w_out, final_norm_g]
    ms = [m_norm_g, m_ffn1_w_gate, m_ffn1_w_up, m_ffn1_w_down, m_ffn2_w_gate, m_ffn2_w_up, m_ffn2_w_down,
          m_even_w_in, m_even_b_forget, m_even_w_out, m_odd_w_qkv, m_odd_w_out, m_final_norm_g]
    vs = [v_norm_g, v_ffn1_w_gate, v_ffn1_w_up, v_ffn1_w_down, v_ffn2_w_gate, v_ffn2_w_up, v_ffn2_w_down,
          v_even_w_in, v_even_b_forget, v_even_w_out, v_odd_w_qkv, v_odd_w_out, v_final_norm_g]
    deltas, new_ms, new_vs = [], [], []
    for i, (wt, gt, mt, vt) in enumerate(zip(weights, grads, ms, vs)):
        d, nm, nv = _adamw(f"adamw_{i}", wt, gt, mt, vt)
        deltas.append(d)
        new_ms.append(nm)
        new_vs.append(nv)
    return (loss, grad_x[None], *grads, *deltas, *new_ms, *new_vs)
```

```python
import math

import jax
import jax.numpy as jnp
from jax import lax
from jax.experimental import pallas as pl
from jax.experimental.pallas import tpu as pltpu

F32 = jnp.float32
BF16 = jnp.bfloat16

HEAD_DIM = 128
ROPE_DIMS = 32
ROPE_THETA = 500000.0
DILATED_PATTERNS = ((128, 1), (512, 4), (2048, 16))
RMS_EPS = 1e-6
NEG_INF = -1e30
ADAM_LR = 0.001
ADAM_B1 = 0.9
ADAM_B2 = 0.999
ADAM_EPS = 1e-08
ADAM_WD = 0.01
ADAM_STEP = 10

N_CHIPS = 4
N_CORES = 2
LANES = 128
BLK = 256
VMEM_BYTES_V7X = 64 * 2**20
MESH = pl.DeviceIdType.MESH


def _vmem_limit(block_bytes, scratch_bytes=0):
    need = 2 * block_bytes + scratch_bytes + 12 * 2**20
    return int(min(need, VMEM_BYTES_V7X - 6 * 2**20))


def _nbytes(shape, dtype):
    return math.prod(shape) * jnp.dtype(dtype).itemsize


def _tile(dim, target):
    best = None
    for t in range(LANES, min(dim, target) + 1, LANES):
        if dim % t == 0:
            best = t
    assert best is not None, (dim, target)
    return best


def _row_tile(rows, target, step=8):
    if rows <= target:
        return rows
    best = None
    for t in range(step, target + 1, step):
        if rows % t == 0:
            best = t
    assert best is not None, (rows, target)
    return best


def _mm(name, a, b, mode, out_dtype, res=None, alpha=1.0, after=None, tm_target=1024, tn_target=1536, tk_target=2048):
    a3 = a.ndim == 3
    b3 = b.ndim == 3
    if mode == "nn":
        assert not a3 and not b3
        (M, K), (K2, N) = a.shape, b.shape
    elif mode == "nt":
        assert not b3
        if a3:
            P, M, Kp = a.shape
            K = P * Kp
        else:
            M, K = a.shape
        N, K2 = b.shape
    else:
        assert mode == "tn" and not a3
        K, M = a.shape
        if b3:
            P, K2, Np = b.shape
            N = P * Np
        else:
            K2, N = b.shape
    assert K == K2, (name, a.shape, b.shape)
    tm = _tile(M, tm_target)
    tn = _tile(Np if b3 else N, tn_target)
    tk = _tile(Kp if a3 else K, tk_target)
    nk = K // tk
    grid = (M // tm, N // tn, nk)

    if mode == "nn":
        a_spec = pl.BlockSpec((tm, tk), lambda i, j, k: (i, k))
        b_spec = pl.BlockSpec((tk, tn), lambda i, j, k: (k, j))
        dims = (((1,), (0,)), ((), ()))
    elif mode == "nt":
        if a3:
            nkp = Kp // tk
            a_spec = pl.BlockSpec((None, tm, tk), lambda i, j, k: (k // nkp, i, k % nkp))
        else:
            a_spec = pl.BlockSpec((tm, tk), lambda i, j, k: (i, k))
        b_spec = pl.BlockSpec((tn, tk), lambda i, j, k: (j, k))
        dims = (((1,), (1,)), ((), ()))
    else:
        a_spec = pl.BlockSpec((tk, tm), lambda i, j, k: (k, i))
        if b3:
            njp = Np // tn
            b_spec = pl.BlockSpec((None, tk, tn), lambda i, j, k: (j // njp, k, j % njp))
        else:
            b_spec = pl.BlockSpec((tk, tn), lambda i, j, k: (k, j))
        dims = (((0,), (0,)), ((), ()))
    o_spec = pl.BlockSpec((tm, tn), lambda i, j, k: (i, j))
    has_res = res is not None

    def finish(y, r_ref, o_ref):
        if alpha != 1.0:
            y = y * alpha
        if has_res:
            y = r_ref[...] + y
        o_ref[...] = y.astype(o_ref.dtype)

    n_in = 2 + has_res + (after is not None)

    def body(*refs):
        a_ref, b_ref = refs[:2]
        r_ref = refs[2] if has_res else None
        o_ref = refs[n_in]
        part = lax.dot_general(a_ref[...], b_ref[...], dims, preferred_element_type=F32)
        if nk == 1:
            finish(part, r_ref, o_ref)
            return
        acc_ref = refs[-1]
        k = pl.program_id(2)

        @pl.when(k == 0)
        def _():
            acc_ref[...] = part

        @pl.when(k > 0)
        def _():
            acc_ref[...] += part

        @pl.when(k == nk - 1)
        def _():
            finish(acc_ref[...], r_ref, o_ref)

    in_specs = [a_spec, b_spec] + ([o_spec] if has_res else []) + ([_ANY] if after is not None else [])
    args = (a, b) + ((res,) if has_res else ()) + ((after,) if after is not None else ())
    blk = (_nbytes((tm, tk), a.dtype) + _nbytes((tk, tn), b.dtype) + _nbytes((tm, tn), out_dtype)
           + (_nbytes((tm, tn), F32) if has_res else 0))
    return pl.pallas_call(
        body, name=name, grid=grid, in_specs=in_specs, out_specs=o_spec,
        out_shape=jax.ShapeDtypeStruct((M, N), out_dtype),
        scratch_shapes=[pltpu.VMEM((tm, tn), F32)] if nk > 1 else [],
        compiler_params=pltpu.CompilerParams(
            dimension_semantics=("parallel", "parallel", "arbitrary"),
            vmem_limit_bytes=_vmem_limit(blk, 2 * _nbytes((tm, tn), F32))),
    )(*args)


def _rms_fwd(name, x, g):
    S, D = x.shape
    tr = _row_tile(S, 256)

    def body(x_ref, g_ref, n_ref):
        xv = x_ref[...]
        r = lax.rsqrt(jnp.mean(xv * xv, axis=-1, keepdims=True) + RMS_EPS)
        n_ref[...] = (xv * r * g_ref[...]).astype(BF16)

    return pl.pallas_call(
        body, name=name, grid=(S // tr,),
        in_specs=[pl.BlockSpec((tr, D), lambda i: (i, 0)), pl.BlockSpec((1, D), lambda i: (0, 0))],
        out_specs=pl.BlockSpec((tr, D), lambda i: (i, 0)),
        out_shape=jax.ShapeDtypeStruct((S, D), BF16),
        compiler_params=pltpu.CompilerParams(dimension_semantics=("parallel",)),
    )(x, g)


def _rms_bwd(name, dn, x, g, dres):
    S, D = x.shape
    tr = _row_tile(S, 256)

    def body(dn_ref, x_ref, g_ref, dres_ref, dx_ref, dxb_ref, dg_ref):
        i = pl.program_id(0)
        xv = x_ref[...]
        dnv = dn_ref[...]
        r = lax.rsqrt(jnp.mean(xv * xv, axis=-1, keepdims=True) + RMS_EPS)
        u = dnv * g_ref[...]
        dot = jnp.mean(u * xv, axis=-1, keepdims=True)
        dx = dres_ref[...] + r * u - xv * (r * r * r * dot)
        dx_ref[...] = dx
        dxb_ref[...] = dx.astype(BF16)

        @pl.when(i == 0)
        def _():
            dg_ref[...] = jnp.zeros_like(dg_ref)

        dg_ref[...] += jnp.sum(dnv * xv * r, axis=0, keepdims=True)

    row = pl.BlockSpec((tr, D), lambda i: (i, 0))
    vec = pl.BlockSpec((1, D), lambda i: (0, 0))
    return pl.pallas_call(
        body, name=name, grid=(S // tr,),
        in_specs=[row, row, vec, row], out_specs=[row, row, vec],
        out_shape=[jax.ShapeDtypeStruct((S, D), F32), jax.ShapeDtypeStruct((S, D), BF16),
                   jax.ShapeDtypeStruct((1, D), F32)],
        compiler_params=pltpu.CompilerParams(dimension_semantics=("arbitrary",)),
    )(dn, x, g, dres)


def _loss_head(name, x, g, target):
    S, D = x.shape
    tr = _row_tile(S, 256)

    def body(x_ref, g_ref, t_ref, dx_ref, dxb_ref, dg_ref, loss_ref):
        i = pl.program_id(0)
        xv = x_ref[...]
        gv = g_ref[...]
        r = lax.rsqrt(jnp.mean(xv * xv, axis=-1, keepdims=True) + RMS_EPS)
        diff = xv * r * gv - t_ref[...]
        part = 0.5 * jnp.sum(jnp.mean(diff * diff, axis=-1, keepdims=True), axis=0, keepdims=True)
        dy = diff * (1.0 / D)
        u = dy * gv
        dot = jnp.mean(u * xv, axis=-1, keepdims=True)
        dx = r * u - xv * (r * r * r * dot)
        dx_ref[...] = dx
        dxb_ref[...] = dx.astype(BF16)

        @pl.when(i == 0)
        def _():
            dg_ref[...] = jnp.zeros_like(dg_ref)
            loss_ref[...] = jnp.zeros_like(loss_ref)

        dg_ref[...] += jnp.sum(dy * xv * r, axis=0, keepdims=True)
        loss_ref[...] += jnp.broadcast_to(part, loss_ref.shape)

    row = pl.BlockSpec((tr, D), lambda i: (i, 0))
    vec = pl.BlockSpec((1, D), lambda i: (0, 0))
    lvec = pl.BlockSpec((1, LANES), lambda i: (0, 0))
    return pl.pallas_call(
        body, name=name, grid=(S // tr,),
        in_specs=[row, vec, row], out_specs=[row, row, vec, lvec],
        out_shape=[jax.ShapeDtypeStruct((S, D), F32), jax.ShapeDtypeStruct((S, D), BF16),
                   jax.ShapeDtypeStruct((1, D), F32), jax.ShapeDtypeStruct((1, LANES), F32)],
        compiler_params=pltpu.CompilerParams(dimension_semantics=("arbitrary",)),
    )(x, g, target)


def _ffn_up(name, n, wgu, fs, tm_target=512):
    S, D = n.shape
    nslab = wgu.shape[1] // (2 * fs)
    tm = _tile(S, tm_target)

    def body(n_ref, w_ref, gu_ref, h_ref):
        y = jnp.dot(n_ref[...], w_ref[...], preferred_element_type=F32)
        gu_ref[...] = y
        gv = y[:, :fs]
        h_ref[...] = (gv * jax.nn.sigmoid(gv) * y[:, fs:]).astype(BF16)

    blk = _nbytes((tm, D), BF16) + _nbytes((D, 2 * fs), BF16) + _nbytes((tm, 2 * fs), F32) + _nbytes((tm, fs), BF16)
    return pl.pallas_call(
        body, name=name, grid=(nslab, S // tm),
        in_specs=[pl.BlockSpec((tm, D), lambda k, i: (i, 0)), pl.BlockSpec((D, 2 * fs), lambda k, i: (0, k))],
        out_specs=[pl.BlockSpec((tm, 2 * fs), lambda k, i: (i, k)), pl.BlockSpec((tm, fs), lambda k, i: (i, k))],
        out_shape=[jax.ShapeDtypeStruct((S, nslab * 2 * fs), F32), jax.ShapeDtypeStruct((S, nslab * fs), BF16)],
        compiler_params=pltpu.CompilerParams(dimension_semantics=("parallel", "parallel"),
                                             vmem_limit_bytes=_vmem_limit(blk, _nbytes((tm, 2 * fs), F32))),
    )(n, wgu)


def _ffn_dact(name, dyb, wd, gu, fs, alpha, after=None, tm_target=512):
    S, D = dyb.shape
    nslab = wd.shape[0] // fs
    tm = _tile(S, tm_target)

    def body(*refs):
        d_ref, w_ref, gu_ref = refs[:3]
        o_ref = refs[-1]
        dhv = _dot_nt(d_ref[...], w_ref[...]) * alpha
        gv = gu_ref[:, :fs]
        uv = gu_ref[:, fs:]
        sg = jax.nn.sigmoid(gv)
        silu = gv * sg
        o_ref[:, :fs] = (dhv * uv * (sg + silu * (1.0 - sg))).astype(BF16)
        o_ref[:, fs:] = (dhv * silu).astype(BF16)

    in_specs = [pl.BlockSpec((tm, D), lambda k, i: (i, 0)), pl.BlockSpec((fs, D), lambda k, i: (k, 0)),
                pl.BlockSpec((tm, 2 * fs), lambda k, i: (i, k))] + ([_ANY] if after is not None else [])
    args = (dyb, wd, gu) + ((after,) if after is not None else ())
    blk = _nbytes((tm, D), BF16) + _nbytes((fs, D), BF16) + _nbytes((tm, 2 * fs), F32) + _nbytes((tm, 2 * fs), BF16)
    return pl.pallas_call(
        body, name=name, grid=(nslab, S // tm), in_specs=in_specs,
        out_specs=pl.BlockSpec((tm, 2 * fs), lambda k, i: (i, k)),
        out_shape=jax.ShapeDtypeStruct((S, nslab * 2 * fs), BF16),
        compiler_params=pltpu.CompilerParams(dimension_semantics=("parallel", "parallel"),
                                             vmem_limit_bytes=_vmem_limit(blk, 2 * _nbytes((tm, fs), F32))),
    )(*args)


def _tri_rows(r0, nrows, ncols, lower):
    row = lax.broadcasted_iota(jnp.int32, (nrows, ncols), 0) + r0
    col = lax.broadcasted_iota(jnp.int32, (nrows, ncols), 1)
    return jnp.where((col <= row) if lower else (col >= row), 1.0, 0.0).astype(F32)


def _gate_fwd(name, hf, b):
    S = hf.shape[0]
    tb = _row_tile(S, 256)

    def body(hf_ref, b_ref, cf_ref, cft_ref, lf_ref):
        zz = hf_ref[...] + b_ref[...]
        lf_ref[...] = jnp.minimum(zz, 0.0) - jnp.log1p(jnp.exp(-jnp.abs(zz)))

        def blk(i, c):
            r0 = pl.multiple_of(i * tb, tb)
            tri = _tri_rows(r0, tb, S, True)
            cf_ref[pl.ds(r0, tb), :] = jnp.dot(tri, lf_ref[...], precision=lax.Precision.HIGHEST,
                                               preferred_element_type=F32)
            return c

        lax.fori_loop(0, S // tb, blk, 0)
        cft_ref[...] = cf_ref[...].T

    full = pl.BlockSpec((S, LANES), lambda: (0, 0))
    return pl.pallas_call(
        body, name=name, in_specs=[full, pl.BlockSpec((1, LANES), lambda: (0, 0))],
        out_specs=[full, pl.BlockSpec((LANES, S), lambda: (0, 0))],
        out_shape=[jax.ShapeDtypeStruct((S, LANES), F32), jax.ShapeDtypeStruct((LANES, S), F32)],
        scratch_shapes=[pltpu.VMEM((S, LANES), F32)],
    )(hf, b)


def _gate_bwd(name, dcft, drow, hf, b):
    S = hf.shape[0]
    tb = _row_tile(S, 256)

    def body(dcft_ref, drow_ref, hf_ref, b_ref, dhf_ref, db_ref, dcf_ref, dlf_ref):
        dcf_ref[...] = dcft_ref[...].T + drow_ref[...]

        def blk(i, c):
            r0 = pl.multiple_of(i * tb, tb)
            tri = _tri_rows(r0, tb, S, False)
            dlf_ref[pl.ds(r0, tb), :] = jnp.dot(tri, dcf_ref[...], precision=lax.Precision.HIGHEST,
                                                preferred_element_type=F32)
            return c

        lax.fori_loop(0, S // tb, blk, 0)
        zz = hf_ref[...] + b_ref[...]
        dhf = dlf_ref[...] * jax.nn.sigmoid(-zz)
        dhf_ref[...] = dhf.astype(BF16)
        db_ref[...] = jnp.sum(dhf, axis=0, keepdims=True)

    full = pl.BlockSpec((S, LANES), lambda: (0, 0))
    vec = pl.BlockSpec((1, LANES), lambda: (0, 0))
    return pl.pallas_call(
        body, name=name, in_specs=[pl.BlockSpec((LANES, S), lambda: (0, 0)), full, full, vec],
        out_specs=[full, vec],
        out_shape=[jax.ShapeDtypeStruct((S, LANES), BF16), jax.ShapeDtypeStruct((1, LANES), F32)],
        scratch_shapes=[pltpu.VMEM((S, LANES), F32), pltpu.VMEM((S, LANES), F32)],
    )(dcft, drow, hf, b)


def _rope_tables(S):
    half = ROPE_DIMS // 2
    freqs = ROPE_THETA ** (-jnp.arange(half, dtype=F32) / half)
    ang = jnp.arange(S, dtype=F32)[:, None] * freqs[None, :]
    cos, sin = jnp.cos(ang), jnp.sin(ang)
    pad = HEAD_DIM - ROPE_DIMS
    c = jnp.concatenate([cos, cos, jnp.ones((S, pad), F32)], axis=1)
    s = jnp.concatenate([-sin, sin, jnp.zeros((S, pad), F32)], axis=1)
    return c, s


def _rope_swap(x):
    half = ROPE_DIMS // 2
    lane = lax.broadcasted_iota(jnp.int32, x.shape, 1)
    upper = jnp.where(lane < ROPE_DIMS, pltpu.roll(x, half, 1), 0.0)
    return jnp.where(lane < half, pltpu.roll(x, HEAD_DIM - half, 1), upper)


def _rope(x, c, s):
    return x * c + _rope_swap(x) * s


def _rope_t(dy, c, s):
    return dy * c + _rope_swap(dy * s)


def _split_dot(x, t):
    hi = x.astype(BF16)
    lo = (x - hi.astype(F32)).astype(BF16)
    return (jnp.dot(hi, t, preferred_element_type=F32) + jnp.dot(lo, t, preferred_element_type=F32))


_NT = (((1,), (1,)), ((), ()))
_TN = (((0,), (0,)), ((), ()))


def _dot_nt(a, b):
    return lax.dot_general(a, b, _NT, preferred_element_type=F32)


def _dot_tn(a, b):
    return lax.dot_general(a, b, _TN, preferred_element_type=F32)


def _blk(i):
    return pl.ds(pl.multiple_of(i * BLK, BLK), BLK)


def _dilated_mult(delta):
    c = jnp.zeros(delta.shape, F32)
    for window, dil in DILATED_PATTERNS:
        ok = (delta >= 0) & (delta <= window) & ((delta & (dil - 1)) == 0)
        c = c + jnp.where(ok, 1.0, 0.0)
    return c


def _query_block(S):
    return min(512, S)


def _offsets(d, bq):
    row = jnp.arange(bq, dtype=jnp.int32)[:, None]
    col = jnp.arange(BLK, dtype=jnp.int32)[None, :]
    return d * BLK + row - col


def _causal_tables(bq, strict):
    r = bq // BLK
    tabs = []
    for d in range(-(r - 1), 1):
        delta = _offsets(d, bq)
        tabs.append(jnp.where((delta > 0) if strict else (delta >= 0), 1.0, 0.0))
    tabs.append(jnp.ones((bq, BLK), F32))
    return jnp.stack(tabs).astype(F32)


def _dilated_tables(bq):
    r = bq // BLK
    limit = sorted(w for w, _ in DILATED_PATTERNS)[-2]
    assert all(BLK % dil == 0 for _, dil in DILATED_PATTERNS)
    d_far = -(-(limit + BLK) // BLK)
    tabs = []
    for d in range(-(r - 1), d_far + 1):
        mult = _dilated_mult(_offsets(d, bq))
        tabs.append(jnp.where(mult > 0, jnp.log(jnp.maximum(mult, 1.0)), NEG_INF))
    return jnp.stack(tabs).astype(F32)


def _qblk(i, bq):
    return pl.ds(pl.multiple_of(i * bq, bq), bq)


def _sb_block(z, valid, t_ex, run):
    t = jnp.log1p(jnp.exp(-jnp.abs(z)))
    lsig = jnp.minimum(z, 0.0) - t
    m = -(jnp.maximum(z, 0.0) + t) * valid
    after = _split_dot(m, t_ex)
    a = jnp.exp(lsig + after + run) * valid
    return a, m, lsig


def _attn_fwd_wide(name, hq, layer_kind, n_heads, n_sb, cf=None, cft=None, rope_c=None, rope_s=None):
    S = hq.shape[0]
    D = n_heads * HEAD_DIM
    bq = _query_block(S)
    r = bq // BLK
    nq = S // bq
    scale = HEAD_DIM ** -0.5
    even = layer_kind == "even"
    if even:
        tabs = (jnp.where(_causal_tables(bq, False) > 0, 0.0, NEG_INF), _causal_tables(bq, True))
    else:
        tabs = (_dilated_tables(bq),)
    n_tab = tabs[0].shape[0]

    def body(*refs):
        if even:
            q_ref, k_ref, v_ref, cf_ref, cft_ref, bias_ref, valid_ref, o_ref, ob_ref, lse_ref, qs, ks, vs = refs
        else:
            q_ref, k_ref, v_ref, c_ref, s_ref, bias_ref, o_ref, ob_ref, lse_ref, qs, ks, vs = refs
        h = pl.program_id(0)
        if even:
            qs[...] = q_ref[...].astype(BF16)
            ks[...] = k_ref[...].astype(BF16)
        else:
            qs[...] = _rope(q_ref[...], c_ref[...], s_ref[...]).astype(BF16)
            ks[...] = _rope(k_ref[...], c_ref[...], s_ref[...]).astype(BF16)
        vs[...] = v_ref[...].astype(BF16)

        def softmax_head(hh):
            def qblock(i, carry):
                qi = qs[_qblk(i, bq), :]
                if even:
                    lane = lax.broadcasted_iota(jnp.int32, (bq, LANES), 1)
                    cfq = jnp.sum(jnp.where(lane == hh, cf_ref[_qblk(i, bq), :], 0.0), axis=1, keepdims=True)

                def kblock(j, c):
                    m_run, l_run, acc = c
                    z = _dot_nt(qi, ks[_blk(j), :]) * scale + bias_ref[jnp.minimum(r * i - j + (r - 1), n_tab - 1)]
                    if even:
                        z = z + (cfq - cft_ref[hh, :, _blk(j)])
                    m_new = jnp.maximum(m_run, jnp.max(z, axis=1, keepdims=True))
                    p = jnp.exp(z - m_new)
                    alpha = jnp.exp(m_run - m_new)
                    l_new = alpha * l_run + jnp.sum(p, axis=1, keepdims=True)
                    acc = alpha * acc + jnp.dot(p.astype(BF16), vs[_blk(j), :], preferred_element_type=F32)
                    return m_new, l_new, acc

                init = (jnp.full((bq, 1), NEG_INF, F32), jnp.zeros((bq, 1), F32), jnp.zeros((bq, HEAD_DIM), F32))
                m_run, l_run, acc = lax.fori_loop(0, r * (i + 1), kblock, init)
                o = acc / l_run
                o_ref[_qblk(i, bq), :] = o
                ob_ref[_qblk(i, bq), :] = o.astype(BF16)
                lse_ref[_qblk(i, bq), :] = jnp.broadcast_to(m_run + jnp.log(l_run), (bq, HEAD_DIM))
                return carry

            lax.fori_loop(0, nq, qblock, 0)

        def sb_head():
            row = lax.broadcasted_iota(jnp.int32, (BLK, BLK), 0)
            col = lax.broadcasted_iota(jnp.int32, (BLK, BLK), 1)
            t_ex = jnp.where(row > col, 1.0, 0.0).astype(BF16)

            def qblock(i, carry):
                qi = qs[_qblk(i, bq), :]

                def kblock(jj, c):
                    run, acc, rest = c
                    j = r * (i + 1) - 1 - jj
                    z = _dot_nt(qi, ks[_blk(j), :]) * scale
                    a, m, _ = _sb_block(z, valid_ref[jnp.minimum(r * i - j + (r - 1), r)], t_ex, run)
                    vj = vs[_blk(j), :]
                    hi = a.astype(BF16)
                    lo = (a - hi.astype(F32)).astype(BF16)
                    acc = acc + jnp.dot(hi, vj, preferred_element_type=F32)
                    rest = rest + jnp.dot(lo, vj, preferred_element_type=F32)
                    return run + jnp.sum(m, axis=1, keepdims=True), acc, rest

                zero = jnp.zeros((bq, HEAD_DIM), F32)
                _, acc, rest = lax.fori_loop(0, r * (i + 1), kblock, (jnp.zeros((bq, 1), F32), zero, zero))
                o_ref[_qblk(i, bq), :] = acc + rest
                ob_ref[_qblk(i, bq), :] = acc.astype(BF16)
                lse_ref[_qblk(i, bq), :] = jnp.zeros((bq, HEAD_DIM), F32)
                return carry

            lax.fori_loop(0, nq, qblock, 0)

        if even:
            @pl.when(h < n_sb)
            def _():
                sb_head()

            @pl.when(h >= n_sb)
            def _():
                softmax_head(h - n_sb)
        else:
            softmax_head(h)

    head = lambda off: pl.BlockSpec((S, HEAD_DIM), lambda h, off=off: (0, off + h))
    full = pl.BlockSpec((S, LANES), lambda h: (0, 0))
    tab_specs = [pl.BlockSpec(t.shape, lambda h: (0, 0, 0)) for t in tabs]
    if even:
        extra_specs = [full, pl.BlockSpec(cft.shape, lambda h: (0, 0, 0))] + tab_specs
        extra = (cf, cft) + tabs
    else:
        extra_specs = [full, full] + tab_specs
        extra = (rope_c, rope_s) + tabs
    blk_bytes = 8 * _nbytes((S, HEAD_DIM), F32) + sum(_nbytes(t.shape, F32) for t in tabs)
    return pl.pallas_call(
        body, name=name, grid=(n_heads,),
        in_specs=[head(0), head(n_heads), head(2 * n_heads)] + extra_specs,
        out_specs=[head(0), head(0), head(0)],
        out_shape=[jax.ShapeDtypeStruct((S, D), F32), jax.ShapeDtypeStruct((S, D), BF16),
                   jax.ShapeDtypeStruct((S, D), F32)],
        scratch_shapes=[pltpu.VMEM((S, HEAD_DIM), BF16)] * 3,
        compiler_params=pltpu.CompilerParams(dimension_semantics=("arbitrary",),
                                             vmem_limit_bytes=_vmem_limit(blk_bytes, 3 * _nbytes((S, HEAD_DIM), BF16))),
    )(hq, hq, hq, *extra)


def _attn_bwd_wide(name, hq, do, o, lse, layer_kind, n_heads, n_sb, cf=None, cft=None, rope_c=None, rope_s=None):
    S = hq.shape[0]
    D = n_heads * HEAD_DIM
    bq = _query_block(S)
    r = bq // BLK
    nq = S // bq
    scale = HEAD_DIM ** -0.5
    even = layer_kind == "even"
    if even:
        tabs = (jnp.where(_causal_tables(bq, False) > 0, 0.0, NEG_INF), _causal_tables(bq, True))
    else:
        tabs = (_dilated_tables(bq),)
    n_tab = tabs[0].shape[0]

    def body(*refs):
        if even:
            (q_ref, k_ref, v_ref, do_ref, o_ref, lse_ref, cf_ref, cft_ref, bias_ref, valid_ref,
             dh_ref, dcft_ref, drow_ref, qs, ks, vs, dos, dq_acc, dk_acc, dv_acc) = refs
        else:
            (q_ref, k_ref, v_ref, do_ref, o_ref, lse_ref, c_ref, s_ref, bias_ref,
             dh_ref, qs, ks, vs, dos, dq_acc, dk_acc, dv_acc) = refs
        h = pl.program_id(0)
        if even:
            qs[...] = q_ref[...].astype(BF16)
            ks[...] = k_ref[...].astype(BF16)

            @pl.when(h == 0)
            def _():
                dcft_ref[...] = jnp.zeros_like(dcft_ref)
                drow_ref[...] = jnp.zeros_like(drow_ref)
        else:
            qs[...] = _rope(q_ref[...], c_ref[...], s_ref[...]).astype(BF16)
            ks[...] = _rope(k_ref[...], c_ref[...], s_ref[...]).astype(BF16)
        vs[...] = v_ref[...].astype(BF16)
        dos[...] = do_ref[...].astype(BF16)
        dk_acc[...] = jnp.zeros_like(dk_acc)
        dv_acc[...] = jnp.zeros_like(dv_acc)

        def softmax_head(hh):
            def qblock(i, carry):
                qi = qs[_qblk(i, bq), :]
                doi = dos[_qblk(i, bq), :]
                dvec = jnp.sum(do_ref[_qblk(i, bq), :] * o_ref[_qblk(i, bq), :], axis=1, keepdims=True)
                lse_i = jnp.max(lse_ref[_qblk(i, bq), :], axis=1, keepdims=True)
                if even:
                    lane = lax.broadcasted_iota(jnp.int32, (bq, LANES), 1)
                    cfq = jnp.sum(jnp.where(lane == hh, cf_ref[_qblk(i, bq), :], 0.0), axis=1, keepdims=True)

                def kblock(j, c):
                    dq, ds_rows = c
                    kj = ks[_blk(j), :]
                    z = _dot_nt(qi, kj) * scale + bias_ref[jnp.minimum(r * i - j + (r - 1), n_tab - 1)]
                    if even:
                        z = z + (cfq - cft_ref[hh, :, _blk(j)])
                    p = jnp.exp(z - lse_i)
                    dp = _dot_nt(doi, vs[_blk(j), :])
                    ds = p * (dp - dvec)
                    dsb = (ds * scale).astype(BF16)
                    dk_acc[_blk(j), :] += _dot_tn(dsb, qi)
                    dv_acc[_blk(j), :] += _dot_tn(p.astype(BF16), doi)
                    if even:
                        dcft_ref[hh, :, _blk(j)] += -jnp.sum(ds, axis=0, keepdims=True)
                    return (dq + jnp.dot(dsb, kj, preferred_element_type=F32),
                            ds_rows + jnp.sum(ds, axis=1, keepdims=True))

                dq, ds_rows = lax.fori_loop(0, r * (i + 1), kblock,
                                            (jnp.zeros((bq, HEAD_DIM), F32), jnp.zeros((bq, 1), F32)))
                dq_acc[_qblk(i, bq), :] = dq
                if even:
                    drow_ref[_qblk(i, bq), :] += jnp.where(lane == hh, ds_rows, 0.0)
                return carry

            lax.fori_loop(0, nq, qblock, 0)

        def sb_head():
            row = lax.broadcasted_iota(jnp.int32, (BLK, BLK), 0)
            col = lax.broadcasted_iota(jnp.int32, (BLK, BLK), 1)
            t_ex = jnp.where(row > col, 1.0, 0.0).astype(BF16)
            t_in = jnp.where(row >= col, 1.0, 0.0).astype(BF16)

            def qblock(i, carry):
                qi = qs[_qblk(i, bq), :]
                doi = dos[_qblk(i, bq), :]
                nkb = r * (i + 1)
                e_tot = jnp.sum(doi.astype(F32) * o_ref[_qblk(i, bq), :], axis=1, keepdims=True)
                zero = jnp.zeros((bq, 1), F32)

                def kblock(jj, c):
                    run, e_run, dq = c
                    j = nkb - 1 - jj
                    kj = ks[_blk(j), :]
                    z = _dot_nt(qi, kj) * scale
                    valid = valid_ref[jnp.minimum(r * i - j + (r - 1), r)]
                    a, m, lsig = _sb_block(z, valid, t_ex, run)
                    sig = jnp.exp(lsig)
                    e = _dot_nt(doi, vs[_blk(j), :]) * a
                    e_before = e_tot - (_split_dot(e, t_in) + e_run)
                    dz = (e * (1.0 - sig) - sig * e_before) * valid
                    dzb = (dz * scale).astype(BF16)
                    dk_acc[_blk(j), :] += _dot_tn(dzb, qi)
                    dv_acc[_blk(j), :] += _dot_tn(a.astype(BF16), doi)
                    return (run + jnp.sum(m, axis=1, keepdims=True), e_run + jnp.sum(e, axis=1, keepdims=True),
                            dq + jnp.dot(dzb, kj, preferred_element_type=F32))

                _, _, dq = lax.fori_loop(0, nkb, kblock, (zero, zero, jnp.zeros((bq, HEAD_DIM), F32)))
                dq_acc[_qblk(i, bq), :] = dq
                return carry

            lax.fori_loop(0, nq, qblock, 0)

        if even:
            @pl.when(h < n_sb)
            def _():
                sb_head()

            @pl.when(h >= n_sb)
            def _():
                softmax_head(h - n_sb)

            dh_ref[0] = dq_acc[...].astype(BF16)
            dh_ref[1] = dk_acc[...].astype(BF16)
        else:
            softmax_head(h)
            dh_ref[0] = _rope_t(dq_acc[...], c_ref[...], s_ref[...]).astype(BF16)
            dh_ref[1] = _rope_t(dk_acc[...], c_ref[...], s_ref[...]).astype(BF16)
        dh_ref[2] = dv_acc[...].astype(BF16)

    head = lambda off: pl.BlockSpec((S, HEAD_DIM), lambda h, off=off: (0, off + h))
    full = pl.BlockSpec((S, LANES), lambda h: (0, 0))
    tfull = pl.BlockSpec((n_heads - n_sb, 1, S), lambda h: (0, 0, 0))
    tab_specs = [pl.BlockSpec(t.shape, lambda h: (0, 0, 0)) for t in tabs]
    dh_spec = pl.BlockSpec((3, S, HEAD_DIM), lambda h: (0, 0, h))
    dh_shape = jax.ShapeDtypeStruct((3, S, D), BF16)
    if even:
        extra_specs, extra = [full, tfull] + tab_specs, (cf, cft) + tabs
        out_specs = [dh_spec, tfull, full]
        out_shape = [dh_shape, jax.ShapeDtypeStruct((n_heads - n_sb, 1, S), F32),
                     jax.ShapeDtypeStruct((S, LANES), F32)]
    else:
        extra_specs, extra = [full, full] + tab_specs, (rope_c, rope_s) + tabs
        out_specs = [dh_spec]
        out_shape = [dh_shape]
    blk_bytes = 10 * _nbytes((S, HEAD_DIM), F32) + sum(_nbytes(t.shape, F32) for t in tabs)
    scratch_bytes = 4 * _nbytes((S, HEAD_DIM), BF16) + 3 * _nbytes((S, HEAD_DIM), F32)
    return pl.pallas_call(
        body, name=name, grid=(n_heads,),
        in_specs=[head(0), head(n_heads), head(2 * n_heads), head(0), head(0), head(0)] + extra_specs,
        out_specs=out_specs, out_shape=out_shape,
        scratch_shapes=[pltpu.VMEM((S, HEAD_DIM), BF16)] * 4 + [pltpu.VMEM((S, HEAD_DIM), F32)] * 3,
        compiler_params=pltpu.CompilerParams(dimension_semantics=("arbitrary",),
                                             vmem_limit_bytes=_vmem_limit(blk_bytes, scratch_bytes)),
    )(hq, hq, hq, do, o, lse, *extra)


def _ffn_fwd(tag, x, g, wgu, wd, fs):
    n = _rms_fwd(tag + "_norm", x, g)
    gu, h = _ffn_up(tag + "_gu", n, wgu, fs)
    if callable(wd):
        wd = wd(h)
    y = _mm(tag + "_down", h, wd, "nn", F32, res=x, alpha=0.5)
    return y, (x, g, n, gu, h), wd


def _ffn_bwd(tag, dx, dxb, wgu, wd, fs, saved, after=None, emit=None):
    x, g, n, gu, h = saved
    dgu = _ffn_dact(tag + "_dgu", dxb, wd, gu, fs, 0.5, after=after)
    dwd = _mm(tag + "_dwd", h, dxb, "tn", BF16, alpha=0.5)
    dwgu = _mm(tag + "_dwgu", n, dgu, "tn", BF16)
    token = emit(dwgu, dwd) if emit else None
    dn = _mm(tag + "_dn", dgu, wgu, "nt", F32, after=token)
    dx_in, dxb_in, dg = _rms_bwd(tag + "_dnorm", dn, x, g, dx)
    return dx_in, dxb_in, dg, dwgu, dwd, token


def _mixer_fwd(tag, kind, x, g, wqkv, wout, n_heads, n_sb, wf=None, bf=None, rope=None):
    n = _rms_fwd(tag + "_norm", x, g)
    hq = _mm(tag + "_qkv", n, wqkv, "nn", F32)
    if kind == "even":
        hf = _mm(tag + "_gate", n, wf, "nn", F32)
        cf, cft = _gate_fwd(tag + "_cumgate", hf, bf)
        cft = cft[:n_heads - n_sb].reshape(n_heads - n_sb, 1, -1)
        o, ob, lse = _attn_fwd_wide(tag + "_attn", hq, kind, n_heads, n_sb, cf=cf, cft=cft)
    else:
        hf = cf = cft = None
        o, ob, lse = _attn_fwd_wide(tag + "_attn", hq, kind, n_heads, n_sb, rope_c=rope[0], rope_s=rope[1])
    y = _mm(tag + "_out", ob, wout, "nn", F32, res=x)
    return y, (x, g, n, hq, hf, cf, cft, o, ob, lse)


def _mixer_bwd(tag, kind, dx, dxb, wqkv, wout, n_heads, n_sb, saved, wf=None, bf=None, rope=None, after=None,
               emit=None):
    x, g, n, hq, hf, cf, cft, o, ob, lse = saved
    do = _mm(tag + "_do", dxb, wout, "nt", F32, after=after)
    dwout = _mm(tag + "_dwout", ob, dxb, "tn", BF16)
    if kind == "even":
        dh3, dcft, drow = _attn_bwd_wide(tag + "_dattn", hq, do, o, lse, kind, n_heads, n_sb, cf=cf, cft=cft)
    else:
        (dh3,) = _attn_bwd_wide(tag + "_dattn", hq, do, o, lse, kind, n_heads, n_sb, rope_c=rope[0], rope_s=rope[1])
    dwqkv = _mm(tag + "_dwqkv", n, dh3, "tn", BF16)
    dwf = db = dhf = None
    if kind == "even":
        n_fox = n_heads - n_sb
        dcft = jnp.pad(dcft.reshape(n_fox, -1), ((0, LANES - n_fox), (0, 0)))
        dhf, db = _gate_bwd(tag + "_dcumgate", dcft, drow, hf, bf)
        dwf = _mm(tag + "_dwf", n, dhf, "tn", BF16)
    token = emit(dwqkv, dwout, dwf) if emit else None
    dn = _mm(tag + "_dn", dh3, wqkv, "nt", F32, after=token)
    if kind == "even":
        dn = _mm(tag + "_dn_gate", dhf, wf, "nt", F32, res=dn)
    dx_in, dxb_in, dg = _rms_bwd(tag + "_dnorm", dn, x, g, dx)
    return dx_in, dxb_in, dg, dwqkv, dwout, dwf, db, token


def _local_step(x, target, w, fs, n_heads, n_sb, emit=None):
    S, D = x.shape
    rope = _rope_tables(S)
    kinds = ("even", "odd")
    saved = []
    h = x
    if callable(w):
        fetch, w = w, {"norm_g": w("norm_g", None), "final_g": w("final_g", None),
                       "wgu1": [None, None], "wd1": [None, None], "wgu2": [None, None], "wd2": [None, None]}
    else:
        fetch = None
    for l, kind in enumerate(kinds):
        ng = [w["norm_g"][l, i][None, :] for i in range(3)]
        if fetch:
            w["wgu1"][l], w["wd1"][l] = fetch(("ffn1", l), h)
        h, s1, wd = _ffn_fwd(f"l{l}_ffn1", h, ng[0], w["wgu1"][l], w["wd1"][l], fs)
        if fetch:
            w["wd1"][l] = wd
        if fetch:
            w.update(fetch(("mix", l), h))
        if kind == "even":
            h, s2 = _mixer_fwd(f"l{l}_mix", kind, h, ng[1], w["wqkv_e"], w["wout_e"], n_heads, n_sb,
                               wf=w["wf"], bf=w["bf"])
        else:
            h, s2 = _mixer_fwd(f"l{l}_mix", kind, h, ng[1], w["wqkv_o"], w["wout_o"], n_heads, n_sb, rope=rope)
        if fetch:
            w["wgu2"][l], w["wd2"][l] = fetch(("ffn2", l), h)
        h, s3, _ = _ffn_fwd(f"l{l}_ffn2", h, ng[2], w["wgu2"][l], w["wd2"][l], fs)
        saved.append((s1, s2, s3))

    dx, dxb, dfinal, loss = _loss_head("loss_head", h, w["final_g"], target)
    grads = {"dfinal": dfinal, "dnorm": [[None] * 3 for _ in kinds],
             "dwgu1": [None, None], "dwd1": [None, None], "dwgu2": [None, None], "dwd2": [None, None]}
    hand = lambda block: (lambda *mats: emit(block, mats)) if emit else None
    token = None
    for l in (1, 0):
        kind = kinds[l]
        s1, s2, s3 = saved[l]
        dx, dxb, dg, grads["dwgu2"][l], grads["dwd2"][l], token = _ffn_bwd(
            f"l{l}_ffn2", dx, dxb, w["wgu2"][l], w["wd2"][l], fs, s3, after=token, emit=hand(("ffn2", l)))
        grads["dnorm"][l][2] = dg
        if kind == "even":
            dx, dxb, dg, grads["dwqkv_e"], grads["dwout_e"], grads["dwf"], grads["db"], token = _mixer_bwd(
                f"l{l}_mix", kind, dx, dxb, w["wqkv_e"], w["wout_e"], n_heads, n_sb, s2, wf=w["wf"], bf=w["bf"],
                after=token, emit=hand(("mix", l)))
        else:
            dx, dxb, dg, grads["dwqkv_o"], grads["dwout_o"], _, _, token = _mixer_bwd(
                f"l{l}_mix", kind, dx, dxb, w["wqkv_o"], w["wout_o"], n_heads, n_sb, s2, rope=rope,
                after=token, emit=hand(("mix", l)))
        grads["dnorm"][l][1] = dg
        dx, dxb, dg, grads["dwgu1"][l], grads["dwd1"][l], token = _ffn_bwd(
            f"l{l}_ffn1", dx, dxb, w["wgu1"][l], w["wd1"][l], fs, s1, after=token, emit=hand(("ffn1", l)))
        grads["dnorm"][l][0] = dg
    return loss, dx, grads


def _cast_into(name, shard, layer, chip, full_shape, place, full=None, after=None):
    R, C = shard.shape[-2:]
    tr = _row_tile(R, 512, step=16)
    if layer is None:
        in_spec = pl.BlockSpec((tr, C), lambda i, k: (i, 0))
    else:
        in_spec = pl.BlockSpec((None, tr, C), lambda i, k: (layer, i, 0))
    lead = (None,) * (len(full_shape) - 2)
    out_spec = pl.BlockSpec(lead + (tr, C), lambda i, k: place(i, k[0]))

    def body(*refs):
        k_ref, w_ref = refs[:2]
        o_ref = refs[-1]
        o_ref[...] = w_ref[...].astype(BF16)

    in_specs = [in_spec] + ([_ANY] if full is not None else []) + ([_ANY] if after is not None else [])
    args = (chip, shard) + ((full,) if full is not None else ()) + ((after,) if after is not None else ())
    grid_spec = pltpu.PrefetchScalarGridSpec(num_scalar_prefetch=1, grid=(R // tr,), in_specs=in_specs, out_specs=out_spec)
    return pl.pallas_call(
        body, name=name, grid_spec=grid_spec, out_shape=jax.ShapeDtypeStruct(full_shape, BF16),
        input_output_aliases={2: 0} if full is not None else {},
        compiler_params=pltpu.CompilerParams(dimension_semantics=("arbitrary",)),
    )(*args)


def _region_shape(grad, kind):
    if kind == "lead":
        return grad.shape[1] // N_CORES, grad.shape[2]
    rows, cols = grad.shape
    if kind == "cols":
        return rows // N_CORES, cols // N_CHIPS
    return rows // (N_CHIPS * N_CORES), cols


def _region_add(name, grad, kind, landed, core):
    rh, cw = _region_shape(grad, kind)
    tr = _row_tile(rh, 256, step=16)
    nrb = rh // tr
    if kind == "cols":
        g_spec = pl.BlockSpec((tr, cw), lambda k, r, c: (c[0] * nrb + r, k))
    elif kind == "rows":
        g_spec = pl.BlockSpec((tr, cw), lambda k, r, c: ((N_CORES * k + c[0]) * nrb + r, 0))
    else:
        g_spec = pl.BlockSpec((None, tr, cw), lambda k, r, c: (k, c[0] * nrb + r, 0))
    l_spec = pl.BlockSpec((None, tr, cw), lambda k, r, c: (k, r, 0))

    def body(c_ref, g_ref, l_ref, o_ref):
        o_ref[...] = (g_ref[...].astype(F32) + l_ref[...].astype(F32)).astype(BF16)

    grid_spec = pltpu.PrefetchScalarGridSpec(
        num_scalar_prefetch=1, grid=(N_CHIPS, nrb), in_specs=[g_spec, l_spec], out_specs=l_spec)
    return pl.pallas_call(
        body, name=name, grid_spec=grid_spec, out_shape=jax.ShapeDtypeStruct(landed.shape, BF16),
        compiler_params=pltpu.CompilerParams(dimension_semantics=("parallel", "parallel"),
                                             vmem_limit_bytes=_vmem_limit(3 * _nbytes((tr, cw), F32))),
    )(core, grad, landed)


def _chip_sum(name, pair, landed, pos):
    _, rh, cw = pair.shape
    tr = _row_tile(rh, max(16, 2**20 // (cw * 4)), step=16)
    nrb = rh // tr

    def body(p_ref, own_ref, l_ref, o_ref):
        acc = own_ref[...].astype(F32)
        for s in range(N_CHIPS - 1):
            acc = acc + l_ref[s].astype(F32)
        o_ref[...] = acc

    grid_spec = pltpu.PrefetchScalarGridSpec(
        num_scalar_prefetch=1, grid=(nrb,),
        in_specs=[pl.BlockSpec((None, tr, cw), lambda r, p: (p[0], r, 0)),
                  pl.BlockSpec((N_CHIPS - 1, tr, cw), lambda r, p: (0, r, 0))],
        out_specs=pl.BlockSpec((tr, cw), lambda r, p: (p[1] * nrb + r, 0)))
    return pl.pallas_call(
        body, name=name, grid_spec=grid_spec, out_shape=jax.ShapeDtypeStruct((N_CORES * rh, cw), F32),
        compiler_params=pltpu.CompilerParams(dimension_semantics=("arbitrary",)),
    )(pos, pair, landed)


def _sum_leading(name, parts):
    n, R, C = parts.shape
    tr = _row_tile(R, max(8, (2**20 // (C * 4)) // 8 * 8))

    def body(p_ref, o_ref):
        acc = p_ref[0]
        for s in range(1, n):
            acc = acc + p_ref[s]
        o_ref[...] = acc

    return pl.pallas_call(
        body, name=name, grid=(R // tr,),
        in_specs=[pl.BlockSpec((n, tr, C), lambda i: (0, i, 0))],
        out_specs=pl.BlockSpec((tr, C), lambda i: (i, 0)),
        out_shape=jax.ShapeDtypeStruct((R, C), F32),
        compiler_params=pltpu.CompilerParams(dimension_semantics=("parallel",)),
    )(parts)


def _adamw(name, w, g, m, v):
    shape = w.shape
    to2d = lambda t: t.reshape(-1, shape[-1]) if t.ndim > 1 else t.reshape(1, -1)
    w2, g2, m2, v2 = (to2d(t) for t in (w, g, m, v))
    R, C = w2.shape
    tr = _row_tile(R, 256)

    def body(w_ref, g_ref, m_ref, v_ref, d_ref, nm_ref, nv_ref):
        gv = g_ref[...]
        nm = ADAM_B1 * m_ref[...] + (1.0 - ADAM_B1) * gv
        nv = ADAM_B2 * v_ref[...] + (1.0 - ADAM_B2) * (gv * gv)
        m_hat = nm / (1.0 - ADAM_B1 ** ADAM_STEP)
        v_hat = nv / (1.0 - ADAM_B2 ** ADAM_STEP)
        d_ref[...] = -ADAM_LR * (m_hat / (jnp.sqrt(v_hat) + ADAM_EPS) + ADAM_WD * w_ref[...])
        nm_ref[...] = nm
        nv_ref[...] = nv

    spec = pl.BlockSpec((tr, C), lambda i: (i, 0))
    sds = jax.ShapeDtypeStruct((R, C), F32)
    d, nm, nv = pl.pallas_call(
        body, name=name, grid=(R // tr,), in_specs=[spec] * 4, out_specs=[spec] * 3, out_shape=[sds] * 3,
        compiler_params=pltpu.CompilerParams(dimension_semantics=("parallel",),
                                             vmem_limit_bytes=_vmem_limit(7 * _nbytes((tr, C), F32))),
    )(w2, g2, m2, v2)
    return d.reshape(shape), nm.reshape(shape), nv.reshape(shape)


_ANY = pl.BlockSpec(memory_space=pl.ANY)


def _mesh_pos():
    return lax.axis_index("x"), lax.axis_index("y"), lax.axis_index("c")


def _other_chips(x, y):
    return [(1 - x, y), (x, 1 - y), (1 - x, 1 - y)]


def _gather_over_chips(name, fulls, views):
    n = len(views)
    nf = len(fulls)

    def body(*refs):
        full = refs[nf:2 * nf]
        ici_send, ici_recv, d2d_send, d2d_recv = refs[2 * nf:]
        x, y, c = _mesh_pos()
        chips = _other_chips(x, y)
        mine = 2 * x + y
        sibling = (x, y, 1 - c)

        def ici(a, p, k):
            i, view, _ = views[a]
            part = view(full[i], k, c)
            return pltpu.make_async_remote_copy(
                src_ref=part, dst_ref=part, send_sem=ici_send.at[a, p], recv_sem=ici_recv.at[a, p],
                device_id=(*chips[p], c), device_id_type=MESH)

        def d2d(a, p, h):
            i, view, _ = views[a]
            px, py = chips[p]
            part = view(full[i], 2 * px + py, h)
            return pltpu.make_async_remote_copy(
                src_ref=part, dst_ref=part, send_sem=d2d_send.at[a, p], recv_sem=d2d_recv.at[a, p],
                device_id=sibling, device_id_type=MESH)

        sends = [ici(a, p, mine) for a in range(n) for p in range(3)]
        for cp in sends:
            cp.start()
        passed = []
        for a in range(n):
            for p, (px, py) in enumerate(chips):
                ici(a, p, 2 * px + py).wait_recv()
                if views[a][2]:
                    fwd = d2d(a, p, c)
                    fwd.start()
                    passed.append(fwd)
        for a in range(n):
            if views[a][2]:
                for p in range(3):
                    d2d(a, p, 1 - c).wait_recv()
        for cp in sends + passed:
            cp.wait_send()

    return pl.pallas_call(
        body, name=name, in_specs=[_ANY] * nf, out_specs=[_ANY] * nf,
        out_shape=[jax.ShapeDtypeStruct(f.shape, f.dtype) for f in fulls],
        input_output_aliases={i: i for i in range(nf)},
        scratch_shapes=[pltpu.SemaphoreType.DMA((n, 3))] * 4,
        compiler_params=pltpu.CompilerParams(has_side_effects=True),
    )(*fulls)


_HBM = pl.BlockSpec(memory_space=pltpu.HBM)
_SEM = pl.BlockSpec(memory_space=pltpu.SEMAPHORE)


def _in_hbm(arrays):
    return [pltpu.with_memory_space_constraint(a, pltpu.HBM) for a in arrays]


def _gather_start(name, fulls, views, after):
    nf = len(fulls)
    ng = 1 + max(g for _, _, g in views)

    def body(*refs):
        full = refs[nf + 1:2 * nf + 1]
        send_sems, recv_sems = refs[2 * nf + 1:2 * nf + 1 + ng], refs[2 * nf + 1 + ng:]
        x, y, c = _mesh_pos()
        chips = _other_chips(x, y)
        for i, view, g in views:
            part = view(full[i], 2 * x + y, c)
            for px, py in chips:
                pltpu.make_async_remote_copy(
                    src_ref=part, dst_ref=part, send_sem=send_sems[g], recv_sem=recv_sems[g],
                    device_id=(px, py, c), device_id_type=MESH).start()

    outs = pl.pallas_call(
        body, name=name, in_specs=[_HBM] * nf + [_ANY], out_specs=[_HBM] * nf + [_SEM] * (2 * ng),
        out_shape=[pltpu.HBM(f.shape, f.dtype) for f in fulls] + [pltpu.SemaphoreType.DMA(())] * (2 * ng),
        input_output_aliases={i: i for i in range(nf)},
        compiler_params=pltpu.CompilerParams(has_side_effects=pltpu.SideEffectType.DATAFLOW_SIDE_EFFECTING),
    )(*_in_hbm(fulls), after)
    return list(outs[:nf]), list(outs[nf:nf + ng]), list(outs[nf + ng:])


def _gather_wait(name, fulls, views, send_sem, recv_sem, after):
    nf = len(fulls)

    def body(*refs):
        send_ref, recv_ref = refs[nf], refs[nf + 1]
        full = refs[nf + 3:]
        x, y, c = _mesh_pos()
        copies = [pltpu.make_async_remote_copy(
            src_ref=view(full[i], 2 * x + y, c), dst_ref=view(full[i], 2 * px + py, c),
            send_sem=send_ref, recv_sem=recv_ref, device_id=(px, py, c), device_id_type=MESH)
            for i, view in views for px, py in _other_chips(x, y)]
        for cp in copies:
            cp.wait_send()
        for cp in copies:
            cp.wait_recv()

    outs = pl.pallas_call(
        body, name=name, in_specs=[_HBM] * nf + [_SEM, _SEM, _ANY], out_specs=[_HBM] * nf,
        out_shape=[pltpu.HBM(f.shape, f.dtype) for f in fulls],
        input_output_aliases={i: i for i in range(nf)},
        compiler_params=pltpu.CompilerParams(has_side_effects=pltpu.SideEffectType.DATAFLOW_SIDE_EFFECTING),
    )(*fulls, send_sem, recv_sem, after)
    return list(outs)


def _forward_to_sibling(name, fulls, views):
    n, nf = len(views), len(fulls)

    def body(*refs):
        full = refs[nf:2 * nf]
        send_sems, recv_sems = refs[2 * nf:]
        x, y, c = _mesh_pos()
        chips = _other_chips(x, y)

        def copy(a, p, h):
            i, view = views[a]
            px, py = chips[p]
            part = view(full[i], 2 * px + py, h)
            return pltpu.make_async_remote_copy(
                src_ref=part, dst_ref=part, send_sem=send_sems.at[a, p], recv_sem=recv_sems.at[a, p],
                device_id=(x, y, 1 - c), device_id_type=MESH)

        sends = [copy(a, p, c) for a in range(n) for p in range(3)]
        for cp in sends:
            cp.start()
        for a in range(n):
            for p in range(3):
                copy(a, p, 1 - c).wait_recv()
        for cp in sends:
            cp.wait_send()

    return pl.pallas_call(
        body, name=name, in_specs=[_ANY] * nf, out_specs=[_ANY] * nf,
        out_shape=[jax.ShapeDtypeStruct(f.shape, f.dtype) for f in fulls],
        input_output_aliases={i: i for i in range(nf)},
        scratch_shapes=[pltpu.SemaphoreType.DMA((n, 3))] * 2,
        compiler_params=pltpu.CompilerParams(has_side_effects=True),
    )(*fulls)


def _pass_start(name, fulls, views, carry):
    nf, nc = len(fulls), len(carry)

    def body(*refs):
        full = refs[nf + nc:2 * nf + nc]
        send_sem, recv_sem = refs[2 * (nf + nc):]
        x, y, c = _mesh_pos()
        for i, view in views:
            for px, py in _other_chips(x, y):
                part = view(full[i], 2 * px + py, c)
                pltpu.make_async_remote_copy(
                    src_ref=part, dst_ref=part, send_sem=send_sem, recv_sem=recv_sem,
                    device_id=(x, y, 1 - c), device_id_type=MESH).start()

    arrays = list(fulls) + list(carry)
    outs = pl.pallas_call(
        body, name=name, in_specs=[_HBM] * (nf + nc), out_specs=[_HBM] * (nf + nc) + [_SEM, _SEM],
        out_shape=[pltpu.HBM(a.shape, a.dtype) for a in arrays] + [pltpu.SemaphoreType.DMA(())] * 2,
        input_output_aliases={i: i for i in range(nf + nc)},
        compiler_params=pltpu.CompilerParams(has_side_effects=pltpu.SideEffectType.DATAFLOW_SIDE_EFFECTING),
    )(*_in_hbm(arrays))
    return list(outs[:nf]), list(outs[nf:nf + nc]), outs[nf + nc], outs[nf + nc + 1]


def _pass_wait(name, fulls, views, send_sem, recv_sem, after):
    nf = len(fulls)

    def body(*refs):
        send_ref, recv_ref = refs[nf], refs[nf + 1]
        full = refs[nf + 3:]
        x, y, c = _mesh_pos()
        copies = [pltpu.make_async_remote_copy(
            src_ref=view(full[i], 2 * px + py, c), dst_ref=view(full[i], 2 * px + py, 1 - c),
            send_sem=send_ref, recv_sem=recv_ref, device_id=(x, y, 1 - c), device_id_type=MESH)
            for i, view in views for px, py in _other_chips(x, y)]
        for cp in copies:
            cp.wait_send()
        for cp in copies:
            cp.wait_recv()

    outs = pl.pallas_call(
        body, name=name, in_specs=[_HBM] * nf + [_SEM, _SEM, _ANY], out_specs=[_HBM] * nf,
        out_shape=[pltpu.HBM(f.shape, f.dtype) for f in fulls],
        input_output_aliases={i: i for i in range(nf)},
        compiler_params=pltpu.CompilerParams(has_side_effects=pltpu.SideEffectType.DATAFLOW_SIDE_EFFECTING),
    )(*fulls, send_sem, recv_sem, after)
    return list(outs)


def _region_view(ref, kind, k, c):
    if kind == "lead":
        rh = ref.shape[1] // N_CORES
        return ref.at[k, pl.ds(pl.multiple_of(c * rh, 8), rh), :]
    rows, cols = ref.shape
    if kind == "cols":
        rh, cw = rows // N_CORES, cols // N_CHIPS
        return ref.at[pl.ds(pl.multiple_of(c * rh, 8), rh), pl.ds(k * cw, cw)]
    rh = rows // (N_CHIPS * N_CORES)
    return ref.at[pl.ds(pl.multiple_of((N_CORES * k + c) * rh, 8), rh), :]


def _send_to_sibling(name, grads, kinds):
    n = len(grads)
    shapes = [jax.ShapeDtypeStruct((N_CHIPS,) + _region_shape(g, kd), g.dtype) for g, kd in zip(grads, kinds)]

    def body(*refs):
        g_ref, land = refs[:n], refs[n:2 * n]
        send_sems, recv_sems = refs[2 * n:]
        x, y, c = _mesh_pos()
        copies = []
        for a in range(n):
            for k in range(N_CHIPS):
                cp = pltpu.make_async_remote_copy(
                    src_ref=_region_view(g_ref[a], kinds[a], k, 1 - c), dst_ref=land[a].at[k],
                    send_sem=send_sems.at[a, k], recv_sem=recv_sems.at[a, k],
                    device_id=(x, y, 1 - c), device_id_type=MESH)
                cp.start()
                copies.append(cp)
        for cp in copies:
            cp.wait_recv()
        for cp in copies:
            cp.wait_send()

    return pl.pallas_call(
        body, name=name, in_specs=[_ANY] * n, out_specs=[_ANY] * n, out_shape=shapes,
        scratch_shapes=[pltpu.SemaphoreType.DMA((n, N_CHIPS)), pltpu.SemaphoreType.DMA((n, N_CHIPS))],
        compiler_params=pltpu.CompilerParams(has_side_effects=True),
    )(*grads)


def _scatter_start(name, pair_sums):
    n = len(pair_sums)
    lands = [lax.empty((N_CHIPS - 1,) + p.shape[1:], p.dtype) for p in pair_sums]

    def body(*refs):
        p_ref, land = refs[2 * n:3 * n], refs[3 * n:4 * n]
        send_sem, recv_sem, token = refs[4 * n:]
        x, y, c = _mesh_pos()
        for a in range(n):
            for p, (px, py) in enumerate(_other_chips(x, y)):
                pltpu.make_async_remote_copy(
                    src_ref=p_ref[a].at[2 * px + py], dst_ref=land[a].at[p], send_sem=send_sem, recv_sem=recv_sem,
                    device_id=(px, py, c), device_id_type=MESH).start()
        token[...] = jnp.zeros_like(token)

    outs = pl.pallas_call(
        body, name=name, in_specs=[_HBM] * (2 * n),
        out_specs=[_HBM] * (2 * n) + [_SEM, _SEM, pl.BlockSpec(memory_space=pltpu.VMEM)],
        out_shape=[pltpu.HBM(t.shape, t.dtype) for t in list(pair_sums) + lands]
        + [pltpu.SemaphoreType.DMA(()), pltpu.SemaphoreType.DMA(()), jax.ShapeDtypeStruct((8, LANES), F32)],
        input_output_aliases={i: i for i in range(2 * n)},
        compiler_params=pltpu.CompilerParams(has_side_effects=pltpu.SideEffectType.DATAFLOW_SIDE_EFFECTING),
    )(*_in_hbm(list(pair_sums) + lands))
    return list(outs[:n]), list(outs[n:2 * n]), outs[2 * n], outs[2 * n + 1], outs[2 * n + 2]


def _scatter_wait(name, pair_sums, lands, send_sem, recv_sem, after):
    n = len(pair_sums)

    def body(*refs):
        send_ref, recv_ref = refs[2 * n], refs[2 * n + 1]
        p_ref, land = refs[2 * n + 3:3 * n + 3], refs[3 * n + 3:]
        x, y, c = _mesh_pos()
        copies = [pltpu.make_async_remote_copy(
            src_ref=p_ref[a].at[2 * px + py], dst_ref=land[a].at[p], send_sem=send_ref, recv_sem=recv_ref,
            device_id=(px, py, c), device_id_type=MESH)
            for a in range(n) for p, (px, py) in enumerate(_other_chips(x, y))]
        for cp in copies:
            cp.wait_send()
        for cp in copies:
            cp.wait_recv()

    outs = pl.pallas_call(
        body, name=name, in_specs=[_HBM] * (2 * n) + [_SEM, _SEM, _ANY], out_specs=[_HBM] * (2 * n),
        out_shape=[pltpu.HBM(t.shape, t.dtype) for t in list(pair_sums) + list(lands)],
        input_output_aliases={i: i for i in range(2 * n)},
        compiler_params=pltpu.CompilerParams(has_side_effects=pltpu.SideEffectType.DATAFLOW_SIDE_EFFECTING),
    )(*pair_sums, *lands, send_sem, recv_sem, after)
    return list(outs[:n]), list(outs[n:])


def _swap_halves(name, shards):
    n = len(shards)

    def body(*refs):
        out = refs[n:2 * n]
        send_sems, recv_sems = refs[2 * n:]
        x, y, c = _mesh_pos()
        sends = []
        for a in range(n):
            rh = out[a].shape[0] // N_CORES
            mine = out[a].at[pl.ds(pl.multiple_of(c * rh, 8), rh), :]
            cp = pltpu.make_async_remote_copy(
                src_ref=mine, dst_ref=mine, send_sem=send_sems.at[a], recv_sem=recv_sems.at[a],
                device_id=(x, y, 1 - c), device_id_type=MESH)
            cp.start()
            sends.append(cp)
        for a in range(n):
            rh = out[a].shape[0] // N_CORES
            theirs = out[a].at[pl.ds(pl.multiple_of((1 - c) * rh, 8), rh), :]
            pltpu.make_async_remote_copy(
                src_ref=theirs, dst_ref=theirs, send_sem=send_sems.at[a], recv_sem=recv_sems.at[a],
                device_id=(x, y, 1 - c), device_id_type=MESH).wait_recv()
        for cp in sends:
            cp.wait_send()

    return pl.pallas_call(
        body, name=name, in_specs=[_ANY] * n, out_specs=[_ANY] * n,
        out_shape=[jax.ShapeDtypeStruct(s.shape, s.dtype) for s in shards],
        input_output_aliases={i: i for i in range(n)},
        scratch_shapes=[pltpu.SemaphoreType.DMA((n,)), pltpu.SemaphoreType.DMA((n,))],
        compiler_params=pltpu.CompilerParams(has_side_effects=True),
    )(*shards)


def _gather_all_devices(name, block):
    R, C = block.shape
    ndev = N_CHIPS * N_CORES

    def body(b_ref, out_ref, send_sems, recv_sems, local_sem):
        x, y, c = _mesh_pos()
        mine = 4 * x + 2 * y + c
        own = pltpu.make_async_copy(b_ref, out_ref.at[mine], local_sem)
        own.start()
        sends = []
        for mask in range(1, ndev):
            fx, fy, fc = (mask >> 2) & 1, (mask >> 1) & 1, mask & 1
            px, py, pc = x ^ fx, y ^ fy, c ^ fc
            cp = pltpu.make_async_remote_copy(
                src_ref=b_ref, dst_ref=out_ref.at[mine], send_sem=send_sems.at[mask - 1],
                recv_sem=recv_sems.at[mask - 1], device_id=(px, py, pc), device_id_type=MESH)
            cp.start()
            sends.append(cp)
        for mask in range(1, ndev):
            fx, fy, fc = (mask >> 2) & 1, (mask >> 1) & 1, mask & 1
            px, py, pc = x ^ fx, y ^ fy, c ^ fc
            pltpu.make_async_remote_copy(
                src_ref=b_ref, dst_ref=out_ref.at[4 * px + 2 * py + pc], send_sem=send_sems.at[mask - 1],
                recv_sem=recv_sems.at[mask - 1], device_id=(px, py, pc), device_id_type=MESH).wait_recv()
        for cp in sends:
            cp.wait_send()
        own.wait()

    return pl.pallas_call(
        body, name=name, in_specs=[_ANY], out_specs=_ANY,
        out_shape=jax.ShapeDtypeStruct((ndev, R, C), F32),
        scratch_shapes=[pltpu.SemaphoreType.DMA((ndev - 1,)), pltpu.SemaphoreType.DMA((ndev - 1,)),
                        pltpu.SemaphoreType.DMA(())],
        compiler_params=pltpu.CompilerParams(has_side_effects=True),
    )(block)


def kernel(x, norm_g, ffn1_w_gate, ffn1_w_up, ffn1_w_down, ffn2_w_gate, ffn2_w_up, ffn2_w_down, even_w_in, even_b_forget, even_w_out, odd_w_qkv, odd_w_out, final_norm_g, loss_target, m_norm_g, m_ffn1_w_gate, m_ffn1_w_up, m_ffn1_w_down, m_ffn2_w_gate, m_ffn2_w_up, m_ffn2_w_down, m_even_w_in, m_even_b_forget, m_even_w_out, m_odd_w_qkv, m_odd_w_out, m_final_norm_g, v_norm_g, v_ffn1_w_gate, v_ffn1_w_up, v_ffn1_w_down, v_ffn2_w_gate, v_ffn2_w_up, v_ffn2_w_down, v_even_w_in, v_even_b_forget, v_even_w_out, v_odd_w_qkv, v_odd_w_out, v_final_norm_g):
    _, S, D = x.shape
    L = norm_g.shape[0]
    assert L == 2 and even_w_in.shape[0] == 1 and odd_w_qkv.shape[0] == 1
    fs = ffn1_w_gate.shape[2]
    F = N_CHIPS * fs
    wc = even_w_in.shape[2]
    n_heads = D // HEAD_DIM
    n_fox = N_CHIPS * wc - 3 * D
    n_sb = n_heads - n_fox
    qs = odd_w_qkv.shape[2]
    os_ = even_w_out.shape[1]
    ns = norm_g.shape[2]
    xi, yi, ci = _mesh_pos()
    chip = 2 * xi + yi

    pos = jnp.stack([chip, ci]).astype(jnp.int32)
    kchip = pos[:1]
    lane = lambda start, size: pl.ds(pl.multiple_of(start, LANES), size)
    sub = lambda start, size: pl.ds(pl.multiple_of(start, 16), size)
    gate_view = lambda r, k, h: r.at[sub(h * (D // 2), D // 2), lane(k * 2 * fs, fs)]
    up_view = lambda r, k, h: r.at[sub(h * (D // 2), D // 2), lane(k * 2 * fs + fs, fs)]
    down_view = lambda r, k, h: r.at[sub(k * fs + h * (fs // 2), fs // 2), :]
    out_view = lambda r, k, h: r.at[sub(k * os_ + h * (os_ // 2), os_ // 2), :]
    tr_d = _row_tile(fs, 512, step=16)
    tr_o = _row_tile(os_, 512, step=16)
    ffn_w = {"ffn1": (ffn1_w_gate, ffn1_w_up, ffn1_w_down), "ffn2": (ffn2_w_gate, ffn2_w_up, ffn2_w_down)}
    win_view = lambda r, k, h: r.at[k, sub(h * (D // 2), D // 2), :]
    qkv_view = lambda r, k, h: r.at[sub(h * (D // 2), D // 2), lane(k * qs, qs)]
    norm_own = lax.dynamic_update_slice(jnp.zeros((L, 3, N_CHIPS * ns), F32), norm_g, (0, 0, chip * ns))
    (norm_full,) = _gather_over_chips("gather_norm", [norm_own], [(0, lambda r, k, h: r.at[:, :, lane(k * ns, ns)], False)])
    first = None
    fulls, views, groups = [], [], {}
    for l in range(L):
        for blk in ("ffn1", "mix", "ffn2"):
            tok = first[0][0] if first else None
            o, v0 = len(fulls), len(views)
            if blk == "mix" and l == 0:
                fulls += [_cast_into("cast_win", even_w_in, 0, kchip, (N_CHIPS, D, wc), lambda i, k: (k, i, 0), after=tok),
                          _cast_into("cast_wout_e", even_w_out, 0, kchip, (D, D), lambda i, k: (k * (os_ // tr_o) + i, 0),
                                     after=tok)]
                views += [(o, win_view), (o + 1, out_view)]
            elif blk == "mix":
                fulls += [_cast_into("cast_wqkv_o", odd_w_qkv, 0, kchip, (D, N_CHIPS * qs), lambda i, k: (i, k), after=tok),
                          _cast_into("cast_wout_o", odd_w_out, 0, kchip, (D, D), lambda i, k: (k * (os_ // tr_o) + i, 0),
                                     after=tok)]
                views += [(o, qkv_view), (o + 1, out_view)]
            else:
                wg, wu, wd = ffn_w[blk]
                t = f"cast_{blk}_l{l}"
                gu = _cast_into(t + "_gate", wg, l, kchip, (D, 2 * F), lambda i, k: (i, 2 * k), after=tok)
                gu = _cast_into(t + "_up", wu, l, kchip, (D, 2 * F), lambda i, k: (i, 2 * k + 1), full=gu)
                if first is None:
                    first = _gather_start("gather_start_first", [gu], [(0, gate_view, 0), (0, up_view, 0)], norm_full)
                    tok = first[0][0]
                    dn = _cast_into(t + "_down", wd, l, kchip, (F, D), lambda i, k: (k * (fs // tr_d) + i, 0), after=tok)
                    fulls += [dn]
                    views += [(o, down_view)]
                else:
                    dn = _cast_into(t + "_down", wd, l, kchip, (F, D), lambda i, k: (k * (fs // tr_d) + i, 0))
                    fulls += [gu, dn]
                    views += [(o, gate_view), (o, up_view), (o + 1, down_view)]
            gid = len(groups)
            views[v0:] = [(i, view, gid) for i, view in views[v0:]]
            groups[(blk, l)] = (gid, list(range(o, len(fulls))), list(range(v0, len(views))))
    started, send_sems, recv_sems = _gather_start("gather_start", fulls, views, first[0][0])

    fwd_order = [(b, l) for l in range(L) for b in ("ffn1", "mix", "ffn2")]
    lookahead = {fwd_order[i]: fwd_order[i + 1] for i in range(2, len(fwd_order) - 1)}
    ahead = {}

    def arrive(tag, arrays, local, ssem, rsem, after):
        got = _gather_wait("gather_wait_" + tag, arrays, local, ssem, rsem, after)
        return _forward_to_sibling("gather_pass_" + tag, got, local)

    def fetch(block, after):
        if block == "norm_g":
            return norm_full
        if block == "final_g":
            return final_norm_g[None, :]
        gid, arrays, rows = groups[block]
        tag = f"{block[0]}_l{block[1]}"
        local = [(views[a][0] - arrays[0], views[a][1]) for a in rows]
        rest = lambda a: arrive(tag, [started[i] for i in arrays], local, send_sems[gid], recv_sems[gid], a)
        if block == ("ffn1", 0):
            (wgu,) = arrive(tag + "_gu", first[0], [(0, gate_view), (0, up_view)], first[1][0], first[2][0], after)
            return wgu, lambda a: rest(a)[0]
        if block in ahead:
            passed, ssem, rsem = ahead.pop(block)
            got = _pass_wait("gather_pass_wait_" + tag, passed, local, ssem, rsem, after)
        else:
            got = rest(after)
        nxt = lookahead.get(block)
        if nxt is not None:
            ngid, narrays, nrows = groups[nxt]
            ntag = f"{nxt[0]}_l{nxt[1]}"
            nlocal = [(views[a][0] - narrays[0], views[a][1]) for a in nrows]
            landed = _gather_wait("gather_wait_" + ntag, [started[i] for i in narrays], nlocal,
                                  send_sems[ngid], recv_sems[ngid], got[0])
            passed, carried, ssem, rsem = _pass_start("gather_pass_start_" + ntag, landed, nlocal, [got[0]])
            ahead[nxt] = (passed, ssem, rsem)
            got = [carried[0]] + list(got[1:])
        if block[0] != "mix":
            return got
        if block[1] == 1:
            return {"wqkv_o": got[0], "wout_o": got[1]}
        win = jnp.concatenate([got[0][k] for k in range(N_CHIPS)], axis=1)
        return {"wqkv_e": win[:, :3 * D], "wf": jnp.pad(win[:, 3 * D:], ((0, 0), (0, LANES - n_fox))),
                "bf": jnp.pad(even_b_forget, ((0, 0), (0, LANES - n_fox))), "wout_e": got[1]}

    pending, shards = [], {}

    def finish(after):
        block, pair, lands, ssem, rsem = pending.pop(0)
        tag = f"{block[0]}_l{block[1]}"
        pair, lands = _scatter_wait("rs_chip_wait_" + tag, pair, lands, ssem, rsem, after)
        shards[block] = [_chip_sum(f"rs_chip_add_{tag}_{a}", p, ld, pos) for a, (p, ld) in enumerate(zip(pair, lands))]

    def emit(block, mats):
        blk, l = block
        tag = f"{blk}_l{l}"
        kinds = ["cols", "rows"]
        if blk == "mix" and l == 0:
            dwqkv, dwout, dwf = mats
            dwin = jnp.concatenate([dwqkv, dwf[:, :n_fox]], axis=1)
            mats = [jnp.stack([dwin[:, k * wc:(k + 1) * wc] for k in range(N_CHIPS)]), dwout]
            kinds = ["lead", "rows"]
        elif blk == "mix":
            mats = list(mats[:2])
        else:
            mats = list(mats)
        landed = _send_to_sibling("rs_pair_send_" + tag, mats, kinds)
        pair = [_region_add(f"rs_pair_add_{tag}_{a}", m, kd, ld, pos[1:])
                for a, (m, kd, ld) in enumerate(zip(mats, kinds, landed))]
        pair, lands, ssem, rsem, token = _scatter_start("rs_chip_start_" + tag, pair)
        if pending:
            finish(token)
        pending.append((block, pair, lands, ssem, rsem))
        return token

    loss_vec, grad_x, g = _local_step(x[0], loss_target[0], fetch, fs, n_heads, n_sb, emit=emit)
    finish(grad_x)
    order = [(b, l) for b in ("ffn1", "ffn2", "mix") for l in range(L)]
    red = _swap_halves("rs_swap_halves", [s for b in order for s in shards[b]])
    red = {b: red[2 * i:2 * i + 2] for i, b in enumerate(order)}
    gu1, gd1 = [red[("ffn1", l)][0] for l in range(L)], [red[("ffn1", l)][1] for l in range(L)]
    gu2, gd2 = [red[("ffn2", l)][0] for l in range(L)], [red[("ffn2", l)][1] for l in range(L)]
    (g_win, g_wout_e), (g_qkv_o, g_wout_o) = red[("mix", 0)], red[("mix", 1)]

    small_rows = [g["dnorm"][l][i] for l in range(L) for i in range(3)] + [
        g["dfinal"], jnp.pad(g["db"], ((0, 0), (0, D - LANES))), jnp.pad(loss_vec, ((0, 0), (0, D - LANES)))]
    small = jnp.concatenate(small_rows + [jnp.zeros((16 - len(small_rows), D), F32)], axis=0)
    small_sum = _sum_leading("small_sum", _gather_all_devices("small_gather", small))
    loss = small_sum[3 * L + 2, 0]
    g_norm = lax.dynamic_slice_in_dim(small_sum[:3 * L].reshape(L, 3, D), chip * ns, ns, axis=2)
    g_final = small_sum[3 * L]
    g_bf = small_sum[3 * L + 1, :n_fox][None, :]

    grads = [
        g_norm,
        jnp.stack([t[:, :fs] for t in gu1]), jnp.stack([t[:, fs:] for t in gu1]), jnp.stack(gd1),
        jnp.stack([t[:, :fs] for t in gu2]), jnp.stack([t[:, fs:] for t in gu2]), jnp.stack(gd2),
        g_win[None], g_bf, g_wout_e[None], g_qkv_o[None], g_wout_o[None], g_final]
    weights = [norm_g, ffn1_w_gate, ffn1_w_up, ffn1_w_down, ffn2_w_gate, ffn2_w_up, ffn2_w_down,
               even_w_in, even_b_forget, even_w_out, odd_w_qkv, odd_w_out, final_norm_g]
    ms = [m_norm_g, m_ffn1_w_gate, m_ffn1_w_up, m_ffn1_w_down, m_ffn2_w_gate, m_ffn2_w_up, m_ffn2_w_down,
          m_even_w_in, m_even_b_forget, m_even_w_out, m_odd_w_qkv, m_odd_w_out, m_final_norm_g]
    vs = [v_norm_g, v_ffn1_w_gate, v_ffn1_w_up, v_ffn1_w_down, v_ffn2_w_gate, v_ffn2_w_up, v_ffn2_w_down,
          v_even_w_in, v_even_b_forget, v_even_w_out, v_odd_w_qkv, v_odd_w_out, v_final_norm_g]
    deltas, new_ms, new_vs = [], [], []
    for i, (wt, gt, mt, vt) in enumerate(zip(weights, grads, ms, vs)):
        d, nm, nv = _adamw(f"adamw_{i}", wt, gt, mt, vt)
        deltas.append(d)
        new_ms.append(nm)
        new_vs.append(nv)
    return (loss, grad_x[None], *grads, *deltas, *new_ms, *new_vs)
```

```python
import math

import jax
import jax.numpy as jnp
from jax import lax
from jax.experimental import pallas as pl
from jax.experimental.pallas import tpu as pltpu

F32 = jnp.float32
BF16 = jnp.bfloat16

HEAD_DIM = 128
ROPE_DIMS = 32
ROPE_THETA = 500000.0
DILATED_PATTERNS = ((128, 1), (512, 4), (2048, 16))
RMS_EPS = 1e-6
NEG_INF = -1e30
ADAM_LR = 0.001
ADAM_B1 = 0.9
ADAM_B2 = 0.999
ADAM_EPS = 1e-08
ADAM_WD = 0.01
ADAM_STEP = 10

N_CHIPS = 4
N_CORES = 2
LANES = 128
BLK = 256
VMEM_BYTES_V7X = 64 * 2**20
MESH = pl.DeviceIdType.MESH


def _vmem_limit(block_bytes, scratch_bytes=0):
    need = 2 * block_bytes + scratch_bytes + 12 * 2**20
    return int(min(need, VMEM_BYTES_V7X - 6 * 2**20))


def _nbytes(shape, dtype):
    return math.prod(shape) * jnp.dtype(dtype).itemsize


def _tile(dim, target):
    best = None
    for t in range(LANES, min(dim, target) + 1, LANES):
        if dim % t == 0:
            best = t
    assert best is not None, (dim, target)
    return best


def _row_tile(rows, target, step=8):
    if rows <= target:
        return rows
    best = None
    for t in range(step, target + 1, step):
        if rows % t == 0:
            best = t
    assert best is not None, (rows, target)
    return best


def _mm(name, a, b, mode, out_dtype, res=None, alpha=1.0, after=None, tm_target=1024, tn_target=1536, tk_target=2048):
    a3 = a.ndim == 3
    b3 = b.ndim == 3
    if mode == "nn":
        assert not a3 and not b3
        (M, K), (K2, N) = a.shape, b.shape
    elif mode == "nt":
        assert not b3
        if a3:
            P, M, Kp = a.shape
            K = P * Kp
        else:
            M, K = a.shape
        N, K2 = b.shape
    else:
        assert mode == "tn" and not a3
        K, M = a.shape
        if b3:
            P, K2, Np = b.shape
            N = P * Np
        else:
            K2, N = b.shape
    assert K == K2, (name, a.shape, b.shape)
    tm = _tile(M, tm_target)
    tn = _tile(Np if b3 else N, tn_target)
    tk = _tile(Kp if a3 else K, tk_target)
    nk = K // tk
    grid = (M // tm, N // tn, nk)

    if mode == "nn":
        a_spec = pl.BlockSpec((tm, tk), lambda i, j, k: (i, k))
        b_spec = pl.BlockSpec((tk, tn), lambda i, j, k: (k, j))
        dims = (((1,), (0,)), ((), ()))
    elif mode == "nt":
        if a3:
            nkp = Kp // tk
            a_spec = pl.BlockSpec((None, tm, tk), lambda i, j, k: (k // nkp, i, k % nkp))
        else:
            a_spec = pl.BlockSpec((tm, tk), lambda i, j, k: (i, k))
        b_spec = pl.BlockSpec((tn, tk), lambda i, j, k: (j, k))
        dims = (((1,), (1,)), ((), ()))
    else:
        a_spec = pl.BlockSpec((tk, tm), lambda i, j, k: (k, i))
        if b3:
            njp = Np // tn
            b_spec = pl.BlockSpec((None, tk, tn), lambda i, j, k: (j // njp, k, j % njp))
        else:
            b_spec = pl.BlockSpec((tk, tn), lambda i, j, k: (k, j))
        dims = (((0,), (0,)), ((), ()))
    o_spec = pl.BlockSpec((tm, tn), lambda i, j, k: (i, j))
    has_res = res is not None

    def finish(y, r_ref, o_ref):
        if alpha != 1.0:
            y = y * alpha
        if has_res:
            y = r_ref[...] + y
        o_ref[...] = y.astype(o_ref.dtype)

    n_in = 2 + has_res + (after is not None)

    def body(*refs):
        a_ref, b_ref = refs[:2]
        r_ref = refs[2] if has_res else None
        o_ref = refs[n_in]
        part = lax.dot_general(a_ref[...], b_ref[...], dims, preferred_element_type=F32)
        if nk == 1:
            finish(part, r_ref, o_ref)
            return
        acc_ref = refs[-1]
        k = pl.program_id(2)

        @pl.when(k == 0)
        def _():
            acc_ref[...] = part

        @pl.when(k > 0)
        def _():
            acc_ref[...] += part

        @pl.when(k == nk - 1)
        def _():
            finish(acc_ref[...], r_ref, o_ref)

    in_specs = [a_spec, b_spec] + ([o_spec] if has_res else []) + ([_ANY] if after is not None else [])
    args = (a, b) + ((res,) if has_res else ()) + ((after,) if after is not None else ())
    blk = (_nbytes((tm, tk), a.dtype) + _nbytes((tk, tn), b.dtype) + _nbytes((tm, tn), out_dtype)
           + (_nbytes((tm, tn), F32) if has_res else 0))
    return pl.pallas_call(
        body, name=name, grid=grid, in_specs=in_specs, out_specs=o_spec,
        out_shape=jax.ShapeDtypeStruct((M, N), out_dtype),
        scratch_shapes=[pltpu.VMEM((tm, tn), F32)] if nk > 1 else [],
        compiler_params=pltpu.CompilerParams(
            dimension_semantics=("parallel", "parallel", "arbitrary"),
            vmem_limit_bytes=_vmem_limit(blk, 2 * _nbytes((tm, tn), F32))),
    )(*args)


def _rms_fwd(name, x, g):
    S, D = x.shape
    tr = _row_tile(S, 256)

    def body(x_ref, g_ref, n_ref):
        xv = x_ref[...]
        r = lax.rsqrt(jnp.mean(xv * xv, axis=-1, keepdims=True) + RMS_EPS)
        n_ref[...] = (xv * r * g_ref[...]).astype(BF16)

    return pl.pallas_call(
        body, name=name, grid=(S // tr,),
        in_specs=[pl.BlockSpec((tr, D), lambda i: (i, 0)), pl.BlockSpec((1, D), lambda i: (0, 0))],
        out_specs=pl.BlockSpec((tr, D), lambda i: (i, 0)),
        out_shape=jax.ShapeDtypeStruct((S, D), BF16),
        compiler_params=pltpu.CompilerParams(dimension_semantics=("parallel",)),
    )(x, g)


def _rms_bwd(name, dn, x, g, dres):
    S, D = x.shape
    tr = _row_tile(S, 256)

    def body(dn_ref, x_ref, g_ref, dres_ref, dx_ref, dxb_ref, dg_ref):
        i = pl.program_id(0)
        xv = x_ref[...]
        dnv = dn_ref[...]
        r = lax.rsqrt(jnp.mean(xv * xv, axis=-1, keepdims=True) + RMS_EPS)
        u = dnv * g_ref[...]
        dot = jnp.mean(u * xv, axis=-1, keepdims=True)
        dx = dres_ref[...] + r * u - xv * (r * r * r * dot)
        dx_ref[...] = dx
        dxb_ref[...] = dx.astype(BF16)

        @pl.when(i == 0)
        def _():
            dg_ref[...] = jnp.zeros_like(dg_ref)

        dg_ref[...] += jnp.sum(dnv * xv * r, axis=0, keepdims=True)

    row = pl.BlockSpec((tr, D), lambda i: (i, 0))
    vec = pl.BlockSpec((1, D), lambda i: (0, 0))
    return pl.pallas_call(
        body, name=name, grid=(S // tr,),
        in_specs=[row, row, vec, row], out_specs=[row, row, vec],
        out_shape=[jax.ShapeDtypeStruct((S, D), F32), jax.ShapeDtypeStruct((S, D), BF16),
                   jax.ShapeDtypeStruct((1, D), F32)],
        compiler_params=pltpu.CompilerParams(dimension_semantics=("arbitrary",)),
    )(dn, x, g, dres)


def _loss_head(name, x, g, target):
    S, D = x.shape
    tr = _row_tile(S, 256)

    def body(x_ref, g_ref, t_ref, dx_ref, dxb_ref, dg_ref, loss_ref):
        i = pl.program_id(0)
        xv = x_ref[...]
        gv = g_ref[...]
        r = lax.rsqrt(jnp.mean(xv * xv, axis=-1, keepdims=True) + RMS_EPS)
        diff = xv * r * gv - t_ref[...]
        part = 0.5 * jnp.sum(jnp.mean(diff * diff, axis=-1, keepdims=True), axis=0, keepdims=True)
        dy = diff * (1.0 / D)
        u = dy * gv
        dot = jnp.mean(u * xv, axis=-1, keepdims=True)
        dx = r * u - xv * (r * r * r * dot)
        dx_ref[...] = dx
        dxb_ref[...] = dx.astype(BF16)

        @pl.when(i == 0)
        def _():
            dg_ref[...] = jnp.zeros_like(dg_ref)
            loss_ref[...] = jnp.zeros_like(loss_ref)

        dg_ref[...] += jnp.sum(dy * xv * r, axis=0, keepdims=True)
        loss_ref[...] += jnp.broadcast_to(part, loss_ref.shape)

    row = pl.BlockSpec((tr, D), lambda i: (i, 0))
    vec = pl.BlockSpec((1, D), lambda i: (0, 0))
    lvec = pl.BlockSpec((1, LANES), lambda i: (0, 0))
    return pl.pallas_call(
        body, name=name, grid=(S // tr,),
        in_specs=[row, vec, row], out_specs=[row, row, vec, lvec],
        out_shape=[jax.ShapeDtypeStruct((S, D), F32), jax.ShapeDtypeStruct((S, D), BF16),
                   jax.ShapeDtypeStruct((1, D), F32), jax.ShapeDtypeStruct((1, LANES), F32)],
        compiler_params=pltpu.CompilerParams(dimension_semantics=("arbitrary",)),
    )(x, g, target)


def _ffn_up(name, n, wgu, fs, tm_target=512):
    S, D = n.shape
    nslab = wgu.shape[1] // (2 * fs)
    tm = _tile(S, tm_target)

    def body(n_ref, w_ref, gu_ref, h_ref):
        y = jnp.dot(n_ref[...], w_ref[...], preferred_element_type=F32)
        gu_ref[...] = y
        gv = y[:, :fs]
        h_ref[...] = (gv * jax.nn.sigmoid(gv) * y[:, fs:]).astype(BF16)

    blk = _nbytes((tm, D), BF16) + _nbytes((D, 2 * fs), BF16) + _nbytes((tm, 2 * fs), F32) + _nbytes((tm, fs), BF16)
    return pl.pallas_call(
        body, name=name, grid=(nslab, S // tm),
        in_specs=[pl.BlockSpec((tm, D), lambda k, i: (i, 0)), pl.BlockSpec((D, 2 * fs), lambda k, i: (0, k))],
        out_specs=[pl.BlockSpec((tm, 2 * fs), lambda k, i: (i, k)), pl.BlockSpec((tm, fs), lambda k, i: (i, k))],
        out_shape=[jax.ShapeDtypeStruct((S, nslab * 2 * fs), F32), jax.ShapeDtypeStruct((S, nslab * fs), BF16)],
        compiler_params=pltpu.CompilerParams(dimension_semantics=("parallel", "parallel"),
                                             vmem_limit_bytes=_vmem_limit(blk, _nbytes((tm, 2 * fs), F32))),
    )(n, wgu)


def _ffn_dact(name, dyb, wd, gu, fs, alpha, after=None, tm_target=512):
    S, D = dyb.shape
    nslab = wd.shape[0] // fs
    tm = _tile(S, tm_target)

    def body(*refs):
        d_ref, w_ref, gu_ref = refs[:3]
        o_ref = refs[-1]
        dhv = _dot_nt(d_ref[...], w_ref[...]) * alpha
        gv = gu_ref[:, :fs]
        uv = gu_ref[:, fs:]
        sg = jax.nn.sigmoid(gv)
        silu = gv * sg
        o_ref[:, :fs] = (dhv * uv * (sg + silu * (1.0 - sg))).astype(BF16)
        o_ref[:, fs:] = (dhv * silu).astype(BF16)

    in_specs = [pl.BlockSpec((tm, D), lambda k, i: (i, 0)), pl.BlockSpec((fs, D), lambda k, i: (k, 0)),
                pl.BlockSpec((tm, 2 * fs), lambda k, i: (i, k))] + ([_ANY] if after is not None else [])
    args = (dyb, wd, gu) + ((after,) if after is not None else ())
    blk = _nbytes((tm, D), BF16) + _nbytes((fs, D), BF16) + _nbytes((tm, 2 * fs), F32) + _nbytes((tm, 2 * fs), BF16)
    return pl.pallas_call(
        body, name=name, grid=(nslab, S // tm), in_specs=in_specs,
        out_specs=pl.BlockSpec((tm, 2 * fs), lambda k, i: (i, k)),
        out_shape=jax.ShapeDtypeStruct((S, nslab * 2 * fs), BF16),
        compiler_params=pltpu.CompilerParams(dimension_semantics=("parallel", "parallel"),
                                             vmem_limit_bytes=_vmem_limit(blk, 2 * _nbytes((tm, fs), F32))),
    )(*args)


def _tri_rows(r0, nrows, ncols, lower):
    row = lax.broadcasted_iota(jnp.int32, (nrows, ncols), 0) + r0
    col = lax.broadcasted_iota(jnp.int32, (nrows, ncols), 1)
    return jnp.where((col <= row) if lower else (col >= row), 1.0, 0.0).astype(F32)


def _gate_fwd(name, hf, b):
    S = hf.shape[0]
    tb = _row_tile(S, 256)

    def body(hf_ref, b_ref, cf_ref, cft_ref, lf_ref):
        zz = hf_ref[...] + b_ref[...]
        lf_ref[...] = jnp.minimum(zz, 0.0) - jnp.log1p(jnp.exp(-jnp.abs(zz)))

        def blk(i, c):
            r0 = pl.multiple_of(i * tb, tb)
            tri = _tri_rows(r0, tb, S, True)
            cf_ref[pl.ds(r0, tb), :] = jnp.dot(tri, lf_ref[...], precision=lax.Precision.HIGHEST,
                                               preferred_element_type=F32)
            return c

        lax.fori_loop(0, S // tb, blk, 0)
        cft_ref[...] = cf_ref[...].T

    full = pl.BlockSpec((S, LANES), lambda: (0, 0))
    return pl.pallas_call(
        body, name=name, in_specs=[full, pl.BlockSpec((1, LANES), lambda: (0, 0))],
        out_specs=[full, pl.BlockSpec((LANES, S), lambda: (0, 0))],
        out_shape=[jax.ShapeDtypeStruct((S, LANES), F32), jax.ShapeDtypeStruct((LANES, S), F32)],
        scratch_shapes=[pltpu.VMEM((S, LANES), F32)],
    )(hf, b)


def _gate_bwd(name, dcft, drow, hf, b):
    S = hf.shape[0]
    tb = _row_tile(S, 256)

    def body(dcft_ref, drow_ref, hf_ref, b_ref, dhf_ref, db_ref, dcf_ref, dlf_ref):
        dcf_ref[...] = dcft_ref[...].T + drow_ref[...]

        def blk(i, c):
            r0 = pl.multiple_of(i * tb, tb)
            tri = _tri_rows(r0, tb, S, False)
            dlf_ref[pl.ds(r0, tb), :] = jnp.dot(tri, dcf_ref[...], precision=lax.Precision.HIGHEST,
                                                preferred_element_type=F32)
            return c

        lax.fori_loop(0, S // tb, blk, 0)
        zz = hf_ref[...] + b_ref[...]
        dhf = dlf_ref[...] * jax.nn.sigmoid(-zz)
        dhf_ref[...] = dhf.astype(BF16)
        db_ref[...] = jnp.sum(dhf, axis=0, keepdims=True)

    full = pl.BlockSpec((S, LANES), lambda: (0, 0))
    vec = pl.BlockSpec((1, LANES), lambda: (0, 0))
    return pl.pallas_call(
        body, name=name, in_specs=[pl.BlockSpec((LANES, S), lambda: (0, 0)), full, full, vec],
        out_specs=[full, vec],
        out_shape=[jax.ShapeDtypeStruct((S, LANES), BF16), jax.ShapeDtypeStruct((1, LANES), F32)],
        scratch_shapes=[pltpu.VMEM((S, LANES), F32), pltpu.VMEM((S, LANES), F32)],
    )(dcft, drow, hf, b)


def _rope_tables(S):
    half = ROPE_DIMS // 2
    freqs = ROPE_THETA ** (-jnp.arange(half, dtype=F32) / half)
    ang = jnp.arange(S, dtype=F32)[:, None] * freqs[None, :]
    cos, sin = jnp.cos(ang), jnp.sin(ang)
    pad = HEAD_DIM - ROPE_DIMS
    c = jnp.concatenate([cos, cos, jnp.ones((S, pad), F32)], axis=1)
    s = jnp.concatenate([-sin, sin, jnp.zeros((S, pad), F32)], axis=1)
    return c, s


def _rope_swap(x):
    half = ROPE_DIMS // 2
    lane = lax.broadcasted_iota(jnp.int32, x.shape, 1)
    upper = jnp.where(lane < ROPE_DIMS, pltpu.roll(x, half, 1), 0.0)
    return jnp.where(lane < half, pltpu.roll(x, HEAD_DIM - half, 1), upper)


def _rope(x, c, s):
    return x * c + _rope_swap(x) * s


def _rope_t(dy, c, s):
    return dy * c + _rope_swap(dy * s)


def _split_dot(x, t):
    hi = x.astype(BF16)
    lo = (x - hi.astype(F32)).astype(BF16)
    return (jnp.dot(hi, t, preferred_element_type=F32) + jnp.dot(lo, t, preferred_element_type=F32))


_NT = (((1,), (1,)), ((), ()))
_TN = (((0,), (0,)), ((), ()))


def _dot_nt(a, b):
    return lax.dot_general(a, b, _NT, preferred_element_type=F32)


def _dot_tn(a, b):
    return lax.dot_general(a, b, _TN, preferred_element_type=F32)


def _blk(i):
    return pl.ds(pl.multiple_of(i * BLK, BLK), BLK)


def _dilated_mult(delta):
    c = jnp.zeros(delta.shape, F32)
    for window, dil in DILATED_PATTERNS:
        ok = (delta >= 0) & (delta <= window) & ((delta & (dil - 1)) == 0)
        c = c + jnp.where(ok, 1.0, 0.0)
    return c


def _query_block(S):
    return min(512, S)


def _offsets(d, bq):
    row = jnp.arange(bq, dtype=jnp.int32)[:, None]
    col = jnp.arange(BLK, dtype=jnp.int32)[None, :]
    return d * BLK + row - col


def _causal_tables(bq, strict):
    r = bq // BLK
    tabs = []
    for d in range(-(r - 1), 1):
        delta = _offsets(d, bq)
        tabs.append(jnp.where((delta > 0) if strict else (delta >= 0), 1.0, 0.0))
    tabs.append(jnp.ones((bq, BLK), F32))
    return jnp.stack(tabs).astype(F32)


def _dilated_tables(bq):
    r = bq // BLK
    limit = sorted(w for w, _ in DILATED_PATTERNS)[-2]
    assert all(BLK % dil == 0 for _, dil in DILATED_PATTERNS)
    d_far = -(-(limit + BLK) // BLK)
    tabs = []
    for d in range(-(r - 1), d_far + 1):
        mult = _dilated_mult(_offsets(d, bq))
        tabs.append(jnp.where(mult > 0, jnp.log(jnp.maximum(mult, 1.0)), NEG_INF))
    return jnp.stack(tabs).astype(F32)


def _qblk(i, bq):
    return pl.ds(pl.multiple_of(i * bq, bq), bq)


def _sb_block(z, valid, t_ex, run):
    t = jnp.log1p(jnp.exp(-jnp.abs(z)))
    lsig = jnp.minimum(z, 0.0) - t
    m = -(jnp.maximum(z, 0.0) + t) * valid
    after = _split_dot(m, t_ex)
    a = jnp.exp(lsig + after + run) * valid
    return a, m, lsig


def _attn_fwd_wide(name, hq, layer_kind, n_heads, n_sb, cf=None, cft=None, rope_c=None, rope_s=None):
    S = hq.shape[0]
    D = n_heads * HEAD_DIM
    bq = _query_block(S)
    r = bq // BLK
    nq = S // bq
    scale = HEAD_DIM ** -0.5
    even = layer_kind == "even"
    if even:
        tabs = (jnp.where(_causal_tables(bq, False) > 0, 0.0, NEG_INF), _causal_tables(bq, True))
    else:
        tabs = (_dilated_tables(bq),)
    n_tab = tabs[0].shape[0]

    def body(*refs):
        if even:
            q_ref, k_ref, v_ref, cf_ref, cft_ref, bias_ref, valid_ref, o_ref, ob_ref, lse_ref, qs, ks, vs = refs
        else:
            q_ref, k_ref, v_ref, c_ref, s_ref, bias_ref, o_ref, ob_ref, lse_ref, qs, ks, vs = refs
        h = pl.program_id(0)
        if even:
            qs[...] = q_ref[...].astype(BF16)
            ks[...] = k_ref[...].astype(BF16)
        else:
            qs[...] = _rope(q_ref[...], c_ref[...], s_ref[...]).astype(BF16)
            ks[...] = _rope(k_ref[...], c_ref[...], s_ref[...]).astype(BF16)
        vs[...] = v_ref[...].astype(BF16)

        def softmax_head(hh):
            def qblock(i, carry):
                qi = qs[_qblk(i, bq), :]
                if even:
                    lane = lax.broadcasted_iota(jnp.int32, (bq, LANES), 1)
                    cfq = jnp.sum(jnp.where(lane == hh, cf_ref[_qblk(i, bq), :], 0.0), axis=1, keepdims=True)

                def kblock(j, c):
                    m_run, l_run, acc = c
                    z = _dot_nt(qi, ks[_blk(j), :]) * scale + bias_ref[jnp.minimum(r * i - j + (r - 1), n_tab - 1)]
                    if even:
                        z = z + (cfq - cft_ref[hh, :, _blk(j)])
                    m_new = jnp.maximum(m_run, jnp.max(z, axis=1, keepdims=True))
                    p = jnp.exp(z - m_new)
                    alpha = jnp.exp(m_run - m_new)
                    l_new = alpha * l_run + jnp.sum(p, axis=1, keepdims=True)
                    acc = alpha * acc + jnp.dot(p.astype(BF16), vs[_blk(j), :], preferred_element_type=F32)
                    return m_new, l_new, acc

                init = (jnp.full((bq, 1), NEG_INF, F32), jnp.zeros((bq, 1), F32), jnp.zeros((bq, HEAD_DIM), F32))
                m_run, l_run, acc = lax.fori_loop(0, r * (i + 1), kblock, init)
                o = acc / l_run
                o_ref[_qblk(i, bq), :] = o
                ob_ref[_qblk(i, bq), :] = o.astype(BF16)
                lse_ref[_qblk(i, bq), :] = jnp.broadcast_to(m_run + jnp.log(l_run), (bq, HEAD_DIM))
                return carry

            lax.fori_loop(0, nq, qblock, 0)

        def sb_head():
            row = lax.broadcasted_iota(jnp.int32, (BLK, BLK), 0)
            col = lax.broadcasted_iota(jnp.int32, (BLK, BLK), 1)
            t_ex = jnp.where(row > col, 1.0, 0.0).astype(BF16)

            def qblock(i, carry):
                qi = qs[_qblk(i, bq), :]

                def kblock(jj, c):
                    run, acc, rest = c
                    j = r * (i + 1) - 1 - jj
                    z = _dot_nt(qi, ks[_blk(j), :]) * scale
                    a, m, _ = _sb_block(z, valid_ref[jnp.minimum(r * i - j + (r - 1), r)], t_ex, run)
                    vj = vs[_blk(j), :]
                    hi = a.astype(BF16)
                    lo = (a - hi.astype(F32)).astype(BF16)
                    acc = acc + jnp.dot(hi, vj, preferred_element_type=F32)
                    rest = rest + jnp.dot(lo, vj, preferred_element_type=F32)
                    return run + jnp.sum(m, axis=1, keepdims=True), acc, rest

                zero = jnp.zeros((bq, HEAD_DIM), F32)
                _, acc, rest = lax.fori_loop(0, r * (i + 1), kblock, (jnp.zeros((bq, 1), F32), zero, zero))
                o_ref[_qblk(i, bq), :] = acc + rest
                ob_ref[_qblk(i, bq), :] = acc.astype(BF16)
                lse_ref[_qblk(i, bq), :] = jnp.zeros((bq, HEAD_DIM), F32)
                return carry

            lax.fori_loop(0, nq, qblock, 0)

        if even:
            @pl.when(h < n_sb)
            def _():
                sb_head()

            @pl.when(h >= n_sb)
            def _():
                softmax_head(h - n_sb)
        else:
            softmax_head(h)

    head = lambda off: pl.BlockSpec((S, HEAD_DIM), lambda h, off=off: (0, off + h))
    full = pl.BlockSpec((S, LANES), lambda h: (0, 0))
    tab_specs = [pl.BlockSpec(t.shape, lambda h: (0, 0, 0)) for t in tabs]
    if even:
        extra_specs = [full, pl.BlockSpec(cft.shape, lambda h: (0, 0, 0))] + tab_specs
        extra = (cf, cft) + tabs
    else:
        extra_specs = [full, full] + tab_specs
        extra = (rope_c, rope_s) + tabs
    blk_bytes = 8 * _nbytes((S, HEAD_DIM), F32) + sum(_nbytes(t.shape, F32) for t in tabs)
    return pl.pallas_call(
        body, name=name, grid=(n_heads,),
        in_specs=[head(0), head(n_heads), head(2 * n_heads)] + extra_specs,
        out_specs=[head(0), head(0), head(0)],
        out_shape=[jax.ShapeDtypeStruct((S, D), F32), jax.ShapeDtypeStruct((S, D), BF16),
                   jax.ShapeDtypeStruct((S, D), F32)],
        scratch_shapes=[pltpu.VMEM((S, HEAD_DIM), BF16)] * 3,
        compiler_params=pltpu.CompilerParams(dimension_semantics=("arbitrary",),
                                             vmem_limit_bytes=_vmem_limit(blk_bytes, 3 * _nbytes((S, HEAD_DIM), BF16))),
    )(hq, hq, hq, *extra)


def _attn_bwd_wide(name, hq, do, o, lse, layer_kind, n_heads, n_sb, cf=None, cft=None, rope_c=None, rope_s=None):
    S = hq.shape[0]
    D = n_heads * HEAD_DIM
    bq = _query_block(S)
    r = bq // BLK
    nq = S // bq
    scale = HEAD_DIM ** -0.5
    even = layer_kind == "even"
    if even:
        tabs = (jnp.where(_causal_tables(bq, False) > 0, 0.0, NEG_INF), _causal_tables(bq, True))
    else:
        tabs = (_dilated_tables(bq),)
    n_tab = tabs[0].shape[0]

    def body(*refs):
        if even:
            (q_ref, k_ref, v_ref, do_ref, o_ref, lse_ref, cf_ref, cft_ref, bias_ref, valid_ref,
             dh_ref, dcft_ref, drow_ref, qs, ks, vs, dos, dq_acc, dk_acc, dv_acc) = refs
        else:
            (q_ref, k_ref, v_ref, do_ref, o_ref, lse_ref, c_ref, s_ref, bias_ref,
             dh_ref, qs, ks, vs, dos, dq_acc, dk_acc, dv_acc) = refs
        h = pl.program_id(0)
        if even:
            qs[...] = q_ref[...].astype(BF16)
            ks[...] = k_ref[...].astype(BF16)

            @pl.when(h == 0)
            def _():
                dcft_ref[...] = jnp.zeros_like(dcft_ref)
                drow_ref[...] = jnp.zeros_like(drow_ref)
        else:
            qs[...] = _rope(q_ref[...], c_ref[...], s_ref[...]).astype(BF16)
            ks[...] = _rope(k_ref[...], c_ref[...], s_ref[...]).astype(BF16)
        vs[...] = v_ref[...].astype(BF16)
        dos[...] = do_ref[...].astype(BF16)
        dk_acc[...] = jnp.zeros_like(dk_acc)
        dv_acc[...] = jnp.zeros_like(dv_acc)

        def softmax_head(hh):
            def qblock(i, carry):
                qi = qs[_qblk(i, bq), :]
                doi = dos[_qblk(i, bq), :]
                dvec = jnp.sum(do_ref[_qblk(i, bq), :] * o_ref[_qblk(i, bq), :], axis=1, keepdims=True)
                lse_i = jnp.max(lse_ref[_qblk(i, bq), :], axis=1, keepdims=True)
                if even:
                    lane = lax.broadcasted_iota(jnp.int32, (bq, LANES), 1)
                    cfq = jnp.sum(jnp.where(lane == hh, cf_ref[_qblk(i, bq), :], 0.0), axis=1, keepdims=True)

                def kblock(j, c):
                    dq, ds_rows = c
                    kj = ks[_blk(j), :]
                    z = _dot_nt(qi, kj) * scale + bias_ref[jnp.minimum(r * i - j + (r - 1), n_tab - 1)]
                    if even:
                        z = z + (cfq - cft_ref[hh, :, _blk(j)])
                    p = jnp.exp(z - lse_i)
                    dp = _dot_nt(doi, vs[_blk(j), :])
                    ds = p * (dp - dvec)
                    dsb = (ds * scale).astype(BF16)
                    dk_acc[_blk(j), :] += _dot_tn(dsb, qi)
                    dv_acc[_blk(j), :] += _dot_tn(p.astype(BF16), doi)
                    if even:
                        dcft_ref[hh, :, _blk(j)] += -jnp.sum(ds, axis=0, keepdims=True)
                    return (dq + jnp.dot(dsb, kj, preferred_element_type=F32),
                            ds_rows + jnp.sum(ds, axis=1, keepdims=True))

                dq, ds_rows = lax.fori_loop(0, r * (i + 1), kblock,
                                            (jnp.zeros((bq, HEAD_DIM), F32), jnp.zeros((bq, 1), F32)))
                dq_acc[_qblk(i, bq), :] = dq
                if even:
                    drow_ref[_qblk(i, bq), :] += jnp.where(lane == hh, ds_rows, 0.0)
                return carry

            lax.fori_loop(0, nq, qblock, 0)

        def sb_head():
            row = lax.broadcasted_iota(jnp.int32, (BLK, BLK), 0)
            col = lax.broadcasted_iota(jnp.int32, (BLK, BLK), 1)
            t_ex = jnp.where(row > col, 1.0, 0.0).astype(BF16)
            t_in = jnp.where(row >= col, 1.0, 0.0).astype(BF16)

            def qblock(i, carry):
                qi = qs[_qblk(i, bq), :]
                doi = dos[_qblk(i, bq), :]
                nkb = r * (i + 1)
                e_tot = jnp.sum(doi.astype(F32) * o_ref[_qblk(i, bq), :], axis=1, keepdims=True)
                zero = jnp.zeros((bq, 1), F32)

                def kblock(jj, c):
                    run, e_run, dq = c
                    j = nkb - 1 - jj
                    kj = ks[_blk(j), :]
                    z = _dot_nt(qi, kj) * scale
                    valid = valid_ref[jnp.minimum(r * i - j + (r - 1), r)]
                    a, m, lsig = _sb_block(z, valid, t_ex, run)
                    sig = jnp.exp(lsig)
                    e = _dot_nt(doi, vs[_blk(j), :]) * a
                    e_before = e_tot - (_split_dot(e, t_in) + e_run)
                    dz = (e * (1.0 - sig) - sig * e_before) * valid
                    dzb = (dz * scale).astype(BF16)
                    dk_acc[_blk(j), :] += _dot_tn(dzb, qi)
                    dv_acc[_blk(j), :] += _dot_tn(a.astype(BF16), doi)
                    return (run + jnp.sum(m, axis=1, keepdims=True), e_run + jnp.sum(e, axis=1, keepdims=True),
                            dq + jnp.dot(dzb, kj, preferred_element_type=F32))

                _, _, dq = lax.fori_loop(0, nkb, kblock, (zero, zero, jnp.zeros((bq, HEAD_DIM), F32)))
                dq_acc[_qblk(i, bq), :] = dq
                return carry

            lax.fori_loop(0, nq, qblock, 0)

        if even:
            @pl.when(h < n_sb)
            def _():
                sb_head()

            @pl.when(h >= n_sb)
            def _():
                softmax_head(h - n_sb)

            dh_ref[0] = dq_acc[...].astype(BF16)
            dh_ref[1] = dk_acc[...].astype(BF16)
        else:
            softmax_head(h)
            dh_ref[0] = _rope_t(dq_acc[...], c_ref[...], s_ref[...]).astype(BF16)
            dh_ref[1] = _rope_t(dk_acc[...], c_ref[...], s_ref[...]).astype(BF16)
        dh_ref[2] = dv_acc[...].astype(BF16)

    head = lambda off: pl.BlockSpec((S, HEAD_DIM), lambda h, off=off: (0, off + h))
    full = pl.BlockSpec((S, LANES), lambda h: (0, 0))
    tfull = pl.BlockSpec((n_heads - n_sb, 1, S), lambda h: (0, 0, 0))
    tab_specs = [pl.BlockSpec(t.shape, lambda h: (0, 0, 0)) for t in tabs]
    dh_spec = pl.BlockSpec((3, S, HEAD_DIM), lambda h: (0, 0, h))
    dh_shape = jax.ShapeDtypeStruct((3, S, D), BF16)
    if even:
        extra_specs, extra = [full, tfull] + tab_specs, (cf, cft) + tabs
        out_specs = [dh_spec, tfull, full]
        out_shape = [dh_shape, jax.ShapeDtypeStruct((n_heads - n_sb, 1, S), F32),
                     jax.ShapeDtypeStruct((S, LANES), F32)]
    else:
        extra_specs, extra = [full, full] + tab_specs, (rope_c, rope_s) + tabs
        out_specs = [dh_spec]
        out_shape = [dh_shape]
    blk_bytes = 10 * _nbytes((S, HEAD_DIM), F32) + sum(_nbytes(t.shape, F32) for t in tabs)
    scratch_bytes = 4 * _nbytes((S, HEAD_DIM), BF16) + 3 * _nbytes((S, HEAD_DIM), F32)
    return pl.pallas_call(
        body, name=name, grid=(n_heads,),
        in_specs=[head(0), head(n_heads), head(2 * n_heads), head(0), head(0), head(0)] + extra_specs,
        out_specs=out_specs, out_shape=out_shape,
        scratch_shapes=[pltpu.VMEM((S, HEAD_DIM), BF16)] * 4 + [pltpu.VMEM((S, HEAD_DIM), F32)] * 3,
        compiler_params=pltpu.CompilerParams(dimension_semantics=("arbitrary",),
                                             vmem_limit_bytes=_vmem_limit(blk_bytes, scratch_bytes)),
    )(hq, hq, hq, do, o, lse, *extra)


def _ffn_fwd(tag, x, g, wgu, wd, fs):
    n = _rms_fwd(tag + "_norm", x, g)
    gu, h = _ffn_up(tag + "_gu", n, wgu, fs)
    if callable(wd):
        wd = wd(h)
    y = _mm(tag + "_down", h, wd, "nn", F32, res=x, alpha=0.5)
    return y, (x, g, n, gu, h), wd


def _ffn_bwd(tag, dx, dxb, wgu, wd, fs, saved, after=None, emit=None):
    x, g, n, gu, h = saved
    dgu = _ffn_dact(tag + "_dgu", dxb, wd, gu, fs, 0.5, after=after)
    dwd = _mm(tag + "_dwd", h, dxb, "tn", BF16, alpha=0.5)
    dwgu = _mm(tag + "_dwgu", n, dgu, "tn", BF16)
    token = emit(dwgu, dwd) if emit else None
    dn = _mm(tag + "_dn", dgu, wgu, "nt", F32, after=token, tk_target=2 * fs)
    dx_in, dxb_in, dg = _rms_bwd(tag + "_dnorm", dn, x, g, dx)
    return dx_in, dxb_in, dg, dwgu, dwd, token


def _mixer_fwd(tag, kind, x, g, wqkv, wout, n_heads, n_sb, wf=None, bf=None, rope=None):
    n = _rms_fwd(tag + "_norm", x, g)
    hq = _mm(tag + "_qkv", n, wqkv, "nn", F32)
    if kind == "even":
        hf = _mm(tag + "_gate", n, wf, "nn", F32)
        cf, cft = _gate_fwd(tag + "_cumgate", hf, bf)
        cft = cft[:n_heads - n_sb].reshape(n_heads - n_sb, 1, -1)
        o, ob, lse = _attn_fwd_wide(tag + "_attn", hq, kind, n_heads, n_sb, cf=cf, cft=cft)
    else:
        hf = cf = cft = None
        o, ob, lse = _attn_fwd_wide(tag + "_attn", hq, kind, n_heads, n_sb, rope_c=rope[0], rope_s=rope[1])
    y = _mm(tag + "_out", ob, wout, "nn", F32, res=x)
    return y, (x, g, n, hq, hf, cf, cft, o, ob, lse)


def _mixer_bwd(tag, kind, dx, dxb, wqkv, wout, n_heads, n_sb, saved, wf=None, bf=None, rope=None, after=None,
               emit=None):
    x, g, n, hq, hf, cf, cft, o, ob, lse = saved
    do = _mm(tag + "_do", dxb, wout, "nt", F32, after=after)
    dwout = _mm(tag + "_dwout", ob, dxb, "tn", BF16)
    if kind == "even":
        dh3, dcft, drow = _attn_bwd_wide(tag + "_dattn", hq, do, o, lse, kind, n_heads, n_sb, cf=cf, cft=cft)
    else:
        (dh3,) = _attn_bwd_wide(tag + "_dattn", hq, do, o, lse, kind, n_heads, n_sb, rope_c=rope[0], rope_s=rope[1])
    dwqkv = _mm(tag + "_dwqkv", n, dh3, "tn", BF16)
    dwf = db = dhf = None
    if kind == "even":
        n_fox = n_heads - n_sb
        dcft = jnp.pad(dcft.reshape(n_fox, -1), ((0, LANES - n_fox), (0, 0)))
        dhf, db = _gate_bwd(tag + "_dcumgate", dcft, drow, hf, bf)
        dwf = _mm(tag + "_dwf", n, dhf, "tn", BF16)
    token = emit(dwqkv, dwout, dwf) if emit else None
    dn = _mm(tag + "_dn", dh3, wqkv, "nt", F32, after=token)
    if kind == "even":
        dn = _mm(tag + "_dn_gate", dhf, wf, "nt", F32, res=dn)
    dx_in, dxb_in, dg = _rms_bwd(tag + "_dnorm", dn, x, g, dx)
    return dx_in, dxb_in, dg, dwqkv, dwout, dwf, db, token


def _local_step(x, target, w, fs, n_heads, n_sb, emit=None):
    S, D = x.shape
    rope = _rope_tables(S)
    kinds = ("even", "odd")
    saved = []
    h = x
    if callable(w):
        fetch, w = w, {"norm_g": w("norm_g", None), "final_g": w("final_g", None),
                       "wgu1": [None, None], "wd1": [None, None], "wgu2": [None, None], "wd2": [None, None]}
    else:
        fetch = None
    for l, kind in enumerate(kinds):
        ng = [w["norm_g"][l, i][None, :] for i in range(3)]
        if fetch:
            w["wgu1"][l], w["wd1"][l] = fetch(("ffn1", l), h)
        h, s1, wd = _ffn_fwd(f"l{l}_ffn1", h, ng[0], w["wgu1"][l], w["wd1"][l], fs)
        if fetch:
            w["wd1"][l] = wd
        if fetch:
            w.update(fetch(("mix", l), h))
        if kind == "even":
            h, s2 = _mixer_fwd(f"l{l}_mix", kind, h, ng[1], w["wqkv_e"], w["wout_e"], n_heads, n_sb,
                               wf=w["wf"], bf=w["bf"])
        else:
            h, s2 = _mixer_fwd(f"l{l}_mix", kind, h, ng[1], w["wqkv_o"], w["wout_o"], n_heads, n_sb, rope=rope)
        if fetch:
            w["wgu2"][l], w["wd2"][l] = fetch(("ffn2", l), h)
        h, s3, _ = _ffn_fwd(f"l{l}_ffn2", h, ng[2], w["wgu2"][l], w["wd2"][l], fs)
        saved.append((s1, s2, s3))

    dx, dxb, dfinal, loss = _loss_head("loss_head", h, w["final_g"], target)
    grads = {"dfinal": dfinal, "dnorm": [[None] * 3 for _ in kinds],
             "dwgu1": [None, None], "dwd1": [None, None], "dwgu2": [None, None], "dwd2": [None, None]}
    hand = lambda block: (lambda *mats: emit(block, mats)) if emit else None
    token = None
    for l in (1, 0):
        kind = kinds[l]
        s1, s2, s3 = saved[l]
        dx, dxb, dg, grads["dwgu2"][l], grads["dwd2"][l], token = _ffn_bwd(
            f"l{l}_ffn2", dx, dxb, w["wgu2"][l], w["wd2"][l], fs, s3, after=token, emit=hand(("ffn2", l)))
        grads["dnorm"][l][2] = dg
        if kind == "even":
            dx, dxb, dg, grads["dwqkv_e"], grads["dwout_e"], grads["dwf"], grads["db"], token = _mixer_bwd(
                f"l{l}_mix", kind, dx, dxb, w["wqkv_e"], w["wout_e"], n_heads, n_sb, s2, wf=w["wf"], bf=w["bf"],
                after=token, emit=hand(("mix", l)))
        else:
            dx, dxb, dg, grads["dwqkv_o"], grads["dwout_o"], _, _, token = _mixer_bwd(
                f"l{l}_mix", kind, dx, dxb, w["wqkv_o"], w["wout_o"], n_heads, n_sb, s2, rope=rope,
                after=token, emit=hand(("mix", l)))
        grads["dnorm"][l][1] = dg
        dx, dxb, dg, grads["dwgu1"][l], grads["dwd1"][l], token = _ffn_bwd(
            f"l{l}_ffn1", dx, dxb, w["wgu1"][l], w["wd1"][l], fs, s1, after=token, emit=hand(("ffn1", l)))
        grads["dnorm"][l][0] = dg
    return loss, dx, grads


def _cast_into(name, shard, layer, chip, full_shape, place, full=None, after=None):
    R, C = shard.shape[-2:]
    tr = _row_tile(R, 512, step=16)
    if layer is None:
        in_spec = pl.BlockSpec((tr, C), lambda i, k: (i, 0))
    else:
        in_spec = pl.BlockSpec((None, tr, C), lambda i, k: (layer, i, 0))
    lead = (None,) * (len(full_shape) - 2)
    out_spec = pl.BlockSpec(lead + (tr, C), lambda i, k: place(i, k[0]))

    def body(*refs):
        k_ref, w_ref = refs[:2]
        o_ref = refs[-1]
        o_ref[...] = w_ref[...].astype(BF16)

    in_specs = [in_spec] + ([_ANY] if full is not None else []) + ([_ANY] if after is not None else [])
    args = (chip, shard) + ((full,) if full is not None else ()) + ((after,) if after is not None else ())
    grid_spec = pltpu.PrefetchScalarGridSpec(num_scalar_prefetch=1, grid=(R // tr,), in_specs=in_specs, out_specs=out_spec)
    return pl.pallas_call(
        body, name=name, grid_spec=grid_spec, out_shape=jax.ShapeDtypeStruct(full_shape, BF16),
        input_output_aliases={2: 0} if full is not None else {},
        compiler_params=pltpu.CompilerParams(dimension_semantics=("arbitrary",)),
    )(*args)


def _region_shape(grad, kind):
    if kind == "lead":
        return grad.shape[1] // N_CORES, grad.shape[2]
    rows, cols = grad.shape
    if kind == "cols":
        return rows // N_CORES, cols // N_CHIPS
    return rows // (N_CHIPS * N_CORES), cols


def _region_add(name, grad, kind, landed, core):
    rh, cw = _region_shape(grad, kind)
    tr = _row_tile(rh, 256, step=16)
    nrb = rh // tr
    if kind == "cols":
        g_spec = pl.BlockSpec((tr, cw), lambda k, r, c: (c[0] * nrb + r, k))
    elif kind == "rows":
        g_spec = pl.BlockSpec((tr, cw), lambda k, r, c: ((N_CORES * k + c[0]) * nrb + r, 0))
    else:
        g_spec = pl.BlockSpec((None, tr, cw), lambda k, r, c: (k, c[0] * nrb + r, 0))
    l_spec = pl.BlockSpec((None, tr, cw), lambda k, r, c: (k, r, 0))

    def body(c_ref, g_ref, l_ref, o_ref):
        o_ref[...] = (g_ref[...].astype(F32) + l_ref[...].astype(F32)).astype(BF16)

    grid_spec = pltpu.PrefetchScalarGridSpec(
        num_scalar_prefetch=1, grid=(N_CHIPS, nrb), in_specs=[g_spec, l_spec], out_specs=l_spec)
    return pl.pallas_call(
        body, name=name, grid_spec=grid_spec, out_shape=jax.ShapeDtypeStruct(landed.shape, BF16),
        compiler_params=pltpu.CompilerParams(dimension_semantics=("parallel", "parallel"),
                                             vmem_limit_bytes=_vmem_limit(3 * _nbytes((tr, cw), F32))),
    )(core, grad, landed)


def _chip_sum(name, pair, landed, pos):
    _, rh, cw = pair.shape
    tr = _row_tile(rh, max(16, 2**20 // (cw * 4)), step=16)
    nrb = rh // tr

    def body(p_ref, own_ref, l_ref, o_ref):
        acc = own_ref[...].astype(F32)
        for s in range(N_CHIPS - 1):
            acc = acc + l_ref[s].astype(F32)
        o_ref[...] = acc

    grid_spec = pltpu.PrefetchScalarGridSpec(
        num_scalar_prefetch=1, grid=(nrb,),
        in_specs=[pl.BlockSpec((None, tr, cw), lambda r, p: (p[0], r, 0)),
                  pl.BlockSpec((N_CHIPS - 1, tr, cw), lambda r, p: (0, r, 0))],
        out_specs=pl.BlockSpec((tr, cw), lambda r, p: (p[1] * nrb + r, 0)))
    return pl.pallas_call(
        body, name=name, grid_spec=grid_spec, out_shape=jax.ShapeDtypeStruct((N_CORES * rh, cw), F32),
        compiler_params=pltpu.CompilerParams(dimension_semantics=("arbitrary",)),
    )(pos, pair, landed)


def _sum_leading(name, parts):
    n, R, C = parts.shape
    tr = _row_tile(R, max(8, (2**20 // (C * 4)) // 8 * 8))

    def body(p_ref, o_ref):
        acc = p_ref[0]
        for s in range(1, n):
            acc = acc + p_ref[s]
        o_ref[...] = acc

    return pl.pallas_call(
        body, name=name, grid=(R // tr,),
        in_specs=[pl.BlockSpec((n, tr, C), lambda i: (0, i, 0))],
        out_specs=pl.BlockSpec((tr, C), lambda i: (i, 0)),
        out_shape=jax.ShapeDtypeStruct((R, C), F32),
        compiler_params=pltpu.CompilerParams(dimension_semantics=("parallel",)),
    )(parts)


def _adamw(name, w, g, m, v):
    shape = w.shape
    to2d = lambda t: t.reshape(-1, shape[-1]) if t.ndim > 1 else t.reshape(1, -1)
    w2, g2, m2, v2 = (to2d(t) for t in (w, g, m, v))
    R, C = w2.shape
    tr = _row_tile(R, 256)

    def body(w_ref, g_ref, m_ref, v_ref, d_ref, nm_ref, nv_ref):
        gv = g_ref[...]
        nm = ADAM_B1 * m_ref[...] + (1.0 - ADAM_B1) * gv
        nv = ADAM_B2 * v_ref[...] + (1.0 - ADAM_B2) * (gv * gv)
        m_hat = nm / (1.0 - ADAM_B1 ** ADAM_STEP)
        v_hat = nv / (1.0 - ADAM_B2 ** ADAM_STEP)
        d_ref[...] = -ADAM_LR * (m_hat / (jnp.sqrt(v_hat) + ADAM_EPS) + ADAM_WD * w_ref[...])
        nm_ref[...] = nm
        nv_ref[...] = nv

    spec = pl.BlockSpec((tr, C), lambda i: (i, 0))
    sds = jax.ShapeDtypeStruct((R, C), F32)
    d, nm, nv = pl.pallas_call(
        body, name=name, grid=(R // tr,), in_specs=[spec] * 4, out_specs=[spec] * 3, out_shape=[sds] * 3,
        compiler_params=pltpu.CompilerParams(dimension_semantics=("parallel",),
                                             vmem_limit_bytes=_vmem_limit(7 * _nbytes((tr, C), F32))),
    )(w2, g2, m2, v2)
    return d.reshape(shape), nm.reshape(shape), nv.reshape(shape)


_ANY = pl.BlockSpec(memory_space=pl.ANY)


def _mesh_pos():
    return lax.axis_index("x"), lax.axis_index("y"), lax.axis_index("c")


def _other_chips(x, y):
    return [(1 - x, y), (x, 1 - y), (1 - x, 1 - y)]


def _gather_over_chips(name, fulls, views):
    n = len(views)
    nf = len(fulls)

    def body(*refs):
        full = refs[nf:2 * nf]
        ici_send, ici_recv, d2d_send, d2d_recv = refs[2 * nf:]
        x, y, c = _mesh_pos()
        chips = _other_chips(x, y)
        mine = 2 * x + y
        sibling = (x, y, 1 - c)

        def ici(a, p, k):
            i, view, _ = views[a]
            part = view(full[i], k, c)
            return pltpu.make_async_remote_copy(
                src_ref=part, dst_ref=part, send_sem=ici_send.at[a, p], recv_sem=ici_recv.at[a, p],
                device_id=(*chips[p], c), device_id_type=MESH)

        def d2d(a, p, h):
            i, view, _ = views[a]
            px, py = chips[p]
            part = view(full[i], 2 * px + py, h)
            return pltpu.make_async_remote_copy(
                src_ref=part, dst_ref=part, send_sem=d2d_send.at[a, p], recv_sem=d2d_recv.at[a, p],
                device_id=sibling, device_id_type=MESH)

        sends = [ici(a, p, mine) for a in range(n) for p in range(3)]
        for cp in sends:
            cp.start()
        passed = []
        for a in range(n):
            for p, (px, py) in enumerate(chips):
                ici(a, p, 2 * px + py).wait_recv()
                if views[a][2]:
                    fwd = d2d(a, p, c)
                    fwd.start()
                    passed.append(fwd)
        for a in range(n):
            if views[a][2]:
                for p in range(3):
                    d2d(a, p, 1 - c).wait_recv()
        for cp in sends + passed:
            cp.wait_send()

    return pl.pallas_call(
        body, name=name, in_specs=[_ANY] * nf, out_specs=[_ANY] * nf,
        out_shape=[jax.ShapeDtypeStruct(f.shape, f.dtype) for f in fulls],
        input_output_aliases={i: i for i in range(nf)},
        scratch_shapes=[pltpu.SemaphoreType.DMA((n, 3))] * 4,
        compiler_params=pltpu.CompilerParams(has_side_effects=True),
    )(*fulls)


_HBM = pl.BlockSpec(memory_space=pltpu.HBM)
_SEM = pl.BlockSpec(memory_space=pltpu.SEMAPHORE)


def _in_hbm(arrays):
    return [pltpu.with_memory_space_constraint(a, pltpu.HBM) for a in arrays]


def _gather_start(name, fulls, views, after):
    nf = len(fulls)
    ng = 1 + max(g for _, _, g in views)

    def body(*refs):
        full = refs[nf + 1:2 * nf + 1]
        send_sems, recv_sems = refs[2 * nf + 1:2 * nf + 1 + ng], refs[2 * nf + 1 + ng:]
        x, y, c = _mesh_pos()
        chips = _other_chips(x, y)
        for i, view, g in views:
            part = view(full[i], 2 * x + y, c)
            for px, py in chips:
                pltpu.make_async_remote_copy(
                    src_ref=part, dst_ref=part, send_sem=send_sems[g], recv_sem=recv_sems[g],
                    device_id=(px, py, c), device_id_type=MESH).start()

    outs = pl.pallas_call(
        body, name=name, in_specs=[_HBM] * nf + [_ANY], out_specs=[_HBM] * nf + [_SEM] * (2 * ng),
        out_shape=[pltpu.HBM(f.shape, f.dtype) for f in fulls] + [pltpu.SemaphoreType.DMA(())] * (2 * ng),
        input_output_aliases={i: i for i in range(nf)},
        compiler_params=pltpu.CompilerParams(has_side_effects=pltpu.SideEffectType.DATAFLOW_SIDE_EFFECTING),
    )(*_in_hbm(fulls), after)
    return list(outs[:nf]), list(outs[nf:nf + ng]), list(outs[nf + ng:])


def _gather_wait(name, fulls, views, send_sem, recv_sem, after):
    nf = len(fulls)

    def body(*refs):
        send_ref, recv_ref = refs[nf], refs[nf + 1]
        full = refs[nf + 3:]
        x, y, c = _mesh_pos()
        copies = [pltpu.make_async_remote_copy(
            src_ref=view(full[i], 2 * x + y, c), dst_ref=view(full[i], 2 * px + py, c),
            send_sem=send_ref, recv_sem=recv_ref, device_id=(px, py, c), device_id_type=MESH)
            for i, view in views for px, py in _other_chips(x, y)]
        for cp in copies:
            cp.wait_send()
        for cp in copies:
            cp.wait_recv()

    outs = pl.pallas_call(
        body, name=name, in_specs=[_HBM] * nf + [_SEM, _SEM, _ANY], out_specs=[_HBM] * nf,
        out_shape=[pltpu.HBM(f.shape, f.dtype) for f in fulls],
        input_output_aliases={i: i for i in range(nf)},
        compiler_params=pltpu.CompilerParams(has_side_effects=pltpu.SideEffectType.DATAFLOW_SIDE_EFFECTING),
    )(*fulls, send_sem, recv_sem, after)
    return list(outs)


def _forward_to_sibling(name, fulls, views):
    n, nf = len(views), len(fulls)

    def body(*refs):
        full = refs[nf:2 * nf]
        send_sems, recv_sems = refs[2 * nf:]
        x, y, c = _mesh_pos()
        chips = _other_chips(x, y)

        def copy(a, p, h):
            i, view = views[a]
            px, py = chips[p]
            part = view(full[i], 2 * px + py, h)
            return pltpu.make_async_remote_copy(
                src_ref=part, dst_ref=part, send_sem=send_sems.at[a, p], recv_sem=recv_sems.at[a, p],
                device_id=(x, y, 1 - c), device_id_type=MESH)

        sends = [copy(a, p, c) for a in range(n) for p in range(3)]
        for cp in sends:
            cp.start()
        for a in range(n):
            for p in range(3):
                copy(a, p, 1 - c).wait_recv()
        for cp in sends:
            cp.wait_send()

    return pl.pallas_call(
        body, name=name, in_specs=[_ANY] * nf, out_specs=[_ANY] * nf,
        out_shape=[jax.ShapeDtypeStruct(f.shape, f.dtype) for f in fulls],
        input_output_aliases={i: i for i in range(nf)},
        scratch_shapes=[pltpu.SemaphoreType.DMA((n, 3))] * 2,
        compiler_params=pltpu.CompilerParams(has_side_effects=True),
    )(*fulls)


def _region_view(ref, kind, k, c):
    if kind == "lead":
        rh = ref.shape[1] // N_CORES
        return ref.at[k, pl.ds(pl.multiple_of(c * rh, 8), rh), :]
    rows, cols = ref.shape
    if kind == "cols":
        rh, cw = rows // N_CORES, cols // N_CHIPS
        return ref.at[pl.ds(pl.multiple_of(c * rh, 8), rh), pl.ds(k * cw, cw)]
    rh = rows // (N_CHIPS * N_CORES)
    return ref.at[pl.ds(pl.multiple_of((N_CORES * k + c) * rh, 8), rh), :]


def _send_to_sibling(name, grads, kinds):
    n = len(grads)
    shapes = [jax.ShapeDtypeStruct((N_CHIPS,) + _region_shape(g, kd), g.dtype) for g, kd in zip(grads, kinds)]

    def body(*refs):
        g_ref, land = refs[:n], refs[n:2 * n]
        send_sems, recv_sems = refs[2 * n:]
        x, y, c = _mesh_pos()
        copies = []
        for a in range(n):
            for k in range(N_CHIPS):
                cp = pltpu.make_async_remote_copy(
                    src_ref=_region_view(g_ref[a], kinds[a], k, 1 - c), dst_ref=land[a].at[k],
                    send_sem=send_sems.at[a, k], recv_sem=recv_sems.at[a, k],
                    device_id=(x, y, 1 - c), device_id_type=MESH)
                cp.start()
                copies.append(cp)
        for cp in copies:
            cp.wait_recv()
        for cp in copies:
            cp.wait_send()

    return pl.pallas_call(
        body, name=name, in_specs=[_ANY] * n, out_specs=[_ANY] * n, out_shape=shapes,
        scratch_shapes=[pltpu.SemaphoreType.DMA((n, N_CHIPS)), pltpu.SemaphoreType.DMA((n, N_CHIPS))],
        compiler_params=pltpu.CompilerParams(has_side_effects=True),
    )(*grads)


def _scatter_start(name, pair_sums):
    n = len(pair_sums)
    lands = [lax.empty((N_CHIPS - 1,) + p.shape[1:], p.dtype) for p in pair_sums]

    def body(*refs):
        p_ref, land = refs[2 * n:3 * n], refs[3 * n:4 * n]
        send_sem, recv_sem, token = refs[4 * n:]
        x, y, c = _mesh_pos()
        for a in range(n):
            for p, (px, py) in enumerate(_other_chips(x, y)):
                pltpu.make_async_remote_copy(
                    src_ref=p_ref[a].at[2 * px + py], dst_ref=land[a].at[p], send_sem=send_sem, recv_sem=recv_sem,
                    device_id=(px, py, c), device_id_type=MESH).start()
        token[...] = jnp.zeros_like(token)

    outs = pl.pallas_call(
        body, name=name, in_specs=[_HBM] * (2 * n),
        out_specs=[_HBM] * (2 * n) + [_SEM, _SEM, pl.BlockSpec(memory_space=pltpu.VMEM)],
        out_shape=[pltpu.HBM(t.shape, t.dtype) for t in list(pair_sums) + lands]
        + [pltpu.SemaphoreType.DMA(()), pltpu.SemaphoreType.DMA(()), jax.ShapeDtypeStruct((8, LANES), F32)],
        input_output_aliases={i: i for i in range(2 * n)},
        compiler_params=pltpu.CompilerParams(has_side_effects=pltpu.SideEffectType.DATAFLOW_SIDE_EFFECTING),
    )(*_in_hbm(list(pair_sums) + lands))
    return list(outs[:n]), list(outs[n:2 * n]), outs[2 * n], outs[2 * n + 1], outs[2 * n + 2]


def _scatter_wait(name, pair_sums, lands, send_sem, recv_sem, after):
    n = len(pair_sums)

    def body(*refs):
        send_ref, recv_ref = refs[2 * n], refs[2 * n + 1]
        p_ref, land = refs[2 * n + 3:3 * n + 3], refs[3 * n + 3:]
        x, y, c = _mesh_pos()
        copies = [pltpu.make_async_remote_copy(
            src_ref=p_ref[a].at[2 * px + py], dst_ref=land[a].at[p], send_sem=send_ref, recv_sem=recv_ref,
            device_id=(px, py, c), device_id_type=MESH)
            for a in range(n) for p, (px, py) in enumerate(_other_chips(x, y))]
        for cp in copies:
            cp.wait_send()
        for cp in copies:
            cp.wait_recv()

    outs = pl.pallas_call(
        body, name=name, in_specs=[_HBM] * (2 * n) + [_SEM, _SEM, _ANY], out_specs=[_HBM] * (2 * n),
        out_shape=[pltpu.HBM(t.shape, t.dtype) for t in list(pair_sums) + list(lands)],
        input_output_aliases={i: i for i in range(2 * n)},
        compiler_params=pltpu.CompilerParams(has_side_effects=pltpu.SideEffectType.DATAFLOW_SIDE_EFFECTING),
    )(*pair_sums, *lands, send_sem, recv_sem, after)
    return list(outs[:n]), list(outs[n:])


def _swap_halves(name, shards):
    n = len(shards)

    def body(*refs):
        out = refs[n:2 * n]
        send_sems, recv_sems = refs[2 * n:]
        x, y, c = _mesh_pos()
        sends = []
        for a in range(n):
            rh = out[a].shape[0] // N_CORES
            mine = out[a].at[pl.ds(pl.multiple_of(c * rh, 8), rh), :]
            cp = pltpu.make_async_remote_copy(
                src_ref=mine, dst_ref=mine, send_sem=send_sems.at[a], recv_sem=recv_sems.at[a],
                device_id=(x, y, 1 - c), device_id_type=MESH)
            cp.start()
            sends.append(cp)
        for a in range(n):
            rh = out[a].shape[0] // N_CORES
            theirs = out[a].at[pl.ds(pl.multiple_of((1 - c) * rh, 8), rh), :]
            pltpu.make_async_remote_copy(
                src_ref=theirs, dst_ref=theirs, send_sem=send_sems.at[a], recv_sem=recv_sems.at[a],
                device_id=(x, y, 1 - c), device_id_type=MESH).wait_recv()
        for cp in sends:
            cp.wait_send()

    return pl.pallas_call(
        body, name=name, in_specs=[_ANY] * n, out_specs=[_ANY] * n,
        out_shape=[jax.ShapeDtypeStruct(s.shape, s.dtype) for s in shards],
        input_output_aliases={i: i for i in range(n)},
        scratch_shapes=[pltpu.SemaphoreType.DMA((n,)), pltpu.SemaphoreType.DMA((n,))],
        compiler_params=pltpu.CompilerParams(has_side_effects=True),
    )(*shards)


def _gather_all_devices(name, block):
    R, C = block.shape
    ndev = N_CHIPS * N_CORES

    def body(b_ref, out_ref, send_sems, recv_sems, local_sem):
        x, y, c = _mesh_pos()
        mine = 4 * x + 2 * y + c
        own = pltpu.make_async_copy(b_ref, out_ref.at[mine], local_sem)
        own.start()
        sends = []
        for mask in range(1, ndev):
            fx, fy, fc = (mask >> 2) & 1, (mask >> 1) & 1, mask & 1
            px, py, pc = x ^ fx, y ^ fy, c ^ fc
            cp = pltpu.make_async_remote_copy(
                src_ref=b_ref, dst_ref=out_ref.at[mine], send_sem=send_sems.at[mask - 1],
                recv_sem=recv_sems.at[mask - 1], device_id=(px, py, pc), device_id_type=MESH)
            cp.start()
            sends.append(cp)
        for mask in range(1, ndev):
            fx, fy, fc = (mask >> 2) & 1, (mask >> 1) & 1, mask & 1
            px, py, pc = x ^ fx, y ^ fy, c ^ fc
            pltpu.make_async_remote_copy(
                src_ref=b_ref, dst_ref=out_ref.at[4 * px + 2 * py + pc], send_sem=send_sems.at[mask - 1],
                recv_sem=recv_sems.at[mask - 1], device_id=(px, py, pc), device_id_type=MESH).wait_recv()
        for cp in sends:
            cp.wait_send()
        own.wait()

    return pl.pallas_call(
        body, name=name, in_specs=[_ANY], out_specs=_ANY,
        out_shape=jax.ShapeDtypeStruct((ndev, R, C), F32),
        scratch_shapes=[pltpu.SemaphoreType.DMA((ndev - 1,)), pltpu.SemaphoreType.DMA((ndev - 1,)),
                        pltpu.SemaphoreType.DMA(())],
        compiler_params=pltpu.CompilerParams(has_side_effects=True),
    )(block)


def kernel(x, norm_g, ffn1_w_gate, ffn1_w_up, ffn1_w_down, ffn2_w_gate, ffn2_w_up, ffn2_w_down, even_w_in, even_b_forget, even_w_out, odd_w_qkv, odd_w_out, final_norm_g, loss_target, m_norm_g, m_ffn1_w_gate, m_ffn1_w_up, m_ffn1_w_down, m_ffn2_w_gate, m_ffn2_w_up, m_ffn2_w_down, m_even_w_in, m_even_b_forget, m_even_w_out, m_odd_w_qkv, m_odd_w_out, m_final_norm_g, v_norm_g, v_ffn1_w_gate, v_ffn1_w_up, v_ffn1_w_down, v_ffn2_w_gate, v_ffn2_w_up, v_ffn2_w_down, v_even_w_in, v_even_b_forget, v_even_w_out, v_odd_w_qkv, v_odd_w_out, v_final_norm_g):
    _, S, D = x.shape
    L = norm_g.shape[0]
    assert L == 2 and even_w_in.shape[0] == 1 and odd_w_qkv.shape[0] == 1
    fs = ffn1_w_gate.shape[2]
    F = N_CHIPS * fs
    wc = even_w_in.shape[2]
    n_heads = D // HEAD_DIM
    n_fox = N_CHIPS * wc - 3 * D
    n_sb = n_heads - n_fox
    qs = odd_w_qkv.shape[2]
    os_ = even_w_out.shape[1]
    ns = norm_g.shape[2]
    xi, yi, ci = _mesh_pos()
    chip = 2 * xi + yi

    pos = jnp.stack([chip, ci]).astype(jnp.int32)
    kchip = pos[:1]
    lane = lambda start, size: pl.ds(pl.multiple_of(start, LANES), size)
    sub = lambda start, size: pl.ds(pl.multiple_of(start, 16), size)
    gate_view = lambda r, k, h: r.at[sub(h * (D // 2), D // 2), lane(k * 2 * fs, fs)]
    up_view = lambda r, k, h: r.at[sub(h * (D // 2), D // 2), lane(k * 2 * fs + fs, fs)]
    down_view = lambda r, k, h: r.at[sub(k * fs + h * (fs // 2), fs // 2), :]
    out_view = lambda r, k, h: r.at[sub(k * os_ + h * (os_ // 2), os_ // 2), :]
    tr_d = _row_tile(fs, 512, step=16)
    tr_o = _row_tile(os_, 512, step=16)
    ffn_w = {"ffn1": (ffn1_w_gate, ffn1_w_up, ffn1_w_down), "ffn2": (ffn2_w_gate, ffn2_w_up, ffn2_w_down)}
    win_view = lambda r, k, h: r.at[k, sub(h * (D // 2), D // 2), :]
    qkv_view = lambda r, k, h: r.at[sub(h * (D // 2), D // 2), lane(k * qs, qs)]
    norm_own = lax.dynamic_update_slice(jnp.zeros((L, 3, N_CHIPS * ns), F32), norm_g, (0, 0, chip * ns))
    (norm_full,) = _gather_over_chips("gather_norm", [norm_own], [(0, lambda r, k, h: r.at[:, :, lane(k * ns, ns)], False)])
    first = None
    fulls, views, groups = [], [], {}
    for l in range(L):
        for blk in ("ffn1", "mix", "ffn2"):
            tok = first[0][0] if first else None
            o, v0 = len(fulls), len(views)
            if blk == "mix" and l == 0:
                fulls += [_cast_into("cast_win", even_w_in, 0, kchip, (N_CHIPS, D, wc), lambda i, k: (k, i, 0), after=tok),
                          _cast_into("cast_wout_e", even_w_out, 0, kchip, (D, D), lambda i, k: (k * (os_ // tr_o) + i, 0),
                                     after=tok)]
                views += [(o, win_view), (o + 1, out_view)]
            elif blk == "mix":
                fulls += [_cast_into("cast_wqkv_o", odd_w_qkv, 0, kchip, (D, N_CHIPS * qs), lambda i, k: (i, k), after=tok),
                          _cast_into("cast_wout_o", odd_w_out, 0, kchip, (D, D), lambda i, k: (k * (os_ // tr_o) + i, 0),
                                     after=tok)]
                views += [(o, qkv_view), (o + 1, out_view)]
            else:
                wg, wu, wd = ffn_w[blk]
                t = f"cast_{blk}_l{l}"
                gu = _cast_into(t + "_gate", wg, l, kchip, (D, 2 * F), lambda i, k: (i, 2 * k), after=tok)
                gu = _cast_into(t + "_up", wu, l, kchip, (D, 2 * F), lambda i, k: (i, 2 * k + 1), full=gu)
                if first is None:
                    first = _gather_start("gather_start_first", [gu], [(0, gate_view, 0), (0, up_view, 0)], norm_full)
                    tok = first[0][0]
                    dn = _cast_into(t + "_down", wd, l, kchip, (F, D), lambda i, k: (k * (fs // tr_d) + i, 0), after=tok)
                    fulls += [dn]
                    views += [(o, down_view)]
                else:
                    dn = _cast_into(t + "_down", wd, l, kchip, (F, D), lambda i, k: (k * (fs // tr_d) + i, 0))
                    fulls += [gu, dn]
                    views += [(o, gate_view), (o, up_view), (o + 1, down_view)]
            gid = len(groups)
            views[v0:] = [(i, view, gid) for i, view in views[v0:]]
            groups[(blk, l)] = (gid, list(range(o, len(fulls))), list(range(v0, len(views))))
    started, send_sems, recv_sems = _gather_start("gather_start", fulls, views, first[0][0])

    def arrive(tag, arrays, local, ssem, rsem, after):
        got = _gather_wait("gather_wait_" + tag, arrays, local, ssem, rsem, after)
        return _forward_to_sibling("gather_pass_" + tag, got, local)

    def fetch(block, after):
        if block == "norm_g":
            return norm_full
        if block == "final_g":
            return final_norm_g[None, :]
        gid, arrays, rows = groups[block]
        tag = f"{block[0]}_l{block[1]}"
        local = [(views[a][0] - arrays[0], views[a][1]) for a in rows]
        rest = lambda a: arrive(tag, [started[i] for i in arrays], local, send_sems[gid], recv_sems[gid], a)
        if block == ("ffn1", 0):
            (wgu,) = arrive(tag + "_gu", first[0], [(0, gate_view), (0, up_view)], first[1][0], first[2][0], after)
            return wgu, lambda a: rest(a)[0]
        got = rest(after)
        if block[0] != "mix":
            return got
        if block[1] == 1:
            return {"wqkv_o": got[0], "wout_o": got[1]}
        win = jnp.concatenate([got[0][k] for k in range(N_CHIPS)], axis=1)
        return {"wqkv_e": win[:, :3 * D], "wf": jnp.pad(win[:, 3 * D:], ((0, 0), (0, LANES - n_fox))),
                "bf": jnp.pad(even_b_forget, ((0, 0), (0, LANES - n_fox))), "wout_e": got[1]}

    pending, shards = [], {}

    def finish(after):
        block, pair, lands, ssem, rsem = pending.pop(0)
        tag = f"{block[0]}_l{block[1]}"
        pair, lands = _scatter_wait("rs_chip_wait_" + tag, pair, lands, ssem, rsem, after)
        shards[block] = [_chip_sum(f"rs_chip_add_{tag}_{a}", p, ld, pos) for a, (p, ld) in enumerate(zip(pair, lands))]

    def emit(block, mats):
        blk, l = block
        tag = f"{blk}_l{l}"
        kinds = ["cols", "rows"]
        if blk == "mix" and l == 0:
            dwqkv, dwout, dwf = mats
            dwin = jnp.concatenate([dwqkv, dwf[:, :n_fox]], axis=1)
            mats = [jnp.stack([dwin[:, k * wc:(k + 1) * wc] for k in range(N_CHIPS)]), dwout]
            kinds = ["lead", "rows"]
        elif blk == "mix":
            mats = list(mats[:2])
        else:
            mats = list(mats)
        landed = _send_to_sibling("rs_pair_send_" + tag, mats, kinds)
        pair = [_region_add(f"rs_pair_add_{tag}_{a}", m, kd, ld, pos[1:])
                for a, (m, kd, ld) in enumerate(zip(mats, kinds, landed))]
        pair, lands, ssem, rsem, token = _scatter_start("rs_chip_start_" + tag, pair)
        if pending:
            finish(token)
        pending.append((block, pair, lands, ssem, rsem))
        return token

    loss_vec, grad_x, g = _local_step(x[0], loss_target[0], fetch, fs, n_heads, n_sb, emit=emit)
    finish(grad_x)
    order = [(b, l) for b in ("ffn1", "ffn2", "mix") for l in range(L)]
    red = _swap_halves("rs_swap_halves", [s for b in order for s in shards[b]])
    red = {b: red[2 * i:2 * i + 2] for i, b in enumerate(order)}
    gu1, gd1 = [red[("ffn1", l)][0] for l in range(L)], [red[("ffn1", l)][1] for l in range(L)]
    gu2, gd2 = [red[("ffn2", l)][0] for l in range(L)], [red[("ffn2", l)][1] for l in range(L)]
    (g_win, g_wout_e), (g_qkv_o, g_wout_o) = red[("mix", 0)], red[("mix", 1)]

    small_rows = [g["dnorm"][l][i] for l in range(L) for i in range(3)] + [
        g["dfinal"], jnp.pad(g["db"], ((0, 0), (0, D - LANES))), jnp.pad(loss_vec, ((0, 0), (0, D - LANES)))]
    small = jnp.concatenate(small_rows + [jnp.zeros((16 - len(small_rows), D), F32)], axis=0)
    small_sum = _sum_leading("small_sum", _gather_all_devices("small_gather", small))
    loss = small_sum[3 * L + 2, 0]
    g_norm = lax.dynamic_slice_in_dim(small_sum[:3 * L].reshape(L, 3, D), chip * ns, ns, axis=2)
    g_final = small_sum[3 * L]
    g_bf = small_sum[3 * L + 1, :n_fox][None, :]

    grads = [
        g_norm,
        jnp.stack([t[:, :fs] for t in gu1]), jnp.stack([t[:, fs:] for t in gu1]), jnp.stack(gd1),
        jnp.stack([t[:, :fs] for t in gu2]), jnp.stack([t[:, fs:] for t in gu2]), jnp.stack(gd2),
        g_win[None], g_bf, g_wout_e[None], g_qkv_o[None], g_wout_o[None], g_final]
    weights = [norm_g, ffn1_w_gate, ffn1_w_up, ffn1_w_down, ffn2_w_gate, ffn2_w_up, ffn2_w_down,
               even_w_in, even_b_forget, even_w_out, odd_w_qkv, odd_w_out, final_norm_g]
    ms = [m_norm_g, m_ffn1_w_gate, m_ffn1_w_up, m_ffn1_w_down, m_ffn2_w_gate, m_ffn2_w_up, m_ffn2_w_down,
          m_even_w_in, m_even_b_forget, m_even_w_out, m_odd_w_qkv, m_odd_w_out, m_final_norm_g]
    vs = [v_norm_g, v_ffn1_w_gate, v_ffn1_w_up, v_ffn1_w_down, v_ffn2_w_gate, v_ffn2_w_up, v_ffn2_w_down,
          v_even_w_in, v_even_b_forget, v_even_w_out, v_odd_w_qkv, v_odd_w_out, v_final_norm_g]
    deltas, new_ms, new_vs = [], [], []
    for i, (wt, gt, mt, vt) in enumerate(zip(weights, grads, ms, vs)):
        d, nm, nv = _adamw(f"adamw_{i}", wt, gt, mt, vt)
        deltas.append(d)
        new_ms.append(nm)
        new_vs.append(nv)
    return (loss, grad_x[None], *grads, *deltas, *new_ms, *new_vs)
```

```python
import math

import jax
import jax.numpy as jnp
from jax import lax
from jax.experimental import pallas as pl
from jax.experimental.pallas import tpu as pltpu

F32 = jnp.float32
BF16 = jnp.bfloat16

HEAD_DIM = 128
ROPE_DIMS = 32
ROPE_THETA = 500000.0
DILATED_PATTERNS = ((128, 1), (512, 4), (2048, 16))
RMS_EPS = 1e-6
NEG_INF = -1e30
ADAM_LR = 0.001
ADAM_B1 = 0.9
ADAM_B2 = 0.999
ADAM_EPS = 1e-08
ADAM_WD = 0.01
ADAM_STEP = 10

N_CHIPS = 4
N_CORES = 2
LANES = 128
BLK = 256
VMEM_BYTES_V7X = 64 * 2**20
MESH = pl.DeviceIdType.MESH


def _vmem_limit(block_bytes, scratch_bytes=0):
    need = 2 * block_bytes + scratch_bytes + 12 * 2**20
    return int(min(need, VMEM_BYTES_V7X - 6 * 2**20))


def _nbytes(shape, dtype):
    return math.prod(shape) * jnp.dtype(dtype).itemsize


def _tile(dim, target):
    best = None
    for t in range(LANES, min(dim, target) + 1, LANES):
        if dim % t == 0:
            best = t
    assert best is not None, (dim, target)
    return best


def _row_tile(rows, target, step=8):
    if rows <= target:
        return rows
    best = None
    for t in range(step, target + 1, step):
        if rows % t == 0:
            best = t
    assert best is not None, (rows, target)
    return best


def _mm(name, a, b, mode, out_dtype, res=None, alpha=1.0, after=None, tm_target=1024, tn_target=1536, tk_target=2048):
    a3 = a.ndim == 3
    b3 = b.ndim == 3
    if mode == "nn":
        assert not a3 and not b3
        (M, K), (K2, N) = a.shape, b.shape
    elif mode == "nt":
        assert not b3
        if a3:
            P, M, Kp = a.shape
            K = P * Kp
        else:
            M, K = a.shape
        N, K2 = b.shape
    else:
        assert mode == "tn" and not a3
        K, M = a.shape
        if b3:
            P, K2, Np = b.shape
            N = P * Np
        else:
            K2, N = b.shape
    assert K == K2, (name, a.shape, b.shape)
    tm = _tile(M, tm_target)
    tn = _tile(Np if b3 else N, tn_target)
    tk = _tile(Kp if a3 else K, tk_target)
    nk = K // tk
    grid = (M // tm, N // tn, nk)

    if mode == "nn":
        a_spec = pl.BlockSpec((tm, tk), lambda i, j, k: (i, k))
        b_spec = pl.BlockSpec((tk, tn), lambda i, j, k: (k, j))
        dims = (((1,), (0,)), ((), ()))
    elif mode == "nt":
        if a3:
            nkp = Kp // tk
            a_spec = pl.BlockSpec((None, tm, tk), lambda i, j, k: (k // nkp, i, k % nkp))
        else:
            a_spec = pl.BlockSpec((tm, tk), lambda i, j, k: (i, k))
        b_spec = pl.BlockSpec((tn, tk), lambda i, j, k: (j, k))
        dims = (((1,), (1,)), ((), ()))
    else:
        a_spec = pl.BlockSpec((tk, tm), lambda i, j, k: (k, i))
        if b3:
            njp = Np // tn
            b_spec = pl.BlockSpec((None, tk, tn), lambda i, j, k: (j // njp, k, j % njp))
        else:
            b_spec = pl.BlockSpec((tk, tn), lambda i, j, k: (k, j))
        dims = (((0,), (0,)), ((), ()))
    o_spec = pl.BlockSpec((tm, tn), lambda i, j, k: (i, j))
    has_res = res is not None

    def finish(y, r_ref, o_ref):
        if alpha != 1.0:
            y = y * alpha
        if has_res:
            y = r_ref[...] + y
        o_ref[...] = y.astype(o_ref.dtype)

    n_in = 2 + has_res + (after is not None)

    def body(*refs):
        a_ref, b_ref = refs[:2]
        r_ref = refs[2] if has_res else None
        o_ref = refs[n_in]
        part = lax.dot_general(a_ref[...], b_ref[...], dims, preferred_element_type=F32)
        if nk == 1:
            finish(part, r_ref, o_ref)
            return
        acc_ref = refs[-1]
        k = pl.program_id(2)

        @pl.when(k == 0)
        def _():
            acc_ref[...] = part

        @pl.when(k > 0)
        def _():
            acc_ref[...] += part

        @pl.when(k == nk - 1)
        def _():
            finish(acc_ref[...], r_ref, o_ref)

    in_specs = [a_spec, b_spec] + ([o_spec] if has_res else []) + ([_ANY] if after is not None else [])
    args = (a, b) + ((res,) if has_res else ()) + ((after,) if after is not None else ())
    blk = (_nbytes((tm, tk), a.dtype) + _nbytes((tk, tn), b.dtype) + _nbytes((tm, tn), out_dtype)
           + (_nbytes((tm, tn), F32) if has_res else 0))
    return pl.pallas_call(
        body, name=name, grid=grid, in_specs=in_specs, out_specs=o_spec,
        out_shape=jax.ShapeDtypeStruct((M, N), out_dtype),
        scratch_shapes=[pltpu.VMEM((tm, tn), F32)] if nk > 1 else [],
        compiler_params=pltpu.CompilerParams(
            dimension_semantics=("parallel", "parallel", "arbitrary"),
            vmem_limit_bytes=_vmem_limit(blk, 2 * _nbytes((tm, tn), F32))),
    )(*args)


def _rms_fwd(name, x, g):
    S, D = x.shape
    tr = _row_tile(S, 256)

    def body(x_ref, g_ref, n_ref):
        xv = x_ref[...]
        r = lax.rsqrt(jnp.mean(xv * xv, axis=-1, keepdims=True) + RMS_EPS)
        n_ref[...] = (xv * r * g_ref[...]).astype(BF16)

    return pl.pallas_call(
        body, name=name, grid=(S // tr,),
        in_specs=[pl.BlockSpec((tr, D), lambda i: (i, 0)), pl.BlockSpec((1, D), lambda i: (0, 0))],
        out_specs=pl.BlockSpec((tr, D), lambda i: (i, 0)),
        out_shape=jax.ShapeDtypeStruct((S, D), BF16),
        compiler_params=pltpu.CompilerParams(dimension_semantics=("parallel",)),
    )(x, g)


def _rms_bwd(name, dn, x, g, dres):
    S, D = x.shape
    tr = _row_tile(S, 256)

    def body(dn_ref, x_ref, g_ref, dres_ref, dx_ref, dxb_ref, dg_ref):
        i = pl.program_id(0)
        xv = x_ref[...]
        dnv = dn_ref[...]
        r = lax.rsqrt(jnp.mean(xv * xv, axis=-1, keepdims=True) + RMS_EPS)
        u = dnv * g_ref[...]
        dot = jnp.mean(u * xv, axis=-1, keepdims=True)
        dx = dres_ref[...] + r * u - xv * (r * r * r * dot)
        dx_ref[...] = dx
        dxb_ref[...] = dx.astype(BF16)

        @pl.when(i == 0)
        def _():
            dg_ref[...] = jnp.zeros_like(dg_ref)

        dg_ref[...] += jnp.sum(dnv * xv * r, axis=0, keepdims=True)

    row = pl.BlockSpec((tr, D), lambda i: (i, 0))
    vec = pl.BlockSpec((1, D), lambda i: (0, 0))
    return pl.pallas_call(
        body, name=name, grid=(S // tr,),
        in_specs=[row, row, vec, row], out_specs=[row, row, vec],
        out_shape=[jax.ShapeDtypeStruct((S, D), F32), jax.ShapeDtypeStruct((S, D), BF16),
                   jax.ShapeDtypeStruct((1, D), F32)],
        compiler_params=pltpu.CompilerParams(dimension_semantics=("arbitrary",)),
    )(dn, x, g, dres)


def _loss_head(name, x, g, target):
    S, D = x.shape
    tr = _row_tile(S, 256)

    def body(x_ref, g_ref, t_ref, dx_ref, dxb_ref, dg_ref, loss_ref):
        i = pl.program_id(0)
        xv = x_ref[...]
        gv = g_ref[...]
        r = lax.rsqrt(jnp.mean(xv * xv, axis=-1, keepdims=True) + RMS_EPS)
        diff = xv * r * gv - t_ref[...]
        part = 0.5 * jnp.sum(jnp.mean(diff * diff, axis=-1, keepdims=True), axis=0, keepdims=True)
        dy = diff * (1.0 / D)
        u = dy * gv
        dot = jnp.mean(u * xv, axis=-1, keepdims=True)
        dx = r * u - xv * (r * r * r * dot)
        dx_ref[...] = dx
        dxb_ref[...] = dx.astype(BF16)

        @pl.when(i == 0)
        def _():
            dg_ref[...] = jnp.zeros_like(dg_ref)
            loss_ref[...] = jnp.zeros_like(loss_ref)

        dg_ref[...] += jnp.sum(dy * xv * r, axis=0, keepdims=True)
        loss_ref[...] += jnp.broadcast_to(part, loss_ref.shape)

    row = pl.BlockSpec((tr, D), lambda i: (i, 0))
    vec = pl.BlockSpec((1, D), lambda i: (0, 0))
    lvec = pl.BlockSpec((1, LANES), lambda i: (0, 0))
    return pl.pallas_call(
        body, name=name, grid=(S // tr,),
        in_specs=[row, vec, row], out_specs=[row, row, vec, lvec],
        out_shape=[jax.ShapeDtypeStruct((S, D), F32), jax.ShapeDtypeStruct((S, D), BF16),
                   jax.ShapeDtypeStruct((1, D), F32), jax.ShapeDtypeStruct((1, LANES), F32)],
        compiler_params=pltpu.CompilerParams(dimension_semantics=("arbitrary",)),
    )(x, g, target)


def _ffn_up(name, n, wgu, fs, tm_target=512):
    S, D = n.shape
    nslab = wgu.shape[1] // (2 * fs)
    tm = _tile(S, tm_target)

    def body(n_ref, w_ref, gu_ref, h_ref):
        y = jnp.dot(n_ref[...], w_ref[...], preferred_element_type=F32)
        gu_ref[...] = y
        gv = y[:, :fs]
        h_ref[...] = (gv * jax.nn.sigmoid(gv) * y[:, fs:]).astype(BF16)

    blk = _nbytes((tm, D), BF16) + _nbytes((D, 2 * fs), BF16) + _nbytes((tm, 2 * fs), F32) + _nbytes((tm, fs), BF16)
    return pl.pallas_call(
        body, name=name, grid=(nslab, S // tm),
        in_specs=[pl.BlockSpec((tm, D), lambda k, i: (i, 0)), pl.BlockSpec((D, 2 * fs), lambda k, i: (0, k))],
        out_specs=[pl.BlockSpec((tm, 2 * fs), lambda k, i: (i, k)), pl.BlockSpec((tm, fs), lambda k, i: (i, k))],
        out_shape=[jax.ShapeDtypeStruct((S, nslab * 2 * fs), F32), jax.ShapeDtypeStruct((S, nslab * fs), BF16)],
        compiler_params=pltpu.CompilerParams(dimension_semantics=("parallel", "parallel"),
                                             vmem_limit_bytes=_vmem_limit(blk, _nbytes((tm, 2 * fs), F32))),
    )(n, wgu)


def _ffn_dact(name, dyb, wd, gu, fs, alpha, after=None, tm_target=512):
    S, D = dyb.shape
    nslab = wd.shape[0] // fs
    tm = _tile(S, tm_target)

    def body(*refs):
        d_ref, w_ref, gu_ref = refs[:3]
        o_ref = refs[-1]
        dhv = _dot_nt(d_ref[...], w_ref[...]) * alpha
        gv = gu_ref[:, :fs]
        uv = gu_ref[:, fs:]
        sg = jax.nn.sigmoid(gv)
        silu = gv * sg
        o_ref[:, :fs] = (dhv * uv * (sg + silu * (1.0 - sg))).astype(BF16)
        o_ref[:, fs:] = (dhv * silu).astype(BF16)

    in_specs = [pl.BlockSpec((tm, D), lambda k, i: (i, 0)), pl.BlockSpec((fs, D), lambda k, i: (k, 0)),
                pl.BlockSpec((tm, 2 * fs), lambda k, i: (i, k))] + ([_ANY] if after is not None else [])
    args = (dyb, wd, gu) + ((after,) if after is not None else ())
    blk = _nbytes((tm, D), BF16) + _nbytes((fs, D), BF16) + _nbytes((tm, 2 * fs), F32) + _nbytes((tm, 2 * fs), BF16)
    return pl.pallas_call(
        body, name=name, grid=(nslab, S // tm), in_specs=in_specs,
        out_specs=pl.BlockSpec((tm, 2 * fs), lambda k, i: (i, k)),
        out_shape=jax.ShapeDtypeStruct((S, nslab * 2 * fs), BF16),
        compiler_params=pltpu.CompilerParams(dimension_semantics=("parallel", "parallel"),
                                             vmem_limit_bytes=_vmem_limit(blk, 2 * _nbytes((tm, fs), F32))),
    )(*args)


def _tri_rows(r0, nrows, ncols, lower):
    row = lax.broadcasted_iota(jnp.int32, (nrows, ncols), 0) + r0
    col = lax.broadcasted_iota(jnp.int32, (nrows, ncols), 1)
    return jnp.where((col <= row) if lower else (col >= row), 1.0, 0.0).astype(F32)


def _gate_fwd(name, hf, b):
    S = hf.shape[0]
    tb = _row_tile(S, 256)

    def body(hf_ref, b_ref, cf_ref, cft_ref, lf_ref):
        zz = hf_ref[...] + b_ref[...]
        lf_ref[...] = jnp.minimum(zz, 0.0) - jnp.log1p(jnp.exp(-jnp.abs(zz)))

        def blk(i, c):
            r0 = pl.multiple_of(i * tb, tb)
            tri = _tri_rows(r0, tb, S, True)
            cf_ref[pl.ds(r0, tb), :] = jnp.dot(tri, lf_ref[...], precision=lax.Precision.HIGHEST,
                                               preferred_element_type=F32)
            return c

        lax.fori_loop(0, S // tb, blk, 0)
        cft_ref[...] = cf_ref[...].T

    full = pl.BlockSpec((S, LANES), lambda: (0, 0))
    return pl.pallas_call(
        body, name=name, in_specs=[full, pl.BlockSpec((1, LANES), lambda: (0, 0))],
        out_specs=[full, pl.BlockSpec((LANES, S), lambda: (0, 0))],
        out_shape=[jax.ShapeDtypeStruct((S, LANES), F32), jax.ShapeDtypeStruct((LANES, S), F32)],
        scratch_shapes=[pltpu.VMEM((S, LANES), F32)],
    )(hf, b)


def _gate_bwd(name, dcft, drow, hf, b):
    S = hf.shape[0]
    tb = _row_tile(S, 256)

    def body(dcft_ref, drow_ref, hf_ref, b_ref, dhf_ref, db_ref, dcf_ref, dlf_ref):
        dcf_ref[...] = dcft_ref[...].T + drow_ref[...]

        def blk(i, c):
            r0 = pl.multiple_of(i * tb, tb)
            tri = _tri_rows(r0, tb, S, False)
            dlf_ref[pl.ds(r0, tb), :] = jnp.dot(tri, dcf_ref[...], precision=lax.Precision.HIGHEST,
                                                preferred_element_type=F32)
            return c

        lax.fori_loop(0, S // tb, blk, 0)
        zz = hf_ref[...] + b_ref[...]
        dhf = dlf_ref[...] * jax.nn.sigmoid(-zz)
        dhf_ref[...] = dhf.astype(BF16)
        db_ref[...] = jnp.sum(dhf, axis=0, keepdims=True)

    full = pl.BlockSpec((S, LANES), lambda: (0, 0))
    vec = pl.BlockSpec((1, LANES), lambda: (0, 0))
    return pl.pallas_call(
        body, name=name, in_specs=[pl.BlockSpec((LANES, S), lambda: (0, 0)), full, full, vec],
        out_specs=[full, vec],
        out_shape=[jax.ShapeDtypeStruct((S, LANES), BF16), jax.ShapeDtypeStruct((1, LANES), F32)],
        scratch_shapes=[pltpu.VMEM((S, LANES), F32), pltpu.VMEM((S, LANES), F32)],
    )(dcft, drow, hf, b)


def _rope_tables(S):
    half = ROPE_DIMS // 2
    freqs = ROPE_THETA ** (-jnp.arange(half, dtype=F32) / half)
    ang = jnp.arange(S, dtype=F32)[:, None] * freqs[None, :]
    cos, sin = jnp.cos(ang), jnp.sin(ang)
    pad = HEAD_DIM - ROPE_DIMS
    c = jnp.concatenate([cos, cos, jnp.ones((S, pad), F32)], axis=1)
    s = jnp.concatenate([-sin, sin, jnp.zeros((S, pad), F32)], axis=1)
    return c, s


def _rope_swap(x):
    half = ROPE_DIMS // 2
    lane = lax.broadcasted_iota(jnp.int32, x.shape, 1)
    upper = jnp.where(lane < ROPE_DIMS, pltpu.roll(x, half, 1), 0.0)
    return jnp.where(lane < half, pltpu.roll(x, HEAD_DIM - half, 1), upper)


def _rope(x, c, s):
    return x * c + _rope_swap(x) * s


def _rope_t(dy, c, s):
    return dy * c + _rope_swap(dy * s)


def _split_dot(x, t):
    hi = x.astype(BF16)
    lo = (x - hi.astype(F32)).astype(BF16)
    return (jnp.dot(hi, t, preferred_element_type=F32) + jnp.dot(lo, t, preferred_element_type=F32))


_NT = (((1,), (1,)), ((), ()))
_TN = (((0,), (0,)), ((), ()))


def _dot_nt(a, b):
    return lax.dot_general(a, b, _NT, preferred_element_type=F32)


def _dot_tn(a, b):
    return lax.dot_general(a, b, _TN, preferred_element_type=F32)


def _blk(i):
    return pl.ds(pl.multiple_of(i * BLK, BLK), BLK)


def _dilated_mult(delta):
    c = jnp.zeros(delta.shape, F32)
    for window, dil in DILATED_PATTERNS:
        ok = (delta >= 0) & (delta <= window) & ((delta & (dil - 1)) == 0)
        c = c + jnp.where(ok, 1.0, 0.0)
    return c


def _query_block(S):
    return min(512, S)


def _offsets(d, bq):
    row = jnp.arange(bq, dtype=jnp.int32)[:, None]
    col = jnp.arange(BLK, dtype=jnp.int32)[None, :]
    return d * BLK + row - col


def _causal_tables(bq, strict):
    r = bq // BLK
    tabs = []
    for d in range(-(r - 1), 1):
        delta = _offsets(d, bq)
        tabs.append(jnp.where((delta > 0) if strict else (delta >= 0), 1.0, 0.0))
    tabs.append(jnp.ones((bq, BLK), F32))
    return jnp.stack(tabs).astype(F32)


def _dilated_tables(bq):
    r = bq // BLK
    limit = sorted(w for w, _ in DILATED_PATTERNS)[-2]
    assert all(BLK % dil == 0 for _, dil in DILATED_PATTERNS)
    d_far = -(-(limit + BLK) // BLK)
    tabs = []
    for d in range(-(r - 1), d_far + 1):
        mult = _dilated_mult(_offsets(d, bq))
        tabs.append(jnp.where(mult > 0, jnp.log(jnp.maximum(mult, 1.0)), NEG_INF))
    return jnp.stack(tabs).astype(F32)


def _qblk(i, bq):
    return pl.ds(pl.multiple_of(i * bq, bq), bq)


def _sb_block(z, valid, t_ex, run):
    t = jnp.log1p(jnp.exp(-jnp.abs(z)))
    lsig = jnp.minimum(z, 0.0) - t
    m = -(jnp.maximum(z, 0.0) + t) * valid
    after = _split_dot(m, t_ex)
    a = jnp.exp(lsig + after + run) * valid
    return a, m, lsig


def _attn_fwd_wide(name, hq, layer_kind, n_heads, n_sb, cf=None, cft=None, rope_c=None, rope_s=None):
    S = hq.shape[0]
    D = n_heads * HEAD_DIM
    bq = _query_block(S)
    r = bq // BLK
    nq = S // bq
    scale = HEAD_DIM ** -0.5
    even = layer_kind == "even"
    if even:
        tabs = (jnp.where(_causal_tables(bq, False) > 0, 0.0, NEG_INF), _causal_tables(bq, True))
    else:
        tabs = (_dilated_tables(bq),)
    n_tab = tabs[0].shape[0]

    def body(*refs):
        if even:
            q_ref, k_ref, v_ref, cf_ref, cft_ref, bias_ref, valid_ref, o_ref, ob_ref, lse_ref, qs, ks, vs = refs
        else:
            q_ref, k_ref, v_ref, c_ref, s_ref, bias_ref, o_ref, ob_ref, lse_ref, qs, ks, vs = refs
        h = pl.program_id(0)
        if even:
            qs[...] = q_ref[...].astype(BF16)
            ks[...] = k_ref[...].astype(BF16)
        else:
            qs[...] = _rope(q_ref[...], c_ref[...], s_ref[...]).astype(BF16)
            ks[...] = _rope(k_ref[...], c_ref[...], s_ref[...]).astype(BF16)
        vs[...] = v_ref[...].astype(BF16)

        def softmax_head(hh):
            def qblock(i, carry):
                qi = qs[_qblk(i, bq), :]
                if even:
                    lane = lax.broadcasted_iota(jnp.int32, (bq, LANES), 1)
                    cfq = jnp.sum(jnp.where(lane == hh, cf_ref[_qblk(i, bq), :], 0.0), axis=1, keepdims=True)

                def kblock(j, c):
                    m_run, l_run, acc = c
                    z = _dot_nt(qi, ks[_blk(j), :]) * scale + bias_ref[jnp.minimum(r * i - j + (r - 1), n_tab - 1)]
                    if even:
                        z = z + (cfq - cft_ref[hh, :, _blk(j)])
                    m_new = jnp.maximum(m_run, jnp.max(z, axis=1, keepdims=True))
                    p = jnp.exp(z - m_new)
                    alpha = jnp.exp(m_run - m_new)
                    l_new = alpha * l_run + jnp.sum(p, axis=1, keepdims=True)
                    acc = alpha * acc + jnp.dot(p.astype(BF16), vs[_blk(j), :], preferred_element_type=F32)
                    return m_new, l_new, acc

                init = (jnp.full((bq, 1), NEG_INF, F32), jnp.zeros((bq, 1), F32), jnp.zeros((bq, HEAD_DIM), F32))
                m_run, l_run, acc = lax.fori_loop(0, r * (i + 1), kblock, init)
                o = acc / l_run
                o_ref[_qblk(i, bq), :] = o
                ob_ref[_qblk(i, bq), :] = o.astype(BF16)
                lse_ref[_qblk(i, bq), :] = jnp.broadcast_to(m_run + jnp.log(l_run), (bq, HEAD_DIM))
                return carry

            lax.fori_loop(0, nq, qblock, 0)

        def sb_head():
            row = lax.broadcasted_iota(jnp.int32, (BLK, BLK), 0)
            col = lax.broadcasted_iota(jnp.int32, (BLK, BLK), 1)
            t_ex = jnp.where(row > col, 1.0, 0.0).astype(BF16)

            def qblock(i, carry):
                qi = qs[_qblk(i, bq), :]

                def kblock(jj, c):
                    run, acc, rest = c
                    j = r * (i + 1) - 1 - jj
                    z = _dot_nt(qi, ks[_blk(j), :]) * scale
                    a, m, _ = _sb_block(z, valid_ref[jnp.minimum(r * i - j + (r - 1), r)], t_ex, run)
                    vj = vs[_blk(j), :]
                    hi = a.astype(BF16)
                    lo = (a - hi.astype(F32)).astype(BF16)
                    acc = acc + jnp.dot(hi, vj, preferred_element_type=F32)
                    rest = rest + jnp.dot(lo, vj, preferred_element_type=F32)
                    return run + jnp.sum(m, axis=1, keepdims=True), acc, rest

                zero = jnp.zeros((bq, HEAD_DIM), F32)
                _, acc, rest = lax.fori_loop(0, r * (i + 1), kblock, (jnp.zeros((bq, 1), F32), zero, zero))
                o_ref[_qblk(i, bq), :] = acc + rest
                ob_ref[_qblk(i, bq), :] = acc.astype(BF16)
                lse_ref[_qblk(i, bq), :] = jnp.zeros((bq, HEAD_DIM), F32)
                return carry

            lax.fori_loop(0, nq, qblock, 0)

        if even:
            @pl.when(h < n_sb)
            def _():
                sb_head()

            @pl.when(h >= n_sb)
            def _():
                softmax_head(h - n_sb)
        else:
            softmax_head(h)

    head = lambda off: pl.BlockSpec((S, HEAD_DIM), lambda h, off=off: (0, off + h))
    full = pl.BlockSpec((S, LANES), lambda h: (0, 0))
    tab_specs = [pl.BlockSpec(t.shape, lambda h: (0, 0, 0)) for t in tabs]
    if even:
        extra_specs = [full, pl.BlockSpec(cft.shape, lambda h: (0, 0, 0))] + tab_specs
        extra = (cf, cft) + tabs
    else:
        extra_specs = [full, full] + tab_specs
        extra = (rope_c, rope_s) + tabs
    blk_bytes = 8 * _nbytes((S, HEAD_DIM), F32) + sum(_nbytes(t.shape, F32) for t in tabs)
    return pl.pallas_call(
        body, name=name, grid=(n_heads,),
        in_specs=[head(0), head(n_heads), head(2 * n_heads)] + extra_specs,
        out_specs=[head(0), head(0), head(0)],
        out_shape=[jax.ShapeDtypeStruct((S, D), F32), jax.ShapeDtypeStruct((S, D), BF16),
                   jax.ShapeDtypeStruct((S, D), F32)],
        scratch_shapes=[pltpu.VMEM((S, HEAD_DIM), BF16)] * 3,
        compiler_params=pltpu.CompilerParams(dimension_semantics=("arbitrary",),
                                             vmem_limit_bytes=_vmem_limit(blk_bytes, 3 * _nbytes((S, HEAD_DIM), BF16))),
    )(hq, hq, hq, *extra)


def _attn_bwd_wide(name, hq, do, o, lse, layer_kind, n_heads, n_sb, cf=None, cft=None, rope_c=None, rope_s=None):
    S = hq.shape[0]
    D = n_heads * HEAD_DIM
    bq = _query_block(S)
    r = bq // BLK
    nq = S // bq
    scale = HEAD_DIM ** -0.5
    even = layer_kind == "even"
    if even:
        tabs = (jnp.where(_causal_tables(bq, False) > 0, 0.0, NEG_INF), _causal_tables(bq, True))
    else:
        tabs = (_dilated_tables(bq),)
    n_tab = tabs[0].shape[0]

    def body(*refs):
        if even:
            (q_ref, k_ref, v_ref, do_ref, o_ref, lse_ref, cf_ref, cft_ref, bias_ref, valid_ref,
             dh_ref, dcft_ref, drow_ref, qs, ks, vs, dos, dq_acc, dk_acc, dv_acc) = refs
        else:
            (q_ref, k_ref, v_ref, do_ref, o_ref, lse_ref, c_ref, s_ref, bias_ref,
             dh_ref, qs, ks, vs, dos, dq_acc, dk_acc, dv_acc) = refs
        h = pl.program_id(0)
        if even:
            qs[...] = q_ref[...].astype(BF16)
            ks[...] = k_ref[...].astype(BF16)

            @pl.when(h == 0)
            def _():
                dcft_ref[...] = jnp.zeros_like(dcft_ref)
                drow_ref[...] = jnp.zeros_like(drow_ref)
        else:
            qs[...] = _rope(q_ref[...], c_ref[...], s_ref[...]).astype(BF16)
            ks[...] = _rope(k_ref[...], c_ref[...], s_ref[...]).astype(BF16)
        vs[...] = v_ref[...].astype(BF16)
        dos[...] = do_ref[...].astype(BF16)
        dk_acc[...] = jnp.zeros_like(dk_acc)
        dv_acc[...] = jnp.zeros_like(dv_acc)

        def softmax_head(hh):
            def qblock(i, carry):
                qi = qs[_qblk(i, bq), :]
                doi = dos[_qblk(i, bq), :]
                dvec = jnp.sum(do_ref[_qblk(i, bq), :] * o_ref[_qblk(i, bq), :], axis=1, keepdims=True)
                lse_i = jnp.max(lse_ref[_qblk(i, bq), :], axis=1, keepdims=True)
                if even:
                    lane = lax.broadcasted_iota(jnp.int32, (bq, LANES), 1)
                    cfq = jnp.sum(jnp.where(lane == hh, cf_ref[_qblk(i, bq), :], 0.0), axis=1, keepdims=True)

                def kblock(j, c):
                    dq, ds_rows = c
                    kj = ks[_blk(j), :]
                    z = _dot_nt(qi, kj) * scale + bias_ref[jnp.minimum(r * i - j + (r - 1), n_tab - 1)]
                    if even:
                        z = z + (cfq - cft_ref[hh, :, _blk(j)])
                    p = jnp.exp(z - lse_i)
                    dp = _dot_nt(doi, vs[_blk(j), :])
                    ds = p * (dp - dvec)
                    dsb = (ds * scale).astype(BF16)
                    dk_acc[_blk(j), :] += _dot_tn(dsb, qi)
                    dv_acc[_blk(j), :] += _dot_tn(p.astype(BF16), doi)
                    if even:
                        dcft_ref[hh, :, _blk(j)] += -jnp.sum(ds, axis=0, keepdims=True)
                    return (dq + jnp.dot(dsb, kj, preferred_element_type=F32),
                            ds_rows + jnp.sum(ds, axis=1, keepdims=True))

                dq, ds_rows = lax.fori_loop(0, r * (i + 1), kblock,
                                            (jnp.zeros((bq, HEAD_DIM), F32), jnp.zeros((bq, 1), F32)))
                dq_acc[_qblk(i, bq), :] = dq
                if even:
                    drow_ref[_qblk(i, bq), :] += jnp.where(lane == hh, ds_rows, 0.0)
                return carry

            lax.fori_loop(0, nq, qblock, 0)

        def sb_head():
            row = lax.broadcasted_iota(jnp.int32, (BLK, BLK), 0)
            col = lax.broadcasted_iota(jnp.int32, (BLK, BLK), 1)
            t_ex = jnp.where(row > col, 1.0, 0.0).astype(BF16)
            t_in = jnp.where(row >= col, 1.0, 0.0).astype(BF16)

            def qblock(i, carry):
                qi = qs[_qblk(i, bq), :]
                doi = dos[_qblk(i, bq), :]
                nkb = r * (i + 1)
                e_tot = jnp.sum(doi.astype(F32) * o_ref[_qblk(i, bq), :], axis=1, keepdims=True)
                zero = jnp.zeros((bq, 1), F32)

                def kblock(jj, c):
                    run, e_run, dq = c
                    j = nkb - 1 - jj
                    kj = ks[_blk(j), :]
                    z = _dot_nt(qi, kj) * scale
                    valid = valid_ref[jnp.minimum(r * i - j + (r - 1), r)]
                    a, m, lsig = _sb_block(z, valid, t_ex, run)
                    sig = jnp.exp(lsig)
                    e = _dot_nt(doi, vs[_blk(j), :]) * a
                    e_before = e_tot - (_split_dot(e, t_in) + e_run)
                    dz = (e * (1.0 - sig) - sig * e_before) * valid
                    dzb = (dz * scale).astype(BF16)
                    dk_acc[_blk(j), :] += _dot_tn(dzb, qi)
                    dv_acc[_blk(j), :] += _dot_tn(a.astype(BF16), doi)
                    return (run + jnp.sum(m, axis=1, keepdims=True), e_run + jnp.sum(e, axis=1, keepdims=True),
                            dq + jnp.dot(dzb, kj, preferred_element_type=F32))

                _, _, dq = lax.fori_loop(0, nkb, kblock, (zero, zero, jnp.zeros((bq, HEAD_DIM), F32)))
                dq_acc[_qblk(i, bq), :] = dq
                return carry

            lax.fori_loop(0, nq, qblock, 0)

        if even:
            @pl.when(h < n_sb)
            def _():
                sb_head()

            @pl.when(h >= n_sb)
            def _():
                softmax_head(h - n_sb)

            dh_ref[0] = dq_acc[...].astype(BF16)
            dh_ref[1] = dk_acc[...].astype(BF16)
        else:
            softmax_head(h)
            dh_ref[0] = _rope_t(dq_acc[...], c_ref[...], s_ref[...]).astype(BF16)
            dh_ref[1] = _rope_t(dk_acc[...], c_ref[...], s_ref[...]).astype(BF16)
        dh_ref[2] = dv_acc[...].astype(BF16)

    head = lambda off: pl.BlockSpec((S, HEAD_DIM), lambda h, off=off: (0, off + h))
    full = pl.BlockSpec((S, LANES), lambda h: (0, 0))
    tfull = pl.BlockSpec((n_heads - n_sb, 1, S), lambda h: (0, 0, 0))
    tab_specs = [pl.BlockSpec(t.shape, lambda h: (0, 0, 0)) for t in tabs]
    dh_spec = pl.BlockSpec((3, S, HEAD_DIM), lambda h: (0, 0, h))
    dh_shape = jax.ShapeDtypeStruct((3, S, D), BF16)
    if even:
        extra_specs, extra = [full, tfull] + tab_specs, (cf, cft) + tabs
        out_specs = [dh_spec, tfull, full]
        out_shape = [dh_shape, jax.ShapeDtypeStruct((n_heads - n_sb, 1, S), F32),
                     jax.ShapeDtypeStruct((S, LANES), F32)]
    else:
        extra_specs, extra = [full, full] + tab_specs, (rope_c, rope_s) + tabs
        out_specs = [dh_spec]
        out_shape = [dh_shape]
    blk_bytes = 10 * _nbytes((S, HEAD_DIM), F32) + sum(_nbytes(t.shape, F32) for t in tabs)
    scratch_bytes = 4 * _nbytes((S, HEAD_DIM), BF16) + 3 * _nbytes((S, HEAD_DIM), F32)
    return pl.pallas_call(
        body, name=name, grid=(n_heads,),
        in_specs=[head(0), head(n_heads), head(2 * n_heads), head(0), head(0), head(0)] + extra_specs,
        out_specs=out_specs, out_shape=out_shape,
        scratch_shapes=[pltpu.VMEM((S, HEAD_DIM), BF16)] * 4 + [pltpu.VMEM((S, HEAD_DIM), F32)] * 3,
        compiler_params=pltpu.CompilerParams(dimension_semantics=("arbitrary",),
                                             vmem_limit_bytes=_vmem_limit(blk_bytes, scratch_bytes)),
    )(hq, hq, hq, do, o, lse, *extra)


def _ffn_fwd(tag, x, g, wgu, wd, fs):
    n = _rms_fwd(tag + "_norm", x, g)
    gu, h = _ffn_up(tag + "_gu", n, wgu, fs)
    if callable(wd):
        wd = wd(h)
    y = _mm(tag + "_down", h, wd, "nn", F32, res=x, alpha=0.5, tk_target=2 * fs)
    return y, (x, g, n, gu, h), wd


def _ffn_bwd(tag, dx, dxb, wgu, wd, fs, saved, after=None, emit=None):
    x, g, n, gu, h = saved
    dgu = _ffn_dact(tag + "_dgu", dxb, wd, gu, fs, 0.5, after=after)
    dwd = _mm(tag + "_dwd", h, dxb, "tn", BF16, alpha=0.5)
    dwgu = _mm(tag + "_dwgu", n, dgu, "tn", BF16)
    token = emit(dwgu, dwd) if emit else None
    dn = _mm(tag + "_dn", dgu, wgu, "nt", F32, after=token, tk_target=2 * fs)
    dx_in, dxb_in, dg = _rms_bwd(tag + "_dnorm", dn, x, g, dx)
    return dx_in, dxb_in, dg, dwgu, dwd, token


def _mixer_fwd(tag, kind, x, g, wqkv, wout, n_heads, n_sb, wf=None, bf=None, rope=None):
    n = _rms_fwd(tag + "_norm", x, g)
    hq = _mm(tag + "_qkv", n, wqkv, "nn", F32)
    if kind == "even":
        hf = _mm(tag + "_gate", n, wf, "nn", F32)
        cf, cft = _gate_fwd(tag + "_cumgate", hf, bf)
        cft = cft[:n_heads - n_sb].reshape(n_heads - n_sb, 1, -1)
        o, ob, lse = _attn_fwd_wide(tag + "_attn", hq, kind, n_heads, n_sb, cf=cf, cft=cft)
    else:
        hf = cf = cft = None
        o, ob, lse = _attn_fwd_wide(tag + "_attn", hq, kind, n_heads, n_sb, rope_c=rope[0], rope_s=rope[1])
    y = _mm(tag + "_out", ob, wout, "nn", F32, res=x)
    return y, (x, g, n, hq, hf, cf, cft, o, ob, lse)


def _mixer_bwd(tag, kind, dx, dxb, wqkv, wout, n_heads, n_sb, saved, wf=None, bf=None, rope=None, after=None,
               emit=None):
    x, g, n, hq, hf, cf, cft, o, ob, lse = saved
    do = _mm(tag + "_do", dxb, wout, "nt", F32, after=after)
    dwout = _mm(tag + "_dwout", ob, dxb, "tn", BF16)
    if kind == "even":
        dh3, dcft, drow = _attn_bwd_wide(tag + "_dattn", hq, do, o, lse, kind, n_heads, n_sb, cf=cf, cft=cft)
    else:
        (dh3,) = _attn_bwd_wide(tag + "_dattn", hq, do, o, lse, kind, n_heads, n_sb, rope_c=rope[0], rope_s=rope[1])
    dwqkv = _mm(tag + "_dwqkv", n, dh3, "tn", BF16)
    dwf = db = dhf = None
    if kind == "even":
        n_fox = n_heads - n_sb
        dcft = jnp.pad(dcft.reshape(n_fox, -1), ((0, LANES - n_fox), (0, 0)))
        dhf, db = _gate_bwd(tag + "_dcumgate", dcft, drow, hf, bf)
        dwf = _mm(tag + "_dwf", n, dhf, "tn", BF16)
    token = emit(dwqkv, dwout, dwf) if emit else None
    dn = _mm(tag + "_dn", dh3, wqkv, "nt", F32, after=token)
    if kind == "even":
        dn = _mm(tag + "_dn_gate", dhf, wf, "nt", F32, res=dn)
    dx_in, dxb_in, dg = _rms_bwd(tag + "_dnorm", dn, x, g, dx)
    return dx_in, dxb_in, dg, dwqkv, dwout, dwf, db, token


def _local_step(x, target, w, fs, n_heads, n_sb, emit=None):
    S, D = x.shape
    rope = _rope_tables(S)
    kinds = ("even", "odd")
    saved = []
    h = x
    if callable(w):
        fetch, w = w, {"norm_g": w("norm_g", None), "final_g": w("final_g", None),
                       "wgu1": [None, None], "wd1": [None, None], "wgu2": [None, None], "wd2": [None, None]}
    else:
        fetch = None
    for l, kind in enumerate(kinds):
        ng = [w["norm_g"][l, i][None, :] for i in range(3)]
        if fetch:
            w["wgu1"][l], w["wd1"][l] = fetch(("ffn1", l), h)
        h, s1, wd = _ffn_fwd(f"l{l}_ffn1", h, ng[0], w["wgu1"][l], w["wd1"][l], fs)
        if fetch:
            w["wd1"][l] = wd
        if fetch:
            w.update(fetch(("mix", l), h))
        if kind == "even":
            h, s2 = _mixer_fwd(f"l{l}_mix", kind, h, ng[1], w["wqkv_e"], w["wout_e"], n_heads, n_sb,
                               wf=w["wf"], bf=w["bf"])
        else:
            h, s2 = _mixer_fwd(f"l{l}_mix", kind, h, ng[1], w["wqkv_o"], w["wout_o"], n_heads, n_sb, rope=rope)
        if fetch:
            w["wgu2"][l], w["wd2"][l] = fetch(("ffn2", l), h)
        h, s3, _ = _ffn_fwd(f"l{l}_ffn2", h, ng[2], w["wgu2"][l], w["wd2"][l], fs)
        saved.append((s1, s2, s3))

    dx, dxb, dfinal, loss = _loss_head("loss_head", h, w["final_g"], target)
    grads = {"dfinal": dfinal, "dnorm": [[None] * 3 for _ in kinds],
             "dwgu1": [None, None], "dwd1": [None, None], "dwgu2": [None, None], "dwd2": [None, None]}
    hand = lambda block: (lambda *mats: emit(block, mats)) if emit else None
    token = None
    for l in (1, 0):
        kind = kinds[l]
        s1, s2, s3 = saved[l]
        dx, dxb, dg, grads["dwgu2"][l], grads["dwd2"][l], token = _ffn_bwd(
            f"l{l}_ffn2", dx, dxb, w["wgu2"][l], w["wd2"][l], fs, s3, after=token, emit=hand(("ffn2", l)))
        grads["dnorm"][l][2] = dg
        if kind == "even":
            dx, dxb, dg, grads["dwqkv_e"], grads["dwout_e"], grads["dwf"], grads["db"], token = _mixer_bwd(
                f"l{l}_mix", kind, dx, dxb, w["wqkv_e"], w["wout_e"], n_heads, n_sb, s2, wf=w["wf"], bf=w["bf"],
                after=token, emit=hand(("mix", l)))
        else:
            dx, dxb, dg, grads["dwqkv_o"], grads["dwout_o"], _, _, token = _mixer_bwd(
                f"l{l}_mix", kind, dx, dxb, w["wqkv_o"], w["wout_o"], n_heads, n_sb, s2, rope=rope,
                after=token, emit=hand(("mix", l)))
        grads["dnorm"][l][1] = dg
        dx, dxb, dg, grads["dwgu1"][l], grads["dwd1"][l], token = _ffn_bwd(
            f"l{l}_ffn1", dx, dxb, w["wgu1"][l], w["wd1"][l], fs, s1, after=token, emit=hand(("ffn1", l)))
        grads["dnorm"][l][0] = dg
    return loss, dx, grads


def _cast_into(name, shard, layer, chip, full_shape, place, full=None, after=None):
    R, C = shard.shape[-2:]
    tr = _row_tile(R, 512, step=16)
    if layer is None:
        in_spec = pl.BlockSpec((tr, C), lambda i, k: (i, 0))
    else:
        in_spec = pl.BlockSpec((None, tr, C), lambda i, k: (layer, i, 0))
    lead = (None,) * (len(full_shape) - 2)
    out_spec = pl.BlockSpec(lead + (tr, C), lambda i, k: place(i, k[0]))

    def body(*refs):
        k_ref, w_ref = refs[:2]
        o_ref = refs[-1]
        o_ref[...] = w_ref[...].astype(BF16)

    in_specs = [in_spec] + ([_ANY] if full is not None else []) + ([_ANY] if after is not None else [])
    args = (chip, shard) + ((full,) if full is not None else ()) + ((after,) if after is not None else ())
    grid_spec = pltpu.PrefetchScalarGridSpec(num_scalar_prefetch=1, grid=(R // tr,), in_specs=in_specs, out_specs=out_spec)
    return pl.pallas_call(
        body, name=name, grid_spec=grid_spec, out_shape=jax.ShapeDtypeStruct(full_shape, BF16),
        input_output_aliases={2: 0} if full is not None else {},
        compiler_params=pltpu.CompilerParams(dimension_semantics=("arbitrary",)),
    )(*args)


def _region_shape(grad, kind):
    if kind == "lead":
        return grad.shape[1] // N_CORES, grad.shape[2]
    rows, cols = grad.shape
    if kind == "cols":
        return rows // N_CORES, cols // N_CHIPS
    return rows // (N_CHIPS * N_CORES), cols


def _region_add(name, grad, kind, landed, core):
    rh, cw = _region_shape(grad, kind)
    tr = _row_tile(rh, 256, step=16)
    nrb = rh // tr
    if kind == "cols":
        g_spec = pl.BlockSpec((tr, cw), lambda k, r, c: (c[0] * nrb + r, k))
    elif kind == "rows":
        g_spec = pl.BlockSpec((tr, cw), lambda k, r, c: ((N_CORES * k + c[0]) * nrb + r, 0))
    else:
        g_spec = pl.BlockSpec((None, tr, cw), lambda k, r, c: (k, c[0] * nrb + r, 0))
    l_spec = pl.BlockSpec((None, tr, cw), lambda k, r, c: (k, r, 0))

    def body(c_ref, g_ref, l_ref, o_ref):
        o_ref[...] = (g_ref[...].astype(F32) + l_ref[...].astype(F32)).astype(BF16)

    grid_spec = pltpu.PrefetchScalarGridSpec(
        num_scalar_prefetch=1, grid=(N_CHIPS, nrb), in_specs=[g_spec, l_spec], out_specs=l_spec)
    return pl.pallas_call(
        body, name=name, grid_spec=grid_spec, out_shape=jax.ShapeDtypeStruct(landed.shape, BF16),
        compiler_params=pltpu.CompilerParams(dimension_semantics=("parallel", "parallel"),
                                             vmem_limit_bytes=_vmem_limit(3 * _nbytes((tr, cw), F32))),
    )(core, grad, landed)


def _chip_sum(name, pair, landed, pos):
    _, rh, cw = pair.shape
    tr = _row_tile(rh, max(16, 2**20 // (cw * 4)), step=16)
    nrb = rh // tr

    def body(p_ref, own_ref, l_ref, o_ref):
        acc = own_ref[...].astype(F32)
        for s in range(N_CHIPS - 1):
            acc = acc + l_ref[s].astype(F32)
        o_ref[...] = acc

    grid_spec = pltpu.PrefetchScalarGridSpec(
        num_scalar_prefetch=1, grid=(nrb,),
        in_specs=[pl.BlockSpec((None, tr, cw), lambda r, p: (p[0], r, 0)),
                  pl.BlockSpec((N_CHIPS - 1, tr, cw), lambda r, p: (0, r, 0))],
        out_specs=pl.BlockSpec((tr, cw), lambda r, p: (p[1] * nrb + r, 0)))
    return pl.pallas_call(
        body, name=name, grid_spec=grid_spec, out_shape=jax.ShapeDtypeStruct((N_CORES * rh, cw), F32),
        compiler_params=pltpu.CompilerParams(dimension_semantics=("arbitrary",)),
    )(pos, pair, landed)


def _sum_leading(name, parts):
    n, R, C = parts.shape
    tr = _row_tile(R, max(8, (2**20 // (C * 4)) // 8 * 8))

    def body(p_ref, o_ref):
        acc = p_ref[0]
        for s in range(1, n):
            acc = acc + p_ref[s]
        o_ref[...] = acc

    return pl.pallas_call(
        body, name=name, grid=(R // tr,),
        in_specs=[pl.BlockSpec((n, tr, C), lambda i: (0, i, 0))],
        out_specs=pl.BlockSpec((tr, C), lambda i: (i, 0)),
        out_shape=jax.ShapeDtypeStruct((R, C), F32),
        compiler_params=pltpu.CompilerParams(dimension_semantics=("parallel",)),
    )(parts)


def _adamw(name, w, g, m, v):
    shape = w.shape
    to2d = lambda t: t.reshape(-1, shape[-1]) if t.ndim > 1 else t.reshape(1, -1)
    w2, g2, m2, v2 = (to2d(t) for t in (w, g, m, v))
    R, C = w2.shape
    tr = _row_tile(R, 256)

    def body(w_ref, g_ref, m_ref, v_ref, d_ref, nm_ref, nv_ref):
        gv = g_ref[...]
        nm = ADAM_B1 * m_ref[...] + (1.0 - ADAM_B1) * gv
        nv = ADAM_B2 * v_ref[...] + (1.0 - ADAM_B2) * (gv * gv)
        m_hat = nm / (1.0 - ADAM_B1 ** ADAM_STEP)
        v_hat = nv / (1.0 - ADAM_B2 ** ADAM_STEP)
        d_ref[...] = -ADAM_LR * (m_hat / (jnp.sqrt(v_hat) + ADAM_EPS) + ADAM_WD * w_ref[...])
        nm_ref[...] = nm
        nv_ref[...] = nv

    spec = pl.BlockSpec((tr, C), lambda i: (i, 0))
    sds = jax.ShapeDtypeStruct((R, C), F32)
    d, nm, nv = pl.pallas_call(
        body, name=name, grid=(R // tr,), in_specs=[spec] * 4, out_specs=[spec] * 3, out_shape=[sds] * 3,
        compiler_params=pltpu.CompilerParams(dimension_semantics=("parallel",),
                                             vmem_limit_bytes=_vmem_limit(7 * _nbytes((tr, C), F32))),
    )(w2, g2, m2, v2)
    return d.reshape(shape), nm.reshape(shape), nv.reshape(shape)


_ANY = pl.BlockSpec(memory_space=pl.ANY)


def _mesh_pos():
    return lax.axis_index("x"), lax.axis_index("y"), lax.axis_index("c")


def _other_chips(x, y):
    return [(1 - x, y), (x, 1 - y), (1 - x, 1 - y)]


def _gather_over_chips(name, fulls, views):
    n = len(views)
    nf = len(fulls)

    def body(*refs):
        full = refs[nf:2 * nf]
        ici_send, ici_recv, d2d_send, d2d_recv = refs[2 * nf:]
        x, y, c = _mesh_pos()
        chips = _other_chips(x, y)
        mine = 2 * x + y
        sibling = (x, y, 1 - c)

        def ici(a, p, k):
            i, view, _ = views[a]
            part = view(full[i], k, c)
            return pltpu.make_async_remote_copy(
                src_ref=part, dst_ref=part, send_sem=ici_send.at[a, p], recv_sem=ici_recv.at[a, p],
                device_id=(*chips[p], c), device_id_type=MESH)

        def d2d(a, p, h):
            i, view, _ = views[a]
            px, py = chips[p]
            part = view(full[i], 2 * px + py, h)
            return pltpu.make_async_remote_copy(
                src_ref=part, dst_ref=part, send_sem=d2d_send.at[a, p], recv_sem=d2d_recv.at[a, p],
                device_id=sibling, device_id_type=MESH)

        sends = [ici(a, p, mine) for a in range(n) for p in range(3)]
        for cp in sends:
            cp.start()
        passed = []
        for a in range(n):
            for p, (px, py) in enumerate(chips):
                ici(a, p, 2 * px + py).wait_recv()
                if views[a][2]:
                    fwd = d2d(a, p, c)
                    fwd.start()
                    passed.append(fwd)
        for a in range(n):
            if views[a][2]:
                for p in range(3):
                    d2d(a, p, 1 - c).wait_recv()
        for cp in sends + passed:
            cp.wait_send()

    return pl.pallas_call(
        body, name=name, in_specs=[_ANY] * nf, out_specs=[_ANY] * nf,
        out_shape=[jax.ShapeDtypeStruct(f.shape, f.dtype) for f in fulls],
        input_output_aliases={i: i for i in range(nf)},
        scratch_shapes=[pltpu.SemaphoreType.DMA((n, 3))] * 4,
        compiler_params=pltpu.CompilerParams(has_side_effects=True),
    )(*fulls)


_HBM = pl.BlockSpec(memory_space=pltpu.HBM)
_SEM = pl.BlockSpec(memory_space=pltpu.SEMAPHORE)


def _in_hbm(arrays):
    return [pltpu.with_memory_space_constraint(a, pltpu.HBM) for a in arrays]


def _gather_start(name, fulls, views, after):
    nf = len(fulls)
    ng = 1 + max(g for _, _, g in views)

    def body(*refs):
        full = refs[nf + 1:2 * nf + 1]
        send_sems, recv_sems = refs[2 * nf + 1:2 * nf + 1 + ng], refs[2 * nf + 1 + ng:]
        x, y, c = _mesh_pos()
        chips = _other_chips(x, y)
        for i, view, g in views:
            part = view(full[i], 2 * x + y, c)
            for px, py in chips:
                pltpu.make_async_remote_copy(
                    src_ref=part, dst_ref=part, send_sem=send_sems[g], recv_sem=recv_sems[g],
                    device_id=(px, py, c), device_id_type=MESH).start()

    outs = pl.pallas_call(
        body, name=name, in_specs=[_HBM] * nf + [_ANY], out_specs=[_HBM] * nf + [_SEM] * (2 * ng),
        out_shape=[pltpu.HBM(f.shape, f.dtype) for f in fulls] + [pltpu.SemaphoreType.DMA(())] * (2 * ng),
        input_output_aliases={i: i for i in range(nf)},
        compiler_params=pltpu.CompilerParams(has_side_effects=pltpu.SideEffectType.DATAFLOW_SIDE_EFFECTING),
    )(*_in_hbm(fulls), after)
    return list(outs[:nf]), list(outs[nf:nf + ng]), list(outs[nf + ng:])


def _gather_wait(name, fulls, views, send_sem, recv_sem, after):
    nf = len(fulls)

    def body(*refs):
        send_ref, recv_ref = refs[nf], refs[nf + 1]
        full = refs[nf + 3:]
        x, y, c = _mesh_pos()
        copies = [pltpu.make_async_remote_copy(
            src_ref=view(full[i], 2 * x + y, c), dst_ref=view(full[i], 2 * px + py, c),
            send_sem=send_ref, recv_sem=recv_ref, device_id=(px, py, c), device_id_type=MESH)
            for i, view in views for px, py in _other_chips(x, y)]
        for cp in copies:
            cp.wait_send()
        for cp in copies:
            cp.wait_recv()

    outs = pl.pallas_call(
        body, name=name, in_specs=[_HBM] * nf + [_SEM, _SEM, _ANY], out_specs=[_HBM] * nf,
        out_shape=[pltpu.HBM(f.shape, f.dtype) for f in fulls],
        input_output_aliases={i: i for i in range(nf)},
        compiler_params=pltpu.CompilerParams(has_side_effects=pltpu.SideEffectType.DATAFLOW_SIDE_EFFECTING),
    )(*fulls, send_sem, recv_sem, after)
    return list(outs)


def _forward_to_sibling(name, fulls, views):
    n, nf = len(views), len(fulls)

    def body(*refs):
        full = refs[nf:2 * nf]
        send_sems, recv_sems = refs[2 * nf:]
        x, y, c = _mesh_pos()
        chips = _other_chips(x, y)

        def copy(a, p, h):
            i, view = views[a]
            px, py = chips[p]
            part = view(full[i], 2 * px + py, h)
            return pltpu.make_async_remote_copy(
                src_ref=part, dst_ref=part, send_sem=send_sems.at[a, p], recv_sem=recv_sems.at[a, p],
                device_id=(x, y, 1 - c), device_id_type=MESH)

        sends = [copy(a, p, c) for a in range(n) for p in range(3)]
        for cp in sends:
            cp.start()
        for a in range(n):
            for p in range(3):
                copy(a, p, 1 - c).wait_recv()
        for cp in sends:
            cp.wait_send()

    return pl.pallas_call(
        body, name=name, in_specs=[_ANY] * nf, out_specs=[_ANY] * nf,
        out_shape=[jax.ShapeDtypeStruct(f.shape, f.dtype) for f in fulls],
        input_output_aliases={i: i for i in range(nf)},
        scratch_shapes=[pltpu.SemaphoreType.DMA((n, 3))] * 2,
        compiler_params=pltpu.CompilerParams(has_side_effects=True),
    )(*fulls)


def _region_view(ref, kind, k, c):
    if kind == "lead":
        rh = ref.shape[1] // N_CORES
        return ref.at[k, pl.ds(pl.multiple_of(c * rh, 8), rh), :]
    rows, cols = ref.shape
    if kind == "cols":
        rh, cw = rows // N_CORES, cols // N_CHIPS
        return ref.at[pl.ds(pl.multiple_of(c * rh, 8), rh), pl.ds(k * cw, cw)]
    rh = rows // (N_CHIPS * N_CORES)
    return ref.at[pl.ds(pl.multiple_of((N_CORES * k + c) * rh, 8), rh), :]


def _send_to_sibling(name, grads, kinds):
    n = len(grads)
    shapes = [jax.ShapeDtypeStruct((N_CHIPS,) + _region_shape(g, kd), g.dtype) for g, kd in zip(grads, kinds)]

    def body(*refs):
        g_ref, land = refs[:n], refs[n:2 * n]
        send_sems, recv_sems = refs[2 * n:]
        x, y, c = _mesh_pos()
        copies = []
        for a in range(n):
            for k in range(N_CHIPS):
                cp = pltpu.make_async_remote_copy(
                    src_ref=_region_view(g_ref[a], kinds[a], k, 1 - c), dst_ref=land[a].at[k],
                    send_sem=send_sems.at[a, k], recv_sem=recv_sems.at[a, k],
                    device_id=(x, y, 1 - c), device_id_type=MESH)
                cp.start()
                copies.append(cp)
        for cp in copies:
            cp.wait_recv()
        for cp in copies:
            cp.wait_send()

    return pl.pallas_call(
        body, name=name, in_specs=[_ANY] * n, out_specs=[_ANY] * n, out_shape=shapes,
        scratch_shapes=[pltpu.SemaphoreType.DMA((n, N_CHIPS)), pltpu.SemaphoreType.DMA((n, N_CHIPS))],
        compiler_params=pltpu.CompilerParams(has_side_effects=True),
    )(*grads)


def _scatter_start(name, pair_sums):
    n = len(pair_sums)
    lands = [lax.empty((N_CHIPS - 1,) + p.shape[1:], p.dtype) for p in pair_sums]

    def body(*refs):
        p_ref, land = refs[2 * n:3 * n], refs[3 * n:4 * n]
        send_sem, recv_sem, token = refs[4 * n:]
        x, y, c = _mesh_pos()
        for a in range(n):
            for p, (px, py) in enumerate(_other_chips(x, y)):
                pltpu.make_async_remote_copy(
                    src_ref=p_ref[a].at[2 * px + py], dst_ref=land[a].at[p], send_sem=send_sem, recv_sem=recv_sem,
                    device_id=(px, py, c), device_id_type=MESH).start()
        token[...] = jnp.zeros_like(token)

    outs = pl.pallas_call(
        body, name=name, in_specs=[_HBM] * (2 * n),
        out_specs=[_HBM] * (2 * n) + [_SEM, _SEM, pl.BlockSpec(memory_space=pltpu.VMEM)],
        out_shape=[pltpu.HBM(t.shape, t.dtype) for t in list(pair_sums) + lands]
        + [pltpu.SemaphoreType.DMA(()), pltpu.SemaphoreType.DMA(()), jax.ShapeDtypeStruct((8, LANES), F32)],
        input_output_aliases={i: i for i in range(2 * n)},
        compiler_params=pltpu.CompilerParams(has_side_effects=pltpu.SideEffectType.DATAFLOW_SIDE_EFFECTING),
    )(*_in_hbm(list(pair_sums) + lands))
    return list(outs[:n]), list(outs[n:2 * n]), outs[2 * n], outs[2 * n + 1], outs[2 * n + 2]


def _scatter_wait(name, pair_sums, lands, send_sem, recv_sem, after):
    n = len(pair_sums)

    def body(*refs):
        send_ref, recv_ref = refs[2 * n], refs[2 * n + 1]
        p_ref, land = refs[2 * n + 3:3 * n + 3], refs[3 * n + 3:]
        x, y, c = _mesh_pos()
        copies = [pltpu.make_async_remote_copy(
            src_ref=p_ref[a].at[2 * px + py], dst_ref=land[a].at[p], send_sem=send_ref, recv_sem=recv_ref,
            device_id=(px, py, c), device_id_type=MESH)
            for a in range(n) for p, (px, py) in enumerate(_other_chips(x, y))]
        for cp in copies:
            cp.wait_send()
        for cp in copies:
            cp.wait_recv()

    outs = pl.pallas_call(
        body, name=name, in_specs=[_HBM] * (2 * n) + [_SEM, _SEM, _ANY], out_specs=[_HBM] * (2 * n),
        out_shape=[pltpu.HBM(t.shape, t.dtype) for t in list(pair_sums) + list(lands)],
        input_output_aliases={i: i for i in range(2 * n)},
        compiler_params=pltpu.CompilerParams(has_side_effects=pltpu.SideEffectType.DATAFLOW_SIDE_EFFECTING),
    )(*pair_sums, *lands, send_sem, recv_sem, after)
    return list(outs[:n]), list(outs[n:])


def _swap_halves(name, shards):
    n = len(shards)

    def body(*refs):
        out = refs[n:2 * n]
        send_sems, recv_sems = refs[2 * n:]
        x, y, c = _mesh_pos()
        sends = []
        for a in range(n):
            rh = out[a].shape[0] // N_CORES
            mine = out[a].at[pl.ds(pl.multiple_of(c * rh, 8), rh), :]
            cp = pltpu.make_async_remote_copy(
                src_ref=mine, dst_ref=mine, send_sem=send_sems.at[a], recv_sem=recv_sems.at[a],
                device_id=(x, y, 1 - c), device_id_type=MESH)
            cp.start()
            sends.append(cp)
        for a in range(n):
            rh = out[a].shape[0] // N_CORES
            theirs = out[a].at[pl.ds(pl.multiple_of((1 - c) * rh, 8), rh), :]
            pltpu.make_async_remote_copy(
                src_ref=theirs, dst_ref=theirs, send_sem=send_sems.at[a], recv_sem=recv_sems.at[a],
                device_id=(x, y, 1 - c), device_id_type=MESH).wait_recv()
        for cp in sends:
            cp.wait_send()

    return pl.pallas_call(
        body, name=name, in_specs=[_ANY] * n, out_specs=[_ANY] * n,
        out_shape=[jax.ShapeDtypeStruct(s.shape, s.dtype) for s in shards],
        input_output_aliases={i: i for i in range(n)},
        scratch_shapes=[pltpu.SemaphoreType.DMA((n,)), pltpu.SemaphoreType.DMA((n,))],
        compiler_params=pltpu.CompilerParams(has_side_effects=True),
    )(*shards)


def _gather_all_devices(name, block):
    R, C = block.shape
    ndev = N_CHIPS * N_CORES

    def body(b_ref, out_ref, send_sems, recv_sems, local_sem):
        x, y, c = _mesh_pos()
        mine = 4 * x + 2 * y + c
        own = pltpu.make_async_copy(b_ref, out_ref.at[mine], local_sem)
        own.start()
        sends = []
        for mask in range(1, ndev):
            fx, fy, fc = (mask >> 2) & 1, (mask >> 1) & 1, mask & 1
            px, py, pc = x ^ fx, y ^ fy, c ^ fc
            cp = pltpu.make_async_remote_copy(
                src_ref=b_ref, dst_ref=out_ref.at[mine], send_sem=send_sems.at[mask - 1],
                recv_sem=recv_sems.at[mask - 1], device_id=(px, py, pc), device_id_type=MESH)
            cp.start()
            sends.append(cp)
        for mask in range(1, ndev):
            fx, fy, fc = (mask >> 2) & 1, (mask >> 1) & 1, mask & 1
            px, py, pc = x ^ fx, y ^ fy, c ^ fc
            pltpu.make_async_remote_copy(
                src_ref=b_ref, dst_ref=out_ref.at[4 * px + 2 * py + pc], send_sem=send_sems.at[mask - 1],
                recv_sem=recv_sems.at[mask - 1], device_id=(px, py, pc), device_id_type=MESH).wait_recv()
        for cp in sends:
            cp.wait_send()
        own.wait()

    return pl.pallas_call(
        body, name=name, in_specs=[_ANY], out_specs=_ANY,
        out_shape=jax.ShapeDtypeStruct((ndev, R, C), F32),
        scratch_shapes=[pltpu.SemaphoreType.DMA((ndev - 1,)), pltpu.SemaphoreType.DMA((ndev - 1,)),
                        pltpu.SemaphoreType.DMA(())],
        compiler_params=pltpu.CompilerParams(has_side_effects=True),
    )(block)


def kernel(x, norm_g, ffn1_w_gate, ffn1_w_up, ffn1_w_down, ffn2_w_gate, ffn2_w_up, ffn2_w_down, even_w_in, even_b_forget, even_w_out, odd_w_qkv, odd_w_out, final_norm_g, loss_target, m_norm_g, m_ffn1_w_gate, m_ffn1_w_up, m_ffn1_w_down, m_ffn2_w_gate, m_ffn2_w_up, m_ffn2_w_down, m_even_w_in, m_even_b_forget, m_even_w_out, m_odd_w_qkv, m_odd_w_out, m_final_norm_g, v_norm_g, v_ffn1_w_gate, v_ffn1_w_up, v_ffn1_w_down, v_ffn2_w_gate, v_ffn2_w_up, v_ffn2_w_down, v_even_w_in, v_even_b_forget, v_even_w_out, v_odd_w_qkv, v_odd_w_out, v_final_norm_g):
    _, S, D = x.shape
    L = norm_g.shape[0]
    assert L == 2 and even_w_in.shape[0] == 1 and odd_w_qkv.shape[0] == 1
    fs = ffn1_w_gate.shape[2]
    F = N_CHIPS * fs
    wc = even_w_in.shape[2]
    n_heads = D // HEAD_DIM
    n_fox = N_CHIPS * wc - 3 * D
    n_sb = n_heads - n_fox
    qs = odd_w_qkv.shape[2]
    os_ = even_w_out.shape[1]
    ns = norm_g.shape[2]
    xi, yi, ci = _mesh_pos()
    chip = 2 * xi + yi

    pos = jnp.stack([chip, ci]).astype(jnp.int32)
    kchip = pos[:1]
    lane = lambda start, size: pl.ds(pl.multiple_of(start, LANES), size)
    sub = lambda start, size: pl.ds(pl.multiple_of(start, 16), size)
    gate_view = lambda r, k, h: r.at[sub(h * (D // 2), D // 2), lane(k * 2 * fs, fs)]
    up_view = lambda r, k, h: r.at[sub(h * (D // 2), D // 2), lane(k * 2 * fs + fs, fs)]
    down_view = lambda r, k, h: r.at[sub(k * fs + h * (fs // 2), fs // 2), :]
    out_view = lambda r, k, h: r.at[sub(k * os_ + h * (os_ // 2), os_ // 2), :]
    tr_d = _row_tile(fs, 512, step=16)
    tr_o = _row_tile(os_, 512, step=16)
    ffn_w = {"ffn1": (ffn1_w_gate, ffn1_w_up, ffn1_w_down), "ffn2": (ffn2_w_gate, ffn2_w_up, ffn2_w_down)}
    win_view = lambda r, k, h: r.at[k, sub(h * (D // 2), D // 2), :]
    qkv_view = lambda r, k, h: r.at[sub(h * (D // 2), D // 2), lane(k * qs, qs)]
    norm_own = lax.dynamic_update_slice(jnp.zeros((L, 3, N_CHIPS * ns), F32), norm_g, (0, 0, chip * ns))
    (norm_full,) = _gather_over_chips("gather_norm", [norm_own], [(0, lambda r, k, h: r.at[:, :, lane(k * ns, ns)], False)])
    first = None
    fulls, views, groups = [], [], {}
    for l in range(L):
        for blk in ("ffn1", "mix", "ffn2"):
            tok = first[0][0] if first else None
            o, v0 = len(fulls), len(views)
            if blk == "mix" and l == 0:
                fulls += [_cast_into("cast_win", even_w_in, 0, kchip, (N_CHIPS, D, wc), lambda i, k: (k, i, 0), after=tok),
                          _cast_into("cast_wout_e", even_w_out, 0, kchip, (D, D), lambda i, k: (k * (os_ // tr_o) + i, 0),
                                     after=tok)]
                views += [(o, win_view), (o + 1, out_view)]
            elif blk == "mix":
                fulls += [_cast_into("cast_wqkv_o", odd_w_qkv, 0, kchip, (D, N_CHIPS * qs), lambda i, k: (i, k), after=tok),
                          _cast_into("cast_wout_o", odd_w_out, 0, kchip, (D, D), lambda i, k: (k * (os_ // tr_o) + i, 0),
                                     after=tok)]
                views += [(o, qkv_view), (o + 1, out_view)]
            else:
                wg, wu, wd = ffn_w[blk]
                t = f"cast_{blk}_l{l}"
                gu = _cast_into(t + "_gate", wg, l, kchip, (D, 2 * F), lambda i, k: (i, 2 * k), after=tok)
                gu = _cast_into(t + "_up", wu, l, kchip, (D, 2 * F), lambda i, k: (i, 2 * k + 1), full=gu)
                if first is None:
                    first = _gather_start("gather_start_first", [gu], [(0, gate_view, 0), (0, up_view, 0)], norm_full)
                    tok = first[0][0]
                    dn = _cast_into(t + "_down", wd, l, kchip, (F, D), lambda i, k: (k * (fs // tr_d) + i, 0), after=tok)
                    fulls += [dn]
                    views += [(o, down_view)]
                else:
                    dn = _cast_into(t + "_down", wd, l, kchip, (F, D), lambda i, k: (k * (fs // tr_d) + i, 0))
                    fulls += [gu, dn]
                    views += [(o, gate_view), (o, up_view), (o + 1, down_view)]
            gid = len(groups)
            views[v0:] = [(i, view, gid) for i, view in views[v0:]]
            groups[(blk, l)] = (gid, list(range(o, len(fulls))), list(range(v0, len(views))))
    started, send_sems, recv_sems = _gather_start("gather_start", fulls, views, first[0][0])

    def arrive(tag, arrays, local, ssem, rsem, after):
        got = _gather_wait("gather_wait_" + tag, arrays, local, ssem, rsem, after)
        return _forward_to_sibling("gather_pass_" + tag, got, local)

    def fetch(block, after):
        if block == "norm_g":
            return norm_full
        if block == "final_g":
            return final_norm_g[None, :]
        gid, arrays, rows = groups[block]
        tag = f"{block[0]}_l{block[1]}"
        local = [(views[a][0] - arrays[0], views[a][1]) for a in rows]
        rest = lambda a: arrive(tag, [started[i] for i in arrays], local, send_sems[gid], recv_sems[gid], a)
        if block == ("ffn1", 0):
            (wgu,) = arrive(tag + "_gu", first[0], [(0, gate_view), (0, up_view)], first[1][0], first[2][0], after)
            return wgu, lambda a: rest(a)[0]
        got = rest(after)
        if block[0] != "mix":
            return got
        if block[1] == 1:
            return {"wqkv_o": got[0], "wout_o": got[1]}
        win = jnp.concatenate([got[0][k] for k in range(N_CHIPS)], axis=1)
        return {"wqkv_e": win[:, :3 * D], "wf": jnp.pad(win[:, 3 * D:], ((0, 0), (0, LANES - n_fox))),
                "bf": jnp.pad(even_b_forget, ((0, 0), (0, LANES - n_fox))), "wout_e": got[1]}

    pending, shards = [], {}

    def finish(after):
        block, pair, lands, ssem, rsem = pending.pop(0)
        tag = f"{block[0]}_l{block[1]}"
        pair, lands = _scatter_wait("rs_chip_wait_" + tag, pair, lands, ssem, rsem, after)
        shards[block] = [_chip_sum(f"rs_chip_add_{tag}_{a}", p, ld, pos) for a, (p, ld) in enumerate(zip(pair, lands))]

    def emit(block, mats):
        blk, l = block
        tag = f"{blk}_l{l}"
        kinds = ["cols", "rows"]
        if blk == "mix" and l == 0:
            dwqkv, dwout, dwf = mats
            dwin = jnp.concatenate([dwqkv, dwf[:, :n_fox]], axis=1)
            mats = [jnp.stack([dwin[:, k * wc:(k + 1) * wc] for k in range(N_CHIPS)]), dwout]
            kinds = ["lead", "rows"]
        elif blk == "mix":
            mats = list(mats[:2])
        else:
            mats = list(mats)
        landed = _send_to_sibling("rs_pair_send_" + tag, mats, kinds)
        pair = [_region_add(f"rs_pair_add_{tag}_{a}", m, kd, ld, pos[1:])
                for a, (m, kd, ld) in enumerate(zip(mats, kinds, landed))]
        pair, lands, ssem, rsem, token = _scatter_start("rs_chip_start_" + tag, pair)
        if pending:
            finish(token)
        pending.append((block, pair, lands, ssem, rsem))
        return token

    loss_vec, grad_x, g = _local_step(x[0], loss_target[0], fetch, fs, n_heads, n_sb, emit=emit)
    finish(grad_x)
    order = [(b, l) for b in ("ffn1", "ffn2", "mix") for l in range(L)]
    red = _swap_halves("rs_swap_halves", [s for b in order for s in shards[b]])
    red = {b: red[2 * i:2 * i + 2] for i, b in enumerate(order)}
    gu1, gd1 = [red[("ffn1", l)][0] for l in range(L)], [red[("ffn1", l)][1] for l in range(L)]
    gu2, gd2 = [red[("ffn2", l)][0] for l in range(L)], [red[("ffn2", l)][1] for l in range(L)]
    (g_win, g_wout_e), (g_qkv_o, g_wout_o) = red[("mix", 0)], red[("mix", 1)]

    small_rows = [g["dnorm"][l][i] for l in range(L) for i in range(3)] + [
        g["dfinal"], jnp.pad(g["db"], ((0, 0), (0, D - LANES))), jnp.pad(loss_vec, ((0, 0), (0, D - LANES)))]
    small = jnp.concatenate(small_rows + [jnp.zeros((16 - len(small_rows), D), F32)], axis=0)
    small_sum = _sum_leading("small_sum", _gather_all_devices("small_gather", small))
    loss = small_sum[3 * L + 2, 0]
    g_norm = lax.dynamic_slice_in_dim(small_sum[:3 * L].reshape(L, 3, D), chip * ns, ns, axis=2)
    g_final = small_sum[3 * L]
    g_bf = small_sum[3 * L + 1, :n_fox][None, :]

    grads = [
        g_norm,
        jnp.stack([t[:, :fs] for t in gu1]), jnp.stack([t[:, fs:] for t in gu1]), jnp.stack(gd1),
        jnp.stack([t[:, :fs] for t in gu2]), jnp.stack([t[:, fs:] for t in gu2]), jnp.stack(gd2),
        g_win[None], g_bf, g_wout_e[None], g_qkv_o[None], g_wout_o[None], g_final]
    weights = [norm_g, ffn1_w_gate, ffn1_w_up, ffn1_w_down, ffn2_w_gate, ffn2_w_up, ffn2_w_down,
               even_w_in, even_b_forget, even_w_out, odd_w_qkv, odd_w_out, final_norm_g]
    ms = [m_norm_g, m_ffn1_w_gate, m_ffn1_w_up, m_ffn1_w_down, m_ffn2_w_gate, m_ffn2_w_up, m_ffn2_w_down,
          m_even_w_in, m_even_b_forget, m_even_w_out, m_odd_w_qkv, m_odd_w_out, m_final_norm_g]
    vs = [v_norm_g, v_ffn1_w_gate, v_ffn1_w_up, v_ffn1_w_down, v_ffn2_w_gate, v_ffn2_w_up, v_ffn2_w_down,
          v_even_w_in, v_even_b_forget, v_even_w_out, v_odd_w_qkv, v_odd_w_out, v_final_norm_g]
    deltas, new_ms, new_vs = [], [], []
    for i, (wt, gt, mt, vt) in enumerate(zip(weights, grads, ms, vs)):
        d, nm, nv = _adamw(f"adamw_{i}", wt, gt, mt, vt)
        deltas.append(d)
        new_ms.append(nm)
        new_vs.append(nv)
    return (loss, grad_x[None], *grads, *deltas, *new_ms, *new_vs)
```
